```python
import jax, jax.numpy as jnp
from jax import lax
import numpy as np

D_MODEL = 1024
BATCH = 2
SEQ = 8192
DEPTH = 1
DEC_BATCH = 128
DEC_SEQ = 4
PAST_LEN = 2048
PAGE_SIZE = 128

POOL_WINDOWS = (2, 4, 8, 16)
POOL_GROUPS = 4
POOL_GW = D_MODEL // 16
POOL_W = POOL_GROUPS * POOL_GW
POOL_BUF = max(POOL_WINDOWS) - 1
N_HEADS = 8
HEAD_DIM = 64
N_KV = 2
GROUP = N_HEADS // N_KV
CMP_STRIDE = 16
CMP_BLOCK = 2 * CMP_STRIDE
SEL_BLOCK = 64
TOP_BLOCKS = 16
WINDOW = 512
Q_BLOCK = 128
FORCE_SCORE = 1e4
N_EXPERTS = 64
N_EGROUPS = 8
TOPK_GROUPS = 4
TOP_K = 8
D_EXPERT = 256
D_SHARED = 256
ROUTED_SCALE = 2.5
MOE_BLOCK = 128
EPS = 1e-6
NEG = -1e30

kernel_name = 'hybrid_pool_nsa_moe_decode_step'


def in_sizes():
    kvw = N_KV * HEAD_DIM
    return (POOL_W, N_HEADS * HEAD_DIM, kvw, kvw, kvw, kvw, kvw, kvw, 3 * N_HEADS, 2 * D_MODEL)


def split_cols(z, sizes):
    out, start = [], 0
    for n in sizes:
        out.append(z[..., start:start + n])
        start += n
    return out


def rmsnorm(x, g):
    xf = x.astype(jnp.float32)
    xf = xf * lax.rsqrt(jnp.mean(xf * xf, axis=-1, keepdims=True) + EPS)
    return (xf * g.astype(jnp.float32)).astype(x.dtype)


def adaln(c, w_ada, b_ada):
    mod = jax.nn.silu(c) @ w_ada + b_ada
    return [m[:, None, :] for m in jnp.split(mod, 6, axis=-1)]


def modulate(x, shift, scale):
    return x * (1 + scale) + shift


def masked_softmax(s, mask):
    p = jax.nn.softmax(jnp.where(mask, s, NEG), axis=-1)
    return jnp.where(mask, p, 0.0)


def alibi_slopes():
    return 2.0 ** (-8.0 * (jnp.arange(N_HEADS, dtype=jnp.float32) + 1.0) / N_HEADS)


def pool_mix(v, pos, w_lin, scale):
    cs = jnp.cumsum(v.astype(jnp.float32), axis=1)
    L = v.shape[1]
    outs = []
    for g, w in enumerate(POOL_WINDOWS):
        sl = slice(g * POOL_GW, (g + 1) * POOL_GW)
        cg = cs[..., sl]
        prev = jnp.pad(cg, ((0, 0), (w, 0), (0, 0)))[:, :L]
        cnt = jnp.minimum(pos + 1, w).astype(jnp.float32)[None, :, None]
        outs.append((cg - prev) / cnt - v[..., sl].astype(jnp.float32))
    pooled = jnp.stack(outs, axis=2).astype(v.dtype)
    y = jnp.einsum('blgc,gcd->blgd', pooled, w_lin)
    return y.reshape(v.shape) * scale


def pad_rows(x, mult):
    extra = (-x.shape[1]) % mult
    return jnp.pad(x, ((0, 0), (0, extra), (0, 0), (0, 0)))


def compress(k, w_cmp):
    b, t = k.shape[:2]
    st = k.reshape(b, t // CMP_STRIDE, CMP_STRIDE, N_KV, HEAD_DIM)
    return (jnp.einsum('bnrkd,rkd->bnkd', st[:, :-1], w_cmp[:CMP_STRIDE])
            + jnp.einsum('bnrkd,rkd->bnkd', st[:, 1:], w_cmp[CMP_STRIDE:]))


def nsa_attend(q, q_pos, kc, vc, ks, vs, kw, vw, kw_pos, g):
    b, nq = q.shape[:2]
    dt = q.dtype
    f32 = jnp.float32
    qg = q.reshape(b, nq, N_KV, GROUP, HEAD_DIM) * (HEAD_DIM ** -0.5)
    m = alibi_slopes().reshape(N_KV, GROUP)
    qp = q_pos.astype(f32)
    n_cmp = kc.shape[1]
    c_end = (jnp.arange(n_cmp) * CMP_STRIDE + CMP_BLOCK - 1).astype(f32)
    d_c = qp[:, None] - c_end[None, :]
    s_c = jnp.einsum('bqkgd,bnkd->bkgqn', qg, kc).astype(f32) - m[:, :, None, None] * d_c
    p_c = masked_softmax(s_c, d_c >= 0)
    o_c = jnp.einsum('bkgqn,bnkd->bqkgd', p_c.astype(dt), vc)
    imp = jnp.pad(p_c.sum(axis=2), ((0, 0), (0, 0), (0, 0), (0, 1)))
    imp = imp + jnp.pad(imp[..., :-1], ((0, 0), (0, 0), (0, 0), (1, 0)))
    n_sel = ks.shape[1] // SEL_BLOCK
    imp = imp.reshape(b, N_KV, nq, n_sel, SEL_BLOCK // CMP_STRIDE).sum(-1)
    blk = jnp.arange(n_sel)
    cur = (q_pos // SEL_BLOCK)[:, None]
    forced = (blk == 0) | (blk == cur) | (blk == cur - 1)
    visible = blk * SEL_BLOCK <= q_pos[:, None]
    imp = jnp.where(visible, jnp.where(forced, FORCE_SCORE, imp), NEG)
    n_top = min(TOP_BLOCKS, n_sel)
    _, idx = lax.top_k(imp, n_top)
    ks_b = ks.reshape(b, n_sel, SEL_BLOCK, N_KV, HEAD_DIM).transpose(0, 3, 1, 2, 4)
    vs_b = vs.reshape(b, n_sel, SEL_BLOCK, N_KV, HEAD_DIM).transpose(0, 3, 1, 2, 4)
    bi = jnp.arange(b)[:, None, None, None]
    ki = jnp.arange(N_KV)[None, :, None, None]
    ks_g = ks_b[bi, ki, idx]
    vs_g = vs_b[bi, ki, idx].reshape(b, N_KV, nq, n_top * SEL_BLOCK, HEAD_DIM)
    pos_s = (idx[..., None] * SEL_BLOCK + jnp.arange(SEL_BLOCK)).astype(f32)
    d_s = (qp[:, None, None] - pos_s).reshape(b, N_KV, 1, nq, n_top * SEL_BLOCK)
    s_s = jnp.einsum('bqkgd,bkqnld->bkgqnl', qg, ks_g).astype(f32)
    s_s = s_s.reshape(b, N_KV, GROUP, nq, n_top * SEL_BLOCK) - m[None, :, :, None, None] * d_s
    p_s = masked_softmax(s_s, d_s >= 0)
    o_s = jnp.einsum('bkgqm,bkqmd->bqkgd', p_s.astype(dt), vs_g)
    kwp = kw_pos.astype(f32)
    d_w = qp[:, None] - kwp[None, :]
    mask_w = (d_w >= 0) & (d_w < WINDOW) & (kwp >= 0)[None, :]
    s_w = jnp.einsum('bqkgd,blkd->bkgql', qg, kw).astype(f32) - m[:, :, None, None] * d_w
    p_w = masked_softmax(s_w, mask_w)
    o_w = jnp.einsum('bkgql,blkd->bqkgd', p_w.astype(dt), vw)
    gs = jax.nn.sigmoid(g.astype(f32)).astype(dt).reshape(b, nq, 3, N_KV, GROUP, 1)
    o = gs[:, :, 0] * o_c + gs[:, :, 1] * o_s + gs[:, :, 2] * o_w
    return o.reshape(b, nq, N_HEADS * HEAD_DIM)


def nsa_prompt(q, kc, vc, ks, vs, kw, vw, g, w_cmp_k, w_cmp_v):
    b, s = q.shape[:2]
    kc_c = compress(pad_rows(kc, SEL_BLOCK), w_cmp_k)
    vc_c = compress(pad_rows(vc, SEL_BLOCK), w_cmp_v)
    ks_p = pad_rows(ks, SEL_BLOCK)
    vs_p = pad_rows(vs, SEL_BLOCK)
    kw_p = jnp.pad(kw, ((0, 0), (WINDOW, 0), (0, 0), (0, 0)))
    vw_p = jnp.pad(vw, ((0, 0), (WINDOW, 0), (0, 0), (0, 0)))

    def block(i):
        q0 = i * Q_BLOCK
        return nsa_attend(
            lax.dynamic_slice_in_dim(q, q0, Q_BLOCK, axis=1), q0 + jnp.arange(Q_BLOCK),
            kc_c, vc_c, ks_p, vs_p,
            lax.dynamic_slice_in_dim(kw_p, q0, WINDOW + Q_BLOCK, axis=1),
            lax.dynamic_slice_in_dim(vw_p, q0, WINDOW + Q_BLOCK, axis=1),
            q0 - WINDOW + jnp.arange(WINDOW + Q_BLOCK),
            lax.dynamic_slice_in_dim(g, q0, Q_BLOCK, axis=1))

    o = lax.map(block, jnp.arange(s // Q_BLOCK))
    return o.transpose(1, 0, 2, 3).reshape(b, s, N_HEADS * HEAD_DIM)


def moe_tokens(u, lw):
    f32 = jnp.float32
    n = u.shape[0]
    epg = N_EXPERTS // N_EGROUPS
    scores = jax.nn.sigmoid((u @ lw['w_router']).astype(f32))
    biased = scores + lw['b_router'].astype(f32)
    grp = lax.top_k(biased.reshape(n, N_EGROUPS, epg), 2)[0].sum(-1)
    _, top_g = lax.top_k(grp, TOPK_GROUPS)
    g_keep = jax.nn.one_hot(top_g, N_EGROUPS, dtype=f32).sum(1) > 0
    e_keep = jnp.repeat(g_keep, epg, axis=1)
    _, top_e = lax.top_k(jnp.where(e_keep, biased, NEG), TOP_K)
    w = jnp.take_along_axis(scores, top_e, axis=1)
    w = ROUTED_SCALE * w / jnp.sum(w, axis=-1, keepdims=True)
    gates = jnp.einsum('nk,nke->ne', w, jax.nn.one_hot(top_e, N_EXPERTS, dtype=f32)).astype(u.dtype)
    h = jax.nn.silu(jnp.einsum('nd,edf->nef', u, lw['w_gate'])) * jnp.einsum('nd,edf->nef', u, lw['w_up'])
    routed = jnp.einsum('nef,ne,efd->nd', h, gates, lw['w_down'])
    shared = (jax.nn.silu(u @ lw['ws_gate']) * (u @ lw['ws_up'])) @ lw['ws_down']
    return routed + shared


def moe_prompt(u, lw):
    b, s, d = u.shape
    y = lax.map(lambda t: moe_tokens(t, lw), u.reshape(b * s // MOE_BLOCK, MOE_BLOCK, d))
    return y.reshape(b, s, d)


def moe_sample(u, lw):
    return moe_tokens(u.reshape(-1, u.shape[-1]), lw).reshape(u.shape)


def mixer_inputs(x, mod, lw):
    u = modulate(rmsnorm(x, lw['norm1_g']), mod[0], mod[1])
    return split_cols(u @ lw['w_in'], in_sizes())


def finish_block(x, mod, y_pool, y_nsa, g_merge, lw, moe_apply):
    ga, gb = jnp.split(jax.nn.sigmoid(g_merge.astype(jnp.float32)).astype(x.dtype), 2, axis=-1)
    merged = ga * (y_pool @ lw['w_pool_out']) + gb * (y_nsa @ lw['w_nsa_out'])
    x = x + mod[2] * (merged @ lw['w_o'])
    u = modulate(rmsnorm(x, lw['norm2_g']), mod[3], mod[4])
    return x + mod[5] * moe_apply(u, lw)


def layer_prompt(x, c, lw, w_buf):
    mod = adaln(c, lw['w_ada'], lw['b_ada'])
    vp, q, kc, vc, ks, vs, kw, vw, g_nsa, g_merge = mixer_inputs(x, mod, lw)
    b, s = x.shape[:2]
    kc, vc, ks, vs, kw, vw = [t.reshape(b, s, N_KV, HEAD_DIM) for t in (kc, vc, ks, vs, kw, vw)]
    y_pool = pool_mix(vp, jnp.arange(s), lw['w_pool_lin'], lw['pool_scale'])
    y_nsa = nsa_prompt(q.reshape(b, s, N_HEADS, HEAD_DIM), kc, vc, ks, vs, kw, vw, g_nsa,
                       lw['w_cmp_k'], lw['w_cmp_v'])
    x = finish_block(x, mod, y_pool, y_nsa, g_merge, lw, moe_prompt)
    tail = lambda t: jnp.pad(t, ((0, 0), (w_buf, 0), (0, 0), (0, 0)))[:, -w_buf:]
    return x, (kc, vc, ks, vs, tail(kw), tail(vw), vp[:, -POOL_BUF:])


def layer_sample(x, c, lw, cache_kc, cache_vc, cache_ks, cache_vs, kw_buf, vw_buf, pool_buf, page_table):
    mod = adaln(c, lw['w_ada'], lw['b_ada'])
    vp, q, kc, vc, ks, vs, kw, vw, g_nsa, g_merge = mixer_inputs(x, mod, lw)
    b, s = x.shape[:2]
    past = page_table.shape[1] * PAGE_SIZE
    kc, vc, ks, vs, kw, vw = [t.reshape(b, s, N_KV, HEAD_DIM) for t in (kc, vc, ks, vs, kw, vw)]
    v_ext = jnp.concatenate([pool_buf, vp], axis=1)
    y_pool = pool_mix(v_ext, past - POOL_BUF + jnp.arange(POOL_BUF + s),
                      lw['w_pool_lin'], lw['pool_scale'])[:, POOL_BUF:]

    def full_rows(cache, new):
        old = cache[page_table].reshape(b, past, N_KV, HEAD_DIM)
        return pad_rows(jnp.concatenate([old, new], axis=1), SEL_BLOCK)

    kc_c = compress(full_rows(cache_kc, kc), lw['w_cmp_k'])
    vc_c = compress(full_rows(cache_vc, vc), lw['w_cmp_v'])
    ks_f = full_rows(cache_ks, ks)
    vs_f = full_rows(cache_vs, vs)
    w_buf = kw_buf.shape[1]
    kw_ext = jnp.concatenate([kw_buf, kw], axis=1)
    vw_ext = jnp.concatenate([vw_buf, vw], axis=1)
    y_nsa = nsa_attend(q.reshape(b, s, N_HEADS, HEAD_DIM), past + jnp.arange(s), kc_c, vc_c, ks_f, vs_f,
                       kw_ext, vw_ext, past - w_buf + jnp.arange(w_buf + s), g_nsa)
    x = finish_block(x, mod, y_pool, y_nsa, g_merge, lw, moe_sample)
    return x, (kc, vc, ks, vs, kw_ext[:, -w_buf:], vw_ext[:, -w_buf:], v_ext[:, -POOL_BUF:])


def setup_inputs(seed: int = 0) -> dict:
    key = jax.random.key(seed)
    keys = iter(jax.random.split(key, 48))
    f32 = jnp.float32
    D = D_MODEL

    def nrm(shape, scale):
        return jax.random.normal(next(keys), shape, f32) * scale

    n_pages = PAST_LEN // PAGE_SIZE
    n_used = DEC_BATCH * n_pages
    n_pool = n_used + max(1, n_used // 4)
    w_buf = min(WINDOW, PAST_LEN)
    kv_pool = (DEPTH, n_pool, PAGE_SIZE, N_KV, HEAD_DIM)
    kv_win = (DEPTH, DEC_BATCH, w_buf, N_KV, HEAD_DIM)
    page_table = jax.random.permutation(next(keys), n_pool)[:n_used].reshape(DEC_BATCH, n_pages).astype(jnp.int32)
    in_w = sum(in_sizes())
    qw = N_HEADS * HEAD_DIM
    return {
        'x_prompt': nrm((BATCH, SEQ, D), 1.0),
        'x_sample': nrm((DEC_BATCH, DEC_SEQ, D), 1.0),
        'cache_kc': nrm(kv_pool, 1.0),
        'cache_vc': nrm(kv_pool, 1.0),
        'cache_ks': nrm(kv_pool, 1.0),
        'cache_vs': nrm(kv_pool, 1.0),
        'state_kw': nrm(kv_win, 1.0),
        'state_vw': nrm(kv_win, 1.0),
        'state_pool': nrm((DEPTH, DEC_BATCH, POOL_BUF, POOL_W), 1.0),
        'page_table': page_table,
        'c_prompt': nrm((BATCH, D), 1.0),
        'c_sample': nrm((DEC_BATCH, D), 1.0),
        'norm1_g': 1.0 + nrm((DEPTH, D), 0.02),
        'norm2_g': 1.0 + nrm((DEPTH, D), 0.02),
        'normf_g': 1.0 + nrm((D,), 0.02),
        'w_ada': nrm((DEPTH, D, 6 * D), 0.5 * D ** -0.5),
        'b_ada': nrm((DEPTH, 6 * D), 0.02),
        'w_in': nrm((DEPTH, D, in_w), D ** -0.5),
        'w_pool_lin': nrm((DEPTH, POOL_GROUPS, POOL_GW, POOL_GW), POOL_GW ** -0.5),
        'pool_scale': 1.0 + nrm((DEPTH, POOL_W), 0.02),
        'w_cmp_k': 1.0 / CMP_BLOCK + nrm((DEPTH, CMP_BLOCK, N_KV, HEAD_DIM), 0.5 * CMP_BLOCK ** -0.5),
        'w_cmp_v': 1.0 / CMP_BLOCK + nrm((DEPTH, CMP_BLOCK, N_KV, HEAD_DIM), 0.5 * CMP_BLOCK ** -0.5),
        'w_pool_out': nrm((DEPTH, POOL_W, D), POOL_W ** -0.5),
        'w_nsa_out': nrm((DEPTH, qw, D), qw ** -0.5),
        'w_o': nrm((DEPTH, D, D), D ** -0.5),
        'w_router': nrm((DEPTH, D, N_EXPERTS), D ** -0.5),
        'b_router': nrm((DEPTH, N_EXPERTS), 0.01),
        'w_gate': nrm((DEPTH, N_EXPERTS, D, D_EXPERT), D ** -0.5),
        'w_up': nrm((DEPTH, N_EXPERTS, D, D_EXPERT), D ** -0.5),
        'w_down': nrm((DEPTH, N_EXPERTS, D_EXPERT, D), D_EXPERT ** -0.5),
        'ws_gate': nrm((DEPTH, D, D_SHARED), D ** -0.5),
        'ws_up': nrm((DEPTH, D, D_SHARED), D ** -0.5),
        'ws_down': nrm((DEPTH, D_SHARED, D), D_SHARED ** -0.5),
    }


def reference(x_prompt, x_sample, cache_kc, cache_vc, cache_ks, cache_vs, state_kw, state_vw, state_pool,
              page_table, c_prompt, c_sample, norm1_g, norm2_g, normf_g, w_ada, b_ada, w_in, w_pool_lin,
              pool_scale, w_cmp_k, w_cmp_v, w_pool_out, w_nsa_out, w_o, w_router, b_router, w_gate, w_up,
              w_down, ws_gate, ws_up, ws_down):
    w_buf = state_kw.shape[2]
    xp, xs = x_prompt, x_sample
    outs_p, outs_s = [], []
    for l in range(DEPTH):
        lw = dict(norm1_g=norm1_g[l], norm2_g=norm2_g[l], w_ada=w_ada[l], b_ada=b_ada[l], w_in=w_in[l],
                  w_pool_lin=w_pool_lin[l], pool_scale=pool_scale[l], w_cmp_k=w_cmp_k[l], w_cmp_v=w_cmp_v[l],
                  w_pool_out=w_pool_out[l], w_nsa_out=w_nsa_out[l], w_o=w_o[l], w_router=w_router[l],
                  b_router=b_router[l], w_gate=w_gate[l], w_up=w_up[l], w_down=w_down[l],
                  ws_gate=ws_gate[l], ws_up=ws_up[l], ws_down=ws_down[l])
        xp, st_p = layer_prompt(xp, c_prompt, lw, w_buf)
        xs, st_s = layer_sample(xs, c_sample, lw, cache_kc[l], cache_vc[l], cache_ks[l], cache_vs[l],
                                state_kw[l], state_vw[l], state_pool[l], page_table)
        outs_p.append(st_p)
        outs_s.append(st_s)
    kc_p, vc_p, ks_p, vs_p, kw_p, vw_p, pool_p = [jnp.stack([o[i] for o in outs_p], axis=0) for i in range(7)]
    kc_s, vc_s, ks_s, vs_s, kw_s, vw_s, pool_s = [jnp.stack([o[i] for o in outs_s], axis=0) for i in range(7)]
    y_prompt = rmsnorm(xp, normf_g)
    y_sample = rmsnorm(xs, normf_g)
    return (y_prompt, y_sample, kc_p, vc_p, ks_p, vs_p, kw_p, vw_p, pool_p,
            kc_s, vc_s, ks_s, vs_s, kw_s, vw_s, pool_s)
```

```python
import functools

import jax
import jax.numpy as jnp
from jax import lax
from jax.experimental import pallas as pl
from jax.experimental.pallas import tpu as pltpu

F32 = jnp.float32
BF16 = jnp.bfloat16
I32 = jnp.int32

POOL_WINDOWS = (2, 4, 8, 16)
POOL_GW = 64
POOL_W = 256
POOL_BUF = 15
N_HEADS = 8
HEAD_DIM = 64
N_KV = 2
GROUP = N_HEADS // N_KV
CMP_STRIDE = 16
CMP_BLOCK = 32
SEL_BLOCK = 64
TOP_BLOCKS = 16
WINDOW = 512
Q_BLOCK = 128
FORCE_SCORE = 1e4
N_EXPERTS = 64
N_EGROUPS = 8
EXPERTS_PER_GROUP = N_EXPERTS // N_EGROUPS
TOPK_GROUPS = 4
TOP_K = 8
ROUTED_SCALE = 2.5
EPS = 1e-6
NEG = -1e30
SLOPES = tuple(2.0 ** (-8.0 * (h + 1.0) / N_HEADS) for h in range(N_HEADS))

LANES = 128
QPAD = N_HEADS * LANES
KVW = N_KV * HEAD_DIM
VMEM_LIMIT = 56 * 1024 * 1024


def _cparams(sem):
    return pltpu.CompilerParams(dimension_semantics=sem, vmem_limit_bytes=VMEM_LIMIT)


def _dot(a, b):
    return jnp.dot(a, b, preferred_element_type=F32)


def _dot_nt(a, b):
    return lax.dot_general(a, b, (((1,), (1,)), ((), ())), preferred_element_type=F32)


def _dot_exact(a, b):
    return jnp.dot(a, b, preferred_element_type=F32, precision=lax.Precision.HIGHEST)


def _rows2d(ref):
    v = ref[...]
    return v.reshape(v.shape[-2], v.shape[-1])


def _rmsnorm(x, g):
    return x * lax.rsqrt(jnp.mean(x * x, axis=-1, keepdims=True) + EPS) * g


def _silu(x):
    return x * jax.nn.sigmoid(x)


def _adaln_kernel(c_ref, w_ref, b_ref, o_ref):
    s = _silu(c_ref[...]).astype(BF16)
    o_ref[...] = _dot(s, w_ref[...].astype(BF16)) + b_ref[...]


def _adaln(c, w_ada, b_ada):
    rows, d = c.shape
    n = w_ada.shape[1]
    tn = 512
    return pl.pallas_call(
        _adaln_kernel,
        out_shape=jax.ShapeDtypeStruct((rows, n), F32),
        grid=(n // tn,),
        in_specs=[pl.BlockSpec((rows, d), lambda j: (0, 0)),
                  pl.BlockSpec((d, tn), lambda j: (0, j)),
                  pl.BlockSpec((1, tn), lambda j: (0, j))],
        out_specs=pl.BlockSpec((rows, tn), lambda j: (0, j)),
        compiler_params=_cparams(("arbitrary",)),
        name="adaln",
    )(c, w_ada, b_ada.reshape(1, n))


_C_VP = 0
_C_Q = _C_VP + POOL_W
_C_KV = _C_Q + QPAD
_C_GN = _C_KV + 6 * KVW
_C_GM = _C_GN + LANES


def _pool_window_sums(ext, tm):
    s2 = ext + pltpu.roll(ext, 1, 0)
    s4 = s2 + pltpu.roll(s2, 2, 0)
    s8 = s4 + pltpu.roll(s4, 4, 0)
    s16 = s8 + pltpu.roll(s8, 8, 0)
    grp = lax.broadcasted_iota(I32, (1, POOL_W), 1) // POOL_GW
    pick = jnp.where(grp == 0, s2, jnp.where(grp == 1, s4, jnp.where(grp == 2, s8, s16)))
    return pick[16:16 + tm]


def _in_proj_kernel(x_ref, shift_ref, scale_ref, g_ref, w_ref,
                    vp_ref, q_ref, kc_ref, vc_ref, ks_ref, vs_ref, kw_ref, vw_ref,
                    ksb_ref, vsb_ref, kwb_ref, vwb_ref, gs_ref, gm_ref, *rest, tm, d, with_pool):
    x = x_ref[...].reshape(tm, d)
    u = _rmsnorm(x, g_ref[...]) * (1.0 + _rows2d(scale_ref)) + _rows2d(shift_ref)
    ub = u.astype(BF16)

    def proj(c0, n):
        return _dot(ub, w_ref[:, c0:c0 + n])

    vp = proj(_C_VP, POOL_W)
    vp_ref[...] = vp.reshape(vp_ref.shape)
    q_ref[...] = proj(_C_Q, QPAD).astype(BF16).reshape(q_ref.shape)
    f32_outs = (kc_ref, vc_ref, ks_ref, vs_ref, kw_ref, vw_ref)
    b16_outs = (None, None, ksb_ref, vsb_ref, kwb_ref, vwb_ref)
    for n, (o32, o16) in enumerate(zip(f32_outs, b16_outs)):
        v = proj(_C_KV + n * KVW, KVW)
        o32[...] = v.reshape(o32.shape)
        if o16 is not None:
            o16[...] = v.astype(BF16).reshape(o16.shape)
    gs_ref[...] = jax.nn.sigmoid(proj(_C_GN, LANES)).reshape(gs_ref.shape)
    gm_ref[...] = jax.nn.sigmoid(proj(_C_GM, 2 * d)).reshape(gm_ref.shape)

    if with_pool:
        pooled_ref, halo_ref = rest
        j = pl.program_id(1)

        @pl.when(j == 0)
        def _():
            halo_ref[...] = jnp.zeros_like(halo_ref)

        ext = jnp.concatenate([halo_ref[...], vp], axis=0)
        sums = _pool_window_sums(ext, tm)
        pos = j * tm + lax.broadcasted_iota(I32, (tm, 1), 0)
        wcol = 2 << (lax.broadcasted_iota(I32, (1, POOL_W), 1) // POOL_GW)
        cnt = jnp.minimum(pos + 1, wcol).astype(F32)
        pooled_ref[...] = (sums / cnt - vp).astype(BF16).reshape(pooled_ref.shape)
        halo_ref[...] = vp[tm - 16:tm]


def _in_proj(x3, shift, scale, g1, w2, *, tm, with_pool):
    g, r, d = x3.shape
    nt = r // tm
    per_row = shift.ndim == 2

    def tok(width, dtype):
        return (jax.ShapeDtypeStruct((g, r, width), dtype),
                pl.BlockSpec((1, tm, width), lambda b, j: (b, j, 0)))

    outs = [tok(POOL_W, F32), tok(QPAD, BF16)] + [tok(KVW, F32)] * 6 + [tok(KVW, BF16)] * 4
    outs += [tok(LANES, F32), tok(2 * d, F32)]
    scratch = []
    if with_pool:
        outs.append(tok(POOL_W, BF16))
        scratch.append(pltpu.VMEM((16, POOL_W), F32))
    if per_row:
        mod_spec = lambda col: pl.BlockSpec((tm, d), lambda b, j, col=col: (b * nt + j, col))
    else:
        mod_spec = lambda col: pl.BlockSpec((1, 1, d), lambda b, j, col=col: (b, 0, col))
    kern = functools.partial(_in_proj_kernel, tm=tm, d=d, with_pool=with_pool)
    return pl.pallas_call(
        kern,
        out_shape=[o[0] for o in outs],
        grid=(g, nt),
        in_specs=[pl.BlockSpec((1, tm, d), lambda b, j: (b, j, 0)),
                  mod_spec(0), mod_spec(1),
                  pl.BlockSpec((1, d), lambda b, j: (0, 0)),
                  pl.BlockSpec(w2.shape, lambda b, j: (0, 0))],
        out_specs=[o[1] for o in outs],
        scratch_shapes=scratch,
        compiler_params=_cparams(("arbitrary", "arbitrary")),
        name="in_proj_pool" if with_pool else "in_proj",
    )(x3, shift, scale, g1, w2)


def _compress_kernel(kc_ref, vc_ref, wk_ref, wv_ref, okc_ref, ovc_ref, sh_ref, *, nc):
    last = lax.broadcasted_iota(I32, (nc, 1), 0) == nc - 1
    for src, w_ref, dst in ((kc_ref, wk_ref, okc_ref), (vc_ref, wv_ref, ovc_ref)):
        head = jnp.zeros((nc, KVW), F32)
        tail = jnp.zeros((nc, KVW), F32)
        for r in range(CMP_STRIDE):
            rows = src[pl.ds(r, nc, stride=CMP_STRIDE), :]
            head = head + rows * w_ref[r:r + 1, :]
            tail = tail + rows * w_ref[CMP_STRIDE + r:CMP_STRIDE + r + 1, :]
        sh_ref[0:nc, :] = tail
        sh_ref[nc:nc + 8, :] = jnp.zeros((8, KVW), F32)
        out = head + sh_ref[1:nc + 1, :]
        dst[...] = jnp.where(last, 0.0, out).astype(BF16)


def _compress(kc, vc, wk, wv):
    b, s, _ = kc.shape
    nc = s // CMP_STRIDE
    big = pl.BlockSpec((None, s, KVW), lambda i: (i, 0, 0))
    wsp = pl.BlockSpec((CMP_BLOCK, KVW), lambda i: (0, 0))
    osp = pl.BlockSpec((None, nc, KVW), lambda i: (i, 0, 0))
    return pl.pallas_call(
        functools.partial(_compress_kernel, nc=nc),
        out_shape=[jax.ShapeDtypeStruct((b, nc, KVW), BF16)] * 2,
        grid=(b,),
        in_specs=[big, big, wsp, wsp],
        out_specs=[osp, osp],
        scratch_shapes=[pltpu.VMEM((nc + 8, KVW), F32)],
        compiler_params=_cparams(("arbitrary",)),
        name="compress",
    )(kc, vc, wk, wv)


def _topk_mask(vals, blk_f, n_top):
    sel = jnp.zeros(vals.shape, F32)
    big = float(vals.shape[1])
    for _ in range(n_top):
        mx = jnp.max(vals, axis=1, keepdims=True)
        first = jnp.min(jnp.where(vals == mx, blk_f, big), axis=1, keepdims=True)
        hit = blk_f == first
        sel = jnp.where(hit, 1.0, sel)
        vals = jnp.where(hit, -jnp.inf, vals)
    return sel


def _importance_matrix(nc, nsel):
    j = jnp.arange(nc)[:, None]
    s = jnp.arange(nsel)[None, :]
    r = SEL_BLOCK // CMP_STRIDE
    a = (j >= r * s) & (j <= r * s + r - 1)
    b = (j + 1 >= r * s) & (j + 1 <= r * s + r - 1)
    return a.astype(F32) + b.astype(F32)


def _nsa_prompt_kernel(q_ref, gs_ref, kc_ref, vc_ref, ks_ref, vs_ref, kw_ref, vw_ref, imat_ref,
                       y_ref, q_scr, m_scr, l_scr, acc_scr, p_scr, pw_scr, *, seq, tk, wl):
    i = pl.program_id(1)
    q0 = i * Q_BLOCK
    nq = Q_BLOCK
    nc = kc_ref.shape[0]
    nsel = seq // SEL_BLOCK
    n_top = min(TOP_BLOCKS, nsel)
    rowq = lax.broadcasted_iota(I32, (nq, 1), 0)
    qpos = q0 + rowq
    lane = lax.broadcasted_iota(I32, (1, LANES), 1)
    gs = gs_ref[...]

    cend = lax.broadcasted_iota(I32, (1, nc), 1) * CMP_STRIDE + (CMP_BLOCK - 1)
    mask_c = qpos >= cend
    cbias = (cend - q0).astype(F32)
    kc = kc_ref[...]
    vc = vc_ref[...]
    blk = lax.broadcasted_iota(I32, (1, nsel), 1)
    blk_f = blk.astype(F32)
    cur = qpos // SEL_BLOCK
    forced = (blk == 0) | (blk == cur) | (blk == cur - 1)
    visible = blk * SEL_BLOCK <= qpos
    jrow = lax.broadcasted_iota(I32, (nsel, 1), 0)
    ws = pl.multiple_of(jnp.maximum(q0 - WINDOW, 0), Q_BLOCK)
    wpos = ws + lax.broadcasted_iota(I32, (1, wl), 1)
    dw = qpos - wpos
    valid_w = lax.bitcast_convert_type(dw, jnp.uint32) < WINDOW
    wbias = (wpos - q0).astype(F32)
    n_tiles = (q0 + nq + tk - 1) // tk

    for k in range(N_KV):
        for g in range(GROUP):
            h = k * GROUP + g
            q_scr[g * nq:(g + 1) * nq, :] = q_ref[:, h * LANES:(h + 1) * LANES]
        qk = q_scr[...]

        s_all = _dot_nt(qk, kc)
        psum = jnp.zeros((nq, nc), F32)
        for g in range(GROUP):
            h = k * GROUP + g
            s = jnp.where(mask_c, s_all[g * nq:(g + 1) * nq] + SLOPES[h] * cbias, NEG)
            mx = jnp.max(s, axis=1, keepdims=True)
            e = jnp.where(mask_c, jnp.exp(s - mx), 0.0)
            l = jnp.sum(e, axis=1, keepdims=True)
            p = e * jnp.where(l > 0.0, 1.0 / l, 0.0)
            psum = psum + p
            pw_scr[g * nq:(g + 1) * nq, 0:nc] = p.astype(BF16)
        o_c = _dot(pw_scr[:, 0:nc], vc)

        imp = _dot_exact(psum, imat_ref[...])
        vals = jnp.where(visible, jnp.where(forced, FORCE_SCORE, imp), NEG)
        sel_b = _topk_mask(vals, blk_f, n_top).astype(BF16)

        m_scr[...] = jnp.full(m_scr.shape, NEG, F32)
        l_scr[...] = jnp.zeros(l_scr.shape, F32)
        acc_scr[...] = jnp.zeros(acc_scr.shape, F32)

        def sel_body(t, carry, k=k, sel_b=sel_b):
            k0 = pl.multiple_of(t * tk, tk)
            kt = ks_ref[pl.ds(k0, tk), :]
            vt = vs_ref[pl.ds(k0, tk), :]
            s_all = _dot_nt(q_scr[...], kt)
            kpos = k0 + lax.broadcasted_iota(I32, (1, tk), 1)
            expand = jnp.where(jrow == kpos // SEL_BLOCK, 1.0, 0.0).astype(BF16)
            chosen = _dot(sel_b, expand)
            neg = jnp.where(qpos >= kpos, jnp.where(chosen > 0.5, 0.0, NEG), NEG)
            kb = (kpos - q0).astype(F32)
            for g in range(GROUP):
                h = k * GROUP + g
                rs = slice(g * nq, (g + 1) * nq)
                s = s_all[rs] + SLOPES[h] * kb + neg
                m_old = m_scr[rs]
                m_new = jnp.maximum(m_old, jnp.max(s, axis=1, keepdims=True))
                alpha = jnp.exp(m_old - m_new)
                p = jnp.exp(s - m_new)
                l_scr[rs] = alpha * l_scr[rs] + jnp.sum(p, axis=1, keepdims=True)
                m_scr[rs] = m_new
                acc_scr[rs] = acc_scr[rs] * alpha
                p_scr[rs] = p.astype(BF16)
            acc_scr[...] += _dot(p_scr[...], vt)
            return carry

        lax.fori_loop(0, n_tiles, sel_body, 0)

        kwt = kw_ref[pl.ds(ws, wl), :]
        vwt = vw_ref[pl.ds(ws, wl), :]
        s_all = _dot_nt(qk, kwt)
        for g in range(GROUP):
            h = k * GROUP + g
            s = jnp.where(valid_w, s_all[g * nq:(g + 1) * nq] + SLOPES[h] * wbias, NEG)
            mx = jnp.max(s, axis=1, keepdims=True)
            e = jnp.exp(s - mx)
            p = e / jnp.sum(e, axis=1, keepdims=True)
            pw_scr[g * nq:(g + 1) * nq, 0:wl] = p.astype(BF16)
        o_w = _dot(pw_scr[:, 0:wl], vwt)

        half = (lane // HEAD_DIM) == k
        for g in range(GROUP):
            h = k * GROUP + g
            rs = slice(g * nq, (g + 1) * nq)
            o_s = acc_scr[rs] / l_scr[rs]
            o = (gs[:, h:h + 1] * o_c[rs] + gs[:, N_HEADS + h:N_HEADS + h + 1] * o_s
                 + gs[:, 2 * N_HEADS + h:2 * N_HEADS + h + 1] * o_w[rs])
            y_ref[:, h * LANES:(h + 1) * LANES] = jnp.where(half, o, 0.0).astype(BF16)


def _nsa_prompt(q, gs, kcmp, vcmp, ksb, vsb, kwb, vwb):
    b, s, _ = q.shape
    nq = Q_BLOCK
    nc = kcmp.shape[1]
    nsel = s // SEL_BLOCK
    tk = 256
    wl = WINDOW + Q_BLOCK
    assert s % tk == 0 and s >= wl
    imat = _importance_matrix(nc, nsel)
    tile = lambda w: pl.BlockSpec((None, nq, w), lambda bi, i: (bi, i, 0))
    full = lambda r: pl.BlockSpec((None, r, KVW), lambda bi, i: (bi, 0, 0))
    pw = max(nc, wl)
    return pl.pallas_call(
        functools.partial(_nsa_prompt_kernel, seq=s, tk=tk, wl=wl),
        out_shape=jax.ShapeDtypeStruct((b, s, QPAD), BF16),
        grid=(b, s // nq),
        in_specs=[tile(QPAD), tile(LANES), full(nc), full(nc), full(s), full(s), full(s), full(s),
                  pl.BlockSpec(imat.shape, lambda bi, i: (0, 0))],
        out_specs=tile(QPAD),
        scratch_shapes=[pltpu.VMEM((GROUP * nq, LANES), BF16),
                        pltpu.VMEM((GROUP * nq, 1), F32),
                        pltpu.VMEM((GROUP * nq, 1), F32),
                        pltpu.VMEM((GROUP * nq, LANES), F32),
                        pltpu.VMEM((GROUP * nq, tk), BF16),
                        pltpu.VMEM((GROUP * nq, pw), BF16)],
        compiler_params=_cparams(("arbitrary", "arbitrary")),
        name="nsa_prompt",
    )(q, gs, kcmp, vcmp, ksb, vsb, kwb, vwb, imat)


def _nsa_sample_kernel(pt_ref, q_ref, gate_ref, kcn_ref, vcn_ref, ksn_ref, vsn_ref, kwn_ref, vwn_ref,
                       vpn_ref, skw_ref, svw_ref, spool_ref, wk_ref, wv_ref, imat_ref, emat_ref,
                       ckc_ref, cvc_ref, cks_ref, cvs_ref,
                       o_ref, pooled_ref, buf, win_scr, vext_scr, sem,
                       *, sb, ns, past, n_pages, page, n_seq, ncv, ncp, nks, wls, nselp, n_sel):
    step = pl.program_id(0)
    caches = (ckc_ref, cvc_ref, cks_ref, cvs_ref)
    nrow = ns * N_HEADS

    def copies(n, slot):
        out = []
        for c, cref in enumerate(caches):
            for p in range(n_pages):
                pg = pt_ref[n * n_pages + p]
                out.append(pltpu.make_async_copy(cref.at[pg], buf.at[slot, c, pl.ds(p * page, page), :],
                                                 sem.at[slot]))
        return out

    @pl.when(step == 0)
    def _():
        buf[:, :, past:, :] = jnp.zeros((2, 4, buf.shape[2] - past, KVW), F32)
        win_scr[...] = jnp.zeros_like(win_scr)
        vext_scr[...] = jnp.zeros_like(vext_scr)
        for cp in copies(0, 0):
            cp.start()

    row = lax.broadcasted_iota(I32, (nrow, 1), 0)
    hrow = row % N_HEADS
    qpos = past + row // N_HEADS
    slope = jnp.exp2(-8.0 * (hrow.astype(F32) + 1.0) / N_HEADS)
    kvrow = hrow // GROUP
    lane = lax.broadcasted_iota(I32, (1, LANES), 1)
    half = (lane // HEAD_DIM) == kvrow
    grow = (row // N_HEADS) * N_KV + kvrow
    row8 = lax.broadcasted_iota(I32, (ns * N_KV, 1), 0)
    qpos8 = past + lax.broadcasted_iota(I32, (ns * N_KV, 1), 0) // N_KV
    blk = lax.broadcasted_iota(I32, (1, nselp), 1)
    blk_f = blk.astype(F32)
    cur = qpos8 // SEL_BLOCK
    forced = (blk == 0) | (blk == cur) | (blk == cur - 1)
    visible = (blk * SEL_BLOCK <= qpos8)
    inrange = blk < n_sel
    cend = lax.broadcasted_iota(I32, (1, ncp), 1) * CMP_STRIDE + (CMP_BLOCK - 1)
    mask_c = qpos >= cend
    bias_c = slope * (cend - qpos).astype(F32)
    kpos = lax.broadcasted_iota(I32, (1, nks), 1)
    causal_s = qpos >= kpos
    bias_s = slope * (kpos - qpos).astype(F32)
    wbuf = wls[0]
    wpos = past - wbuf + lax.broadcasted_iota(I32, (1, wls[1]), 1)
    dw = qpos - wpos
    valid_w = lax.bitcast_convert_type(dw, jnp.uint32) < WINDOW
    bias_w = slope * (wpos - qpos).astype(F32)
    prow = lax.broadcasted_iota(I32, (vext_scr.shape[0], 1), 0)
    wcol = 2 << (lax.broadcasted_iota(I32, (1, POOL_W), 1) // POOL_GW)
    n_top = min(TOP_BLOCKS, n_sel)

    def softmax_rows(s, mask):
        s = jnp.where(mask, s, NEG)
        mx = jnp.max(s, axis=1, keepdims=True)
        e = jnp.where(mask, jnp.exp(s - mx), 0.0)
        l = jnp.sum(e, axis=1, keepdims=True)
        return e * jnp.where(l > 0.0, 1.0 / l, 0.0)

    def seq_body(r):
        n = step * sb + r
        slot = r % 2

        @pl.when(n + 1 < n_seq)
        def _():
            for cp in copies(n + 1, 1 - slot):
                cp.start()

        for cp in copies(n, slot):
            cp.wait()

        r4 = r * ns
        new_rows = (kcn_ref, vcn_ref, ksn_ref, vsn_ref)
        for c in range(4):
            buf[slot, c, past:past + ns, :] = new_rows[c][pl.ds(r4, ns), :]

        qall = q_ref[pl.ds(r * nrow, nrow), :]
        gates = gate_ref[pl.ds(r * nrow, nrow), :]

        cmp = []
        for c, w_ref in ((0, wk_ref), (1, wv_ref)):
            acc = jnp.zeros((ncv, KVW), F32)
            for j in range(CMP_BLOCK):
                acc = acc + buf[slot, c, pl.ds(j, ncv, stride=CMP_STRIDE), :] * w_ref[j:j + 1, :]
            cmp.append(jnp.concatenate([acc, jnp.zeros((ncp - ncv, KVW), F32)], axis=0).astype(BF16))
        p_c = softmax_rows(_dot_nt(qall, cmp[0]) + bias_c, mask_c)
        o_c = _dot(p_c.astype(BF16), cmp[1])

        psum = jnp.zeros((ns * N_KV, ncp), F32)
        for i in range(ns * N_KV):
            r0 = (i // N_KV) * N_HEADS + (i % N_KV) * GROUP
            psum = jnp.where(row8 == i, jnp.sum(p_c[r0:r0 + GROUP], axis=0, keepdims=True), psum)
        imp = _dot_exact(psum, imat_ref[...])
        vals = jnp.where(inrange, jnp.where(visible, jnp.where(forced, FORCE_SCORE, imp), NEG), -jnp.inf)
        sel8 = _topk_mask(vals, blk_f, n_top)
        sel_rows = jnp.zeros((nrow, nselp), F32)
        for i in range(ns * N_KV):
            sel_rows = jnp.where(grow == i, sel8[i:i + 1], sel_rows)
        chosen = _dot(sel_rows.astype(BF16), emat_ref[...])

        ks = buf[slot, 2, 0:nks, :].astype(BF16)
        vs = buf[slot, 3, 0:nks, :].astype(BF16)
        p_s = softmax_rows(_dot_nt(qall, ks) + bias_s, causal_s & (chosen > 0.5))
        o_s = _dot(p_s.astype(BF16), vs)

        win_k = (skw_ref, kwn_ref)
        win_v = (svw_ref, vwn_ref)
        outs_w = []
        for state_ref, new_ref in (win_k, win_v):
            win_scr[0:wbuf, :] = state_ref[r]
            win_scr[wbuf:wbuf + ns, :] = new_ref[pl.ds(r4, ns), :]
            outs_w.append(win_scr[...].astype(BF16))
        p_w = softmax_rows(_dot_nt(qall, outs_w[0]) + bias_w, valid_w)
        o_w = _dot(p_w.astype(BF16), outs_w[1])

        o = gates[:, 0:1] * o_c + gates[:, 1:2] * o_s + gates[:, 2:3] * o_w
        o_ref[pl.ds(r * nrow, nrow), :] = jnp.where(half, o, 0.0).astype(BF16)

        vext_scr[0:POOL_BUF, :] = spool_ref[r]
        vext_scr[POOL_BUF:POOL_BUF + ns, :] = vpn_ref[pl.ds(r4, ns), :]
        ext = vext_scr[...]
        for t in range(ns):
            hi = POOL_BUF + t
            inwin = (prow <= hi) & (prow > hi - wcol)
            ssum = jnp.sum(jnp.where(inwin, ext, 0.0), axis=0, keepdims=True)
            cnt = jnp.minimum(past + t + 1, wcol).astype(F32)
            pooled_ref[pl.ds(r4 + t, 1), :] = ssum / cnt - ext[hi:hi + 1, :]

    for r in range(sb):
        seq_body(r)


def _nsa_sample(page_table, q_rows, gate_rows, new6, vp_new, state_kw, state_vw, state_pool, wk, wv, caches):
    n_seq, n_pages = page_table.shape
    page = caches[0].shape[1]
    past = n_pages * page
    ns = vp_new.shape[0] // n_seq
    wbuf = state_kw.shape[1]
    sb = 2
    nrow = ns * N_HEADS
    t_pad = -(-(past + ns) // SEL_BLOCK) * SEL_BLOCK
    n_cmp = t_pad // CMP_STRIDE - 1
    ncv = -(-n_cmp // 8) * 8
    ncp = -(-ncv // LANES) * LANES
    nks = -(-t_pad // LANES) * LANES
    n_sel = t_pad // SEL_BLOCK
    nselp = LANES
    assert n_sel <= nselp
    wlp = -(-(wbuf + ns) // LANES) * LANES
    buf_rows = -(-max(CMP_STRIDE * (ncv - 1) + CMP_BLOCK, nks) // 8) * 8
    imat = _importance_matrix(ncp, nselp)
    emat = (jnp.arange(nselp)[:, None] == (jnp.arange(nks)[None, :] // SEL_BLOCK)).astype(BF16)

    seqblk = lambda rows, w: pl.BlockSpec((sb * rows, w), lambda i, pt: (i, 0))
    const = lambda a: pl.BlockSpec(a.shape, lambda i, pt: (0,) * a.ndim)
    kern = functools.partial(
        _nsa_sample_kernel, sb=sb, ns=ns, past=past, n_pages=n_pages, page=page, n_seq=n_seq,
        ncv=ncv, ncp=ncp, nks=nks, wls=(wbuf, wlp), nselp=nselp, n_sel=n_sel)
    grid_spec = pltpu.PrefetchScalarGridSpec(
        num_scalar_prefetch=1,
        grid=(n_seq // sb,),
        in_specs=[seqblk(nrow, LANES), seqblk(nrow, LANES)] + [seqblk(ns, KVW)] * 6 + [seqblk(ns, POOL_W)]
        + [pl.BlockSpec((sb, wbuf, KVW), lambda i, pt: (i, 0, 0))] * 2
        + [pl.BlockSpec((sb, POOL_BUF, POOL_W), lambda i, pt: (i, 0, 0))]
        + [const(wk), const(wv), const(imat), const(emat)]
        + [pl.BlockSpec(memory_space=pl.ANY)] * 4,
        out_specs=[seqblk(nrow, LANES), seqblk(ns, POOL_W)],
        scratch_shapes=[pltpu.VMEM((2, 4, buf_rows, KVW), F32),
                        pltpu.VMEM((wlp, KVW), F32),
                        pltpu.VMEM((24, POOL_W), F32),
                        pltpu.SemaphoreType.DMA((2,))],
    )
    return pl.pallas_call(
        kern,
        out_shape=[jax.ShapeDtypeStruct((n_seq * nrow, LANES), BF16),
                   jax.ShapeDtypeStruct((n_seq * ns, POOL_W), F32)],
        grid_spec=grid_spec,
        compiler_params=_cparams(("arbitrary",)),
        name="nsa_sample",
    )(page_table.reshape(-1), q_rows, gate_rows, *new6, vp_new, state_kw, state_vw, state_pool, wk, wv,
      imat, emat, *caches)


def _route(logits_t, bias_col, tm):
    sc = jax.nn.sigmoid(logits_t)
    biased = sc + bias_col
    epg = EXPERTS_PER_GROUP
    row8 = lax.broadcasted_iota(I32, (epg, tm), 0).astype(F32)
    ninf = -jnp.inf
    grp = jnp.zeros((N_EGROUPS, tm), F32)
    for g in range(N_EGROUPS):
        bg = biased[g * epg:(g + 1) * epg]
        m1 = jnp.max(bg, axis=0, keepdims=True)
        first = jnp.min(jnp.where(bg == m1, row8, float(epg)), axis=0, keepdims=True)
        m2 = jnp.max(jnp.where(row8 == first, ninf, bg), axis=0, keepdims=True)
        grp = jnp.where(row8 == float(g), m1 + m2, grp)
    keep = jnp.zeros((N_EGROUPS, tm), F32)
    vals = grp
    for _ in range(TOPK_GROUPS):
        mx = jnp.max(vals, axis=0, keepdims=True)
        first = jnp.min(jnp.where(vals == mx, row8, float(N_EGROUPS)), axis=0, keepdims=True)
        hit = row8 == first
        keep = jnp.where(hit, 1.0, keep)
        vals = jnp.where(hit, ninf, vals)
    masked = jnp.concatenate(
        [jnp.where(keep[g:g + 1] > 0.5, biased[g * epg:(g + 1) * epg], NEG) for g in range(N_EGROUPS)], axis=0)
    rowe = lax.broadcasted_iota(I32, (N_EXPERTS, tm), 0).astype(F32)
    chosen = jnp.zeros((N_EXPERTS, tm), F32)
    vals = masked
    for _ in range(TOP_K):
        mx = jnp.max(vals, axis=0, keepdims=True)
        first = jnp.min(jnp.where(vals == mx, rowe, float(N_EXPERTS)), axis=0, keepdims=True)
        hit = rowe == first
        chosen = jnp.where(hit, sc, chosen)
        vals = jnp.where(hit, ninf, vals)
    return ROUTED_SCALE * chosen / jnp.sum(chosen, axis=0, keepdims=True)


def _finish_kernel(x_ref, pooled_ref, y_ref, gm_ref, g1_ref, shift_ref, scale_ref,
                   wlin_ref, pscale_ref, wpo_ref, wno_ref, wo_ref, n2_ref, wr_ref, br_ref,
                   x1_ref, u2_ref, gates_ref, *, tm, d):
    x = x_ref[...].reshape(tm, d)
    pooled = pooled_ref[...].reshape(tm, POOL_W).astype(BF16)
    y_pool = _dot(pooled, wlin_ref[...]) * pscale_ref[...]
    a = _dot(y_pool.astype(BF16), wpo_ref[...])
    b = _dot(y_ref[...].reshape(tm, QPAD), wno_ref[...])
    gm = gm_ref[...].reshape(tm, 2 * d)
    merged = gm[:, :d] * a + gm[:, d:] * b
    x1 = x + _rows2d(g1_ref) * _dot(merged.astype(BF16), wo_ref[...])
    x1_ref[...] = x1.reshape(x1_ref.shape)
    u2 = _rmsnorm(x1, n2_ref[...]) * (1.0 + _rows2d(scale_ref)) + _rows2d(shift_ref)
    u2b = u2.astype(BF16)
    u2_ref[...] = u2b.reshape(u2_ref.shape)
    logits_t = _dot_nt(wr_ref[...], u2b)
    gates_t = _route(logits_t[:N_EXPERTS], br_ref[...], tm)
    gates_t = jnp.concatenate([gates_t, jnp.zeros((LANES - N_EXPERTS, tm), F32)], axis=0)
    gates_ref[...] = gates_t.T.reshape(gates_ref.shape)


def _finish(x3, pooled, ynsa, gm, mods, wts, *, tm):
    g, r, d = x3.shape
    nt = r // tm
    g1, shift2, scale2 = mods
    per_row = g1.ndim == 2
    tok = lambda w: pl.BlockSpec((1, tm, w), lambda b, j: (b, j, 0))
    if per_row:
        mod_spec = lambda col: pl.BlockSpec((tm, d), lambda b, j, col=col: (b * nt + j, col))
    else:
        mod_spec = lambda col: pl.BlockSpec((1, 1, d), lambda b, j, col=col: (b, 0, col))
    const = lambda a: pl.BlockSpec(a.shape, lambda b, j: (0,) * a.ndim)
    return pl.pallas_call(
        functools.partial(_finish_kernel, tm=tm, d=d),
        out_shape=[jax.ShapeDtypeStruct((g, r, d), F32), jax.ShapeDtypeStruct((g, r, d), BF16),
                   jax.ShapeDtypeStruct((g, r, LANES), F32)],
        grid=(g, nt),
        in_specs=[tok(d), tok(POOL_W), tok(QPAD), tok(2 * d), mod_spec(2), mod_spec(3), mod_spec(4)]
        + [const(w) for w in wts],
        out_specs=[tok(d), tok(d), tok(LANES)],
        compiler_params=_cparams(("arbitrary", "arbitrary")),
        name="finish",
    )(x3, pooled, ynsa, gm, g1, shift2, scale2, *wts)


def _moe_kernel(u_ref, gates_ref, x1_ref, g2_ref, nf_ref, wg_ref, wu_ref, wd_ref, sg_ref, su_ref, sd_ref,
                y_ref, acc_ref, *, tm, d):
    e = pl.program_id(2)
    u = u_ref[...].reshape(tm, d)

    @pl.when(e == 0)
    def _():
        hs = _silu(_dot(u, sg_ref[...])) * _dot(u, su_ref[...])
        acc_ref[...] = _dot(hs.astype(BF16), sd_ref[...])

    h = _silu(_dot(u, wg_ref[...].astype(BF16))) * _dot(u, wu_ref[...].astype(BF16))
    gates = gates_ref[...].reshape(tm, LANES)
    lane = lax.broadcasted_iota(I32, (1, LANES), 1)
    gate = jnp.sum(jnp.where(lane == e, gates, 0.0), axis=1, keepdims=True)
    acc_ref[...] += _dot((h * gate).astype(BF16), wd_ref[...].astype(BF16))

    @pl.when(e == pl.num_programs(2) - 1)
    def _():
        x2 = x1_ref[...].reshape(tm, d) + _rows2d(g2_ref) * acc_ref[...]
        y_ref[...] = _rmsnorm(x2, nf_ref[...]).reshape(y_ref.shape)


def _moe(u2, gates, x1, g2, normf, w_gate, w_up, w_down, sg, su, sd, *, tm):
    g, r, d = x1.shape
    nt = r // tm
    ne, _, f = w_gate.shape
    per_row = g2.ndim == 2
    tok = lambda w: pl.BlockSpec((1, tm, w), lambda b, j, e: (b, j, 0))
    if per_row:
        g2_spec = pl.BlockSpec((tm, d), lambda b, j, e: (b * nt + j, 5))
    else:
        g2_spec = pl.BlockSpec((1, 1, d), lambda b, j, e: (b, 0, 5))
    const = lambda a: pl.BlockSpec(a.shape, lambda b, j, e: (0,) * a.ndim)
    return pl.pallas_call(
        functools.partial(_moe_kernel, tm=tm, d=d),
        out_shape=jax.ShapeDtypeStruct((g, r, d), F32),
        grid=(g, nt, ne),
        in_specs=[tok(d), tok(LANES), tok(d), g2_spec, const(normf),
                  pl.BlockSpec((None, d, f), lambda b, j, e: (e, 0, 0)),
                  pl.BlockSpec((None, d, f), lambda b, j, e: (e, 0, 0)),
                  pl.BlockSpec((None, f, d), lambda b, j, e: (e, 0, 0)),
                  const(sg), const(su), const(sd)],
        out_specs=tok(d),
        scratch_shapes=[pltpu.VMEM((tm, d), F32)],
        compiler_params=_cparams(("arbitrary", "arbitrary", "arbitrary")),
        name="moe",
    )(u2, gates, x1, g2, normf, w_gate, w_up, w_down, sg, su, sd)


def _kv_slot_mask():
    return (jnp.arange(N_HEADS)[:, None] // GROUP == jnp.arange(N_KV)[None, :]).astype(F32)


def _prep_w_in(w_in, d):
    q0 = POOL_W
    kv0 = q0 + N_HEADS * HEAD_DIM
    gn0 = kv0 + 6 * KVW
    gm0 = gn0 + 3 * N_HEADS
    wq = w_in[:, q0:kv0].reshape(d, N_HEADS, 1, HEAD_DIM) * (HEAD_DIM ** -0.5)
    wq = (wq * _kv_slot_mask()[None, :, :, None]).reshape(d, QPAD)
    wgn = jnp.pad(w_in[:, gn0:gm0], ((0, 0), (0, LANES - 3 * N_HEADS)))
    return jnp.concatenate([w_in[:, :q0], wq, w_in[:, kv0:gn0], wgn, w_in[:, gm0:]], axis=1).astype(BF16)


def _prep_w_nsa_out(w, d):
    w = w.reshape(N_HEADS, 1, HEAD_DIM, d) * _kv_slot_mask()[:, :, None, None]
    return w.reshape(QPAD, d).astype(BF16)


def _block_diag(w_lin):
    g, c, _ = w_lin.shape
    eye = jnp.eye(g, dtype=F32)
    return (w_lin[:, :, None, :] * eye[:, None, :, None]).reshape(g * c, g * c).astype(BF16)


def kernel(x_prompt, x_sample, cache_kc, cache_vc, cache_ks, cache_vs, state_kw, state_vw, state_pool,
           page_table, c_prompt, c_sample, norm1_g, norm2_g, normf_g, w_ada, b_ada, w_in, w_pool_lin,
           pool_scale, w_cmp_k, w_cmp_v, w_pool_out, w_nsa_out, w_o, w_router, b_router, w_gate, w_up,
           w_down, ws_gate, ws_up, ws_down):
    depth = w_in.shape[0]
    assert depth == 1, "single-layer stack"
    bsz, seq, d = x_prompt.shape
    n_seq, ns, _ = x_sample.shape
    wbuf = state_kw.shape[2]
    lyr = 0

    c_all = jnp.concatenate([c_prompt, c_sample], axis=0)
    rows = c_all.shape[0]
    rows_p = -(-rows // 8) * 8
    mod = _adaln(jnp.pad(c_all, ((0, rows_p - rows), (0, 0))), w_ada[lyr], b_ada[lyr])
    mod_p = mod[:bsz].reshape(bsz, 1, 6 * d)
    mod_s = jnp.repeat(mod[bsz:bsz + n_seq], ns, axis=0)

    w2 = _prep_w_in(w_in[lyr], d)
    g1n = norm1_g[lyr].reshape(1, d)
    wk = w_cmp_k[lyr].reshape(CMP_BLOCK, KVW)
    wv = w_cmp_v[lyr].reshape(CMP_BLOCK, KVW)
    fin_w = (_block_diag(w_pool_lin[lyr]), pool_scale[lyr].reshape(1, POOL_W), w_pool_out[lyr].astype(BF16),
             _prep_w_nsa_out(w_nsa_out[lyr], d), w_o[lyr].astype(BF16), norm2_g[lyr].reshape(1, d),
             jnp.pad(w_router[lyr].T, ((0, LANES - N_EXPERTS), (0, 0))).astype(BF16),
             b_router[lyr].reshape(N_EXPERTS, 1))
    moe_w = (w_gate[lyr], w_up[lyr], w_down[lyr], ws_gate[lyr].astype(BF16), ws_up[lyr].astype(BF16),
             ws_down[lyr].astype(BF16))
    nf = normf_g.reshape(1, d)

    tm_p = 512
    (vp, q, kc, vc, ks, vs, kw, vw, ksb, vsb, kwb, vwb, gs, gm, pooled) = _in_proj(
        x_prompt, mod_p, mod_p, g1n, w2, tm=tm_p, with_pool=True)
    kcmp, vcmp = _compress(kc, vc, wk, wv)
    ynsa = _nsa_prompt(q, gs, kcmp, vcmp, ksb, vsb, kwb, vwb)
    x1, u2, gates = _finish(x_prompt, pooled, ynsa, gm, (mod_p, mod_p, mod_p), fin_w, tm=tm_p)
    y_prompt = _moe(u2, gates, x1, mod_p, nf, *moe_w, tm=1024)

    n_tok = n_seq * ns
    xs3 = x_sample.reshape(1, n_tok, d)
    tm_s = 128
    (vp_s, q_s, kc_s, vc_s, ks_s, vs_s, kw_s, vw_s, _, _, _, _, gs_s, gm_s) = _in_proj(
        xs3, mod_s, mod_s, g1n, w2, tm=tm_s, with_pool=False)
    two = lambda a: a.reshape(n_tok, a.shape[-1])
    q_rows = q_s.reshape(n_tok * N_HEADS, LANES)
    gate_rows = two(gs_s)[:, :3 * N_HEADS].reshape(n_tok, 3, N_HEADS).transpose(0, 2, 1)
    gate_rows = jnp.pad(gate_rows.reshape(n_tok * N_HEADS, 3), ((0, 0), (0, LANES - 3)))
    n_pool = cache_kc.shape[1]
    page = cache_kc.shape[2]
    caches = [c[lyr].reshape(n_pool, page, KVW) for c in (cache_kc, cache_vc, cache_ks, cache_vs)]
    o_rows, pooled_s = _nsa_sample(
        page_table, q_rows, gate_rows, [two(a) for a in (kc_s, vc_s, ks_s, vs_s, kw_s, vw_s)], two(vp_s),
        state_kw[lyr].reshape(n_seq, wbuf, KVW), state_vw[lyr].reshape(n_seq, wbuf, KVW), state_pool[lyr],
        wk, wv, caches)
    ynsa_s = o_rows.reshape(1, n_tok, QPAD)
    x1_s, u2_s, gates_s = _finish(xs3, pooled_s.reshape(1, n_tok, POOL_W), ynsa_s, gm_s,
                                  (mod_s, mod_s, mod_s), fin_w, tm=tm_s)
    y_sample = _moe(u2_s, gates_s, x1_s, mod_s, nf, *moe_w, tm=n_tok).reshape(n_seq, ns, d)

    kvp = lambda a: a.reshape(1, bsz, seq, N_KV, HEAD_DIM)
    tailp = lambda a: jnp.pad(a, ((0, 0), (wbuf, 0), (0, 0)))[:, -wbuf:].reshape(1, bsz, wbuf, N_KV, HEAD_DIM)
    kvs = lambda a: a.reshape(1, n_seq, ns, N_KV, HEAD_DIM)
    wins = lambda st, new: jnp.concatenate(
        [st[lyr], new.reshape(n_seq, ns, N_KV, HEAD_DIM)], axis=1)[None, :, -wbuf:]
    pool_p = vp[:, -POOL_BUF:][None]
    pool_s = jnp.concatenate([state_pool[lyr], vp_s.reshape(n_seq, ns, POOL_W)], axis=1)[None, :, -POOL_BUF:]
    return (y_prompt, y_sample, kvp(kc), kvp(vc), kvp(ks), kvp(vs), tailp(kw), tailp(vw), pool_p,
            kvs(kc_s), kvs(vc_s), kvs(ks_s), kvs(vs_s), wins(state_kw, kw_s), wins(state_vw, vw_s), pool_s)
```

```python
import functools

import jax
import jax.numpy as jnp
from jax import lax
from jax.experimental import pallas as pl
from jax.experimental.pallas import tpu as pltpu

F32 = jnp.float32
BF16 = jnp.bfloat16
I32 = jnp.int32

POOL_WINDOWS = (2, 4, 8, 16)
POOL_GW = 64
POOL_W = 256
POOL_BUF = 15
N_HEADS = 8
HEAD_DIM = 64
N_KV = 2
GROUP = N_HEADS // N_KV
CMP_STRIDE = 16
CMP_BLOCK = 32
SEL_BLOCK = 64
TOP_BLOCKS = 16
WINDOW = 512
Q_BLOCK = 128
FORCE_SCORE = 1e4
N_EXPERTS = 64
N_EGROUPS = 8
EXPERTS_PER_GROUP = N_EXPERTS // N_EGROUPS
TOPK_GROUPS = 4
TOP_K = 8
ROUTED_SCALE = 2.5
EPS = 1e-6
NEG = -1e30
SLOPES = tuple(2.0 ** (-8.0 * (h + 1.0) / N_HEADS) for h in range(N_HEADS))

LANES = 128
QPAD = N_HEADS * LANES
KVW = N_KV * HEAD_DIM
VMEM_LIMIT = 56 * 1024 * 1024


def _cparams(sem):
    return pltpu.CompilerParams(dimension_semantics=sem, vmem_limit_bytes=VMEM_LIMIT)


def _dot(a, b):
    return jnp.dot(a, b, preferred_element_type=F32)


def _dot_nt(a, b):
    return lax.dot_general(a, b, (((1,), (1,)), ((), ())), preferred_element_type=F32)


def _dot_exact(a, b):
    return jnp.dot(a, b, preferred_element_type=F32, precision=lax.Precision.HIGHEST)


def _rows2d(ref):
    v = ref[...]
    return v.reshape(v.shape[-2], v.shape[-1])


def _rmsnorm(x, g):
    return x * lax.rsqrt(jnp.mean(x * x, axis=-1, keepdims=True) + EPS) * g


def _silu(x):
    return x * jax.nn.sigmoid(x)


def _adaln_kernel(c_ref, w_ref, b_ref, o_ref):
    s = _silu(c_ref[...]).astype(BF16)
    o_ref[...] = _dot(s, w_ref[...].astype(BF16)) + b_ref[...]


def _adaln(c, w_ada, b_ada):
    rows, d = c.shape
    n = w_ada.shape[1]
    tn = 512
    return pl.pallas_call(
        _adaln_kernel,
        out_shape=jax.ShapeDtypeStruct((rows, n), F32),
        grid=(n // tn,),
        in_specs=[pl.BlockSpec((rows, d), lambda j: (0, 0)),
                  pl.BlockSpec((d, tn), lambda j: (0, j)),
                  pl.BlockSpec((1, tn), lambda j: (0, j))],
        out_specs=pl.BlockSpec((rows, tn), lambda j: (0, j)),
        compiler_params=_cparams(("arbitrary",)),
        name="adaln",
    )(c, w_ada, b_ada.reshape(1, n))


_C_VP = 0
_C_Q = _C_VP + POOL_W
_C_KV = _C_Q + QPAD
_C_GN = _C_KV + 6 * KVW
_C_GM = _C_GN + LANES


def _pool_window_sums(ext, tm):
    s2 = ext + pltpu.roll(ext, 1, 0)
    s4 = s2 + pltpu.roll(s2, 2, 0)
    s8 = s4 + pltpu.roll(s4, 4, 0)
    s16 = s8 + pltpu.roll(s8, 8, 0)
    grp = lax.broadcasted_iota(I32, (1, POOL_W), 1) // POOL_GW
    pick = jnp.where(grp == 0, s2, jnp.where(grp == 1, s4, jnp.where(grp == 2, s8, s16)))
    return pick[16:16 + tm]


def _in_proj_kernel(x_ref, shift_ref, scale_ref, g_ref, w_ref,
                    vp_ref, kc_ref, vc_ref, ks_ref, vs_ref, kw_ref, vw_ref, gm_ref, *rest, tm, d, prompt):
    x = x_ref[...].reshape(tm, d)
    u = _rmsnorm(x, g_ref[...]) * (1.0 + _rows2d(scale_ref)) + _rows2d(shift_ref)
    ub = u.astype(BF16)

    def proj(c0, n):
        return _dot(ub, w_ref[:, c0:c0 + n])

    vp = proj(_C_VP, POOL_W)
    vp_ref[...] = vp.reshape(vp_ref.shape)
    kv = []
    for n, o32 in enumerate((kc_ref, vc_ref, ks_ref, vs_ref, kw_ref, vw_ref)):
        v = proj(_C_KV + n * KVW, KVW)
        o32[...] = v.reshape(o32.shape)
        kv.append(v)
    gm_ref[...] = jax.nn.sigmoid(proj(_C_GM, 2 * d)).reshape(gm_ref.shape)
    gs = jax.nn.sigmoid(proj(_C_GN, LANES))

    if not prompt:
        q_ref, gs_ref = rest
        q_ref[...] = proj(_C_Q, QPAD).astype(BF16).reshape(q_ref.shape)
        gs_ref[...] = gs.reshape(gs_ref.shape)
    else:
        ksb_ref, kwb_ref, vst_ref, vwt_ref, qt_ref, gst_ref, pooled_ref, halo_ref = rest
        ksb_ref[...] = kv[2].astype(BF16).reshape(ksb_ref.shape)
        kwb_ref[...] = kv[4].astype(BF16).reshape(kwb_ref.shape)
        vst_ref[...] = kv[3].T.astype(BF16).reshape(vst_ref.shape)
        vwt_ref[...] = kv[5].T.astype(BF16).reshape(vwt_ref.shape)
        gst_ref[...] = gs.T.reshape(gst_ref.shape)
        for h in range(N_HEADS):
            qt_ref[0, h] = proj(_C_Q + h * LANES, LANES).T.astype(BF16)
        j = pl.program_id(1)

        @pl.when(j == 0)
        def _():
            halo_ref[...] = jnp.zeros_like(halo_ref)

        ext = jnp.concatenate([halo_ref[...], vp], axis=0)
        sums = _pool_window_sums(ext, tm)
        pos = j * tm + lax.broadcasted_iota(I32, (tm, 1), 0)
        wcol = 2 << (lax.broadcasted_iota(I32, (1, POOL_W), 1) // POOL_GW)
        cnt = jnp.minimum(pos + 1, wcol).astype(F32)
        pooled_ref[...] = (sums / cnt - vp).astype(BF16).reshape(pooled_ref.shape)
        halo_ref[...] = vp[tm - 16:tm]


def _in_proj(x3, shift, scale, g1, w2, *, tm, prompt):
    g, r, d = x3.shape
    nt = r // tm
    per_row = shift.ndim == 2

    def tok(width, dtype):
        return (jax.ShapeDtypeStruct((g, r, width), dtype),
                pl.BlockSpec((1, tm, width), lambda b, j: (b, j, 0)))

    def tok_t(rows, dtype):
        return (jax.ShapeDtypeStruct((g, rows, r), dtype),
                pl.BlockSpec((1, rows, tm), lambda b, j: (b, 0, j)))

    outs = [tok(POOL_W, F32)] + [tok(KVW, F32)] * 6 + [tok(2 * d, F32)]
    scratch = []
    if prompt:
        outs += [tok(KVW, BF16), tok(KVW, BF16), tok_t(KVW, BF16), tok_t(KVW, BF16)]
        outs.append((jax.ShapeDtypeStruct((g, N_HEADS, LANES, r), BF16),
                     pl.BlockSpec((1, N_HEADS, LANES, tm), lambda b, j: (b, 0, 0, j))))
        outs += [tok_t(LANES, F32), tok(POOL_W, BF16)]
        scratch.append(pltpu.VMEM((16, POOL_W), F32))
    else:
        outs += [tok(QPAD, BF16), tok(LANES, F32)]
    if per_row:
        mod_spec = lambda col: pl.BlockSpec((tm, d), lambda b, j, col=col: (b * nt + j, col))
    else:
        mod_spec = lambda col: pl.BlockSpec((1, 1, d), lambda b, j, col=col: (b, 0, col))
    kern = functools.partial(_in_proj_kernel, tm=tm, d=d, prompt=prompt)
    return pl.pallas_call(
        kern,
        out_shape=[o[0] for o in outs],
        grid=(g, nt),
        in_specs=[pl.BlockSpec((1, tm, d), lambda b, j: (b, j, 0)),
                  mod_spec(0), mod_spec(1),
                  pl.BlockSpec((1, d), lambda b, j: (0, 0)),
                  pl.BlockSpec(w2.shape, lambda b, j: (0, 0))],
        out_specs=[o[1] for o in outs],
        scratch_shapes=scratch,
        compiler_params=_cparams(("arbitrary", "arbitrary")),
        name="in_proj_prompt" if prompt else "in_proj_sample",
    )(x3, shift, scale, g1, w2)


def _compress_kernel(kc_ref, vc_ref, wk_ref, wv_ref, okc_ref, ovc_ref, sh_ref, *, nc):
    last = lax.broadcasted_iota(I32, (nc, 1), 0) == nc - 1
    for src, w_ref, dst in ((kc_ref, wk_ref, okc_ref), (vc_ref, wv_ref, ovc_ref)):
        head = jnp.zeros((nc, KVW), F32)
        tail = jnp.zeros((nc, KVW), F32)
        for r in range(CMP_STRIDE):
            rows = src[pl.ds(r, nc, stride=CMP_STRIDE), :]
            head = head + rows * w_ref[r:r + 1, :]
            tail = tail + rows * w_ref[CMP_STRIDE + r:CMP_STRIDE + r + 1, :]
        sh_ref[0:nc, :] = tail
        sh_ref[nc:nc + 8, :] = jnp.zeros((8, KVW), F32)
        out = jnp.where(last, 0.0, head + sh_ref[1:nc + 1, :])
        dst[...] = (out if dst is okc_ref else out.T).astype(BF16)


def _compress(kc, vc, wk, wv):
    b, s, _ = kc.shape
    nc = s // CMP_STRIDE
    big = pl.BlockSpec((None, s, KVW), lambda i: (i, 0, 0))
    wsp = pl.BlockSpec((CMP_BLOCK, KVW), lambda i: (0, 0))
    return pl.pallas_call(
        functools.partial(_compress_kernel, nc=nc),
        out_shape=[jax.ShapeDtypeStruct((b, nc, KVW), BF16), jax.ShapeDtypeStruct((b, KVW, nc), BF16)],
        grid=(b,),
        in_specs=[big, big, wsp, wsp],
        out_specs=[pl.BlockSpec((None, nc, KVW), lambda i: (i, 0, 0)),
                   pl.BlockSpec((None, KVW, nc), lambda i: (i, 0, 0))],
        scratch_shapes=[pltpu.VMEM((nc + 8, KVW), F32)],
        compiler_params=_cparams(("arbitrary",)),
        name="compress",
    )(kc, vc, wk, wv)


def _topk_mask(vals, blk_f, n_top, axis=1):
    sel = jnp.zeros(vals.shape, F32)
    big = float(vals.shape[axis])
    for _ in range(n_top):
        mx = jnp.max(vals, axis=axis, keepdims=True)
        first = jnp.min(jnp.where(vals == mx, blk_f, big), axis=axis, keepdims=True)
        hit = blk_f == first
        sel = jnp.where(hit, 1.0, sel)
        vals = jnp.where(hit, -jnp.inf, vals)
    return sel


def _pos_features(pos):
    hi = (pos // SEL_BLOCK).astype(F32)[:, None]
    lo = (pos % SEL_BLOCK).astype(F32)[:, None]
    return jnp.concatenate([hi, lo, jnp.zeros((pos.shape[0], LANES - 2), F32)], axis=1).astype(BF16)


def _importance_matrix(nc, nsel):
    j = jnp.arange(nc)[:, None]
    s = jnp.arange(nsel)[None, :]
    r = SEL_BLOCK // CMP_STRIDE
    a = (j >= r * s) & (j <= r * s + r - 1)
    b = (j + 1 >= r * s) & (j + 1 <= r * s + r - 1)
    return a.astype(F32) + b.astype(F32)


def _nsa_prompt_kernel(qt_ref, gst_ref, kc_ref, vct_ref, ks_ref, vst_ref, kw_ref, vwt_ref, imat_ref,
                       cfeat_ref, wfeat_ref, qfeat_ref, slope_ref,
                       y_ref, qk_scr, m_scr, l_scr, acc_scr, o_scr, sel_scr, flag_scr, *, seq, tk, wl):
    i = pl.program_id(1)
    q0 = i * Q_BLOCK
    nq = Q_BLOCK
    gq = GROUP * nq
    nc = kc_ref.shape[0]
    nsel = seq // SEL_BLOCK
    n_top = min(TOP_BLOCKS, nsel)
    blk_per_tile = tk // SEL_BLOCK
    qpos = q0 + lax.broadcasted_iota(I32, (1, nq), 1)
    gst = gst_ref[...]

    cend = lax.broadcasted_iota(I32, (nc, nq), 0) * CMP_STRIDE + (CMP_BLOCK - 1)
    mask_c = qpos >= cend
    kc = jnp.concatenate([kc_ref[...], cfeat_ref[...]], axis=1)
    vct = vct_ref[...]
    blk = lax.broadcasted_iota(I32, (nsel, nq), 0)
    blk_f = blk.astype(F32)
    cur = qpos // SEL_BLOCK
    forced = (blk == 0) | (blk == cur) | (blk == cur - 1)
    visible = blk * SEL_BLOCK <= qpos
    ws = pl.multiple_of(jnp.maximum(q0 - WINDOW, 0), Q_BLOCK)
    wpos = ws + lax.broadcasted_iota(I32, (wl, nq), 0)
    valid_w = lax.bitcast_convert_type(qpos - wpos, jnp.uint32) < WINDOW
    n_tiles = (q0 + nq + tk - 1) // tk
    half_rows = lax.broadcasted_iota(I32, (KVW, nq), 0) // HEAD_DIM
    tile_pos = lax.broadcasted_iota(I32, (SEL_BLOCK, nq), 0)

    def lanes4(x):
        return jnp.concatenate([x] * GROUP, axis=1)

    def gate_row(branch, k):
        r0 = branch * N_HEADS + k * GROUP
        return jnp.concatenate([gst[r0 + g:r0 + g + 1] for g in range(GROUP)], axis=1)

    mask_c4 = lanes4(mask_c)
    valid_w4 = lanes4(valid_w)
    kwt = jnp.concatenate([kw_ref[pl.ds(ws, wl), :], wfeat_ref[...]], axis=1)
    vwtt = vwt_ref[:, pl.ds(ws, wl)]

    for k in range(N_KV):
        for g in range(GROUP):
            qk_scr[k, 0:LANES, g * nq:(g + 1) * nq] = qt_ref[k * GROUP + g]
        qk_scr[k, LANES:2 * LANES, :] = qfeat_ref[k]
        qk = qk_scr[k]

        s = jnp.where(mask_c4, _dot(kc, qk), NEG)
        e = jnp.where(mask_c4, jnp.exp(s - jnp.max(s, axis=0, keepdims=True)), 0.0)
        l = jnp.sum(e, axis=0, keepdims=True)
        p = e * jnp.where(l > 0.0, 1.0 / l, 0.0)
        o_c = _dot(vct, p.astype(BF16))
        psum = p[:, 0:nq]
        for g in range(1, GROUP):
            psum = psum + p[:, g * nq:(g + 1) * nq]

        imp = _dot_exact(imat_ref[...], psum)
        vals = jnp.where(visible, jnp.where(forced, FORCE_SCORE, imp), NEG)
        sel = jnp.where(visible, _topk_mask(vals, blk_f, n_top, axis=0), 0.0)
        sel_scr[k] = jnp.where(sel > 0.5, 0.0, NEG)
        blk_any = jnp.max(sel, axis=1, keepdims=True)
        for t in range(seq // tk):
            hit = (jnp.max(blk_any[t * blk_per_tile:(t + 1) * blk_per_tile, :]) > 0.5).astype(I32)
            flag_scr[t] = hit if k == 0 else flag_scr[t] | hit

        s = jnp.where(valid_w4, _dot(kwt, qk), NEG)
        e = jnp.exp(s - jnp.max(s, axis=0, keepdims=True))
        p = e / jnp.sum(e, axis=0, keepdims=True)
        o_w = _dot(vwtt, p.astype(BF16))
        o_scr[k] = gate_row(0, k) * o_c + gate_row(2, k) * o_w

    m_scr[...] = jnp.full(m_scr.shape, NEG, F32)
    l_scr[...] = jnp.zeros(l_scr.shape, F32)
    acc_scr[...] = jnp.zeros(acc_scr.shape, F32)

    def sel_tile(t):
        k0 = pl.multiple_of(t * tk, tk)
        kt = jnp.concatenate([ks_ref[pl.ds(k0, tk), :], wfeat_ref[0:tk, :]], axis=1)
        vtt = vst_ref[:, pl.ds(k0, tk)]
        causal = [qpos >= k0 + j * SEL_BLOCK + tile_pos for j in range(blk_per_tile)]
        base = (k0 - q0).astype(F32)
        for k in range(N_KV):
            neg = []
            for j in range(blk_per_tile):
                row = sel_scr[k, pl.ds(t * blk_per_tile + j, 1), :]
                neg.append(jnp.where(causal[j], jnp.broadcast_to(row, (SEL_BLOCK, nq)), NEG))
            neg = lanes4(jnp.concatenate(neg, axis=0))
            off = slope_ref[k] * base
            s = _dot(kt, qk_scr[k]) + neg
            m_old = m_scr[k]
            m_new = jnp.maximum(m_old, jnp.max(s, axis=0, keepdims=True) + off)
            alpha = jnp.exp(m_old - m_new)
            p = jnp.exp(s - (m_new - off))
            l_scr[k] = alpha * l_scr[k] + jnp.sum(p, axis=0, keepdims=True)
            m_scr[k] = m_new
            acc_scr[k] = acc_scr[k] * alpha + _dot(vtt, p.astype(BF16))

    def sel_body(t, carry):
        pl.when(flag_scr[t] > 0)(functools.partial(sel_tile, t))
        return carry

    lax.fori_loop(0, n_tiles, sel_body, 0)

    for k in range(N_KV):
        o = o_scr[k] + gate_row(1, k) * (acc_scr[k] / l_scr[k])
        for g in range(GROUP):
            h = k * GROUP + g
            oh = jnp.where(half_rows == k, o[:, g * nq:(g + 1) * nq], 0.0)
            y_ref[:, h * LANES:(h + 1) * LANES] = oh.T.astype(BF16)


def _nsa_prompt(qt, gst, kcmp, vcmpt, ksb, vst, kwb, vwt):
    b, _, _, s = qt.shape
    nq = Q_BLOCK
    gq = GROUP * nq
    nc = kcmp.shape[1]
    nsel = s // SEL_BLOCK
    tk = 256
    wl = WINDOW + Q_BLOCK
    assert s % tk == 0 and s >= wl
    assert s // SEL_BLOCK <= 2 * LANES, "position // 64 must stay exact in bf16"
    imat = _importance_matrix(nc, nsel).T
    cfeat = _pos_features(jnp.arange(nc) * CMP_STRIDE + (CMP_BLOCK - 1))
    wfeat = _pos_features(jnp.arange(wl))
    slope_rows = jnp.repeat(jnp.asarray(SLOPES, F32).reshape(N_KV, 1, GROUP), nq, axis=2)
    qfeat = jnp.concatenate([slope_rows * SEL_BLOCK, slope_rows, jnp.zeros((N_KV, LANES - 2, gq), F32)],
                            axis=1).astype(BF16)
    rows = lambda r: pl.BlockSpec((None, r, KVW), lambda bi, i: (bi, 0, 0))
    cols = lambda c: pl.BlockSpec((None, KVW, c), lambda bi, i: (bi, 0, 0))
    const = lambda a: pl.BlockSpec(a.shape, lambda bi, i: (0,) * a.ndim)
    return pl.pallas_call(
        functools.partial(_nsa_prompt_kernel, seq=s, tk=tk, wl=wl),
        out_shape=jax.ShapeDtypeStruct((b, s, QPAD), BF16),
        grid=(b, s // nq),
        in_specs=[pl.BlockSpec((None, N_HEADS, LANES, nq), lambda bi, i: (bi, 0, 0, i)),
                  pl.BlockSpec((None, LANES, nq), lambda bi, i: (bi, 0, i)),
                  rows(nc), cols(nc), rows(s), cols(s), rows(s), cols(s),
                  const(imat), const(cfeat), const(wfeat), const(qfeat), const(slope_rows)],
        out_specs=pl.BlockSpec((None, nq, QPAD), lambda bi, i: (bi, i, 0)),
        scratch_shapes=[pltpu.VMEM((N_KV, 2 * LANES, gq), BF16),
                        pltpu.VMEM((N_KV, 1, gq), F32),
                        pltpu.VMEM((N_KV, 1, gq), F32),
                        pltpu.VMEM((N_KV, KVW, gq), F32),
                        pltpu.VMEM((N_KV, KVW, gq), F32),
                        pltpu.VMEM((N_KV, nsel, nq), F32),
                        pltpu.SMEM((s // tk,), I32)],
        compiler_params=_cparams(("arbitrary", "arbitrary")),
        name="nsa_prompt",
    )(qt, gst, kcmp, vcmpt, ksb, vst, kwb, vwt, imat, cfeat, wfeat, qfeat, slope_rows)


def _nsa_sample_kernel(pt_ref, q_ref, gate_ref, kcn_ref, vcn_ref, ksn_ref, vsn_ref, kwn_ref, vwn_ref,
                       vpn_ref, skw_ref, svw_ref, spool_ref, wk_ref, wv_ref, imat_ref, emat_ref,
                       ckc_ref, cvc_ref, cks_ref, cvs_ref,
                       o_ref, pooled_ref, buf, win_scr, vext_scr, sem,
                       *, sb, ns, past, n_pages, page, n_seq, ncv, ncp, nks, wls, nselp, n_sel):
    step = pl.program_id(0)
    caches = (ckc_ref, cvc_ref, cks_ref, cvs_ref)
    nrow = ns * N_HEADS

    def copies(n, slot):
        out = []
        for c, cref in enumerate(caches):
            for p in range(n_pages):
                pg = pt_ref[n * n_pages + p]
                out.append(pltpu.make_async_copy(cref.at[pg], buf.at[slot, c, pl.ds(p * page, page), :],
                                                 sem.at[slot]))
        return out

    @pl.when(step == 0)
    def _():
        buf[:, :, past:, :] = jnp.zeros((2, 4, buf.shape[2] - past, KVW), F32)
        win_scr[...] = jnp.zeros_like(win_scr)
        vext_scr[...] = jnp.zeros_like(vext_scr)
        for cp in copies(0, 0):
            cp.start()

    row = lax.broadcasted_iota(I32, (nrow, 1), 0)
    hrow = row % N_HEADS
    qpos = past + row // N_HEADS
    slope = jnp.exp2(-8.0 * (hrow.astype(F32) + 1.0) / N_HEADS)
    kvrow = hrow // GROUP
    lane = lax.broadcasted_iota(I32, (1, LANES), 1)
    half = (lane // HEAD_DIM) == kvrow
    grow = (row // N_HEADS) * N_KV + kvrow
    row8 = lax.broadcasted_iota(I32, (ns * N_KV, 1), 0)
    qpos8 = past + lax.broadcasted_iota(I32, (ns * N_KV, 1), 0) // N_KV
    blk = lax.broadcasted_iota(I32, (1, nselp), 1)
    blk_f = blk.astype(F32)
    cur = qpos8 // SEL_BLOCK
    forced = (blk == 0) | (blk == cur) | (blk == cur - 1)
    visible = (blk * SEL_BLOCK <= qpos8)
    inrange = blk < n_sel
    cend = lax.broadcasted_iota(I32, (1, ncp), 1) * CMP_STRIDE + (CMP_BLOCK - 1)
    mask_c = qpos >= cend
    bias_c = slope * (cend - qpos).astype(F32)
    kpos = lax.broadcasted_iota(I32, (1, nks), 1)
    causal_s = qpos >= kpos
    bias_s = slope * (kpos - qpos).astype(F32)
    wbuf = wls[0]
    wpos = past - wbuf + lax.broadcasted_iota(I32, (1, wls[1]), 1)
    dw = qpos - wpos
    valid_w = lax.bitcast_convert_type(dw, jnp.uint32) < WINDOW
    bias_w = slope * (wpos - qpos).astype(F32)
    prow = lax.broadcasted_iota(I32, (vext_scr.shape[0], 1), 0)
    wcol = 2 << (lax.broadcasted_iota(I32, (1, POOL_W), 1) // POOL_GW)
    n_top = min(TOP_BLOCKS, n_sel)

    def softmax_rows(s, mask):
        s = jnp.where(mask, s, NEG)
        mx = jnp.max(s, axis=1, keepdims=True)
        e = jnp.where(mask, jnp.exp(s - mx), 0.0)
        l = jnp.sum(e, axis=1, keepdims=True)
        return e * jnp.where(l > 0.0, 1.0 / l, 0.0)

    def seq_body(r):
        n = step * sb + r
        slot = r % 2

        @pl.when(n + 1 < n_seq)
        def _():
            for cp in copies(n + 1, 1 - slot):
                cp.start()

        for cp in copies(n, slot):
            cp.wait()

        r4 = r * ns
        new_rows = (kcn_ref, vcn_ref, ksn_ref, vsn_ref)
        for c in range(4):
            buf[slot, c, past:past + ns, :] = new_rows[c][pl.ds(r4, ns), :]

        qall = q_ref[pl.ds(r * nrow, nrow), :]
        gates = gate_ref[pl.ds(r * nrow, nrow), :]

        cmp = []
        for c, w_ref in ((0, wk_ref), (1, wv_ref)):
            acc = jnp.zeros((ncv, KVW), F32)
            for j in range(CMP_BLOCK):
                acc = acc + buf[slot, c, pl.ds(j, ncv, stride=CMP_STRIDE), :] * w_ref[j:j + 1, :]
            cmp.append(jnp.concatenate([acc, jnp.zeros((ncp - ncv, KVW), F32)], axis=0).astype(BF16))
        p_c = softmax_rows(_dot_nt(qall, cmp[0]) + bias_c, mask_c)
        o_c = _dot(p_c.astype(BF16), cmp[1])

        psum = jnp.zeros((ns * N_KV, ncp), F32)
        for i in range(ns * N_KV):
            r0 = (i // N_KV) * N_HEADS + (i % N_KV) * GROUP
            psum = jnp.where(row8 == i, jnp.sum(p_c[r0:r0 + GROUP], axis=0, keepdims=True), psum)
        imp = _dot_exact(psum, imat_ref[...])
        vals = jnp.where(inrange, jnp.where(visible, jnp.where(forced, FORCE_SCORE, imp), NEG), -jnp.inf)
        sel8 = _topk_mask(vals, blk_f, n_top)
        sel_rows = jnp.zeros((nrow, nselp), F32)
        for i in range(ns * N_KV):
            sel_rows = jnp.where(grow == i, sel8[i:i + 1], sel_rows)
        chosen = _dot(sel_rows.astype(BF16), emat_ref[...])

        ks = buf[slot, 2, 0:nks, :].astype(BF16)
        vs = buf[slot, 3, 0:nks, :].astype(BF16)
        p_s = softmax_rows(_dot_nt(qall, ks) + bias_s, causal_s & (chosen > 0.5))
        o_s = _dot(p_s.astype(BF16), vs)

        win_k = (skw_ref, kwn_ref)
        win_v = (svw_ref, vwn_ref)
        outs_w = []
        for state_ref, new_ref in (win_k, win_v):
            win_scr[0:wbuf, :] = state_ref[r]
            win_scr[wbuf:wbuf + ns, :] = new_ref[pl.ds(r4, ns), :]
            outs_w.append(win_scr[...].astype(BF16))
        p_w = softmax_rows(_dot_nt(qall, outs_w[0]) + bias_w, valid_w)
        o_w = _dot(p_w.astype(BF16), outs_w[1])

        o = gates[:, 0:1] * o_c + gates[:, 1:2] * o_s + gates[:, 2:3] * o_w
        o_ref[pl.ds(r * nrow, nrow), :] = jnp.where(half, o, 0.0).astype(BF16)

        vext_scr[0:POOL_BUF, :] = spool_ref[r]
        vext_scr[POOL_BUF:POOL_BUF + ns, :] = vpn_ref[pl.ds(r4, ns), :]
        ext = vext_scr[...]
        for t in range(ns):
            hi = POOL_BUF + t
            inwin = (prow <= hi) & (prow > hi - wcol)
            ssum = jnp.sum(jnp.where(inwin, ext, 0.0), axis=0, keepdims=True)
            cnt = jnp.minimum(past + t + 1, wcol).astype(F32)
            pooled_ref[pl.ds(r4 + t, 1), :] = ssum / cnt - ext[hi:hi + 1, :]

    for r in range(sb):
        seq_body(r)


def _nsa_sample(page_table, q_rows, gate_rows, new6, vp_new, state_kw, state_vw, state_pool, wk, wv, caches):
    n_seq, n_pages = page_table.shape
    page = caches[0].shape[1]
    past = n_pages * page
    ns = vp_new.shape[0] // n_seq
    wbuf = state_kw.shape[1]
    sb = 2
    nrow = ns * N_HEADS
    t_pad = -(-(past + ns) // SEL_BLOCK) * SEL_BLOCK
    n_cmp = t_pad // CMP_STRIDE - 1
    ncv = -(-n_cmp // 8) * 8
    ncp = -(-ncv // LANES) * LANES
    nks = -(-t_pad // LANES) * LANES
    n_sel = t_pad // SEL_BLOCK
    nselp = LANES
    assert n_sel <= nselp
    wlp = -(-(wbuf + ns) // LANES) * LANES
    buf_rows = -(-max(CMP_STRIDE * (ncv - 1) + CMP_BLOCK, nks) // 8) * 8
    imat = _importance_matrix(ncp, nselp)
    emat = (jnp.arange(nselp)[:, None] == (jnp.arange(nks)[None, :] // SEL_BLOCK)).astype(BF16)

    seqblk = lambda rows, w: pl.BlockSpec((sb * rows, w), lambda i, pt: (i, 0))
    const = lambda a: pl.BlockSpec(a.shape, lambda i, pt: (0,) * a.ndim)
    kern = functools.partial(
        _nsa_sample_kernel, sb=sb, ns=ns, past=past, n_pages=n_pages, page=page, n_seq=n_seq,
        ncv=ncv, ncp=ncp, nks=nks, wls=(wbuf, wlp), nselp=nselp, n_sel=n_sel)
    grid_spec = pltpu.PrefetchScalarGridSpec(
        num_scalar_prefetch=1,
        grid=(n_seq // sb,),
        in_specs=[seqblk(nrow, LANES), seqblk(nrow, LANES)] + [seqblk(ns, KVW)] * 6 + [seqblk(ns, POOL_W)]
        + [pl.BlockSpec((sb, wbuf, KVW), lambda i, pt: (i, 0, 0))] * 2
        + [pl.BlockSpec((sb, POOL_BUF, POOL_W), lambda i, pt: (i, 0, 0))]
        + [const(wk), const(wv), const(imat), const(emat)]
        + [pl.BlockSpec(memory_space=pl.ANY)] * 4,
        out_specs=[seqblk(nrow, LANES), seqblk(ns, POOL_W)],
        scratch_shapes=[pltpu.VMEM((2, 4, buf_rows, KVW), F32),
                        pltpu.VMEM((wlp, KVW), F32),
                        pltpu.VMEM((24, POOL_W), F32),
                        pltpu.SemaphoreType.DMA((2,))],
    )
    return pl.pallas_call(
        kern,
        out_shape=[jax.ShapeDtypeStruct((n_seq * nrow, LANES), BF16),
                   jax.ShapeDtypeStruct((n_seq * ns, POOL_W), F32)],
        grid_spec=grid_spec,
        compiler_params=_cparams(("arbitrary",)),
        name="nsa_sample",
    )(page_table.reshape(-1), q_rows, gate_rows, *new6, vp_new, state_kw, state_vw, state_pool, wk, wv,
      imat, emat, *caches)


def _route(logits_t, bias_col, tm):
    sc = jax.nn.sigmoid(logits_t)
    biased = sc + bias_col
    epg = EXPERTS_PER_GROUP
    row8 = lax.broadcasted_iota(I32, (epg, tm), 0).astype(F32)
    ninf = -jnp.inf
    grp = jnp.zeros((N_EGROUPS, tm), F32)
    for g in range(N_EGROUPS):
        bg = biased[g * epg:(g + 1) * epg]
        m1 = jnp.max(bg, axis=0, keepdims=True)
        first = jnp.min(jnp.where(bg == m1, row8, float(epg)), axis=0, keepdims=True)
        m2 = jnp.max(jnp.where(row8 == first, ninf, bg), axis=0, keepdims=True)
        grp = jnp.where(row8 == float(g), m1 + m2, grp)
    keep = jnp.zeros((N_EGROUPS, tm), F32)
    vals = grp
    for _ in range(TOPK_GROUPS):
        mx = jnp.max(vals, axis=0, keepdims=True)
        first = jnp.min(jnp.where(vals == mx, row8, float(N_EGROUPS)), axis=0, keepdims=True)
        hit = row8 == first
        keep = jnp.where(hit, 1.0, keep)
        vals = jnp.where(hit, ninf, vals)
    masked = jnp.concatenate(
        [jnp.where(keep[g:g + 1] > 0.5, biased[g * epg:(g + 1) * epg], NEG) for g in range(N_EGROUPS)], axis=0)
    rowe = lax.broadcasted_iota(I32, (N_EXPERTS, tm), 0).astype(F32)
    chosen = jnp.zeros((N_EXPERTS, tm), F32)
    vals = masked
    for _ in range(TOP_K):
        mx = jnp.max(vals, axis=0, keepdims=True)
        first = jnp.min(jnp.where(vals == mx, rowe, float(N_EXPERTS)), axis=0, keepdims=True)
        hit = rowe == first
        chosen = jnp.where(hit, sc, chosen)
        vals = jnp.where(hit, ninf, vals)
    return ROUTED_SCALE * chosen / jnp.sum(chosen, axis=0, keepdims=True)


def _finish_kernel(x_ref, pooled_ref, y_ref, gm_ref, g1_ref, shift_ref, scale_ref,
                   wlin_ref, pscale_ref, wpo_ref, wno_ref, wo_ref, n2_ref, wr_ref, br_ref,
                   x1_ref, u2_ref, gates_ref, *, tm, d):
    x = x_ref[...].reshape(tm, d)
    pooled = pooled_ref[...].reshape(tm, POOL_W).astype(BF16)
    y_pool = _dot(pooled, wlin_ref[...]) * pscale_ref[...]
    a = _dot(y_pool.astype(BF16), wpo_ref[...])
    b = _dot(y_ref[...].reshape(tm, QPAD), wno_ref[...])
    gm = gm_ref[...].reshape(tm, 2 * d)
    merged = gm[:, :d] * a + gm[:, d:] * b
    x1 = x + _rows2d(g1_ref) * _dot(merged.astype(BF16), wo_ref[...])
    x1_ref[...] = x1.reshape(x1_ref.shape)
    u2 = _rmsnorm(x1, n2_ref[...]) * (1.0 + _rows2d(scale_ref)) + _rows2d(shift_ref)
    u2b = u2.astype(BF16)
    u2_ref[...] = u2b.reshape(u2_ref.shape)
    logits_t = _dot_nt(wr_ref[...], u2b)
    gates_t = _route(logits_t[:N_EXPERTS], br_ref[...], tm)
    gates_t = jnp.concatenate([gates_t, jnp.zeros((LANES - N_EXPERTS, tm), F32)], axis=0)
    gates_ref[...] = gates_t.T.reshape(gates_ref.shape)


def _finish(x3, pooled, ynsa, gm, mods, wts, *, tm):
    g, r, d = x3.shape
    nt = r // tm
    g1, shift2, scale2 = mods
    per_row = g1.ndim == 2
    tok = lambda w: pl.BlockSpec((1, tm, w), lambda b, j: (b, j, 0))
    if per_row:
        mod_spec = lambda col: pl.BlockSpec((tm, d), lambda b, j, col=col: (b * nt + j, col))
    else:
        mod_spec = lambda col: pl.BlockSpec((1, 1, d), lambda b, j, col=col: (b, 0, col))
    const = lambda a: pl.BlockSpec(a.shape, lambda b, j: (0,) * a.ndim)
    return pl.pallas_call(
        functools.partial(_finish_kernel, tm=tm, d=d),
        out_shape=[jax.ShapeDtypeStruct((g, r, d), F32), jax.ShapeDtypeStruct((g, r, d), BF16),
                   jax.ShapeDtypeStruct((g, r, LANES), F32)],
        grid=(g, nt),
        in_specs=[tok(d), tok(POOL_W), tok(QPAD), tok(2 * d), mod_spec(2), mod_spec(3), mod_spec(4)]
        + [const(w) for w in wts],
        out_specs=[tok(d), tok(d), tok(LANES)],
        compiler_params=_cparams(("arbitrary", "arbitrary")),
        name="finish",
    )(x3, pooled, ynsa, gm, g1, shift2, scale2, *wts)


def _moe_kernel(u_ref, gates_ref, x1_ref, g2_ref, nf_ref, wg_ref, wu_ref, wd_ref, sg_ref, su_ref, sd_ref,
                y_ref, acc_ref, *, tm, d):
    e = pl.program_id(2)
    u = u_ref[...].reshape(tm, d)

    @pl.when(e == 0)
    def _():
        hs = _silu(_dot(u, sg_ref[...])) * _dot(u, su_ref[...])
        acc_ref[...] = _dot(hs.astype(BF16), sd_ref[...])

    h = _silu(_dot(u, wg_ref[...].astype(BF16))) * _dot(u, wu_ref[...].astype(BF16))
    gates = gates_ref[...].reshape(tm, LANES)
    lane = lax.broadcasted_iota(I32, (1, LANES), 1)
    gate = jnp.sum(jnp.where(lane == e, gates, 0.0), axis=1, keepdims=True)
    acc_ref[...] += _dot((h * gate).astype(BF16), wd_ref[...].astype(BF16))

    @pl.when(e == pl.num_programs(2) - 1)
    def _():
        x2 = x1_ref[...].reshape(tm, d) + _rows2d(g2_ref) * acc_ref[...]
        y_ref[...] = _rmsnorm(x2, nf_ref[...]).reshape(y_ref.shape)


def _moe(u2, gates, x1, g2, normf, w_gate, w_up, w_down, sg, su, sd, *, tm):
    g, r, d = x1.shape
    nt = r // tm
    ne, _, f = w_gate.shape
    per_row = g2.ndim == 2
    tok = lambda w: pl.BlockSpec((1, tm, w), lambda b, j, e: (b, j, 0))
    if per_row:
        g2_spec = pl.BlockSpec((tm, d), lambda b, j, e: (b * nt + j, 5))
    else:
        g2_spec = pl.BlockSpec((1, 1, d), lambda b, j, e: (b, 0, 5))
    const = lambda a: pl.BlockSpec(a.shape, lambda b, j, e: (0,) * a.ndim)
    return pl.pallas_call(
        functools.partial(_moe_kernel, tm=tm, d=d),
        out_shape=jax.ShapeDtypeStruct((g, r, d), F32),
        grid=(g, nt, ne),
        in_specs=[tok(d), tok(LANES), tok(d), g2_spec, const(normf),
                  pl.BlockSpec((None, d, f), lambda b, j, e: (e, 0, 0)),
                  pl.BlockSpec((None, d, f), lambda b, j, e: (e, 0, 0)),
                  pl.BlockSpec((None, f, d), lambda b, j, e: (e, 0, 0)),
                  const(sg), const(su), const(sd)],
        out_specs=tok(d),
        scratch_shapes=[pltpu.VMEM((tm, d), F32)],
        compiler_params=_cparams(("arbitrary", "arbitrary", "arbitrary")),
        name="moe",
    )(u2, gates, x1, g2, normf, w_gate, w_up, w_down, sg, su, sd)


def _kv_slot_mask():
    return (jnp.arange(N_HEADS)[:, None] // GROUP == jnp.arange(N_KV)[None, :]).astype(F32)


def _prep_w_in(w_in, d):
    q0 = POOL_W
    kv0 = q0 + N_HEADS * HEAD_DIM
    gn0 = kv0 + 6 * KVW
    gm0 = gn0 + 3 * N_HEADS
    wq = w_in[:, q0:kv0].reshape(d, N_HEADS, 1, HEAD_DIM) * (HEAD_DIM ** -0.5)
    wq = (wq * _kv_slot_mask()[None, :, :, None]).reshape(d, QPAD)
    wgn = jnp.pad(w_in[:, gn0:gm0], ((0, 0), (0, LANES - 3 * N_HEADS)))
    return jnp.concatenate([w_in[:, :q0], wq, w_in[:, kv0:gn0], wgn, w_in[:, gm0:]], axis=1).astype(BF16)


def _prep_w_nsa_out(w, d):
    w = w.reshape(N_HEADS, 1, HEAD_DIM, d) * _kv_slot_mask()[:, :, None, None]
    return w.reshape(QPAD, d).astype(BF16)


def _block_diag(w_lin):
    g, c, _ = w_lin.shape
    eye = jnp.eye(g, dtype=F32)
    return (w_lin[:, :, None, :] * eye[:, None, :, None]).reshape(g * c, g * c).astype(BF16)


def kernel(x_prompt, x_sample, cache_kc, cache_vc, cache_ks, cache_vs, state_kw, state_vw, state_pool,
           page_table, c_prompt, c_sample, norm1_g, norm2_g, normf_g, w_ada, b_ada, w_in, w_pool_lin,
           pool_scale, w_cmp_k, w_cmp_v, w_pool_out, w_nsa_out, w_o, w_router, b_router, w_gate, w_up,
           w_down, ws_gate, ws_up, ws_down):
    depth = w_in.shape[0]
    assert depth == 1, "single-layer stack"
    bsz, seq, d = x_prompt.shape
    n_seq, ns, _ = x_sample.shape
    wbuf = state_kw.shape[2]
    lyr = 0

    c_all = jnp.concatenate([c_prompt, c_sample], axis=0)
    rows = c_all.shape[0]
    rows_p = -(-rows // 8) * 8
    mod = _adaln(jnp.pad(c_all, ((0, rows_p - rows), (0, 0))), w_ada[lyr], b_ada[lyr])
    mod_p = mod[:bsz].reshape(bsz, 1, 6 * d)
    mod_s = jnp.repeat(mod[bsz:bsz + n_seq], ns, axis=0)

    w2 = _prep_w_in(w_in[lyr], d)
    g1n = norm1_g[lyr].reshape(1, d)
    wk = w_cmp_k[lyr].reshape(CMP_BLOCK, KVW)
    wv = w_cmp_v[lyr].reshape(CMP_BLOCK, KVW)
    fin_w = (_block_diag(w_pool_lin[lyr]), pool_scale[lyr].reshape(1, POOL_W), w_pool_out[lyr].astype(BF16),
             _prep_w_nsa_out(w_nsa_out[lyr], d), w_o[lyr].astype(BF16), norm2_g[lyr].reshape(1, d),
             jnp.pad(w_router[lyr].T, ((0, LANES - N_EXPERTS), (0, 0))).astype(BF16),
             b_router[lyr].reshape(N_EXPERTS, 1))
    moe_w = (w_gate[lyr], w_up[lyr], w_down[lyr], ws_gate[lyr].astype(BF16), ws_up[lyr].astype(BF16),
             ws_down[lyr].astype(BF16))
    nf = normf_g.reshape(1, d)

    tm_p = 512
    (vp, kc, vc, ks, vs, kw, vw, gm, ksb, kwb, vst, vwt, qt, gst, pooled) = _in_proj(
        x_prompt, mod_p, mod_p, g1n, w2, tm=tm_p, prompt=True)
    kcmp, vcmpt = _compress(kc, vc, wk, wv)
    ynsa = _nsa_prompt(qt, gst, kcmp, vcmpt, ksb, vst, kwb, vwt)
    x1, u2, gates = _finish(x_prompt, pooled, ynsa, gm, (mod_p, mod_p, mod_p), fin_w, tm=tm_p)
    y_prompt = _moe(u2, gates, x1, mod_p, nf, *moe_w, tm=1024)

    n_tok = n_seq * ns
    xs3 = x_sample.reshape(1, n_tok, d)
    tm_s = 128
    (vp_s, kc_s, vc_s, ks_s, vs_s, kw_s, vw_s, gm_s, q_s, gs_s) = _in_proj(
        xs3, mod_s, mod_s, g1n, w2, tm=tm_s, prompt=False)
    two = lambda a: a.reshape(n_tok, a.shape[-1])
    q_rows = q_s.reshape(n_tok * N_HEADS, LANES)
    gate_rows = two(gs_s)[:, :3 * N_HEADS].reshape(n_tok, 3, N_HEADS).transpose(0, 2, 1)
    gate_rows = jnp.pad(gate_rows.reshape(n_tok * N_HEADS, 3), ((0, 0), (0, LANES - 3)))
    n_pool = cache_kc.shape[1]
    page = cache_kc.shape[2]
    caches = [c[lyr].reshape(n_pool, page, KVW) for c in (cache_kc, cache_vc, cache_ks, cache_vs)]
    o_rows, pooled_s = _nsa_sample(
        page_table, q_rows, gate_rows, [two(a) for a in (kc_s, vc_s, ks_s, vs_s, kw_s, vw_s)], two(vp_s),
        state_kw[lyr].reshape(n_seq, wbuf, KVW), state_vw[lyr].reshape(n_seq, wbuf, KVW), state_pool[lyr],
        wk, wv, caches)
    ynsa_s = o_rows.reshape(1, n_tok, QPAD)
    x1_s, u2_s, gates_s = _finish(xs3, pooled_s.reshape(1, n_tok, POOL_W), ynsa_s, gm_s,
                                  (mod_s, mod_s, mod_s), fin_w, tm=tm_s)
    y_sample = _moe(u2_s, gates_s, x1_s, mod_s, nf, *moe_w, tm=n_tok).reshape(n_seq, ns, d)

    kvp = lambda a: a.reshape(1, bsz, seq, N_KV, HEAD_DIM)
    tailp = lambda a: jnp.pad(a, ((0, 0), (wbuf, 0), (0, 0)))[:, -wbuf:].reshape(1, bsz, wbuf, N_KV, HEAD_DIM)
    kvs = lambda a: a.reshape(1, n_seq, ns, N_KV, HEAD_DIM)
    wins = lambda st, new: jnp.concatenate(
        [st[lyr], new.reshape(n_seq, ns, N_KV, HEAD_DIM)], axis=1)[None, :, -wbuf:]
    pool_p = vp[:, -POOL_BUF:][None]
    pool_s = jnp.concatenate([state_pool[lyr], vp_s.reshape(n_seq, ns, POOL_W)], axis=1)[None, :, -POOL_BUF:]
    return (y_prompt, y_sample, kvp(kc), kvp(vc), kvp(ks), kvp(vs), tailp(kw), tailp(vw), pool_p,
            kvs(kc_s), kvs(vc_s), kvs(ks_s), kvs(vs_s), wins(state_kw, kw_s), wins(state_vw, vw_s), pool_s)
```

```python
import functools

import jax
import jax.numpy as jnp
from jax import lax
from jax.experimental import pallas as pl
from jax.experimental.pallas import tpu as pltpu

F32 = jnp.float32
BF16 = jnp.bfloat16
I32 = jnp.int32

POOL_WINDOWS = (2, 4, 8, 16)
POOL_GW = 64
POOL_W = 256
POOL_BUF = 15
N_HEADS = 8
HEAD_DIM = 64
N_KV = 2
GROUP = N_HEADS // N_KV
CMP_STRIDE = 16
CMP_BLOCK = 32
SEL_BLOCK = 64
TOP_BLOCKS = 16
WINDOW = 512
Q_BLOCK = 128
FORCE_SCORE = 1e4
N_EXPERTS = 64
N_EGROUPS = 8
EXPERTS_PER_GROUP = N_EXPERTS // N_EGROUPS
TOPK_GROUPS = 4
TOP_K = 8
ROUTED_SCALE = 2.5
EPS = 1e-6
NEG = -1e30
SLOPES = tuple(2.0 ** (-8.0 * (h + 1.0) / N_HEADS) for h in range(N_HEADS))

LANES = 128
QPAD = N_HEADS * LANES
KVW = N_KV * HEAD_DIM
VMEM_LIMIT = 56 * 1024 * 1024


def _cparams(sem):
    return pltpu.CompilerParams(dimension_semantics=sem, vmem_limit_bytes=VMEM_LIMIT)


def _dot(a, b):
    return jnp.dot(a, b, preferred_element_type=F32)


def _dot_nt(a, b):
    return lax.dot_general(a, b, (((1,), (1,)), ((), ())), preferred_element_type=F32)


def _dot_exact(a, b):
    return jnp.dot(a, b, preferred_element_type=F32, precision=lax.Precision.HIGHEST)


def _rows2d(ref):
    v = ref[...]
    return v.reshape(v.shape[-2], v.shape[-1])


def _rmsnorm(x, g):
    return x * lax.rsqrt(jnp.mean(x * x, axis=-1, keepdims=True) + EPS) * g


def _silu(x):
    return x * jax.nn.sigmoid(x)


def _adaln_kernel(c_ref, w_ref, b_ref, o_ref):
    s = _silu(c_ref[...]).astype(BF16)
    o_ref[...] = _dot(s, w_ref[...].astype(BF16)) + b_ref[...]


def _adaln(c, w_ada, b_ada):
    rows, d = c.shape
    n = w_ada.shape[1]
    tn = 512
    return pl.pallas_call(
        _adaln_kernel,
        out_shape=jax.ShapeDtypeStruct((rows, n), F32),
        grid=(n // tn,),
        in_specs=[pl.BlockSpec((rows, d), lambda j: (0, 0)),
                  pl.BlockSpec((d, tn), lambda j: (0, j)),
                  pl.BlockSpec((1, tn), lambda j: (0, j))],
        out_specs=pl.BlockSpec((rows, tn), lambda j: (0, j)),
        compiler_params=_cparams(("arbitrary",)),
        name="adaln",
    )(c, w_ada, b_ada.reshape(1, n))


_C_VP = 0
_C_Q = _C_VP + POOL_W
_C_KV = _C_Q + QPAD
_C_GN = _C_KV + 6 * KVW
_C_GM = _C_GN + LANES


def _pool_window_sums(ext, tm):
    s2 = ext + pltpu.roll(ext, 1, 0)
    s4 = s2 + pltpu.roll(s2, 2, 0)
    s8 = s4 + pltpu.roll(s4, 4, 0)
    s16 = s8 + pltpu.roll(s8, 8, 0)
    grp = lax.broadcasted_iota(I32, (1, POOL_W), 1) // POOL_GW
    pick = jnp.where(grp == 0, s2, jnp.where(grp == 1, s4, jnp.where(grp == 2, s8, s16)))
    return pick[16:16 + tm]


def _in_proj_kernel(x_ref, shift_ref, scale_ref, g_ref, w_ref,
                    vp_ref, kc_ref, vc_ref, ks_ref, vs_ref, kw_ref, vw_ref, gm_ref, *rest, tm, d, prompt):
    x = x_ref[...].reshape(tm, d)
    u = _rmsnorm(x, g_ref[...]) * (1.0 + _rows2d(scale_ref)) + _rows2d(shift_ref)
    ub = u.astype(BF16)

    def proj(c0, n):
        return _dot(ub, w_ref[:, c0:c0 + n])

    vp = proj(_C_VP, POOL_W)
    vp_ref[...] = vp.reshape(vp_ref.shape)
    kv = []
    for n, o32 in enumerate((kc_ref, vc_ref, ks_ref, vs_ref, kw_ref, vw_ref)):
        v = proj(_C_KV + n * KVW, KVW)
        o32[...] = v.reshape(o32.shape)
        kv.append(v)
    gm_ref[...] = jax.nn.sigmoid(proj(_C_GM, 2 * d)).reshape(gm_ref.shape)
    gs = jax.nn.sigmoid(proj(_C_GN, LANES))

    if not prompt:
        q_ref, gs_ref = rest
        q_ref[...] = proj(_C_Q, QPAD).astype(BF16).reshape(q_ref.shape)
        gs_ref[...] = gs.reshape(gs_ref.shape)
    else:
        ksb_ref, kwb_ref, vst_ref, vwt_ref, qt_ref, gst_ref, pooled_ref, halo_ref = rest
        ksb_ref[...] = kv[2].astype(BF16).reshape(ksb_ref.shape)
        kwb_ref[...] = kv[4].astype(BF16).reshape(kwb_ref.shape)
        vst_ref[...] = kv[3].T.astype(BF16).reshape(vst_ref.shape)
        vwt_ref[...] = kv[5].T.astype(BF16).reshape(vwt_ref.shape)
        gst_ref[...] = gs.T.reshape(gst_ref.shape)
        for h in range(N_HEADS):
            qt_ref[0, h] = proj(_C_Q + h * LANES, LANES).T.astype(BF16)
        j = pl.program_id(1)

        @pl.when(j == 0)
        def _():
            halo_ref[...] = jnp.zeros_like(halo_ref)

        ext = jnp.concatenate([halo_ref[...], vp], axis=0)
        sums = _pool_window_sums(ext, tm)
        pos = j * tm + lax.broadcasted_iota(I32, (tm, 1), 0)
        wcol = 2 << (lax.broadcasted_iota(I32, (1, POOL_W), 1) // POOL_GW)
        cnt = jnp.minimum(pos + 1, wcol).astype(F32)
        pooled_ref[...] = (sums / cnt - vp).astype(BF16).reshape(pooled_ref.shape)
        halo_ref[...] = vp[tm - 16:tm]


def _in_proj(x3, shift, scale, g1, w2, *, tm, prompt):
    g, r, d = x3.shape
    nt = r // tm
    per_row = shift.ndim == 2

    def tok(width, dtype):
        return (jax.ShapeDtypeStruct((g, r, width), dtype),
                pl.BlockSpec((1, tm, width), lambda b, j: (b, j, 0)))

    def tok_t(rows, dtype):
        return (jax.ShapeDtypeStruct((g, rows, r), dtype),
                pl.BlockSpec((1, rows, tm), lambda b, j: (b, 0, j)))

    outs = [tok(POOL_W, F32)] + [tok(KVW, F32)] * 6 + [tok(2 * d, F32)]
    scratch = []
    if prompt:
        outs += [tok(KVW, BF16), tok(KVW, BF16), tok_t(KVW, BF16), tok_t(KVW, BF16)]
        outs.append((jax.ShapeDtypeStruct((g, N_HEADS, LANES, r), BF16),
                     pl.BlockSpec((1, N_HEADS, LANES, tm), lambda b, j: (b, 0, 0, j))))
        outs += [tok_t(LANES, F32), tok(POOL_W, BF16)]
        scratch.append(pltpu.VMEM((16, POOL_W), F32))
    else:
        outs += [tok(QPAD, BF16), tok(LANES, F32)]
    if per_row:
        mod_spec = lambda col: pl.BlockSpec((tm, d), lambda b, j, col=col: (b * nt + j, col))
    else:
        mod_spec = lambda col: pl.BlockSpec((1, 1, d), lambda b, j, col=col: (b, 0, col))
    kern = functools.partial(_in_proj_kernel, tm=tm, d=d, prompt=prompt)
    return pl.pallas_call(
        kern,
        out_shape=[o[0] for o in outs],
        grid=(g, nt),
        in_specs=[pl.BlockSpec((1, tm, d), lambda b, j: (b, j, 0)),
                  mod_spec(0), mod_spec(1),
                  pl.BlockSpec((1, d), lambda b, j: (0, 0)),
                  pl.BlockSpec(w2.shape, lambda b, j: (0, 0))],
        out_specs=[o[1] for o in outs],
        scratch_shapes=scratch,
        compiler_params=_cparams(("arbitrary", "arbitrary")),
        name="in_proj_prompt" if prompt else "in_proj_sample",
    )(x3, shift, scale, g1, w2)


def _compress_kernel(kc_ref, vc_ref, wk_ref, wv_ref, okc_ref, ovc_ref, sh_ref, *, nc):
    last = lax.broadcasted_iota(I32, (nc, 1), 0) == nc - 1
    for src, w_ref, dst in ((kc_ref, wk_ref, okc_ref), (vc_ref, wv_ref, ovc_ref)):
        head = jnp.zeros((nc, KVW), F32)
        tail = jnp.zeros((nc, KVW), F32)
        for r in range(CMP_STRIDE):
            rows = src[pl.ds(r, nc, stride=CMP_STRIDE), :]
            head = head + rows * w_ref[r:r + 1, :]
            tail = tail + rows * w_ref[CMP_STRIDE + r:CMP_STRIDE + r + 1, :]
        sh_ref[0:nc, :] = tail
        sh_ref[nc:nc + 8, :] = jnp.zeros((8, KVW), F32)
        out = jnp.where(last, 0.0, head + sh_ref[1:nc + 1, :])
        dst[...] = (out if dst is okc_ref else out.T).astype(BF16)


def _compress(kc, vc, wk, wv):
    b, s, _ = kc.shape
    nc = s // CMP_STRIDE
    big = pl.BlockSpec((None, s, KVW), lambda i: (i, 0, 0))
    wsp = pl.BlockSpec((CMP_BLOCK, KVW), lambda i: (0, 0))
    return pl.pallas_call(
        functools.partial(_compress_kernel, nc=nc),
        out_shape=[jax.ShapeDtypeStruct((b, nc, KVW), BF16), jax.ShapeDtypeStruct((b, KVW, nc), BF16)],
        grid=(b,),
        in_specs=[big, big, wsp, wsp],
        out_specs=[pl.BlockSpec((None, nc, KVW), lambda i: (i, 0, 0)),
                   pl.BlockSpec((None, KVW, nc), lambda i: (i, 0, 0))],
        scratch_shapes=[pltpu.VMEM((nc + 8, KVW), F32)],
        compiler_params=_cparams(("arbitrary",)),
        name="compress",
    )(kc, vc, wk, wv)


def _topk_mask(vals, blk_f, n_top, axis=1):
    sel = jnp.zeros(vals.shape, F32)
    big = float(vals.shape[axis])
    for _ in range(n_top):
        mx = jnp.max(vals, axis=axis, keepdims=True)
        first = jnp.min(jnp.where(vals == mx, blk_f, big), axis=axis, keepdims=True)
        hit = blk_f == first
        sel = jnp.where(hit, 1.0, sel)
        vals = jnp.where(hit, -jnp.inf, vals)
    return sel


def _topk_mask_by_rank(vals, blk, n_valid, n_top):
    rank = jnp.zeros(vals.shape, F32)
    for j in range(n_valid):
        vj = vals[:, j:j + 1]
        beats = (vj > vals) | ((vj == vals) & (blk > j))
        rank = rank + jnp.where(beats, 1.0, 0.0)
    return jnp.where(rank < float(n_top), 1.0, 0.0)


def _pos_features(pos):
    hi = (pos // SEL_BLOCK).astype(F32)[:, None]
    lo = (pos % SEL_BLOCK).astype(F32)[:, None]
    return jnp.concatenate([hi, lo, jnp.zeros((pos.shape[0], LANES - 2), F32)], axis=1).astype(BF16)


def _importance_matrix(nc, nsel):
    j = jnp.arange(nc)[:, None]
    s = jnp.arange(nsel)[None, :]
    r = SEL_BLOCK // CMP_STRIDE
    a = (j >= r * s) & (j <= r * s + r - 1)
    b = (j + 1 >= r * s) & (j + 1 <= r * s + r - 1)
    return a.astype(F32) + b.astype(F32)


def _nsa_prompt_kernel(qt_ref, gst_ref, kc_ref, vct_ref, ks_ref, vst_ref, kw_ref, vwt_ref, imat_ref,
                       cfeat_ref, wfeat_ref, qfeat_ref, slope_ref,
                       y_ref, qk_scr, m_scr, l_scr, acc_scr, o_scr, sel_scr, flag_scr, *, seq, tk, wl):
    i = pl.program_id(1)
    q0 = i * Q_BLOCK
    nq = Q_BLOCK
    gq = GROUP * nq
    nc = kc_ref.shape[0]
    nsel = seq // SEL_BLOCK
    n_top = min(TOP_BLOCKS, nsel)
    blk_per_tile = tk // SEL_BLOCK
    qpos = q0 + lax.broadcasted_iota(I32, (1, nq), 1)
    gst = gst_ref[...]

    cend = lax.broadcasted_iota(I32, (nc, nq), 0) * CMP_STRIDE + (CMP_BLOCK - 1)
    mask_c = qpos >= cend
    kc = jnp.concatenate([kc_ref[...], cfeat_ref[...]], axis=1)
    vct = vct_ref[...]
    blk = lax.broadcasted_iota(I32, (nsel, nq), 0)
    blk_f = blk.astype(F32)
    cur = qpos // SEL_BLOCK
    forced = (blk == 0) | (blk == cur) | (blk == cur - 1)
    visible = blk * SEL_BLOCK <= qpos
    ws = pl.multiple_of(jnp.maximum(q0 - WINDOW, 0), Q_BLOCK)
    wpos = ws + lax.broadcasted_iota(I32, (wl, nq), 0)
    valid_w = lax.bitcast_convert_type(qpos - wpos, jnp.uint32) < WINDOW
    n_tiles = (q0 + nq + tk - 1) // tk
    half_rows = lax.broadcasted_iota(I32, (KVW, nq), 0) // HEAD_DIM
    tile_pos = lax.broadcasted_iota(I32, (SEL_BLOCK, nq), 0)

    def lanes4(x):
        return jnp.concatenate([x] * GROUP, axis=1)

    def gate_row(branch, k):
        r0 = branch * N_HEADS + k * GROUP
        return jnp.concatenate([gst[r0 + g:r0 + g + 1] for g in range(GROUP)], axis=1)

    mask_c4 = lanes4(mask_c)
    valid_w4 = lanes4(valid_w)
    kwt = jnp.concatenate([kw_ref[pl.ds(ws, wl), :], wfeat_ref[...]], axis=1)
    vwtt = vwt_ref[:, pl.ds(ws, wl)]

    for k in range(N_KV):
        for g in range(GROUP):
            qk_scr[k, 0:LANES, g * nq:(g + 1) * nq] = qt_ref[k * GROUP + g]
        qk_scr[k, LANES:2 * LANES, :] = qfeat_ref[k]
        qk = qk_scr[k]

        s = jnp.where(mask_c4, _dot(kc, qk), NEG)
        e = jnp.where(mask_c4, jnp.exp(s - jnp.max(s, axis=0, keepdims=True)), 0.0)
        l = jnp.sum(e, axis=0, keepdims=True)
        p = e * jnp.where(l > 0.0, 1.0 / l, 0.0)
        o_c = _dot(vct, p.astype(BF16))
        psum = p[:, 0:nq]
        for g in range(1, GROUP):
            psum = psum + p[:, g * nq:(g + 1) * nq]

        imp = _dot_exact(imat_ref[...], psum)
        vals = jnp.where(visible, jnp.where(forced, FORCE_SCORE, imp), NEG)
        sel = jnp.where(visible, _topk_mask(vals, blk_f, n_top, axis=0), 0.0)
        sel_scr[k] = jnp.where(sel > 0.5, 0.0, NEG)
        blk_any = jnp.max(sel, axis=1, keepdims=True)
        for t in range(seq // tk):
            hit = (jnp.max(blk_any[t * blk_per_tile:(t + 1) * blk_per_tile, :]) > 0.5).astype(I32)
            flag_scr[t] = hit if k == 0 else flag_scr[t] | hit

        s = jnp.where(valid_w4, _dot(kwt, qk), NEG)
        e = jnp.exp(s - jnp.max(s, axis=0, keepdims=True))
        p = e / jnp.sum(e, axis=0, keepdims=True)
        o_w = _dot(vwtt, p.astype(BF16))
        o_scr[k] = gate_row(0, k) * o_c + gate_row(2, k) * o_w

    m_scr[...] = jnp.full(m_scr.shape, NEG, F32)
    l_scr[...] = jnp.zeros(l_scr.shape, F32)
    acc_scr[...] = jnp.zeros(acc_scr.shape, F32)

    def sel_tile(t):
        k0 = pl.multiple_of(t * tk, tk)
        kt = jnp.concatenate([ks_ref[pl.ds(k0, tk), :], wfeat_ref[0:tk, :]], axis=1)
        vtt = vst_ref[:, pl.ds(k0, tk)]
        causal = [qpos >= k0 + j * SEL_BLOCK + tile_pos for j in range(blk_per_tile)]
        base = (k0 - q0).astype(F32)
        scores = [_dot(kt, qk_scr[k]) for k in range(N_KV)]
        for k in range(N_KV):
            neg = []
            for j in range(blk_per_tile):
                row = sel_scr[k, pl.ds(t * blk_per_tile + j, 1), :]
                neg.append(jnp.where(causal[j], jnp.broadcast_to(row, (SEL_BLOCK, nq)), NEG))
            neg = lanes4(jnp.concatenate(neg, axis=0))
            off = slope_ref[k] * base
            s = scores[k] + neg
            m_old = m_scr[k]
            m_new = jnp.maximum(m_old, jnp.max(s, axis=0, keepdims=True) + off)
            alpha = jnp.exp(m_old - m_new)
            p = jnp.exp(s - (m_new - off))
            l_scr[k] = alpha * l_scr[k] + jnp.sum(p, axis=0, keepdims=True)
            m_scr[k] = m_new
            acc_scr[k] = acc_scr[k] * alpha + _dot(vtt, p.astype(BF16))

    def sel_body(t, carry):
        pl.when(flag_scr[t] > 0)(functools.partial(sel_tile, t))
        return carry

    lax.fori_loop(0, n_tiles, sel_body, 0)

    for k in range(N_KV):
        o = o_scr[k] + gate_row(1, k) * (acc_scr[k] / l_scr[k])
        for g in range(GROUP):
            h = k * GROUP + g
            oh = jnp.where(half_rows == k, o[:, g * nq:(g + 1) * nq], 0.0)
            y_ref[:, h * LANES:(h + 1) * LANES] = oh.T.astype(BF16)


def _nsa_prompt(qt, gst, kcmp, vcmpt, ksb, vst, kwb, vwt):
    b, _, _, s = qt.shape
    nq = Q_BLOCK
    gq = GROUP * nq
    nc = kcmp.shape[1]
    nsel = s // SEL_BLOCK
    tk = 256
    wl = WINDOW + Q_BLOCK
    assert s % tk == 0 and s >= wl
    assert s // SEL_BLOCK <= 2 * LANES, "position // 64 must stay exact in bf16"
    imat = _importance_matrix(nc, nsel).T
    cfeat = _pos_features(jnp.arange(nc) * CMP_STRIDE + (CMP_BLOCK - 1))
    wfeat = _pos_features(jnp.arange(wl))
    slope_rows = jnp.repeat(jnp.asarray(SLOPES, F32).reshape(N_KV, 1, GROUP), nq, axis=2)
    qfeat = jnp.concatenate([slope_rows * SEL_BLOCK, slope_rows, jnp.zeros((N_KV, LANES - 2, gq), F32)],
                            axis=1).astype(BF16)
    rows = lambda r: pl.BlockSpec((None, r, KVW), lambda bi, i: (bi, 0, 0))
    cols = lambda c: pl.BlockSpec((None, KVW, c), lambda bi, i: (bi, 0, 0))
    const = lambda a: pl.BlockSpec(a.shape, lambda bi, i: (0,) * a.ndim)
    return pl.pallas_call(
        functools.partial(_nsa_prompt_kernel, seq=s, tk=tk, wl=wl),
        out_shape=jax.ShapeDtypeStruct((b, s, QPAD), BF16),
        grid=(b, s // nq),
        in_specs=[pl.BlockSpec((None, N_HEADS, LANES, nq), lambda bi, i: (bi, 0, 0, i)),
                  pl.BlockSpec((None, LANES, nq), lambda bi, i: (bi, 0, i)),
                  rows(nc), cols(nc), rows(s), cols(s), rows(s), cols(s),
                  const(imat), const(cfeat), const(wfeat), const(qfeat), const(slope_rows)],
        out_specs=pl.BlockSpec((None, nq, QPAD), lambda bi, i: (bi, i, 0)),
        scratch_shapes=[pltpu.VMEM((N_KV, 2 * LANES, gq), BF16),
                        pltpu.VMEM((N_KV, 1, gq), F32),
                        pltpu.VMEM((N_KV, 1, gq), F32),
                        pltpu.VMEM((N_KV, KVW, gq), F32),
                        pltpu.VMEM((N_KV, KVW, gq), F32),
                        pltpu.VMEM((N_KV, nsel, nq), F32),
                        pltpu.SMEM((s // tk,), I32)],
        compiler_params=_cparams(("arbitrary", "arbitrary")),
        name="nsa_prompt",
    )(qt, gst, kcmp, vcmpt, ksb, vst, kwb, vwt, imat, cfeat, wfeat, qfeat, slope_rows)


def _nsa_sample_kernel(pt_ref, q_ref, gate_ref, kcn_ref, vcn_ref, ksn_ref, vsn_ref, kwn_ref, vwn_ref,
                       vpn_ref, skw_ref, svw_ref, spool_ref, wk_ref, wv_ref, imat_ref, emat_ref,
                       ckc_ref, cvc_ref, cks_ref, cvs_ref,
                       o_ref, pooled_ref, buf, buft, win_scr, tail_scr, vext_scr, sem,
                       *, sb, ns, past, n_pages, page, n_seq, ncv, ncp, nks, wls, nselp, n_sel):
    step = pl.program_id(0)
    nrow = ns * N_HEADS

    def copies(n, slot):
        out = []
        for p in range(n_pages):
            pg = pt_ref[n * n_pages + p]
            for c, cref in enumerate((ckc_ref, cvc_ref)):
                out.append(pltpu.make_async_copy(cref.at[pg], buf.at[slot, c, pl.ds(p * page, page), :],
                                                 sem.at[slot]))
            for c, cref in enumerate((cks_ref, cvs_ref)):
                out.append(pltpu.make_async_copy(cref.at[pg], buft.at[slot, c, :, pl.ds(p * page, page)],
                                                 sem.at[slot]))
        return out

    @pl.when(step == 0)
    def _():
        buf[:, :, past:, :] = jnp.zeros((2, 2, buf.shape[2] - past, KVW), F32)
        tail_scr[...] = jnp.zeros_like(tail_scr)
        vext_scr[...] = jnp.zeros_like(vext_scr)
        for cp in copies(0, 0):
            cp.start()

    def new_rows_t(ref, r4):
        tail_scr[0:ns, :] = ref[pl.ds(r4, ns), :]
        return tail_scr[...].T

    row = lax.broadcasted_iota(I32, (nrow, 1), 0)
    hrow = row % N_HEADS
    qpos = past + row // N_HEADS
    slope = jnp.exp2(-8.0 * (hrow.astype(F32) + 1.0) / N_HEADS)
    kvrow = hrow // GROUP
    lane = lax.broadcasted_iota(I32, (1, LANES), 1)
    half = (lane // HEAD_DIM) == kvrow
    grow = (row // N_HEADS) * N_KV + kvrow
    row8 = lax.broadcasted_iota(I32, (ns * N_KV, 1), 0)
    qpos8 = past + lax.broadcasted_iota(I32, (ns * N_KV, 1), 0) // N_KV
    blk = lax.broadcasted_iota(I32, (1, nselp), 1)
    blk_f = blk.astype(F32)
    cur = qpos8 // SEL_BLOCK
    forced = (blk == 0) | (blk == cur) | (blk == cur - 1)
    visible = (blk * SEL_BLOCK <= qpos8)
    inrange = blk < n_sel
    cend = lax.broadcasted_iota(I32, (1, ncp), 1) * CMP_STRIDE + (CMP_BLOCK - 1)
    mask_c = qpos >= cend
    bias_c = slope * (cend - qpos).astype(F32)
    kpos = lax.broadcasted_iota(I32, (1, nks), 1)
    causal_s = qpos >= kpos
    bias_s = slope * (kpos - qpos).astype(F32)
    wbuf = wls[0]
    wpos = past - wbuf + lax.broadcasted_iota(I32, (1, wls[1]), 1)
    dw = qpos - wpos
    valid_w = lax.bitcast_convert_type(dw, jnp.uint32) < WINDOW
    bias_w = slope * (wpos - qpos).astype(F32)
    prow = lax.broadcasted_iota(I32, (vext_scr.shape[0], 1), 0)
    wcol = 2 << (lax.broadcasted_iota(I32, (1, POOL_W), 1) // POOL_GW)
    n_top = min(TOP_BLOCKS, n_sel)

    def softmax_rows(s, mask):
        s = jnp.where(mask, s, NEG)
        mx = jnp.max(s, axis=1, keepdims=True)
        e = jnp.where(mask, jnp.exp(s - mx), 0.0)
        l = jnp.sum(e, axis=1, keepdims=True)
        return e * jnp.where(l > 0.0, 1.0 / l, 0.0)

    def seq_body(r):
        n = step * sb + r
        slot = r % 2

        @pl.when(n + 1 < n_seq)
        def _():
            for cp in copies(n + 1, 1 - slot):
                cp.start()

        for cp in copies(n, slot):
            cp.wait()

        r4 = r * ns
        for c, new_ref in enumerate((kcn_ref, vcn_ref)):
            buf[slot, c, past:past + ns, :] = new_ref[pl.ds(r4, ns), :]
        for c, new_ref in enumerate((ksn_ref, vsn_ref)):
            buft[slot, c, :, past:past + LANES] = new_rows_t(new_ref, r4)

        qall = q_ref[pl.ds(r * nrow, nrow), :]
        gates = gate_ref[pl.ds(r * nrow, nrow), :]

        cmp = []
        for c, w_ref in ((0, wk_ref), (1, wv_ref)):
            span = CMP_STRIDE * ncv
            lo = buf[slot, c, 0:span, :].reshape(ncv, CMP_STRIDE, KVW) * w_ref[0:CMP_STRIDE, :][None]
            hi = (buf[slot, c, CMP_STRIDE:CMP_STRIDE + span, :].reshape(ncv, CMP_STRIDE, KVW)
                  * w_ref[CMP_STRIDE:CMP_BLOCK, :][None])
            acc = jnp.sum(lo + hi, axis=1)
            cmp.append(jnp.concatenate([acc, jnp.zeros((ncp - ncv, KVW), F32)], axis=0).astype(BF16))
        p_c = softmax_rows(_dot_nt(qall, cmp[0]) + bias_c, mask_c)
        o_c = _dot(p_c.astype(BF16), cmp[1])

        psum = jnp.zeros((ns * N_KV, ncp), F32)
        for i in range(ns * N_KV):
            r0 = (i // N_KV) * N_HEADS + (i % N_KV) * GROUP
            psum = jnp.where(row8 == i, jnp.sum(p_c[r0:r0 + GROUP], axis=0, keepdims=True), psum)
        imp = _dot_exact(psum, imat_ref[...])
        vals = jnp.where(inrange, jnp.where(visible, jnp.where(forced, FORCE_SCORE, imp), NEG), -jnp.inf)
        sel8 = _topk_mask_by_rank(vals, blk, n_sel, n_top)
        sel_rows = jnp.zeros((nrow, nselp), F32)
        for i in range(ns * N_KV):
            sel_rows = jnp.where(grow == i, sel8[i:i + 1], sel_rows)
        chosen = _dot(sel_rows.astype(BF16), emat_ref[...])

        kst = buft[slot, 0].astype(BF16)
        vst = buft[slot, 1].astype(BF16)
        p_s = softmax_rows(_dot(qall, kst) + bias_s, causal_s & (chosen > 0.5))
        o_s = _dot_nt(p_s.astype(BF16), vst)

        outs_w = []
        for state_ref, new_ref in ((skw_ref, kwn_ref), (svw_ref, vwn_ref)):
            win_scr[:, 0:wbuf] = state_ref[r]
            win_scr[:, wbuf:wbuf + LANES] = new_rows_t(new_ref, r4)
            outs_w.append(win_scr[...].astype(BF16))
        p_w = softmax_rows(_dot(qall, outs_w[0]) + bias_w, valid_w)
        o_w = _dot_nt(p_w.astype(BF16), outs_w[1])

        o = gates[:, 0:1] * o_c + gates[:, 1:2] * o_s + gates[:, 2:3] * o_w
        o_ref[pl.ds(r * nrow, nrow), :] = jnp.where(half, o, 0.0).astype(BF16)

        vext_scr[0:POOL_BUF, :] = spool_ref[r]
        vext_scr[POOL_BUF:POOL_BUF + ns, :] = vpn_ref[pl.ds(r4, ns), :]
        ext = vext_scr[...]
        for t in range(ns):
            hi = POOL_BUF + t
            inwin = (prow <= hi) & (prow > hi - wcol)
            ssum = jnp.sum(jnp.where(inwin, ext, 0.0), axis=0, keepdims=True)
            cnt = jnp.minimum(past + t + 1, wcol).astype(F32)
            pooled_ref[pl.ds(r4 + t, 1), :] = ssum / cnt - ext[hi:hi + 1, :]

    for r in range(sb):
        seq_body(r)


def _nsa_sample(page_table, q_rows, gate_rows, new6, vp_new, state_kwt, state_vwt, state_pool, wk, wv, caches):
    n_seq, n_pages = page_table.shape
    page = caches[0].shape[1]
    past = n_pages * page
    ns = vp_new.shape[0] // n_seq
    wbuf = state_kwt.shape[2]
    sb = 2
    nrow = ns * N_HEADS
    assert ns <= SEL_BLOCK and page == LANES
    t_pad = -(-(past + ns) // SEL_BLOCK) * SEL_BLOCK
    n_cmp = t_pad // CMP_STRIDE - 1
    ncv = -(-n_cmp // 8) * 8
    ncp = -(-ncv // LANES) * LANES
    nks = past + LANES
    n_sel = t_pad // SEL_BLOCK
    nselp = LANES
    assert n_sel <= nselp
    wlp = wbuf + LANES
    buf_rows = -(-(CMP_STRIDE * ncv + CMP_STRIDE) // 8) * 8
    imat = _importance_matrix(ncp, nselp)
    emat = (jnp.arange(nselp)[:, None] == (jnp.arange(nks)[None, :] // SEL_BLOCK)).astype(BF16)

    seqblk = lambda rows, w: pl.BlockSpec((sb * rows, w), lambda i, pt: (i, 0))
    const = lambda a: pl.BlockSpec(a.shape, lambda i, pt: (0,) * a.ndim)
    kern = functools.partial(
        _nsa_sample_kernel, sb=sb, ns=ns, past=past, n_pages=n_pages, page=page, n_seq=n_seq,
        ncv=ncv, ncp=ncp, nks=nks, wls=(wbuf, wlp), nselp=nselp, n_sel=n_sel)
    grid_spec = pltpu.PrefetchScalarGridSpec(
        num_scalar_prefetch=1,
        grid=(n_seq // sb,),
        in_specs=[seqblk(nrow, LANES), seqblk(nrow, LANES)] + [seqblk(ns, KVW)] * 6 + [seqblk(ns, POOL_W)]
        + [pl.BlockSpec((sb, KVW, wbuf), lambda i, pt: (i, 0, 0))] * 2
        + [pl.BlockSpec((sb, POOL_BUF, POOL_W), lambda i, pt: (i, 0, 0))]
        + [const(wk), const(wv), const(imat), const(emat)]
        + [pl.BlockSpec(memory_space=pl.ANY)] * 4,
        out_specs=[seqblk(nrow, LANES), seqblk(ns, POOL_W)],
        scratch_shapes=[pltpu.VMEM((2, 2, buf_rows, KVW), F32),
                        pltpu.VMEM((2, 2, KVW, nks), F32),
                        pltpu.VMEM((KVW, wlp), F32),
                        pltpu.VMEM((LANES, KVW), F32),
                        pltpu.VMEM((24, POOL_W), F32),
                        pltpu.SemaphoreType.DMA((2,))],
    )
    return pl.pallas_call(
        kern,
        out_shape=[jax.ShapeDtypeStruct((n_seq * nrow, LANES), BF16),
                   jax.ShapeDtypeStruct((n_seq * ns, POOL_W), F32)],
        grid_spec=grid_spec,
        compiler_params=_cparams(("arbitrary",)),
        name="nsa_sample",
    )(page_table.reshape(-1), q_rows, gate_rows, *new6, vp_new, state_kwt, state_vwt, state_pool, wk, wv,
      imat, emat, *caches)


def _route(logits_t, bias_col, tm):
    sc = jax.nn.sigmoid(logits_t)
    biased = sc + bias_col
    epg = EXPERTS_PER_GROUP
    row8 = lax.broadcasted_iota(I32, (epg, tm), 0).astype(F32)
    ninf = -jnp.inf
    grp = jnp.zeros((N_EGROUPS, tm), F32)
    for g in range(N_EGROUPS):
        bg = biased[g * epg:(g + 1) * epg]
        m1 = jnp.max(bg, axis=0, keepdims=True)
        first = jnp.min(jnp.where(bg == m1, row8, float(epg)), axis=0, keepdims=True)
        m2 = jnp.max(jnp.where(row8 == first, ninf, bg), axis=0, keepdims=True)
        grp = jnp.where(row8 == float(g), m1 + m2, grp)
    keep = jnp.zeros((N_EGROUPS, tm), F32)
    vals = grp
    for _ in range(TOPK_GROUPS):
        mx = jnp.max(vals, axis=0, keepdims=True)
        first = jnp.min(jnp.where(vals == mx, row8, float(N_EGROUPS)), axis=0, keepdims=True)
        hit = row8 == first
        keep = jnp.where(hit, 1.0, keep)
        vals = jnp.where(hit, ninf, vals)
    masked = jnp.concatenate(
        [jnp.where(keep[g:g + 1] > 0.5, biased[g * epg:(g + 1) * epg], NEG) for g in range(N_EGROUPS)], axis=0)
    rowe = lax.broadcasted_iota(I32, (N_EXPERTS, tm), 0).astype(F32)
    chosen = jnp.zeros((N_EXPERTS, tm), F32)
    vals = masked
    for _ in range(TOP_K):
        mx = jnp.max(vals, axis=0, keepdims=True)
        first = jnp.min(jnp.where(vals == mx, rowe, float(N_EXPERTS)), axis=0, keepdims=True)
        hit = rowe == first
        chosen = jnp.where(hit, sc, chosen)
        vals = jnp.where(hit, ninf, vals)
    return ROUTED_SCALE * chosen / jnp.sum(chosen, axis=0, keepdims=True)


def _finish_kernel(x_ref, pooled_ref, y_ref, gm_ref, g1_ref, shift_ref, scale_ref,
                   wlin_ref, pscale_ref, wpo_ref, wno_ref, wo_ref, n2_ref, wr_ref, br_ref,
                   x1_ref, u2_ref, gates_ref, *, tm, d):
    x = x_ref[...].reshape(tm, d)
    pooled = pooled_ref[...].reshape(tm, POOL_W).astype(BF16)
    y_pool = _dot(pooled, wlin_ref[...]) * pscale_ref[...]
    a = _dot(y_pool.astype(BF16), wpo_ref[...])
    b = _dot(y_ref[...].reshape(tm, QPAD), wno_ref[...])
    gm = gm_ref[...].reshape(tm, 2 * d)
    merged = gm[:, :d] * a + gm[:, d:] * b
    x1 = x + _rows2d(g1_ref) * _dot(merged.astype(BF16), wo_ref[...])
    x1_ref[...] = x1.reshape(x1_ref.shape)
    u2 = _rmsnorm(x1, n2_ref[...]) * (1.0 + _rows2d(scale_ref)) + _rows2d(shift_ref)
    u2b = u2.astype(BF16)
    u2_ref[...] = u2b.reshape(u2_ref.shape)
    logits_t = _dot_nt(wr_ref[...], u2b)
    gates_t = _route(logits_t[:N_EXPERTS], br_ref[...], tm)
    gates_t = jnp.concatenate([gates_t, jnp.zeros((LANES - N_EXPERTS, tm), F32)], axis=0)
    gates_ref[...] = gates_t.T.reshape(gates_ref.shape)


def _finish(x3, pooled, ynsa, gm, mods, wts, *, tm):
    g, r, d = x3.shape
    nt = r // tm
    g1, shift2, scale2 = mods
    per_row = g1.ndim == 2
    tok = lambda w: pl.BlockSpec((1, tm, w), lambda b, j: (b, j, 0))
    if per_row:
        mod_spec = lambda col: pl.BlockSpec((tm, d), lambda b, j, col=col: (b * nt + j, col))
    else:
        mod_spec = lambda col: pl.BlockSpec((1, 1, d), lambda b, j, col=col: (b, 0, col))
    const = lambda a: pl.BlockSpec(a.shape, lambda b, j: (0,) * a.ndim)
    return pl.pallas_call(
        functools.partial(_finish_kernel, tm=tm, d=d),
        out_shape=[jax.ShapeDtypeStruct((g, r, d), F32), jax.ShapeDtypeStruct((g, r, d), BF16),
                   jax.ShapeDtypeStruct((g, r, LANES), F32)],
        grid=(g, nt),
        in_specs=[tok(d), tok(POOL_W), tok(QPAD), tok(2 * d), mod_spec(2), mod_spec(3), mod_spec(4)]
        + [const(w) for w in wts],
        out_specs=[tok(d), tok(d), tok(LANES)],
        compiler_params=_cparams(("arbitrary", "arbitrary")),
        name="finish",
    )(x3, pooled, ynsa, gm, g1, shift2, scale2, *wts)


def _moe_kernel(u_ref, gates_ref, x1_ref, g2_ref, nf_ref, wg_ref, wu_ref, wd_ref, sg_ref, su_ref, sd_ref,
                y_ref, acc_ref, *, tm, d, eps):
    e = pl.program_id(2)
    u = u_ref[...].reshape(tm, d)

    @pl.when(e == 0)
    def _():
        hs = _silu(_dot(u, sg_ref[...])) * _dot(u, su_ref[...])
        acc_ref[...] = _dot(hs.astype(BF16), sd_ref[...])

    gates = gates_ref[...].reshape(tm, LANES)
    lane = lax.broadcasted_iota(I32, (1, LANES), 1)
    hidden = []
    for j in range(eps):
        h = _silu(_dot(u, wg_ref[j].astype(BF16))) * _dot(u, wu_ref[j].astype(BF16))
        gate = jnp.sum(jnp.where(lane == e * eps + j, gates, 0.0), axis=1, keepdims=True)
        hidden.append((h * gate).astype(BF16))
    f = wd_ref.shape[1]
    acc_ref[...] += _dot(jnp.concatenate(hidden, axis=1), wd_ref[...].reshape(eps * f, d).astype(BF16))

    @pl.when(e == pl.num_programs(2) - 1)
    def _():
        x2 = x1_ref[...].reshape(tm, d) + _rows2d(g2_ref) * acc_ref[...]
        y_ref[...] = _rmsnorm(x2, nf_ref[...]).reshape(y_ref.shape)


def _moe(u2, gates, x1, g2, normf, w_gate, w_up, w_down, sg, su, sd, *, tm):
    g, r, d = x1.shape
    nt = r // tm
    ne, _, f = w_gate.shape
    per_row = g2.ndim == 2
    tok = lambda w: pl.BlockSpec((1, tm, w), lambda b, j, e: (b, j, 0))
    if per_row:
        g2_spec = pl.BlockSpec((tm, d), lambda b, j, e: (b * nt + j, 5))
    else:
        g2_spec = pl.BlockSpec((1, 1, d), lambda b, j, e: (b, 0, 5))
    once = pl.Buffered(buffer_count=1)
    const = lambda a: pl.BlockSpec(a.shape, lambda b, j, e: (0,) * a.ndim, pipeline_mode=once)
    eps = 4
    return pl.pallas_call(
        functools.partial(_moe_kernel, tm=tm, d=d, eps=eps),
        out_shape=jax.ShapeDtypeStruct((g, r, d), F32),
        grid=(g, nt, ne // eps),
        in_specs=[tok(d), tok(LANES),
                  pl.BlockSpec((1, tm, d), lambda b, j, e: (b, j, 0), pipeline_mode=once),
                  g2_spec, const(normf),
                  pl.BlockSpec((eps, d, f), lambda b, j, e: (e, 0, 0)),
                  pl.BlockSpec((eps, d, f), lambda b, j, e: (e, 0, 0)),
                  pl.BlockSpec((eps, f, d), lambda b, j, e: (e, 0, 0)),
                  const(sg), const(su), const(sd)],
        out_specs=tok(d),
        scratch_shapes=[pltpu.VMEM((tm, d), F32)],
        compiler_params=_cparams(("arbitrary", "arbitrary", "arbitrary")),
        name="moe",
    )(u2, gates, x1, g2, normf, w_gate, w_up, w_down, sg, su, sd)


def _kv_slot_mask():
    return (jnp.arange(N_HEADS)[:, None] // GROUP == jnp.arange(N_KV)[None, :]).astype(F32)


def _prep_w_in(w_in, d):
    q0 = POOL_W
    kv0 = q0 + N_HEADS * HEAD_DIM
    gn0 = kv0 + 6 * KVW
    gm0 = gn0 + 3 * N_HEADS
    wq = w_in[:, q0:kv0].reshape(d, N_HEADS, 1, HEAD_DIM) * (HEAD_DIM ** -0.5)
    wq = (wq * _kv_slot_mask()[None, :, :, None]).reshape(d, QPAD)
    wgn = jnp.pad(w_in[:, gn0:gm0], ((0, 0), (0, LANES - 3 * N_HEADS)))
    return jnp.concatenate([w_in[:, :q0], wq, w_in[:, kv0:gn0], wgn, w_in[:, gm0:]], axis=1).astype(BF16)


def _prep_w_nsa_out(w, d):
    w = w.reshape(N_HEADS, 1, HEAD_DIM, d) * _kv_slot_mask()[:, :, None, None]
    return w.reshape(QPAD, d).astype(BF16)


def _block_diag(w_lin):
    g, c, _ = w_lin.shape
    eye = jnp.eye(g, dtype=F32)
    return (w_lin[:, :, None, :] * eye[:, None, :, None]).reshape(g * c, g * c).astype(BF16)


def kernel(x_prompt, x_sample, cache_kc, cache_vc, cache_ks, cache_vs, state_kw, state_vw, state_pool,
           page_table, c_prompt, c_sample, norm1_g, norm2_g, normf_g, w_ada, b_ada, w_in, w_pool_lin,
           pool_scale, w_cmp_k, w_cmp_v, w_pool_out, w_nsa_out, w_o, w_router, b_router, w_gate, w_up,
           w_down, ws_gate, ws_up, ws_down):
    depth = w_in.shape[0]
    assert depth == 1, "single-layer stack"
    bsz, seq, d = x_prompt.shape
    n_seq, ns, _ = x_sample.shape
    wbuf = state_kw.shape[2]
    lyr = 0

    c_all = jnp.concatenate([c_prompt, c_sample], axis=0)
    rows = c_all.shape[0]
    rows_p = -(-rows // 8) * 8
    mod = _adaln(jnp.pad(c_all, ((0, rows_p - rows), (0, 0))), w_ada[lyr], b_ada[lyr])
    mod_p = mod[:bsz].reshape(bsz, 1, 6 * d)
    mod_s = jnp.repeat(mod[bsz:bsz + n_seq], ns, axis=0)

    w2 = _prep_w_in(w_in[lyr], d)
    g1n = norm1_g[lyr].reshape(1, d)
    wk = w_cmp_k[lyr].reshape(CMP_BLOCK, KVW)
    wv = w_cmp_v[lyr].reshape(CMP_BLOCK, KVW)
    fin_w = (_block_diag(w_pool_lin[lyr]), pool_scale[lyr].reshape(1, POOL_W), w_pool_out[lyr].astype(BF16),
             _prep_w_nsa_out(w_nsa_out[lyr], d), w_o[lyr].astype(BF16), norm2_g[lyr].reshape(1, d),
             jnp.pad(w_router[lyr].T, ((0, LANES - N_EXPERTS), (0, 0))).astype(BF16),
             b_router[lyr].reshape(N_EXPERTS, 1))
    moe_w = (w_gate[lyr], w_up[lyr], w_down[lyr], ws_gate[lyr].astype(BF16), ws_up[lyr].astype(BF16),
             ws_down[lyr].astype(BF16))
    nf = normf_g.reshape(1, d)

    tm_p = 512
    (vp, kc, vc, ks, vs, kw, vw, gm, ksb, kwb, vst, vwt, qt, gst, pooled) = _in_proj(
        x_prompt, mod_p, mod_p, g1n, w2, tm=tm_p, prompt=True)
    kcmp, vcmpt = _compress(kc, vc, wk, wv)
    ynsa = _nsa_prompt(qt, gst, kcmp, vcmpt, ksb, vst, kwb, vwt)
    x1, u2, gates = _finish(x_prompt, pooled, ynsa, gm, (mod_p, mod_p, mod_p), fin_w, tm=tm_p)
    y_prompt = _moe(u2, gates, x1, mod_p, nf, *moe_w, tm=1024)

    n_tok = n_seq * ns
    xs3 = x_sample.reshape(1, n_tok, d)
    tm_s = 128
    (vp_s, kc_s, vc_s, ks_s, vs_s, kw_s, vw_s, gm_s, q_s, gs_s) = _in_proj(
        xs3, mod_s, mod_s, g1n, w2, tm=tm_s, prompt=False)
    two = lambda a: a.reshape(n_tok, a.shape[-1])
    q_rows = q_s.reshape(n_tok * N_HEADS, LANES)
    gate_rows = two(gs_s)[:, :3 * N_HEADS].reshape(n_tok, 3, N_HEADS).transpose(0, 2, 1)
    gate_rows = jnp.pad(gate_rows.reshape(n_tok * N_HEADS, 3), ((0, 0), (0, LANES - 3)))
    n_pool = cache_kc.shape[1]
    page = cache_kc.shape[2]
    rows_minor = lambda a: jnp.transpose(a, (0, 2, 3, 1)).reshape(a.shape[0], KVW, a.shape[1])
    caches = [c[lyr].reshape(n_pool, page, KVW) for c in (cache_kc, cache_vc)]
    caches += [rows_minor(c[lyr]) for c in (cache_ks, cache_vs)]
    o_rows, pooled_s = _nsa_sample(
        page_table, q_rows, gate_rows, [two(a) for a in (kc_s, vc_s, ks_s, vs_s, kw_s, vw_s)], two(vp_s),
        rows_minor(state_kw[lyr]), rows_minor(state_vw[lyr]), state_pool[lyr], wk, wv, caches)
    ynsa_s = o_rows.reshape(1, n_tok, QPAD)
    x1_s, u2_s, gates_s = _finish(xs3, pooled_s.reshape(1, n_tok, POOL_W), ynsa_s, gm_s,
                                  (mod_s, mod_s, mod_s), fin_w, tm=tm_s)
    y_sample = _moe(u2_s, gates_s, x1_s, mod_s, nf, *moe_w, tm=n_tok).reshape(n_seq, ns, d)

    kvp = lambda a: a.reshape(1, bsz, seq, N_KV, HEAD_DIM)
    tailp = lambda a: jnp.pad(a, ((0, 0), (wbuf, 0), (0, 0)))[:, -wbuf:].reshape(1, bsz, wbuf, N_KV, HEAD_DIM)
    kvs = lambda a: a.reshape(1, n_seq, ns, N_KV, HEAD_DIM)
    wins = lambda st, new: jnp.concatenate(
        [st[lyr], new.reshape(n_seq, ns, N_KV, HEAD_DIM)], axis=1)[None, :, -wbuf:]
    pool_p = vp[:, -POOL_BUF:][None]
    pool_s = jnp.concatenate([state_pool[lyr], vp_s.reshape(n_seq, ns, POOL_W)], axis=1)[None, :, -POOL_BUF:]
    return (y_prompt, y_sample, kvp(kc), kvp(vc), kvp(ks), kvp(vs), tailp(kw), tailp(vw), pool_p,
            kvs(kc_s), kvs(vc_s), kvs(ks_s), kvs(vs_s), wins(state_kw, kw_s), wins(state_vw, vw_s), pool_s)
```

```python
import functools

import jax
import jax.numpy as jnp
from jax import lax
from jax.experimental import pallas as pl
from jax.experimental.pallas import tpu as pltpu

F32 = jnp.float32
BF16 = jnp.bfloat16
I32 = jnp.int32

POOL_WINDOWS = (2, 4, 8, 16)
POOL_GW = 64
POOL_W = 256
POOL_BUF = 15
N_HEADS = 8
HEAD_DIM = 64
N_KV = 2
GROUP = N_HEADS // N_KV
CMP_STRIDE = 16
CMP_BLOCK = 32
SEL_BLOCK = 64
TOP_BLOCKS = 16
WINDOW = 512
Q_BLOCK = 128
FORCE_SCORE = 1e4
N_EXPERTS = 64
N_EGROUPS = 8
EXPERTS_PER_GROUP = N_EXPERTS // N_EGROUPS
TOPK_GROUPS = 4
TOP_K = 8
ROUTED_SCALE = 2.5
EPS = 1e-6
NEG = -1e30
SLOPES = tuple(2.0 ** (-8.0 * (h + 1.0) / N_HEADS) for h in range(N_HEADS))

LANES = 128
QPAD = N_HEADS * LANES
KVW = N_KV * HEAD_DIM
VMEM_LIMIT = 56 * 1024 * 1024


def _cparams(sem):
    return pltpu.CompilerParams(dimension_semantics=sem, vmem_limit_bytes=VMEM_LIMIT)


def _dot(a, b):
    return jnp.dot(a, b, preferred_element_type=F32)


def _dot_nt(a, b):
    return lax.dot_general(a, b, (((1,), (1,)), ((), ())), preferred_element_type=F32)


def _dot_exact(a, b):
    return jnp.dot(a, b, preferred_element_type=F32, precision=lax.Precision.HIGHEST)


def _rows2d(ref):
    v = ref[...]
    return v.reshape(v.shape[-2], v.shape[-1])


def _rmsnorm(x, g):
    return x * lax.rsqrt(jnp.mean(x * x, axis=-1, keepdims=True) + EPS) * g


def _silu(x):
    return x * jax.nn.sigmoid(x)


def _adaln_kernel(c_ref, w_ref, b_ref, o_ref):
    s = _silu(c_ref[...]).astype(BF16)
    o_ref[...] = _dot(s, w_ref[...].astype(BF16)) + b_ref[...]


def _adaln(c, w_ada, b_ada):
    rows, d = c.shape
    n = w_ada.shape[1]
    tn = 512
    return pl.pallas_call(
        _adaln_kernel,
        out_shape=jax.ShapeDtypeStruct((rows, n), F32),
        grid=(n // tn,),
        in_specs=[pl.BlockSpec((rows, d), lambda j: (0, 0)),
                  pl.BlockSpec((d, tn), lambda j: (0, j)),
                  pl.BlockSpec((1, tn), lambda j: (0, j))],
        out_specs=pl.BlockSpec((rows, tn), lambda j: (0, j)),
        compiler_params=_cparams(("arbitrary",)),
        name="adaln",
    )(c, w_ada, b_ada.reshape(1, n))


_C_VP = 0
_C_Q = _C_VP + POOL_W
_C_KV = _C_Q + QPAD
_C_GN = _C_KV + 6 * KVW
_C_GM = _C_GN + LANES


def _pool_window_sums(ext, tm):
    s2 = ext + pltpu.roll(ext, 1, 0)
    s4 = s2 + pltpu.roll(s2, 2, 0)
    s8 = s4 + pltpu.roll(s4, 4, 0)
    s16 = s8 + pltpu.roll(s8, 8, 0)
    grp = lax.broadcasted_iota(I32, (1, POOL_W), 1) // POOL_GW
    pick = jnp.where(grp == 0, s2, jnp.where(grp == 1, s4, jnp.where(grp == 2, s8, s16)))
    return pick[16:16 + tm]


def _in_proj_kernel(x_ref, shift_ref, scale_ref, g_ref, w_ref,
                    vp_ref, kc_ref, vc_ref, ks_ref, vs_ref, kw_ref, vw_ref, gm_ref, *rest, tm, d, prompt):
    x = x_ref[...].reshape(tm, d)
    u = _rmsnorm(x, g_ref[...]) * (1.0 + _rows2d(scale_ref)) + _rows2d(shift_ref)
    ub = u.astype(BF16)

    def proj(c0, n):
        return _dot(ub, w_ref[:, c0:c0 + n])

    vp = proj(_C_VP, POOL_W)
    vp_ref[...] = vp.reshape(vp_ref.shape)
    kv = []
    for n, o32 in enumerate((kc_ref, vc_ref, ks_ref, vs_ref, kw_ref, vw_ref)):
        v = proj(_C_KV + n * KVW, KVW)
        o32[...] = v.reshape(o32.shape)
        kv.append(v)
    gm_ref[...] = jax.nn.sigmoid(proj(_C_GM, 2 * d)).reshape(gm_ref.shape)
    gs = jax.nn.sigmoid(proj(_C_GN, LANES))

    if not prompt:
        q_ref, gs_ref = rest
        q_ref[...] = proj(_C_Q, QPAD).astype(BF16).reshape(q_ref.shape)
        gs_ref[...] = gs.reshape(gs_ref.shape)
    else:
        ksb_ref, kwb_ref, vst_ref, vwt_ref, qt_ref, gst_ref, pooled_ref, halo_ref = rest
        ksb_ref[...] = kv[2].astype(BF16).reshape(ksb_ref.shape)
        kwb_ref[...] = kv[4].astype(BF16).reshape(kwb_ref.shape)
        vst_ref[...] = kv[3].T.astype(BF16).reshape(vst_ref.shape)
        vwt_ref[...] = kv[5].T.astype(BF16).reshape(vwt_ref.shape)
        gst_ref[...] = gs.T.reshape(gst_ref.shape)
        for h in range(N_HEADS):
            qt_ref[0, h] = proj(_C_Q + h * LANES, LANES).T.astype(BF16)
        j = pl.program_id(1)

        @pl.when(j == 0)
        def _():
            halo_ref[...] = jnp.zeros_like(halo_ref)

        ext = jnp.concatenate([halo_ref[...], vp], axis=0)
        sums = _pool_window_sums(ext, tm)
        pos = j * tm + lax.broadcasted_iota(I32, (tm, 1), 0)
        wcol = 2 << (lax.broadcasted_iota(I32, (1, POOL_W), 1) // POOL_GW)
        cnt = jnp.minimum(pos + 1, wcol).astype(F32)
        pooled_ref[...] = (sums / cnt - vp).astype(BF16).reshape(pooled_ref.shape)
        halo_ref[...] = vp[tm - 16:tm]


def _in_proj(x3, shift, scale, g1, w2, *, tm, prompt):
    g, r, d = x3.shape
    nt = r // tm
    per_row = shift.ndim == 2

    def tok(width, dtype):
        return (jax.ShapeDtypeStruct((g, r, width), dtype),
                pl.BlockSpec((1, tm, width), lambda b, j: (b, j, 0)))

    def tok_t(rows, dtype):
        return (jax.ShapeDtypeStruct((g, rows, r), dtype),
                pl.BlockSpec((1, rows, tm), lambda b, j: (b, 0, j)))

    outs = [tok(POOL_W, F32)] + [tok(KVW, F32)] * 6 + [tok(2 * d, F32)]
    scratch = []
    if prompt:
        outs += [tok(KVW, BF16), tok(KVW, BF16), tok_t(KVW, BF16), tok_t(KVW, BF16)]
        outs.append((jax.ShapeDtypeStruct((g, N_HEADS, LANES, r), BF16),
                     pl.BlockSpec((1, N_HEADS, LANES, tm), lambda b, j: (b, 0, 0, j))))
        outs += [tok_t(LANES, F32), tok(POOL_W, BF16)]
        scratch.append(pltpu.VMEM((16, POOL_W), F32))
    else:
        outs += [tok(QPAD, BF16), tok(LANES, F32)]
    if per_row:
        mod_spec = lambda col: pl.BlockSpec((tm, d), lambda b, j, col=col: (b * nt + j, col))
    else:
        mod_spec = lambda col: pl.BlockSpec((1, 1, d), lambda b, j, col=col: (b, 0, col))
    kern = functools.partial(_in_proj_kernel, tm=tm, d=d, prompt=prompt)
    return pl.pallas_call(
        kern,
        out_shape=[o[0] for o in outs],
        grid=(g, nt),
        in_specs=[pl.BlockSpec((1, tm, d), lambda b, j: (b, j, 0)),
                  mod_spec(0), mod_spec(1),
                  pl.BlockSpec((1, d), lambda b, j: (0, 0)),
                  pl.BlockSpec(w2.shape, lambda b, j: (0, 0))],
        out_specs=[o[1] for o in outs],
        scratch_shapes=scratch,
        compiler_params=_cparams(("arbitrary", "arbitrary")),
        name="in_proj_prompt" if prompt else "in_proj_sample",
    )(x3, shift, scale, g1, w2)


def _compress_kernel(kc_ref, vc_ref, wk_ref, wv_ref, okc_ref, ovc_ref, sh_ref, *, nc):
    last = lax.broadcasted_iota(I32, (nc, 1), 0) == nc - 1
    for src, w_ref, dst in ((kc_ref, wk_ref, okc_ref), (vc_ref, wv_ref, ovc_ref)):
        head = jnp.zeros((nc, KVW), F32)
        tail = jnp.zeros((nc, KVW), F32)
        for r in range(CMP_STRIDE):
            rows = src[pl.ds(r, nc, stride=CMP_STRIDE), :]
            head = head + rows * w_ref[r:r + 1, :]
            tail = tail + rows * w_ref[CMP_STRIDE + r:CMP_STRIDE + r + 1, :]
        sh_ref[0:nc, :] = tail
        sh_ref[nc:nc + 8, :] = jnp.zeros((8, KVW), F32)
        out = jnp.where(last, 0.0, head + sh_ref[1:nc + 1, :])
        dst[...] = (out if dst is okc_ref else out.T).astype(BF16)


def _compress(kc, vc, wk, wv):
    b, s, _ = kc.shape
    nc = s // CMP_STRIDE
    big = pl.BlockSpec((None, s, KVW), lambda i: (i, 0, 0))
    wsp = pl.BlockSpec((CMP_BLOCK, KVW), lambda i: (0, 0))
    return pl.pallas_call(
        functools.partial(_compress_kernel, nc=nc),
        out_shape=[jax.ShapeDtypeStruct((b, nc, KVW), BF16), jax.ShapeDtypeStruct((b, KVW, nc), BF16)],
        grid=(b,),
        in_specs=[big, big, wsp, wsp],
        out_specs=[pl.BlockSpec((None, nc, KVW), lambda i: (i, 0, 0)),
                   pl.BlockSpec((None, KVW, nc), lambda i: (i, 0, 0))],
        scratch_shapes=[pltpu.VMEM((nc + 8, KVW), F32)],
        compiler_params=_cparams(("arbitrary",)),
        name="compress",
    )(kc, vc, wk, wv)


def _topk_mask(vals, blk_f, n_top, axis=1):
    sel = jnp.zeros(vals.shape, F32)
    big = float(vals.shape[axis])
    for _ in range(n_top):
        mx = jnp.max(vals, axis=axis, keepdims=True)
        first = jnp.min(jnp.where(vals == mx, blk_f, big), axis=axis, keepdims=True)
        hit = blk_f == first
        sel = jnp.where(hit, 1.0, sel)
        vals = jnp.where(hit, -jnp.inf, vals)
    return sel


def _topk_mask_by_rank(vals, blk, n_valid, n_top):
    rank = jnp.zeros(vals.shape, F32)
    for j in range(n_valid):
        vj = vals[:, j:j + 1]
        beats = (vj > vals) | ((vj == vals) & (blk > j))
        rank = rank + jnp.where(beats, 1.0, 0.0)
    return jnp.where(rank < float(n_top), 1.0, 0.0)


def _pos_features(pos):
    hi = (pos // SEL_BLOCK).astype(F32)[:, None]
    lo = (pos % SEL_BLOCK).astype(F32)[:, None]
    return jnp.concatenate([hi, lo, jnp.zeros((pos.shape[0], LANES - 2), F32)], axis=1).astype(BF16)


def _importance_matrix(nc, nsel):
    j = jnp.arange(nc)[:, None]
    s = jnp.arange(nsel)[None, :]
    r = SEL_BLOCK // CMP_STRIDE
    a = (j >= r * s) & (j <= r * s + r - 1)
    b = (j + 1 >= r * s) & (j + 1 <= r * s + r - 1)
    return a.astype(F32) + b.astype(F32)


def _nsa_prompt_kernel(qt_ref, gst_ref, kc_ref, vct_ref, ks_ref, vst_ref, kw_ref, vwt_ref,
                       cfeat_ref, wfeat_ref, qfeat_ref, slope_ref,
                       y_ref, qk_scr, m_scr, l_scr, acc_scr, o_scr, sel_scr, imp_scr, flag_scr, *, seq, tk, wl):
    i = pl.program_id(1)
    q0 = i * Q_BLOCK
    nq = Q_BLOCK
    gq = GROUP * nq
    nc = kc_ref.shape[0]
    nsel = seq // SEL_BLOCK
    n_top = min(TOP_BLOCKS, nsel)
    blk_per_tile = tk // SEL_BLOCK
    qpos = q0 + lax.broadcasted_iota(I32, (1, nq), 1)
    gst = gst_ref[...]

    crow = lax.broadcasted_iota(I32, (nc, nq), 0)
    cend = crow * CMP_STRIDE + (CMP_BLOCK - 1)
    mask_c = qpos >= cend
    kc = jnp.concatenate([kc_ref[...], cfeat_ref[...]], axis=1)
    vct = vct_ref[...]
    blk = lax.broadcasted_iota(I32, (nsel, nq), 0)
    blk_f = blk.astype(F32)
    cur = qpos // SEL_BLOCK
    forced = (blk == 0) | (blk == cur) | (blk == cur - 1)
    visible = blk * SEL_BLOCK <= qpos
    ws = pl.multiple_of(jnp.maximum(q0 - WINDOW, 0), Q_BLOCK)
    wpos = ws + lax.broadcasted_iota(I32, (wl, nq), 0)
    valid_w = lax.bitcast_convert_type(qpos - wpos, jnp.uint32) < WINDOW
    n_tiles = (q0 + nq + tk - 1) // tk
    half_rows = lax.broadcasted_iota(I32, (KVW, nq), 0) // HEAD_DIM
    tile_pos = lax.broadcasted_iota(I32, (SEL_BLOCK, nq), 0)

    def lanes4(x):
        return jnp.concatenate([x] * GROUP, axis=1)

    def gate_row(branch, k):
        r0 = branch * N_HEADS + k * GROUP
        return jnp.concatenate([gst[r0 + g:r0 + g + 1] for g in range(GROUP)], axis=1)

    mask_c4 = lanes4(mask_c)
    valid_w4 = lanes4(valid_w)
    kwt = jnp.concatenate([kw_ref[pl.ds(ws, wl), :], wfeat_ref[...]], axis=1)
    vwtt = vwt_ref[:, pl.ds(ws, wl)]

    for k in range(N_KV):
        for g in range(GROUP):
            qk_scr[k, 0:LANES, g * nq:(g + 1) * nq] = qt_ref[k * GROUP + g]
        qk_scr[k, LANES:2 * LANES, :] = qfeat_ref[k]
        qk = qk_scr[k]

        s = jnp.where(mask_c4, _dot(kc, qk), NEG)
        e = jnp.where(mask_c4, jnp.exp(s - jnp.max(s, axis=0, keepdims=True)), 0.0)
        l = jnp.sum(e, axis=0, keepdims=True)
        p = e * jnp.where(l > 0.0, 1.0 / l, 0.0)
        o_c = _dot(vct, p.astype(BF16))
        psum = p[:, 0:nq]
        for g in range(1, GROUP):
            psum = psum + p[:, g * nq:(g + 1) * nq]

        a = psum + jnp.where(crow == 0, 0.0, pltpu.roll(psum, 1, 0))
        a = a + pltpu.roll(a, nc - 1, 0)
        imp_scr[...] = a + pltpu.roll(a, nc - 2, 0)
        imp = imp_scr[pl.ds(0, nsel, stride=nc // nsel), :]
        vals = jnp.where(visible, jnp.where(forced, FORCE_SCORE, imp), NEG)
        sel = jnp.where(visible, _topk_mask(vals, blk_f, n_top, axis=0), 0.0)
        sel_scr[k] = jnp.where(sel > 0.5, 0.0, NEG)
        blk_any = jnp.max(sel, axis=1, keepdims=True)
        for t in range(seq // tk):
            hit = (jnp.max(blk_any[t * blk_per_tile:(t + 1) * blk_per_tile, :]) > 0.5).astype(I32)
            flag_scr[t] = hit if k == 0 else flag_scr[t] | hit

        s = jnp.where(valid_w4, _dot(kwt, qk), NEG)
        e = jnp.exp(s - jnp.max(s, axis=0, keepdims=True))
        p = e / jnp.sum(e, axis=0, keepdims=True)
        o_w = _dot(vwtt, p.astype(BF16))
        o_scr[k] = gate_row(0, k) * o_c + gate_row(2, k) * o_w

    m_scr[...] = jnp.full(m_scr.shape, NEG, F32)
    l_scr[...] = jnp.zeros(l_scr.shape, F32)
    acc_scr[...] = jnp.zeros(acc_scr.shape, F32)

    def sel_tile(t):
        k0 = pl.multiple_of(t * tk, tk)
        kt = jnp.concatenate([ks_ref[pl.ds(k0, tk), :], wfeat_ref[0:tk, :]], axis=1)
        vtt = vst_ref[:, pl.ds(k0, tk)]
        causal = [qpos >= k0 + j * SEL_BLOCK + tile_pos for j in range(blk_per_tile)]
        base = (k0 - q0).astype(F32)
        scores = _dot(kt, jnp.concatenate([qk_scr[k] for k in range(N_KV)], axis=1))
        probs, alphas = [], []
        for k in range(N_KV):
            neg = []
            for j in range(blk_per_tile):
                row = sel_scr[k, pl.ds(t * blk_per_tile + j, 1), :]
                neg.append(jnp.where(causal[j], jnp.broadcast_to(row, (SEL_BLOCK, nq)), NEG))
            neg = lanes4(jnp.concatenate(neg, axis=0))
            off = slope_ref[k] * base
            s = scores[:, k * gq:(k + 1) * gq] + neg
            m_old = m_scr[k]
            m_new = jnp.maximum(m_old, jnp.max(s, axis=0, keepdims=True) + off)
            alpha = jnp.exp(m_old - m_new)
            p = jnp.exp(s - (m_new - off))
            l_scr[k] = alpha * l_scr[k] + jnp.sum(p, axis=0, keepdims=True)
            m_scr[k] = m_new
            probs.append(p.astype(BF16))
            alphas.append(alpha)
        pv = _dot(vtt, jnp.concatenate(probs, axis=1))
        for k in range(N_KV):
            acc_scr[k] = acc_scr[k] * alphas[k] + pv[:, k * gq:(k + 1) * gq]

    def sel_body(t, carry):
        pl.when(flag_scr[t] > 0)(functools.partial(sel_tile, t))
        return carry

    lax.fori_loop(0, n_tiles, sel_body, 0)

    for k in range(N_KV):
        o = o_scr[k] + gate_row(1, k) * (acc_scr[k] / l_scr[k])
        for g in range(GROUP):
            h = k * GROUP + g
            oh = jnp.where(half_rows == k, o[:, g * nq:(g + 1) * nq], 0.0)
            y_ref[:, h * LANES:(h + 1) * LANES] = oh.T.astype(BF16)


def _nsa_prompt(qt, gst, kcmp, vcmpt, ksb, vst, kwb, vwt):
    b, _, _, s = qt.shape
    nq = Q_BLOCK
    gq = GROUP * nq
    nc = kcmp.shape[1]
    nsel = s // SEL_BLOCK
    tk = 256
    wl = WINDOW + Q_BLOCK
    assert s % tk == 0 and s >= wl
    assert s // SEL_BLOCK <= 2 * LANES, "position // 64 must stay exact in bf16"
    cfeat = _pos_features(jnp.arange(nc) * CMP_STRIDE + (CMP_BLOCK - 1))
    wfeat = _pos_features(jnp.arange(wl))
    slope_rows = jnp.repeat(jnp.asarray(SLOPES, F32).reshape(N_KV, 1, GROUP), nq, axis=2)
    qfeat = jnp.concatenate([slope_rows * SEL_BLOCK, slope_rows, jnp.zeros((N_KV, LANES - 2, gq), F32)],
                            axis=1).astype(BF16)
    rows = lambda r: pl.BlockSpec((None, r, KVW), lambda bi, i: (bi, 0, 0))
    cols = lambda c: pl.BlockSpec((None, KVW, c), lambda bi, i: (bi, 0, 0))
    const = lambda a: pl.BlockSpec(a.shape, lambda bi, i: (0,) * a.ndim)
    return pl.pallas_call(
        functools.partial(_nsa_prompt_kernel, seq=s, tk=tk, wl=wl),
        out_shape=jax.ShapeDtypeStruct((b, s, QPAD), BF16),
        grid=(b, s // nq),
        in_specs=[pl.BlockSpec((None, N_HEADS, LANES, nq), lambda bi, i: (bi, 0, 0, i)),
                  pl.BlockSpec((None, LANES, nq), lambda bi, i: (bi, 0, i)),
                  rows(nc), cols(nc), rows(s), cols(s), rows(s), cols(s),
                  const(cfeat), const(wfeat), const(qfeat), const(slope_rows)],
        out_specs=pl.BlockSpec((None, nq, QPAD), lambda bi, i: (bi, i, 0)),
        scratch_shapes=[pltpu.VMEM((N_KV, 2 * LANES, gq), BF16),
                        pltpu.VMEM((N_KV, 1, gq), F32),
                        pltpu.VMEM((N_KV, 1, gq), F32),
                        pltpu.VMEM((N_KV, KVW, gq), F32),
                        pltpu.VMEM((N_KV, KVW, gq), F32),
                        pltpu.VMEM((N_KV, nsel, nq), F32),
                        pltpu.VMEM((nc, nq), F32),
                        pltpu.SMEM((s // tk,), I32)],
        compiler_params=_cparams(("arbitrary", "arbitrary")),
        name="nsa_prompt",
    )(qt, gst, kcmp, vcmpt, ksb, vst, kwb, vwt, cfeat, wfeat, qfeat, slope_rows)


def _nsa_sample_kernel(pt_ref, q_ref, gate_ref, kcn_ref, vcn_ref, ksn_ref, vsn_ref, kwn_ref, vwn_ref,
                       vpn_ref, skw_ref, svw_ref, spool_ref, wk_ref, wv_ref, imat_ref, emat_ref,
                       ckc_ref, cvc_ref, cks_ref, cvs_ref,
                       o_ref, pooled_ref, buf, buft, win_scr, tail_scr, vext_scr, sem,
                       *, sb, ns, past, n_pages, page, n_seq, ncv, ncp, nks, wls, nselp, n_sel):
    step = pl.program_id(0)
    nrow = ns * N_HEADS

    def copies(n, slot):
        out = []
        for p in range(n_pages):
            pg = pt_ref[n * n_pages + p]
            for c, cref in enumerate((ckc_ref, cvc_ref, cks_ref, cvs_ref)):
                out.append(pltpu.make_async_copy(cref.at[pg], buft.at[slot, c, :, pl.ds(p * page, page)],
                                                 sem.at[slot]))
        return out

    @pl.when(step == 0)
    def _():
        buf[:, past:, :] = jnp.zeros((2, buf.shape[1] - past, KVW), F32)
        tail_scr[...] = jnp.zeros_like(tail_scr)
        vext_scr[...] = jnp.zeros_like(vext_scr)
        for cp in copies(0, 0):
            cp.start()

    def new_rows_t(ref, r4):
        tail_scr[0:ns, :] = ref[pl.ds(r4, ns), :]
        return tail_scr[...].T

    row = lax.broadcasted_iota(I32, (nrow, 1), 0)
    hrow = row % N_HEADS
    qpos = past + row // N_HEADS
    slope = jnp.exp2(-8.0 * (hrow.astype(F32) + 1.0) / N_HEADS)
    kvrow = hrow // GROUP
    lane = lax.broadcasted_iota(I32, (1, LANES), 1)
    half = (lane // HEAD_DIM) == kvrow
    grow = (row // N_HEADS) * N_KV + kvrow
    row8 = lax.broadcasted_iota(I32, (ns * N_KV, 1), 0)
    qpos8 = past + lax.broadcasted_iota(I32, (ns * N_KV, 1), 0) // N_KV
    blk = lax.broadcasted_iota(I32, (1, nselp), 1)
    blk_f = blk.astype(F32)
    cur = qpos8 // SEL_BLOCK
    forced = (blk == 0) | (blk == cur) | (blk == cur - 1)
    visible = (blk * SEL_BLOCK <= qpos8)
    inrange = blk < n_sel
    cend = lax.broadcasted_iota(I32, (1, ncp), 1) * CMP_STRIDE + (CMP_BLOCK - 1)
    mask_c = qpos >= cend
    bias_c = slope * (cend - qpos).astype(F32)
    kpos = lax.broadcasted_iota(I32, (1, nks), 1)
    causal_s = qpos >= kpos
    bias_s = slope * (kpos - qpos).astype(F32)
    wbuf = wls[0]
    wpos = past - wbuf + lax.broadcasted_iota(I32, (1, wls[1]), 1)
    dw = qpos - wpos
    valid_w = lax.bitcast_convert_type(dw, jnp.uint32) < WINDOW
    bias_w = slope * (wpos - qpos).astype(F32)
    prow = lax.broadcasted_iota(I32, (vext_scr.shape[0], 1), 0)
    wcol = 2 << (lax.broadcasted_iota(I32, (1, POOL_W), 1) // POOL_GW)
    n_top = min(TOP_BLOCKS, n_sel)

    def softmax_rows(s, mask):
        s = jnp.where(mask, s, NEG)
        mx = jnp.max(s, axis=1, keepdims=True)
        e = jnp.where(mask, jnp.exp(s - mx), 0.0)
        l = jnp.sum(e, axis=1, keepdims=True)
        return e * jnp.where(l > 0.0, 1.0 / l, 0.0)

    def seq_body(r):
        n = step * sb + r
        slot = r % 2

        @pl.when(n + 1 < n_seq)
        def _():
            for cp in copies(n + 1, 1 - slot):
                cp.start()

        for cp in copies(n, slot):
            cp.wait()

        r4 = r * ns
        for c, new_ref in enumerate((kcn_ref, vcn_ref)):
            for p in range(n_pages):
                buf[c, p * page:(p + 1) * page, :] = buft[slot, c, :, p * page:(p + 1) * page].T
            buf[c, past:past + ns, :] = new_ref[pl.ds(r4, ns), :]
        for c, new_ref in ((2, ksn_ref), (3, vsn_ref)):
            buft[slot, c, :, past:past + LANES] = new_rows_t(new_ref, r4)

        qall = q_ref[pl.ds(r * nrow, nrow), :]
        gates = gate_ref[pl.ds(r * nrow, nrow), :]

        cmp = []
        for c, w_ref in ((0, wk_ref), (1, wv_ref)):
            span = CMP_STRIDE * ncv
            lo = buf[c, 0:span, :].reshape(ncv, CMP_STRIDE, KVW) * w_ref[0:CMP_STRIDE, :][None]
            hi = (buf[c, CMP_STRIDE:CMP_STRIDE + span, :].reshape(ncv, CMP_STRIDE, KVW)
                  * w_ref[CMP_STRIDE:CMP_BLOCK, :][None])
            acc = jnp.sum(lo + hi, axis=1)
            cmp.append(jnp.concatenate([acc, jnp.zeros((ncp - ncv, KVW), F32)], axis=0).astype(BF16))
        p_c = softmax_rows(_dot_nt(qall, cmp[0]) + bias_c, mask_c)
        o_c = _dot(p_c.astype(BF16), cmp[1])

        psum = jnp.zeros((ns * N_KV, ncp), F32)
        for i in range(ns * N_KV):
            r0 = (i // N_KV) * N_HEADS + (i % N_KV) * GROUP
            psum = jnp.where(row8 == i, jnp.sum(p_c[r0:r0 + GROUP], axis=0, keepdims=True), psum)
        imp = _dot_exact(psum, imat_ref[...])
        vals = jnp.where(inrange, jnp.where(visible, jnp.where(forced, FORCE_SCORE, imp), NEG), -jnp.inf)
        sel8 = _topk_mask_by_rank(vals, blk, n_sel, n_top)
        sel_rows = jnp.zeros((nrow, nselp), F32)
        for i in range(ns * N_KV):
            sel_rows = jnp.where(grow == i, sel8[i:i + 1], sel_rows)
        chosen = _dot(sel_rows.astype(BF16), emat_ref[...])

        kst = buft[slot, 2].astype(BF16)
        vst = buft[slot, 3].astype(BF16)
        p_s = softmax_rows(_dot(qall, kst) + bias_s, causal_s & (chosen > 0.5))
        o_s = _dot_nt(p_s.astype(BF16), vst)

        outs_w = []
        for state_ref, new_ref in ((skw_ref, kwn_ref), (svw_ref, vwn_ref)):
            win_scr[:, 0:wbuf] = state_ref[r]
            win_scr[:, wbuf:wbuf + LANES] = new_rows_t(new_ref, r4)
            outs_w.append(win_scr[...].astype(BF16))
        p_w = softmax_rows(_dot(qall, outs_w[0]) + bias_w, valid_w)
        o_w = _dot_nt(p_w.astype(BF16), outs_w[1])

        o = gates[:, 0:1] * o_c + gates[:, 1:2] * o_s + gates[:, 2:3] * o_w
        o_ref[pl.ds(r * nrow, nrow), :] = jnp.where(half, o, 0.0).astype(BF16)

        vext_scr[0:POOL_BUF, :] = spool_ref[r]
        vext_scr[POOL_BUF:POOL_BUF + ns, :] = vpn_ref[pl.ds(r4, ns), :]
        ext = vext_scr[...]
        for t in range(ns):
            hi = POOL_BUF + t
            inwin = (prow <= hi) & (prow > hi - wcol)
            ssum = jnp.sum(jnp.where(inwin, ext, 0.0), axis=0, keepdims=True)
            cnt = jnp.minimum(past + t + 1, wcol).astype(F32)
            pooled_ref[pl.ds(r4 + t, 1), :] = ssum / cnt - ext[hi:hi + 1, :]

    for r in range(sb):
        seq_body(r)


def _nsa_sample(page_table, q_rows, gate_rows, new6, vp_new, state_kwt, state_vwt, state_pool, wk, wv, caches):
    n_seq, n_pages = page_table.shape
    page = caches[0].shape[2]
    past = n_pages * page
    ns = vp_new.shape[0] // n_seq
    wbuf = state_kwt.shape[2]
    sb = 2
    nrow = ns * N_HEADS
    assert ns <= SEL_BLOCK and page == LANES
    t_pad = -(-(past + ns) // SEL_BLOCK) * SEL_BLOCK
    n_cmp = t_pad // CMP_STRIDE - 1
    ncv = -(-n_cmp // 8) * 8
    ncp = -(-ncv // LANES) * LANES
    nks = past + LANES
    n_sel = t_pad // SEL_BLOCK
    nselp = LANES
    assert n_sel <= nselp
    wlp = wbuf + LANES
    buf_rows = -(-(CMP_STRIDE * ncv + CMP_STRIDE) // 8) * 8
    imat = _importance_matrix(ncp, nselp)
    emat = (jnp.arange(nselp)[:, None] == (jnp.arange(nks)[None, :] // SEL_BLOCK)).astype(BF16)

    seqblk = lambda rows, w: pl.BlockSpec((sb * rows, w), lambda i, pt: (i, 0))
    const = lambda a: pl.BlockSpec(a.shape, lambda i, pt: (0,) * a.ndim)
    kern = functools.partial(
        _nsa_sample_kernel, sb=sb, ns=ns, past=past, n_pages=n_pages, page=page, n_seq=n_seq,
        ncv=ncv, ncp=ncp, nks=nks, wls=(wbuf, wlp), nselp=nselp, n_sel=n_sel)
    grid_spec = pltpu.PrefetchScalarGridSpec(
        num_scalar_prefetch=1,
        grid=(n_seq // sb,),
        in_specs=[seqblk(nrow, LANES), seqblk(nrow, LANES)] + [seqblk(ns, KVW)] * 6 + [seqblk(ns, POOL_W)]
        + [pl.BlockSpec((sb, KVW, wbuf), lambda i, pt: (i, 0, 0))] * 2
        + [pl.BlockSpec((sb, POOL_BUF, POOL_W), lambda i, pt: (i, 0, 0))]
        + [const(wk), const(wv), const(imat), const(emat)]
        + [pl.BlockSpec(memory_space=pl.ANY)] * 4,
        out_specs=[seqblk(nrow, LANES), seqblk(ns, POOL_W)],
        scratch_shapes=[pltpu.VMEM((2, buf_rows, KVW), F32),
                        pltpu.VMEM((2, 4, KVW, nks), F32),
                        pltpu.VMEM((KVW, wlp), F32),
                        pltpu.VMEM((LANES, KVW), F32),
                        pltpu.VMEM((24, POOL_W), F32),
                        pltpu.SemaphoreType.DMA((2,))],
    )
    return pl.pallas_call(
        kern,
        out_shape=[jax.ShapeDtypeStruct((n_seq * nrow, LANES), BF16),
                   jax.ShapeDtypeStruct((n_seq * ns, POOL_W), F32)],
        grid_spec=grid_spec,
        compiler_params=_cparams(("arbitrary",)),
        name="nsa_sample",
    )(page_table.reshape(-1), q_rows, gate_rows, *new6, vp_new, state_kwt, state_vwt, state_pool, wk, wv,
      imat, emat, *caches)


def _route(logits_t, bias_col, tm):
    sc = jax.nn.sigmoid(logits_t)
    biased = sc + bias_col
    epg = EXPERTS_PER_GROUP
    row8 = lax.broadcasted_iota(I32, (epg, tm), 0).astype(F32)
    ninf = -jnp.inf
    grp = jnp.zeros((N_EGROUPS, tm), F32)
    for g in range(N_EGROUPS):
        bg = biased[g * epg:(g + 1) * epg]
        m1 = jnp.max(bg, axis=0, keepdims=True)
        first = jnp.min(jnp.where(bg == m1, row8, float(epg)), axis=0, keepdims=True)
        m2 = jnp.max(jnp.where(row8 == first, ninf, bg), axis=0, keepdims=True)
        grp = jnp.where(row8 == float(g), m1 + m2, grp)
    keep = jnp.zeros((N_EGROUPS, tm), F32)
    vals = grp
    for _ in range(TOPK_GROUPS):
        mx = jnp.max(vals, axis=0, keepdims=True)
        first = jnp.min(jnp.where(vals == mx, row8, float(N_EGROUPS)), axis=0, keepdims=True)
        hit = row8 == first
        keep = jnp.where(hit, 1.0, keep)
        vals = jnp.where(hit, ninf, vals)
    masked = jnp.concatenate(
        [jnp.where(keep[g:g + 1] > 0.5, biased[g * epg:(g + 1) * epg], NEG) for g in range(N_EGROUPS)], axis=0)
    rowe = lax.broadcasted_iota(I32, (N_EXPERTS, tm), 0).astype(F32)
    chosen = jnp.zeros((N_EXPERTS, tm), F32)
    vals = masked
    for _ in range(TOP_K):
        mx = jnp.max(vals, axis=0, keepdims=True)
        first = jnp.min(jnp.where(vals == mx, rowe, float(N_EXPERTS)), axis=0, keepdims=True)
        hit = rowe == first
        chosen = jnp.where(hit, sc, chosen)
        vals = jnp.where(hit, ninf, vals)
    return ROUTED_SCALE * chosen / jnp.sum(chosen, axis=0, keepdims=True)


def _finish_kernel(x_ref, pooled_ref, y_ref, gm_ref, g1_ref, shift_ref, scale_ref,
                   wlin_ref, pscale_ref, wpo_ref, wno_ref, wo_ref, n2_ref, wr_ref, br_ref,
                   x1_ref, u2_ref, gates_ref, *, tm, d):
    x = x_ref[...].reshape(tm, d)
    pooled = pooled_ref[...].reshape(tm, POOL_W).astype(BF16)
    y_pool = _dot(pooled, wlin_ref[...]) * pscale_ref[...]
    a = _dot(y_pool.astype(BF16), wpo_ref[...])
    b = _dot(y_ref[...].reshape(tm, QPAD), wno_ref[...])
    gm = gm_ref[...].reshape(tm, 2 * d)
    merged = gm[:, :d] * a + gm[:, d:] * b
    x1 = x + _rows2d(g1_ref) * _dot(merged.astype(BF16), wo_ref[...])
    x1_ref[...] = x1.reshape(x1_ref.shape)
    u2 = _rmsnorm(x1, n2_ref[...]) * (1.0 + _rows2d(scale_ref)) + _rows2d(shift_ref)
    u2b = u2.astype(BF16)
    u2_ref[...] = u2b.reshape(u2_ref.shape)
    logits_t = _dot_nt(wr_ref[...], u2b)
    gates_t = _route(logits_t[:N_EXPERTS], br_ref[...], tm)
    gates_t = jnp.concatenate([gates_t, jnp.zeros((LANES - N_EXPERTS, tm), F32)], axis=0)
    gates_ref[...] = gates_t.T.reshape(gates_ref.shape)


def _finish(x3, pooled, ynsa, gm, mods, wts, *, tm):
    g, r, d = x3.shape
    nt = r // tm
    g1, shift2, scale2 = mods
    per_row = g1.ndim == 2
    tok = lambda w: pl.BlockSpec((1, tm, w), lambda b, j: (b, j, 0))
    if per_row:
        mod_spec = lambda col: pl.BlockSpec((tm, d), lambda b, j, col=col: (b * nt + j, col))
    else:
        mod_spec = lambda col: pl.BlockSpec((1, 1, d), lambda b, j, col=col: (b, 0, col))
    const = lambda a: pl.BlockSpec(a.shape, lambda b, j: (0,) * a.ndim)
    return pl.pallas_call(
        functools.partial(_finish_kernel, tm=tm, d=d),
        out_shape=[jax.ShapeDtypeStruct((g, r, d), F32), jax.ShapeDtypeStruct((g, r, d), BF16),
                   jax.ShapeDtypeStruct((g, r, LANES), F32)],
        grid=(g, nt),
        in_specs=[tok(d), tok(POOL_W), tok(QPAD), tok(2 * d), mod_spec(2), mod_spec(3), mod_spec(4)]
        + [const(w) for w in wts],
        out_specs=[tok(d), tok(d), tok(LANES)],
        compiler_params=_cparams(("arbitrary", "arbitrary")),
        name="finish",
    )(x3, pooled, ynsa, gm, g1, shift2, scale2, *wts)


def _moe_kernel(u_ref, gates_ref, x1_ref, g2_ref, nf_ref, wg_ref, wu_ref, wd_ref, sg_ref, su_ref, sd_ref,
                y_ref, acc_ref, *, tm, d, eps):
    e = pl.program_id(2)
    u = u_ref[...].reshape(tm, d)

    @pl.when(e == 0)
    def _():
        hs = _silu(_dot(u, sg_ref[...])) * _dot(u, su_ref[...])
        acc_ref[...] = _dot(hs.astype(BF16), sd_ref[...])

    gates = gates_ref[...].reshape(tm, LANES)
    lane = lax.broadcasted_iota(I32, (1, LANES), 1)
    hidden = []
    for j in range(eps):
        h = _silu(_dot(u, wg_ref[j].astype(BF16))) * _dot(u, wu_ref[j].astype(BF16))
        gate = jnp.sum(jnp.where(lane == e * eps + j, gates, 0.0), axis=1, keepdims=True)
        hidden.append((h * gate).astype(BF16))
    f = wd_ref.shape[1]
    acc_ref[...] += _dot(jnp.concatenate(hidden, axis=1), wd_ref[...].reshape(eps * f, d).astype(BF16))

    @pl.when(e == pl.num_programs(2) - 1)
    def _():
        x2 = x1_ref[...].reshape(tm, d) + _rows2d(g2_ref) * acc_ref[...]
        y_ref[...] = _rmsnorm(x2, nf_ref[...]).reshape(y_ref.shape)


def _moe(u2, gates, x1, g2, normf, w_gate, w_up, w_down, sg, su, sd, *, tm):
    g, r, d = x1.shape
    nt = r // tm
    ne, _, f = w_gate.shape
    per_row = g2.ndim == 2
    tok = lambda w: pl.BlockSpec((1, tm, w), lambda b, j, e: (b, j, 0))
    if per_row:
        g2_spec = pl.BlockSpec((tm, d), lambda b, j, e: (b * nt + j, 5))
    else:
        g2_spec = pl.BlockSpec((1, 1, d), lambda b, j, e: (b, 0, 5))
    once = pl.Buffered(buffer_count=1)
    const = lambda a: pl.BlockSpec(a.shape, lambda b, j, e: (0,) * a.ndim, pipeline_mode=once)
    eps = 4
    return pl.pallas_call(
        functools.partial(_moe_kernel, tm=tm, d=d, eps=eps),
        out_shape=jax.ShapeDtypeStruct((g, r, d), F32),
        grid=(g, nt, ne // eps),
        in_specs=[tok(d), tok(LANES),
                  pl.BlockSpec((1, tm, d), lambda b, j, e: (b, j, 0), pipeline_mode=once),
                  g2_spec, const(normf),
                  pl.BlockSpec((eps, d, f), lambda b, j, e: (e, 0, 0)),
                  pl.BlockSpec((eps, d, f), lambda b, j, e: (e, 0, 0)),
                  pl.BlockSpec((eps, f, d), lambda b, j, e: (e, 0, 0)),
                  const(sg), const(su), const(sd)],
        out_specs=tok(d),
        scratch_shapes=[pltpu.VMEM((tm, d), F32)],
        compiler_params=_cparams(("arbitrary", "arbitrary", "arbitrary")),
        name="moe",
    )(u2, gates, x1, g2, normf, w_gate, w_up, w_down, sg, su, sd)


def _kv_slot_mask():
    return (jnp.arange(N_HEADS)[:, None] // GROUP == jnp.arange(N_KV)[None, :]).astype(F32)


def _prep_w_in(w_in, d):
    q0 = POOL_W
    kv0 = q0 + N_HEADS * HEAD_DIM
    gn0 = kv0 + 6 * KVW
    gm0 = gn0 + 3 * N_HEADS
    wq = w_in[:, q0:kv0].reshape(d, N_HEADS, 1, HEAD_DIM) * (HEAD_DIM ** -0.5)
    wq = (wq * _kv_slot_mask()[None, :, :, None]).reshape(d, QPAD)
    wgn = jnp.pad(w_in[:, gn0:gm0], ((0, 0), (0, LANES - 3 * N_HEADS)))
    return jnp.concatenate([w_in[:, :q0], wq, w_in[:, kv0:gn0], wgn, w_in[:, gm0:]], axis=1).astype(BF16)


def _prep_w_nsa_out(w, d):
    w = w.reshape(N_HEADS, 1, HEAD_DIM, d) * _kv_slot_mask()[:, :, None, None]
    return w.reshape(QPAD, d).astype(BF16)


def _block_diag(w_lin):
    g, c, _ = w_lin.shape
    eye = jnp.eye(g, dtype=F32)
    return (w_lin[:, :, None, :] * eye[:, None, :, None]).reshape(g * c, g * c).astype(BF16)


def kernel(x_prompt, x_sample, cache_kc, cache_vc, cache_ks, cache_vs, state_kw, state_vw, state_pool,
           page_table, c_prompt, c_sample, norm1_g, norm2_g, normf_g, w_ada, b_ada, w_in, w_pool_lin,
           pool_scale, w_cmp_k, w_cmp_v, w_pool_out, w_nsa_out, w_o, w_router, b_router, w_gate, w_up,
           w_down, ws_gate, ws_up, ws_down):
    depth = w_in.shape[0]
    assert depth == 1, "single-layer stack"
    bsz, seq, d = x_prompt.shape
    n_seq, ns, _ = x_sample.shape
    wbuf = state_kw.shape[2]
    lyr = 0

    c_all = jnp.concatenate([c_prompt, c_sample], axis=0)
    rows = c_all.shape[0]
    rows_p = -(-rows // 8) * 8
    mod = _adaln(jnp.pad(c_all, ((0, rows_p - rows), (0, 0))), w_ada[lyr], b_ada[lyr])
    mod_p = mod[:bsz].reshape(bsz, 1, 6 * d)
    mod_s = jnp.repeat(mod[bsz:bsz + n_seq], ns, axis=0)

    w2 = _prep_w_in(w_in[lyr], d)
    g1n = norm1_g[lyr].reshape(1, d)
    wk = w_cmp_k[lyr].reshape(CMP_BLOCK, KVW)
    wv = w_cmp_v[lyr].reshape(CMP_BLOCK, KVW)
    fin_w = (_block_diag(w_pool_lin[lyr]), pool_scale[lyr].reshape(1, POOL_W), w_pool_out[lyr].astype(BF16),
             _prep_w_nsa_out(w_nsa_out[lyr], d), w_o[lyr].astype(BF16), norm2_g[lyr].reshape(1, d),
             jnp.pad(w_router[lyr].T, ((0, LANES - N_EXPERTS), (0, 0))).astype(BF16),
             b_router[lyr].reshape(N_EXPERTS, 1))
    moe_w = (w_gate[lyr], w_up[lyr], w_down[lyr], ws_gate[lyr].astype(BF16), ws_up[lyr].astype(BF16),
             ws_down[lyr].astype(BF16))
    nf = normf_g.reshape(1, d)

    tm_p = 512
    (vp, kc, vc, ks, vs, kw, vw, gm, ksb, kwb, vst, vwt, qt, gst, pooled) = _in_proj(
        x_prompt, mod_p, mod_p, g1n, w2, tm=tm_p, prompt=True)
    kcmp, vcmpt = _compress(kc, vc, wk, wv)
    ynsa = _nsa_prompt(qt, gst, kcmp, vcmpt, ksb, vst, kwb, vwt)
    x1, u2, gates = _finish(x_prompt, pooled, ynsa, gm, (mod_p, mod_p, mod_p), fin_w, tm=tm_p)
    y_prompt = _moe(u2, gates, x1, mod_p, nf, *moe_w, tm=1024)

    n_tok = n_seq * ns
    xs3 = x_sample.reshape(1, n_tok, d)
    tm_s = 128
    (vp_s, kc_s, vc_s, ks_s, vs_s, kw_s, vw_s, gm_s, q_s, gs_s) = _in_proj(
        xs3, mod_s, mod_s, g1n, w2, tm=tm_s, prompt=False)
    two = lambda a: a.reshape(n_tok, a.shape[-1])
    q_rows = q_s.reshape(n_tok * N_HEADS, LANES)
    gate_rows = two(gs_s)[:, :3 * N_HEADS].reshape(n_tok, 3, N_HEADS).transpose(0, 2, 1)
    gate_rows = jnp.pad(gate_rows.reshape(n_tok * N_HEADS, 3), ((0, 0), (0, LANES - 3)))
    n_pool = cache_kc.shape[1]
    page = cache_kc.shape[2]
    rows_minor = lambda a: jnp.transpose(a, (0, 2, 3, 1)).reshape(a.shape[0], KVW, a.shape[1])
    caches = [rows_minor(c[lyr]) for c in (cache_kc, cache_vc, cache_ks, cache_vs)]
    o_rows, pooled_s = _nsa_sample(
        page_table, q_rows, gate_rows, [two(a) for a in (kc_s, vc_s, ks_s, vs_s, kw_s, vw_s)], two(vp_s),
        rows_minor(state_kw[lyr]), rows_minor(state_vw[lyr]), state_pool[lyr], wk, wv, caches)
    ynsa_s = o_rows.reshape(1, n_tok, QPAD)
    x1_s, u2_s, gates_s = _finish(xs3, pooled_s.reshape(1, n_tok, POOL_W), ynsa_s, gm_s,
                                  (mod_s, mod_s, mod_s), fin_w, tm=tm_s)
    y_sample = _moe(u2_s, gates_s, x1_s, mod_s, nf, *moe_w, tm=n_tok).reshape(n_seq, ns, d)

    kvp = lambda a: a.reshape(1, bsz, seq, N_KV, HEAD_DIM)
    tailp = lambda a: jnp.pad(a, ((0, 0), (wbuf, 0), (0, 0)))[:, -wbuf:].reshape(1, bsz, wbuf, N_KV, HEAD_DIM)
    kvs = lambda a: a.reshape(1, n_seq, ns, N_KV, HEAD_DIM)
    wins = lambda st, new: jnp.concatenate(
        [st[lyr], new.reshape(n_seq, ns, N_KV, HEAD_DIM)], axis=1)[None, :, -wbuf:]
    pool_p = vp[:, -POOL_BUF:][None]
    pool_s = jnp.concatenate([state_pool[lyr], vp_s.reshape(n_seq, ns, POOL_W)], axis=1)[None, :, -POOL_BUF:]
    return (y_prompt, y_sample, kvp(kc), kvp(vc), kvp(ks), kvp(vs), tailp(kw), tailp(vw), pool_p,
            kvs(kc_s), kvs(vc_s), kvs(ks_s), kvs(vs_s), wins(state_kw, kw_s), wins(state_vw, vw_s), pool_s)
```

```python
import functools

import jax
import jax.numpy as jnp
from jax import lax
from jax.experimental import pallas as pl
from jax.experimental.pallas import tpu as pltpu
from jax.experimental.pallas import tpu_sc as plsc

F32 = jnp.float32
BF16 = jnp.bfloat16
I32 = jnp.int32

POOL_WINDOWS = (2, 4, 8, 16)
POOL_GW = 64
POOL_W = 256
POOL_BUF = 15
N_HEADS = 8
HEAD_DIM = 64
N_KV = 2
GROUP = N_HEADS // N_KV
CMP_STRIDE = 16
CMP_BLOCK = 32
SEL_BLOCK = 64
TOP_BLOCKS = 16
WINDOW = 512
Q_BLOCK = 128
FORCE_SCORE = 1e4
N_EXPERTS = 64
N_EGROUPS = 8
EXPERTS_PER_GROUP = N_EXPERTS // N_EGROUPS
TOPK_GROUPS = 4
TOP_K = 8
ROUTED_SCALE = 2.5
EPS = 1e-6
NEG = -1e30
SLOPES = tuple(2.0 ** (-8.0 * (h + 1.0) / N_HEADS) for h in range(N_HEADS))

LANES = 128
QPAD = N_HEADS * LANES
KVW = N_KV * HEAD_DIM
VMEM_LIMIT = 56 * 1024 * 1024


def _cparams(sem):
    return pltpu.CompilerParams(dimension_semantics=sem, vmem_limit_bytes=VMEM_LIMIT)


def _dot(a, b):
    return jnp.dot(a, b, preferred_element_type=F32)


def _dot_nt(a, b):
    return lax.dot_general(a, b, (((1,), (1,)), ((), ())), preferred_element_type=F32)


def _dot_exact(a, b):
    return jnp.dot(a, b, preferred_element_type=F32, precision=lax.Precision.HIGHEST)


def _rows2d(ref):
    v = ref[...]
    return v.reshape(v.shape[-2], v.shape[-1])


def _rmsnorm(x, g):
    return x * lax.rsqrt(jnp.mean(x * x, axis=-1, keepdims=True) + EPS) * g


def _silu(x):
    return x * jax.nn.sigmoid(x)


def _adaln_kernel(c_ref, w_ref, b_ref, o_ref):
    s = _silu(c_ref[...]).astype(BF16)
    o_ref[...] = _dot(s, w_ref[...].astype(BF16)) + b_ref[...]


def _adaln(c, w_ada, b_ada):
    rows, d = c.shape
    n = w_ada.shape[1]
    tn = 512
    return pl.pallas_call(
        _adaln_kernel,
        out_shape=jax.ShapeDtypeStruct((rows, n), F32),
        grid=(n // tn,),
        in_specs=[pl.BlockSpec((rows, d), lambda j: (0, 0)),
                  pl.BlockSpec((d, tn), lambda j: (0, j)),
                  pl.BlockSpec((1, tn), lambda j: (0, j))],
        out_specs=pl.BlockSpec((rows, tn), lambda j: (0, j)),
        compiler_params=_cparams(("arbitrary",)),
        name="adaln",
    )(c, w_ada, b_ada.reshape(1, n))


_C_VP = 0
_C_Q = _C_VP + POOL_W
_C_KV = _C_Q + QPAD
_C_GN = _C_KV + 6 * KVW
_C_GM = _C_GN + LANES


def _pool_window_sums(ext, tm):
    s2 = ext + pltpu.roll(ext, 1, 0)
    s4 = s2 + pltpu.roll(s2, 2, 0)
    s8 = s4 + pltpu.roll(s4, 4, 0)
    s16 = s8 + pltpu.roll(s8, 8, 0)
    grp = lax.broadcasted_iota(I32, (1, POOL_W), 1) // POOL_GW
    pick = jnp.where(grp == 0, s2, jnp.where(grp == 1, s4, jnp.where(grp == 2, s8, s16)))
    return pick[16:16 + tm]


def _in_proj_kernel(x_ref, shift_ref, scale_ref, g_ref, w_ref,
                    vp_ref, kc_ref, vc_ref, ks_ref, vs_ref, kw_ref, vw_ref, gm_ref, *rest, tm, d, prompt):
    x = x_ref[...].reshape(tm, d)
    u = _rmsnorm(x, g_ref[...]) * (1.0 + _rows2d(scale_ref)) + _rows2d(shift_ref)
    ub = u.astype(BF16)

    def proj(c0, n):
        return _dot(ub, w_ref[:, c0:c0 + n])

    vp = proj(_C_VP, POOL_W)
    vp_ref[...] = vp.reshape(vp_ref.shape)
    kv = []
    for n, o32 in enumerate((kc_ref, vc_ref, ks_ref, vs_ref, kw_ref, vw_ref)):
        v = proj(_C_KV + n * KVW, KVW)
        o32[...] = v.reshape(o32.shape)
        kv.append(v)
    gm_ref[...] = jax.nn.sigmoid(proj(_C_GM, 2 * d)).reshape(gm_ref.shape)
    gs = jax.nn.sigmoid(proj(_C_GN, LANES))

    if not prompt:
        q_ref, gs_ref = rest
        q_ref[...] = proj(_C_Q, QPAD).astype(BF16).reshape(q_ref.shape)
        gs_ref[...] = gs.reshape(gs_ref.shape)
    else:
        ksb_ref, kwb_ref, vst_ref, vwt_ref, qt_ref, gst_ref, pooled_ref, halo_ref = rest
        ksb_ref[...] = kv[2].astype(BF16).reshape(ksb_ref.shape)
        kwb_ref[...] = kv[4].astype(BF16).reshape(kwb_ref.shape)
        vst_ref[...] = kv[3].T.astype(BF16).reshape(vst_ref.shape)
        vwt_ref[...] = kv[5].T.astype(BF16).reshape(vwt_ref.shape)
        gst_ref[...] = gs.T.reshape(gst_ref.shape)
        for h in range(N_HEADS):
            qt_ref[0, h] = proj(_C_Q + h * LANES, LANES).T.astype(BF16)
        j = pl.program_id(1)

        @pl.when(j == 0)
        def _():
            halo_ref[...] = jnp.zeros_like(halo_ref)

        ext = jnp.concatenate([halo_ref[...], vp], axis=0)
        sums = _pool_window_sums(ext, tm)
        pos = j * tm + lax.broadcasted_iota(I32, (tm, 1), 0)
        wcol = 2 << (lax.broadcasted_iota(I32, (1, POOL_W), 1) // POOL_GW)
        cnt = jnp.minimum(pos + 1, wcol).astype(F32)
        pooled_ref[...] = (sums / cnt - vp).astype(BF16).reshape(pooled_ref.shape)
        halo_ref[...] = vp[tm - 16:tm]


def _in_proj(x3, shift, scale, g1, w2, *, tm, prompt):
    g, r, d = x3.shape
    nt = r // tm
    per_row = shift.ndim == 2

    def tok(width, dtype):
        return (jax.ShapeDtypeStruct((g, r, width), dtype),
                pl.BlockSpec((1, tm, width), lambda b, j: (b, j, 0)))

    def tok_t(rows, dtype):
        return (jax.ShapeDtypeStruct((g, rows, r), dtype),
                pl.BlockSpec((1, rows, tm), lambda b, j: (b, 0, j)))

    outs = [tok(POOL_W, F32)] + [tok(KVW, F32)] * 6 + [tok(2 * d, F32)]
    scratch = []
    if prompt:
        outs += [tok(KVW, BF16), tok(KVW, BF16), tok_t(KVW, BF16), tok_t(KVW, BF16)]
        outs.append((jax.ShapeDtypeStruct((g, N_HEADS, LANES, r), BF16),
                     pl.BlockSpec((1, N_HEADS, LANES, tm), lambda b, j: (b, 0, 0, j))))
        outs += [tok_t(LANES, F32), tok(POOL_W, BF16)]
        scratch.append(pltpu.VMEM((16, POOL_W), F32))
    else:
        outs += [tok(QPAD, BF16), tok(LANES, F32)]
    if per_row:
        mod_spec = lambda col: pl.BlockSpec((tm, d), lambda b, j, col=col: (b * nt + j, col))
    else:
        mod_spec = lambda col: pl.BlockSpec((1, 1, d), lambda b, j, col=col: (b, 0, col))
    kern = functools.partial(_in_proj_kernel, tm=tm, d=d, prompt=prompt)
    return pl.pallas_call(
        kern,
        out_shape=[o[0] for o in outs],
        grid=(g, nt),
        in_specs=[pl.BlockSpec((1, tm, d), lambda b, j: (b, j, 0)),
                  mod_spec(0), mod_spec(1),
                  pl.BlockSpec((1, d), lambda b, j: (0, 0)),
                  pl.BlockSpec(w2.shape, lambda b, j: (0, 0))],
        out_specs=[o[1] for o in outs],
        scratch_shapes=scratch,
        compiler_params=_cparams(("arbitrary", "arbitrary")),
        name="in_proj_prompt" if prompt else "in_proj_sample",
    )(x3, shift, scale, g1, w2)


def _compress_kernel(kc_ref, vc_ref, wk_ref, wv_ref, okc_ref, ovc_ref, sh_ref, *, nc):
    last = lax.broadcasted_iota(I32, (nc, 1), 0) == nc - 1
    for src, w_ref, dst in ((kc_ref, wk_ref, okc_ref), (vc_ref, wv_ref, ovc_ref)):
        head = jnp.zeros((nc, KVW), F32)
        tail = jnp.zeros((nc, KVW), F32)
        for r in range(CMP_STRIDE):
            rows = src[pl.ds(r, nc, stride=CMP_STRIDE), :]
            head = head + rows * w_ref[r:r + 1, :]
            tail = tail + rows * w_ref[CMP_STRIDE + r:CMP_STRIDE + r + 1, :]
        sh_ref[0:nc, :] = tail
        sh_ref[nc:nc + 8, :] = jnp.zeros((8, KVW), F32)
        out = jnp.where(last, 0.0, head + sh_ref[1:nc + 1, :])
        dst[...] = (out if dst is okc_ref else out.T).astype(BF16)


def _compress(kc, vc, wk, wv):
    b, s, _ = kc.shape
    nc = s // CMP_STRIDE
    big = pl.BlockSpec((None, s, KVW), lambda i: (i, 0, 0))
    wsp = pl.BlockSpec((CMP_BLOCK, KVW), lambda i: (0, 0))
    return pl.pallas_call(
        functools.partial(_compress_kernel, nc=nc),
        out_shape=[jax.ShapeDtypeStruct((b, nc, KVW), BF16), jax.ShapeDtypeStruct((b, KVW, nc), BF16)],
        grid=(b,),
        in_specs=[big, big, wsp, wsp],
        out_specs=[pl.BlockSpec((None, nc, KVW), lambda i: (i, 0, 0)),
                   pl.BlockSpec((None, KVW, nc), lambda i: (i, 0, 0))],
        scratch_shapes=[pltpu.VMEM((nc + 8, KVW), F32)],
        compiler_params=_cparams(("arbitrary",)),
        name="compress",
    )(kc, vc, wk, wv)


def _topk_mask(vals, blk_f, n_top, axis=1):
    sel = jnp.zeros(vals.shape, F32)
    big = float(vals.shape[axis])
    for _ in range(n_top):
        mx = jnp.max(vals, axis=axis, keepdims=True)
        first = jnp.min(jnp.where(vals == mx, blk_f, big), axis=axis, keepdims=True)
        hit = blk_f == first
        sel = jnp.where(hit, 1.0, sel)
        vals = jnp.where(hit, -jnp.inf, vals)
    return sel


def _topk_mask_by_rank(vals, blk, n_valid, n_top):
    rank = jnp.zeros(vals.shape, F32)
    for j in range(n_valid):
        vj = vals[:, j:j + 1]
        beats = (vj > vals) | ((vj == vals) & (blk > j))
        rank = rank + jnp.where(beats, 1.0, 0.0)
    return jnp.where(rank < float(n_top), 1.0, 0.0)


def _pos_features(pos):
    hi = (pos // SEL_BLOCK).astype(F32)[:, None]
    lo = (pos % SEL_BLOCK).astype(F32)[:, None]
    return jnp.concatenate([hi, lo, jnp.zeros((pos.shape[0], LANES - 2), F32)], axis=1).astype(BF16)


def _importance_matrix(nc, nsel):
    j = jnp.arange(nc)[:, None]
    s = jnp.arange(nsel)[None, :]
    r = SEL_BLOCK // CMP_STRIDE
    a = (j >= r * s) & (j <= r * s + r - 1)
    b = (j + 1 >= r * s) & (j + 1 <= r * s + r - 1)
    return a.astype(F32) + b.astype(F32)


def _nsa_prompt_kernel(qt_ref, gst_ref, kc_ref, vct_ref, ks_ref, vst_ref, kw_ref, vwt_ref,
                       cfeat_ref, wfeat_ref, qfeat_ref, slope_ref,
                       y_ref, qk_scr, m_scr, l_scr, acc_scr, o_scr, sel_scr, imp_scr, flag_scr, *, seq, tk, wl):
    i = pl.program_id(1)
    q0 = i * Q_BLOCK
    nq = Q_BLOCK
    gq = GROUP * nq
    nc = kc_ref.shape[0]
    nsel = seq // SEL_BLOCK
    n_top = min(TOP_BLOCKS, nsel)
    blk_per_tile = tk // SEL_BLOCK
    qpos = q0 + lax.broadcasted_iota(I32, (1, nq), 1)
    gst = gst_ref[...]

    crow = lax.broadcasted_iota(I32, (nc, nq), 0)
    cend = crow * CMP_STRIDE + (CMP_BLOCK - 1)
    mask_c = qpos >= cend
    kc = jnp.concatenate([kc_ref[...], cfeat_ref[...]], axis=1)
    vct = vct_ref[...]
    blk = lax.broadcasted_iota(I32, (nsel, nq), 0)
    blk_f = blk.astype(F32)
    cur = qpos // SEL_BLOCK
    forced = (blk == 0) | (blk == cur) | (blk == cur - 1)
    visible = blk * SEL_BLOCK <= qpos
    ws = pl.multiple_of(jnp.maximum(q0 - WINDOW, 0), Q_BLOCK)
    wpos = ws + lax.broadcasted_iota(I32, (wl, nq), 0)
    valid_w = lax.bitcast_convert_type(qpos - wpos, jnp.uint32) < WINDOW
    n_tiles = (q0 + nq + tk - 1) // tk
    half_rows = lax.broadcasted_iota(I32, (KVW, nq), 0) // HEAD_DIM
    tile_pos = lax.broadcasted_iota(I32, (SEL_BLOCK, nq), 0)

    def lanes4(x):
        return jnp.concatenate([x] * GROUP, axis=1)

    def gate_row(branch, k):
        r0 = branch * N_HEADS + k * GROUP
        return jnp.concatenate([gst[r0 + g:r0 + g + 1] for g in range(GROUP)], axis=1)

    mask_c4 = lanes4(mask_c)
    valid_w4 = lanes4(valid_w)
    kwt = jnp.concatenate([kw_ref[pl.ds(ws, wl), :], wfeat_ref[...]], axis=1)
    vwtt = vwt_ref[:, pl.ds(ws, wl)]

    for k in range(N_KV):
        for g in range(GROUP):
            qk_scr[k, 0:LANES, g * nq:(g + 1) * nq] = qt_ref[k * GROUP + g]
        qk_scr[k, LANES:2 * LANES, :] = qfeat_ref[k]
        qk = qk_scr[k]

        s = jnp.where(mask_c4, _dot(kc, qk), NEG)
        e = jnp.where(mask_c4, jnp.exp(s - jnp.max(s, axis=0, keepdims=True)), 0.0)
        l = jnp.sum(e, axis=0, keepdims=True)
        p = e * jnp.where(l > 0.0, 1.0 / l, 0.0)
        o_c = _dot(vct, p.astype(BF16))
        psum = p[:, 0:nq]
        for g in range(1, GROUP):
            psum = psum + p[:, g * nq:(g + 1) * nq]

        a = psum + jnp.where(crow == 0, 0.0, pltpu.roll(psum, 1, 0))
        a = a + pltpu.roll(a, nc - 1, 0)
        imp_scr[...] = a + pltpu.roll(a, nc - 2, 0)
        imp = imp_scr[pl.ds(0, nsel, stride=nc // nsel), :]
        vals = jnp.where(visible, jnp.where(forced, FORCE_SCORE, imp), NEG)
        sel = jnp.where(visible, _topk_mask(vals, blk_f, n_top, axis=0), 0.0)
        sel_scr[k] = jnp.where(sel > 0.5, 0.0, NEG)
        blk_any = jnp.max(sel, axis=1, keepdims=True)
        for t in range(seq // tk):
            hit = (jnp.max(blk_any[t * blk_per_tile:(t + 1) * blk_per_tile, :]) > 0.5).astype(I32)
            flag_scr[t] = hit if k == 0 else flag_scr[t] | hit

        s = jnp.where(valid_w4, _dot(kwt, qk), NEG)
        e = jnp.exp(s - jnp.max(s, axis=0, keepdims=True))
        p = e / jnp.sum(e, axis=0, keepdims=True)
        o_w = _dot(vwtt, p.astype(BF16))
        o_scr[k] = gate_row(0, k) * o_c + gate_row(2, k) * o_w

    m_scr[...] = jnp.full(m_scr.shape, NEG, F32)
    l_scr[...] = jnp.zeros(l_scr.shape, F32)
    acc_scr[...] = jnp.zeros(acc_scr.shape, F32)

    def sel_tile(t):
        k0 = pl.multiple_of(t * tk, tk)
        kt = jnp.concatenate([ks_ref[pl.ds(k0, tk), :], wfeat_ref[0:tk, :]], axis=1)
        vtt = vst_ref[:, pl.ds(k0, tk)]
        causal = [qpos >= k0 + j * SEL_BLOCK + tile_pos for j in range(blk_per_tile)]
        base = (k0 - q0).astype(F32)
        scores = _dot(kt, jnp.concatenate([qk_scr[k] for k in range(N_KV)], axis=1))
        probs, alphas = [], []
        for k in range(N_KV):
            neg = []
            for j in range(blk_per_tile):
                row = sel_scr[k, pl.ds(t * blk_per_tile + j, 1), :]
                neg.append(jnp.where(causal[j], jnp.broadcast_to(row, (SEL_BLOCK, nq)), NEG))
            neg = lanes4(jnp.concatenate(neg, axis=0))
            off = slope_ref[k] * base
            s = scores[:, k * gq:(k + 1) * gq] + neg
            m_old = m_scr[k]
            m_new = jnp.maximum(m_old, jnp.max(s, axis=0, keepdims=True) + off)
            alpha = jnp.exp(m_old - m_new)
            p = jnp.exp(s - (m_new - off))
            l_scr[k] = alpha * l_scr[k] + jnp.sum(p, axis=0, keepdims=True)
            m_scr[k] = m_new
            probs.append(p.astype(BF16))
            alphas.append(alpha)
        pv = _dot(vtt, jnp.concatenate(probs, axis=1))
        for k in range(N_KV):
            acc_scr[k] = acc_scr[k] * alphas[k] + pv[:, k * gq:(k + 1) * gq]

    def sel_body(t, carry):
        pl.when(flag_scr[t] > 0)(functools.partial(sel_tile, t))
        return carry

    lax.fori_loop(0, n_tiles, sel_body, 0)

    for k in range(N_KV):
        o = o_scr[k] + gate_row(1, k) * (acc_scr[k] / l_scr[k])
        for g in range(GROUP):
            h = k * GROUP + g
            oh = jnp.where(half_rows == k, o[:, g * nq:(g + 1) * nq], 0.0)
            y_ref[:, h * LANES:(h + 1) * LANES] = oh.T.astype(BF16)


def _nsa_prompt(qt, gst, kcmp, vcmpt, ksb, vst, kwb, vwt):
    b, _, _, s = qt.shape
    nq = Q_BLOCK
    gq = GROUP * nq
    nc = kcmp.shape[1]
    nsel = s // SEL_BLOCK
    tk = 256
    wl = WINDOW + Q_BLOCK
    assert s % tk == 0 and s >= wl
    assert s // SEL_BLOCK <= 2 * LANES, "position // 64 must stay exact in bf16"
    cfeat = _pos_features(jnp.arange(nc) * CMP_STRIDE + (CMP_BLOCK - 1))
    wfeat = _pos_features(jnp.arange(wl))
    slope_rows = jnp.repeat(jnp.asarray(SLOPES, F32).reshape(N_KV, 1, GROUP), nq, axis=2)
    qfeat = jnp.concatenate([slope_rows * SEL_BLOCK, slope_rows, jnp.zeros((N_KV, LANES - 2, gq), F32)],
                            axis=1).astype(BF16)
    rows = lambda r: pl.BlockSpec((None, r, KVW), lambda bi, i: (bi, 0, 0))
    cols = lambda c: pl.BlockSpec((None, KVW, c), lambda bi, i: (bi, 0, 0))
    const = lambda a: pl.BlockSpec(a.shape, lambda bi, i: (0,) * a.ndim)
    return pl.pallas_call(
        functools.partial(_nsa_prompt_kernel, seq=s, tk=tk, wl=wl),
        out_shape=jax.ShapeDtypeStruct((b, s, QPAD), BF16),
        grid=(b, s // nq),
        in_specs=[pl.BlockSpec((None, N_HEADS, LANES, nq), lambda bi, i: (bi, 0, 0, i)),
                  pl.BlockSpec((None, LANES, nq), lambda bi, i: (bi, 0, i)),
                  rows(nc), cols(nc), rows(s), cols(s), rows(s), cols(s),
                  const(cfeat), const(wfeat), const(qfeat), const(slope_rows)],
        out_specs=pl.BlockSpec((None, nq, QPAD), lambda bi, i: (bi, i, 0)),
        scratch_shapes=[pltpu.VMEM((N_KV, 2 * LANES, gq), BF16),
                        pltpu.VMEM((N_KV, 1, gq), F32),
                        pltpu.VMEM((N_KV, 1, gq), F32),
                        pltpu.VMEM((N_KV, KVW, gq), F32),
                        pltpu.VMEM((N_KV, KVW, gq), F32),
                        pltpu.VMEM((N_KV, nsel, nq), F32),
                        pltpu.VMEM((nc, nq), F32),
                        pltpu.SMEM((s // tk,), I32)],
        compiler_params=_cparams(("arbitrary", "arbitrary")),
        name="nsa_prompt",
    )(qt, gst, kcmp, vcmpt, ksb, vst, kwb, vwt, cfeat, wfeat, qfeat, slope_rows)


def _nsa_sample_kernel(pt_ref, q_ref, gate_ref, kcn_ref, vcn_ref, ksn_ref, vsn_ref, kwn_ref, vwn_ref,
                       vpn_ref, skw_ref, svw_ref, spool_ref, wk_ref, wv_ref, imat_ref, emat_ref,
                       ckc_ref, cvc_ref, cks_ref, cvs_ref,
                       o_ref, pooled_ref, buf, buft, win_scr, tail_scr, vext_scr, sem,
                       *, sb, ns, past, n_pages, page, n_seq, ncv, ncp, nks, wls, nselp, n_sel):
    step = pl.program_id(0)
    nrow = ns * N_HEADS

    def copies(n, slot):
        out = []
        for p in range(n_pages):
            pg = pt_ref[n * n_pages + p]
            for c, cref in enumerate((ckc_ref, cvc_ref, cks_ref, cvs_ref)):
                out.append(pltpu.make_async_copy(cref.at[pg], buft.at[slot, c, :, pl.ds(p * page, page)],
                                                 sem.at[slot]))
        return out

    @pl.when(step == 0)
    def _():
        buf[:, past:, :] = jnp.zeros((2, buf.shape[1] - past, KVW), F32)
        tail_scr[...] = jnp.zeros_like(tail_scr)
        vext_scr[...] = jnp.zeros_like(vext_scr)
        for cp in copies(0, 0):
            cp.start()

    def new_rows_t(ref, r4):
        tail_scr[0:ns, :] = ref[pl.ds(r4, ns), :]
        return tail_scr[...].T

    row = lax.broadcasted_iota(I32, (nrow, 1), 0)
    hrow = row % N_HEADS
    qpos = past + row // N_HEADS
    slope = jnp.exp2(-8.0 * (hrow.astype(F32) + 1.0) / N_HEADS)
    kvrow = hrow // GROUP
    lane = lax.broadcasted_iota(I32, (1, LANES), 1)
    half = (lane // HEAD_DIM) == kvrow
    grow = (row // N_HEADS) * N_KV + kvrow
    row8 = lax.broadcasted_iota(I32, (ns * N_KV, 1), 0)
    qpos8 = past + lax.broadcasted_iota(I32, (ns * N_KV, 1), 0) // N_KV
    blk = lax.broadcasted_iota(I32, (1, nselp), 1)
    blk_f = blk.astype(F32)
    cur = qpos8 // SEL_BLOCK
    forced = (blk == 0) | (blk == cur) | (blk == cur - 1)
    visible = (blk * SEL_BLOCK <= qpos8)
    inrange = blk < n_sel
    cend = lax.broadcasted_iota(I32, (1, ncp), 1) * CMP_STRIDE + (CMP_BLOCK - 1)
    mask_c = qpos >= cend
    bias_c = slope * (cend - qpos).astype(F32)
    kpos = lax.broadcasted_iota(I32, (1, nks), 1)
    causal_s = qpos >= kpos
    bias_s = slope * (kpos - qpos).astype(F32)
    wbuf = wls[0]
    wpos = past - wbuf + lax.broadcasted_iota(I32, (1, wls[1]), 1)
    dw = qpos - wpos
    valid_w = lax.bitcast_convert_type(dw, jnp.uint32) < WINDOW
    bias_w = slope * (wpos - qpos).astype(F32)
    prow = lax.broadcasted_iota(I32, (vext_scr.shape[0], 1), 0)
    wcol = 2 << (lax.broadcasted_iota(I32, (1, POOL_W), 1) // POOL_GW)
    n_top = min(TOP_BLOCKS, n_sel)

    def softmax_rows(s, mask):
        s = jnp.where(mask, s, NEG)
        mx = jnp.max(s, axis=1, keepdims=True)
        e = jnp.where(mask, jnp.exp(s - mx), 0.0)
        l = jnp.sum(e, axis=1, keepdims=True)
        return e * jnp.where(l > 0.0, 1.0 / l, 0.0)

    def seq_body(r):
        n = step * sb + r
        slot = r % 2

        @pl.when(n + 1 < n_seq)
        def _():
            for cp in copies(n + 1, 1 - slot):
                cp.start()

        for cp in copies(n, slot):
            cp.wait()

        r4 = r * ns
        for c, new_ref in enumerate((kcn_ref, vcn_ref)):
            for p in range(n_pages):
                buf[c, p * page:(p + 1) * page, :] = buft[slot, c, :, p * page:(p + 1) * page].T
            buf[c, past:past + ns, :] = new_ref[pl.ds(r4, ns), :]
        for c, new_ref in ((2, ksn_ref), (3, vsn_ref)):
            buft[slot, c, :, past:past + LANES] = new_rows_t(new_ref, r4)

        qall = q_ref[pl.ds(r * nrow, nrow), :]
        gates = gate_ref[pl.ds(r * nrow, nrow), :]

        cmp = []
        for c, w_ref in ((0, wk_ref), (1, wv_ref)):
            span = CMP_STRIDE * ncv
            lo = buf[c, 0:span, :].reshape(ncv, CMP_STRIDE, KVW) * w_ref[0:CMP_STRIDE, :][None]
            hi = (buf[c, CMP_STRIDE:CMP_STRIDE + span, :].reshape(ncv, CMP_STRIDE, KVW)
                  * w_ref[CMP_STRIDE:CMP_BLOCK, :][None])
            acc = jnp.sum(lo + hi, axis=1)
            cmp.append(jnp.concatenate([acc, jnp.zeros((ncp - ncv, KVW), F32)], axis=0).astype(BF16))
        p_c = softmax_rows(_dot_nt(qall, cmp[0]) + bias_c, mask_c)
        o_c = _dot(p_c.astype(BF16), cmp[1])

        psum = jnp.zeros((ns * N_KV, ncp), F32)
        for i in range(ns * N_KV):
            r0 = (i // N_KV) * N_HEADS + (i % N_KV) * GROUP
            psum = jnp.where(row8 == i, jnp.sum(p_c[r0:r0 + GROUP], axis=0, keepdims=True), psum)
        imp = _dot_exact(psum, imat_ref[...])
        vals = jnp.where(inrange, jnp.where(visible, jnp.where(forced, FORCE_SCORE, imp), NEG), -jnp.inf)
        sel8 = _topk_mask_by_rank(vals, blk, n_sel, n_top)
        sel_rows = jnp.zeros((nrow, nselp), F32)
        for i in range(ns * N_KV):
            sel_rows = jnp.where(grow == i, sel8[i:i + 1], sel_rows)
        chosen = _dot(sel_rows.astype(BF16), emat_ref[...])

        kst = buft[slot, 2].astype(BF16)
        vst = buft[slot, 3].astype(BF16)
        p_s = softmax_rows(_dot(qall, kst) + bias_s, causal_s & (chosen > 0.5))
        o_s = _dot_nt(p_s.astype(BF16), vst)

        outs_w = []
        for state_ref, new_ref in ((skw_ref, kwn_ref), (svw_ref, vwn_ref)):
            win_scr[:, 0:wbuf] = state_ref[r]
            win_scr[:, wbuf:wbuf + LANES] = new_rows_t(new_ref, r4)
            outs_w.append(win_scr[...].astype(BF16))
        p_w = softmax_rows(_dot(qall, outs_w[0]) + bias_w, valid_w)
        o_w = _dot_nt(p_w.astype(BF16), outs_w[1])

        o = gates[:, 0:1] * o_c + gates[:, 1:2] * o_s + gates[:, 2:3] * o_w
        o_ref[pl.ds(r * nrow, nrow), :] = jnp.where(half, o, 0.0).astype(BF16)

        vext_scr[0:POOL_BUF, :] = spool_ref[r]
        vext_scr[POOL_BUF:POOL_BUF + ns, :] = vpn_ref[pl.ds(r4, ns), :]
        ext = vext_scr[...]
        for t in range(ns):
            hi = POOL_BUF + t
            inwin = (prow <= hi) & (prow > hi - wcol)
            ssum = jnp.sum(jnp.where(inwin, ext, 0.0), axis=0, keepdims=True)
            cnt = jnp.minimum(past + t + 1, wcol).astype(F32)
            pooled_ref[pl.ds(r4 + t, 1), :] = ssum / cnt - ext[hi:hi + 1, :]

    for r in range(sb):
        seq_body(r)


def _nsa_sample(page_table, q_rows, gate_rows, new6, vp_new, state_kwt, state_vwt, state_pool, wk, wv, caches):
    n_seq, n_pages = page_table.shape
    page = caches[0].shape[2]
    past = n_pages * page
    ns = vp_new.shape[0] // n_seq
    wbuf = state_kwt.shape[2]
    sb = 2
    nrow = ns * N_HEADS
    assert ns <= SEL_BLOCK and page == LANES
    t_pad = -(-(past + ns) // SEL_BLOCK) * SEL_BLOCK
    n_cmp = t_pad // CMP_STRIDE - 1
    ncv = -(-n_cmp // 8) * 8
    ncp = -(-ncv // LANES) * LANES
    nks = past + LANES
    n_sel = t_pad // SEL_BLOCK
    nselp = LANES
    assert n_sel <= nselp
    wlp = wbuf + LANES
    buf_rows = -(-(CMP_STRIDE * ncv + CMP_STRIDE) // 8) * 8
    imat = _importance_matrix(ncp, nselp)
    emat = (jnp.arange(nselp)[:, None] == (jnp.arange(nks)[None, :] // SEL_BLOCK)).astype(BF16)

    seqblk = lambda rows, w: pl.BlockSpec((sb * rows, w), lambda i, pt: (i, 0))
    const = lambda a: pl.BlockSpec(a.shape, lambda i, pt: (0,) * a.ndim)
    kern = functools.partial(
        _nsa_sample_kernel, sb=sb, ns=ns, past=past, n_pages=n_pages, page=page, n_seq=n_seq,
        ncv=ncv, ncp=ncp, nks=nks, wls=(wbuf, wlp), nselp=nselp, n_sel=n_sel)
    grid_spec = pltpu.PrefetchScalarGridSpec(
        num_scalar_prefetch=1,
        grid=(n_seq // sb,),
        in_specs=[seqblk(nrow, LANES), seqblk(nrow, LANES)] + [seqblk(ns, KVW)] * 6 + [seqblk(ns, POOL_W)]
        + [pl.BlockSpec((sb, KVW, wbuf), lambda i, pt: (i, 0, 0))] * 2
        + [pl.BlockSpec((sb, POOL_BUF, POOL_W), lambda i, pt: (i, 0, 0))]
        + [const(wk), const(wv), const(imat), const(emat)]
        + [pl.BlockSpec(memory_space=pl.ANY)] * 4,
        out_specs=[seqblk(nrow, LANES), seqblk(ns, POOL_W)],
        scratch_shapes=[pltpu.VMEM((2, buf_rows, KVW), F32),
                        pltpu.VMEM((2, 4, KVW, nks), F32),
                        pltpu.VMEM((KVW, wlp), F32),
                        pltpu.VMEM((LANES, KVW), F32),
                        pltpu.VMEM((24, POOL_W), F32),
                        pltpu.SemaphoreType.DMA((2,))],
    )
    return pl.pallas_call(
        kern,
        out_shape=[jax.ShapeDtypeStruct((n_seq * nrow, LANES), BF16),
                   jax.ShapeDtypeStruct((n_seq * ns, POOL_W), F32)],
        grid_spec=grid_spec,
        compiler_params=_cparams(("arbitrary",)),
        name="nsa_sample",
    )(page_table.reshape(-1), q_rows, gate_rows, *new6, vp_new, state_kwt, state_vwt, state_pool, wk, wv,
      imat, emat, *caches)


def _route(logits_t, bias_col, tm):
    sc = jax.nn.sigmoid(logits_t)
    biased = sc + bias_col
    epg = EXPERTS_PER_GROUP
    row8 = lax.broadcasted_iota(I32, (epg, tm), 0).astype(F32)
    ninf = -jnp.inf
    grp = jnp.zeros((N_EGROUPS, tm), F32)
    for g in range(N_EGROUPS):
        bg = biased[g * epg:(g + 1) * epg]
        m1 = jnp.max(bg, axis=0, keepdims=True)
        first = jnp.min(jnp.where(bg == m1, row8, float(epg)), axis=0, keepdims=True)
        m2 = jnp.max(jnp.where(row8 == first, ninf, bg), axis=0, keepdims=True)
        grp = jnp.where(row8 == float(g), m1 + m2, grp)
    keep = jnp.zeros((N_EGROUPS, tm), F32)
    vals = grp
    for _ in range(TOPK_GROUPS):
        mx = jnp.max(vals, axis=0, keepdims=True)
        first = jnp.min(jnp.where(vals == mx, row8, float(N_EGROUPS)), axis=0, keepdims=True)
        hit = row8 == first
        keep = jnp.where(hit, 1.0, keep)
        vals = jnp.where(hit, ninf, vals)
    masked = jnp.concatenate(
        [jnp.where(keep[g:g + 1] > 0.5, biased[g * epg:(g + 1) * epg], NEG) for g in range(N_EGROUPS)], axis=0)
    rowe = lax.broadcasted_iota(I32, (N_EXPERTS, tm), 0).astype(F32)
    chosen = jnp.zeros((N_EXPERTS, tm), F32)
    vals = masked
    picks = []
    for _ in range(TOP_K):
        mx = jnp.max(vals, axis=0, keepdims=True)
        first = jnp.min(jnp.where(vals == mx, rowe, float(N_EXPERTS)), axis=0, keepdims=True)
        hit = rowe == first
        chosen = jnp.where(hit, sc, chosen)
        vals = jnp.where(hit, ninf, vals)
        picks.append((hit, first))
    return ROUTED_SCALE * chosen / jnp.sum(chosen, axis=0, keepdims=True), picks


def _pack_bf16_pairs(x):
    c = x.shape[1] // 2
    bits = lambda v: lax.bitcast_convert_type(v.astype(BF16).astype(F32), jnp.uint32)
    return (bits(x[:, :c]) >> 16) | (bits(x[:, c:]) & jnp.uint32(0xFFFF0000))


def _unpack_bf16_pairs(w):
    lo = lax.bitcast_convert_type(w << 16, F32)
    hi = lax.bitcast_convert_type(w & jnp.uint32(0xFFFF0000), F32)
    return jnp.concatenate([lo, hi], axis=1)


def _finish_kernel(x_ref, pooled_ref, y_ref, gm_ref, g1_ref, shift_ref, scale_ref,
                   wlin_ref, pscale_ref, wpo_ref, wno_ref, wo_ref, n2_ref, wr_ref, br_ref,
                   *rest, tm, d, sparse):
    if sparse:
        tri_ref, x1_ref, u2_ref, up_ref, eid_ref, gk_ref, rank_ref, cnt_ref, carry_scr = rest
    else:
        x1_ref, u2_ref, gates_ref = rest
    x = x_ref[...].reshape(tm, d)
    pooled = pooled_ref[...].reshape(tm, POOL_W).astype(BF16)
    y_pool = _dot(pooled, wlin_ref[...]) * pscale_ref[...]
    a = _dot(y_pool.astype(BF16), wpo_ref[...])
    b = _dot(y_ref[...].reshape(tm, QPAD), wno_ref[...])
    gm = gm_ref[...].reshape(tm, 2 * d)
    merged = gm[:, :d] * a + gm[:, d:] * b
    x1 = x + _rows2d(g1_ref) * _dot(merged.astype(BF16), wo_ref[...])
    x1_ref[...] = x1.reshape(x1_ref.shape)
    u2 = _rmsnorm(x1, n2_ref[...]) * (1.0 + _rows2d(scale_ref)) + _rows2d(shift_ref)
    u2b = u2.astype(BF16)
    u2_ref[...] = u2b.reshape(u2_ref.shape)
    logits_t = _dot_nt(wr_ref[...], u2b)
    gates_t, picks = _route(logits_t[:N_EXPERTS], br_ref[...], tm)
    if not sparse:
        gates_t = jnp.concatenate([gates_t, jnp.zeros((LANES - N_EXPERTS, tm), F32)], axis=0)
        gates_ref[...] = gates_t.T.reshape(gates_ref.shape)
        return

    @pl.when((pl.program_id(0) == 0) & (pl.program_id(1) == 0))
    def _():
        carry_scr[...] = jnp.zeros_like(carry_scr)

    up_ref[...] = _pack_bf16_pairs(u2).reshape(up_ref.shape)
    hit_all = picks[0][0]
    for hit, _ in picks[1:]:
        hit_all = hit_all | hit
    hits = jnp.where(hit_all, 1.0, 0.0).astype(BF16)
    before = _dot(hits, tri_ref[...]) + jnp.concatenate([carry_scr[...]] * (tm // LANES), axis=1)
    eids, gks, ranks = [], [], []
    for hit, first in picks:
        eids.append(first)
        gks.append(jnp.sum(jnp.where(hit, gates_t, 0.0), axis=0, keepdims=True))
        ranks.append(jnp.sum(jnp.where(hit, before, 0.0), axis=0, keepdims=True))
    pick_row = lax.broadcasted_iota(I32, (TOP_K, tm), 0)

    def stack(rows):
        out = jnp.zeros((TOP_K, tm), F32)
        for r, v in enumerate(rows):
            out = jnp.where(pick_row == r, v, out)
        return out

    eid_ref[...] = stack(eids).astype(I32).reshape(eid_ref.shape)
    gk_ref[...] = stack(gks).reshape(gk_ref.shape)
    rank_ref[...] = stack(ranks).astype(I32).reshape(rank_ref.shape)
    carry_scr[...] += _dot(hits, jnp.ones((tm, LANES), BF16))
    cnt_ref[...] = carry_scr[...]


def _finish(x3, pooled, ynsa, gm, mods, wts, *, tm, sparse):
    g, r, d = x3.shape
    nt = r // tm
    g1, shift2, scale2 = mods
    per_row = g1.ndim == 2
    tok = lambda w: pl.BlockSpec((1, tm, w), lambda b, j: (b, j, 0))
    tok_t = lambda rows: pl.BlockSpec((1, rows, tm), lambda b, j: (b, 0, j))
    if per_row:
        mod_spec = lambda col: pl.BlockSpec((tm, d), lambda b, j, col=col: (b * nt + j, col))
    else:
        mod_spec = lambda col: pl.BlockSpec((1, 1, d), lambda b, j, col=col: (b, 0, col))
    const = lambda a: pl.BlockSpec(a.shape, lambda b, j: (0,) * a.ndim)
    out_shape = [jax.ShapeDtypeStruct((g, r, d), F32), jax.ShapeDtypeStruct((g, r, d), BF16)]
    out_specs = [tok(d), tok(d)]
    scratch = []
    if sparse:
        tri = (jnp.arange(tm)[:, None] < jnp.arange(tm)[None, :]).astype(BF16)
        wts = tuple(wts) + (tri,)
        out_shape += [jax.ShapeDtypeStruct((g, r, d // 2), jnp.uint32),
                      jax.ShapeDtypeStruct((g, TOP_K, r), I32), jax.ShapeDtypeStruct((g, TOP_K, r), F32),
                      jax.ShapeDtypeStruct((g, TOP_K, r), I32), jax.ShapeDtypeStruct((N_EXPERTS, LANES), F32)]
        out_specs += [tok(d // 2), tok_t(TOP_K), tok_t(TOP_K), tok_t(TOP_K),
                      pl.BlockSpec((N_EXPERTS, LANES), lambda b, j: (0, 0))]
        scratch.append(pltpu.VMEM((N_EXPERTS, LANES), F32))
    else:
        out_shape.append(jax.ShapeDtypeStruct((g, r, LANES), F32))
        out_specs.append(tok(LANES))
    return pl.pallas_call(
        functools.partial(_finish_kernel, tm=tm, d=d, sparse=sparse),
        out_shape=out_shape,
        grid=(g, nt),
        in_specs=[tok(d), tok(POOL_W), tok(QPAD), tok(2 * d), mod_spec(2), mod_spec(3), mod_spec(4)]
        + [const(w) for w in wts],
        out_specs=out_specs,
        scratch_shapes=scratch,
        compiler_params=_cparams(("arbitrary", "arbitrary")),
        name="finish_route" if sparse else "finish",
    )(x3, pooled, ynsa, gm, g1, shift2, scale2, *wts)


def _moe_kernel(u_ref, gates_ref, x1_ref, g2_ref, nf_ref, wg_ref, wu_ref, wd_ref, sg_ref, su_ref, sd_ref,
                y_ref, acc_ref, *, tm, d, eps):
    e = pl.program_id(2)
    u = u_ref[...].reshape(tm, d)

    @pl.when(e == 0)
    def _():
        hs = _silu(_dot(u, sg_ref[...])) * _dot(u, su_ref[...])
        acc_ref[...] = _dot(hs.astype(BF16), sd_ref[...])

    gates = gates_ref[...].reshape(tm, LANES)
    lane = lax.broadcasted_iota(I32, (1, LANES), 1)
    hidden = []
    for j in range(eps):
        h = _silu(_dot(u, wg_ref[j].astype(BF16))) * _dot(u, wu_ref[j].astype(BF16))
        gate = jnp.sum(jnp.where(lane == e * eps + j, gates, 0.0), axis=1, keepdims=True)
        hidden.append((h * gate).astype(BF16))
    f = wd_ref.shape[1]
    acc_ref[...] += _dot(jnp.concatenate(hidden, axis=1), wd_ref[...].reshape(eps * f, d).astype(BF16))

    @pl.when(e == pl.num_programs(2) - 1)
    def _():
        x2 = x1_ref[...].reshape(tm, d) + _rows2d(g2_ref) * acc_ref[...]
        y_ref[...] = _rmsnorm(x2, nf_ref[...]).reshape(y_ref.shape)


def _moe(u2, gates, x1, g2, normf, w_gate, w_up, w_down, sg, su, sd, *, tm):
    g, r, d = x1.shape
    nt = r // tm
    ne, _, f = w_gate.shape
    per_row = g2.ndim == 2
    tok = lambda w: pl.BlockSpec((1, tm, w), lambda b, j, e: (b, j, 0))
    if per_row:
        g2_spec = pl.BlockSpec((tm, d), lambda b, j, e: (b * nt + j, 5))
    else:
        g2_spec = pl.BlockSpec((1, 1, d), lambda b, j, e: (b, 0, 5))
    once = pl.Buffered(buffer_count=1)
    const = lambda a: pl.BlockSpec(a.shape, lambda b, j, e: (0,) * a.ndim, pipeline_mode=once)
    eps = 4
    return pl.pallas_call(
        functools.partial(_moe_kernel, tm=tm, d=d, eps=eps),
        out_shape=jax.ShapeDtypeStruct((g, r, d), F32),
        grid=(g, nt, ne // eps),
        in_specs=[tok(d), tok(LANES),
                  pl.BlockSpec((1, tm, d), lambda b, j, e: (b, j, 0), pipeline_mode=once),
                  g2_spec, const(normf),
                  pl.BlockSpec((eps, d, f), lambda b, j, e: (e, 0, 0)),
                  pl.BlockSpec((eps, d, f), lambda b, j, e: (e, 0, 0)),
                  pl.BlockSpec((eps, f, d), lambda b, j, e: (e, 0, 0)),
                  const(sg), const(su), const(sd)],
        out_specs=tok(d),
        scratch_shapes=[pltpu.VMEM((tm, d), F32)],
        compiler_params=_cparams(("arbitrary", "arbitrary", "arbitrary")),
        name="moe",
    )(u2, gates, x1, g2, normf, w_gate, w_up, w_down, sg, su, sd)


SC_WINDOW = 128
SC_ROW_WORDS = 256
MOE_ROWS = 512


def _sc_mesh():
    return plsc.VectorSubcoreMesh(core_axis_name="c", subcore_axis_name="s")


def _sc_scatter_rows(src, dst_idx, n_dst):
    n, w = src.shape
    nk = dst_idx.shape[0]

    @pl.kernel(out_type=jax.ShapeDtypeStruct((n_dst, w), src.dtype), mesh=_sc_mesh(), scratch_types=[])
    def scatter(src_hbm, idx_hbm, dst_hbm):
        def body(rows_vmem, idx_vmem):
            pltpu.sync_copy(rows_vmem, dst_hbm.at[idx_vmem.at[0]])

        pltpu.emit_pipeline(
            body,
            grid=(nk, n // SC_WINDOW),
            in_specs=[pl.BlockSpec((SC_WINDOW, w), index_map=lambda k, i: (i, 0)),
                      pl.BlockSpec((1, SC_WINDOW), index_map=lambda k, i: (k, i))],
            out_specs=[],
            core_axis_name=("c", "s"),
            dimension_semantics=(pltpu.PARALLEL, pltpu.PARALLEL),
        )(src_hbm, idx_hbm)

    return scatter(src, dst_idx)


def _sc_gather_rows(src, idx):
    n, w = idx.shape[0], src.shape[1]

    @pl.kernel(out_type=jax.ShapeDtypeStruct((n, w), src.dtype), mesh=_sc_mesh(), scratch_types=[])
    def gather(src_hbm, idx_hbm, out_hbm):
        def body(idx_vmem, out_vmem):
            pltpu.sync_copy(src_hbm.at[idx_vmem.at[0]], out_vmem)

        pltpu.emit_pipeline(
            body,
            grid=(n // SC_WINDOW,),
            in_specs=[pl.BlockSpec((1, SC_WINDOW), index_map=lambda i: (0, i))],
            out_specs=[pl.BlockSpec((SC_WINDOW, w), index_map=lambda i: (i, 0))],
            core_axis_name=("c", "s"),
            dimension_semantics=(pltpu.PARALLEL,),
        )(idx_hbm, out_hbm)

    return gather(src, idx.reshape(1, n))


def _expert_rows_kernel(te_ref, nt_ref, x_ref, wg_ref, wu_ref, wd_ref, y_ref):
    i = pl.program_id(0)

    @pl.when(i < nt_ref[0])
    def _():
        x = _unpack_bf16_pairs(x_ref[...]).astype(BF16)
        h = _silu(_dot(x, wg_ref[...].astype(BF16))) * _dot(x, wu_ref[...].astype(BF16))
        y_ref[...] = _pack_bf16_pairs(_dot(h.astype(BF16), wd_ref[...].astype(BF16)))


def _expert_rows(tile_expert, n_tiles, x_sorted, w_gate, w_up, w_down):
    p, half = x_sorted.shape
    ne, d, f = w_gate.shape
    wspec = lambda a, b: pl.BlockSpec((None, a, b), lambda i, te, nt: (te[i], 0, 0))
    grid_spec = pltpu.PrefetchScalarGridSpec(
        num_scalar_prefetch=2,
        grid=(p // MOE_ROWS,),
        in_specs=[pl.BlockSpec((MOE_ROWS, half), lambda i, te, nt: (i, 0)), wspec(d, f), wspec(d, f), wspec(f, d)],
        out_specs=pl.BlockSpec((MOE_ROWS, half), lambda i, te, nt: (i, 0)),
    )
    return pl.pallas_call(
        _expert_rows_kernel,
        out_shape=jax.ShapeDtypeStruct((p, half), jnp.uint32),
        grid_spec=grid_spec,
        compiler_params=_cparams(("arbitrary",)),
        name="expert_rows",
    )(tile_expert, n_tiles, x_sorted, w_gate, w_up, w_down)


def _combine_kernel(yg_ref, gk_ref, u_ref, x1_ref, g2_ref, nf_ref, sg_ref, su_ref, sd_ref, y_ref, *, tm, d):
    u = u_ref[...].reshape(tm, d)
    hs = _silu(_dot(u, sg_ref[...])) * _dot(u, su_ref[...])
    acc = _dot(hs.astype(BF16), sd_ref[...])
    gk = gk_ref[...].reshape(tm, LANES)
    lane = lax.broadcasted_iota(I32, (1, LANES), 1)
    half = d // 2
    for k in range(TOP_K):
        gate = jnp.sum(jnp.where(lane == k, gk, 0.0), axis=1, keepdims=True)
        acc = acc + gate * _unpack_bf16_pairs(yg_ref[0, :, k * half:(k + 1) * half])
    x2 = x1_ref[...].reshape(tm, d) + _rows2d(g2_ref) * acc
    y_ref[...] = _rmsnorm(x2, nf_ref[...]).reshape(y_ref.shape)


def _combine(yg, gk, u2, x1, g2, normf, sg, su, sd, *, tm):
    g, r, d = x1.shape
    tok = lambda w: pl.BlockSpec((1, tm, w), lambda b, j: (b, j, 0))
    const = lambda a: pl.BlockSpec(a.shape, lambda b, j: (0,) * a.ndim)
    return pl.pallas_call(
        functools.partial(_combine_kernel, tm=tm, d=d),
        out_shape=jax.ShapeDtypeStruct((g, r, d), F32),
        grid=(g, r // tm),
        in_specs=[tok(TOP_K * d // 2), tok(LANES), tok(d), tok(d),
                  pl.BlockSpec((1, 1, d), lambda b, j: (b, 0, 5)), const(normf), const(sg), const(su), const(sd)],
        out_specs=tok(d),
        compiler_params=_cparams(("arbitrary", "arbitrary")),
        name="moe_combine",
    )(yg, gk, u2, x1, g2, normf, sg, su, sd)


def _moe_sorted(u2, u2p, eid_t, gk_t, rank_t, counts, x1, g2, normf, w_gate, w_up, w_down, sg, su, sd):
    g, r, d = x1.shape
    n = g * r
    ne = w_gate.shape[0]
    half = d // 2
    split = half // SC_ROW_WORDS
    cnt = counts[:, 0].astype(I32)
    padded = -(-cnt // MOE_ROWS) * MOE_ROWS
    seg_end = jnp.cumsum(padded)
    seg_start = seg_end - padded
    p_rows = n * TOP_K + ne * MOE_ROWS
    eid = eid_t.transpose(1, 0, 2).reshape(TOP_K, n)
    pos = seg_start[eid] + rank_t.transpose(1, 0, 2).reshape(TOP_K, n)
    tile_expert = jnp.minimum(
        jnp.searchsorted(seg_end, jnp.arange(p_rows // MOE_ROWS, dtype=I32) * MOE_ROWS, side="right"),
        ne - 1).astype(I32)
    n_tiles = (seg_end[-1:] // MOE_ROWS).astype(I32)
    part = jnp.arange(split, dtype=I32)
    scat_idx = (pos[:, :, None] * split + part).reshape(TOP_K, n * split)
    gath_idx = (pos.T[:, :, None] * split + part).reshape(n * TOP_K * split)
    x_sorted = _sc_scatter_rows(u2p.reshape(n * split, SC_ROW_WORDS), scat_idx, p_rows * split)
    y_sorted = _expert_rows(tile_expert, n_tiles, x_sorted.reshape(p_rows, half), w_gate, w_up, w_down)
    yg = _sc_gather_rows(y_sorted.reshape(p_rows * split, SC_ROW_WORDS), gath_idx)
    gk = jnp.pad(gk_t.transpose(0, 2, 1), ((0, 0), (0, 0), (0, LANES - TOP_K)))
    return _combine(yg.reshape(g, r, TOP_K * half), gk, u2, x1, g2, normf, sg, su, sd, tm=512)


def _kv_slot_mask():
    return (jnp.arange(N_HEADS)[:, None] // GROUP == jnp.arange(N_KV)[None, :]).astype(F32)


def _prep_w_in(w_in, d):
    q0 = POOL_W
    kv0 = q0 + N_HEADS * HEAD_DIM
    gn0 = kv0 + 6 * KVW
    gm0 = gn0 + 3 * N_HEADS
    wq = w_in[:, q0:kv0].reshape(d, N_HEADS, 1, HEAD_DIM) * (HEAD_DIM ** -0.5)
    wq = (wq * _kv_slot_mask()[None, :, :, None]).reshape(d, QPAD)
    wgn = jnp.pad(w_in[:, gn0:gm0], ((0, 0), (0, LANES - 3 * N_HEADS)))
    return jnp.concatenate([w_in[:, :q0], wq, w_in[:, kv0:gn0], wgn, w_in[:, gm0:]], axis=1).astype(BF16)


def _prep_w_nsa_out(w, d):
    w = w.reshape(N_HEADS, 1, HEAD_DIM, d) * _kv_slot_mask()[:, :, None, None]
    return w.reshape(QPAD, d).astype(BF16)


def _block_diag(w_lin):
    g, c, _ = w_lin.shape
    eye = jnp.eye(g, dtype=F32)
    return (w_lin[:, :, None, :] * eye[:, None, :, None]).reshape(g * c, g * c).astype(BF16)


def kernel(x_prompt, x_sample, cache_kc, cache_vc, cache_ks, cache_vs, state_kw, state_vw, state_pool,
           page_table, c_prompt, c_sample, norm1_g, norm2_g, normf_g, w_ada, b_ada, w_in, w_pool_lin,
           pool_scale, w_cmp_k, w_cmp_v, w_pool_out, w_nsa_out, w_o, w_router, b_router, w_gate, w_up,
           w_down, ws_gate, ws_up, ws_down):
    depth = w_in.shape[0]
    assert depth == 1, "single-layer stack"
    bsz, seq, d = x_prompt.shape
    n_seq, ns, _ = x_sample.shape
    wbuf = state_kw.shape[2]
    lyr = 0

    c_all = jnp.concatenate([c_prompt, c_sample], axis=0)
    rows = c_all.shape[0]
    rows_p = -(-rows // 8) * 8
    mod = _adaln(jnp.pad(c_all, ((0, rows_p - rows), (0, 0))), w_ada[lyr], b_ada[lyr])
    mod_p = mod[:bsz].reshape(bsz, 1, 6 * d)
    mod_s = jnp.repeat(mod[bsz:bsz + n_seq], ns, axis=0)

    w2 = _prep_w_in(w_in[lyr], d)
    g1n = norm1_g[lyr].reshape(1, d)
    wk = w_cmp_k[lyr].reshape(CMP_BLOCK, KVW)
    wv = w_cmp_v[lyr].reshape(CMP_BLOCK, KVW)
    fin_w = (_block_diag(w_pool_lin[lyr]), pool_scale[lyr].reshape(1, POOL_W), w_pool_out[lyr].astype(BF16),
             _prep_w_nsa_out(w_nsa_out[lyr], d), w_o[lyr].astype(BF16), norm2_g[lyr].reshape(1, d),
             jnp.pad(w_router[lyr].T, ((0, LANES - N_EXPERTS), (0, 0))).astype(BF16),
             b_router[lyr].reshape(N_EXPERTS, 1))
    moe_w = (w_gate[lyr], w_up[lyr], w_down[lyr], ws_gate[lyr].astype(BF16), ws_up[lyr].astype(BF16),
             ws_down[lyr].astype(BF16))
    nf = normf_g.reshape(1, d)

    tm_p = 512
    (vp, kc, vc, ks, vs, kw, vw, gm, ksb, kwb, vst, vwt, qt, gst, pooled) = _in_proj(
        x_prompt, mod_p, mod_p, g1n, w2, tm=tm_p, prompt=True)
    kcmp, vcmpt = _compress(kc, vc, wk, wv)
    ynsa = _nsa_prompt(qt, gst, kcmp, vcmpt, ksb, vst, kwb, vwt)
    x1, u2, u2p, eid_t, gk_t, rank_t, counts = _finish(
        x_prompt, pooled, ynsa, gm, (mod_p, mod_p, mod_p), fin_w, tm=tm_p, sparse=True)
    y_prompt = _moe_sorted(u2, u2p, eid_t, gk_t, rank_t, counts, x1, mod_p, nf, *moe_w)

    n_tok = n_seq * ns
    xs3 = x_sample.reshape(1, n_tok, d)
    tm_s = 128
    (vp_s, kc_s, vc_s, ks_s, vs_s, kw_s, vw_s, gm_s, q_s, gs_s) = _in_proj(
        xs3, mod_s, mod_s, g1n, w2, tm=tm_s, prompt=False)
    two = lambda a: a.reshape(n_tok, a.shape[-1])
    q_rows = q_s.reshape(n_tok * N_HEADS, LANES)
    gate_rows = two(gs_s)[:, :3 * N_HEADS].reshape(n_tok, 3, N_HEADS).transpose(0, 2, 1)
    gate_rows = jnp.pad(gate_rows.reshape(n_tok * N_HEADS, 3), ((0, 0), (0, LANES - 3)))
    n_pool = cache_kc.shape[1]
    page = cache_kc.shape[2]
    rows_minor = lambda a: jnp.transpose(a, (0, 2, 3, 1)).reshape(a.shape[0], KVW, a.shape[1])
    caches = [rows_minor(c[lyr]) for c in (cache_kc, cache_vc, cache_ks, cache_vs)]
    o_rows, pooled_s = _nsa_sample(
        page_table, q_rows, gate_rows, [two(a) for a in (kc_s, vc_s, ks_s, vs_s, kw_s, vw_s)], two(vp_s),
        rows_minor(state_kw[lyr]), rows_minor(state_vw[lyr]), state_pool[lyr], wk, wv, caches)
    ynsa_s = o_rows.reshape(1, n_tok, QPAD)
    x1_s, u2_s, gates_s = _finish(xs3, pooled_s.reshape(1, n_tok, POOL_W), ynsa_s, gm_s,
                                  (mod_s, mod_s, mod_s), fin_w, tm=tm_s, sparse=False)
    y_sample = _moe(u2_s, gates_s, x1_s, mod_s, nf, *moe_w, tm=n_tok).reshape(n_seq, ns, d)

    kvp = lambda a: a.reshape(1, bsz, seq, N_KV, HEAD_DIM)
    tailp = lambda a: jnp.pad(a, ((0, 0), (wbuf, 0), (0, 0)))[:, -wbuf:].reshape(1, bsz, wbuf, N_KV, HEAD_DIM)
    kvs = lambda a: a.reshape(1, n_seq, ns, N_KV, HEAD_DIM)
    wins = lambda st, new: jnp.concatenate(
        [st[lyr], new.reshape(n_seq, ns, N_KV, HEAD_DIM)], axis=1)[None, :, -wbuf:]
    pool_p = vp[:, -POOL_BUF:][None]
    pool_s = jnp.concatenate([state_pool[lyr], vp_s.reshape(n_seq, ns, POOL_W)], axis=1)[None, :, -POOL_BUF:]
    return (y_prompt, y_sample, kvp(kc), kvp(vc), kvp(ks), kvp(vs), tailp(kw), tailp(vw), pool_p,
            kvs(kc_s), kvs(vc_s), kvs(ks_s), kvs(vs_s), wins(state_kw, kw_s), wins(state_vw, vw_s), pool_s)
```

```python
import functools

import jax
import jax.numpy as jnp
from jax import lax
from jax.experimental import pallas as pl
from jax.experimental.pallas import tpu as pltpu
from jax.experimental.pallas import tpu_sc as plsc

F32 = jnp.float32
BF16 = jnp.bfloat16
I32 = jnp.int32

POOL_WINDOWS = (2, 4, 8, 16)
POOL_GW = 64
POOL_W = 256
POOL_BUF = 15
N_HEADS = 8
HEAD_DIM = 64
N_KV = 2
GROUP = N_HEADS // N_KV
CMP_STRIDE = 16
CMP_BLOCK = 32
SEL_BLOCK = 64
TOP_BLOCKS = 16
WINDOW = 512
Q_BLOCK = 128
FORCE_SCORE = 1e4
N_EXPERTS = 64
N_EGROUPS = 8
EXPERTS_PER_GROUP = N_EXPERTS // N_EGROUPS
TOPK_GROUPS = 4
TOP_K = 8
ROUTED_SCALE = 2.5
EPS = 1e-6
NEG = -1e30
SLOPES = tuple(2.0 ** (-8.0 * (h + 1.0) / N_HEADS) for h in range(N_HEADS))

LANES = 128
QPAD = N_HEADS * LANES
KVW = N_KV * HEAD_DIM
VMEM_LIMIT = 56 * 1024 * 1024


def _cparams(sem):
    return pltpu.CompilerParams(dimension_semantics=sem, vmem_limit_bytes=VMEM_LIMIT)


def _dot(a, b):
    return jnp.dot(a, b, preferred_element_type=F32)


def _dot_nt(a, b):
    return lax.dot_general(a, b, (((1,), (1,)), ((), ())), preferred_element_type=F32)


def _dot_exact(a, b):
    return jnp.dot(a, b, preferred_element_type=F32, precision=lax.Precision.HIGHEST)


def _rows2d(ref):
    v = ref[...]
    return v.reshape(v.shape[-2], v.shape[-1])


def _rmsnorm(x, g):
    return x * lax.rsqrt(jnp.mean(x * x, axis=-1, keepdims=True) + EPS) * g


def _silu(x):
    return x * jax.nn.sigmoid(x)


def _adaln_kernel(c_ref, w_ref, b_ref, o_ref):
    s = _silu(c_ref[...]).astype(BF16)
    o_ref[...] = _dot(s, w_ref[...].astype(BF16)) + b_ref[...]


def _adaln(c, w_ada, b_ada):
    rows, d = c.shape
    n = w_ada.shape[1]
    tn = 512
    return pl.pallas_call(
        _adaln_kernel,
        out_shape=jax.ShapeDtypeStruct((rows, n), F32),
        grid=(n // tn,),
        in_specs=[pl.BlockSpec((rows, d), lambda j: (0, 0)),
                  pl.BlockSpec((d, tn), lambda j: (0, j)),
                  pl.BlockSpec((1, tn), lambda j: (0, j))],
        out_specs=pl.BlockSpec((rows, tn), lambda j: (0, j)),
        compiler_params=_cparams(("arbitrary",)),
        name="adaln",
    )(c, w_ada, b_ada.reshape(1, n))


_C_VP = 0
_C_Q = _C_VP + POOL_W
_C_KV = _C_Q + QPAD
_C_GN = _C_KV + 6 * KVW
_C_GM = _C_GN + LANES


def _pool_window_sums(ext, tm):
    s2 = ext + pltpu.roll(ext, 1, 0)
    s4 = s2 + pltpu.roll(s2, 2, 0)
    s8 = s4 + pltpu.roll(s4, 4, 0)
    s16 = s8 + pltpu.roll(s8, 8, 0)
    grp = lax.broadcasted_iota(I32, (1, POOL_W), 1) // POOL_GW
    pick = jnp.where(grp == 0, s2, jnp.where(grp == 1, s4, jnp.where(grp == 2, s8, s16)))
    return pick[16:16 + tm]


def _in_proj_kernel(x_ref, shift_ref, scale_ref, g_ref, w_ref,
                    vp_ref, kc_ref, vc_ref, ks_ref, vs_ref, kw_ref, vw_ref, gm_ref, *rest, tm, d, prompt):
    x = x_ref[...].reshape(tm, d)
    u = _rmsnorm(x, g_ref[...]) * (1.0 + _rows2d(scale_ref)) + _rows2d(shift_ref)
    ub = u.astype(BF16)

    def proj(c0, n):
        return _dot(ub, w_ref[:, c0:c0 + n])

    vp = proj(_C_VP, POOL_W)
    vp_ref[...] = vp.reshape(vp_ref.shape)
    kv = []
    for n, o32 in enumerate((kc_ref, vc_ref, ks_ref, vs_ref, kw_ref, vw_ref)):
        v = proj(_C_KV + n * KVW, KVW)
        o32[...] = v.reshape(o32.shape)
        kv.append(v)
    gm_ref[...] = jax.nn.sigmoid(proj(_C_GM, 2 * d)).reshape(gm_ref.shape)
    gs = jax.nn.sigmoid(proj(_C_GN, LANES))

    if not prompt:
        q_ref, gs_ref = rest
        q_ref[...] = proj(_C_Q, QPAD).astype(BF16).reshape(q_ref.shape)
        gs_ref[...] = gs.reshape(gs_ref.shape)
    else:
        ksb_ref, kwb_ref, vst_ref, vwt_ref, qt_ref, gst_ref, pooled_ref, halo_ref = rest
        ksb_ref[...] = kv[2].astype(BF16).reshape(ksb_ref.shape)
        kwb_ref[...] = kv[4].astype(BF16).reshape(kwb_ref.shape)
        vst_ref[...] = kv[3].T.astype(BF16).reshape(vst_ref.shape)
        vwt_ref[...] = kv[5].T.astype(BF16).reshape(vwt_ref.shape)
        gst_ref[...] = gs.T.reshape(gst_ref.shape)
        for h in range(N_HEADS):
            qt_ref[0, h] = proj(_C_Q + h * LANES, LANES).T.astype(BF16)
        j = pl.program_id(1)

        @pl.when(j == 0)
        def _():
            halo_ref[...] = jnp.zeros_like(halo_ref)

        ext = jnp.concatenate([halo_ref[...], vp], axis=0)
        sums = _pool_window_sums(ext, tm)
        pos = j * tm + lax.broadcasted_iota(I32, (tm, 1), 0)
        wcol = 2 << (lax.broadcasted_iota(I32, (1, POOL_W), 1) // POOL_GW)
        cnt = jnp.minimum(pos + 1, wcol).astype(F32)
        pooled_ref[...] = (sums / cnt - vp).astype(BF16).reshape(pooled_ref.shape)
        halo_ref[...] = vp[tm - 16:tm]


def _in_proj(x3, shift, scale, g1, w2, *, tm, prompt):
    g, r, d = x3.shape
    nt = r // tm
    per_row = shift.ndim == 2

    def tok(width, dtype):
        return (jax.ShapeDtypeStruct((g, r, width), dtype),
                pl.BlockSpec((1, tm, width), lambda b, j: (b, j, 0)))

    def tok_t(rows, dtype):
        return (jax.ShapeDtypeStruct((g, rows, r), dtype),
                pl.BlockSpec((1, rows, tm), lambda b, j: (b, 0, j)))

    outs = [tok(POOL_W, F32)] + [tok(KVW, F32)] * 6 + [tok(2 * d, F32)]
    scratch = []
    if prompt:
        outs += [tok(KVW, BF16), tok(KVW, BF16), tok_t(KVW, BF16), tok_t(KVW, BF16)]
        outs.append((jax.ShapeDtypeStruct((g, N_HEADS, LANES, r), BF16),
                     pl.BlockSpec((1, N_HEADS, LANES, tm), lambda b, j: (b, 0, 0, j))))
        outs += [tok_t(LANES, F32), tok(POOL_W, BF16)]
        scratch.append(pltpu.VMEM((16, POOL_W), F32))
    else:
        outs += [tok(QPAD, BF16), tok(LANES, F32)]
    if per_row:
        mod_spec = lambda col: pl.BlockSpec((tm, d), lambda b, j, col=col: (b * nt + j, col))
    else:
        mod_spec = lambda col: pl.BlockSpec((1, 1, d), lambda b, j, col=col: (b, 0, col))
    kern = functools.partial(_in_proj_kernel, tm=tm, d=d, prompt=prompt)
    return pl.pallas_call(
        kern,
        out_shape=[o[0] for o in outs],
        grid=(g, nt),
        in_specs=[pl.BlockSpec((1, tm, d), lambda b, j: (b, j, 0)),
                  mod_spec(0), mod_spec(1),
                  pl.BlockSpec((1, d), lambda b, j: (0, 0)),
                  pl.BlockSpec(w2.shape, lambda b, j: (0, 0))],
        out_specs=[o[1] for o in outs],
        scratch_shapes=scratch,
        compiler_params=_cparams(("arbitrary", "arbitrary")),
        name="in_proj_prompt" if prompt else "in_proj_sample",
    )(x3, shift, scale, g1, w2)


def _compress_kernel(kc_ref, vc_ref, wk_ref, wv_ref, okc_ref, ovc_ref, sh_ref, *, nc):
    last = lax.broadcasted_iota(I32, (nc, 1), 0) == nc - 1
    for src, w_ref, dst in ((kc_ref, wk_ref, okc_ref), (vc_ref, wv_ref, ovc_ref)):
        head = jnp.zeros((nc, KVW), F32)
        tail = jnp.zeros((nc, KVW), F32)
        for r in range(CMP_STRIDE):
            rows = src[pl.ds(r, nc, stride=CMP_STRIDE), :]
            head = head + rows * w_ref[r:r + 1, :]
            tail = tail + rows * w_ref[CMP_STRIDE + r:CMP_STRIDE + r + 1, :]
        sh_ref[0:nc, :] = tail
        sh_ref[nc:nc + 8, :] = jnp.zeros((8, KVW), F32)
        out = jnp.where(last, 0.0, head + sh_ref[1:nc + 1, :])
        dst[...] = (out if dst is okc_ref else out.T).astype(BF16)


def _compress(kc, vc, wk, wv):
    b, s, _ = kc.shape
    nc = s // CMP_STRIDE
    big = pl.BlockSpec((None, s, KVW), lambda i: (i, 0, 0))
    wsp = pl.BlockSpec((CMP_BLOCK, KVW), lambda i: (0, 0))
    return pl.pallas_call(
        functools.partial(_compress_kernel, nc=nc),
        out_shape=[jax.ShapeDtypeStruct((b, nc, KVW), BF16), jax.ShapeDtypeStruct((b, KVW, nc), BF16)],
        grid=(b,),
        in_specs=[big, big, wsp, wsp],
        out_specs=[pl.BlockSpec((None, nc, KVW), lambda i: (i, 0, 0)),
                   pl.BlockSpec((None, KVW, nc), lambda i: (i, 0, 0))],
        scratch_shapes=[pltpu.VMEM((nc + 8, KVW), F32)],
        compiler_params=_cparams(("arbitrary",)),
        name="compress",
    )(kc, vc, wk, wv)


def _topk_mask(vals, blk_f, n_top, axis=1):
    sel = jnp.zeros(vals.shape, F32)
    big = float(vals.shape[axis])
    for _ in range(n_top):
        mx = jnp.max(vals, axis=axis, keepdims=True)
        first = jnp.min(jnp.where(vals == mx, blk_f, big), axis=axis, keepdims=True)
        hit = blk_f == first
        sel = jnp.where(hit, 1.0, sel)
        vals = jnp.where(hit, -jnp.inf, vals)
    return sel


def _topk_mask_by_rank(vals, blk, n_valid, n_top):
    rank = jnp.zeros(vals.shape, F32)
    for j in range(n_valid):
        vj = vals[:, j:j + 1]
        beats = (vj > vals) | ((vj == vals) & (blk > j))
        rank = rank + jnp.where(beats, 1.0, 0.0)
    return jnp.where(rank < float(n_top), 1.0, 0.0)


def _pos_features(pos):
    hi = (pos // SEL_BLOCK).astype(F32)[:, None]
    lo = (pos % SEL_BLOCK).astype(F32)[:, None]
    return jnp.concatenate([hi, lo, jnp.zeros((pos.shape[0], LANES - 2), F32)], axis=1).astype(BF16)


def _importance_matrix(nc, nsel):
    j = jnp.arange(nc)[:, None]
    s = jnp.arange(nsel)[None, :]
    r = SEL_BLOCK // CMP_STRIDE
    a = (j >= r * s) & (j <= r * s + r - 1)
    b = (j + 1 >= r * s) & (j + 1 <= r * s + r - 1)
    return a.astype(F32) + b.astype(F32)


def _nsa_prompt_kernel(qt_ref, gst_ref, kc_ref, vct_ref, ks_ref, vst_ref, kw_ref, vwt_ref,
                       cfeat_ref, wfeat_ref, qfeat_ref, slope_ref,
                       y_ref, qk_scr, m_scr, l_scr, acc_scr, o_scr, sel_scr, imp_scr, flag_scr, *, seq, tk, wl):
    i = pl.program_id(1)
    q0 = i * Q_BLOCK
    nq = Q_BLOCK
    gq = GROUP * nq
    nc = kc_ref.shape[0]
    nsel = seq // SEL_BLOCK
    n_top = min(TOP_BLOCKS, nsel)
    blk_per_tile = tk // SEL_BLOCK
    qpos = q0 + lax.broadcasted_iota(I32, (1, nq), 1)
    gst = gst_ref[...]

    crow = lax.broadcasted_iota(I32, (nc, nq), 0)
    cend = crow * CMP_STRIDE + (CMP_BLOCK - 1)
    mask_c = qpos >= cend
    kc = jnp.concatenate([kc_ref[...], cfeat_ref[...]], axis=1)
    vct = vct_ref[...]
    blk = lax.broadcasted_iota(I32, (nsel, nq), 0)
    blk_f = blk.astype(F32)
    cur = qpos // SEL_BLOCK
    forced = (blk == 0) | (blk == cur) | (blk == cur - 1)
    visible = blk * SEL_BLOCK <= qpos
    ws = pl.multiple_of(jnp.maximum(q0 - WINDOW, 0), Q_BLOCK)
    wpos = ws + lax.broadcasted_iota(I32, (wl, nq), 0)
    valid_w = lax.bitcast_convert_type(qpos - wpos, jnp.uint32) < WINDOW
    n_tiles = (q0 + nq + tk - 1) // tk
    half_rows = lax.broadcasted_iota(I32, (KVW, nq), 0) // HEAD_DIM
    tile_pos = lax.broadcasted_iota(I32, (SEL_BLOCK, nq), 0)

    def lanes4(x):
        return jnp.concatenate([x] * GROUP, axis=1)

    def gate_row(branch, k):
        r0 = branch * N_HEADS + k * GROUP
        return jnp.concatenate([gst[r0 + g:r0 + g + 1] for g in range(GROUP)], axis=1)

    mask_c4 = lanes4(mask_c)
    valid_w4 = lanes4(valid_w)
    kwt = jnp.concatenate([kw_ref[pl.ds(ws, wl), :], wfeat_ref[...]], axis=1)
    vwtt = vwt_ref[:, pl.ds(ws, wl)]

    for k in range(N_KV):
        for g in range(GROUP):
            qk_scr[k, 0:LANES, g * nq:(g + 1) * nq] = qt_ref[k * GROUP + g]
        qk_scr[k, LANES:2 * LANES, :] = qfeat_ref[k]
        qk = qk_scr[k]

        s = jnp.where(mask_c4, _dot(kc, qk), NEG)
        e = jnp.where(mask_c4, jnp.exp(s - jnp.max(s, axis=0, keepdims=True)), 0.0)
        l = jnp.sum(e, axis=0, keepdims=True)
        p = e * jnp.where(l > 0.0, 1.0 / l, 0.0)
        o_c = _dot(vct, p.astype(BF16))
        psum = p[:, 0:nq]
        for g in range(1, GROUP):
            psum = psum + p[:, g * nq:(g + 1) * nq]

        a = psum + jnp.where(crow == 0, 0.0, pltpu.roll(psum, 1, 0))
        a = a + pltpu.roll(a, nc - 1, 0)
        imp_scr[...] = a + pltpu.roll(a, nc - 2, 0)
        imp = imp_scr[pl.ds(0, nsel, stride=nc // nsel), :]
        vals = jnp.where(visible, jnp.where(forced, FORCE_SCORE, imp), NEG)
        sel = jnp.where(visible, _topk_mask(vals, blk_f, n_top, axis=0), 0.0)
        sel_scr[k] = jnp.where(sel > 0.5, 0.0, NEG)
        blk_any = jnp.max(sel, axis=1, keepdims=True)
        for t in range(seq // tk):
            hit = (jnp.max(blk_any[t * blk_per_tile:(t + 1) * blk_per_tile, :]) > 0.5).astype(I32)
            flag_scr[t] = hit if k == 0 else flag_scr[t] | hit

        s = jnp.where(valid_w4, _dot(kwt, qk), NEG)
        e = jnp.exp(s - jnp.max(s, axis=0, keepdims=True))
        p = e / jnp.sum(e, axis=0, keepdims=True)
        o_w = _dot(vwtt, p.astype(BF16))
        o_scr[k] = gate_row(0, k) * o_c + gate_row(2, k) * o_w

    m_scr[...] = jnp.full(m_scr.shape, NEG, F32)
    l_scr[...] = jnp.zeros(l_scr.shape, F32)
    acc_scr[...] = jnp.zeros(acc_scr.shape, F32)

    def sel_tile(t):
        k0 = pl.multiple_of(t * tk, tk)
        kt = jnp.concatenate([ks_ref[pl.ds(k0, tk), :], wfeat_ref[0:tk, :]], axis=1)
        vtt = vst_ref[:, pl.ds(k0, tk)]
        causal = [qpos >= k0 + j * SEL_BLOCK + tile_pos for j in range(blk_per_tile)]
        base = (k0 - q0).astype(F32)
        scores = _dot(kt, jnp.concatenate([qk_scr[k] for k in range(N_KV)], axis=1))
        probs, alphas = [], []
        for k in range(N_KV):
            neg = []
            for j in range(blk_per_tile):
                row = sel_scr[k, pl.ds(t * blk_per_tile + j, 1), :]
                neg.append(jnp.where(causal[j], jnp.broadcast_to(row, (SEL_BLOCK, nq)), NEG))
            neg = lanes4(jnp.concatenate(neg, axis=0))
            off = slope_ref[k] * base
            s = scores[:, k * gq:(k + 1) * gq] + neg
            m_old = m_scr[k]
            m_new = jnp.maximum(m_old, jnp.max(s, axis=0, keepdims=True) + off)
            alpha = jnp.exp(m_old - m_new)
            p = jnp.exp(s - (m_new - off))
            l_scr[k] = alpha * l_scr[k] + jnp.sum(p, axis=0, keepdims=True)
            m_scr[k] = m_new
            probs.append(p.astype(BF16))
            alphas.append(alpha)
        pv = _dot(vtt, jnp.concatenate(probs, axis=1))
        for k in range(N_KV):
            acc_scr[k] = acc_scr[k] * alphas[k] + pv[:, k * gq:(k + 1) * gq]

    def sel_body(t, carry):
        pl.when(flag_scr[t] > 0)(functools.partial(sel_tile, t))
        return carry

    lax.fori_loop(0, n_tiles, sel_body, 0)

    for k in range(N_KV):
        o = o_scr[k] + gate_row(1, k) * (acc_scr[k] / l_scr[k])
        for g in range(GROUP):
            h = k * GROUP + g
            oh = jnp.where(half_rows == k, o[:, g * nq:(g + 1) * nq], 0.0)
            y_ref[:, h * LANES:(h + 1) * LANES] = oh.T.astype(BF16)


def _nsa_prompt(qt, gst, kcmp, vcmpt, ksb, vst, kwb, vwt):
    b, _, _, s = qt.shape
    nq = Q_BLOCK
    gq = GROUP * nq
    nc = kcmp.shape[1]
    nsel = s // SEL_BLOCK
    tk = 256
    wl = WINDOW + Q_BLOCK
    assert s % tk == 0 and s >= wl
    assert s // SEL_BLOCK <= 2 * LANES, "position // 64 must stay exact in bf16"
    cfeat = _pos_features(jnp.arange(nc) * CMP_STRIDE + (CMP_BLOCK - 1))
    wfeat = _pos_features(jnp.arange(wl))
    slope_rows = jnp.repeat(jnp.asarray(SLOPES, F32).reshape(N_KV, 1, GROUP), nq, axis=2)
    qfeat = jnp.concatenate([slope_rows * SEL_BLOCK, slope_rows, jnp.zeros((N_KV, LANES - 2, gq), F32)],
                            axis=1).astype(BF16)
    rows = lambda r: pl.BlockSpec((None, r, KVW), lambda bi, i: (bi, 0, 0))
    cols = lambda c: pl.BlockSpec((None, KVW, c), lambda bi, i: (bi, 0, 0))
    const = lambda a: pl.BlockSpec(a.shape, lambda bi, i: (0,) * a.ndim)
    return pl.pallas_call(
        functools.partial(_nsa_prompt_kernel, seq=s, tk=tk, wl=wl),
        out_shape=jax.ShapeDtypeStruct((b, s, QPAD), BF16),
        grid=(b, s // nq),
        in_specs=[pl.BlockSpec((None, N_HEADS, LANES, nq), lambda bi, i: (bi, 0, 0, i)),
                  pl.BlockSpec((None, LANES, nq), lambda bi, i: (bi, 0, i)),
                  rows(nc), cols(nc), rows(s), cols(s), rows(s), cols(s),
                  const(cfeat), const(wfeat), const(qfeat), const(slope_rows)],
        out_specs=pl.BlockSpec((None, nq, QPAD), lambda bi, i: (bi, i, 0)),
        scratch_shapes=[pltpu.VMEM((N_KV, 2 * LANES, gq), BF16),
                        pltpu.VMEM((N_KV, 1, gq), F32),
                        pltpu.VMEM((N_KV, 1, gq), F32),
                        pltpu.VMEM((N_KV, KVW, gq), F32),
                        pltpu.VMEM((N_KV, KVW, gq), F32),
                        pltpu.VMEM((N_KV, nsel, nq), F32),
                        pltpu.VMEM((nc, nq), F32),
                        pltpu.SMEM((s // tk,), I32)],
        compiler_params=_cparams(("arbitrary", "arbitrary")),
        name="nsa_prompt",
    )(qt, gst, kcmp, vcmpt, ksb, vst, kwb, vwt, cfeat, wfeat, qfeat, slope_rows)


def _nsa_sample_kernel(pt_ref, q_ref, gate_ref, kcn_ref, vcn_ref, ksn_ref, vsn_ref, kwn_ref, vwn_ref,
                       vpn_ref, skw_ref, svw_ref, spool_ref, wk_ref, wv_ref, imat_ref, emat_ref,
                       ckc_ref, cvc_ref, cks_ref, cvs_ref,
                       o_ref, pooled_ref, buf, buft, win_scr, tail_scr, vext_scr, sem,
                       *, sb, ns, past, n_pages, page, n_seq, ncv, ncp, nks, wls, nselp, n_sel):
    step = pl.program_id(0)
    nrow = ns * N_HEADS

    def copies(n, slot):
        out = []
        for p in range(n_pages):
            pg = pt_ref[n * n_pages + p]
            for c, cref in enumerate((ckc_ref, cvc_ref, cks_ref, cvs_ref)):
                out.append(pltpu.make_async_copy(cref.at[pg], buft.at[slot, c, :, pl.ds(p * page, page)],
                                                 sem.at[slot]))
        return out

    @pl.when(step == 0)
    def _():
        buf[:, past:, :] = jnp.zeros((2, buf.shape[1] - past, KVW), F32)
        tail_scr[...] = jnp.zeros_like(tail_scr)
        vext_scr[...] = jnp.zeros_like(vext_scr)
        for cp in copies(0, 0):
            cp.start()

    def new_rows_t(ref, r4):
        tail_scr[0:ns, :] = ref[pl.ds(r4, ns), :]
        return tail_scr[...].T

    row = lax.broadcasted_iota(I32, (nrow, 1), 0)
    hrow = row % N_HEADS
    qpos = past + row // N_HEADS
    slope = jnp.exp2(-8.0 * (hrow.astype(F32) + 1.0) / N_HEADS)
    kvrow = hrow // GROUP
    lane = lax.broadcasted_iota(I32, (1, LANES), 1)
    half = (lane // HEAD_DIM) == kvrow
    grow = (row // N_HEADS) * N_KV + kvrow
    row8 = lax.broadcasted_iota(I32, (ns * N_KV, 1), 0)
    qpos8 = past + lax.broadcasted_iota(I32, (ns * N_KV, 1), 0) // N_KV
    blk = lax.broadcasted_iota(I32, (1, nselp), 1)
    blk_f = blk.astype(F32)
    cur = qpos8 // SEL_BLOCK
    forced = (blk == 0) | (blk == cur) | (blk == cur - 1)
    visible = (blk * SEL_BLOCK <= qpos8)
    inrange = blk < n_sel
    cend = lax.broadcasted_iota(I32, (1, ncp), 1) * CMP_STRIDE + (CMP_BLOCK - 1)
    mask_c = qpos >= cend
    bias_c = slope * (cend - qpos).astype(F32)
    kpos = lax.broadcasted_iota(I32, (1, nks), 1)
    causal_s = qpos >= kpos
    bias_s = slope * (kpos - qpos).astype(F32)
    wbuf = wls[0]
    wpos = past - wbuf + lax.broadcasted_iota(I32, (1, wls[1]), 1)
    dw = qpos - wpos
    valid_w = lax.bitcast_convert_type(dw, jnp.uint32) < WINDOW
    bias_w = slope * (wpos - qpos).astype(F32)
    prow = lax.broadcasted_iota(I32, (vext_scr.shape[0], 1), 0)
    wcol = 2 << (lax.broadcasted_iota(I32, (1, POOL_W), 1) // POOL_GW)
    n_top = min(TOP_BLOCKS, n_sel)

    def softmax_rows(s, mask):
        s = jnp.where(mask, s, NEG)
        mx = jnp.max(s, axis=1, keepdims=True)
        e = jnp.where(mask, jnp.exp(s - mx), 0.0)
        l = jnp.sum(e, axis=1, keepdims=True)
        return e * jnp.where(l > 0.0, 1.0 / l, 0.0)

    def seq_body(r):
        n = step * sb + r
        slot = r % 2

        @pl.when(n + 1 < n_seq)
        def _():
            for cp in copies(n + 1, 1 - slot):
                cp.start()

        for cp in copies(n, slot):
            cp.wait()

        r4 = r * ns
        for c, new_ref in enumerate((kcn_ref, vcn_ref)):
            for p in range(n_pages):
                buf[c, p * page:(p + 1) * page, :] = buft[slot, c, :, p * page:(p + 1) * page].T
            buf[c, past:past + ns, :] = new_ref[pl.ds(r4, ns), :]
        for c, new_ref in ((2, ksn_ref), (3, vsn_ref)):
            buft[slot, c, :, past:past + LANES] = new_rows_t(new_ref, r4)

        qall = q_ref[pl.ds(r * nrow, nrow), :]
        gates = gate_ref[pl.ds(r * nrow, nrow), :]

        cmp = []
        for c, w_ref in ((0, wk_ref), (1, wv_ref)):
            span = CMP_STRIDE * ncv
            lo = buf[c, 0:span, :].reshape(ncv, CMP_STRIDE, KVW) * w_ref[0:CMP_STRIDE, :][None]
            hi = (buf[c, CMP_STRIDE:CMP_STRIDE + span, :].reshape(ncv, CMP_STRIDE, KVW)
                  * w_ref[CMP_STRIDE:CMP_BLOCK, :][None])
            acc = jnp.sum(lo + hi, axis=1)
            cmp.append(jnp.concatenate([acc, jnp.zeros((ncp - ncv, KVW), F32)], axis=0).astype(BF16))
        p_c = softmax_rows(_dot_nt(qall, cmp[0]) + bias_c, mask_c)
        o_c = _dot(p_c.astype(BF16), cmp[1])

        psum = jnp.zeros((ns * N_KV, ncp), F32)
        for i in range(ns * N_KV):
            r0 = (i // N_KV) * N_HEADS + (i % N_KV) * GROUP
            psum = jnp.where(row8 == i, jnp.sum(p_c[r0:r0 + GROUP], axis=0, keepdims=True), psum)
        imp = _dot_exact(psum, imat_ref[...])
        vals = jnp.where(inrange, jnp.where(visible, jnp.where(forced, FORCE_SCORE, imp), NEG), -jnp.inf)
        sel8 = _topk_mask_by_rank(vals, blk, n_sel, n_top)
        sel_rows = jnp.zeros((nrow, nselp), F32)
        for i in range(ns * N_KV):
            sel_rows = jnp.where(grow == i, sel8[i:i + 1], sel_rows)
        chosen = _dot(sel_rows.astype(BF16), emat_ref[...])

        kst = buft[slot, 2].astype(BF16)
        vst = buft[slot, 3].astype(BF16)
        p_s = softmax_rows(_dot(qall, kst) + bias_s, causal_s & (chosen > 0.5))
        o_s = _dot_nt(p_s.astype(BF16), vst)

        outs_w = []
        for state_ref, new_ref in ((skw_ref, kwn_ref), (svw_ref, vwn_ref)):
            win_scr[:, 0:wbuf] = state_ref[r]
            win_scr[:, wbuf:wbuf + LANES] = new_rows_t(new_ref, r4)
            outs_w.append(win_scr[...].astype(BF16))
        p_w = softmax_rows(_dot(qall, outs_w[0]) + bias_w, valid_w)
        o_w = _dot_nt(p_w.astype(BF16), outs_w[1])

        o = gates[:, 0:1] * o_c + gates[:, 1:2] * o_s + gates[:, 2:3] * o_w
        o_ref[pl.ds(r * nrow, nrow), :] = jnp.where(half, o, 0.0).astype(BF16)

        vext_scr[0:POOL_BUF, :] = spool_ref[r]
        vext_scr[POOL_BUF:POOL_BUF + ns, :] = vpn_ref[pl.ds(r4, ns), :]
        ext = vext_scr[...]
        for t in range(ns):
            hi = POOL_BUF + t
            inwin = (prow <= hi) & (prow > hi - wcol)
            ssum = jnp.sum(jnp.where(inwin, ext, 0.0), axis=0, keepdims=True)
            cnt = jnp.minimum(past + t + 1, wcol).astype(F32)
            pooled_ref[pl.ds(r4 + t, 1), :] = ssum / cnt - ext[hi:hi + 1, :]

    for r in range(sb):
        seq_body(r)


def _nsa_sample(page_table, q_rows, gate_rows, new6, vp_new, state_kwt, state_vwt, state_pool, wk, wv, caches):
    n_seq, n_pages = page_table.shape
    page = caches[0].shape[2]
    past = n_pages * page
    ns = vp_new.shape[0] // n_seq
    wbuf = state_kwt.shape[2]
    sb = 2
    nrow = ns * N_HEADS
    assert ns <= SEL_BLOCK and page == LANES
    t_pad = -(-(past + ns) // SEL_BLOCK) * SEL_BLOCK
    n_cmp = t_pad // CMP_STRIDE - 1
    ncv = -(-n_cmp // 8) * 8
    ncp = -(-ncv // LANES) * LANES
    nks = past + LANES
    n_sel = t_pad // SEL_BLOCK
    nselp = LANES
    assert n_sel <= nselp
    wlp = wbuf + LANES
    buf_rows = -(-(CMP_STRIDE * ncv + CMP_STRIDE) // 8) * 8
    imat = _importance_matrix(ncp, nselp)
    emat = (jnp.arange(nselp)[:, None] == (jnp.arange(nks)[None, :] // SEL_BLOCK)).astype(BF16)

    seqblk = lambda rows, w: pl.BlockSpec((sb * rows, w), lambda i, pt: (i, 0))
    const = lambda a: pl.BlockSpec(a.shape, lambda i, pt: (0,) * a.ndim)
    kern = functools.partial(
        _nsa_sample_kernel, sb=sb, ns=ns, past=past, n_pages=n_pages, page=page, n_seq=n_seq,
        ncv=ncv, ncp=ncp, nks=nks, wls=(wbuf, wlp), nselp=nselp, n_sel=n_sel)
    grid_spec = pltpu.PrefetchScalarGridSpec(
        num_scalar_prefetch=1,
        grid=(n_seq // sb,),
        in_specs=[seqblk(nrow, LANES), seqblk(nrow, LANES)] + [seqblk(ns, KVW)] * 6 + [seqblk(ns, POOL_W)]
        + [pl.BlockSpec((sb, KVW, wbuf), lambda i, pt: (i, 0, 0))] * 2
        + [pl.BlockSpec((sb, POOL_BUF, POOL_W), lambda i, pt: (i, 0, 0))]
        + [const(wk), const(wv), const(imat), const(emat)]
        + [pl.BlockSpec(memory_space=pl.ANY)] * 4,
        out_specs=[seqblk(nrow, LANES), seqblk(ns, POOL_W)],
        scratch_shapes=[pltpu.VMEM((2, buf_rows, KVW), F32),
                        pltpu.VMEM((2, 4, KVW, nks), F32),
                        pltpu.VMEM((KVW, wlp), F32),
                        pltpu.VMEM((LANES, KVW), F32),
                        pltpu.VMEM((24, POOL_W), F32),
                        pltpu.SemaphoreType.DMA((2,))],
    )
    return pl.pallas_call(
        kern,
        out_shape=[jax.ShapeDtypeStruct((n_seq * nrow, LANES), BF16),
                   jax.ShapeDtypeStruct((n_seq * ns, POOL_W), F32)],
        grid_spec=grid_spec,
        compiler_params=_cparams(("arbitrary",)),
        name="nsa_sample",
    )(page_table.reshape(-1), q_rows, gate_rows, *new6, vp_new, state_kwt, state_vwt, state_pool, wk, wv,
      imat, emat, *caches)


def _route(logits_t, bias_col, tm):
    sc = jax.nn.sigmoid(logits_t)
    biased = sc + bias_col
    epg = EXPERTS_PER_GROUP
    row8 = lax.broadcasted_iota(I32, (epg, tm), 0).astype(F32)
    ninf = -jnp.inf
    grp = jnp.zeros((N_EGROUPS, tm), F32)
    for g in range(N_EGROUPS):
        bg = biased[g * epg:(g + 1) * epg]
        m1 = jnp.max(bg, axis=0, keepdims=True)
        first = jnp.min(jnp.where(bg == m1, row8, float(epg)), axis=0, keepdims=True)
        m2 = jnp.max(jnp.where(row8 == first, ninf, bg), axis=0, keepdims=True)
        grp = jnp.where(row8 == float(g), m1 + m2, grp)
    keep = jnp.zeros((N_EGROUPS, tm), F32)
    vals = grp
    for _ in range(TOPK_GROUPS):
        mx = jnp.max(vals, axis=0, keepdims=True)
        first = jnp.min(jnp.where(vals == mx, row8, float(N_EGROUPS)), axis=0, keepdims=True)
        hit = row8 == first
        keep = jnp.where(hit, 1.0, keep)
        vals = jnp.where(hit, ninf, vals)
    masked = jnp.concatenate(
        [jnp.where(keep[g:g + 1] > 0.5, biased[g * epg:(g + 1) * epg], NEG) for g in range(N_EGROUPS)], axis=0)
    rowe = lax.broadcasted_iota(I32, (N_EXPERTS, tm), 0).astype(F32)
    chosen = jnp.zeros((N_EXPERTS, tm), F32)
    vals = masked
    picks = []
    for _ in range(TOP_K):
        mx = jnp.max(vals, axis=0, keepdims=True)
        first = jnp.min(jnp.where(vals == mx, rowe, float(N_EXPERTS)), axis=0, keepdims=True)
        hit = rowe == first
        chosen = jnp.where(hit, sc, chosen)
        vals = jnp.where(hit, ninf, vals)
        picks.append((hit, first))
    return ROUTED_SCALE * chosen / jnp.sum(chosen, axis=0, keepdims=True), picks


def _pack_bf16_pairs(x):
    c = x.shape[1] // 2
    bits = lambda v: lax.bitcast_convert_type(v.astype(BF16).astype(F32), jnp.uint32)
    return (bits(x[:, :c]) >> 16) | (bits(x[:, c:]) & jnp.uint32(0xFFFF0000))


def _unpack_bf16_pairs(w):
    lo = lax.bitcast_convert_type(w << 16, F32)
    hi = lax.bitcast_convert_type(w & jnp.uint32(0xFFFF0000), F32)
    return jnp.concatenate([lo, hi], axis=1)


def _finish_kernel(x_ref, pooled_ref, y_ref, gm_ref, g1_ref, shift_ref, scale_ref,
                   wlin_ref, pscale_ref, wpo_ref, wno_ref, wo_ref, n2_ref, wr_ref, br_ref,
                   *rest, tm, d, sparse):
    if sparse:
        tri_ref, x1_ref, u2_ref, up_ref, eid_ref, gk_ref, rank_ref, cnt_ref, carry_scr = rest
    else:
        x1_ref, u2_ref, gates_ref = rest
    x = x_ref[...].reshape(tm, d)
    pooled = pooled_ref[...].reshape(tm, POOL_W).astype(BF16)
    y_pool = _dot(pooled, wlin_ref[...]) * pscale_ref[...]
    a = _dot(y_pool.astype(BF16), wpo_ref[...])
    b = _dot(y_ref[...].reshape(tm, QPAD), wno_ref[...])
    gm = gm_ref[...].reshape(tm, 2 * d)
    merged = gm[:, :d] * a + gm[:, d:] * b
    x1 = x + _rows2d(g1_ref) * _dot(merged.astype(BF16), wo_ref[...])
    x1_ref[...] = x1.reshape(x1_ref.shape)
    u2 = _rmsnorm(x1, n2_ref[...]) * (1.0 + _rows2d(scale_ref)) + _rows2d(shift_ref)
    u2b = u2.astype(BF16)
    u2_ref[...] = u2b.reshape(u2_ref.shape)
    logits_t = _dot_nt(wr_ref[...], u2b)
    gates_t, picks = _route(logits_t[:N_EXPERTS], br_ref[...], tm)
    if not sparse:
        gates_t = jnp.concatenate([gates_t, jnp.zeros((LANES - N_EXPERTS, tm), F32)], axis=0)
        gates_ref[...] = gates_t.T.reshape(gates_ref.shape)
        return

    @pl.when((pl.program_id(0) == 0) & (pl.program_id(1) == 0))
    def _():
        carry_scr[...] = jnp.zeros_like(carry_scr)

    packed = _pack_bf16_pairs(u2)
    for s in range(up_ref.shape[0]):
        up_ref[s] = packed[:, s * SC_ROW_WORDS:(s + 1) * SC_ROW_WORDS]
    hit_all = picks[0][0]
    for hit, _ in picks[1:]:
        hit_all = hit_all | hit
    hits = jnp.where(hit_all, 1.0, 0.0).astype(BF16)
    before = _dot(hits, tri_ref[...]) + jnp.concatenate([carry_scr[...]] * (tm // LANES), axis=1)
    eids, gks, ranks = [], [], []
    for hit, first in picks:
        eids.append(first)
        gks.append(jnp.sum(jnp.where(hit, gates_t, 0.0), axis=0, keepdims=True))
        ranks.append(jnp.sum(jnp.where(hit, before, 0.0), axis=0, keepdims=True))
    pick_row = lax.broadcasted_iota(I32, (TOP_K, tm), 0)

    def stack(rows):
        out = jnp.zeros((TOP_K, tm), F32)
        for r, v in enumerate(rows):
            out = jnp.where(pick_row == r, v, out)
        return out

    eid_ref[...] = stack(eids).astype(I32).reshape(eid_ref.shape)
    gk_ref[...] = stack(gks).reshape(gk_ref.shape)
    rank_ref[...] = stack(ranks).astype(I32).reshape(rank_ref.shape)
    carry_scr[...] += _dot(hits, jnp.ones((tm, LANES), BF16))
    cnt_ref[...] = carry_scr[...]


def _finish(x3, pooled, ynsa, gm, mods, wts, *, tm, sparse):
    g, r, d = x3.shape
    nt = r // tm
    g1, shift2, scale2 = mods
    per_row = g1.ndim == 2
    tok = lambda w: pl.BlockSpec((1, tm, w), lambda b, j: (b, j, 0))
    tok_t = lambda rows: pl.BlockSpec((1, rows, tm), lambda b, j: (b, 0, j))
    if per_row:
        mod_spec = lambda col: pl.BlockSpec((tm, d), lambda b, j, col=col: (b * nt + j, col))
    else:
        mod_spec = lambda col: pl.BlockSpec((1, 1, d), lambda b, j, col=col: (b, 0, col))
    const = lambda a: pl.BlockSpec(a.shape, lambda b, j: (0,) * a.ndim)
    out_shape = [jax.ShapeDtypeStruct((g, r, d), F32), jax.ShapeDtypeStruct((g, r, d), BF16)]
    out_specs = [tok(d), tok(d)]
    scratch = []
    if sparse:
        tri = (jnp.arange(tm)[:, None] < jnp.arange(tm)[None, :]).astype(BF16)
        wts = tuple(wts) + (tri,)
        split = d // 2 // SC_ROW_WORDS
        out_shape += [jax.ShapeDtypeStruct((split, g * r, SC_ROW_WORDS), jnp.uint32),
                      jax.ShapeDtypeStruct((g, TOP_K, r), I32), jax.ShapeDtypeStruct((g, TOP_K, r), F32),
                      jax.ShapeDtypeStruct((g, TOP_K, r), I32), jax.ShapeDtypeStruct((N_EXPERTS, LANES), F32)]
        out_specs += [pl.BlockSpec((split, tm, SC_ROW_WORDS), lambda b, j: (0, b * nt + j, 0)),
                      tok_t(TOP_K), tok_t(TOP_K), tok_t(TOP_K),
                      pl.BlockSpec((N_EXPERTS, LANES), lambda b, j: (0, 0))]
        scratch.append(pltpu.VMEM((N_EXPERTS, LANES), F32))
    else:
        out_shape.append(jax.ShapeDtypeStruct((g, r, LANES), F32))
        out_specs.append(tok(LANES))
    return pl.pallas_call(
        functools.partial(_finish_kernel, tm=tm, d=d, sparse=sparse),
        out_shape=out_shape,
        grid=(g, nt),
        in_specs=[tok(d), tok(POOL_W), tok(QPAD), tok(2 * d), mod_spec(2), mod_spec(3), mod_spec(4)]
        + [const(w) for w in wts],
        out_specs=out_specs,
        scratch_shapes=scratch,
        compiler_params=_cparams(("arbitrary", "arbitrary")),
        name="finish_route" if sparse else "finish",
    )(x3, pooled, ynsa, gm, g1, shift2, scale2, *wts)


def _moe_kernel(u_ref, gates_ref, x1_ref, g2_ref, nf_ref, wg_ref, wu_ref, wd_ref, sg_ref, su_ref, sd_ref,
                y_ref, acc_ref, *, tm, d, eps):
    e = pl.program_id(2)
    u = u_ref[...].reshape(tm, d)

    @pl.when(e == 0)
    def _():
        hs = _silu(_dot(u, sg_ref[...])) * _dot(u, su_ref[...])
        acc_ref[...] = _dot(hs.astype(BF16), sd_ref[...])

    gates = gates_ref[...].reshape(tm, LANES)
    lane = lax.broadcasted_iota(I32, (1, LANES), 1)
    hidden = []
    for j in range(eps):
        h = _silu(_dot(u, wg_ref[j].astype(BF16))) * _dot(u, wu_ref[j].astype(BF16))
        gate = jnp.sum(jnp.where(lane == e * eps + j, gates, 0.0), axis=1, keepdims=True)
        hidden.append((h * gate).astype(BF16))
    f = wd_ref.shape[1]
    acc_ref[...] += _dot(jnp.concatenate(hidden, axis=1), wd_ref[...].reshape(eps * f, d).astype(BF16))

    @pl.when(e == pl.num_programs(2) - 1)
    def _():
        x2 = x1_ref[...].reshape(tm, d) + _rows2d(g2_ref) * acc_ref[...]
        y_ref[...] = _rmsnorm(x2, nf_ref[...]).reshape(y_ref.shape)


def _moe(u2, gates, x1, g2, normf, w_gate, w_up, w_down, sg, su, sd, *, tm):
    g, r, d = x1.shape
    nt = r // tm
    ne, _, f = w_gate.shape
    per_row = g2.ndim == 2
    tok = lambda w: pl.BlockSpec((1, tm, w), lambda b, j, e: (b, j, 0))
    if per_row:
        g2_spec = pl.BlockSpec((tm, d), lambda b, j, e: (b * nt + j, 5))
    else:
        g2_spec = pl.BlockSpec((1, 1, d), lambda b, j, e: (b, 0, 5))
    once = pl.Buffered(buffer_count=1)
    const = lambda a: pl.BlockSpec(a.shape, lambda b, j, e: (0,) * a.ndim, pipeline_mode=once)
    eps = 4
    return pl.pallas_call(
        functools.partial(_moe_kernel, tm=tm, d=d, eps=eps),
        out_shape=jax.ShapeDtypeStruct((g, r, d), F32),
        grid=(g, nt, ne // eps),
        in_specs=[tok(d), tok(LANES),
                  pl.BlockSpec((1, tm, d), lambda b, j, e: (b, j, 0), pipeline_mode=once),
                  g2_spec, const(normf),
                  pl.BlockSpec((eps, d, f), lambda b, j, e: (e, 0, 0)),
                  pl.BlockSpec((eps, d, f), lambda b, j, e: (e, 0, 0)),
                  pl.BlockSpec((eps, f, d), lambda b, j, e: (e, 0, 0)),
                  const(sg), const(su), const(sd)],
        out_specs=tok(d),
        scratch_shapes=[pltpu.VMEM((tm, d), F32)],
        compiler_params=_cparams(("arbitrary", "arbitrary", "arbitrary")),
        name="moe",
    )(u2, gates, x1, g2, normf, w_gate, w_up, w_down, sg, su, sd)


SC_WINDOW = 128
SC_ROW_WORDS = 256
MOE_ROWS = 512


def _sc_mesh():
    return plsc.VectorSubcoreMesh(core_axis_name="c", subcore_axis_name="s")


def _sc_scatter_rows(src, dst_idx, n_dst):
    n, w = src.shape
    nk = dst_idx.shape[0]

    @pl.kernel(out_type=jax.ShapeDtypeStruct((n_dst, w), src.dtype), mesh=_sc_mesh(), scratch_types=[])
    def scatter(src_hbm, idx_hbm, dst_hbm):
        def body(rows_vmem, idx_vmem):
            pltpu.sync_copy(rows_vmem, dst_hbm.at[idx_vmem.at[0]])

        pltpu.emit_pipeline(
            body,
            grid=(nk, n // SC_WINDOW),
            in_specs=[pl.BlockSpec((SC_WINDOW, w), index_map=lambda k, i: (i, 0)),
                      pl.BlockSpec((1, SC_WINDOW), index_map=lambda k, i: (k, i))],
            out_specs=[],
            core_axis_name=("c", "s"),
            dimension_semantics=(pltpu.PARALLEL, pltpu.PARALLEL),
        )(src_hbm, idx_hbm)

    return scatter(src, dst_idx)


def _sc_gather_rows(src, idx):
    n, w = idx.shape[0], src.shape[1]

    @pl.kernel(out_type=jax.ShapeDtypeStruct((n, w), src.dtype), mesh=_sc_mesh(), scratch_types=[])
    def gather(src_hbm, idx_hbm, out_hbm):
        def body(idx_vmem, out_vmem):
            pltpu.sync_copy(src_hbm.at[idx_vmem.at[0]], out_vmem)

        pltpu.emit_pipeline(
            body,
            grid=(n // SC_WINDOW,),
            in_specs=[pl.BlockSpec((1, SC_WINDOW), index_map=lambda i: (0, i))],
            out_specs=[pl.BlockSpec((SC_WINDOW, w), index_map=lambda i: (i, 0))],
            core_axis_name=("c", "s"),
            dimension_semantics=(pltpu.PARALLEL,),
        )(idx_hbm, out_hbm)

    return gather(src, idx.reshape(1, n))


def _expert_rows_kernel(te_ref, nt_ref, x_ref, wg_ref, wu_ref, wd_ref, y_ref):
    i = pl.program_id(0)

    @pl.when(i < nt_ref[0])
    def _():
        split = x_ref.shape[0]
        x = _unpack_bf16_pairs(jnp.concatenate([x_ref[s] for s in range(split)], axis=1)).astype(BF16)
        h = _silu(_dot(x, wg_ref[...].astype(BF16))) * _dot(x, wu_ref[...].astype(BF16))
        y = _pack_bf16_pairs(_dot(h.astype(BF16), wd_ref[...].astype(BF16)))
        for s in range(split):
            y_ref[s] = y[:, s * SC_ROW_WORDS:(s + 1) * SC_ROW_WORDS]


def _expert_rows(tile_expert, n_tiles, x_sorted, w_gate, w_up, w_down):
    split, p, words = x_sorted.shape
    ne, d, f = w_gate.shape
    wspec = lambda a, b: pl.BlockSpec((None, a, b), lambda i, te, nt: (te[i], 0, 0))
    rows = pl.BlockSpec((split, MOE_ROWS, words), lambda i, te, nt: (0, i, 0))
    grid_spec = pltpu.PrefetchScalarGridSpec(
        num_scalar_prefetch=2,
        grid=(p // MOE_ROWS,),
        in_specs=[rows, wspec(d, f), wspec(d, f), wspec(f, d)],
        out_specs=rows,
    )
    return pl.pallas_call(
        _expert_rows_kernel,
        out_shape=jax.ShapeDtypeStruct((split, p, words), jnp.uint32),
        grid_spec=grid_spec,
        compiler_params=_cparams(("arbitrary",)),
        name="expert_rows",
    )(tile_expert, n_tiles, x_sorted, w_gate, w_up, w_down)


def _combine_kernel(yg_ref, gk_ref, u_ref, x1_ref, g2_ref, nf_ref, sg_ref, su_ref, sd_ref, y_ref, *, tm, d):
    u = u_ref[...].reshape(tm, d)
    hs = _silu(_dot(u, sg_ref[...])) * _dot(u, su_ref[...])
    acc = _dot(hs.astype(BF16), sd_ref[...])
    gk = gk_ref[...].reshape(tm, LANES)
    lane = lax.broadcasted_iota(I32, (1, LANES), 1)
    split = yg_ref.shape[0]
    for k in range(TOP_K):
        gate = jnp.sum(jnp.where(lane == k, gk, 0.0), axis=1, keepdims=True)
        words = jnp.concatenate([yg_ref[s, k] for s in range(split)], axis=1)
        acc = acc + gate * _unpack_bf16_pairs(words)
    x2 = x1_ref[...].reshape(tm, d) + _rows2d(g2_ref) * acc
    y_ref[...] = _rmsnorm(x2, nf_ref[...]).reshape(y_ref.shape)


def _combine(yg, gk, u2, x1, g2, normf, sg, su, sd, *, tm):
    g, r, d = x1.shape
    nt = r // tm
    split, _, _, words = yg.shape
    tok = lambda w: pl.BlockSpec((1, tm, w), lambda b, j: (b, j, 0))
    const = lambda a: pl.BlockSpec(a.shape, lambda b, j: (0,) * a.ndim)
    return pl.pallas_call(
        functools.partial(_combine_kernel, tm=tm, d=d),
        out_shape=jax.ShapeDtypeStruct((g, r, d), F32),
        grid=(g, nt),
        in_specs=[pl.BlockSpec((split, TOP_K, tm, words), lambda b, j: (0, 0, b * nt + j, 0)),
                  tok(LANES), tok(d), tok(d),
                  pl.BlockSpec((1, 1, d), lambda b, j: (b, 0, 5)), const(normf), const(sg), const(su), const(sd)],
        out_specs=tok(d),
        compiler_params=_cparams(("arbitrary", "arbitrary")),
        name="moe_combine",
    )(yg, gk, u2, x1, g2, normf, sg, su, sd)


def _moe_sorted(u2, u2p, eid_t, gk_t, rank_t, counts, x1, g2, normf, w_gate, w_up, w_down, sg, su, sd):
    g, r, d = x1.shape
    n = g * r
    ne = w_gate.shape[0]
    half = d // 2
    split = half // SC_ROW_WORDS
    cnt = counts[:, 0].astype(I32)
    padded = -(-cnt // MOE_ROWS) * MOE_ROWS
    seg_end = jnp.cumsum(padded)
    seg_start = seg_end - padded
    p_rows = n * TOP_K + ne * MOE_ROWS
    eid = eid_t.transpose(1, 0, 2).reshape(TOP_K, n)
    start = jnp.sum(jnp.where(eid[:, :, None] == jnp.arange(ne, dtype=I32), seg_start, 0), axis=-1)
    pos = start + rank_t.transpose(1, 0, 2).reshape(TOP_K, n)
    first_row = jnp.arange(p_rows // MOE_ROWS, dtype=I32) * MOE_ROWS
    tile_expert = jnp.minimum(jnp.sum(seg_end[None, :] <= first_row[:, None], axis=1), ne - 1).astype(I32)
    n_tiles = (seg_end[-1:] // MOE_ROWS).astype(I32)
    scat_idx = jnp.concatenate([pos + s * p_rows for s in range(split)], axis=1)
    gath_idx = jnp.concatenate([pos.reshape(-1) + s * p_rows for s in range(split)])
    x_sorted = _sc_scatter_rows(u2p.reshape(split * n, SC_ROW_WORDS), scat_idx, split * p_rows)
    y_sorted = _expert_rows(tile_expert, n_tiles, x_sorted.reshape(split, p_rows, SC_ROW_WORDS),
                            w_gate, w_up, w_down)
    yg = _sc_gather_rows(y_sorted.reshape(split * p_rows, SC_ROW_WORDS), gath_idx)
    gk = jnp.pad(gk_t.transpose(0, 2, 1), ((0, 0), (0, 0), (0, LANES - TOP_K)))
    return _combine(yg.reshape(split, TOP_K, n, SC_ROW_WORDS), gk, u2, x1, g2, normf, sg, su, sd, tm=512)


def _kv_slot_mask():
    return (jnp.arange(N_HEADS)[:, None] // GROUP == jnp.arange(N_KV)[None, :]).astype(F32)


def _prep_w_in(w_in, d):
    q0 = POOL_W
    kv0 = q0 + N_HEADS * HEAD_DIM
    gn0 = kv0 + 6 * KVW
    gm0 = gn0 + 3 * N_HEADS
    wq = w_in[:, q0:kv0].reshape(d, N_HEADS, 1, HEAD_DIM) * (HEAD_DIM ** -0.5)
    wq = (wq * _kv_slot_mask()[None, :, :, None]).reshape(d, QPAD)
    wgn = jnp.pad(w_in[:, gn0:gm0], ((0, 0), (0, LANES - 3 * N_HEADS)))
    return jnp.concatenate([w_in[:, :q0], wq, w_in[:, kv0:gn0], wgn, w_in[:, gm0:]], axis=1).astype(BF16)


def _prep_w_nsa_out(w, d):
    w = w.reshape(N_HEADS, 1, HEAD_DIM, d) * _kv_slot_mask()[:, :, None, None]
    return w.reshape(QPAD, d).astype(BF16)


def _block_diag(w_lin):
    g, c, _ = w_lin.shape
    eye = jnp.eye(g, dtype=F32)
    return (w_lin[:, :, None, :] * eye[:, None, :, None]).reshape(g * c, g * c).astype(BF16)


def kernel(x_prompt, x_sample, cache_kc, cache_vc, cache_ks, cache_vs, state_kw, state_vw, state_pool,
           page_table, c_prompt, c_sample, norm1_g, norm2_g, normf_g, w_ada, b_ada, w_in, w_pool_lin,
           pool_scale, w_cmp_k, w_cmp_v, w_pool_out, w_nsa_out, w_o, w_router, b_router, w_gate, w_up,
           w_down, ws_gate, ws_up, ws_down):
    depth = w_in.shape[0]
    assert depth == 1, "single-layer stack"
    bsz, seq, d = x_prompt.shape
    n_seq, ns, _ = x_sample.shape
    wbuf = state_kw.shape[2]
    lyr = 0

    c_all = jnp.concatenate([c_prompt, c_sample], axis=0)
    rows = c_all.shape[0]
    rows_p = -(-rows // 8) * 8
    mod = _adaln(jnp.pad(c_all, ((0, rows_p - rows), (0, 0))), w_ada[lyr], b_ada[lyr])
    mod_p = mod[:bsz].reshape(bsz, 1, 6 * d)
    mod_s = jnp.repeat(mod[bsz:bsz + n_seq], ns, axis=0)

    w2 = _prep_w_in(w_in[lyr], d)
    g1n = norm1_g[lyr].reshape(1, d)
    wk = w_cmp_k[lyr].reshape(CMP_BLOCK, KVW)
    wv = w_cmp_v[lyr].reshape(CMP_BLOCK, KVW)
    fin_w = (_block_diag(w_pool_lin[lyr]), pool_scale[lyr].reshape(1, POOL_W), w_pool_out[lyr].astype(BF16),
             _prep_w_nsa_out(w_nsa_out[lyr], d), w_o[lyr].astype(BF16), norm2_g[lyr].reshape(1, d),
             jnp.pad(w_router[lyr].T, ((0, LANES - N_EXPERTS), (0, 0))).astype(BF16),
             b_router[lyr].reshape(N_EXPERTS, 1))
    moe_w = (w_gate[lyr], w_up[lyr], w_down[lyr], ws_gate[lyr].astype(BF16), ws_up[lyr].astype(BF16),
             ws_down[lyr].astype(BF16))
    nf = normf_g.reshape(1, d)

    tm_p = 512
    (vp, kc, vc, ks, vs, kw, vw, gm, ksb, kwb, vst, vwt, qt, gst, pooled) = _in_proj(
        x_prompt, mod_p, mod_p, g1n, w2, tm=tm_p, prompt=True)
    kcmp, vcmpt = _compress(kc, vc, wk, wv)
    ynsa = _nsa_prompt(qt, gst, kcmp, vcmpt, ksb, vst, kwb, vwt)
    x1, u2, u2p, eid_t, gk_t, rank_t, counts = _finish(
        x_prompt, pooled, ynsa, gm, (mod_p, mod_p, mod_p), fin_w, tm=tm_p, sparse=True)
    y_prompt = _moe_sorted(u2, u2p, eid_t, gk_t, rank_t, counts, x1, mod_p, nf, *moe_w)

    n_tok = n_seq * ns
    xs3 = x_sample.reshape(1, n_tok, d)
    tm_s = 128
    (vp_s, kc_s, vc_s, ks_s, vs_s, kw_s, vw_s, gm_s, q_s, gs_s) = _in_proj(
        xs3, mod_s, mod_s, g1n, w2, tm=tm_s, prompt=False)
    two = lambda a: a.reshape(n_tok, a.shape[-1])
    q_rows = q_s.reshape(n_tok * N_HEADS, LANES)
    gate_rows = two(gs_s)[:, :3 * N_HEADS].reshape(n_tok, 3, N_HEADS).transpose(0, 2, 1)
    gate_rows = jnp.pad(gate_rows.reshape(n_tok * N_HEADS, 3), ((0, 0), (0, LANES - 3)))
    n_pool = cache_kc.shape[1]
    page = cache_kc.shape[2]
    rows_minor = lambda a: jnp.transpose(a, (0, 2, 3, 1)).reshape(a.shape[0], KVW, a.shape[1])
    caches = [rows_minor(c[lyr]) for c in (cache_kc, cache_vc, cache_ks, cache_vs)]
    o_rows, pooled_s = _nsa_sample(
        page_table, q_rows, gate_rows, [two(a) for a in (kc_s, vc_s, ks_s, vs_s, kw_s, vw_s)], two(vp_s),
        rows_minor(state_kw[lyr]), rows_minor(state_vw[lyr]), state_pool[lyr], wk, wv, caches)
    ynsa_s = o_rows.reshape(1, n_tok, QPAD)
    x1_s, u2_s, gates_s = _finish(xs3, pooled_s.reshape(1, n_tok, POOL_W), ynsa_s, gm_s,
                                  (mod_s, mod_s, mod_s), fin_w, tm=tm_s, sparse=False)
    y_sample = _moe(u2_s, gates_s, x1_s, mod_s, nf, *moe_w, tm=n_tok).reshape(n_seq, ns, d)

    kvp = lambda a: a.reshape(1, bsz, seq, N_KV, HEAD_DIM)
    tailp = lambda a: jnp.pad(a, ((0, 0), (wbuf, 0), (0, 0)))[:, -wbuf:].reshape(1, bsz, wbuf, N_KV, HEAD_DIM)
    kvs = lambda a: a.reshape(1, n_seq, ns, N_KV, HEAD_DIM)
    wins = lambda st, new: jnp.concatenate(
        [st[lyr], new.reshape(n_seq, ns, N_KV, HEAD_DIM)], axis=1)[None, :, -wbuf:]
    pool_p = vp[:, -POOL_BUF:][None]
    pool_s = jnp.concatenate([state_pool[lyr], vp_s.reshape(n_seq, ns, POOL_W)], axis=1)[None, :, -POOL_BUF:]
    return (y_prompt, y_sample, kvp(kc), kvp(vc), kvp(ks), kvp(vs), tailp(kw), tailp(vw), pool_p,
            kvs(kc_s), kvs(vc_s), kvs(ks_s), kvs(vs_s), wins(state_kw, kw_s), wins(state_vw, vw_s), pool_s)
```

```python
import functools

import jax
import jax.numpy as jnp
from jax import lax
from jax.experimental import pallas as pl
from jax.experimental.pallas import tpu as pltpu
from jax.experimental.pallas import tpu_sc as plsc

F32 = jnp.float32
BF16 = jnp.bfloat16
I32 = jnp.int32

POOL_WINDOWS = (2, 4, 8, 16)
POOL_GW = 64
POOL_W = 256
POOL_BUF = 15
N_HEADS = 8
HEAD_DIM = 64
N_KV = 2
GROUP = N_HEADS // N_KV
CMP_STRIDE = 16
CMP_BLOCK = 32
SEL_BLOCK = 64
TOP_BLOCKS = 16
WINDOW = 512
Q_BLOCK = 128
FORCE_SCORE = 1e4
N_EXPERTS = 64
N_EGROUPS = 8
EXPERTS_PER_GROUP = N_EXPERTS // N_EGROUPS
TOPK_GROUPS = 4
TOP_K = 8
ROUTED_SCALE = 2.5
EPS = 1e-6
NEG = -1e30
SLOPES = tuple(2.0 ** (-8.0 * (h + 1.0) / N_HEADS) for h in range(N_HEADS))

LANES = 128
QPAD = N_HEADS * LANES
KVW = N_KV * HEAD_DIM
VMEM_LIMIT = 56 * 1024 * 1024


def _cparams(sem):
    return pltpu.CompilerParams(dimension_semantics=sem, vmem_limit_bytes=VMEM_LIMIT)


def _dot(a, b):
    return jnp.dot(a, b, preferred_element_type=F32)


def _dot_nt(a, b):
    return lax.dot_general(a, b, (((1,), (1,)), ((), ())), preferred_element_type=F32)


def _dot_exact(a, b):
    return jnp.dot(a, b, preferred_element_type=F32, precision=lax.Precision.HIGHEST)


def _rows2d(ref):
    v = ref[...]
    return v.reshape(v.shape[-2], v.shape[-1])


def _rmsnorm(x, g):
    return x * lax.rsqrt(jnp.mean(x * x, axis=-1, keepdims=True) + EPS) * g


def _silu(x):
    return x * jax.nn.sigmoid(x)


def _adaln_kernel(c_ref, w_ref, b_ref, o_ref):
    s = _silu(c_ref[...]).astype(BF16)
    o_ref[...] = _dot(s, w_ref[...].astype(BF16)) + b_ref[...]


def _adaln(c, w_ada, b_ada):
    rows, d = c.shape
    n = w_ada.shape[1]
    tn = 512
    return pl.pallas_call(
        _adaln_kernel,
        out_shape=jax.ShapeDtypeStruct((rows, n), F32),
        grid=(n // tn,),
        in_specs=[pl.BlockSpec((rows, d), lambda j: (0, 0)),
                  pl.BlockSpec((d, tn), lambda j: (0, j)),
                  pl.BlockSpec((1, tn), lambda j: (0, j))],
        out_specs=pl.BlockSpec((rows, tn), lambda j: (0, j)),
        compiler_params=_cparams(("arbitrary",)),
        name="adaln",
    )(c, w_ada, b_ada.reshape(1, n))


_C_VP = 0
_C_Q = _C_VP + POOL_W
_C_KV = _C_Q + QPAD
_C_GN = _C_KV + 6 * KVW
_C_GM = _C_GN + LANES


def _pool_window_sums(ext, tm):
    s2 = ext + pltpu.roll(ext, 1, 0)
    s4 = s2 + pltpu.roll(s2, 2, 0)
    s8 = s4 + pltpu.roll(s4, 4, 0)
    s16 = s8 + pltpu.roll(s8, 8, 0)
    grp = lax.broadcasted_iota(I32, (1, POOL_W), 1) // POOL_GW
    pick = jnp.where(grp == 0, s2, jnp.where(grp == 1, s4, jnp.where(grp == 2, s8, s16)))
    return pick[16:16 + tm]


def _in_proj_kernel(x_ref, shift_ref, scale_ref, g_ref, w_ref,
                    vp_ref, kc_ref, vc_ref, ks_ref, vs_ref, kw_ref, vw_ref, gm_ref, *rest, tm, d, prompt):
    x = x_ref[...].reshape(tm, d)
    u = _rmsnorm(x, g_ref[...]) * (1.0 + _rows2d(scale_ref)) + _rows2d(shift_ref)
    ub = u.astype(BF16)

    head = _dot(ub, w_ref[:, 0:_C_GM])

    def proj(c0, n):
        return head[:, c0:c0 + n] if c0 + n <= _C_GM else _dot(ub, w_ref[:, c0:c0 + n])

    vp = proj(_C_VP, POOL_W)
    vp_ref[...] = vp.reshape(vp_ref.shape)
    kv = []
    for n, o32 in enumerate((kc_ref, vc_ref, ks_ref, vs_ref, kw_ref, vw_ref)):
        v = proj(_C_KV + n * KVW, KVW)
        o32[...] = v.reshape(o32.shape)
        kv.append(v)
    gm_ref[...] = jax.nn.sigmoid(proj(_C_GM, 2 * d)).reshape(gm_ref.shape)
    gs = jax.nn.sigmoid(proj(_C_GN, LANES))

    if not prompt:
        q_ref, gs_ref = rest
        q_ref[...] = proj(_C_Q, QPAD).astype(BF16).reshape(q_ref.shape)
        gs_ref[...] = gs.reshape(gs_ref.shape)
    else:
        ksb_ref, kwb_ref, vst_ref, vwt_ref, qt_ref, gst_ref, pooled_ref, halo_ref = rest
        ksb_ref[...] = kv[2].astype(BF16).reshape(ksb_ref.shape)
        kwb_ref[...] = kv[4].astype(BF16).reshape(kwb_ref.shape)
        vst_ref[...] = kv[3].T.astype(BF16).reshape(vst_ref.shape)
        vwt_ref[...] = kv[5].T.astype(BF16).reshape(vwt_ref.shape)
        gst_ref[...] = gs.T.reshape(gst_ref.shape)
        for h in range(N_HEADS):
            qt_ref[0, h] = proj(_C_Q + h * LANES, LANES).T.astype(BF16)
        j = pl.program_id(1)

        @pl.when(j == 0)
        def _():
            halo_ref[...] = jnp.zeros_like(halo_ref)

        ext = jnp.concatenate([halo_ref[...], vp], axis=0)
        sums = _pool_window_sums(ext, tm)
        pos = j * tm + lax.broadcasted_iota(I32, (tm, 1), 0)
        wcol = 2 << (lax.broadcasted_iota(I32, (1, POOL_W), 1) // POOL_GW)
        cnt = jnp.minimum(pos + 1, wcol).astype(F32)
        pooled_ref[...] = (sums / cnt - vp).astype(BF16).reshape(pooled_ref.shape)
        halo_ref[...] = vp[tm - 16:tm]


def _in_proj(x3, shift, scale, g1, w2, *, tm, prompt):
    g, r, d = x3.shape
    nt = r // tm
    per_row = shift.ndim == 2

    def tok(width, dtype):
        return (jax.ShapeDtypeStruct((g, r, width), dtype),
                pl.BlockSpec((1, tm, width), lambda b, j: (b, j, 0)))

    def tok_t(rows, dtype):
        return (jax.ShapeDtypeStruct((g, rows, r), dtype),
                pl.BlockSpec((1, rows, tm), lambda b, j: (b, 0, j)))

    outs = [tok(POOL_W, F32)] + [tok(KVW, F32)] * 6 + [tok(2 * d, F32)]
    scratch = []
    if prompt:
        outs += [tok(KVW, BF16), tok(KVW, BF16), tok_t(KVW, BF16), tok_t(KVW, BF16)]
        outs.append((jax.ShapeDtypeStruct((g, N_HEADS, LANES, r), BF16),
                     pl.BlockSpec((1, N_HEADS, LANES, tm), lambda b, j: (b, 0, 0, j))))
        outs += [tok_t(LANES, F32), tok(POOL_W, BF16)]
        scratch.append(pltpu.VMEM((16, POOL_W), F32))
    else:
        outs += [tok(QPAD, BF16), tok(LANES, F32)]
    if per_row:
        mod_spec = lambda col: pl.BlockSpec((tm, d), lambda b, j, col=col: (b * nt + j, col))
    else:
        mod_spec = lambda col: pl.BlockSpec((1, 1, d), lambda b, j, col=col: (b, 0, col))
    kern = functools.partial(_in_proj_kernel, tm=tm, d=d, prompt=prompt)
    return pl.pallas_call(
        kern,
        out_shape=[o[0] for o in outs],
        grid=(g, nt),
        in_specs=[pl.BlockSpec((1, tm, d), lambda b, j: (b, j, 0)),
                  mod_spec(0), mod_spec(1),
                  pl.BlockSpec((1, d), lambda b, j: (0, 0)),
                  pl.BlockSpec(w2.shape, lambda b, j: (0, 0))],
        out_specs=[o[1] for o in outs],
        scratch_shapes=scratch,
        compiler_params=_cparams(("arbitrary", "arbitrary")),
        name="in_proj_prompt" if prompt else "in_proj_sample",
    )(x3, shift, scale, g1, w2)


def _compress_kernel(kc_ref, vc_ref, wk_ref, wv_ref, okc_ref, ovc_ref, sh_ref, *, nc):
    last = lax.broadcasted_iota(I32, (nc, 1), 0) == nc - 1
    for src, w_ref, dst in ((kc_ref, wk_ref, okc_ref), (vc_ref, wv_ref, ovc_ref)):
        head = jnp.zeros((nc, KVW), F32)
        tail = jnp.zeros((nc, KVW), F32)
        for r in range(CMP_STRIDE):
            rows = src[pl.ds(r, nc, stride=CMP_STRIDE), :]
            head = head + rows * w_ref[r:r + 1, :]
            tail = tail + rows * w_ref[CMP_STRIDE + r:CMP_STRIDE + r + 1, :]
        sh_ref[0:nc, :] = tail
        sh_ref[nc:nc + 8, :] = jnp.zeros((8, KVW), F32)
        out = jnp.where(last, 0.0, head + sh_ref[1:nc + 1, :])
        dst[...] = (out if dst is okc_ref else out.T).astype(BF16)


def _compress(kc, vc, wk, wv):
    b, s, _ = kc.shape
    nc = s // CMP_STRIDE
    big = pl.BlockSpec((None, s, KVW), lambda i: (i, 0, 0))
    wsp = pl.BlockSpec((CMP_BLOCK, KVW), lambda i: (0, 0))
    return pl.pallas_call(
        functools.partial(_compress_kernel, nc=nc),
        out_shape=[jax.ShapeDtypeStruct((b, nc, KVW), BF16), jax.ShapeDtypeStruct((b, KVW, nc), BF16)],
        grid=(b,),
        in_specs=[big, big, wsp, wsp],
        out_specs=[pl.BlockSpec((None, nc, KVW), lambda i: (i, 0, 0)),
                   pl.BlockSpec((None, KVW, nc), lambda i: (i, 0, 0))],
        scratch_shapes=[pltpu.VMEM((nc + 8, KVW), F32)],
        compiler_params=_cparams(("arbitrary",)),
        name="compress",
    )(kc, vc, wk, wv)


def _topk_mask(vals, blk_f, n_top, axis=1):
    sel = jnp.zeros(vals.shape, F32)
    big = float(vals.shape[axis])
    for _ in range(n_top):
        mx = jnp.max(vals, axis=axis, keepdims=True)
        first = jnp.min(jnp.where(vals == mx, blk_f, big), axis=axis, keepdims=True)
        hit = blk_f == first
        sel = jnp.where(hit, 1.0, sel)
        vals = jnp.where(hit, -jnp.inf, vals)
    return sel


def _topk_mask_by_rank(vals, blk, n_valid, n_top):
    rank = jnp.zeros(vals.shape, F32)
    for j in range(n_valid):
        vj = vals[:, j:j + 1]
        beats = (vj > vals) | ((vj == vals) & (blk > j))
        rank = rank + jnp.where(beats, 1.0, 0.0)
    return jnp.where(rank < float(n_top), 1.0, 0.0)


def _pos_features(pos):
    hi = (pos // SEL_BLOCK).astype(F32)[:, None]
    lo = (pos % SEL_BLOCK).astype(F32)[:, None]
    return jnp.concatenate([hi, lo, jnp.zeros((pos.shape[0], LANES - 2), F32)], axis=1).astype(BF16)


def _importance_matrix(nc, nsel):
    j = jnp.arange(nc)[:, None]
    s = jnp.arange(nsel)[None, :]
    r = SEL_BLOCK // CMP_STRIDE
    a = (j >= r * s) & (j <= r * s + r - 1)
    b = (j + 1 >= r * s) & (j + 1 <= r * s + r - 1)
    return a.astype(F32) + b.astype(F32)


def _nsa_prompt_kernel(qt_ref, gst_ref, kc_ref, vct_ref, ks_ref, vst_ref, kw_ref, vwt_ref,
                       cfeat_ref, wfeat_ref, qfeat_ref, slope_ref,
                       y_ref, qk_scr, m_scr, l_scr, acc_scr, o_scr, sel_scr, imp_scr, flag_scr, *, seq, tk, wl):
    i = pl.program_id(1)
    q0 = i * Q_BLOCK
    nq = Q_BLOCK
    gq = GROUP * nq
    nc = kc_ref.shape[0]
    nsel = seq // SEL_BLOCK
    n_top = min(TOP_BLOCKS, nsel)
    blk_per_tile = tk // SEL_BLOCK
    qpos = q0 + lax.broadcasted_iota(I32, (1, nq), 1)
    gst = gst_ref[...]

    crow = lax.broadcasted_iota(I32, (nc, nq), 0)
    cend = crow * CMP_STRIDE + (CMP_BLOCK - 1)
    mask_c = qpos >= cend
    kc = jnp.concatenate([kc_ref[...], cfeat_ref[...]], axis=1)
    vct = vct_ref[...]
    blk = lax.broadcasted_iota(I32, (nsel, nq), 0)
    blk_f = blk.astype(F32)
    cur = qpos // SEL_BLOCK
    forced = (blk == 0) | (blk == cur) | (blk == cur - 1)
    visible = blk * SEL_BLOCK <= qpos
    ws = pl.multiple_of(jnp.maximum(q0 - WINDOW, 0), Q_BLOCK)
    wpos = ws + lax.broadcasted_iota(I32, (wl, nq), 0)
    valid_w = lax.bitcast_convert_type(qpos - wpos, jnp.uint32) < WINDOW
    n_tiles = (q0 + nq + tk - 1) // tk
    half_rows = lax.broadcasted_iota(I32, (KVW, nq), 0) // HEAD_DIM
    tile_pos = lax.broadcasted_iota(I32, (SEL_BLOCK, nq), 0)

    def lanes4(x):
        return jnp.concatenate([x] * GROUP, axis=1)

    def gate_row(branch, k):
        r0 = branch * N_HEADS + k * GROUP
        return jnp.concatenate([gst[r0 + g:r0 + g + 1] for g in range(GROUP)], axis=1)

    mask_c4 = lanes4(mask_c)
    valid_w4 = lanes4(valid_w)
    kwt = jnp.concatenate([kw_ref[pl.ds(ws, wl), :], wfeat_ref[...]], axis=1)
    vwtt = vwt_ref[:, pl.ds(ws, wl)]

    for k in range(N_KV):
        for g in range(GROUP):
            qk_scr[k, 0:LANES, g * nq:(g + 1) * nq] = qt_ref[k * GROUP + g]
        qk_scr[k, LANES:2 * LANES, :] = qfeat_ref[k]
        qk = qk_scr[k]

        s = jnp.where(mask_c4, _dot(kc, qk), NEG)
        e = jnp.where(mask_c4, jnp.exp(s - jnp.max(s, axis=0, keepdims=True)), 0.0)
        l = jnp.sum(e, axis=0, keepdims=True)
        p = e * jnp.where(l > 0.0, 1.0 / l, 0.0)
        o_c = _dot(vct, p.astype(BF16))
        psum = p[:, 0:nq]
        for g in range(1, GROUP):
            psum = psum + p[:, g * nq:(g + 1) * nq]

        a = psum + jnp.where(crow == 0, 0.0, pltpu.roll(psum, 1, 0))
        a = a + pltpu.roll(a, nc - 1, 0)
        imp_scr[...] = a + pltpu.roll(a, nc - 2, 0)
        imp = imp_scr[pl.ds(0, nsel, stride=nc // nsel), :]
        vals = jnp.where(visible, jnp.where(forced, FORCE_SCORE, imp), NEG)
        sel = jnp.where(visible, _topk_mask(vals, blk_f, n_top, axis=0), 0.0)
        sel_scr[k] = jnp.where(sel > 0.5, 0.0, NEG)
        blk_any = jnp.max(sel, axis=1, keepdims=True)
        for t in range(seq // tk):
            hit = (jnp.max(blk_any[t * blk_per_tile:(t + 1) * blk_per_tile, :]) > 0.5).astype(I32)
            flag_scr[t] = hit if k == 0 else flag_scr[t] | hit

        s = jnp.where(valid_w4, _dot(kwt, qk), NEG)
        e = jnp.exp(s - jnp.max(s, axis=0, keepdims=True))
        p = e / jnp.sum(e, axis=0, keepdims=True)
        o_w = _dot(vwtt, p.astype(BF16))
        o_scr[k] = gate_row(0, k) * o_c + gate_row(2, k) * o_w

    m_scr[...] = jnp.full(m_scr.shape, NEG, F32)
    l_scr[...] = jnp.zeros(l_scr.shape, F32)
    acc_scr[...] = jnp.zeros(acc_scr.shape, F32)

    def sel_tile(t):
        k0 = pl.multiple_of(t * tk, tk)
        kt = jnp.concatenate([ks_ref[pl.ds(k0, tk), :], wfeat_ref[0:tk, :]], axis=1)
        vtt = vst_ref[:, pl.ds(k0, tk)]
        causal = [qpos >= k0 + j * SEL_BLOCK + tile_pos for j in range(blk_per_tile)]
        base = (k0 - q0).astype(F32)
        scores = _dot(kt, jnp.concatenate([qk_scr[k] for k in range(N_KV)], axis=1))
        probs, alphas = [], []
        for k in range(N_KV):
            neg = []
            for j in range(blk_per_tile):
                row = sel_scr[k, pl.ds(t * blk_per_tile + j, 1), :]
                neg.append(jnp.where(causal[j], jnp.broadcast_to(row, (SEL_BLOCK, nq)), NEG))
            neg = lanes4(jnp.concatenate(neg, axis=0))
            off = slope_ref[k] * base
            s = scores[:, k * gq:(k + 1) * gq] + neg
            m_old = m_scr[k]
            m_new = jnp.maximum(m_old, jnp.max(s, axis=0, keepdims=True) + off)
            alpha = jnp.exp(m_old - m_new)
            p = jnp.exp(s - (m_new - off))
            l_scr[k] = alpha * l_scr[k] + jnp.sum(p, axis=0, keepdims=True)
            m_scr[k] = m_new
            probs.append(p.astype(BF16))
            alphas.append(alpha)
        pv = _dot(vtt, jnp.concatenate(probs, axis=1))
        for k in range(N_KV):
            acc_scr[k] = acc_scr[k] * alphas[k] + pv[:, k * gq:(k + 1) * gq]

    def sel_body(t, carry):
        pl.when(flag_scr[t] > 0)(functools.partial(sel_tile, t))
        return carry

    lax.fori_loop(0, n_tiles, sel_body, 0)

    for k in range(N_KV):
        o = o_scr[k] + gate_row(1, k) * (acc_scr[k] / l_scr[k])
        for g in range(GROUP):
            h = k * GROUP + g
            oh = jnp.where(half_rows == k, o[:, g * nq:(g + 1) * nq], 0.0)
            y_ref[:, h * LANES:(h + 1) * LANES] = oh.T.astype(BF16)


def _nsa_prompt(qt, gst, kcmp, vcmpt, ksb, vst, kwb, vwt):
    b, _, _, s = qt.shape
    nq = Q_BLOCK
    gq = GROUP * nq
    nc = kcmp.shape[1]
    nsel = s // SEL_BLOCK
    tk = 256
    wl = WINDOW + Q_BLOCK
    assert s % tk == 0 and s >= wl
    assert s // SEL_BLOCK <= 2 * LANES, "position // 64 must stay exact in bf16"
    cfeat = _pos_features(jnp.arange(nc) * CMP_STRIDE + (CMP_BLOCK - 1))
    wfeat = _pos_features(jnp.arange(wl))
    slope_rows = jnp.repeat(jnp.asarray(SLOPES, F32).reshape(N_KV, 1, GROUP), nq, axis=2)
    qfeat = jnp.concatenate([slope_rows * SEL_BLOCK, slope_rows, jnp.zeros((N_KV, LANES - 2, gq), F32)],
                            axis=1).astype(BF16)
    rows = lambda r: pl.BlockSpec((None, r, KVW), lambda bi, i: (bi, 0, 0))
    cols = lambda c: pl.BlockSpec((None, KVW, c), lambda bi, i: (bi, 0, 0))
    const = lambda a: pl.BlockSpec(a.shape, lambda bi, i: (0,) * a.ndim)
    return pl.pallas_call(
        functools.partial(_nsa_prompt_kernel, seq=s, tk=tk, wl=wl),
        out_shape=jax.ShapeDtypeStruct((b, s, QPAD), BF16),
        grid=(b, s // nq),
        in_specs=[pl.BlockSpec((None, N_HEADS, LANES, nq), lambda bi, i: (bi, 0, 0, i)),
                  pl.BlockSpec((None, LANES, nq), lambda bi, i: (bi, 0, i)),
                  rows(nc), cols(nc), rows(s), cols(s), rows(s), cols(s),
                  const(cfeat), const(wfeat), const(qfeat), const(slope_rows)],
        out_specs=pl.BlockSpec((None, nq, QPAD), lambda bi, i: (bi, i, 0)),
        scratch_shapes=[pltpu.VMEM((N_KV, 2 * LANES, gq), BF16),
                        pltpu.VMEM((N_KV, 1, gq), F32),
                        pltpu.VMEM((N_KV, 1, gq), F32),
                        pltpu.VMEM((N_KV, KVW, gq), F32),
                        pltpu.VMEM((N_KV, KVW, gq), F32),
                        pltpu.VMEM((N_KV, nsel, nq), F32),
                        pltpu.VMEM((nc, nq), F32),
                        pltpu.SMEM((s // tk,), I32)],
        compiler_params=_cparams(("arbitrary", "arbitrary")),
        name="nsa_prompt",
    )(qt, gst, kcmp, vcmpt, ksb, vst, kwb, vwt, cfeat, wfeat, qfeat, slope_rows)


def _nsa_sample_kernel(pt_ref, q_ref, gate_ref, kcn_ref, vcn_ref, ksn_ref, vsn_ref, kwn_ref, vwn_ref,
                       vpn_ref, skw_ref, svw_ref, spool_ref, wk_ref, wv_ref, imat_ref, emat_ref,
                       ckc_ref, cvc_ref, cks_ref, cvs_ref,
                       o_ref, pooled_ref, buf, buft, win_scr, tail_scr, vext_scr, sem,
                       *, sb, ns, past, n_pages, page, n_seq, ncv, ncp, nks, wls, nselp, n_sel):
    step = pl.program_id(0)
    nrow = ns * N_HEADS

    def copies(n, slot):
        out = []
        for p in range(n_pages):
            pg = pt_ref[n * n_pages + p]
            for c, cref in enumerate((ckc_ref, cvc_ref, cks_ref, cvs_ref)):
                out.append(pltpu.make_async_copy(cref.at[pg], buft.at[slot, c, :, pl.ds(p * page, page)],
                                                 sem.at[slot]))
        return out

    @pl.when(step == 0)
    def _():
        buf[:, past:, :] = jnp.zeros((2, buf.shape[1] - past, KVW), F32)
        tail_scr[...] = jnp.zeros_like(tail_scr)
        vext_scr[...] = jnp.zeros_like(vext_scr)
        for cp in copies(0, 0):
            cp.start()

    def new_rows_t(ref, r4):
        tail_scr[0:ns, :] = ref[pl.ds(r4, ns), :]
        return tail_scr[...].T

    row = lax.broadcasted_iota(I32, (nrow, 1), 0)
    hrow = row % N_HEADS
    qpos = past + row // N_HEADS
    slope = jnp.exp2(-8.0 * (hrow.astype(F32) + 1.0) / N_HEADS)
    kvrow = hrow // GROUP
    lane = lax.broadcasted_iota(I32, (1, LANES), 1)
    half = (lane // HEAD_DIM) == kvrow
    grow = (row // N_HEADS) * N_KV + kvrow
    row8 = lax.broadcasted_iota(I32, (ns * N_KV, 1), 0)
    qpos8 = past + lax.broadcasted_iota(I32, (ns * N_KV, 1), 0) // N_KV
    blk = lax.broadcasted_iota(I32, (1, nselp), 1)
    blk_f = blk.astype(F32)
    cur = qpos8 // SEL_BLOCK
    forced = (blk == 0) | (blk == cur) | (blk == cur - 1)
    visible = (blk * SEL_BLOCK <= qpos8)
    inrange = blk < n_sel
    cend = lax.broadcasted_iota(I32, (1, ncp), 1) * CMP_STRIDE + (CMP_BLOCK - 1)
    mask_c = qpos >= cend
    bias_c = slope * (cend - qpos).astype(F32)
    kpos = lax.broadcasted_iota(I32, (1, nks), 1)
    causal_s = qpos >= kpos
    bias_s = slope * (kpos - qpos).astype(F32)
    wbuf = wls[0]
    wpos = past - wbuf + lax.broadcasted_iota(I32, (1, wls[1]), 1)
    dw = qpos - wpos
    valid_w = lax.bitcast_convert_type(dw, jnp.uint32) < WINDOW
    bias_w = slope * (wpos - qpos).astype(F32)
    prow = lax.broadcasted_iota(I32, (vext_scr.shape[0], 1), 0)
    wcol = 2 << (lax.broadcasted_iota(I32, (1, POOL_W), 1) // POOL_GW)
    n_top = min(TOP_BLOCKS, n_sel)

    def softmax_rows(s, mask):
        s = jnp.where(mask, s, NEG)
        mx = jnp.max(s, axis=1, keepdims=True)
        e = jnp.where(mask, jnp.exp(s - mx), 0.0)
        l = jnp.sum(e, axis=1, keepdims=True)
        return e * jnp.where(l > 0.0, 1.0 / l, 0.0)

    def seq_body(r):
        n = step * sb + r
        slot = r % 2

        @pl.when(n + 1 < n_seq)
        def _():
            for cp in copies(n + 1, 1 - slot):
                cp.start()

        for cp in copies(n, slot):
            cp.wait()

        r4 = r * ns
        for c, new_ref in enumerate((kcn_ref, vcn_ref)):
            for p in range(n_pages):
                buf[c, p * page:(p + 1) * page, :] = buft[slot, c, :, p * page:(p + 1) * page].T
            buf[c, past:past + ns, :] = new_ref[pl.ds(r4, ns), :]
        for c, new_ref in ((2, ksn_ref), (3, vsn_ref)):
            buft[slot, c, :, past:past + LANES] = new_rows_t(new_ref, r4)

        qall = q_ref[pl.ds(r * nrow, nrow), :]
        gates = gate_ref[pl.ds(r * nrow, nrow), :]

        cmp = []
        for c, w_ref in ((0, wk_ref), (1, wv_ref)):
            span = CMP_STRIDE * ncv
            lo = buf[c, 0:span, :].reshape(ncv, CMP_STRIDE, KVW) * w_ref[0:CMP_STRIDE, :][None]
            hi = (buf[c, CMP_STRIDE:CMP_STRIDE + span, :].reshape(ncv, CMP_STRIDE, KVW)
                  * w_ref[CMP_STRIDE:CMP_BLOCK, :][None])
            acc = jnp.sum(lo + hi, axis=1)
            cmp.append(jnp.concatenate([acc, jnp.zeros((ncp - ncv, KVW), F32)], axis=0).astype(BF16))
        p_c = softmax_rows(_dot_nt(qall, cmp[0]) + bias_c, mask_c)
        o_c = _dot(p_c.astype(BF16), cmp[1])

        psum = jnp.zeros((ns * N_KV, ncp), F32)
        for i in range(ns * N_KV):
            r0 = (i // N_KV) * N_HEADS + (i % N_KV) * GROUP
            psum = jnp.where(row8 == i, jnp.sum(p_c[r0:r0 + GROUP], axis=0, keepdims=True), psum)
        imp = _dot_exact(psum, imat_ref[...])
        vals = jnp.where(inrange, jnp.where(visible, jnp.where(forced, FORCE_SCORE, imp), NEG), -jnp.inf)
        sel8 = _topk_mask_by_rank(vals, blk, n_sel, n_top)
        sel_rows = jnp.zeros((nrow, nselp), F32)
        for i in range(ns * N_KV):
            sel_rows = jnp.where(grow == i, sel8[i:i + 1], sel_rows)
        chosen = _dot(sel_rows.astype(BF16), emat_ref[...])

        kst = buft[slot, 2].astype(BF16)
        vst = buft[slot, 3].astype(BF16)
        p_s = softmax_rows(_dot(qall, kst) + bias_s, causal_s & (chosen > 0.5))
        o_s = _dot_nt(p_s.astype(BF16), vst)

        outs_w = []
        for state_ref, new_ref in ((skw_ref, kwn_ref), (svw_ref, vwn_ref)):
            win_scr[:, 0:wbuf] = state_ref[r]
            win_scr[:, wbuf:wbuf + LANES] = new_rows_t(new_ref, r4)
            outs_w.append(win_scr[...].astype(BF16))
        p_w = softmax_rows(_dot(qall, outs_w[0]) + bias_w, valid_w)
        o_w = _dot_nt(p_w.astype(BF16), outs_w[1])

        o = gates[:, 0:1] * o_c + gates[:, 1:2] * o_s + gates[:, 2:3] * o_w
        o_ref[pl.ds(r * nrow, nrow), :] = jnp.where(half, o, 0.0).astype(BF16)

        vext_scr[0:POOL_BUF, :] = spool_ref[r]
        vext_scr[POOL_BUF:POOL_BUF + ns, :] = vpn_ref[pl.ds(r4, ns), :]
        ext = vext_scr[...]
        for t in range(ns):
            hi = POOL_BUF + t
            inwin = (prow <= hi) & (prow > hi - wcol)
            ssum = jnp.sum(jnp.where(inwin, ext, 0.0), axis=0, keepdims=True)
            cnt = jnp.minimum(past + t + 1, wcol).astype(F32)
            pooled_ref[pl.ds(r4 + t, 1), :] = ssum / cnt - ext[hi:hi + 1, :]

    for r in range(sb):
        seq_body(r)


def _nsa_sample(page_table, q_rows, gate_rows, new6, vp_new, state_kwt, state_vwt, state_pool, wk, wv, caches):
    n_seq, n_pages = page_table.shape
    page = caches[0].shape[2]
    past = n_pages * page
    ns = vp_new.shape[0] // n_seq
    wbuf = state_kwt.shape[2]
    sb = 2
    nrow = ns * N_HEADS
    assert ns <= SEL_BLOCK and page == LANES
    t_pad = -(-(past + ns) // SEL_BLOCK) * SEL_BLOCK
    n_cmp = t_pad // CMP_STRIDE - 1
    ncv = -(-n_cmp // 8) * 8
    ncp = -(-ncv // LANES) * LANES
    nks = past + LANES
    n_sel = t_pad // SEL_BLOCK
    nselp = LANES
    assert n_sel <= nselp
    wlp = wbuf + LANES
    buf_rows = -(-(CMP_STRIDE * ncv + CMP_STRIDE) // 8) * 8
    imat = _importance_matrix(ncp, nselp)
    emat = (jnp.arange(nselp)[:, None] == (jnp.arange(nks)[None, :] // SEL_BLOCK)).astype(BF16)

    seqblk = lambda rows, w: pl.BlockSpec((sb * rows, w), lambda i, pt: (i, 0))
    const = lambda a: pl.BlockSpec(a.shape, lambda i, pt: (0,) * a.ndim)
    kern = functools.partial(
        _nsa_sample_kernel, sb=sb, ns=ns, past=past, n_pages=n_pages, page=page, n_seq=n_seq,
        ncv=ncv, ncp=ncp, nks=nks, wls=(wbuf, wlp), nselp=nselp, n_sel=n_sel)
    grid_spec = pltpu.PrefetchScalarGridSpec(
        num_scalar_prefetch=1,
        grid=(n_seq // sb,),
        in_specs=[seqblk(nrow, LANES), seqblk(nrow, LANES)] + [seqblk(ns, KVW)] * 6 + [seqblk(ns, POOL_W)]
        + [pl.BlockSpec((sb, KVW, wbuf), lambda i, pt: (i, 0, 0))] * 2
        + [pl.BlockSpec((sb, POOL_BUF, POOL_W), lambda i, pt: (i, 0, 0))]
        + [const(wk), const(wv), const(imat), const(emat)]
        + [pl.BlockSpec(memory_space=pl.ANY)] * 4,
        out_specs=[seqblk(nrow, LANES), seqblk(ns, POOL_W)],
        scratch_shapes=[pltpu.VMEM((2, buf_rows, KVW), F32),
                        pltpu.VMEM((2, 4, KVW, nks), F32),
                        pltpu.VMEM((KVW, wlp), F32),
                        pltpu.VMEM((LANES, KVW), F32),
                        pltpu.VMEM((24, POOL_W), F32),
                        pltpu.SemaphoreType.DMA((2,))],
    )
    return pl.pallas_call(
        kern,
        out_shape=[jax.ShapeDtypeStruct((n_seq * nrow, LANES), BF16),
                   jax.ShapeDtypeStruct((n_seq * ns, POOL_W), F32)],
        grid_spec=grid_spec,
        compiler_params=_cparams(("arbitrary",)),
        name="nsa_sample",
    )(page_table.reshape(-1), q_rows, gate_rows, *new6, vp_new, state_kwt, state_vwt, state_pool, wk, wv,
      imat, emat, *caches)


def _route(logits_t, bias_col, tm):
    sc = jax.nn.sigmoid(logits_t)
    biased = sc + bias_col
    epg = EXPERTS_PER_GROUP
    row8 = lax.broadcasted_iota(I32, (epg, tm), 0).astype(F32)
    ninf = -jnp.inf
    grp = jnp.zeros((N_EGROUPS, tm), F32)
    for g in range(N_EGROUPS):
        bg = biased[g * epg:(g + 1) * epg]
        m1 = jnp.max(bg, axis=0, keepdims=True)
        first = jnp.min(jnp.where(bg == m1, row8, float(epg)), axis=0, keepdims=True)
        m2 = jnp.max(jnp.where(row8 == first, ninf, bg), axis=0, keepdims=True)
        grp = jnp.where(row8 == float(g), m1 + m2, grp)
    keep = jnp.zeros((N_EGROUPS, tm), F32)
    vals = grp
    for _ in range(TOPK_GROUPS):
        mx = jnp.max(vals, axis=0, keepdims=True)
        first = jnp.min(jnp.where(vals == mx, row8, float(N_EGROUPS)), axis=0, keepdims=True)
        hit = row8 == first
        keep = jnp.where(hit, 1.0, keep)
        vals = jnp.where(hit, ninf, vals)
    masked = jnp.concatenate(
        [jnp.where(keep[g:g + 1] > 0.5, biased[g * epg:(g + 1) * epg], NEG) for g in range(N_EGROUPS)], axis=0)
    rowe = lax.broadcasted_iota(I32, (N_EXPERTS, tm), 0).astype(F32)
    chosen = jnp.zeros((N_EXPERTS, tm), F32)
    vals = masked
    picks = []
    for _ in range(TOP_K):
        mx = jnp.max(vals, axis=0, keepdims=True)
        first = jnp.min(jnp.where(vals == mx, rowe, float(N_EXPERTS)), axis=0, keepdims=True)
        hit = rowe == first
        chosen = jnp.where(hit, sc, chosen)
        vals = jnp.where(hit, ninf, vals)
        picks.append((hit, first))
    return ROUTED_SCALE * chosen / jnp.sum(chosen, axis=0, keepdims=True), picks


def _pack_bf16_pairs(x):
    c = x.shape[1] // 2
    bits = lambda v: lax.bitcast_convert_type(v.astype(BF16).astype(F32), jnp.uint32)
    return (bits(x[:, :c]) >> 16) | (bits(x[:, c:]) & jnp.uint32(0xFFFF0000))


def _unpack_bf16_pairs(w):
    lo = lax.bitcast_convert_type(w << 16, F32)
    hi = lax.bitcast_convert_type(w & jnp.uint32(0xFFFF0000), F32)
    return jnp.concatenate([lo, hi], axis=1)


def _finish_kernel(x_ref, pooled_ref, y_ref, gm_ref, g1_ref, shift_ref, scale_ref,
                   wlin_ref, pscale_ref, wpo_ref, wno_ref, wo_ref, n2_ref, wr_ref, br_ref,
                   *rest, tm, d, sparse):
    if sparse:
        tri_ref, x1_ref, u2_ref, up_ref, eid_ref, gk_ref, rank_ref, cnt_ref, carry_scr = rest
    else:
        x1_ref, u2_ref, gates_ref = rest
    x = x_ref[...].reshape(tm, d)
    pooled = pooled_ref[...].reshape(tm, POOL_W).astype(BF16)
    y_pool = _dot(pooled, wlin_ref[...]) * pscale_ref[...]
    a = _dot(y_pool.astype(BF16), wpo_ref[...])
    b = _dot(y_ref[...].reshape(tm, QPAD), wno_ref[...])
    gm = gm_ref[...].reshape(tm, 2 * d)
    merged = gm[:, :d] * a + gm[:, d:] * b
    x1 = x + _rows2d(g1_ref) * _dot(merged.astype(BF16), wo_ref[...])
    x1_ref[...] = x1.reshape(x1_ref.shape)
    u2 = _rmsnorm(x1, n2_ref[...]) * (1.0 + _rows2d(scale_ref)) + _rows2d(shift_ref)
    u2b = u2.astype(BF16)
    u2_ref[...] = u2b.reshape(u2_ref.shape)
    logits_t = _dot_nt(wr_ref[...], u2b)
    gates_t, picks = _route(logits_t[:N_EXPERTS], br_ref[...], tm)
    if not sparse:
        gates_t = jnp.concatenate([gates_t, jnp.zeros((LANES - N_EXPERTS, tm), F32)], axis=0)
        gates_ref[...] = gates_t.T.reshape(gates_ref.shape)
        return

    @pl.when((pl.program_id(0) == 0) & (pl.program_id(1) == 0))
    def _():
        carry_scr[...] = jnp.zeros_like(carry_scr)

    packed = _pack_bf16_pairs(u2)
    for s in range(up_ref.shape[0]):
        up_ref[s] = packed[:, s * SC_ROW_WORDS:(s + 1) * SC_ROW_WORDS]
    hit_all = picks[0][0]
    for hit, _ in picks[1:]:
        hit_all = hit_all | hit
    hits = jnp.where(hit_all, 1.0, 0.0).astype(BF16)
    before = _dot(hits, tri_ref[...]) + jnp.concatenate([carry_scr[...]] * (tm // LANES), axis=1)
    eids, gks, ranks = [], [], []
    for hit, first in picks:
        eids.append(first)
        gks.append(jnp.sum(jnp.where(hit, gates_t, 0.0), axis=0, keepdims=True))
        ranks.append(jnp.sum(jnp.where(hit, before, 0.0), axis=0, keepdims=True))
    pick_row = lax.broadcasted_iota(I32, (TOP_K, tm), 0)

    def stack(rows):
        out = jnp.zeros((TOP_K, tm), F32)
        for r, v in enumerate(rows):
            out = jnp.where(pick_row == r, v, out)
        return out

    eid_ref[...] = stack(eids).astype(I32).reshape(eid_ref.shape)
    gk_ref[...] = stack(gks).reshape(gk_ref.shape)
    rank_ref[...] = stack(ranks).astype(I32).reshape(rank_ref.shape)
    carry_scr[...] += _dot(hits, jnp.ones((tm, LANES), BF16))
    cnt_ref[...] = carry_scr[...]


def _finish(x3, pooled, ynsa, gm, mods, wts, *, tm, sparse):
    g, r, d = x3.shape
    nt = r // tm
    g1, shift2, scale2 = mods
    per_row = g1.ndim == 2
    tok = lambda w: pl.BlockSpec((1, tm, w), lambda b, j: (b, j, 0))
    tok_t = lambda rows: pl.BlockSpec((1, rows, tm), lambda b, j: (b, 0, j))
    if per_row:
        mod_spec = lambda col: pl.BlockSpec((tm, d), lambda b, j, col=col: (b * nt + j, col))
    else:
        mod_spec = lambda col: pl.BlockSpec((1, 1, d), lambda b, j, col=col: (b, 0, col))
    const = lambda a: pl.BlockSpec(a.shape, lambda b, j: (0,) * a.ndim)
    out_shape = [jax.ShapeDtypeStruct((g, r, d), F32), jax.ShapeDtypeStruct((g, r, d), BF16)]
    out_specs = [tok(d), tok(d)]
    scratch = []
    if sparse:
        tri = (jnp.arange(tm)[:, None] < jnp.arange(tm)[None, :]).astype(BF16)
        wts = tuple(wts) + (tri,)
        split = d // 2 // SC_ROW_WORDS
        out_shape += [jax.ShapeDtypeStruct((split, g * r, SC_ROW_WORDS), jnp.uint32),
                      jax.ShapeDtypeStruct((g, TOP_K, r), I32), jax.ShapeDtypeStruct((g, TOP_K, r), F32),
                      jax.ShapeDtypeStruct((g, TOP_K, r), I32), jax.ShapeDtypeStruct((N_EXPERTS, LANES), F32)]
        out_specs += [pl.BlockSpec((split, tm, SC_ROW_WORDS), lambda b, j: (0, b * nt + j, 0)),
                      tok_t(TOP_K), tok_t(TOP_K), tok_t(TOP_K),
                      pl.BlockSpec((N_EXPERTS, LANES), lambda b, j: (0, 0))]
        scratch.append(pltpu.VMEM((N_EXPERTS, LANES), F32))
    else:
        out_shape.append(jax.ShapeDtypeStruct((g, r, LANES), F32))
        out_specs.append(tok(LANES))
    return pl.pallas_call(
        functools.partial(_finish_kernel, tm=tm, d=d, sparse=sparse),
        out_shape=out_shape,
        grid=(g, nt),
        in_specs=[tok(d), tok(POOL_W), tok(QPAD), tok(2 * d), mod_spec(2), mod_spec(3), mod_spec(4)]
        + [const(w) for w in wts],
        out_specs=out_specs,
        scratch_shapes=scratch,
        compiler_params=_cparams(("arbitrary", "arbitrary")),
        name="finish_route" if sparse else "finish",
    )(x3, pooled, ynsa, gm, g1, shift2, scale2, *wts)


def _moe_kernel(u_ref, gates_ref, x1_ref, g2_ref, nf_ref, wg_ref, wu_ref, wd_ref, sg_ref, su_ref, sd_ref,
                y_ref, acc_ref, *, tm, d, eps):
    e = pl.program_id(2)
    u = u_ref[...].reshape(tm, d)

    @pl.when(e == 0)
    def _():
        hs = _silu(_dot(u, sg_ref[...])) * _dot(u, su_ref[...])
        acc_ref[...] = _dot(hs.astype(BF16), sd_ref[...])

    gates = gates_ref[...].reshape(tm, LANES)
    lane = lax.broadcasted_iota(I32, (1, LANES), 1)
    hidden = []
    for j in range(eps):
        h = _silu(_dot(u, wg_ref[j].astype(BF16))) * _dot(u, wu_ref[j].astype(BF16))
        gate = jnp.sum(jnp.where(lane == e * eps + j, gates, 0.0), axis=1, keepdims=True)
        hidden.append((h * gate).astype(BF16))
    f = wd_ref.shape[1]
    acc_ref[...] += _dot(jnp.concatenate(hidden, axis=1), wd_ref[...].reshape(eps * f, d).astype(BF16))

    @pl.when(e == pl.num_programs(2) - 1)
    def _():
        x2 = x1_ref[...].reshape(tm, d) + _rows2d(g2_ref) * acc_ref[...]
        y_ref[...] = _rmsnorm(x2, nf_ref[...]).reshape(y_ref.shape)


def _moe(u2, gates, x1, g2, normf, w_gate, w_up, w_down, sg, su, sd, *, tm):
    g, r, d = x1.shape
    nt = r // tm
    ne, _, f = w_gate.shape
    per_row = g2.ndim == 2
    tok = lambda w: pl.BlockSpec((1, tm, w), lambda b, j, e: (b, j, 0))
    if per_row:
        g2_spec = pl.BlockSpec((tm, d), lambda b, j, e: (b * nt + j, 5))
    else:
        g2_spec = pl.BlockSpec((1, 1, d), lambda b, j, e: (b, 0, 5))
    once = pl.Buffered(buffer_count=1)
    const = lambda a: pl.BlockSpec(a.shape, lambda b, j, e: (0,) * a.ndim, pipeline_mode=once)
    eps = 4
    return pl.pallas_call(
        functools.partial(_moe_kernel, tm=tm, d=d, eps=eps),
        out_shape=jax.ShapeDtypeStruct((g, r, d), F32),
        grid=(g, nt, ne // eps),
        in_specs=[tok(d), tok(LANES),
                  pl.BlockSpec((1, tm, d), lambda b, j, e: (b, j, 0), pipeline_mode=once),
                  g2_spec, const(normf),
                  pl.BlockSpec((eps, d, f), lambda b, j, e: (e, 0, 0)),
                  pl.BlockSpec((eps, d, f), lambda b, j, e: (e, 0, 0)),
                  pl.BlockSpec((eps, f, d), lambda b, j, e: (e, 0, 0)),
                  const(sg), const(su), const(sd)],
        out_specs=tok(d),
        scratch_shapes=[pltpu.VMEM((tm, d), F32)],
        compiler_params=_cparams(("arbitrary", "arbitrary", "arbitrary")),
        name="moe",
    )(u2, gates, x1, g2, normf, w_gate, w_up, w_down, sg, su, sd)


SC_WINDOW = 128
SC_ROW_WORDS = 256
MOE_ROWS = 512


def _sc_mesh():
    return plsc.VectorSubcoreMesh(core_axis_name="c", subcore_axis_name="s")


def _sc_scatter_rows(src, dst_idx, n_dst):
    n, w = src.shape
    nk = dst_idx.shape[0]

    @pl.kernel(out_type=jax.ShapeDtypeStruct((n_dst, w), src.dtype), mesh=_sc_mesh(), scratch_types=[])
    def scatter(src_hbm, idx_hbm, dst_hbm):
        def body(rows_vmem, idx_vmem):
            pltpu.sync_copy(rows_vmem, dst_hbm.at[idx_vmem.at[0]])

        pltpu.emit_pipeline(
            body,
            grid=(nk, n // SC_WINDOW),
            in_specs=[pl.BlockSpec((SC_WINDOW, w), index_map=lambda k, i: (i, 0)),
                      pl.BlockSpec((1, SC_WINDOW), index_map=lambda k, i: (k, i))],
            out_specs=[],
            core_axis_name=("c", "s"),
            dimension_semantics=(pltpu.PARALLEL, pltpu.PARALLEL),
        )(src_hbm, idx_hbm)

    return scatter(src, dst_idx)


def _sc_gather_rows(src, idx):
    n, w = idx.shape[0], src.shape[1]

    @pl.kernel(out_type=jax.ShapeDtypeStruct((n, w), src.dtype), mesh=_sc_mesh(), scratch_types=[])
    def gather(src_hbm, idx_hbm, out_hbm):
        def body(idx_vmem, out_vmem):
            pltpu.sync_copy(src_hbm.at[idx_vmem.at[0]], out_vmem)

        pltpu.emit_pipeline(
            body,
            grid=(n // SC_WINDOW,),
            in_specs=[pl.BlockSpec((1, SC_WINDOW), index_map=lambda i: (0, i))],
            out_specs=[pl.BlockSpec((SC_WINDOW, w), index_map=lambda i: (i, 0))],
            core_axis_name=("c", "s"),
            dimension_semantics=(pltpu.PARALLEL,),
        )(idx_hbm, out_hbm)

    return gather(src, idx.reshape(1, n))


MOE_BLOCKS_PER_STEP = 2


def _expert_rows_kernel(te_ref, nt_ref, x_ref, *refs):
    y_ref = refs[-1]
    split = x_ref.shape[0]
    for j in range(MOE_BLOCKS_PER_STEP):
        wg_ref, wu_ref, wd_ref = refs[3 * j:3 * j + 3]
        rows = slice(j * MOE_ROWS, (j + 1) * MOE_ROWS)

        @pl.when(pl.program_id(0) * MOE_BLOCKS_PER_STEP + j < nt_ref[0])
        def _():
            x = _unpack_bf16_pairs(jnp.concatenate([x_ref[s, rows] for s in range(split)], axis=1)).astype(BF16)
            h = _silu(_dot(x, wg_ref[...].astype(BF16))) * _dot(x, wu_ref[...].astype(BF16))
            y = _pack_bf16_pairs(_dot(h.astype(BF16), wd_ref[...].astype(BF16)))
            for s in range(split):
                y_ref[s, rows] = y[:, s * SC_ROW_WORDS:(s + 1) * SC_ROW_WORDS]


def _expert_rows(tile_expert, n_tiles, x_sorted, w_gate, w_up, w_down):
    split, p, words = x_sorted.shape
    ne, d, f = w_gate.shape
    bps = MOE_BLOCKS_PER_STEP
    wspec = lambda a, b, j: pl.BlockSpec((None, a, b), lambda i, te, nt: (te[i * bps + j], 0, 0))
    rows = pl.BlockSpec((split, bps * MOE_ROWS, words), lambda i, te, nt: (0, i, 0))
    weights, wspecs = [], []
    for j in range(bps):
        weights += [w_gate, w_up, w_down]
        wspecs += [wspec(d, f, j), wspec(d, f, j), wspec(f, d, j)]
    grid_spec = pltpu.PrefetchScalarGridSpec(
        num_scalar_prefetch=2,
        grid=(p // (bps * MOE_ROWS),),
        in_specs=[rows] + wspecs,
        out_specs=rows,
    )
    return pl.pallas_call(
        _expert_rows_kernel,
        out_shape=jax.ShapeDtypeStruct((split, p, words), jnp.uint32),
        grid_spec=grid_spec,
        compiler_params=_cparams(("arbitrary",)),
        name="expert_rows",
    )(tile_expert, n_tiles, x_sorted, *weights)


def _combine_kernel(yg_ref, gk_ref, u_ref, x1_ref, g2_ref, nf_ref, sg_ref, su_ref, sd_ref, y_ref, *, tm, d):
    u = u_ref[...].reshape(tm, d)
    hs = _silu(_dot(u, sg_ref[...])) * _dot(u, su_ref[...])
    acc = _dot(hs.astype(BF16), sd_ref[...])
    gk = gk_ref[...].reshape(tm, LANES)
    lane = lax.broadcasted_iota(I32, (1, LANES), 1)
    split = yg_ref.shape[0]
    for k in range(TOP_K):
        gate = jnp.sum(jnp.where(lane == k, gk, 0.0), axis=1, keepdims=True)
        words = jnp.concatenate([yg_ref[s, k] for s in range(split)], axis=1)
        acc = acc + gate * _unpack_bf16_pairs(words)
    x2 = x1_ref[...].reshape(tm, d) + _rows2d(g2_ref) * acc
    y_ref[...] = _rmsnorm(x2, nf_ref[...]).reshape(y_ref.shape)


def _combine(yg, gk, u2, x1, g2, normf, sg, su, sd, *, tm):
    g, r, d = x1.shape
    nt = r // tm
    split, _, _, words = yg.shape
    tok = lambda w: pl.BlockSpec((1, tm, w), lambda b, j: (b, j, 0))
    const = lambda a: pl.BlockSpec(a.shape, lambda b, j: (0,) * a.ndim)
    return pl.pallas_call(
        functools.partial(_combine_kernel, tm=tm, d=d),
        out_shape=jax.ShapeDtypeStruct((g, r, d), F32),
        grid=(g, nt),
        in_specs=[pl.BlockSpec((split, TOP_K, tm, words), lambda b, j: (0, 0, b * nt + j, 0)),
                  tok(LANES), tok(d), tok(d),
                  pl.BlockSpec((1, 1, d), lambda b, j: (b, 0, 5)), const(normf), const(sg), const(su), const(sd)],
        out_specs=tok(d),
        compiler_params=_cparams(("arbitrary", "arbitrary")),
        name="moe_combine",
    )(yg, gk, u2, x1, g2, normf, sg, su, sd)


def _moe_sorted(u2, u2p, eid_t, gk_t, rank_t, counts, x1, g2, normf, w_gate, w_up, w_down, sg, su, sd):
    g, r, d = x1.shape
    n = g * r
    ne = w_gate.shape[0]
    half = d // 2
    split = half // SC_ROW_WORDS
    cnt = counts[:, 0].astype(I32)
    padded = -(-cnt // MOE_ROWS) * MOE_ROWS
    seg_end = jnp.cumsum(padded)
    seg_start = seg_end - padded
    p_rows = n * TOP_K + ne * MOE_ROWS
    eid = eid_t.transpose(1, 0, 2).reshape(TOP_K, n)
    start = jnp.sum(jnp.where(eid[:, :, None] == jnp.arange(ne, dtype=I32), seg_start, 0), axis=-1)
    pos = start + rank_t.transpose(1, 0, 2).reshape(TOP_K, n)
    first_row = jnp.arange(p_rows // MOE_ROWS, dtype=I32) * MOE_ROWS
    tile_expert = jnp.minimum(jnp.sum(seg_end[None, :] <= first_row[:, None], axis=1), ne - 1).astype(I32)
    n_tiles = (seg_end[-1:] // MOE_ROWS).astype(I32)
    scat_idx = jnp.concatenate([pos + s * p_rows for s in range(split)], axis=1)
    gath_idx = jnp.concatenate([pos.reshape(-1) + s * p_rows for s in range(split)])
    x_sorted = _sc_scatter_rows(u2p.reshape(split * n, SC_ROW_WORDS), scat_idx, split * p_rows)
    y_sorted = _expert_rows(tile_expert, n_tiles, x_sorted.reshape(split, p_rows, SC_ROW_WORDS),
                            w_gate, w_up, w_down)
    yg = _sc_gather_rows(y_sorted.reshape(split * p_rows, SC_ROW_WORDS), gath_idx)
    gk = jnp.pad(gk_t.transpose(0, 2, 1), ((0, 0), (0, 0), (0, LANES - TOP_K)))
    return _combine(yg.reshape(split, TOP_K, n, SC_ROW_WORDS), gk, u2, x1, g2, normf, sg, su, sd, tm=512)


def _kv_slot_mask():
    return (jnp.arange(N_HEADS)[:, None] // GROUP == jnp.arange(N_KV)[None, :]).astype(F32)


def _prep_w_in(w_in, d):
    q0 = POOL_W
    kv0 = q0 + N_HEADS * HEAD_DIM
    gn0 = kv0 + 6 * KVW
    gm0 = gn0 + 3 * N_HEADS
    wq = w_in[:, q0:kv0].reshape(d, N_HEADS, 1, HEAD_DIM) * (HEAD_DIM ** -0.5)
    wq = (wq * _kv_slot_mask()[None, :, :, None]).reshape(d, QPAD)
    wgn = jnp.pad(w_in[:, gn0:gm0], ((0, 0), (0, LANES - 3 * N_HEADS)))
    return jnp.concatenate([w_in[:, :q0], wq, w_in[:, kv0:gn0], wgn, w_in[:, gm0:]], axis=1).astype(BF16)


def _prep_w_nsa_out(w, d):
    w = w.reshape(N_HEADS, 1, HEAD_DIM, d) * _kv_slot_mask()[:, :, None, None]
    return w.reshape(QPAD, d).astype(BF16)


def _block_diag(w_lin):
    g, c, _ = w_lin.shape
    eye = jnp.eye(g, dtype=F32)
    return (w_lin[:, :, None, :] * eye[:, None, :, None]).reshape(g * c, g * c).astype(BF16)


def kernel(x_prompt, x_sample, cache_kc, cache_vc, cache_ks, cache_vs, state_kw, state_vw, state_pool,
           page_table, c_prompt, c_sample, norm1_g, norm2_g, normf_g, w_ada, b_ada, w_in, w_pool_lin,
           pool_scale, w_cmp_k, w_cmp_v, w_pool_out, w_nsa_out, w_o, w_router, b_router, w_gate, w_up,
           w_down, ws_gate, ws_up, ws_down):
    depth = w_in.shape[0]
    assert depth == 1, "single-layer stack"
    bsz, seq, d = x_prompt.shape
    n_seq, ns, _ = x_sample.shape
    wbuf = state_kw.shape[2]
    lyr = 0

    c_all = jnp.concatenate([c_prompt, c_sample], axis=0)
    rows = c_all.shape[0]
    rows_p = -(-rows // 8) * 8
    mod = _adaln(jnp.pad(c_all, ((0, rows_p - rows), (0, 0))), w_ada[lyr], b_ada[lyr])
    mod_p = mod[:bsz].reshape(bsz, 1, 6 * d)
    mod_s = jnp.repeat(mod[bsz:bsz + n_seq], ns, axis=0)

    w2 = _prep_w_in(w_in[lyr], d)
    g1n = norm1_g[lyr].reshape(1, d)
    wk = w_cmp_k[lyr].reshape(CMP_BLOCK, KVW)
    wv = w_cmp_v[lyr].reshape(CMP_BLOCK, KVW)
    fin_w = (_block_diag(w_pool_lin[lyr]), pool_scale[lyr].reshape(1, POOL_W), w_pool_out[lyr].astype(BF16),
             _prep_w_nsa_out(w_nsa_out[lyr], d), w_o[lyr].astype(BF16), norm2_g[lyr].reshape(1, d),
             jnp.pad(w_router[lyr].T, ((0, LANES - N_EXPERTS), (0, 0))).astype(BF16),
             b_router[lyr].reshape(N_EXPERTS, 1))
    moe_w = (w_gate[lyr], w_up[lyr], w_down[lyr], ws_gate[lyr].astype(BF16), ws_up[lyr].astype(BF16),
             ws_down[lyr].astype(BF16))
    nf = normf_g.reshape(1, d)

    tm_p = 512
    (vp, kc, vc, ks, vs, kw, vw, gm, ksb, kwb, vst, vwt, qt, gst, pooled) = _in_proj(
        x_prompt, mod_p, mod_p, g1n, w2, tm=tm_p, prompt=True)
    kcmp, vcmpt = _compress(kc, vc, wk, wv)
    ynsa = _nsa_prompt(qt, gst, kcmp, vcmpt, ksb, vst, kwb, vwt)
    x1, u2, u2p, eid_t, gk_t, rank_t, counts = _finish(
        x_prompt, pooled, ynsa, gm, (mod_p, mod_p, mod_p), fin_w, tm=tm_p, sparse=True)
    y_prompt = _moe_sorted(u2, u2p, eid_t, gk_t, rank_t, counts, x1, mod_p, nf, *moe_w)

    n_tok = n_seq * ns
    xs3 = x_sample.reshape(1, n_tok, d)
    tm_s = 128
    (vp_s, kc_s, vc_s, ks_s, vs_s, kw_s, vw_s, gm_s, q_s, gs_s) = _in_proj(
        xs3, mod_s, mod_s, g1n, w2, tm=tm_s, prompt=False)
    two = lambda a: a.reshape(n_tok, a.shape[-1])
    q_rows = q_s.reshape(n_tok * N_HEADS, LANES)
    gate_rows = two(gs_s)[:, :3 * N_HEADS].reshape(n_tok, 3, N_HEADS).transpose(0, 2, 1)
    gate_rows = jnp.pad(gate_rows.reshape(n_tok * N_HEADS, 3), ((0, 0), (0, LANES - 3)))
    n_pool = cache_kc.shape[1]
    page = cache_kc.shape[2]
    rows_minor = lambda a: jnp.transpose(a, (0, 2, 3, 1)).reshape(a.shape[0], KVW, a.shape[1])
    caches = [rows_minor(c[lyr]) for c in (cache_kc, cache_vc, cache_ks, cache_vs)]
    o_rows, pooled_s = _nsa_sample(
        page_table, q_rows, gate_rows, [two(a) for a in (kc_s, vc_s, ks_s, vs_s, kw_s, vw_s)], two(vp_s),
        rows_minor(state_kw[lyr]), rows_minor(state_vw[lyr]), state_pool[lyr], wk, wv, caches)
    ynsa_s = o_rows.reshape(1, n_tok, QPAD)
    x1_s, u2_s, gates_s = _finish(xs3, pooled_s.reshape(1, n_tok, POOL_W), ynsa_s, gm_s,
                                  (mod_s, mod_s, mod_s), fin_w, tm=tm_s, sparse=False)
    y_sample = _moe(u2_s, gates_s, x1_s, mod_s, nf, *moe_w, tm=n_tok).reshape(n_seq, ns, d)

    kvp = lambda a: a.reshape(1, bsz, seq, N_KV, HEAD_DIM)
    tailp = lambda a: jnp.pad(a, ((0, 0), (wbuf, 0), (0, 0)))[:, -wbuf:].reshape(1, bsz, wbuf, N_KV, HEAD_DIM)
    kvs = lambda a: a.reshape(1, n_seq, ns, N_KV, HEAD_DIM)
    wins = lambda st, new: jnp.concatenate(
        [st[lyr], new.reshape(n_seq, ns, N_KV, HEAD_DIM)], axis=1)[None, :, -wbuf:]
    pool_p = vp[:, -POOL_BUF:][None]
    pool_s = jnp.concatenate([state_pool[lyr], vp_s.reshape(n_seq, ns, POOL_W)], axis=1)[None, :, -POOL_BUF:]
    return (y_prompt, y_sample, kvp(kc), kvp(vc), kvp(ks), kvp(vs), tailp(kw), tailp(vw), pool_p,
            kvs(kc_s), kvs(vc_s), kvs(ks_s), kvs(vs_s), wins(state_kw, kw_s), wins(state_vw, vw_s), pool_s)
```

```python
import functools

import jax
import jax.numpy as jnp
from jax import lax
from jax.experimental import pallas as pl
from jax.experimental.pallas import tpu as pltpu
from jax.experimental.pallas import tpu_sc as plsc

F32 = jnp.float32
BF16 = jnp.bfloat16
I32 = jnp.int32

POOL_WINDOWS = (2, 4, 8, 16)
POOL_GW = 64
POOL_W = 256
POOL_BUF = 15
N_HEADS = 8
HEAD_DIM = 64
N_KV = 2
GROUP = N_HEADS // N_KV
CMP_STRIDE = 16
CMP_BLOCK = 32
SEL_BLOCK = 64
TOP_BLOCKS = 16
WINDOW = 512
Q_BLOCK = 128
FORCE_SCORE = 1e4
N_EXPERTS = 64
N_EGROUPS = 8
EXPERTS_PER_GROUP = N_EXPERTS // N_EGROUPS
TOPK_GROUPS = 4
TOP_K = 8
ROUTED_SCALE = 2.5
EPS = 1e-6
NEG = -1e30
SLOPES = tuple(2.0 ** (-8.0 * (h + 1.0) / N_HEADS) for h in range(N_HEADS))

LANES = 128
QPAD = N_HEADS * LANES
KVW = N_KV * HEAD_DIM
VMEM_LIMIT = 56 * 1024 * 1024


def _cparams(sem):
    return pltpu.CompilerParams(dimension_semantics=sem, vmem_limit_bytes=VMEM_LIMIT)


def _dot(a, b):
    return jnp.dot(a, b, preferred_element_type=F32)


def _dot_nt(a, b):
    return lax.dot_general(a, b, (((1,), (1,)), ((), ())), preferred_element_type=F32)


def _dot_exact(a, b):
    return jnp.dot(a, b, preferred_element_type=F32, precision=lax.Precision.HIGHEST)


def _rows2d(ref):
    v = ref[...]
    return v.reshape(v.shape[-2], v.shape[-1])


def _rmsnorm(x, g):
    return x * lax.rsqrt(jnp.mean(x * x, axis=-1, keepdims=True) + EPS) * g


def _silu(x):
    return x * jax.nn.sigmoid(x)


def _adaln_kernel(c_ref, w_ref, b_ref, o_ref):
    s = _silu(c_ref[...]).astype(BF16)
    o_ref[...] = _dot(s, w_ref[...].astype(BF16)) + b_ref[...]


def _adaln(c, w_ada, b_ada):
    rows, d = c.shape
    n = w_ada.shape[1]
    tn = 512
    return pl.pallas_call(
        _adaln_kernel,
        out_shape=jax.ShapeDtypeStruct((rows, n), F32),
        grid=(n // tn,),
        in_specs=[pl.BlockSpec((rows, d), lambda j: (0, 0)),
                  pl.BlockSpec((d, tn), lambda j: (0, j)),
                  pl.BlockSpec((1, tn), lambda j: (0, j))],
        out_specs=pl.BlockSpec((rows, tn), lambda j: (0, j)),
        compiler_params=_cparams(("arbitrary",)),
        name="adaln",
    )(c, w_ada, b_ada.reshape(1, n))


_C_VP = 0
_C_Q = _C_VP + POOL_W
_C_KV = _C_Q + QPAD
_C_GN = _C_KV + 6 * KVW
_C_GM = _C_GN + LANES


def _pool_window_sums(ext, tm):
    s2 = ext + pltpu.roll(ext, 1, 0)
    s4 = s2 + pltpu.roll(s2, 2, 0)
    s8 = s4 + pltpu.roll(s4, 4, 0)
    s16 = s8 + pltpu.roll(s8, 8, 0)
    grp = lax.broadcasted_iota(I32, (1, POOL_W), 1) // POOL_GW
    pick = jnp.where(grp == 0, s2, jnp.where(grp == 1, s4, jnp.where(grp == 2, s8, s16)))
    return pick[16:16 + tm]


def _in_proj_kernel(x_ref, shift_ref, scale_ref, g_ref, w_ref,
                    vp_ref, kc_ref, vc_ref, ks_ref, vs_ref, kw_ref, vw_ref, gm_ref, *rest, tm, d, prompt):
    x = x_ref[...].reshape(tm, d)
    u = _rmsnorm(x, g_ref[...]) * (1.0 + _rows2d(scale_ref)) + _rows2d(shift_ref)
    ub = u.astype(BF16)

    head = _dot(ub, w_ref[:, 0:_C_GM])

    def proj(c0, n):
        return head[:, c0:c0 + n] if c0 + n <= _C_GM else _dot(ub, w_ref[:, c0:c0 + n])

    vp = proj(_C_VP, POOL_W)
    vp_ref[...] = vp.reshape(vp_ref.shape)
    kv = []
    for n, o32 in enumerate((kc_ref, vc_ref, ks_ref, vs_ref, kw_ref, vw_ref)):
        v = proj(_C_KV + n * KVW, KVW)
        o32[...] = v.reshape(o32.shape)
        kv.append(v)
    gm_ref[...] = jax.nn.sigmoid(proj(_C_GM, 2 * d)).reshape(gm_ref.shape)
    gs = jax.nn.sigmoid(proj(_C_GN, LANES))

    if not prompt:
        q_ref, gs_ref = rest
        q_ref[...] = proj(_C_Q, QPAD).astype(BF16).reshape(q_ref.shape)
        gs_ref[...] = gs.reshape(gs_ref.shape)
    else:
        ksb_ref, kwb_ref, vst_ref, vwt_ref, qt_ref, gst_ref, pooled_ref, halo_ref = rest
        ksb_ref[...] = kv[2].astype(BF16).reshape(ksb_ref.shape)
        kwb_ref[...] = kv[4].astype(BF16).reshape(kwb_ref.shape)
        vst_ref[...] = kv[3].T.astype(BF16).reshape(vst_ref.shape)
        vwt_ref[...] = kv[5].T.astype(BF16).reshape(vwt_ref.shape)
        gst_ref[...] = gs.T.reshape(gst_ref.shape)
        for h in range(N_HEADS):
            qt_ref[0, h] = proj(_C_Q + h * LANES, LANES).T.astype(BF16)
        j = pl.program_id(1)

        @pl.when(j == 0)
        def _():
            halo_ref[...] = jnp.zeros_like(halo_ref)

        ext = jnp.concatenate([halo_ref[...], vp], axis=0)
        sums = _pool_window_sums(ext, tm)
        pos = j * tm + lax.broadcasted_iota(I32, (tm, 1), 0)
        wcol = 2 << (lax.broadcasted_iota(I32, (1, POOL_W), 1) // POOL_GW)
        cnt = jnp.minimum(pos + 1, wcol).astype(F32)
        pooled_ref[...] = (sums / cnt - vp).astype(BF16).reshape(pooled_ref.shape)
        halo_ref[...] = vp[tm - 16:tm]


def _in_proj(x3, shift, scale, g1, w2, *, tm, prompt):
    g, r, d = x3.shape
    nt = r // tm
    per_row = shift.ndim == 2

    def tok(width, dtype):
        return (jax.ShapeDtypeStruct((g, r, width), dtype),
                pl.BlockSpec((1, tm, width), lambda b, j: (b, j, 0)))

    def tok_t(rows, dtype):
        return (jax.ShapeDtypeStruct((g, rows, r), dtype),
                pl.BlockSpec((1, rows, tm), lambda b, j: (b, 0, j)))

    outs = [tok(POOL_W, F32)] + [tok(KVW, F32)] * 6 + [tok(2 * d, F32)]
    scratch = []
    if prompt:
        outs += [tok(KVW, BF16), tok(KVW, BF16), tok_t(KVW, BF16), tok_t(KVW, BF16)]
        outs.append((jax.ShapeDtypeStruct((g, N_HEADS, LANES, r), BF16),
                     pl.BlockSpec((1, N_HEADS, LANES, tm), lambda b, j: (b, 0, 0, j))))
        outs += [tok_t(LANES, F32), tok(POOL_W, BF16)]
        scratch.append(pltpu.VMEM((16, POOL_W), F32))
    else:
        outs += [tok(QPAD, BF16), tok(LANES, F32)]
    if per_row:
        mod_spec = lambda col: pl.BlockSpec((tm, d), lambda b, j, col=col: (b * nt + j, col))
    else:
        mod_spec = lambda col: pl.BlockSpec((1, 1, d), lambda b, j, col=col: (b, 0, col))
    kern = functools.partial(_in_proj_kernel, tm=tm, d=d, prompt=prompt)
    return pl.pallas_call(
        kern,
        out_shape=[o[0] for o in outs],
        grid=(g, nt),
        in_specs=[pl.BlockSpec((1, tm, d), lambda b, j: (b, j, 0)),
                  mod_spec(0), mod_spec(1),
                  pl.BlockSpec((1, d), lambda b, j: (0, 0)),
                  pl.BlockSpec(w2.shape, lambda b, j: (0, 0))],
        out_specs=[o[1] for o in outs],
        scratch_shapes=scratch,
        compiler_params=_cparams(("arbitrary", "arbitrary")),
        name="in_proj_prompt" if prompt else "in_proj_sample",
    )(x3, shift, scale, g1, w2)


def _compress_kernel(kc_ref, vc_ref, wk_ref, wv_ref, okc_ref, ovc_ref, sh_ref, *, nc):
    last = lax.broadcasted_iota(I32, (nc, 1), 0) == nc - 1
    for src, w_ref, dst in ((kc_ref, wk_ref, okc_ref), (vc_ref, wv_ref, ovc_ref)):
        head = jnp.zeros((nc, KVW), F32)
        tail = jnp.zeros((nc, KVW), F32)
        for r in range(CMP_STRIDE):
            rows = src[pl.ds(r, nc, stride=CMP_STRIDE), :]
            head = head + rows * w_ref[r:r + 1, :]
            tail = tail + rows * w_ref[CMP_STRIDE + r:CMP_STRIDE + r + 1, :]
        sh_ref[0:nc, :] = tail
        sh_ref[nc:nc + 8, :] = jnp.zeros((8, KVW), F32)
        out = jnp.where(last, 0.0, head + sh_ref[1:nc + 1, :])
        dst[...] = (out if dst is okc_ref else out.T).astype(BF16)


def _compress(kc, vc, wk, wv):
    b, s, _ = kc.shape
    nc = s // CMP_STRIDE
    big = pl.BlockSpec((None, s, KVW), lambda i: (i, 0, 0))
    wsp = pl.BlockSpec((CMP_BLOCK, KVW), lambda i: (0, 0))
    return pl.pallas_call(
        functools.partial(_compress_kernel, nc=nc),
        out_shape=[jax.ShapeDtypeStruct((b, nc, KVW), BF16), jax.ShapeDtypeStruct((b, KVW, nc), BF16)],
        grid=(b,),
        in_specs=[big, big, wsp, wsp],
        out_specs=[pl.BlockSpec((None, nc, KVW), lambda i: (i, 0, 0)),
                   pl.BlockSpec((None, KVW, nc), lambda i: (i, 0, 0))],
        scratch_shapes=[pltpu.VMEM((nc + 8, KVW), F32)],
        compiler_params=_cparams(("arbitrary",)),
        name="compress",
    )(kc, vc, wk, wv)


def _topk_mask(vals, blk_f, n_top, axis=1):
    sel = jnp.zeros(vals.shape, F32)
    big = float(vals.shape[axis])
    for _ in range(n_top):
        mx = jnp.max(vals, axis=axis, keepdims=True)
        first = jnp.min(jnp.where(vals == mx, blk_f, big), axis=axis, keepdims=True)
        hit = blk_f == first
        sel = jnp.where(hit, 1.0, sel)
        vals = jnp.where(hit, -jnp.inf, vals)
    return sel


def _topk_mask_by_rank(vals, blk, n_valid, n_top):
    rank = jnp.zeros(vals.shape, F32)
    for j in range(n_valid):
        vj = vals[:, j:j + 1]
        beats = (vj > vals) | ((vj == vals) & (blk > j))
        rank = rank + jnp.where(beats, 1.0, 0.0)
    return jnp.where(rank < float(n_top), 1.0, 0.0)


def _pos_features(pos):
    hi = (pos // SEL_BLOCK).astype(F32)[:, None]
    lo = (pos % SEL_BLOCK).astype(F32)[:, None]
    return jnp.concatenate([hi, lo, jnp.zeros((pos.shape[0], LANES - 2), F32)], axis=1).astype(BF16)


def _importance_matrix(nc, nsel):
    j = jnp.arange(nc)[:, None]
    s = jnp.arange(nsel)[None, :]
    r = SEL_BLOCK // CMP_STRIDE
    a = (j >= r * s) & (j <= r * s + r - 1)
    b = (j + 1 >= r * s) & (j + 1 <= r * s + r - 1)
    return a.astype(F32) + b.astype(F32)


def _nsa_prompt_kernel(qt_ref, gst_ref, kc_ref, vct_ref, ks_ref, vst_ref, kw_ref, vwt_ref,
                       cfeat_ref, wfeat_ref, qfeat_ref, slope_ref,
                       y_ref, qk_scr, m_scr, l_scr, acc_scr, o_scr, sel_scr, imp_scr, flag_scr, *, seq, tk, wl):
    i = pl.program_id(1)
    q0 = i * Q_BLOCK
    nq = Q_BLOCK
    gq = GROUP * nq
    nc = kc_ref.shape[0]
    nsel = seq // SEL_BLOCK
    n_top = min(TOP_BLOCKS, nsel)
    blk_per_tile = tk // SEL_BLOCK
    qpos = q0 + lax.broadcasted_iota(I32, (1, nq), 1)
    gst = gst_ref[...]

    crow = lax.broadcasted_iota(I32, (nc, nq), 0)
    cend = crow * CMP_STRIDE + (CMP_BLOCK - 1)
    mask_c = qpos >= cend
    kc = jnp.concatenate([kc_ref[...], cfeat_ref[...]], axis=1)
    vct = vct_ref[...]
    blk = lax.broadcasted_iota(I32, (nsel, nq), 0)
    blk_f = blk.astype(F32)
    cur = qpos // SEL_BLOCK
    forced = (blk == 0) | (blk == cur) | (blk == cur - 1)
    visible = blk * SEL_BLOCK <= qpos
    ws = pl.multiple_of(jnp.maximum(q0 - WINDOW, 0), Q_BLOCK)
    wpos = ws + lax.broadcasted_iota(I32, (wl, nq), 0)
    valid_w = lax.bitcast_convert_type(qpos - wpos, jnp.uint32) < WINDOW
    n_tiles = (q0 + nq + tk - 1) // tk
    half_rows = lax.broadcasted_iota(I32, (KVW, nq), 0) // HEAD_DIM
    tile_pos = lax.broadcasted_iota(I32, (SEL_BLOCK, nq), 0)

    def lanes4(x):
        return jnp.concatenate([x] * GROUP, axis=1)

    def gate_row(branch, k):
        r0 = branch * N_HEADS + k * GROUP
        return jnp.concatenate([gst[r0 + g:r0 + g + 1] for g in range(GROUP)], axis=1)

    mask_c4 = lanes4(mask_c)
    valid_w4 = lanes4(valid_w)
    kwt = jnp.concatenate([kw_ref[pl.ds(ws, wl), :], wfeat_ref[...]], axis=1)
    vwtt = vwt_ref[:, pl.ds(ws, wl)]

    for k in range(N_KV):
        for g in range(GROUP):
            qk_scr[k, 0:LANES, g * nq:(g + 1) * nq] = qt_ref[k * GROUP + g]
        qk_scr[k, LANES:2 * LANES, :] = qfeat_ref[k]
        qk = qk_scr[k]

        s = jnp.where(mask_c4, _dot(kc, qk), NEG)
        e = jnp.where(mask_c4, jnp.exp(s - jnp.max(s, axis=0, keepdims=True)), 0.0)
        l = jnp.sum(e, axis=0, keepdims=True)
        p = e * jnp.where(l > 0.0, 1.0 / l, 0.0)
        o_c = _dot(vct, p.astype(BF16))
        psum = p[:, 0:nq]
        for g in range(1, GROUP):
            psum = psum + p[:, g * nq:(g + 1) * nq]

        a = psum + jnp.where(crow == 0, 0.0, pltpu.roll(psum, 1, 0))
        a = a + pltpu.roll(a, nc - 1, 0)
        imp_scr[...] = a + pltpu.roll(a, nc - 2, 0)
        imp = imp_scr[pl.ds(0, nsel, stride=nc // nsel), :]
        vals = jnp.where(visible, jnp.where(forced, FORCE_SCORE, imp), NEG)
        sel = jnp.where(visible, _topk_mask(vals, blk_f, n_top, axis=0), 0.0)
        sel_scr[k] = jnp.where(sel > 0.5, 0.0, NEG)
        blk_any = jnp.max(sel, axis=1, keepdims=True)
        for t in range(seq // tk):
            hit = (jnp.max(blk_any[t * blk_per_tile:(t + 1) * blk_per_tile, :]) > 0.5).astype(I32)
            flag_scr[t] = hit if k == 0 else flag_scr[t] | hit

        s = jnp.where(valid_w4, _dot(kwt, qk), NEG)
        e = jnp.exp(s - jnp.max(s, axis=0, keepdims=True))
        p = e / jnp.sum(e, axis=0, keepdims=True)
        o_w = _dot(vwtt, p.astype(BF16))
        o_scr[k] = gate_row(0, k) * o_c + gate_row(2, k) * o_w

    m_scr[...] = jnp.full(m_scr.shape, NEG, F32)
    l_scr[...] = jnp.zeros(l_scr.shape, F32)
    acc_scr[...] = jnp.zeros(acc_scr.shape, F32)

    def sel_tile(t):
        k0 = pl.multiple_of(t * tk, tk)
        kt = jnp.concatenate([ks_ref[pl.ds(k0, tk), :], wfeat_ref[0:tk, :]], axis=1)
        vtt = vst_ref[:, pl.ds(k0, tk)]
        causal = [qpos >= k0 + j * SEL_BLOCK + tile_pos for j in range(blk_per_tile)]
        base = (k0 - q0).astype(F32)
        scores = _dot(kt, jnp.concatenate([qk_scr[k] for k in range(N_KV)], axis=1))
        probs, alphas = [], []
        for k in range(N_KV):
            neg = []
            for j in range(blk_per_tile):
                row = sel_scr[k, pl.ds(t * blk_per_tile + j, 1), :]
                neg.append(jnp.where(causal[j], jnp.broadcast_to(row, (SEL_BLOCK, nq)), NEG))
            neg = lanes4(jnp.concatenate(neg, axis=0))
            off = slope_ref[k] * base
            s = scores[:, k * gq:(k + 1) * gq] + neg
            m_old = m_scr[k]
            m_new = jnp.maximum(m_old, jnp.max(s, axis=0, keepdims=True) + off)
            alpha = jnp.exp(m_old - m_new)
            p = jnp.exp(s - (m_new - off))
            l_scr[k] = alpha * l_scr[k] + jnp.sum(p, axis=0, keepdims=True)
            m_scr[k] = m_new
            probs.append(p.astype(BF16))
            alphas.append(alpha)
        pv = _dot(vtt, jnp.concatenate(probs, axis=1))
        for k in range(N_KV):
            acc_scr[k] = acc_scr[k] * alphas[k] + pv[:, k * gq:(k + 1) * gq]

    def sel_body(t, carry):
        pl.when(flag_scr[t] > 0)(functools.partial(sel_tile, t))
        return carry

    lax.fori_loop(0, n_tiles, sel_body, 0)

    for k in range(N_KV):
        o = o_scr[k] + gate_row(1, k) * (acc_scr[k] / l_scr[k])
        for g in range(GROUP):
            h = k * GROUP + g
            oh = jnp.where(half_rows == k, o[:, g * nq:(g + 1) * nq], 0.0)
            y_ref[:, h * LANES:(h + 1) * LANES] = oh.T.astype(BF16)


def _nsa_prompt(qt, gst, kcmp, vcmpt, ksb, vst, kwb, vwt):
    b, _, _, s = qt.shape
    nq = Q_BLOCK
    gq = GROUP * nq
    nc = kcmp.shape[1]
    nsel = s // SEL_BLOCK
    tk = 256
    wl = WINDOW + Q_BLOCK
    assert s % tk == 0 and s >= wl
    assert s // SEL_BLOCK <= 2 * LANES, "position // 64 must stay exact in bf16"
    cfeat = _pos_features(jnp.arange(nc) * CMP_STRIDE + (CMP_BLOCK - 1))
    wfeat = _pos_features(jnp.arange(wl))
    slope_rows = jnp.repeat(jnp.asarray(SLOPES, F32).reshape(N_KV, 1, GROUP), nq, axis=2)
    qfeat = jnp.concatenate([slope_rows * SEL_BLOCK, slope_rows, jnp.zeros((N_KV, LANES - 2, gq), F32)],
                            axis=1).astype(BF16)
    rows = lambda r: pl.BlockSpec((None, r, KVW), lambda bi, i: (bi, 0, 0))
    cols = lambda c: pl.BlockSpec((None, KVW, c), lambda bi, i: (bi, 0, 0))
    const = lambda a: pl.BlockSpec(a.shape, lambda bi, i: (0,) * a.ndim)
    return pl.pallas_call(
        functools.partial(_nsa_prompt_kernel, seq=s, tk=tk, wl=wl),
        out_shape=jax.ShapeDtypeStruct((b, s, QPAD), BF16),
        grid=(b, s // nq),
        in_specs=[pl.BlockSpec((None, N_HEADS, LANES, nq), lambda bi, i: (bi, 0, 0, i)),
                  pl.BlockSpec((None, LANES, nq), lambda bi, i: (bi, 0, i)),
                  rows(nc), cols(nc), rows(s), cols(s), rows(s), cols(s),
                  const(cfeat), const(wfeat), const(qfeat), const(slope_rows)],
        out_specs=pl.BlockSpec((None, nq, QPAD), lambda bi, i: (bi, i, 0)),
        scratch_shapes=[pltpu.VMEM((N_KV, 2 * LANES, gq), BF16),
                        pltpu.VMEM((N_KV, 1, gq), F32),
                        pltpu.VMEM((N_KV, 1, gq), F32),
                        pltpu.VMEM((N_KV, KVW, gq), F32),
                        pltpu.VMEM((N_KV, KVW, gq), F32),
                        pltpu.VMEM((N_KV, nsel, nq), F32),
                        pltpu.VMEM((nc, nq), F32),
                        pltpu.SMEM((s // tk,), I32)],
        compiler_params=_cparams(("arbitrary", "arbitrary")),
        name="nsa_prompt",
    )(qt, gst, kcmp, vcmpt, ksb, vst, kwb, vwt, cfeat, wfeat, qfeat, slope_rows)


def _nsa_sample_kernel(pt_ref, q_ref, gate_ref, kcn_ref, vcn_ref, ksn_ref, vsn_ref, kwn_ref, vwn_ref,
                       vpn_ref, skw_ref, svw_ref, spool_ref, wk_ref, wv_ref, imat_ref, emat_ref,
                       ckc_ref, cvc_ref, cks_ref, cvs_ref,
                       o_ref, pooled_ref, buf, buft, win_scr, tail_scr, vext_scr, sem,
                       *, sb, ns, past, n_pages, page, n_seq, ncv, ncp, nks, wls, nselp, n_sel):
    step = pl.program_id(0)
    nrow = ns * N_HEADS

    def copies(n, slot):
        out = []
        for p in range(n_pages):
            pg = pt_ref[n * n_pages + p]
            for c, cref in enumerate((ckc_ref, cvc_ref, cks_ref, cvs_ref)):
                out.append(pltpu.make_async_copy(cref.at[pg], buft.at[slot, c, :, pl.ds(p * page, page)],
                                                 sem.at[slot]))
        return out

    @pl.when(step == 0)
    def _():
        buf[:, past:, :] = jnp.zeros((2, buf.shape[1] - past, KVW), F32)
        tail_scr[...] = jnp.zeros_like(tail_scr)
        vext_scr[...] = jnp.zeros_like(vext_scr)
        for cp in copies(0, 0):
            cp.start()

    def new_rows_t(ref, r4):
        tail_scr[0:ns, :] = ref[pl.ds(r4, ns), :]
        return tail_scr[...].T

    row = lax.broadcasted_iota(I32, (nrow, 1), 0)
    hrow = row % N_HEADS
    qpos = past + row // N_HEADS
    slope = jnp.exp2(-8.0 * (hrow.astype(F32) + 1.0) / N_HEADS)
    kvrow = hrow // GROUP
    lane = lax.broadcasted_iota(I32, (1, LANES), 1)
    half = (lane // HEAD_DIM) == kvrow
    grow = (row // N_HEADS) * N_KV + kvrow
    row8 = lax.broadcasted_iota(I32, (ns * N_KV, 1), 0)
    qpos8 = past + lax.broadcasted_iota(I32, (ns * N_KV, 1), 0) // N_KV
    blk = lax.broadcasted_iota(I32, (1, nselp), 1)
    blk_f = blk.astype(F32)
    cur = qpos8 // SEL_BLOCK
    forced = (blk == 0) | (blk == cur) | (blk == cur - 1)
    visible = (blk * SEL_BLOCK <= qpos8)
    inrange = blk < n_sel
    cend = lax.broadcasted_iota(I32, (1, ncp), 1) * CMP_STRIDE + (CMP_BLOCK - 1)
    mask_c = qpos >= cend
    bias_c = slope * (cend - qpos).astype(F32)
    kpos = lax.broadcasted_iota(I32, (1, nks), 1)
    causal_s = qpos >= kpos
    bias_s = slope * (kpos - qpos).astype(F32)
    wbuf = wls[0]
    wpos = past - wbuf + lax.broadcasted_iota(I32, (1, wls[1]), 1)
    dw = qpos - wpos
    valid_w = lax.bitcast_convert_type(dw, jnp.uint32) < WINDOW
    bias_w = slope * (wpos - qpos).astype(F32)
    prow = lax.broadcasted_iota(I32, (vext_scr.shape[0], 1), 0)
    wcol = 2 << (lax.broadcasted_iota(I32, (1, POOL_W), 1) // POOL_GW)
    n_top = min(TOP_BLOCKS, n_sel)

    def softmax_rows(s, mask):
        s = jnp.where(mask, s, NEG)
        mx = jnp.max(s, axis=1, keepdims=True)
        e = jnp.where(mask, jnp.exp(s - mx), 0.0)
        l = jnp.sum(e, axis=1, keepdims=True)
        return e * jnp.where(l > 0.0, 1.0 / l, 0.0)

    def seq_body(r):
        n = step * sb + r
        slot = r % 2

        @pl.when(n + 1 < n_seq)
        def _():
            for cp in copies(n + 1, 1 - slot):
                cp.start()

        for cp in copies(n, slot):
            cp.wait()

        r4 = r * ns
        for c, new_ref in enumerate((kcn_ref, vcn_ref)):
            for p in range(n_pages):
                buf[c, p * page:(p + 1) * page, :] = buft[slot, c, :, p * page:(p + 1) * page].T
            buf[c, past:past + ns, :] = new_ref[pl.ds(r4, ns), :]
        for c, new_ref in ((2, ksn_ref), (3, vsn_ref)):
            buft[slot, c, :, past:past + LANES] = new_rows_t(new_ref, r4)

        qall = q_ref[pl.ds(r * nrow, nrow), :]
        gates = gate_ref[pl.ds(r * nrow, nrow), :]

        cmp = []
        for c, w_ref in ((0, wk_ref), (1, wv_ref)):
            span = CMP_STRIDE * ncv
            lo = buf[c, 0:span, :].reshape(ncv, CMP_STRIDE, KVW) * w_ref[0:CMP_STRIDE, :][None]
            hi = (buf[c, CMP_STRIDE:CMP_STRIDE + span, :].reshape(ncv, CMP_STRIDE, KVW)
                  * w_ref[CMP_STRIDE:CMP_BLOCK, :][None])
            acc = jnp.sum(lo + hi, axis=1)
            cmp.append(jnp.concatenate([acc, jnp.zeros((ncp - ncv, KVW), F32)], axis=0).astype(BF16))
        p_c = softmax_rows(_dot_nt(qall, cmp[0]) + bias_c, mask_c)
        o_c = _dot(p_c.astype(BF16), cmp[1])

        psum = jnp.zeros((ns * N_KV, ncp), F32)
        for i in range(ns * N_KV):
            r0 = (i // N_KV) * N_HEADS + (i % N_KV) * GROUP
            psum = jnp.where(row8 == i, jnp.sum(p_c[r0:r0 + GROUP], axis=0, keepdims=True), psum)
        imp = _dot_exact(psum, imat_ref[...])
        vals = jnp.where(inrange, jnp.where(visible, jnp.where(forced, FORCE_SCORE, imp), NEG), -jnp.inf)
        sel8 = _topk_mask_by_rank(vals, blk, n_sel, n_top)
        sel_rows = jnp.zeros((nrow, nselp), F32)
        for i in range(ns * N_KV):
            sel_rows = jnp.where(grow == i, sel8[i:i + 1], sel_rows)
        chosen = _dot(sel_rows.astype(BF16), emat_ref[...])

        kst = buft[slot, 2].astype(BF16)
        vst = buft[slot, 3].astype(BF16)
        p_s = softmax_rows(_dot(qall, kst) + bias_s, causal_s & (chosen > 0.5))
        o_s = _dot_nt(p_s.astype(BF16), vst)

        outs_w = []
        for state_ref, new_ref in ((skw_ref, kwn_ref), (svw_ref, vwn_ref)):
            win_scr[:, 0:wbuf] = state_ref[r]
            win_scr[:, wbuf:wbuf + LANES] = new_rows_t(new_ref, r4)
            outs_w.append(win_scr[...].astype(BF16))
        p_w = softmax_rows(_dot(qall, outs_w[0]) + bias_w, valid_w)
        o_w = _dot_nt(p_w.astype(BF16), outs_w[1])

        o = gates[:, 0:1] * o_c + gates[:, 1:2] * o_s + gates[:, 2:3] * o_w
        o_ref[pl.ds(r * nrow, nrow), :] = jnp.where(half, o, 0.0).astype(BF16)

        vext_scr[0:POOL_BUF, :] = spool_ref[r]
        vext_scr[POOL_BUF:POOL_BUF + ns, :] = vpn_ref[pl.ds(r4, ns), :]
        ext = vext_scr[...]
        for t in range(ns):
            hi = POOL_BUF + t
            inwin = (prow <= hi) & (prow > hi - wcol)
            ssum = jnp.sum(jnp.where(inwin, ext, 0.0), axis=0, keepdims=True)
            cnt = jnp.minimum(past + t + 1, wcol).astype(F32)
            pooled_ref[pl.ds(r4 + t, 1), :] = ssum / cnt - ext[hi:hi + 1, :]

    for r in range(sb):
        seq_body(r)


def _nsa_sample(page_table, q_rows, gate_rows, new6, vp_new, state_kwt, state_vwt, state_pool, wk, wv, caches):
    n_seq, n_pages = page_table.shape
    page = caches[0].shape[2]
    past = n_pages * page
    ns = vp_new.shape[0] // n_seq
    wbuf = state_kwt.shape[2]
    sb = 2
    nrow = ns * N_HEADS
    assert ns <= SEL_BLOCK and page == LANES
    t_pad = -(-(past + ns) // SEL_BLOCK) * SEL_BLOCK
    n_cmp = t_pad // CMP_STRIDE - 1
    ncv = -(-n_cmp // 8) * 8
    ncp = -(-ncv // LANES) * LANES
    nks = past + LANES
    n_sel = t_pad // SEL_BLOCK
    nselp = LANES
    assert n_sel <= nselp
    wlp = wbuf + LANES
    buf_rows = -(-(CMP_STRIDE * ncv + CMP_STRIDE) // 8) * 8
    imat = _importance_matrix(ncp, nselp)
    emat = (jnp.arange(nselp)[:, None] == (jnp.arange(nks)[None, :] // SEL_BLOCK)).astype(BF16)

    seqblk = lambda rows, w: pl.BlockSpec((sb * rows, w), lambda i, pt: (i, 0))
    const = lambda a: pl.BlockSpec(a.shape, lambda i, pt: (0,) * a.ndim)
    kern = functools.partial(
        _nsa_sample_kernel, sb=sb, ns=ns, past=past, n_pages=n_pages, page=page, n_seq=n_seq,
        ncv=ncv, ncp=ncp, nks=nks, wls=(wbuf, wlp), nselp=nselp, n_sel=n_sel)
    grid_spec = pltpu.PrefetchScalarGridSpec(
        num_scalar_prefetch=1,
        grid=(n_seq // sb,),
        in_specs=[seqblk(nrow, LANES), seqblk(nrow, LANES)] + [seqblk(ns, KVW)] * 6 + [seqblk(ns, POOL_W)]
        + [pl.BlockSpec((sb, KVW, wbuf), lambda i, pt: (i, 0, 0))] * 2
        + [pl.BlockSpec((sb, POOL_BUF, POOL_W), lambda i, pt: (i, 0, 0))]
        + [const(wk), const(wv), const(imat), const(emat)]
        + [pl.BlockSpec(memory_space=pl.ANY)] * 4,
        out_specs=[seqblk(nrow, LANES), seqblk(ns, POOL_W)],
        scratch_shapes=[pltpu.VMEM((2, buf_rows, KVW), F32),
                        pltpu.VMEM((2, 4, KVW, nks), F32),
                        pltpu.VMEM((KVW, wlp), F32),
                        pltpu.VMEM((LANES, KVW), F32),
                        pltpu.VMEM((24, POOL_W), F32),
                        pltpu.SemaphoreType.DMA((2,))],
    )
    return pl.pallas_call(
        kern,
        out_shape=[jax.ShapeDtypeStruct((n_seq * nrow, LANES), BF16),
                   jax.ShapeDtypeStruct((n_seq * ns, POOL_W), F32)],
        grid_spec=grid_spec,
        compiler_params=_cparams(("arbitrary",)),
        name="nsa_sample",
    )(page_table.reshape(-1), q_rows, gate_rows, *new6, vp_new, state_kwt, state_vwt, state_pool, wk, wv,
      imat, emat, *caches)


def _route(logits_t, bias_col, tm):
    sc = jax.nn.sigmoid(logits_t)
    biased = sc + bias_col
    epg = EXPERTS_PER_GROUP
    row8 = lax.broadcasted_iota(I32, (epg, tm), 0).astype(F32)
    ninf = -jnp.inf
    grp = jnp.zeros((N_EGROUPS, tm), F32)
    for g in range(N_EGROUPS):
        bg = biased[g * epg:(g + 1) * epg]
        m1 = jnp.max(bg, axis=0, keepdims=True)
        first = jnp.min(jnp.where(bg == m1, row8, float(epg)), axis=0, keepdims=True)
        m2 = jnp.max(jnp.where(row8 == first, ninf, bg), axis=0, keepdims=True)
        grp = jnp.where(row8 == float(g), m1 + m2, grp)
    keep = jnp.zeros((N_EGROUPS, tm), F32)
    vals = grp
    for _ in range(TOPK_GROUPS):
        mx = jnp.max(vals, axis=0, keepdims=True)
        first = jnp.min(jnp.where(vals == mx, row8, float(N_EGROUPS)), axis=0, keepdims=True)
        hit = row8 == first
        keep = jnp.where(hit, 1.0, keep)
        vals = jnp.where(hit, ninf, vals)
    masked = jnp.concatenate(
        [jnp.where(keep[g:g + 1] > 0.5, biased[g * epg:(g + 1) * epg], NEG) for g in range(N_EGROUPS)], axis=0)
    rowe = lax.broadcasted_iota(I32, (N_EXPERTS, tm), 0).astype(F32)
    chosen = jnp.zeros((N_EXPERTS, tm), F32)
    vals = masked
    picks = []
    for _ in range(TOP_K):
        mx = jnp.max(vals, axis=0, keepdims=True)
        first = jnp.min(jnp.where(vals == mx, rowe, float(N_EXPERTS)), axis=0, keepdims=True)
        hit = rowe == first
        chosen = jnp.where(hit, sc, chosen)
        vals = jnp.where(hit, ninf, vals)
        picks.append((hit, first))
    return ROUTED_SCALE * chosen / jnp.sum(chosen, axis=0, keepdims=True), picks


def _pack_bf16_pairs(x):
    c = x.shape[1] // 2
    bits = lambda v: lax.bitcast_convert_type(v.astype(BF16).astype(F32), jnp.uint32)
    return (bits(x[:, :c]) >> 16) | (bits(x[:, c:]) & jnp.uint32(0xFFFF0000))


def _unpack_bf16_pairs(w):
    lo = lax.bitcast_convert_type(w << 16, F32)
    hi = lax.bitcast_convert_type(w & jnp.uint32(0xFFFF0000), F32)
    return jnp.concatenate([lo, hi], axis=1)


def _finish_kernel(x_ref, pooled_ref, y_ref, gm_ref, g1_ref, shift_ref, scale_ref,
                   wlin_ref, pscale_ref, wpo_ref, wno_ref, wo_ref, n2_ref, wr_ref, br_ref,
                   *rest, tm, d, sparse):
    if sparse:
        tri_ref, x1_ref, u2_ref, up_ref, eid_ref, gk_ref, rank_ref, cnt_ref, carry_scr = rest
    else:
        x1_ref, u2_ref, gates_ref = rest
    x = x_ref[...].reshape(tm, d)
    pooled = pooled_ref[...].reshape(tm, POOL_W).astype(BF16)
    y_pool = _dot(pooled, wlin_ref[...]) * pscale_ref[...]
    a = _dot(y_pool.astype(BF16), wpo_ref[...])
    b = _dot(y_ref[...].reshape(tm, QPAD), wno_ref[...])
    gm = gm_ref[...].reshape(tm, 2 * d)
    merged = gm[:, :d] * a + gm[:, d:] * b
    x1 = x + _rows2d(g1_ref) * _dot(merged.astype(BF16), wo_ref[...])
    x1_ref[...] = x1.reshape(x1_ref.shape)
    u2 = _rmsnorm(x1, n2_ref[...]) * (1.0 + _rows2d(scale_ref)) + _rows2d(shift_ref)
    u2b = u2.astype(BF16)
    u2_ref[...] = u2b.reshape(u2_ref.shape)
    logits_t = _dot_nt(wr_ref[...], u2b)
    gates_t, picks = _route(logits_t[:N_EXPERTS], br_ref[...], tm)
    if not sparse:
        gates_t = jnp.concatenate([gates_t, jnp.zeros((LANES - N_EXPERTS, tm), F32)], axis=0)
        gates_ref[...] = gates_t.T.reshape(gates_ref.shape)
        return

    @pl.when((pl.program_id(0) == 0) & (pl.program_id(1) == 0))
    def _():
        carry_scr[...] = jnp.zeros_like(carry_scr)

    packed = _pack_bf16_pairs(u2)
    for s in range(up_ref.shape[0]):
        up_ref[s] = packed[:, s * SC_ROW_WORDS:(s + 1) * SC_ROW_WORDS]
    hit_all = picks[0][0]
    for hit, _ in picks[1:]:
        hit_all = hit_all | hit
    hits = jnp.where(hit_all, 1.0, 0.0).astype(BF16)
    before = _dot(hits, tri_ref[...]) + jnp.concatenate([carry_scr[...]] * (tm // LANES), axis=1)
    eids, gks, ranks = [], [], []
    for hit, first in picks:
        eids.append(first)
        gks.append(jnp.sum(jnp.where(hit, gates_t, 0.0), axis=0, keepdims=True))
        ranks.append(jnp.sum(jnp.where(hit, before, 0.0), axis=0, keepdims=True))
    pick_row = lax.broadcasted_iota(I32, (TOP_K, tm), 0)

    def stack(rows):
        out = jnp.zeros((TOP_K, tm), F32)
        for r, v in enumerate(rows):
            out = jnp.where(pick_row == r, v, out)
        return out

    eid_ref[...] = stack(eids).astype(I32).reshape(eid_ref.shape)
    gk_ref[...] = stack(gks).reshape(gk_ref.shape)
    rank_ref[...] = stack(ranks).astype(I32).reshape(rank_ref.shape)
    carry_scr[...] += _dot(hits, jnp.ones((tm, LANES), BF16))
    cnt_ref[...] = carry_scr[...]


def _finish(x3, pooled, ynsa, gm, mods, wts, *, tm, sparse):
    g, r, d = x3.shape
    nt = r // tm
    g1, shift2, scale2 = mods
    per_row = g1.ndim == 2
    tok = lambda w: pl.BlockSpec((1, tm, w), lambda b, j: (b, j, 0))
    tok_t = lambda rows: pl.BlockSpec((1, rows, tm), lambda b, j: (b, 0, j))
    if per_row:
        mod_spec = lambda col: pl.BlockSpec((tm, d), lambda b, j, col=col: (b * nt + j, col))
    else:
        mod_spec = lambda col: pl.BlockSpec((1, 1, d), lambda b, j, col=col: (b, 0, col))
    const = lambda a: pl.BlockSpec(a.shape, lambda b, j: (0,) * a.ndim)
    out_shape = [jax.ShapeDtypeStruct((g, r, d), F32), jax.ShapeDtypeStruct((g, r, d), BF16)]
    out_specs = [tok(d), tok(d)]
    scratch = []
    if sparse:
        tri = (jnp.arange(tm)[:, None] < jnp.arange(tm)[None, :]).astype(BF16)
        wts = tuple(wts) + (tri,)
        split = d // 2 // SC_ROW_WORDS
        out_shape += [jax.ShapeDtypeStruct((split, g * r, SC_ROW_WORDS), jnp.uint32),
                      jax.ShapeDtypeStruct((g, TOP_K, r), I32), jax.ShapeDtypeStruct((g, TOP_K, r), F32),
                      jax.ShapeDtypeStruct((g, TOP_K, r), I32), jax.ShapeDtypeStruct((N_EXPERTS, LANES), F32)]
        out_specs += [pl.BlockSpec((split, tm, SC_ROW_WORDS), lambda b, j: (0, b * nt + j, 0)),
                      tok_t(TOP_K), tok_t(TOP_K), tok_t(TOP_K),
                      pl.BlockSpec((N_EXPERTS, LANES), lambda b, j: (0, 0))]
        scratch.append(pltpu.VMEM((N_EXPERTS, LANES), F32))
    else:
        out_shape.append(jax.ShapeDtypeStruct((g, r, LANES), F32))
        out_specs.append(tok(LANES))
    return pl.pallas_call(
        functools.partial(_finish_kernel, tm=tm, d=d, sparse=sparse),
        out_shape=out_shape,
        grid=(g, nt),
        in_specs=[tok(d), tok(POOL_W), tok(QPAD), tok(2 * d), mod_spec(2), mod_spec(3), mod_spec(4)]
        + [const(w) for w in wts],
        out_specs=out_specs,
        scratch_shapes=scratch,
        compiler_params=_cparams(("arbitrary", "arbitrary")),
        name="finish_route" if sparse else "finish",
    )(x3, pooled, ynsa, gm, g1, shift2, scale2, *wts)


def _moe_kernel(u_ref, gates_ref, x1_ref, g2_ref, nf_ref, wg_ref, wu_ref, wd_ref, sg_ref, su_ref, sd_ref,
                y_ref, acc_ref, *, tm, d, eps):
    e = pl.program_id(2)
    u = u_ref[...].reshape(tm, d)

    @pl.when(e == 0)
    def _():
        hs = _silu(_dot(u, sg_ref[...])) * _dot(u, su_ref[...])
        acc_ref[...] = _dot(hs.astype(BF16), sd_ref[...])

    gates = gates_ref[...].reshape(tm, LANES)
    lane = lax.broadcasted_iota(I32, (1, LANES), 1)
    hidden = []
    for j in range(eps):
        h = _silu(_dot(u, wg_ref[j].astype(BF16))) * _dot(u, wu_ref[j].astype(BF16))
        gate = jnp.sum(jnp.where(lane == e * eps + j, gates, 0.0), axis=1, keepdims=True)
        hidden.append((h * gate).astype(BF16))
    f = wd_ref.shape[1]
    acc_ref[...] += _dot(jnp.concatenate(hidden, axis=1), wd_ref[...].reshape(eps * f, d).astype(BF16))

    @pl.when(e == pl.num_programs(2) - 1)
    def _():
        x2 = x1_ref[...].reshape(tm, d) + _rows2d(g2_ref) * acc_ref[...]
        y_ref[...] = _rmsnorm(x2, nf_ref[...]).reshape(y_ref.shape)


def _moe(u2, gates, x1, g2, normf, w_gate, w_up, w_down, sg, su, sd, *, tm):
    g, r, d = x1.shape
    nt = r // tm
    ne, _, f = w_gate.shape
    per_row = g2.ndim == 2
    tok = lambda w: pl.BlockSpec((1, tm, w), lambda b, j, e: (b, j, 0))
    if per_row:
        g2_spec = pl.BlockSpec((tm, d), lambda b, j, e: (b * nt + j, 5))
    else:
        g2_spec = pl.BlockSpec((1, 1, d), lambda b, j, e: (b, 0, 5))
    once = pl.Buffered(buffer_count=1)
    const = lambda a: pl.BlockSpec(a.shape, lambda b, j, e: (0,) * a.ndim, pipeline_mode=once)
    eps = 4
    return pl.pallas_call(
        functools.partial(_moe_kernel, tm=tm, d=d, eps=eps),
        out_shape=jax.ShapeDtypeStruct((g, r, d), F32),
        grid=(g, nt, ne // eps),
        in_specs=[tok(d), tok(LANES),
                  pl.BlockSpec((1, tm, d), lambda b, j, e: (b, j, 0), pipeline_mode=once),
                  g2_spec, const(normf),
                  pl.BlockSpec((eps, d, f), lambda b, j, e: (e, 0, 0)),
                  pl.BlockSpec((eps, d, f), lambda b, j, e: (e, 0, 0)),
                  pl.BlockSpec((eps, f, d), lambda b, j, e: (e, 0, 0)),
                  const(sg), const(su), const(sd)],
        out_specs=tok(d),
        scratch_shapes=[pltpu.VMEM((tm, d), F32)],
        compiler_params=_cparams(("arbitrary", "arbitrary", "arbitrary")),
        name="moe",
    )(u2, gates, x1, g2, normf, w_gate, w_up, w_down, sg, su, sd)


SC_WINDOW = 128
SC_ROW_WORDS = 256
MOE_ROWS = 512


def _sc_mesh():
    return plsc.VectorSubcoreMesh(core_axis_name="c", subcore_axis_name="s")


def _sc_scatter_rows(src, dst_idx, n_dst):
    n, w = src.shape
    nk = dst_idx.shape[0]

    @pl.kernel(out_type=jax.ShapeDtypeStruct((n_dst, w), src.dtype), mesh=_sc_mesh(), scratch_types=[])
    def scatter(src_hbm, idx_hbm, dst_hbm):
        def body(rows_vmem, idx_vmem):
            pltpu.sync_copy(rows_vmem, dst_hbm.at[idx_vmem.at[0]])

        pltpu.emit_pipeline(
            body,
            grid=(nk, n // SC_WINDOW),
            in_specs=[pl.BlockSpec((SC_WINDOW, w), index_map=lambda k, i: (i, 0)),
                      pl.BlockSpec((1, SC_WINDOW), index_map=lambda k, i: (k, i))],
            out_specs=[],
            core_axis_name=("c", "s"),
            dimension_semantics=(pltpu.PARALLEL, pltpu.PARALLEL),
        )(src_hbm, idx_hbm)

    return scatter(src, dst_idx)


def _sc_gather_rows(src, idx):
    n, w = idx.shape[0], src.shape[1]

    @pl.kernel(out_type=jax.ShapeDtypeStruct((n, w), src.dtype), mesh=_sc_mesh(), scratch_types=[])
    def gather(src_hbm, idx_hbm, out_hbm):
        def body(idx_vmem, out_vmem):
            pltpu.sync_copy(src_hbm.at[idx_vmem.at[0]], out_vmem)

        pltpu.emit_pipeline(
            body,
            grid=(n // SC_WINDOW,),
            in_specs=[pl.BlockSpec((1, SC_WINDOW), index_map=lambda i: (0, i))],
            out_specs=[pl.BlockSpec((SC_WINDOW, w), index_map=lambda i: (i, 0))],
            core_axis_name=("c", "s"),
            dimension_semantics=(pltpu.PARALLEL,),
        )(idx_hbm, out_hbm)

    return gather(src, idx.reshape(1, n))


MOE_BLOCKS_PER_STEP = 2


def _expert_rows_kernel(te_ref, nt_ref, x_ref, *refs):
    y_ref = refs[-1]
    split = x_ref.shape[0]
    for j in range(MOE_BLOCKS_PER_STEP):
        wg_ref, wu_ref, wd_ref = refs[3 * j:3 * j + 3]
        rows = slice(j * MOE_ROWS, (j + 1) * MOE_ROWS)

        @pl.when(pl.program_id(0) * MOE_BLOCKS_PER_STEP + j < nt_ref[0])
        def _():
            x = _unpack_bf16_pairs(jnp.concatenate([x_ref[s, rows] for s in range(split)], axis=1)).astype(BF16)
            h = _silu(_dot(x, wg_ref[...].astype(BF16))) * _dot(x, wu_ref[...].astype(BF16))
            y = _pack_bf16_pairs(_dot(h.astype(BF16), wd_ref[...].astype(BF16)))
            for s in range(split):
                y_ref[s, rows] = y[:, s * SC_ROW_WORDS:(s + 1) * SC_ROW_WORDS]


def _expert_rows(tile_expert, n_tiles, x_sorted, w_gate, w_up, w_down):
    split, p, words = x_sorted.shape
    ne, d, f = w_gate.shape
    bps = MOE_BLOCKS_PER_STEP
    wspec = lambda a, b, j: pl.BlockSpec((None, a, b), lambda i, te, nt: (te[i * bps + j], 0, 0))
    rows = pl.BlockSpec((split, bps * MOE_ROWS, words), lambda i, te, nt: (0, i, 0))
    weights, wspecs = [], []
    for j in range(bps):
        weights += [w_gate, w_up, w_down]
        wspecs += [wspec(d, f, j), wspec(d, f, j), wspec(f, d, j)]
    grid_spec = pltpu.PrefetchScalarGridSpec(
        num_scalar_prefetch=2,
        grid=(p // (bps * MOE_ROWS),),
        in_specs=[rows] + wspecs,
        out_specs=rows,
    )
    return pl.pallas_call(
        _expert_rows_kernel,
        out_shape=jax.ShapeDtypeStruct((split, p, words), jnp.uint32),
        grid_spec=grid_spec,
        compiler_params=_cparams(("arbitrary",)),
        name="expert_rows",
    )(tile_expert, n_tiles, x_sorted, *weights)


def _combine_kernel(yg_ref, gk_ref, u_ref, x1_ref, g2_ref, nf_ref, sg_ref, su_ref, sd_ref, y_ref, *, tm, d):
    u = u_ref[...].reshape(tm, d)
    hs = _silu(_dot(u, sg_ref[...])) * _dot(u, su_ref[...])
    acc = _dot(hs.astype(BF16), sd_ref[...])
    gk = gk_ref[...].reshape(tm, LANES)
    lane = lax.broadcasted_iota(I32, (1, LANES), 1)
    split = yg_ref.shape[0]
    for k in range(TOP_K):
        gate = jnp.sum(jnp.where(lane == k, gk, 0.0), axis=1, keepdims=True)
        words = jnp.concatenate([yg_ref[s, k] for s in range(split)], axis=1)
        acc = acc + gate * _unpack_bf16_pairs(words)
    x2 = x1_ref[...].reshape(tm, d) + _rows2d(g2_ref) * acc
    y_ref[...] = _rmsnorm(x2, nf_ref[...]).reshape(y_ref.shape)


def _combine_kernel_into(yg_ref, gk_ref, u_ref, x1_ref, g2_ref, nf_ref, sg_ref, su_ref, sd_ref, prev_ref, y_ref,
                         *, tm, d):
    del prev_ref
    _combine_kernel(yg_ref, gk_ref, u_ref, x1_ref, g2_ref, nf_ref, sg_ref, su_ref, sd_ref, y_ref, tm=tm, d=d)


def _combine(yg, gk, u2, x1, g2, normf, sg, su, sd, *, tm, group, prev):
    g, r, d = x1.shape
    split, _, _, words = yg.shape
    tok = lambda w: pl.BlockSpec((1, tm, w), lambda j: (group, j, 0))
    const = lambda a: pl.BlockSpec(a.shape, lambda j: (0,) * a.ndim)
    in_specs = [pl.BlockSpec((split, TOP_K, tm, words), lambda j: (0, 0, j, 0)),
                tok(LANES), tok(d), tok(d),
                pl.BlockSpec((1, 1, d), lambda j: (group, 0, 5)), const(normf), const(sg), const(su), const(sd)]
    args = [yg, gk, u2, x1, g2, normf, sg, su, sd]
    kern, aliases = _combine_kernel, {}
    if prev is not None:
        in_specs.append(pl.BlockSpec(memory_space=pl.ANY))
        args.append(prev)
        kern, aliases = _combine_kernel_into, {len(args) - 1: 0}
    return pl.pallas_call(
        functools.partial(kern, tm=tm, d=d),
        out_shape=jax.ShapeDtypeStruct((g, r, d), F32),
        grid=(r // tm,),
        in_specs=in_specs,
        out_specs=tok(d),
        input_output_aliases=aliases,
        compiler_params=_cparams(("arbitrary",)),
        name="moe_combine",
    )(*args)


def _moe_sorted(u2, u2p, eid_t, gk_t, rank_t, counts, x1, g2, normf, w_gate, w_up, w_down, sg, su, sd):
    g, r, d = x1.shape
    n = g * r
    ne = w_gate.shape[0]
    half = d // 2
    split = half // SC_ROW_WORDS
    cnt = counts[:, 0].astype(I32)
    padded = -(-cnt // MOE_ROWS) * MOE_ROWS
    seg_end = jnp.cumsum(padded)
    seg_start = seg_end - padded
    p_rows = n * TOP_K + ne * MOE_ROWS
    eid = eid_t.transpose(1, 0, 2).reshape(TOP_K, n)
    start = jnp.sum(jnp.where(eid[:, :, None] == jnp.arange(ne, dtype=I32), seg_start, 0), axis=-1)
    pos = start + rank_t.transpose(1, 0, 2).reshape(TOP_K, n)
    first_row = jnp.arange(p_rows // MOE_ROWS, dtype=I32) * MOE_ROWS
    tile_expert = jnp.minimum(jnp.sum(seg_end[None, :] <= first_row[:, None], axis=1), ne - 1).astype(I32)
    n_tiles = (seg_end[-1:] // MOE_ROWS).astype(I32)
    scat_idx = jnp.concatenate([pos + s * p_rows for s in range(split)], axis=1)
    x_sorted = _sc_scatter_rows(u2p.reshape(split * n, SC_ROW_WORDS), scat_idx, split * p_rows)
    y_sorted = _expert_rows(tile_expert, n_tiles, x_sorted.reshape(split, p_rows, SC_ROW_WORDS),
                            w_gate, w_up, w_down).reshape(split * p_rows, SC_ROW_WORDS)
    gk = jnp.pad(gk_t.transpose(0, 2, 1), ((0, 0), (0, 0), (0, LANES - TOP_K)))
    out = None
    for b in range(g):
        pos_b = pos[:, b * r:(b + 1) * r].reshape(-1)
        yg = _sc_gather_rows(y_sorted, jnp.concatenate([pos_b + s * p_rows for s in range(split)]))
        out = _combine(yg.reshape(split, TOP_K, r, SC_ROW_WORDS), gk, u2, x1, g2, normf, sg, su, sd,
                       tm=512, group=b, prev=out)
    return out


def _kv_slot_mask():
    return (jnp.arange(N_HEADS)[:, None] // GROUP == jnp.arange(N_KV)[None, :]).astype(F32)


def _prep_w_in(w_in, d):
    q0 = POOL_W
    kv0 = q0 + N_HEADS * HEAD_DIM
    gn0 = kv0 + 6 * KVW
    gm0 = gn0 + 3 * N_HEADS
    wq = w_in[:, q0:kv0].reshape(d, N_HEADS, 1, HEAD_DIM) * (HEAD_DIM ** -0.5)
    wq = (wq * _kv_slot_mask()[None, :, :, None]).reshape(d, QPAD)
    wgn = jnp.pad(w_in[:, gn0:gm0], ((0, 0), (0, LANES - 3 * N_HEADS)))
    return jnp.concatenate([w_in[:, :q0], wq, w_in[:, kv0:gn0], wgn, w_in[:, gm0:]], axis=1).astype(BF16)


def _prep_w_nsa_out(w, d):
    w = w.reshape(N_HEADS, 1, HEAD_DIM, d) * _kv_slot_mask()[:, :, None, None]
    return w.reshape(QPAD, d).astype(BF16)


def _block_diag(w_lin):
    g, c, _ = w_lin.shape
    eye = jnp.eye(g, dtype=F32)
    return (w_lin[:, :, None, :] * eye[:, None, :, None]).reshape(g * c, g * c).astype(BF16)


def kernel(x_prompt, x_sample, cache_kc, cache_vc, cache_ks, cache_vs, state_kw, state_vw, state_pool,
           page_table, c_prompt, c_sample, norm1_g, norm2_g, normf_g, w_ada, b_ada, w_in, w_pool_lin,
           pool_scale, w_cmp_k, w_cmp_v, w_pool_out, w_nsa_out, w_o, w_router, b_router, w_gate, w_up,
           w_down, ws_gate, ws_up, ws_down):
    depth = w_in.shape[0]
    assert depth == 1, "single-layer stack"
    bsz, seq, d = x_prompt.shape
    n_seq, ns, _ = x_sample.shape
    wbuf = state_kw.shape[2]
    lyr = 0

    c_all = jnp.concatenate([c_prompt, c_sample], axis=0)
    rows = c_all.shape[0]
    rows_p = -(-rows // 8) * 8
    mod = _adaln(jnp.pad(c_all, ((0, rows_p - rows), (0, 0))), w_ada[lyr], b_ada[lyr])
    mod_p = mod[:bsz].reshape(bsz, 1, 6 * d)
    mod_s = jnp.repeat(mod[bsz:bsz + n_seq], ns, axis=0)

    w2 = _prep_w_in(w_in[lyr], d)
    g1n = norm1_g[lyr].reshape(1, d)
    wk = w_cmp_k[lyr].reshape(CMP_BLOCK, KVW)
    wv = w_cmp_v[lyr].reshape(CMP_BLOCK, KVW)
    fin_w = (_block_diag(w_pool_lin[lyr]), pool_scale[lyr].reshape(1, POOL_W), w_pool_out[lyr].astype(BF16),
             _prep_w_nsa_out(w_nsa_out[lyr], d), w_o[lyr].astype(BF16), norm2_g[lyr].reshape(1, d),
             jnp.pad(w_router[lyr].T, ((0, LANES - N_EXPERTS), (0, 0))).astype(BF16),
             b_router[lyr].reshape(N_EXPERTS, 1))
    moe_w = (w_gate[lyr], w_up[lyr], w_down[lyr], ws_gate[lyr].astype(BF16), ws_up[lyr].astype(BF16),
             ws_down[lyr].astype(BF16))
    nf = normf_g.reshape(1, d)

    tm_p = 512
    (vp, kc, vc, ks, vs, kw, vw, gm, ksb, kwb, vst, vwt, qt, gst, pooled) = _in_proj(
        x_prompt, mod_p, mod_p, g1n, w2, tm=tm_p, prompt=True)
    kcmp, vcmpt = _compress(kc, vc, wk, wv)
    ynsa = _nsa_prompt(qt, gst, kcmp, vcmpt, ksb, vst, kwb, vwt)
    x1, u2, u2p, eid_t, gk_t, rank_t, counts = _finish(
        x_prompt, pooled, ynsa, gm, (mod_p, mod_p, mod_p), fin_w, tm=tm_p, sparse=True)
    y_prompt = _moe_sorted(u2, u2p, eid_t, gk_t, rank_t, counts, x1, mod_p, nf, *moe_w)

    n_tok = n_seq * ns
    xs3 = x_sample.reshape(1, n_tok, d)
    tm_s = 128
    (vp_s, kc_s, vc_s, ks_s, vs_s, kw_s, vw_s, gm_s, q_s, gs_s) = _in_proj(
        xs3, mod_s, mod_s, g1n, w2, tm=tm_s, prompt=False)
    two = lambda a: a.reshape(n_tok, a.shape[-1])
    q_rows = q_s.reshape(n_tok * N_HEADS, LANES)
    gate_rows = two(gs_s)[:, :3 * N_HEADS].reshape(n_tok, 3, N_HEADS).transpose(0, 2, 1)
    gate_rows = jnp.pad(gate_rows.reshape(n_tok * N_HEADS, 3), ((0, 0), (0, LANES - 3)))
    n_pool = cache_kc.shape[1]
    page = cache_kc.shape[2]
    rows_minor = lambda a: jnp.transpose(a, (0, 2, 3, 1)).reshape(a.shape[0], KVW, a.shape[1])
    caches = [rows_minor(c[lyr]) for c in (cache_kc, cache_vc, cache_ks, cache_vs)]
    o_rows, pooled_s = _nsa_sample(
        page_table, q_rows, gate_rows, [two(a) for a in (kc_s, vc_s, ks_s, vs_s, kw_s, vw_s)], two(vp_s),
        rows_minor(state_kw[lyr]), rows_minor(state_vw[lyr]), state_pool[lyr], wk, wv, caches)
    ynsa_s = o_rows.reshape(1, n_tok, QPAD)
    x1_s, u2_s, gates_s = _finish(xs3, pooled_s.reshape(1, n_tok, POOL_W), ynsa_s, gm_s,
                                  (mod_s, mod_s, mod_s), fin_w, tm=tm_s, sparse=False)
    y_sample = _moe(u2_s, gates_s, x1_s, mod_s, nf, *moe_w, tm=n_tok).reshape(n_seq, ns, d)

    kvp = lambda a: a.reshape(1, bsz, seq, N_KV, HEAD_DIM)
    tailp = lambda a: jnp.pad(a, ((0, 0), (wbuf, 0), (0, 0)))[:, -wbuf:].reshape(1, bsz, wbuf, N_KV, HEAD_DIM)
    kvs = lambda a: a.reshape(1, n_seq, ns, N_KV, HEAD_DIM)
    wins = lambda st, new: jnp.concatenate(
        [st[lyr], new.reshape(n_seq, ns, N_KV, HEAD_DIM)], axis=1)[None, :, -wbuf:]
    pool_p = vp[:, -POOL_BUF:][None]
    pool_s = jnp.concatenate([state_pool[lyr], vp_s.reshape(n_seq, ns, POOL_W)], axis=1)[None, :, -POOL_BUF:]
    return (y_prompt, y_sample, kvp(kc), kvp(vc), kvp(ks), kvp(vs), tailp(kw), tailp(vw), pool_p,
            kvs(kc_s), kvs(vc_s), kvs(ks_s), kvs(vs_s), wins(state_kw, kw_s), wins(state_vw, vw_s), pool_s)
```

```python
import functools

import jax
import jax.numpy as jnp
from jax import lax
from jax.experimental import pallas as pl
from jax.experimental.pallas import tpu as pltpu
from jax.experimental.pallas import tpu_sc as plsc

F32 = jnp.float32
BF16 = jnp.bfloat16
I32 = jnp.int32

POOL_WINDOWS = (2, 4, 8, 16)
POOL_GW = 64
POOL_W = 256
POOL_BUF = 15
N_HEADS = 8
HEAD_DIM = 64
N_KV = 2
GROUP = N_HEADS // N_KV
CMP_STRIDE = 16
CMP_BLOCK = 32
SEL_BLOCK = 64
TOP_BLOCKS = 16
WINDOW = 512
Q_BLOCK = 128
FORCE_SCORE = 1e4
N_EXPERTS = 64
N_EGROUPS = 8
EXPERTS_PER_GROUP = N_EXPERTS // N_EGROUPS
TOPK_GROUPS = 4
TOP_K = 8
ROUTED_SCALE = 2.5
EPS = 1e-6
NEG = -1e30
SLOPES = tuple(2.0 ** (-8.0 * (h + 1.0) / N_HEADS) for h in range(N_HEADS))

LANES = 128
QPAD = N_HEADS * LANES
KVW = N_KV * HEAD_DIM
VMEM_LIMIT = 56 * 1024 * 1024


def _cparams(sem):
    return pltpu.CompilerParams(dimension_semantics=sem, vmem_limit_bytes=VMEM_LIMIT)


def _dot(a, b):
    return jnp.dot(a, b, preferred_element_type=F32)


def _dot_nt(a, b):
    return lax.dot_general(a, b, (((1,), (1,)), ((), ())), preferred_element_type=F32)


def _dot_exact(a, b):
    return jnp.dot(a, b, preferred_element_type=F32, precision=lax.Precision.HIGHEST)


def _rows2d(ref):
    v = ref[...]
    return v.reshape(v.shape[-2], v.shape[-1])


def _rmsnorm(x, g):
    return x * lax.rsqrt(jnp.mean(x * x, axis=-1, keepdims=True) + EPS) * g


def _silu(x):
    return x * jax.nn.sigmoid(x)


def _adaln_kernel(c_ref, w_ref, b_ref, o_ref):
    s = _silu(c_ref[...]).astype(BF16)
    o_ref[...] = _dot(s, w_ref[...].astype(BF16)) + b_ref[...]


def _adaln(c, w_ada, b_ada):
    rows, d = c.shape
    n = w_ada.shape[1]
    tn = 512
    return pl.pallas_call(
        _adaln_kernel,
        out_shape=jax.ShapeDtypeStruct((rows, n), F32),
        grid=(n // tn,),
        in_specs=[pl.BlockSpec((rows, d), lambda j: (0, 0)),
                  pl.BlockSpec((d, tn), lambda j: (0, j)),
                  pl.BlockSpec((1, tn), lambda j: (0, j))],
        out_specs=pl.BlockSpec((rows, tn), lambda j: (0, j)),
        compiler_params=_cparams(("arbitrary",)),
        name="adaln",
    )(c, w_ada, b_ada.reshape(1, n))


_C_VP = 0
_C_Q = _C_VP + POOL_W
_C_KV = _C_Q + QPAD
_C_GN = _C_KV + 6 * KVW
_C_GM = _C_GN + LANES


def _pool_window_sums(ext, tm):
    s2 = ext + pltpu.roll(ext, 1, 0)
    s4 = s2 + pltpu.roll(s2, 2, 0)
    s8 = s4 + pltpu.roll(s4, 4, 0)
    s16 = s8 + pltpu.roll(s8, 8, 0)
    grp = lax.broadcasted_iota(I32, (1, POOL_W), 1) // POOL_GW
    pick = jnp.where(grp == 0, s2, jnp.where(grp == 1, s4, jnp.where(grp == 2, s8, s16)))
    return pick[16:16 + tm]


def _in_proj_kernel(x_ref, shift_ref, scale_ref, g_ref, w_ref,
                    vp_ref, kc_ref, vc_ref, ks_ref, vs_ref, kw_ref, vw_ref, gm_ref, *rest, tm, d, prompt):
    x = x_ref[...].reshape(tm, d)
    u = _rmsnorm(x, g_ref[...]) * (1.0 + _rows2d(scale_ref)) + _rows2d(shift_ref)
    ub = u.astype(BF16)

    head = _dot(ub, w_ref[:, 0:_C_GM])

    def proj(c0, n):
        return head[:, c0:c0 + n] if c0 + n <= _C_GM else _dot(ub, w_ref[:, c0:c0 + n])

    vp = proj(_C_VP, POOL_W)
    vp_ref[...] = vp.reshape(vp_ref.shape)
    kv = []
    for n, o32 in enumerate((kc_ref, vc_ref, ks_ref, vs_ref, kw_ref, vw_ref)):
        v = proj(_C_KV + n * KVW, KVW)
        o32[...] = v.reshape(o32.shape)
        kv.append(v)
    gm_ref[...] = jax.nn.sigmoid(proj(_C_GM, 2 * d)).reshape(gm_ref.shape)
    gs = jax.nn.sigmoid(proj(_C_GN, LANES))

    if not prompt:
        q_ref, gs_ref = rest
        q_ref[...] = proj(_C_Q, QPAD).astype(BF16).reshape(q_ref.shape)
        gs_ref[...] = gs.reshape(gs_ref.shape)
    else:
        ksb_ref, kwb_ref, vst_ref, vwt_ref, qt_ref, gst_ref, pooled_ref, halo_ref = rest
        ksb_ref[...] = kv[2].astype(BF16).reshape(ksb_ref.shape)
        kwb_ref[...] = kv[4].astype(BF16).reshape(kwb_ref.shape)
        vst_ref[...] = kv[3].T.astype(BF16).reshape(vst_ref.shape)
        vwt_ref[...] = kv[5].T.astype(BF16).reshape(vwt_ref.shape)
        gst_ref[...] = gs.T.reshape(gst_ref.shape)
        for h in range(N_HEADS):
            qt_ref[0, h] = proj(_C_Q + h * LANES, LANES).T.astype(BF16)
        j = pl.program_id(1)

        @pl.when(j == 0)
        def _():
            halo_ref[...] = jnp.zeros_like(halo_ref)

        ext = jnp.concatenate([halo_ref[...], vp], axis=0)
        sums = _pool_window_sums(ext, tm)
        pos = j * tm + lax.broadcasted_iota(I32, (tm, 1), 0)
        wcol = 2 << (lax.broadcasted_iota(I32, (1, POOL_W), 1) // POOL_GW)
        cnt = jnp.minimum(pos + 1, wcol).astype(F32)
        pooled_ref[...] = (sums / cnt - vp).astype(BF16).reshape(pooled_ref.shape)
        halo_ref[...] = vp[tm - 16:tm]


def _in_proj(x3, shift, scale, g1, w2, *, tm, prompt):
    g, r, d = x3.shape
    nt = r // tm
    per_row = shift.ndim == 2

    def tok(width, dtype):
        return (jax.ShapeDtypeStruct((g, r, width), dtype),
                pl.BlockSpec((1, tm, width), lambda b, j: (b, j, 0)))

    def tok_t(rows, dtype):
        return (jax.ShapeDtypeStruct((g, rows, r), dtype),
                pl.BlockSpec((1, rows, tm), lambda b, j: (b, 0, j)))

    outs = [tok(POOL_W, F32)] + [tok(KVW, F32)] * 6 + [tok(2 * d, F32)]
    scratch = []
    if prompt:
        outs += [tok(KVW, BF16), tok(KVW, BF16), tok_t(KVW, BF16), tok_t(KVW, BF16)]
        outs.append((jax.ShapeDtypeStruct((g, N_HEADS, LANES, r), BF16),
                     pl.BlockSpec((1, N_HEADS, LANES, tm), lambda b, j: (b, 0, 0, j))))
        outs += [tok_t(LANES, F32), tok(POOL_W, BF16)]
        scratch.append(pltpu.VMEM((16, POOL_W), F32))
    else:
        outs += [tok(QPAD, BF16), tok(LANES, F32)]
    if per_row:
        mod_spec = lambda col: pl.BlockSpec((tm, d), lambda b, j, col=col: (b * nt + j, col))
    else:
        mod_spec = lambda col: pl.BlockSpec((1, 1, d), lambda b, j, col=col: (b, 0, col))
    kern = functools.partial(_in_proj_kernel, tm=tm, d=d, prompt=prompt)
    return pl.pallas_call(
        kern,
        out_shape=[o[0] for o in outs],
        grid=(g, nt),
        in_specs=[pl.BlockSpec((1, tm, d), lambda b, j: (b, j, 0)),
                  mod_spec(0), mod_spec(1),
                  pl.BlockSpec((1, d), lambda b, j: (0, 0)),
                  pl.BlockSpec(w2.shape, lambda b, j: (0, 0))],
        out_specs=[o[1] for o in outs],
        scratch_shapes=scratch,
        compiler_params=_cparams(("arbitrary", "arbitrary")),
        name="in_proj_prompt" if prompt else "in_proj_sample",
    )(x3, shift, scale, g1, w2)


def _compress_kernel(kc_ref, vc_ref, wk_ref, wv_ref, okc_ref, ovc_ref, sh_ref, *, nc):
    last = lax.broadcasted_iota(I32, (nc, 1), 0) == nc - 1
    for src, w_ref, dst in ((kc_ref, wk_ref, okc_ref), (vc_ref, wv_ref, ovc_ref)):
        head = jnp.zeros((nc, KVW), F32)
        tail = jnp.zeros((nc, KVW), F32)
        for r in range(CMP_STRIDE):
            rows = src[pl.ds(r, nc, stride=CMP_STRIDE), :]
            head = head + rows * w_ref[r:r + 1, :]
            tail = tail + rows * w_ref[CMP_STRIDE + r:CMP_STRIDE + r + 1, :]
        sh_ref[0:nc, :] = tail
        sh_ref[nc:nc + 8, :] = jnp.zeros((8, KVW), F32)
        out = jnp.where(last, 0.0, head + sh_ref[1:nc + 1, :])
        dst[...] = (out if dst is okc_ref else out.T).astype(BF16)


def _compress(kc, vc, wk, wv):
    b, s, _ = kc.shape
    nc = s // CMP_STRIDE
    big = pl.BlockSpec((None, s, KVW), lambda i: (i, 0, 0))
    wsp = pl.BlockSpec((CMP_BLOCK, KVW), lambda i: (0, 0))
    return pl.pallas_call(
        functools.partial(_compress_kernel, nc=nc),
        out_shape=[jax.ShapeDtypeStruct((b, nc, KVW), BF16), jax.ShapeDtypeStruct((b, KVW, nc), BF16)],
        grid=(b,),
        in_specs=[big, big, wsp, wsp],
        out_specs=[pl.BlockSpec((None, nc, KVW), lambda i: (i, 0, 0)),
                   pl.BlockSpec((None, KVW, nc), lambda i: (i, 0, 0))],
        scratch_shapes=[pltpu.VMEM((nc + 8, KVW), F32)],
        compiler_params=_cparams(("arbitrary",)),
        name="compress",
    )(kc, vc, wk, wv)


def _topk_mask(vals, blk_f, n_top, axis=1):
    sel = jnp.zeros(vals.shape, F32)
    big = float(vals.shape[axis])
    for _ in range(n_top):
        mx = jnp.max(vals, axis=axis, keepdims=True)
        first = jnp.min(jnp.where(vals == mx, blk_f, big), axis=axis, keepdims=True)
        hit = blk_f == first
        sel = jnp.where(hit, 1.0, sel)
        vals = jnp.where(hit, -jnp.inf, vals)
    return sel


def _topk_mask_by_rank(vals, blk, n_valid, n_top):
    rank = jnp.zeros(vals.shape, F32)
    for j in range(n_valid):
        vj = vals[:, j:j + 1]
        beats = (vj > vals) | ((vj == vals) & (blk > j))
        rank = rank + jnp.where(beats, 1.0, 0.0)
    return jnp.where(rank < float(n_top), 1.0, 0.0)


def _pos_features(pos):
    hi = (pos // SEL_BLOCK).astype(F32)[:, None]
    lo = (pos % SEL_BLOCK).astype(F32)[:, None]
    return jnp.concatenate([hi, lo, jnp.zeros((pos.shape[0], LANES - 2), F32)], axis=1).astype(BF16)


def _importance_matrix(nc, nsel):
    j = jnp.arange(nc)[:, None]
    s = jnp.arange(nsel)[None, :]
    r = SEL_BLOCK // CMP_STRIDE
    a = (j >= r * s) & (j <= r * s + r - 1)
    b = (j + 1 >= r * s) & (j + 1 <= r * s + r - 1)
    return a.astype(F32) + b.astype(F32)


def _nsa_prompt_kernel(qt_ref, gst_ref, kc_ref, vct_ref, ks_ref, vst_ref, kw_ref, vwt_ref,
                       cfeat_ref, wfeat_ref, qfeat_ref, slope_ref,
                       y_ref, qk_scr, m_scr, l_scr, acc_scr, o_scr, sel_scr, imp_scr, flag_scr, *, seq, tk, wl):
    i = pl.program_id(1)
    q0 = i * Q_BLOCK
    nq = Q_BLOCK
    gq = GROUP * nq
    nc = kc_ref.shape[0]
    nsel = seq // SEL_BLOCK
    n_top = min(TOP_BLOCKS, nsel)
    blk_per_tile = tk // SEL_BLOCK
    qpos = q0 + lax.broadcasted_iota(I32, (1, nq), 1)
    gst = gst_ref[...]

    crow = lax.broadcasted_iota(I32, (nc, nq), 0)
    cend = crow * CMP_STRIDE + (CMP_BLOCK - 1)
    mask_c = qpos >= cend
    kc = jnp.concatenate([kc_ref[...], cfeat_ref[...]], axis=1)
    vct = vct_ref[...]
    blk = lax.broadcasted_iota(I32, (nsel, nq), 0)
    blk_f = blk.astype(F32)
    cur = qpos // SEL_BLOCK
    forced = (blk == 0) | (blk == cur) | (blk == cur - 1)
    visible = blk * SEL_BLOCK <= qpos
    ws = pl.multiple_of(jnp.maximum(q0 - WINDOW, 0), Q_BLOCK)
    wpos = ws + lax.broadcasted_iota(I32, (wl, nq), 0)
    valid_w = lax.bitcast_convert_type(qpos - wpos, jnp.uint32) < WINDOW
    n_tiles = (q0 + nq + tk - 1) // tk
    half_rows = lax.broadcasted_iota(I32, (KVW, nq), 0) // HEAD_DIM
    tile_pos = lax.broadcasted_iota(I32, (SEL_BLOCK, nq), 0)

    def lanes4(x):
        return jnp.concatenate([x] * GROUP, axis=1)

    def gate_row(branch, k):
        r0 = branch * N_HEADS + k * GROUP
        return jnp.concatenate([gst[r0 + g:r0 + g + 1] for g in range(GROUP)], axis=1)

    mask_c4 = lanes4(mask_c)
    valid_w4 = lanes4(valid_w)
    kwt = jnp.concatenate([kw_ref[pl.ds(ws, wl), :], wfeat_ref[...]], axis=1)
    vwtt = vwt_ref[:, pl.ds(ws, wl)]

    for k in range(N_KV):
        for g in range(GROUP):
            qk_scr[k, 0:LANES, g * nq:(g + 1) * nq] = qt_ref[k * GROUP + g]
        qk_scr[k, LANES:2 * LANES, :] = qfeat_ref[k]
        qk = qk_scr[k]

        s = jnp.where(mask_c4, _dot(kc, qk), NEG)
        e = jnp.where(mask_c4, jnp.exp(s - jnp.max(s, axis=0, keepdims=True)), 0.0)
        l = jnp.sum(e, axis=0, keepdims=True)
        p = e * jnp.where(l > 0.0, 1.0 / l, 0.0)
        o_c = _dot(vct, p.astype(BF16))
        psum = p[:, 0:nq]
        for g in range(1, GROUP):
            psum = psum + p[:, g * nq:(g + 1) * nq]

        a = psum + jnp.where(crow == 0, 0.0, pltpu.roll(psum, 1, 0))
        a = a + pltpu.roll(a, nc - 1, 0)
        imp_scr[...] = a + pltpu.roll(a, nc - 2, 0)
        imp = imp_scr[pl.ds(0, nsel, stride=nc // nsel), :]
        vals = jnp.where(visible, jnp.where(forced, FORCE_SCORE, imp), NEG)
        sel = jnp.where(visible, _topk_mask(vals, blk_f, n_top, axis=0), 0.0)
        sel_scr[k] = jnp.where(sel > 0.5, 0.0, NEG)
        blk_any = jnp.max(sel, axis=1, keepdims=True)
        for t in range(seq // tk):
            hit = (jnp.max(blk_any[t * blk_per_tile:(t + 1) * blk_per_tile, :]) > 0.5).astype(I32)
            flag_scr[t] = hit if k == 0 else flag_scr[t] | hit

        s = jnp.where(valid_w4, _dot(kwt, qk), NEG)
        e = jnp.exp(s - jnp.max(s, axis=0, keepdims=True))
        p = e / jnp.sum(e, axis=0, keepdims=True)
        o_w = _dot(vwtt, p.astype(BF16))
        o_scr[k] = gate_row(0, k) * o_c + gate_row(2, k) * o_w

    m_scr[...] = jnp.full(m_scr.shape, NEG, F32)
    l_scr[...] = jnp.zeros(l_scr.shape, F32)
    acc_scr[...] = jnp.zeros(acc_scr.shape, F32)

    def sel_tile(t):
        k0 = pl.multiple_of(t * tk, tk)
        kt = jnp.concatenate([ks_ref[pl.ds(k0, tk), :], wfeat_ref[0:tk, :]], axis=1)
        vtt = vst_ref[:, pl.ds(k0, tk)]
        causal = [qpos >= k0 + j * SEL_BLOCK + tile_pos for j in range(blk_per_tile)]
        base = (k0 - q0).astype(F32)
        scores = _dot(kt, jnp.concatenate([qk_scr[k] for k in range(N_KV)], axis=1))
        probs, alphas = [], []
        for k in range(N_KV):
            neg = []
            for j in range(blk_per_tile):
                row = sel_scr[k, pl.ds(t * blk_per_tile + j, 1), :]
                neg.append(jnp.where(causal[j], jnp.broadcast_to(row, (SEL_BLOCK, nq)), NEG))
            neg = lanes4(jnp.concatenate(neg, axis=0))
            off = slope_ref[k] * base
            s = scores[:, k * gq:(k + 1) * gq] + neg
            m_old = m_scr[k]
            m_new = jnp.maximum(m_old, jnp.max(s, axis=0, keepdims=True) + off)
            alpha = jnp.exp(m_old - m_new)
            p = jnp.exp(s - (m_new - off))
            l_scr[k] = alpha * l_scr[k] + jnp.sum(p, axis=0, keepdims=True)
            m_scr[k] = m_new
            probs.append(p.astype(BF16))
            alphas.append(alpha)
        pv = _dot(vtt, jnp.concatenate(probs, axis=1))
        for k in range(N_KV):
            acc_scr[k] = acc_scr[k] * alphas[k] + pv[:, k * gq:(k + 1) * gq]

    def sel_body(t, carry):
        pl.when(flag_scr[t] > 0)(functools.partial(sel_tile, t))
        return carry

    lax.fori_loop(0, n_tiles, sel_body, 0)

    for k in range(N_KV):
        o = o_scr[k] + gate_row(1, k) * (acc_scr[k] / l_scr[k])
        for g in range(GROUP):
            h = k * GROUP + g
            oh = jnp.where(half_rows == k, o[:, g * nq:(g + 1) * nq], 0.0)
            y_ref[:, h * LANES:(h + 1) * LANES] = oh.T.astype(BF16)


def _nsa_prompt(qt, gst, kcmp, vcmpt, ksb, vst, kwb, vwt):
    b, _, _, s = qt.shape
    nq = Q_BLOCK
    gq = GROUP * nq
    nc = kcmp.shape[1]
    nsel = s // SEL_BLOCK
    tk = 256
    wl = WINDOW + Q_BLOCK
    assert s % tk == 0 and s >= wl
    assert s // SEL_BLOCK <= 2 * LANES, "position // 64 must stay exact in bf16"
    cfeat = _pos_features(jnp.arange(nc) * CMP_STRIDE + (CMP_BLOCK - 1))
    wfeat = _pos_features(jnp.arange(wl))
    slope_rows = jnp.repeat(jnp.asarray(SLOPES, F32).reshape(N_KV, 1, GROUP), nq, axis=2)
    qfeat = jnp.concatenate([slope_rows * SEL_BLOCK, slope_rows, jnp.zeros((N_KV, LANES - 2, gq), F32)],
                            axis=1).astype(BF16)
    rows = lambda r: pl.BlockSpec((None, r, KVW), lambda bi, i: (bi, 0, 0))
    cols = lambda c: pl.BlockSpec((None, KVW, c), lambda bi, i: (bi, 0, 0))
    const = lambda a: pl.BlockSpec(a.shape, lambda bi, i: (0,) * a.ndim)
    return pl.pallas_call(
        functools.partial(_nsa_prompt_kernel, seq=s, tk=tk, wl=wl),
        out_shape=jax.ShapeDtypeStruct((b, s, QPAD), BF16),
        grid=(b, s // nq),
        in_specs=[pl.BlockSpec((None, N_HEADS, LANES, nq), lambda bi, i: (bi, 0, 0, i)),
                  pl.BlockSpec((None, LANES, nq), lambda bi, i: (bi, 0, i)),
                  rows(nc), cols(nc), rows(s), cols(s), rows(s), cols(s),
                  const(cfeat), const(wfeat), const(qfeat), const(slope_rows)],
        out_specs=pl.BlockSpec((None, nq, QPAD), lambda bi, i: (bi, i, 0)),
        scratch_shapes=[pltpu.VMEM((N_KV, 2 * LANES, gq), BF16),
                        pltpu.VMEM((N_KV, 1, gq), F32),
                        pltpu.VMEM((N_KV, 1, gq), F32),
                        pltpu.VMEM((N_KV, KVW, gq), F32),
                        pltpu.VMEM((N_KV, KVW, gq), F32),
                        pltpu.VMEM((N_KV, nsel, nq), F32),
                        pltpu.VMEM((nc, nq), F32),
                        pltpu.SMEM((s // tk,), I32)],
        compiler_params=_cparams(("arbitrary", "arbitrary")),
        name="nsa_prompt",
    )(qt, gst, kcmp, vcmpt, ksb, vst, kwb, vwt, cfeat, wfeat, qfeat, slope_rows)


def _nsa_sample_kernel(pt_ref, q_ref, gate_ref, kcn_ref, vcn_ref, ksn_ref, vsn_ref, kwn_ref, vwn_ref,
                       vpn_ref, skw_ref, svw_ref, spool_ref, wk_ref, wv_ref, imat_ref, emat_ref,
                       ckc_ref, cvc_ref, cks_ref, cvs_ref,
                       o_ref, pooled_ref, buf, buft, win_scr, tail_scr, vext_scr, sem,
                       *, sb, ns, past, n_pages, page, n_seq, ncv, ncp, nks, wls, nselp, n_sel):
    step = pl.program_id(0)
    nrow = ns * N_HEADS

    def copies(n, slot):
        out = []
        for p in range(n_pages):
            pg = pt_ref[n * n_pages + p]
            for c, cref in enumerate((ckc_ref, cvc_ref, cks_ref, cvs_ref)):
                out.append(pltpu.make_async_copy(cref.at[pg], buft.at[slot, c, :, pl.ds(p * page, page)],
                                                 sem.at[slot]))
        return out

    @pl.when(step == 0)
    def _():
        buf[:, past:, :] = jnp.zeros((2, buf.shape[1] - past, KVW), F32)
        tail_scr[...] = jnp.zeros_like(tail_scr)
        vext_scr[...] = jnp.zeros_like(vext_scr)
        for cp in copies(0, 0):
            cp.start()

    def new_rows_t(ref, r4):
        tail_scr[0:ns, :] = ref[pl.ds(r4, ns), :]
        return tail_scr[...].T

    row = lax.broadcasted_iota(I32, (nrow, 1), 0)
    hrow = row % N_HEADS
    qpos = past + row // N_HEADS
    slope = jnp.exp2(-8.0 * (hrow.astype(F32) + 1.0) / N_HEADS)
    kvrow = hrow // GROUP
    lane = lax.broadcasted_iota(I32, (1, LANES), 1)
    half = (lane // HEAD_DIM) == kvrow
    grow = (row // N_HEADS) * N_KV + kvrow
    row8 = lax.broadcasted_iota(I32, (ns * N_KV, 1), 0)
    qpos8 = past + lax.broadcasted_iota(I32, (ns * N_KV, 1), 0) // N_KV
    blk = lax.broadcasted_iota(I32, (1, nselp), 1)
    blk_f = blk.astype(F32)
    cur = qpos8 // SEL_BLOCK
    forced = (blk == 0) | (blk == cur) | (blk == cur - 1)
    visible = (blk * SEL_BLOCK <= qpos8)
    inrange = blk < n_sel
    cend = lax.broadcasted_iota(I32, (1, ncp), 1) * CMP_STRIDE + (CMP_BLOCK - 1)
    mask_c = qpos >= cend
    bias_c = slope * (cend - qpos).astype(F32)
    kpos = lax.broadcasted_iota(I32, (1, nks), 1)
    causal_s = qpos >= kpos
    bias_s = slope * (kpos - qpos).astype(F32)
    wbuf = wls[0]
    wpos = past - wbuf + lax.broadcasted_iota(I32, (1, wls[1]), 1)
    dw = qpos - wpos
    valid_w = lax.bitcast_convert_type(dw, jnp.uint32) < WINDOW
    bias_w = slope * (wpos - qpos).astype(F32)
    prow = lax.broadcasted_iota(I32, (vext_scr.shape[0], 1), 0)
    wcol = 2 << (lax.broadcasted_iota(I32, (1, POOL_W), 1) // POOL_GW)
    n_top = min(TOP_BLOCKS, n_sel)

    def softmax_rows(s, mask):
        s = jnp.where(mask, s, NEG)
        mx = jnp.max(s, axis=1, keepdims=True)
        e = jnp.where(mask, jnp.exp(s - mx), 0.0)
        l = jnp.sum(e, axis=1, keepdims=True)
        return e * jnp.where(l > 0.0, 1.0 / l, 0.0)

    def seq_body(r):
        n = step * sb + r
        slot = r % 2

        @pl.when(n + 1 < n_seq)
        def _():
            for cp in copies(n + 1, 1 - slot):
                cp.start()

        for cp in copies(n, slot):
            cp.wait()

        r4 = r * ns
        for c, new_ref in enumerate((kcn_ref, vcn_ref)):
            for p in range(n_pages):
                buf[c, p * page:(p + 1) * page, :] = buft[slot, c, :, p * page:(p + 1) * page].T
            buf[c, past:past + ns, :] = new_ref[pl.ds(r4, ns), :]
        for c, new_ref in ((2, ksn_ref), (3, vsn_ref)):
            buft[slot, c, :, past:past + LANES] = new_rows_t(new_ref, r4)

        qall = q_ref[pl.ds(r * nrow, nrow), :]
        gates = gate_ref[pl.ds(r * nrow, nrow), :]

        cmp = []
        for c, w_ref in ((0, wk_ref), (1, wv_ref)):
            span = CMP_STRIDE * ncv
            lo = buf[c, 0:span, :].reshape(ncv, CMP_STRIDE, KVW) * w_ref[0:CMP_STRIDE, :][None]
            hi = (buf[c, CMP_STRIDE:CMP_STRIDE + span, :].reshape(ncv, CMP_STRIDE, KVW)
                  * w_ref[CMP_STRIDE:CMP_BLOCK, :][None])
            acc = jnp.sum(lo + hi, axis=1)
            cmp.append(jnp.concatenate([acc, jnp.zeros((ncp - ncv, KVW), F32)], axis=0).astype(BF16))
        p_c = softmax_rows(_dot_nt(qall, cmp[0]) + bias_c, mask_c)
        o_c = _dot(p_c.astype(BF16), cmp[1])

        psum = jnp.zeros((ns * N_KV, ncp), F32)
        for i in range(ns * N_KV):
            r0 = (i // N_KV) * N_HEADS + (i % N_KV) * GROUP
            psum = jnp.where(row8 == i, jnp.sum(p_c[r0:r0 + GROUP], axis=0, keepdims=True), psum)
        imp = _dot_exact(psum, imat_ref[...])
        vals = jnp.where(inrange, jnp.where(visible, jnp.where(forced, FORCE_SCORE, imp), NEG), -jnp.inf)
        sel8 = _topk_mask_by_rank(vals, blk, n_sel, n_top)
        sel_rows = jnp.zeros((nrow, nselp), F32)
        for i in range(ns * N_KV):
            sel_rows = jnp.where(grow == i, sel8[i:i + 1], sel_rows)
        chosen = _dot(sel_rows.astype(BF16), emat_ref[...])

        kst = buft[slot, 2].astype(BF16)
        vst = buft[slot, 3].astype(BF16)
        p_s = softmax_rows(_dot(qall, kst) + bias_s, causal_s & (chosen > 0.5))
        o_s = _dot_nt(p_s.astype(BF16), vst)

        outs_w = []
        for state_ref, new_ref in ((skw_ref, kwn_ref), (svw_ref, vwn_ref)):
            win_scr[:, 0:wbuf] = state_ref[r]
            win_scr[:, wbuf:wbuf + LANES] = new_rows_t(new_ref, r4)
            outs_w.append(win_scr[...].astype(BF16))
        p_w = softmax_rows(_dot(qall, outs_w[0]) + bias_w, valid_w)
        o_w = _dot_nt(p_w.astype(BF16), outs_w[1])

        o = gates[:, 0:1] * o_c + gates[:, 1:2] * o_s + gates[:, 2:3] * o_w
        o_ref[pl.ds(r * nrow, nrow), :] = jnp.where(half, o, 0.0).astype(BF16)

        vext_scr[0:POOL_BUF, :] = spool_ref[r]
        vext_scr[POOL_BUF:POOL_BUF + ns, :] = vpn_ref[pl.ds(r4, ns), :]
        ext = vext_scr[...]
        for t in range(ns):
            hi = POOL_BUF + t
            inwin = (prow <= hi) & (prow > hi - wcol)
            ssum = jnp.sum(jnp.where(inwin, ext, 0.0), axis=0, keepdims=True)
            cnt = jnp.minimum(past + t + 1, wcol).astype(F32)
            pooled_ref[pl.ds(r4 + t, 1), :] = ssum / cnt - ext[hi:hi + 1, :]

    for r in range(sb):
        seq_body(r)


def _nsa_sample(page_table, q_rows, gate_rows, new6, vp_new, state_kwt, state_vwt, state_pool, wk, wv, caches):
    n_seq, n_pages = page_table.shape
    page = caches[0].shape[2]
    past = n_pages * page
    ns = vp_new.shape[0] // n_seq
    wbuf = state_kwt.shape[2]
    sb = 2
    nrow = ns * N_HEADS
    assert ns <= SEL_BLOCK and page == LANES
    t_pad = -(-(past + ns) // SEL_BLOCK) * SEL_BLOCK
    n_cmp = t_pad // CMP_STRIDE - 1
    ncv = -(-n_cmp // 8) * 8
    ncp = -(-ncv // LANES) * LANES
    nks = past + LANES
    n_sel = t_pad // SEL_BLOCK
    nselp = LANES
    assert n_sel <= nselp
    wlp = wbuf + LANES
    buf_rows = -(-(CMP_STRIDE * ncv + CMP_STRIDE) // 8) * 8
    imat = _importance_matrix(ncp, nselp)
    emat = (jnp.arange(nselp)[:, None] == (jnp.arange(nks)[None, :] // SEL_BLOCK)).astype(BF16)

    seqblk = lambda rows, w: pl.BlockSpec((sb * rows, w), lambda i, pt: (i, 0))
    const = lambda a: pl.BlockSpec(a.shape, lambda i, pt: (0,) * a.ndim)
    kern = functools.partial(
        _nsa_sample_kernel, sb=sb, ns=ns, past=past, n_pages=n_pages, page=page, n_seq=n_seq,
        ncv=ncv, ncp=ncp, nks=nks, wls=(wbuf, wlp), nselp=nselp, n_sel=n_sel)
    grid_spec = pltpu.PrefetchScalarGridSpec(
        num_scalar_prefetch=1,
        grid=(n_seq // sb,),
        in_specs=[seqblk(nrow, LANES), seqblk(nrow, LANES)] + [seqblk(ns, KVW)] * 6 + [seqblk(ns, POOL_W)]
        + [pl.BlockSpec((sb, KVW, wbuf), lambda i, pt: (i, 0, 0))] * 2
        + [pl.BlockSpec((sb, POOL_BUF, POOL_W), lambda i, pt: (i, 0, 0))]
        + [const(wk), const(wv), const(imat), const(emat)]
        + [pl.BlockSpec(memory_space=pl.ANY)] * 4,
        out_specs=[seqblk(nrow, LANES), seqblk(ns, POOL_W)],
        scratch_shapes=[pltpu.VMEM((2, buf_rows, KVW), F32),
                        pltpu.VMEM((2, 4, KVW, nks), F32),
                        pltpu.VMEM((KVW, wlp), F32),
                        pltpu.VMEM((LANES, KVW), F32),
                        pltpu.VMEM((24, POOL_W), F32),
                        pltpu.SemaphoreType.DMA((2,))],
    )
    return pl.pallas_call(
        kern,
        out_shape=[jax.ShapeDtypeStruct((n_seq * nrow, LANES), BF16),
                   jax.ShapeDtypeStruct((n_seq * ns, POOL_W), F32)],
        grid_spec=grid_spec,
        compiler_params=_cparams(("arbitrary",)),
        name="nsa_sample",
    )(page_table.reshape(-1), q_rows, gate_rows, *new6, vp_new, state_kwt, state_vwt, state_pool, wk, wv,
      imat, emat, *caches)


def _route(logits_t, bias_col, tm):
    sc = jax.nn.sigmoid(logits_t)
    biased = sc + bias_col
    epg = EXPERTS_PER_GROUP
    row8 = lax.broadcasted_iota(I32, (epg, tm), 0).astype(F32)
    ninf = -jnp.inf
    grp = jnp.zeros((N_EGROUPS, tm), F32)
    for g in range(N_EGROUPS):
        bg = biased[g * epg:(g + 1) * epg]
        m1 = jnp.max(bg, axis=0, keepdims=True)
        first = jnp.min(jnp.where(bg == m1, row8, float(epg)), axis=0, keepdims=True)
        m2 = jnp.max(jnp.where(row8 == first, ninf, bg), axis=0, keepdims=True)
        grp = jnp.where(row8 == float(g), m1 + m2, grp)
    keep = jnp.zeros((N_EGROUPS, tm), F32)
    vals = grp
    for _ in range(TOPK_GROUPS):
        mx = jnp.max(vals, axis=0, keepdims=True)
        first = jnp.min(jnp.where(vals == mx, row8, float(N_EGROUPS)), axis=0, keepdims=True)
        hit = row8 == first
        keep = jnp.where(hit, 1.0, keep)
        vals = jnp.where(hit, ninf, vals)
    masked = jnp.concatenate(
        [jnp.where(keep[g:g + 1] > 0.5, biased[g * epg:(g + 1) * epg], NEG) for g in range(N_EGROUPS)], axis=0)
    rowe = lax.broadcasted_iota(I32, (N_EXPERTS, tm), 0).astype(F32)
    chosen = jnp.zeros((N_EXPERTS, tm), F32)
    vals = masked
    picks = []
    for _ in range(TOP_K):
        mx = jnp.max(vals, axis=0, keepdims=True)
        first = jnp.min(jnp.where(vals == mx, rowe, float(N_EXPERTS)), axis=0, keepdims=True)
        hit = rowe == first
        chosen = jnp.where(hit, sc, chosen)
        vals = jnp.where(hit, ninf, vals)
        picks.append((hit, first))
    return ROUTED_SCALE * chosen / jnp.sum(chosen, axis=0, keepdims=True), picks


def _pack_bf16_pairs(x):
    c = x.shape[1] // 2
    bits = lambda v: lax.bitcast_convert_type(v.astype(BF16).astype(F32), jnp.uint32)
    return (bits(x[:, :c]) >> 16) | (bits(x[:, c:]) & jnp.uint32(0xFFFF0000))


def _unpack_bf16_pairs(w):
    lo = lax.bitcast_convert_type(w << 16, F32)
    hi = lax.bitcast_convert_type(w & jnp.uint32(0xFFFF0000), F32)
    return jnp.concatenate([lo, hi], axis=1)


def _finish_kernel(x_ref, pooled_ref, y_ref, gm_ref, g1_ref, shift_ref, scale_ref,
                   wlin_ref, pscale_ref, wpo_ref, wno_ref, wo_ref, n2_ref, wr_ref, br_ref,
                   *rest, tm, d, sparse):
    if sparse:
        tri_ref, x1_ref, u2_ref, up_ref, eid_ref, gk_ref, rank_ref, cnt_ref, carry_scr = rest
    else:
        x1_ref, u2_ref, gates_ref = rest
    x = x_ref[...].reshape(tm, d)
    pooled = pooled_ref[...].reshape(tm, POOL_W).astype(BF16)
    y_pool = _dot(pooled, wlin_ref[...]) * pscale_ref[...]
    a = _dot(y_pool.astype(BF16), wpo_ref[...])
    b = _dot(y_ref[...].reshape(tm, QPAD), wno_ref[...])
    gm = gm_ref[...].reshape(tm, 2 * d)
    merged = gm[:, :d] * a + gm[:, d:] * b
    x1 = x + _rows2d(g1_ref) * _dot(merged.astype(BF16), wo_ref[...])
    x1_ref[...] = x1.reshape(x1_ref.shape)
    u2 = _rmsnorm(x1, n2_ref[...]) * (1.0 + _rows2d(scale_ref)) + _rows2d(shift_ref)
    u2b = u2.astype(BF16)
    u2_ref[...] = u2b.reshape(u2_ref.shape)
    logits_t = _dot_nt(wr_ref[...], u2b)
    gates_t, picks = _route(logits_t[:N_EXPERTS], br_ref[...], tm)
    if not sparse:
        gates_t = jnp.concatenate([gates_t, jnp.zeros((LANES - N_EXPERTS, tm), F32)], axis=0)
        gates_ref[...] = gates_t.T.reshape(gates_ref.shape)
        return

    @pl.when((pl.program_id(0) == 0) & (pl.program_id(1) == 0))
    def _():
        carry_scr[...] = jnp.zeros_like(carry_scr)

    packed = _pack_bf16_pairs(u2)
    for s in range(up_ref.shape[0]):
        up_ref[s] = packed[:, s * SC_ROW_WORDS:(s + 1) * SC_ROW_WORDS]
    hit_all = picks[0][0]
    for hit, _ in picks[1:]:
        hit_all = hit_all | hit
    hits = jnp.where(hit_all, 1.0, 0.0).astype(BF16)
    before = _dot(hits, tri_ref[...]) + jnp.concatenate([carry_scr[...]] * (tm // LANES), axis=1)
    eids, gks, ranks = [], [], []
    for hit, first in picks:
        eids.append(first)
        gks.append(jnp.sum(jnp.where(hit, gates_t, 0.0), axis=0, keepdims=True))
        ranks.append(jnp.sum(jnp.where(hit, before, 0.0), axis=0, keepdims=True))
    pick_row = lax.broadcasted_iota(I32, (TOP_K, tm), 0)

    def stack(rows):
        out = jnp.zeros((TOP_K, tm), F32)
        for r, v in enumerate(rows):
            out = jnp.where(pick_row == r, v, out)
        return out

    eid_ref[...] = stack(eids).astype(I32).reshape(eid_ref.shape)
    gk_ref[...] = stack(gks).reshape(gk_ref.shape)
    rank_ref[...] = stack(ranks).astype(I32).reshape(rank_ref.shape)
    carry_scr[...] += _dot(hits, jnp.ones((tm, LANES), BF16))
    cnt_ref[...] = carry_scr[...]


def _finish(x3, pooled, ynsa, gm, mods, wts, *, tm, sparse):
    g, r, d = x3.shape
    nt = r // tm
    g1, shift2, scale2 = mods
    per_row = g1.ndim == 2
    tok = lambda w: pl.BlockSpec((1, tm, w), lambda b, j: (b, j, 0))
    tok_t = lambda rows: pl.BlockSpec((1, rows, tm), lambda b, j: (b, 0, j))
    if per_row:
        mod_spec = lambda col: pl.BlockSpec((tm, d), lambda b, j, col=col: (b * nt + j, col))
    else:
        mod_spec = lambda col: pl.BlockSpec((1, 1, d), lambda b, j, col=col: (b, 0, col))
    const = lambda a: pl.BlockSpec(a.shape, lambda b, j: (0,) * a.ndim)
    out_shape = [jax.ShapeDtypeStruct((g, r, d), F32), jax.ShapeDtypeStruct((g, r, d), BF16)]
    out_specs = [tok(d), tok(d)]
    scratch = []
    if sparse:
        tri = (jnp.arange(tm)[:, None] < jnp.arange(tm)[None, :]).astype(BF16)
        wts = tuple(wts) + (tri,)
        split = d // 2 // SC_ROW_WORDS
        out_shape += [jax.ShapeDtypeStruct((split, g * r, SC_ROW_WORDS), jnp.uint32),
                      jax.ShapeDtypeStruct((g, TOP_K, r), I32), jax.ShapeDtypeStruct((g, TOP_K, r), F32),
                      jax.ShapeDtypeStruct((g, TOP_K, r), I32), jax.ShapeDtypeStruct((N_EXPERTS, LANES), F32)]
        out_specs += [pl.BlockSpec((split, tm, SC_ROW_WORDS), lambda b, j: (0, b * nt + j, 0)),
                      tok_t(TOP_K), tok_t(TOP_K), tok_t(TOP_K),
                      pl.BlockSpec((N_EXPERTS, LANES), lambda b, j: (0, 0))]
        scratch.append(pltpu.VMEM((N_EXPERTS, LANES), F32))
    else:
        out_shape.append(jax.ShapeDtypeStruct((g, r, LANES), F32))
        out_specs.append(tok(LANES))
    return pl.pallas_call(
        functools.partial(_finish_kernel, tm=tm, d=d, sparse=sparse),
        out_shape=out_shape,
        grid=(g, nt),
        in_specs=[tok(d), tok(POOL_W), tok(QPAD), tok(2 * d), mod_spec(2), mod_spec(3), mod_spec(4)]
        + [const(w) for w in wts],
        out_specs=out_specs,
        scratch_shapes=scratch,
        compiler_params=_cparams(("arbitrary", "arbitrary")),
        name="finish_route" if sparse else "finish",
    )(x3, pooled, ynsa, gm, g1, shift2, scale2, *wts)


def _moe_kernel(u_ref, gates_ref, x1_ref, g2_ref, nf_ref, wg_ref, wu_ref, wd_ref, sg_ref, su_ref, sd_ref,
                y_ref, acc_ref, *, tm, d, eps):
    e = pl.program_id(2)
    u = u_ref[...].reshape(tm, d)

    @pl.when(e == 0)
    def _():
        hs = _silu(_dot(u, sg_ref[...])) * _dot(u, su_ref[...])
        acc_ref[...] = _dot(hs.astype(BF16), sd_ref[...])

    gates = gates_ref[...].reshape(tm, LANES)
    lane = lax.broadcasted_iota(I32, (1, LANES), 1)
    hidden = []
    for j in range(eps):
        h = _silu(_dot(u, wg_ref[j].astype(BF16))) * _dot(u, wu_ref[j].astype(BF16))
        gate = jnp.sum(jnp.where(lane == e * eps + j, gates, 0.0), axis=1, keepdims=True)
        hidden.append((h * gate).astype(BF16))
    f = wd_ref.shape[1]
    acc_ref[...] += _dot(jnp.concatenate(hidden, axis=1), wd_ref[...].reshape(eps * f, d).astype(BF16))

    @pl.when(e == pl.num_programs(2) - 1)
    def _():
        x2 = x1_ref[...].reshape(tm, d) + _rows2d(g2_ref) * acc_ref[...]
        y_ref[...] = _rmsnorm(x2, nf_ref[...]).reshape(y_ref.shape)


def _moe(u2, gates, x1, g2, normf, w_gate, w_up, w_down, sg, su, sd, *, tm):
    g, r, d = x1.shape
    nt = r // tm
    ne, _, f = w_gate.shape
    per_row = g2.ndim == 2
    tok = lambda w: pl.BlockSpec((1, tm, w), lambda b, j, e: (b, j, 0))
    if per_row:
        g2_spec = pl.BlockSpec((tm, d), lambda b, j, e: (b * nt + j, 5))
    else:
        g2_spec = pl.BlockSpec((1, 1, d), lambda b, j, e: (b, 0, 5))
    once = pl.Buffered(buffer_count=1)
    const = lambda a: pl.BlockSpec(a.shape, lambda b, j, e: (0,) * a.ndim, pipeline_mode=once)
    eps = 4
    return pl.pallas_call(
        functools.partial(_moe_kernel, tm=tm, d=d, eps=eps),
        out_shape=jax.ShapeDtypeStruct((g, r, d), F32),
        grid=(g, nt, ne // eps),
        in_specs=[tok(d), tok(LANES),
                  pl.BlockSpec((1, tm, d), lambda b, j, e: (b, j, 0), pipeline_mode=once),
                  g2_spec, const(normf),
                  pl.BlockSpec((eps, d, f), lambda b, j, e: (e, 0, 0)),
                  pl.BlockSpec((eps, d, f), lambda b, j, e: (e, 0, 0)),
                  pl.BlockSpec((eps, f, d), lambda b, j, e: (e, 0, 0)),
                  const(sg), const(su), const(sd)],
        out_specs=tok(d),
        scratch_shapes=[pltpu.VMEM((tm, d), F32)],
        compiler_params=_cparams(("arbitrary", "arbitrary", "arbitrary")),
        name="moe",
    )(u2, gates, x1, g2, normf, w_gate, w_up, w_down, sg, su, sd)


SC_WINDOW = 128
SC_ROW_WORDS = 256
MOE_ROWS = 512


def _sc_mesh():
    return plsc.VectorSubcoreMesh(core_axis_name="c", subcore_axis_name="s")


def _sc_scatter_rows(src, dst_idx, n_dst):
    n, w = src.shape
    nk = dst_idx.shape[0]

    @pl.kernel(out_type=jax.ShapeDtypeStruct((n_dst, w), src.dtype), mesh=_sc_mesh(), scratch_types=[])
    def scatter(src_hbm, idx_hbm, dst_hbm):
        def body(rows_vmem, idx_vmem):
            pltpu.sync_copy(rows_vmem, dst_hbm.at[idx_vmem.at[0]])

        pltpu.emit_pipeline(
            body,
            grid=(nk, n // SC_WINDOW),
            in_specs=[pl.BlockSpec((SC_WINDOW, w), index_map=lambda k, i: (i, 0)),
                      pl.BlockSpec((1, SC_WINDOW), index_map=lambda k, i: (k, i))],
            out_specs=[],
            core_axis_name=("c", "s"),
            dimension_semantics=(pltpu.PARALLEL, pltpu.PARALLEL),
        )(src_hbm, idx_hbm)

    return scatter(src, dst_idx)


def _sc_gather_rows(src, idx):
    n, w = idx.shape[0], src.shape[1]

    @pl.kernel(out_type=jax.ShapeDtypeStruct((n, w), src.dtype), mesh=_sc_mesh(), scratch_types=[])
    def gather(src_hbm, idx_hbm, out_hbm):
        def body(idx_vmem, out_vmem):
            pltpu.sync_copy(src_hbm.at[idx_vmem.at[0]], out_vmem)

        pltpu.emit_pipeline(
            body,
            grid=(n // SC_WINDOW,),
            in_specs=[pl.BlockSpec((1, SC_WINDOW), index_map=lambda i: (0, i))],
            out_specs=[pl.BlockSpec((SC_WINDOW, w), index_map=lambda i: (i, 0))],
            core_axis_name=("c", "s"),
            dimension_semantics=(pltpu.PARALLEL,),
        )(idx_hbm, out_hbm)

    return gather(src, idx.reshape(1, n))


MOE_BLOCKS_PER_STEP = 4


def _expert_rows_kernel(te_ref, nt_ref, x_ref, *refs):
    y_ref = refs[-1]
    split = x_ref.shape[0]
    for j in range(MOE_BLOCKS_PER_STEP):
        wg_ref, wu_ref, wd_ref = refs[3 * j:3 * j + 3]
        rows = slice(j * MOE_ROWS, (j + 1) * MOE_ROWS)

        @pl.when(pl.program_id(0) * MOE_BLOCKS_PER_STEP + j < nt_ref[0])
        def _():
            x = _unpack_bf16_pairs(jnp.concatenate([x_ref[s, rows] for s in range(split)], axis=1)).astype(BF16)
            h = _silu(_dot(x, wg_ref[...].astype(BF16))) * _dot(x, wu_ref[...].astype(BF16))
            y = _pack_bf16_pairs(_dot(h.astype(BF16), wd_ref[...].astype(BF16)))
            for s in range(split):
                y_ref[s, rows] = y[:, s * SC_ROW_WORDS:(s + 1) * SC_ROW_WORDS]


def _expert_rows(tile_expert, n_tiles, x_sorted, w_gate, w_up, w_down):
    split, p, words = x_sorted.shape
    ne, d, f = w_gate.shape
    bps = MOE_BLOCKS_PER_STEP
    wspec = lambda a, b, j: pl.BlockSpec((None, a, b), lambda i, te, nt: (te[i * bps + j], 0, 0))
    rows = pl.BlockSpec((split, bps * MOE_ROWS, words), lambda i, te, nt: (0, i, 0))
    weights, wspecs = [], []
    for j in range(bps):
        weights += [w_gate, w_up, w_down]
        wspecs += [wspec(d, f, j), wspec(d, f, j), wspec(f, d, j)]
    grid_spec = pltpu.PrefetchScalarGridSpec(
        num_scalar_prefetch=2,
        grid=(p // (bps * MOE_ROWS),),
        in_specs=[rows] + wspecs,
        out_specs=rows,
    )
    return pl.pallas_call(
        _expert_rows_kernel,
        out_shape=jax.ShapeDtypeStruct((split, p, words), jnp.uint32),
        grid_spec=grid_spec,
        compiler_params=_cparams(("arbitrary",)),
        name="expert_rows",
    )(tile_expert, n_tiles, x_sorted, *weights)


def _combine_kernel(yg_ref, gk_ref, u_ref, x1_ref, g2_ref, nf_ref, sg_ref, su_ref, sd_ref, y_ref, *, tm, d):
    u = u_ref[...].reshape(tm, d)
    hs = _silu(_dot(u, sg_ref[...])) * _dot(u, su_ref[...])
    acc = _dot(hs.astype(BF16), sd_ref[...])
    gk = gk_ref[...].reshape(tm, LANES)
    lane = lax.broadcasted_iota(I32, (1, LANES), 1)
    split = yg_ref.shape[0]
    for k in range(TOP_K):
        gate = jnp.sum(jnp.where(lane == k, gk, 0.0), axis=1, keepdims=True)
        words = jnp.concatenate([yg_ref[s, k] for s in range(split)], axis=1)
        acc = acc + gate * _unpack_bf16_pairs(words)
    x2 = x1_ref[...].reshape(tm, d) + _rows2d(g2_ref) * acc
    y_ref[...] = _rmsnorm(x2, nf_ref[...]).reshape(y_ref.shape)


def _combine(yg, gk, u2, x1, g2, normf, sg, su, sd, *, tm):
    g, r, d = x1.shape
    nt = r // tm
    split, _, _, words = yg.shape
    tok = lambda w: pl.BlockSpec((1, tm, w), lambda b, j: (b, j, 0))
    const = lambda a: pl.BlockSpec(a.shape, lambda b, j: (0,) * a.ndim)
    return pl.pallas_call(
        functools.partial(_combine_kernel, tm=tm, d=d),
        out_shape=jax.ShapeDtypeStruct((g, r, d), F32),
        grid=(g, nt),
        in_specs=[pl.BlockSpec((split, TOP_K, tm, words), lambda b, j: (0, 0, b * nt + j, 0)),
                  tok(LANES), tok(d), tok(d),
                  pl.BlockSpec((1, 1, d), lambda b, j: (b, 0, 5)), const(normf), const(sg), const(su), const(sd)],
        out_specs=tok(d),
        compiler_params=_cparams(("arbitrary", "arbitrary")),
        name="moe_combine",
    )(yg, gk, u2, x1, g2, normf, sg, su, sd)


def _moe_sorted_experts(u2p, eid_t, rank_t, counts, w_gate, w_up, w_down):
    split, n, _ = u2p.shape
    ne = w_gate.shape[0]
    cnt = counts[:, 0].astype(I32)
    padded = -(-cnt // MOE_ROWS) * MOE_ROWS
    seg_end = jnp.cumsum(padded)
    seg_start = seg_end - padded
    p_rows = n * TOP_K + ne * MOE_ROWS
    eid = eid_t.transpose(1, 0, 2).reshape(TOP_K, n)
    start = jnp.sum(jnp.where(eid[:, :, None] == jnp.arange(ne, dtype=I32), seg_start, 0), axis=-1)
    pos = start + rank_t.transpose(1, 0, 2).reshape(TOP_K, n)
    first_row = jnp.arange(p_rows // MOE_ROWS, dtype=I32) * MOE_ROWS
    tile_expert = jnp.minimum(jnp.sum(seg_end[None, :] <= first_row[:, None], axis=1), ne - 1).astype(I32)
    n_tiles = (seg_end[-1:] // MOE_ROWS).astype(I32)
    scat_idx = jnp.concatenate([pos + s * p_rows for s in range(split)], axis=1)
    gath_idx = jnp.concatenate([pos.reshape(-1) + s * p_rows for s in range(split)])
    x_sorted = _sc_scatter_rows(u2p.reshape(split * n, SC_ROW_WORDS), scat_idx, split * p_rows)
    y_sorted = _expert_rows(tile_expert, n_tiles, x_sorted.reshape(split, p_rows, SC_ROW_WORDS),
                            w_gate, w_up, w_down)
    return y_sorted.reshape(split * p_rows, SC_ROW_WORDS), gath_idx


def _moe_sorted_combine(y_sorted, gath_idx, gk_t, u2, x1, g2, normf, sg, su, sd):
    g, r, _ = x1.shape
    split = gath_idx.shape[0] // (TOP_K * g * r)
    yg = _sc_gather_rows(y_sorted, gath_idx)
    gk = jnp.pad(gk_t.transpose(0, 2, 1), ((0, 0), (0, 0), (0, LANES - TOP_K)))
    return _combine(yg.reshape(split, TOP_K, g * r, SC_ROW_WORDS), gk, u2, x1, g2, normf, sg, su, sd, tm=512)


def _kv_slot_mask():
    return (jnp.arange(N_HEADS)[:, None] // GROUP == jnp.arange(N_KV)[None, :]).astype(F32)


def _prep_w_in(w_in, d):
    q0 = POOL_W
    kv0 = q0 + N_HEADS * HEAD_DIM
    gn0 = kv0 + 6 * KVW
    gm0 = gn0 + 3 * N_HEADS
    wq = w_in[:, q0:kv0].reshape(d, N_HEADS, 1, HEAD_DIM) * (HEAD_DIM ** -0.5)
    wq = (wq * _kv_slot_mask()[None, :, :, None]).reshape(d, QPAD)
    wgn = jnp.pad(w_in[:, gn0:gm0], ((0, 0), (0, LANES - 3 * N_HEADS)))
    return jnp.concatenate([w_in[:, :q0], wq, w_in[:, kv0:gn0], wgn, w_in[:, gm0:]], axis=1).astype(BF16)


def _prep_w_nsa_out(w, d):
    w = w.reshape(N_HEADS, 1, HEAD_DIM, d) * _kv_slot_mask()[:, :, None, None]
    return w.reshape(QPAD, d).astype(BF16)


def _block_diag(w_lin):
    g, c, _ = w_lin.shape
    eye = jnp.eye(g, dtype=F32)
    return (w_lin[:, :, None, :] * eye[:, None, :, None]).reshape(g * c, g * c).astype(BF16)


def kernel(x_prompt, x_sample, cache_kc, cache_vc, cache_ks, cache_vs, state_kw, state_vw, state_pool,
           page_table, c_prompt, c_sample, norm1_g, norm2_g, normf_g, w_ada, b_ada, w_in, w_pool_lin,
           pool_scale, w_cmp_k, w_cmp_v, w_pool_out, w_nsa_out, w_o, w_router, b_router, w_gate, w_up,
           w_down, ws_gate, ws_up, ws_down):
    depth = w_in.shape[0]
    assert depth == 1, "single-layer stack"
    bsz, seq, d = x_prompt.shape
    n_seq, ns, _ = x_sample.shape
    wbuf = state_kw.shape[2]
    lyr = 0

    c_all = jnp.concatenate([c_prompt, c_sample], axis=0)
    rows = c_all.shape[0]
    rows_p = -(-rows // 8) * 8
    mod = _adaln(jnp.pad(c_all, ((0, rows_p - rows), (0, 0))), w_ada[lyr], b_ada[lyr])
    mod_p = mod[:bsz].reshape(bsz, 1, 6 * d)
    mod_s = jnp.repeat(mod[bsz:bsz + n_seq], ns, axis=0)

    w2 = _prep_w_in(w_in[lyr], d)
    g1n = norm1_g[lyr].reshape(1, d)
    wk = w_cmp_k[lyr].reshape(CMP_BLOCK, KVW)
    wv = w_cmp_v[lyr].reshape(CMP_BLOCK, KVW)
    fin_w = (_block_diag(w_pool_lin[lyr]), pool_scale[lyr].reshape(1, POOL_W), w_pool_out[lyr].astype(BF16),
             _prep_w_nsa_out(w_nsa_out[lyr], d), w_o[lyr].astype(BF16), norm2_g[lyr].reshape(1, d),
             jnp.pad(w_router[lyr].T, ((0, LANES - N_EXPERTS), (0, 0))).astype(BF16),
             b_router[lyr].reshape(N_EXPERTS, 1))
    moe_w = (w_gate[lyr], w_up[lyr], w_down[lyr], ws_gate[lyr].astype(BF16), ws_up[lyr].astype(BF16),
             ws_down[lyr].astype(BF16))
    nf = normf_g.reshape(1, d)

    tm_p = 512
    (vp, kc, vc, ks, vs, kw, vw, gm, ksb, kwb, vst, vwt, qt, gst, pooled) = _in_proj(
        x_prompt, mod_p, mod_p, g1n, w2, tm=tm_p, prompt=True)
    kcmp, vcmpt = _compress(kc, vc, wk, wv)
    ynsa = _nsa_prompt(qt, gst, kcmp, vcmpt, ksb, vst, kwb, vwt)
    x1, u2, u2p, eid_t, gk_t, rank_t, counts = _finish(
        x_prompt, pooled, ynsa, gm, (mod_p, mod_p, mod_p), fin_w, tm=tm_p, sparse=True)
    y_sorted, gath_idx = _moe_sorted_experts(u2p, eid_t, rank_t, counts, *moe_w[:3])

    n_tok = n_seq * ns
    xs3 = x_sample.reshape(1, n_tok, d)
    tm_s = 128
    (vp_s, kc_s, vc_s, ks_s, vs_s, kw_s, vw_s, gm_s, q_s, gs_s) = _in_proj(
        xs3, mod_s, mod_s, g1n, w2, tm=tm_s, prompt=False)
    two = lambda a: a.reshape(n_tok, a.shape[-1])
    q_rows = q_s.reshape(n_tok * N_HEADS, LANES)
    gate_rows = two(gs_s)[:, :3 * N_HEADS].reshape(n_tok, 3, N_HEADS).transpose(0, 2, 1)
    gate_rows = jnp.pad(gate_rows.reshape(n_tok * N_HEADS, 3), ((0, 0), (0, LANES - 3)))
    n_pool = cache_kc.shape[1]
    page = cache_kc.shape[2]
    rows_minor = lambda a: jnp.transpose(a, (0, 2, 3, 1)).reshape(a.shape[0], KVW, a.shape[1])
    caches = [rows_minor(c[lyr]) for c in (cache_kc, cache_vc, cache_ks, cache_vs)]
    o_rows, pooled_s = _nsa_sample(
        page_table, q_rows, gate_rows, [two(a) for a in (kc_s, vc_s, ks_s, vs_s, kw_s, vw_s)], two(vp_s),
        rows_minor(state_kw[lyr]), rows_minor(state_vw[lyr]), state_pool[lyr], wk, wv, caches)
    ynsa_s = o_rows.reshape(1, n_tok, QPAD)
    ynsa_s, y_sorted = lax.optimization_barrier((ynsa_s, y_sorted))
    y_prompt = _moe_sorted_combine(y_sorted, gath_idx, gk_t, u2, x1, mod_p, nf, *moe_w[3:])
    x1_s, u2_s, gates_s = _finish(xs3, pooled_s.reshape(1, n_tok, POOL_W), ynsa_s, gm_s,
                                  (mod_s, mod_s, mod_s), fin_w, tm=tm_s, sparse=False)
    y_sample = _moe(u2_s, gates_s, x1_s, mod_s, nf, *moe_w, tm=n_tok).reshape(n_seq, ns, d)

    kvp = lambda a: a.reshape(1, bsz, seq, N_KV, HEAD_DIM)
    tailp = lambda a: jnp.pad(a, ((0, 0), (wbuf, 0), (0, 0)))[:, -wbuf:].reshape(1, bsz, wbuf, N_KV, HEAD_DIM)
    kvs = lambda a: a.reshape(1, n_seq, ns, N_KV, HEAD_DIM)
    wins = lambda st, new: jnp.concatenate(
        [st[lyr], new.reshape(n_seq, ns, N_KV, HEAD_DIM)], axis=1)[None, :, -wbuf:]
    pool_p = vp[:, -POOL_BUF:][None]
    pool_s = jnp.concatenate([state_pool[lyr], vp_s.reshape(n_seq, ns, POOL_W)], axis=1)[None, :, -POOL_BUF:]
    return (y_prompt, y_sample, kvp(kc), kvp(vc), kvp(ks), kvp(vs), tailp(kw), tailp(vw), pool_p,
            kvs(kc_s), kvs(vc_s), kvs(ks_s), kvs(vs_s), wins(state_kw, kw_s), wins(state_vw, vw_s), pool_s)
```

```python
import functools

import jax
import jax.numpy as jnp
from jax import lax
from jax.experimental import pallas as pl
from jax.experimental.pallas import tpu as pltpu
from jax.experimental.pallas import tpu_sc as plsc

F32 = jnp.float32
BF16 = jnp.bfloat16
I32 = jnp.int32

POOL_WINDOWS = (2, 4, 8, 16)
POOL_GW = 64
POOL_W = 256
POOL_BUF = 15
N_HEADS = 8
HEAD_DIM = 64
N_KV = 2
GROUP = N_HEADS // N_KV
CMP_STRIDE = 16
CMP_BLOCK = 32
SEL_BLOCK = 64
TOP_BLOCKS = 16
WINDOW = 512
Q_BLOCK = 128
FORCE_SCORE = 1e4
N_EXPERTS = 64
N_EGROUPS = 8
EXPERTS_PER_GROUP = N_EXPERTS // N_EGROUPS
TOPK_GROUPS = 4
TOP_K = 8
ROUTED_SCALE = 2.5
EPS = 1e-6
NEG = -1e30
SLOPES = tuple(2.0 ** (-8.0 * (h + 1.0) / N_HEADS) for h in range(N_HEADS))

LANES = 128
QPAD = N_HEADS * LANES
KVW = N_KV * HEAD_DIM
VMEM_LIMIT = 56 * 1024 * 1024


def _cparams(sem):
    return pltpu.CompilerParams(dimension_semantics=sem, vmem_limit_bytes=VMEM_LIMIT)


def _dot(a, b):
    return jnp.dot(a, b, preferred_element_type=F32)


def _dot_nt(a, b):
    return lax.dot_general(a, b, (((1,), (1,)), ((), ())), preferred_element_type=F32)


def _dot_exact(a, b):
    return jnp.dot(a, b, preferred_element_type=F32, precision=lax.Precision.HIGHEST)


def _rows2d(ref):
    v = ref[...]
    return v.reshape(v.shape[-2], v.shape[-1])


def _rmsnorm(x, g):
    return x * lax.rsqrt(jnp.mean(x * x, axis=-1, keepdims=True) + EPS) * g


def _silu(x):
    return x * jax.nn.sigmoid(x)


def _adaln_kernel(c_ref, w_ref, b_ref, o_ref):
    s = _silu(c_ref[...]).astype(BF16)
    o_ref[...] = _dot(s, w_ref[...].astype(BF16)) + b_ref[...]


def _adaln(c, w_ada, b_ada):
    rows, d = c.shape
    n = w_ada.shape[1]
    tn = 512
    return pl.pallas_call(
        _adaln_kernel,
        out_shape=jax.ShapeDtypeStruct((rows, n), F32),
        grid=(n // tn,),
        in_specs=[pl.BlockSpec((rows, d), lambda j: (0, 0)),
                  pl.BlockSpec((d, tn), lambda j: (0, j)),
                  pl.BlockSpec((1, tn), lambda j: (0, j))],
        out_specs=pl.BlockSpec((rows, tn), lambda j: (0, j)),
        compiler_params=_cparams(("arbitrary",)),
        name="adaln",
    )(c, w_ada, b_ada.reshape(1, n))


_C_VP = 0
_C_Q = _C_VP + POOL_W
_C_KV = _C_Q + QPAD
_C_GN = _C_KV + 6 * KVW
_C_GM = _C_GN + LANES


def _pool_window_sums(ext, tm):
    s2 = ext + pltpu.roll(ext, 1, 0)
    s4 = s2 + pltpu.roll(s2, 2, 0)
    s8 = s4 + pltpu.roll(s4, 4, 0)
    s16 = s8 + pltpu.roll(s8, 8, 0)
    grp = lax.broadcasted_iota(I32, (1, POOL_W), 1) // POOL_GW
    pick = jnp.where(grp == 0, s2, jnp.where(grp == 1, s4, jnp.where(grp == 2, s8, s16)))
    return pick[16:16 + tm]


def _in_proj_kernel(x_ref, shift_ref, scale_ref, g_ref, w_ref,
                    vp_ref, kc_ref, vc_ref, ks_ref, vs_ref, kw_ref, vw_ref, gm_ref, *rest, tm, d, prompt):
    x = x_ref[...].reshape(tm, d)
    u = _rmsnorm(x, g_ref[...]) * (1.0 + _rows2d(scale_ref)) + _rows2d(shift_ref)
    ub = u.astype(BF16)

    head = _dot(ub, w_ref[:, 0:_C_GM])

    def proj(c0, n):
        return head[:, c0:c0 + n] if c0 + n <= _C_GM else _dot(ub, w_ref[:, c0:c0 + n])

    vp = proj(_C_VP, POOL_W)
    vp_ref[...] = vp.reshape(vp_ref.shape)
    kv = []
    for n, o32 in enumerate((kc_ref, vc_ref, ks_ref, vs_ref, kw_ref, vw_ref)):
        v = proj(_C_KV + n * KVW, KVW)
        o32[...] = v.reshape(o32.shape)
        kv.append(v)
    gm_ref[...] = jax.nn.sigmoid(proj(_C_GM, 2 * d)).reshape(gm_ref.shape)
    gs = jax.nn.sigmoid(proj(_C_GN, LANES))

    if not prompt:
        q_ref, gs_ref = rest
        q_ref[...] = proj(_C_Q, QPAD).astype(BF16).reshape(q_ref.shape)
        gs_ref[...] = gs.reshape(gs_ref.shape)
    else:
        ksb_ref, kwb_ref, vst_ref, vwt_ref, qt_ref, gst_ref, pooled_ref, halo_ref = rest
        ksb_ref[...] = kv[2].astype(BF16).reshape(ksb_ref.shape)
        kwb_ref[...] = kv[4].astype(BF16).reshape(kwb_ref.shape)
        vst_ref[...] = kv[3].T.astype(BF16).reshape(vst_ref.shape)
        vwt_ref[...] = kv[5].T.astype(BF16).reshape(vwt_ref.shape)
        gst_ref[...] = gs.T.reshape(gst_ref.shape)
        for h in range(N_HEADS):
            qt_ref[0, h] = proj(_C_Q + h * LANES, LANES).T.astype(BF16)
        j = pl.program_id(1)

        @pl.when(j == 0)
        def _():
            halo_ref[...] = jnp.zeros_like(halo_ref)

        ext = jnp.concatenate([halo_ref[...], vp], axis=0)
        sums = _pool_window_sums(ext, tm)
        pos = j * tm + lax.broadcasted_iota(I32, (tm, 1), 0)
        wcol = 2 << (lax.broadcasted_iota(I32, (1, POOL_W), 1) // POOL_GW)
        cnt = jnp.minimum(pos + 1, wcol).astype(F32)
        pooled_ref[...] = (sums / cnt - vp).astype(BF16).reshape(pooled_ref.shape)
        halo_ref[...] = vp[tm - 16:tm]


def _in_proj(x3, shift, scale, g1, w2, *, tm, prompt):
    g, r, d = x3.shape
    nt = r // tm
    per_row = shift.ndim == 2

    def tok(width, dtype):
        return (jax.ShapeDtypeStruct((g, r, width), dtype),
                pl.BlockSpec((1, tm, width), lambda b, j: (b, j, 0)))

    def tok_t(rows, dtype):
        return (jax.ShapeDtypeStruct((g, rows, r), dtype),
                pl.BlockSpec((1, rows, tm), lambda b, j: (b, 0, j)))

    outs = [tok(POOL_W, F32)] + [tok(KVW, F32)] * 6 + [tok(2 * d, F32)]
    scratch = []
    if prompt:
        outs += [tok(KVW, BF16), tok(KVW, BF16), tok_t(KVW, BF16), tok_t(KVW, BF16)]
        outs.append((jax.ShapeDtypeStruct((g, N_HEADS, LANES, r), BF16),
                     pl.BlockSpec((1, N_HEADS, LANES, tm), lambda b, j: (b, 0, 0, j))))
        outs += [tok_t(LANES, F32), tok(POOL_W, BF16)]
        scratch.append(pltpu.VMEM((16, POOL_W), F32))
    else:
        outs += [tok(QPAD, BF16), tok(LANES, F32)]
    if per_row:
        mod_spec = lambda col: pl.BlockSpec((tm, d), lambda b, j, col=col: (b * nt + j, col))
    else:
        mod_spec = lambda col: pl.BlockSpec((1, 1, d), lambda b, j, col=col: (b, 0, col))
    kern = functools.partial(_in_proj_kernel, tm=tm, d=d, prompt=prompt)
    return pl.pallas_call(
        kern,
        out_shape=[o[0] for o in outs],
        grid=(g, nt),
        in_specs=[pl.BlockSpec((1, tm, d), lambda b, j: (b, j, 0)),
                  mod_spec(0), mod_spec(1),
                  pl.BlockSpec((1, d), lambda b, j: (0, 0)),
                  pl.BlockSpec(w2.shape, lambda b, j: (0, 0))],
        out_specs=[o[1] for o in outs],
        scratch_shapes=scratch,
        compiler_params=_cparams(("arbitrary", "arbitrary")),
        name="in_proj_prompt" if prompt else "in_proj_sample",
    )(x3, shift, scale, g1, w2)


def _compress_kernel(kc_ref, vc_ref, wk_ref, wv_ref, okc_ref, ovc_ref, sh_ref, *, nc):
    last = lax.broadcasted_iota(I32, (nc, 1), 0) == nc - 1
    for src, w_ref, dst in ((kc_ref, wk_ref, okc_ref), (vc_ref, wv_ref, ovc_ref)):
        head = jnp.zeros((nc, KVW), F32)
        tail = jnp.zeros((nc, KVW), F32)
        for r in range(CMP_STRIDE):
            rows = src[pl.ds(r, nc, stride=CMP_STRIDE), :]
            head = head + rows * w_ref[r:r + 1, :]
            tail = tail + rows * w_ref[CMP_STRIDE + r:CMP_STRIDE + r + 1, :]
        sh_ref[0:nc, :] = tail
        sh_ref[nc:nc + 8, :] = jnp.zeros((8, KVW), F32)
        out = jnp.where(last, 0.0, head + sh_ref[1:nc + 1, :])
        dst[...] = (out if dst is okc_ref else out.T).astype(BF16)


def _compress(kc, vc, wk, wv):
    b, s, _ = kc.shape
    nc = s // CMP_STRIDE
    big = pl.BlockSpec((None, s, KVW), lambda i: (i, 0, 0))
    wsp = pl.BlockSpec((CMP_BLOCK, KVW), lambda i: (0, 0))
    return pl.pallas_call(
        functools.partial(_compress_kernel, nc=nc),
        out_shape=[jax.ShapeDtypeStruct((b, nc, KVW), BF16), jax.ShapeDtypeStruct((b, KVW, nc), BF16)],
        grid=(b,),
        in_specs=[big, big, wsp, wsp],
        out_specs=[pl.BlockSpec((None, nc, KVW), lambda i: (i, 0, 0)),
                   pl.BlockSpec((None, KVW, nc), lambda i: (i, 0, 0))],
        scratch_shapes=[pltpu.VMEM((nc + 8, KVW), F32)],
        compiler_params=_cparams(("arbitrary",)),
        name="compress",
    )(kc, vc, wk, wv)


def _topk_mask(vals, blk_f, n_top, axis=1):
    sel = jnp.zeros(vals.shape, F32)
    big = float(vals.shape[axis])
    for _ in range(n_top):
        mx = jnp.max(vals, axis=axis, keepdims=True)
        first = jnp.min(jnp.where(vals == mx, blk_f, big), axis=axis, keepdims=True)
        hit = blk_f == first
        sel = jnp.where(hit, 1.0, sel)
        vals = jnp.where(hit, -jnp.inf, vals)
    return sel


def _topk_mask_by_rank(vals, blk, n_valid, n_top):
    rank = jnp.zeros(vals.shape, F32)
    for j in range(n_valid):
        vj = vals[:, j:j + 1]
        beats = (vj > vals) | ((vj == vals) & (blk > j))
        rank = rank + jnp.where(beats, 1.0, 0.0)
    return jnp.where(rank < float(n_top), 1.0, 0.0)


def _pos_features(pos):
    hi = (pos // SEL_BLOCK).astype(F32)[:, None]
    lo = (pos % SEL_BLOCK).astype(F32)[:, None]
    return jnp.concatenate([hi, lo, jnp.zeros((pos.shape[0], LANES - 2), F32)], axis=1).astype(BF16)


def _importance_matrix(nc, nsel):
    j = jnp.arange(nc)[:, None]
    s = jnp.arange(nsel)[None, :]
    r = SEL_BLOCK // CMP_STRIDE
    a = (j >= r * s) & (j <= r * s + r - 1)
    b = (j + 1 >= r * s) & (j + 1 <= r * s + r - 1)
    return a.astype(F32) + b.astype(F32)


def _nsa_prompt_kernel(qt_ref, gst_ref, kc_ref, vct_ref, ks_ref, vst_ref, kw_ref, vwt_ref,
                       cfeat_ref, wfeat_ref, qfeat_ref, slope_ref,
                       y_ref, qk_scr, m_scr, l_scr, acc_scr, o_scr, sel_scr, imp_scr, flag_scr, *, seq, tk, wl):
    i = pl.program_id(1)
    q0 = i * Q_BLOCK
    nq = Q_BLOCK
    gq = GROUP * nq
    nc = kc_ref.shape[0]
    nsel = seq // SEL_BLOCK
    n_top = min(TOP_BLOCKS, nsel)
    blk_per_tile = tk // SEL_BLOCK
    qpos = q0 + lax.broadcasted_iota(I32, (1, nq), 1)
    gst = gst_ref[...]

    crow = lax.broadcasted_iota(I32, (nc, nq), 0)
    cend = crow * CMP_STRIDE + (CMP_BLOCK - 1)
    mask_c = qpos >= cend
    kc = jnp.concatenate([kc_ref[...], cfeat_ref[...]], axis=1)
    vct = vct_ref[...]
    blk = lax.broadcasted_iota(I32, (nsel, nq), 0)
    blk_f = blk.astype(F32)
    cur = qpos // SEL_BLOCK
    forced = (blk == 0) | (blk == cur) | (blk == cur - 1)
    visible = blk * SEL_BLOCK <= qpos
    ws = pl.multiple_of(jnp.maximum(q0 - WINDOW, 0), Q_BLOCK)
    wpos = ws + lax.broadcasted_iota(I32, (wl, nq), 0)
    valid_w = lax.bitcast_convert_type(qpos - wpos, jnp.uint32) < WINDOW
    n_tiles = (q0 + nq + tk - 1) // tk
    half_rows = lax.broadcasted_iota(I32, (KVW, nq), 0) // HEAD_DIM
    tile_pos = lax.broadcasted_iota(I32, (SEL_BLOCK, nq), 0)

    def lanes4(x):
        return jnp.concatenate([x] * GROUP, axis=1)

    def gate_row(branch, k):
        r0 = branch * N_HEADS + k * GROUP
        return jnp.concatenate([gst[r0 + g:r0 + g + 1] for g in range(GROUP)], axis=1)

    mask_c4 = lanes4(mask_c)
    valid_w4 = lanes4(valid_w)
    kwt = jnp.concatenate([kw_ref[pl.ds(ws, wl), :], wfeat_ref[...]], axis=1)
    vwtt = vwt_ref[:, pl.ds(ws, wl)]

    for k in range(N_KV):
        for g in range(GROUP):
            qk_scr[k, 0:LANES, g * nq:(g + 1) * nq] = qt_ref[k * GROUP + g]
        qk_scr[k, LANES:2 * LANES, :] = qfeat_ref[k]
        qk = qk_scr[k]

        s = jnp.where(mask_c4, _dot(kc, qk), NEG)
        e = jnp.where(mask_c4, jnp.exp(s - jnp.max(s, axis=0, keepdims=True)), 0.0)
        l = jnp.sum(e, axis=0, keepdims=True)
        p = e * jnp.where(l > 0.0, 1.0 / l, 0.0)
        o_c = _dot(vct, p.astype(BF16))
        psum = p[:, 0:nq]
        for g in range(1, GROUP):
            psum = psum + p[:, g * nq:(g + 1) * nq]

        a = psum + jnp.where(crow == 0, 0.0, pltpu.roll(psum, 1, 0))
        a = a + pltpu.roll(a, nc - 1, 0)
        imp_scr[...] = a + pltpu.roll(a, nc - 2, 0)
        imp = imp_scr[pl.ds(0, nsel, stride=nc // nsel), :]
        vals = jnp.where(visible, jnp.where(forced, FORCE_SCORE, imp), NEG)
        sel = jnp.where(visible, _topk_mask(vals, blk_f, n_top, axis=0), 0.0)
        sel_scr[k] = jnp.where(sel > 0.5, 0.0, NEG)
        blk_any = jnp.max(sel, axis=1, keepdims=True)
        for t in range(seq // tk):
            hit = (jnp.max(blk_any[t * blk_per_tile:(t + 1) * blk_per_tile, :]) > 0.5).astype(I32)
            flag_scr[t] = hit if k == 0 else flag_scr[t] | hit

        s = jnp.where(valid_w4, _dot(kwt, qk), NEG)
        e = jnp.exp(s - jnp.max(s, axis=0, keepdims=True))
        p = e / jnp.sum(e, axis=0, keepdims=True)
        o_w = _dot(vwtt, p.astype(BF16))
        o_scr[k] = gate_row(0, k) * o_c + gate_row(2, k) * o_w

    m_scr[...] = jnp.full(m_scr.shape, NEG, F32)
    l_scr[...] = jnp.zeros(l_scr.shape, F32)
    acc_scr[...] = jnp.zeros(acc_scr.shape, F32)

    def sel_tile(t):
        k0 = pl.multiple_of(t * tk, tk)
        kt = jnp.concatenate([ks_ref[pl.ds(k0, tk), :], wfeat_ref[0:tk, :]], axis=1)
        vtt = vst_ref[:, pl.ds(k0, tk)]
        causal = [qpos >= k0 + j * SEL_BLOCK + tile_pos for j in range(blk_per_tile)]
        base = (k0 - q0).astype(F32)
        scores = _dot(kt, jnp.concatenate([qk_scr[k] for k in range(N_KV)], axis=1))
        probs, alphas = [], []
        for k in range(N_KV):
            neg = []
            for j in range(blk_per_tile):
                row = sel_scr[k, pl.ds(t * blk_per_tile + j, 1), :]
                neg.append(jnp.where(causal[j], jnp.broadcast_to(row, (SEL_BLOCK, nq)), NEG))
            neg = lanes4(jnp.concatenate(neg, axis=0))
            off = slope_ref[k] * base
            s = scores[:, k * gq:(k + 1) * gq] + neg
            m_old = m_scr[k]
            m_new = jnp.maximum(m_old, jnp.max(s, axis=0, keepdims=True) + off)
            alpha = jnp.exp(m_old - m_new)
            p = jnp.exp(s - (m_new - off))
            l_scr[k] = alpha * l_scr[k] + jnp.sum(p, axis=0, keepdims=True)
            m_scr[k] = m_new
            probs.append(p.astype(BF16))
            alphas.append(alpha)
        pv = _dot(vtt, jnp.concatenate(probs, axis=1))
        for k in range(N_KV):
            acc_scr[k] = acc_scr[k] * alphas[k] + pv[:, k * gq:(k + 1) * gq]

    def sel_body(t, carry):
        pl.when(flag_scr[t] > 0)(functools.partial(sel_tile, t))
        return carry

    lax.fori_loop(0, n_tiles, sel_body, 0)

    for k in range(N_KV):
        o = o_scr[k] + gate_row(1, k) * (acc_scr[k] / l_scr[k])
        for g in range(GROUP):
            h = k * GROUP + g
            oh = jnp.where(half_rows == k, o[:, g * nq:(g + 1) * nq], 0.0)
            y_ref[:, h * LANES:(h + 1) * LANES] = oh.T.astype(BF16)


def _nsa_prompt(qt, gst, kcmp, vcmpt, ksb, vst, kwb, vwt):
    b, _, _, s = qt.shape
    nq = Q_BLOCK
    gq = GROUP * nq
    nc = kcmp.shape[1]
    nsel = s // SEL_BLOCK
    tk = 256
    wl = WINDOW + Q_BLOCK
    assert s % tk == 0 and s >= wl
    assert s // SEL_BLOCK <= 2 * LANES, "position // 64 must stay exact in bf16"
    cfeat = _pos_features(jnp.arange(nc) * CMP_STRIDE + (CMP_BLOCK - 1))
    wfeat = _pos_features(jnp.arange(wl))
    slope_rows = jnp.repeat(jnp.asarray(SLOPES, F32).reshape(N_KV, 1, GROUP), nq, axis=2)
    qfeat = jnp.concatenate([slope_rows * SEL_BLOCK, slope_rows, jnp.zeros((N_KV, LANES - 2, gq), F32)],
                            axis=1).astype(BF16)
    rows = lambda r: pl.BlockSpec((None, r, KVW), lambda bi, i: (bi, 0, 0))
    cols = lambda c: pl.BlockSpec((None, KVW, c), lambda bi, i: (bi, 0, 0))
    const = lambda a: pl.BlockSpec(a.shape, lambda bi, i: (0,) * a.ndim)
    return pl.pallas_call(
        functools.partial(_nsa_prompt_kernel, seq=s, tk=tk, wl=wl),
        out_shape=jax.ShapeDtypeStruct((b, s, QPAD), BF16),
        grid=(b, s // nq),
        in_specs=[pl.BlockSpec((None, N_HEADS, LANES, nq), lambda bi, i: (bi, 0, 0, i)),
                  pl.BlockSpec((None, LANES, nq), lambda bi, i: (bi, 0, i)),
                  rows(nc), cols(nc), rows(s), cols(s), rows(s), cols(s),
                  const(cfeat), const(wfeat), const(qfeat), const(slope_rows)],
        out_specs=pl.BlockSpec((None, nq, QPAD), lambda bi, i: (bi, i, 0)),
        scratch_shapes=[pltpu.VMEM((N_KV, 2 * LANES, gq), BF16),
                        pltpu.VMEM((N_KV, 1, gq), F32),
                        pltpu.VMEM((N_KV, 1, gq), F32),
                        pltpu.VMEM((N_KV, KVW, gq), F32),
                        pltpu.VMEM((N_KV, KVW, gq), F32),
                        pltpu.VMEM((N_KV, nsel, nq), F32),
                        pltpu.VMEM((nc, nq), F32),
                        pltpu.SMEM((s // tk,), I32)],
        compiler_params=_cparams(("arbitrary", "arbitrary")),
        name="nsa_prompt",
    )(qt, gst, kcmp, vcmpt, ksb, vst, kwb, vwt, cfeat, wfeat, qfeat, slope_rows)


def _nsa_sample_kernel(pt_ref, q_ref, gate_ref, kcn_ref, vcn_ref, ksn_ref, vsn_ref, kwn_ref, vwn_ref,
                       vpn_ref, skw_ref, svw_ref, spool_ref, wk_ref, wv_ref, imat_ref, emat_ref,
                       ckc_ref, cvc_ref, cks_ref, cvs_ref,
                       o_ref, pooled_ref, buf, buft, win_scr, tail_scr, vext_scr, sem,
                       *, sb, ns, past, n_pages, page, n_seq, ncv, ncp, nks, wls, nselp, n_sel):
    step = pl.program_id(0)
    nrow = ns * N_HEADS
    par = step % 2

    def copies(n, side, r):
        out = []
        for p in range(n_pages):
            pg = pt_ref[n * n_pages + p]
            for c, cref in enumerate((ckc_ref, cvc_ref, cks_ref, cvs_ref)):
                out.append(pltpu.make_async_copy(cref.at[pg], buft.at[side, r, c, :, pl.ds(p * page, page)],
                                                 sem.at[side, r]))
        return out

    @pl.when(step == 0)
    def _():
        buf[:, :, past:, :] = jnp.zeros((sb, 2, buf.shape[2] - past, KVW), F32)
        tail_scr[...] = jnp.zeros_like(tail_scr)
        vext_scr[...] = jnp.zeros_like(vext_scr)
        for r in range(sb):
            for cp in copies(r, 0, r):
                cp.start()

    @pl.when(step + 1 < pl.num_programs(0))
    def _():
        for r in range(sb):
            for cp in copies((step + 1) * sb + r, 1 - par, r):
                cp.start()

    for r in range(sb):
        for cp in copies(step * sb + r, par, r):
            cp.wait()

    def new_rows_t(ref, r4, r):
        tail_scr[r, 0:ns, :] = ref[pl.ds(r4, ns), :]
        return tail_scr[r].T

    row = lax.broadcasted_iota(I32, (nrow, 1), 0)
    hrow = row % N_HEADS
    qpos = past + row // N_HEADS
    slope = jnp.exp2(-8.0 * (hrow.astype(F32) + 1.0) / N_HEADS)
    kvrow = hrow // GROUP
    lane = lax.broadcasted_iota(I32, (1, LANES), 1)
    half = (lane // HEAD_DIM) == kvrow
    grow = (row // N_HEADS) * N_KV + kvrow
    row8 = lax.broadcasted_iota(I32, (ns * N_KV, 1), 0)
    qpos8 = past + lax.broadcasted_iota(I32, (ns * N_KV, 1), 0) // N_KV
    blk = lax.broadcasted_iota(I32, (1, nselp), 1)
    blk_f = blk.astype(F32)
    cur = qpos8 // SEL_BLOCK
    forced = (blk == 0) | (blk == cur) | (blk == cur - 1)
    visible = (blk * SEL_BLOCK <= qpos8)
    inrange = blk < n_sel
    cend = lax.broadcasted_iota(I32, (1, ncp), 1) * CMP_STRIDE + (CMP_BLOCK - 1)
    mask_c = qpos >= cend
    bias_c = slope * (cend - qpos).astype(F32)
    kpos = lax.broadcasted_iota(I32, (1, nks), 1)
    causal_s = qpos >= kpos
    bias_s = slope * (kpos - qpos).astype(F32)
    wbuf = wls[0]
    wpos = past - wbuf + lax.broadcasted_iota(I32, (1, wls[1]), 1)
    dw = qpos - wpos
    valid_w = lax.bitcast_convert_type(dw, jnp.uint32) < WINDOW
    bias_w = slope * (wpos - qpos).astype(F32)
    prow = lax.broadcasted_iota(I32, (vext_scr.shape[1], 1), 0)
    wcol = 2 << (lax.broadcasted_iota(I32, (1, POOL_W), 1) // POOL_GW)
    n_top = min(TOP_BLOCKS, n_sel)

    def softmax_rows(s, mask):
        s = jnp.where(mask, s, NEG)
        mx = jnp.max(s, axis=1, keepdims=True)
        e = jnp.where(mask, jnp.exp(s - mx), 0.0)
        l = jnp.sum(e, axis=1, keepdims=True)
        return e * jnp.where(l > 0.0, 1.0 / l, 0.0)

    def seq_body(r):
        r4 = r * ns
        for c, new_ref in enumerate((kcn_ref, vcn_ref)):
            for p in range(n_pages):
                buf[r, c, p * page:(p + 1) * page, :] = buft[par, r, c, :, p * page:(p + 1) * page].T
            buf[r, c, past:past + ns, :] = new_ref[pl.ds(r4, ns), :]
        for c, new_ref in ((2, ksn_ref), (3, vsn_ref)):
            buft[par, r, c, :, past:past + LANES] = new_rows_t(new_ref, r4, r)

        qall = q_ref[pl.ds(r * nrow, nrow), :]
        gates = gate_ref[pl.ds(r * nrow, nrow), :]

        cmp = []
        for c, w_ref in ((0, wk_ref), (1, wv_ref)):
            span = CMP_STRIDE * ncv
            lo = buf[r, c, 0:span, :].reshape(ncv, CMP_STRIDE, KVW) * w_ref[0:CMP_STRIDE, :][None]
            hi = (buf[r, c, CMP_STRIDE:CMP_STRIDE + span, :].reshape(ncv, CMP_STRIDE, KVW)
                  * w_ref[CMP_STRIDE:CMP_BLOCK, :][None])
            acc = jnp.sum(lo + hi, axis=1)
            cmp.append(jnp.concatenate([acc, jnp.zeros((ncp - ncv, KVW), F32)], axis=0).astype(BF16))
        p_c = softmax_rows(_dot_nt(qall, cmp[0]) + bias_c, mask_c)
        o_c = _dot(p_c.astype(BF16), cmp[1])

        psum = jnp.zeros((ns * N_KV, ncp), F32)
        for i in range(ns * N_KV):
            r0 = (i // N_KV) * N_HEADS + (i % N_KV) * GROUP
            psum = jnp.where(row8 == i, jnp.sum(p_c[r0:r0 + GROUP], axis=0, keepdims=True), psum)
        imp = _dot_exact(psum, imat_ref[...])
        vals = jnp.where(inrange, jnp.where(visible, jnp.where(forced, FORCE_SCORE, imp), NEG), -jnp.inf)
        sel8 = _topk_mask_by_rank(vals, blk, n_sel, n_top)
        sel_rows = jnp.zeros((nrow, nselp), F32)
        for i in range(ns * N_KV):
            sel_rows = jnp.where(grow == i, sel8[i:i + 1], sel_rows)
        chosen = _dot(sel_rows.astype(BF16), emat_ref[...])

        kst = buft[par, r, 2].astype(BF16)
        vst = buft[par, r, 3].astype(BF16)
        p_s = softmax_rows(_dot(qall, kst) + bias_s, causal_s & (chosen > 0.5))
        o_s = _dot_nt(p_s.astype(BF16), vst)

        outs_w = []
        for state_ref, new_ref in ((skw_ref, kwn_ref), (svw_ref, vwn_ref)):
            win_scr[r, :, 0:wbuf] = state_ref[r]
            win_scr[r, :, wbuf:wbuf + LANES] = new_rows_t(new_ref, r4, r)
            outs_w.append(win_scr[r].astype(BF16))
        p_w = softmax_rows(_dot(qall, outs_w[0]) + bias_w, valid_w)
        o_w = _dot_nt(p_w.astype(BF16), outs_w[1])

        o = gates[:, 0:1] * o_c + gates[:, 1:2] * o_s + gates[:, 2:3] * o_w
        o_ref[pl.ds(r * nrow, nrow), :] = jnp.where(half, o, 0.0).astype(BF16)

        vext_scr[r, 0:POOL_BUF, :] = spool_ref[r]
        vext_scr[r, POOL_BUF:POOL_BUF + ns, :] = vpn_ref[pl.ds(r4, ns), :]
        ext = vext_scr[r]
        for t in range(ns):
            hi = POOL_BUF + t
            inwin = (prow <= hi) & (prow > hi - wcol)
            ssum = jnp.sum(jnp.where(inwin, ext, 0.0), axis=0, keepdims=True)
            cnt = jnp.minimum(past + t + 1, wcol).astype(F32)
            pooled_ref[pl.ds(r4 + t, 1), :] = ssum / cnt - ext[hi:hi + 1, :]

    for r in range(sb):
        seq_body(r)


def _nsa_sample(page_table, q_rows, gate_rows, new6, vp_new, state_kwt, state_vwt, state_pool, wk, wv, caches):
    n_seq, n_pages = page_table.shape
    page = caches[0].shape[2]
    past = n_pages * page
    ns = vp_new.shape[0] // n_seq
    wbuf = state_kwt.shape[2]
    sb = 2
    nrow = ns * N_HEADS
    assert ns <= SEL_BLOCK and page == LANES
    t_pad = -(-(past + ns) // SEL_BLOCK) * SEL_BLOCK
    n_cmp = t_pad // CMP_STRIDE - 1
    ncv = -(-n_cmp // 8) * 8
    ncp = -(-ncv // LANES) * LANES
    nks = past + LANES
    n_sel = t_pad // SEL_BLOCK
    nselp = LANES
    assert n_sel <= nselp
    wlp = wbuf + LANES
    buf_rows = -(-(CMP_STRIDE * ncv + CMP_STRIDE) // 8) * 8
    imat = _importance_matrix(ncp, nselp)
    emat = (jnp.arange(nselp)[:, None] == (jnp.arange(nks)[None, :] // SEL_BLOCK)).astype(BF16)

    seqblk = lambda rows, w: pl.BlockSpec((sb * rows, w), lambda i, pt: (i, 0))
    const = lambda a: pl.BlockSpec(a.shape, lambda i, pt: (0,) * a.ndim)
    kern = functools.partial(
        _nsa_sample_kernel, sb=sb, ns=ns, past=past, n_pages=n_pages, page=page, n_seq=n_seq,
        ncv=ncv, ncp=ncp, nks=nks, wls=(wbuf, wlp), nselp=nselp, n_sel=n_sel)
    grid_spec = pltpu.PrefetchScalarGridSpec(
        num_scalar_prefetch=1,
        grid=(n_seq // sb,),
        in_specs=[seqblk(nrow, LANES), seqblk(nrow, LANES)] + [seqblk(ns, KVW)] * 6 + [seqblk(ns, POOL_W)]
        + [pl.BlockSpec((sb, KVW, wbuf), lambda i, pt: (i, 0, 0))] * 2
        + [pl.BlockSpec((sb, POOL_BUF, POOL_W), lambda i, pt: (i, 0, 0))]
        + [const(wk), const(wv), const(imat), const(emat)]
        + [pl.BlockSpec(memory_space=pl.ANY)] * 4,
        out_specs=[seqblk(nrow, LANES), seqblk(ns, POOL_W)],
        scratch_shapes=[pltpu.VMEM((sb, 2, buf_rows, KVW), F32),
                        pltpu.VMEM((2, sb, 4, KVW, nks), F32),
                        pltpu.VMEM((sb, KVW, wlp), F32),
                        pltpu.VMEM((sb, LANES, KVW), F32),
                        pltpu.VMEM((sb, 24, POOL_W), F32),
                        pltpu.SemaphoreType.DMA((2, sb))],
    )
    return pl.pallas_call(
        kern,
        out_shape=[jax.ShapeDtypeStruct((n_seq * nrow, LANES), BF16),
                   jax.ShapeDtypeStruct((n_seq * ns, POOL_W), F32)],
        grid_spec=grid_spec,
        compiler_params=_cparams(("arbitrary",)),
        name="nsa_sample",
    )(page_table.reshape(-1), q_rows, gate_rows, *new6, vp_new, state_kwt, state_vwt, state_pool, wk, wv,
      imat, emat, *caches)


def _route(logits_t, bias_col, tm):
    sc = jax.nn.sigmoid(logits_t)
    biased = sc + bias_col
    epg = EXPERTS_PER_GROUP
    row8 = lax.broadcasted_iota(I32, (epg, tm), 0).astype(F32)
    ninf = -jnp.inf
    grp = jnp.zeros((N_EGROUPS, tm), F32)
    for g in range(N_EGROUPS):
        bg = biased[g * epg:(g + 1) * epg]
        m1 = jnp.max(bg, axis=0, keepdims=True)
        first = jnp.min(jnp.where(bg == m1, row8, float(epg)), axis=0, keepdims=True)
        m2 = jnp.max(jnp.where(row8 == first, ninf, bg), axis=0, keepdims=True)
        grp = jnp.where(row8 == float(g), m1 + m2, grp)
    keep = jnp.zeros((N_EGROUPS, tm), F32)
    vals = grp
    for _ in range(TOPK_GROUPS):
        mx = jnp.max(vals, axis=0, keepdims=True)
        first = jnp.min(jnp.where(vals == mx, row8, float(N_EGROUPS)), axis=0, keepdims=True)
        hit = row8 == first
        keep = jnp.where(hit, 1.0, keep)
        vals = jnp.where(hit, ninf, vals)
    masked = jnp.concatenate(
        [jnp.where(keep[g:g + 1] > 0.5, biased[g * epg:(g + 1) * epg], NEG) for g in range(N_EGROUPS)], axis=0)
    rowe = lax.broadcasted_iota(I32, (N_EXPERTS, tm), 0).astype(F32)
    chosen = jnp.zeros((N_EXPERTS, tm), F32)
    vals = masked
    picks = []
    for _ in range(TOP_K):
        mx = jnp.max(vals, axis=0, keepdims=True)
        first = jnp.min(jnp.where(vals == mx, rowe, float(N_EXPERTS)), axis=0, keepdims=True)
        hit = rowe == first
        chosen = jnp.where(hit, sc, chosen)
        vals = jnp.where(hit, ninf, vals)
        picks.append((hit, first))
    return ROUTED_SCALE * chosen / jnp.sum(chosen, axis=0, keepdims=True), picks


def _pack_bf16_pairs(x):
    c = x.shape[1] // 2
    bits = lambda v: lax.bitcast_convert_type(v.astype(BF16).astype(F32), jnp.uint32)
    return (bits(x[:, :c]) >> 16) | (bits(x[:, c:]) & jnp.uint32(0xFFFF0000))


def _unpack_bf16_pairs(w):
    lo = lax.bitcast_convert_type(w << 16, F32)
    hi = lax.bitcast_convert_type(w & jnp.uint32(0xFFFF0000), F32)
    return jnp.concatenate([lo, hi], axis=1)


def _finish_kernel(x_ref, pooled_ref, y_ref, gm_ref, g1_ref, shift_ref, scale_ref,
                   wlin_ref, pscale_ref, wpo_ref, wno_ref, wo_ref, n2_ref, wr_ref, br_ref,
                   *rest, tm, d, sparse):
    if sparse:
        tri_ref, x1_ref, u2_ref, up_ref, eid_ref, gk_ref, rank_ref, cnt_ref, carry_scr = rest
    else:
        x1_ref, u2_ref, gates_ref = rest
    x = x_ref[...].reshape(tm, d)
    pooled = pooled_ref[...].reshape(tm, POOL_W).astype(BF16)
    y_pool = _dot(pooled, wlin_ref[...]) * pscale_ref[...]
    a = _dot(y_pool.astype(BF16), wpo_ref[...])
    b = _dot(y_ref[...].reshape(tm, QPAD), wno_ref[...])
    gm = gm_ref[...].reshape(tm, 2 * d)
    merged = gm[:, :d] * a + gm[:, d:] * b
    x1 = x + _rows2d(g1_ref) * _dot(merged.astype(BF16), wo_ref[...])
    x1_ref[...] = x1.reshape(x1_ref.shape)
    u2 = _rmsnorm(x1, n2_ref[...]) * (1.0 + _rows2d(scale_ref)) + _rows2d(shift_ref)
    u2b = u2.astype(BF16)
    u2_ref[...] = u2b.reshape(u2_ref.shape)
    logits_t = _dot_nt(wr_ref[...], u2b)
    gates_t, picks = _route(logits_t[:N_EXPERTS], br_ref[...], tm)
    if not sparse:
        gates_t = jnp.concatenate([gates_t, jnp.zeros((LANES - N_EXPERTS, tm), F32)], axis=0)
        gates_ref[...] = gates_t.T.reshape(gates_ref.shape)
        return

    @pl.when((pl.program_id(0) == 0) & (pl.program_id(1) == 0))
    def _():
        carry_scr[...] = jnp.zeros_like(carry_scr)

    packed = _pack_bf16_pairs(u2)
    for s in range(up_ref.shape[0]):
        up_ref[s] = packed[:, s * SC_ROW_WORDS:(s + 1) * SC_ROW_WORDS]
    hit_all = picks[0][0]
    for hit, _ in picks[1:]:
        hit_all = hit_all | hit
    hits = jnp.where(hit_all, 1.0, 0.0).astype(BF16)
    before = _dot(hits, tri_ref[...]) + jnp.concatenate([carry_scr[...]] * (tm // LANES), axis=1)
    eids, gks, ranks = [], [], []
    for hit, first in picks:
        eids.append(first)
        gks.append(jnp.sum(jnp.where(hit, gates_t, 0.0), axis=0, keepdims=True))
        ranks.append(jnp.sum(jnp.where(hit, before, 0.0), axis=0, keepdims=True))
    pick_row = lax.broadcasted_iota(I32, (TOP_K, tm), 0)

    def stack(rows):
        out = jnp.zeros((TOP_K, tm), F32)
        for r, v in enumerate(rows):
            out = jnp.where(pick_row == r, v, out)
        return out

    eid_ref[...] = stack(eids).astype(I32).reshape(eid_ref.shape)
    gk_ref[...] = stack(gks).reshape(gk_ref.shape)
    rank_ref[...] = stack(ranks).astype(I32).reshape(rank_ref.shape)
    carry_scr[...] += _dot(hits, jnp.ones((tm, LANES), BF16))
    cnt_ref[...] = carry_scr[...]


def _finish(x3, pooled, ynsa, gm, mods, wts, *, tm, sparse):
    g, r, d = x3.shape
    nt = r // tm
    g1, shift2, scale2 = mods
    per_row = g1.ndim == 2
    tok = lambda w: pl.BlockSpec((1, tm, w), lambda b, j: (b, j, 0))
    tok_t = lambda rows: pl.BlockSpec((1, rows, tm), lambda b, j: (b, 0, j))
    if per_row:
        mod_spec = lambda col: pl.BlockSpec((tm, d), lambda b, j, col=col: (b * nt + j, col))
    else:
        mod_spec = lambda col: pl.BlockSpec((1, 1, d), lambda b, j, col=col: (b, 0, col))
    const = lambda a: pl.BlockSpec(a.shape, lambda b, j: (0,) * a.ndim)
    out_shape = [jax.ShapeDtypeStruct((g, r, d), F32), jax.ShapeDtypeStruct((g, r, d), BF16)]
    out_specs = [tok(d), tok(d)]
    scratch = []
    if sparse:
        tri = (jnp.arange(tm)[:, None] < jnp.arange(tm)[None, :]).astype(BF16)
        wts = tuple(wts) + (tri,)
        split = d // 2 // SC_ROW_WORDS
        out_shape += [jax.ShapeDtypeStruct((split, g * r, SC_ROW_WORDS), jnp.uint32),
                      jax.ShapeDtypeStruct((g, TOP_K, r), I32), jax.ShapeDtypeStruct((g, TOP_K, r), F32),
                      jax.ShapeDtypeStruct((g, TOP_K, r), I32), jax.ShapeDtypeStruct((N_EXPERTS, LANES), F32)]
        out_specs += [pl.BlockSpec((split, tm, SC_ROW_WORDS), lambda b, j: (0, b * nt + j, 0)),
                      tok_t(TOP_K), tok_t(TOP_K), tok_t(TOP_K),
                      pl.BlockSpec((N_EXPERTS, LANES), lambda b, j: (0, 0))]
        scratch.append(pltpu.VMEM((N_EXPERTS, LANES), F32))
    else:
        out_shape.append(jax.ShapeDtypeStruct((g, r, LANES), F32))
        out_specs.append(tok(LANES))
    return pl.pallas_call(
        functools.partial(_finish_kernel, tm=tm, d=d, sparse=sparse),
        out_shape=out_shape,
        grid=(g, nt),
        in_specs=[tok(d), tok(POOL_W), tok(QPAD), tok(2 * d), mod_spec(2), mod_spec(3), mod_spec(4)]
        + [const(w) for w in wts],
        out_specs=out_specs,
        scratch_shapes=scratch,
        compiler_params=_cparams(("arbitrary", "arbitrary")),
        name="finish_route" if sparse else "finish",
    )(x3, pooled, ynsa, gm, g1, shift2, scale2, *wts)


def _moe_kernel(u_ref, gates_ref, x1_ref, g2_ref, nf_ref, wg_ref, wu_ref, wd_ref, sg_ref, su_ref, sd_ref,
                y_ref, acc_ref, *, tm, d, eps):
    e = pl.program_id(2)
    u = u_ref[...].reshape(tm, d)

    @pl.when(e == 0)
    def _():
        hs = _silu(_dot(u, sg_ref[...])) * _dot(u, su_ref[...])
        acc_ref[...] = _dot(hs.astype(BF16), sd_ref[...])

    gates = gates_ref[...].reshape(tm, LANES)
    lane = lax.broadcasted_iota(I32, (1, LANES), 1)
    hidden = []
    for j in range(eps):
        h = _silu(_dot(u, wg_ref[j].astype(BF16))) * _dot(u, wu_ref[j].astype(BF16))
        gate = jnp.sum(jnp.where(lane == e * eps + j, gates, 0.0), axis=1, keepdims=True)
        hidden.append((h * gate).astype(BF16))
    f = wd_ref.shape[1]
    acc_ref[...] += _dot(jnp.concatenate(hidden, axis=1), wd_ref[...].reshape(eps * f, d).astype(BF16))

    @pl.when(e == pl.num_programs(2) - 1)
    def _():
        x2 = x1_ref[...].reshape(tm, d) + _rows2d(g2_ref) * acc_ref[...]
        y_ref[...] = _rmsnorm(x2, nf_ref[...]).reshape(y_ref.shape)


def _moe(u2, gates, x1, g2, normf, w_gate, w_up, w_down, sg, su, sd, *, tm):
    g, r, d = x1.shape
    nt = r // tm
    ne, _, f = w_gate.shape
    per_row = g2.ndim == 2
    tok = lambda w: pl.BlockSpec((1, tm, w), lambda b, j, e: (b, j, 0))
    if per_row:
        g2_spec = pl.BlockSpec((tm, d), lambda b, j, e: (b * nt + j, 5))
    else:
        g2_spec = pl.BlockSpec((1, 1, d), lambda b, j, e: (b, 0, 5))
    once = pl.Buffered(buffer_count=1)
    const = lambda a: pl.BlockSpec(a.shape, lambda b, j, e: (0,) * a.ndim, pipeline_mode=once)
    eps = 4
    return pl.pallas_call(
        functools.partial(_moe_kernel, tm=tm, d=d, eps=eps),
        out_shape=jax.ShapeDtypeStruct((g, r, d), F32),
        grid=(g, nt, ne // eps),
        in_specs=[tok(d), tok(LANES),
                  pl.BlockSpec((1, tm, d), lambda b, j, e: (b, j, 0), pipeline_mode=once),
                  g2_spec, const(normf),
                  pl.BlockSpec((eps, d, f), lambda b, j, e: (e, 0, 0)),
                  pl.BlockSpec((eps, d, f), lambda b, j, e: (e, 0, 0)),
                  pl.BlockSpec((eps, f, d), lambda b, j, e: (e, 0, 0)),
                  const(sg), const(su), const(sd)],
        out_specs=tok(d),
        scratch_shapes=[pltpu.VMEM((tm, d), F32)],
        compiler_params=_cparams(("arbitrary", "arbitrary", "arbitrary")),
        name="moe",
    )(u2, gates, x1, g2, normf, w_gate, w_up, w_down, sg, su, sd)


SC_WINDOW = 128
SC_ROW_WORDS = 256
MOE_ROWS = 512


def _sc_mesh():
    return plsc.VectorSubcoreMesh(core_axis_name="c", subcore_axis_name="s")


def _sc_scatter_rows(src, dst_idx, n_dst):
    n, w = src.shape
    nk = dst_idx.shape[0]

    @pl.kernel(out_type=jax.ShapeDtypeStruct((n_dst, w), src.dtype), mesh=_sc_mesh(), scratch_types=[])
    def scatter(src_hbm, idx_hbm, dst_hbm):
        def body(rows_vmem, idx_vmem):
            pltpu.sync_copy(rows_vmem, dst_hbm.at[idx_vmem.at[0]])

        pltpu.emit_pipeline(
            body,
            grid=(nk, n // SC_WINDOW),
            in_specs=[pl.BlockSpec((SC_WINDOW, w), index_map=lambda k, i: (i, 0)),
                      pl.BlockSpec((1, SC_WINDOW), index_map=lambda k, i: (k, i))],
            out_specs=[],
            core_axis_name=("c", "s"),
            dimension_semantics=(pltpu.PARALLEL, pltpu.PARALLEL),
        )(src_hbm, idx_hbm)

    return scatter(src, dst_idx)


def _sc_gather_rows(src, idx):
    n, w = idx.shape[0], src.shape[1]

    @pl.kernel(out_type=jax.ShapeDtypeStruct((n, w), src.dtype), mesh=_sc_mesh(), scratch_types=[])
    def gather(src_hbm, idx_hbm, out_hbm):
        def body(idx_vmem, out_vmem):
            pltpu.sync_copy(src_hbm.at[idx_vmem.at[0]], out_vmem)

        pltpu.emit_pipeline(
            body,
            grid=(n // SC_WINDOW,),
            in_specs=[pl.BlockSpec((1, SC_WINDOW), index_map=lambda i: (0, i))],
            out_specs=[pl.BlockSpec((SC_WINDOW, w), index_map=lambda i: (i, 0))],
            core_axis_name=("c", "s"),
            dimension_semantics=(pltpu.PARALLEL,),
        )(idx_hbm, out_hbm)

    return gather(src, idx.reshape(1, n))


MOE_BLOCKS_PER_STEP = 4


def _expert_rows_kernel(te_ref, nt_ref, x_ref, *refs):
    y_ref = refs[-1]
    split = x_ref.shape[0]
    for j in range(MOE_BLOCKS_PER_STEP):
        wg_ref, wu_ref, wd_ref = refs[3 * j:3 * j + 3]
        rows = slice(j * MOE_ROWS, (j + 1) * MOE_ROWS)

        @pl.when(pl.program_id(0) * MOE_BLOCKS_PER_STEP + j < nt_ref[0])
        def _():
            x = _unpack_bf16_pairs(jnp.concatenate([x_ref[s, rows] for s in range(split)], axis=1)).astype(BF16)
            h = _silu(_dot(x, wg_ref[...].astype(BF16))) * _dot(x, wu_ref[...].astype(BF16))
            y = _pack_bf16_pairs(_dot(h.astype(BF16), wd_ref[...].astype(BF16)))
            for s in range(split):
                y_ref[s, rows] = y[:, s * SC_ROW_WORDS:(s + 1) * SC_ROW_WORDS]


def _expert_rows(tile_expert, n_tiles, x_sorted, w_gate, w_up, w_down):
    split, p, words = x_sorted.shape
    ne, d, f = w_gate.shape
    bps = MOE_BLOCKS_PER_STEP
    wspec = lambda a, b, j: pl.BlockSpec((None, a, b), lambda i, te, nt: (te[i * bps + j], 0, 0))
    rows = pl.BlockSpec((split, bps * MOE_ROWS, words), lambda i, te, nt: (0, i, 0))
    weights, wspecs = [], []
    for j in range(bps):
        weights += [w_gate, w_up, w_down]
        wspecs += [wspec(d, f, j), wspec(d, f, j), wspec(f, d, j)]
    grid_spec = pltpu.PrefetchScalarGridSpec(
        num_scalar_prefetch=2,
        grid=(p // (bps * MOE_ROWS),),
        in_specs=[rows] + wspecs,
        out_specs=rows,
    )
    return pl.pallas_call(
        _expert_rows_kernel,
        out_shape=jax.ShapeDtypeStruct((split, p, words), jnp.uint32),
        grid_spec=grid_spec,
        compiler_params=_cparams(("arbitrary",)),
        name="expert_rows",
    )(tile_expert, n_tiles, x_sorted, *weights)


def _combine_kernel(yg_ref, gk_ref, u_ref, x1_ref, g2_ref, nf_ref, sg_ref, su_ref, sd_ref, y_ref, *, tm, d):
    u = u_ref[...].reshape(tm, d)
    hs = _silu(_dot(u, sg_ref[...])) * _dot(u, su_ref[...])
    acc = _dot(hs.astype(BF16), sd_ref[...])
    gk = gk_ref[...].reshape(tm, LANES)
    lane = lax.broadcasted_iota(I32, (1, LANES), 1)
    split = yg_ref.shape[0]
    for k in range(TOP_K):
        gate = jnp.sum(jnp.where(lane == k, gk, 0.0), axis=1, keepdims=True)
        words = jnp.concatenate([yg_ref[s, k] for s in range(split)], axis=1)
        acc = acc + gate * _unpack_bf16_pairs(words)
    x2 = x1_ref[...].reshape(tm, d) + _rows2d(g2_ref) * acc
    y_ref[...] = _rmsnorm(x2, nf_ref[...]).reshape(y_ref.shape)


def _combine(yg, gk, u2, x1, g2, normf, sg, su, sd, *, tm):
    g, r, d = x1.shape
    nt = r // tm
    split, _, _, words = yg.shape
    tok = lambda w: pl.BlockSpec((1, tm, w), lambda b, j: (b, j, 0))
    const = lambda a: pl.BlockSpec(a.shape, lambda b, j: (0,) * a.ndim)
    return pl.pallas_call(
        functools.partial(_combine_kernel, tm=tm, d=d),
        out_shape=jax.ShapeDtypeStruct((g, r, d), F32),
        grid=(g, nt),
        in_specs=[pl.BlockSpec((split, TOP_K, tm, words), lambda b, j: (0, 0, b * nt + j, 0)),
                  tok(LANES), tok(d), tok(d),
                  pl.BlockSpec((1, 1, d), lambda b, j: (b, 0, 5)), const(normf), const(sg), const(su), const(sd)],
        out_specs=tok(d),
        compiler_params=_cparams(("arbitrary", "arbitrary")),
        name="moe_combine",
    )(yg, gk, u2, x1, g2, normf, sg, su, sd)


def _moe_sorted_experts(u2p, eid_t, rank_t, counts, w_gate, w_up, w_down):
    split, n, _ = u2p.shape
    ne = w_gate.shape[0]
    cnt = counts[:, 0].astype(I32)
    padded = -(-cnt // MOE_ROWS) * MOE_ROWS
    seg_end = jnp.cumsum(padded)
    seg_start = seg_end - padded
    p_rows = n * TOP_K + ne * MOE_ROWS
    eid = eid_t.transpose(1, 0, 2).reshape(TOP_K, n)
    start = jnp.sum(jnp.where(eid[:, :, None] == jnp.arange(ne, dtype=I32), seg_start, 0), axis=-1)
    pos = start + rank_t.transpose(1, 0, 2).reshape(TOP_K, n)
    first_row = jnp.arange(p_rows // MOE_ROWS, dtype=I32) * MOE_ROWS
    tile_expert = jnp.minimum(jnp.sum(seg_end[None, :] <= first_row[:, None], axis=1), ne - 1).astype(I32)
    n_tiles = (seg_end[-1:] // MOE_ROWS).astype(I32)
    scat_idx = jnp.concatenate([pos + s * p_rows for s in range(split)], axis=1)
    gath_idx = jnp.concatenate([pos.reshape(-1) + s * p_rows for s in range(split)])
    x_sorted = _sc_scatter_rows(u2p.reshape(split * n, SC_ROW_WORDS), scat_idx, split * p_rows)
    y_sorted = _expert_rows(tile_expert, n_tiles, x_sorted.reshape(split, p_rows, SC_ROW_WORDS),
                            w_gate, w_up, w_down)
    return y_sorted.reshape(split * p_rows, SC_ROW_WORDS), gath_idx


def _moe_sorted_combine(y_sorted, gath_idx, gk_t, u2, x1, g2, normf, sg, su, sd):
    g, r, _ = x1.shape
    split = gath_idx.shape[0] // (TOP_K * g * r)
    yg = _sc_gather_rows(y_sorted, gath_idx)
    gk = jnp.pad(gk_t.transpose(0, 2, 1), ((0, 0), (0, 0), (0, LANES - TOP_K)))
    return _combine(yg.reshape(split, TOP_K, g * r, SC_ROW_WORDS), gk, u2, x1, g2, normf, sg, su, sd, tm=512)


def _kv_slot_mask():
    return (jnp.arange(N_HEADS)[:, None] // GROUP == jnp.arange(N_KV)[None, :]).astype(F32)


def _prep_w_in(w_in, d):
    q0 = POOL_W
    kv0 = q0 + N_HEADS * HEAD_DIM
    gn0 = kv0 + 6 * KVW
    gm0 = gn0 + 3 * N_HEADS
    wq = w_in[:, q0:kv0].reshape(d, N_HEADS, 1, HEAD_DIM) * (HEAD_DIM ** -0.5)
    wq = (wq * _kv_slot_mask()[None, :, :, None]).reshape(d, QPAD)
    wgn = jnp.pad(w_in[:, gn0:gm0], ((0, 0), (0, LANES - 3 * N_HEADS)))
    return jnp.concatenate([w_in[:, :q0], wq, w_in[:, kv0:gn0], wgn, w_in[:, gm0:]], axis=1).astype(BF16)


def _prep_w_nsa_out(w, d):
    w = w.reshape(N_HEADS, 1, HEAD_DIM, d) * _kv_slot_mask()[:, :, None, None]
    return w.reshape(QPAD, d).astype(BF16)


def _block_diag(w_lin):
    g, c, _ = w_lin.shape
    eye = jnp.eye(g, dtype=F32)
    return (w_lin[:, :, None, :] * eye[:, None, :, None]).reshape(g * c, g * c).astype(BF16)


def kernel(x_prompt, x_sample, cache_kc, cache_vc, cache_ks, cache_vs, state_kw, state_vw, state_pool,
           page_table, c_prompt, c_sample, norm1_g, norm2_g, normf_g, w_ada, b_ada, w_in, w_pool_lin,
           pool_scale, w_cmp_k, w_cmp_v, w_pool_out, w_nsa_out, w_o, w_router, b_router, w_gate, w_up,
           w_down, ws_gate, ws_up, ws_down):
    depth = w_in.shape[0]
    assert depth == 1, "single-layer stack"
    bsz, seq, d = x_prompt.shape
    n_seq, ns, _ = x_sample.shape
    wbuf = state_kw.shape[2]
    lyr = 0

    c_all = jnp.concatenate([c_prompt, c_sample], axis=0)
    rows = c_all.shape[0]
    rows_p = -(-rows // 8) * 8
    mod = _adaln(jnp.pad(c_all, ((0, rows_p - rows), (0, 0))), w_ada[lyr], b_ada[lyr])
    mod_p = mod[:bsz].reshape(bsz, 1, 6 * d)
    mod_s = jnp.repeat(mod[bsz:bsz + n_seq], ns, axis=0)

    w2 = _prep_w_in(w_in[lyr], d)
    g1n = norm1_g[lyr].reshape(1, d)
    wk = w_cmp_k[lyr].reshape(CMP_BLOCK, KVW)
    wv = w_cmp_v[lyr].reshape(CMP_BLOCK, KVW)
    fin_w = (_block_diag(w_pool_lin[lyr]), pool_scale[lyr].reshape(1, POOL_W), w_pool_out[lyr].astype(BF16),
             _prep_w_nsa_out(w_nsa_out[lyr], d), w_o[lyr].astype(BF16), norm2_g[lyr].reshape(1, d),
             jnp.pad(w_router[lyr].T, ((0, LANES - N_EXPERTS), (0, 0))).astype(BF16),
             b_router[lyr].reshape(N_EXPERTS, 1))
    moe_w = (w_gate[lyr], w_up[lyr], w_down[lyr], ws_gate[lyr].astype(BF16), ws_up[lyr].astype(BF16),
             ws_down[lyr].astype(BF16))
    nf = normf_g.reshape(1, d)

    tm_p = 512
    (vp, kc, vc, ks, vs, kw, vw, gm, ksb, kwb, vst, vwt, qt, gst, pooled) = _in_proj(
        x_prompt, mod_p, mod_p, g1n, w2, tm=tm_p, prompt=True)
    kcmp, vcmpt = _compress(kc, vc, wk, wv)
    ynsa = _nsa_prompt(qt, gst, kcmp, vcmpt, ksb, vst, kwb, vwt)
    x1, u2, u2p, eid_t, gk_t, rank_t, counts = _finish(
        x_prompt, pooled, ynsa, gm, (mod_p, mod_p, mod_p), fin_w, tm=tm_p, sparse=True)

    n_tok = n_seq * ns
    xs3 = x_sample.reshape(1, n_tok, d)
    tm_s = 128
    (vp_s, kc_s, vc_s, ks_s, vs_s, kw_s, vw_s, gm_s, q_s, gs_s) = _in_proj(
        xs3, mod_s, mod_s, g1n, w2, tm=tm_s, prompt=False)
    two = lambda a: a.reshape(n_tok, a.shape[-1])
    q_rows = q_s.reshape(n_tok * N_HEADS, LANES)
    gate_rows = two(gs_s)[:, :3 * N_HEADS].reshape(n_tok, 3, N_HEADS).transpose(0, 2, 1)
    gate_rows = jnp.pad(gate_rows.reshape(n_tok * N_HEADS, 3), ((0, 0), (0, LANES - 3)))
    n_pool = cache_kc.shape[1]
    page = cache_kc.shape[2]
    rows_minor = lambda a: jnp.transpose(a, (0, 2, 3, 1)).reshape(a.shape[0], KVW, a.shape[1])
    caches = [rows_minor(c[lyr]) for c in (cache_kc, cache_vc, cache_ks, cache_vs)]
    o_rows, pooled_s = _nsa_sample(
        page_table, q_rows, gate_rows, [two(a) for a in (kc_s, vc_s, ks_s, vs_s, kw_s, vw_s)], two(vp_s),
        rows_minor(state_kw[lyr]), rows_minor(state_vw[lyr]), state_pool[lyr], wk, wv, caches)
    ynsa_s = o_rows.reshape(1, n_tok, QPAD)
    y_sorted, gath_idx = _moe_sorted_experts(u2p, eid_t, rank_t, counts, *moe_w[:3])
    ynsa_s, y_sorted = lax.optimization_barrier((ynsa_s, y_sorted))
    y_prompt = _moe_sorted_combine(y_sorted, gath_idx, gk_t, u2, x1, mod_p, nf, *moe_w[3:])
    x1_s, u2_s, gates_s = _finish(xs3, pooled_s.reshape(1, n_tok, POOL_W), ynsa_s, gm_s,
                                  (mod_s, mod_s, mod_s), fin_w, tm=tm_s, sparse=False)
    y_sample = _moe(u2_s, gates_s, x1_s, mod_s, nf, *moe_w, tm=n_tok).reshape(n_seq, ns, d)

    kvp = lambda a: a.reshape(1, bsz, seq, N_KV, HEAD_DIM)
    tailp = lambda a: jnp.pad(a, ((0, 0), (wbuf, 0), (0, 0)))[:, -wbuf:].reshape(1, bsz, wbuf, N_KV, HEAD_DIM)
    kvs = lambda a: a.reshape(1, n_seq, ns, N_KV, HEAD_DIM)
    wins = lambda st, new: jnp.concatenate(
        [st[lyr], new.reshape(n_seq, ns, N_KV, HEAD_DIM)], axis=1)[None, :, -wbuf:]
    pool_p = vp[:, -POOL_BUF:][None]
    pool_s = jnp.concatenate([state_pool[lyr], vp_s.reshape(n_seq, ns, POOL_W)], axis=1)[None, :, -POOL_BUF:]
    return (y_prompt, y_sample, kvp(kc), kvp(vc), kvp(ks), kvp(vs), tailp(kw), tailp(vw), pool_p,
            kvs(kc_s), kvs(vc_s), kvs(ks_s), kvs(vs_s), wins(state_kw, kw_s), wins(state_vw, vw_s), pool_s)
```

```python
import functools

import jax
import jax.numpy as jnp
from jax import lax
from jax.experimental import pallas as pl
from jax.experimental.pallas import tpu as pltpu
from jax.experimental.pallas import tpu_sc as plsc

F32 = jnp.float32
BF16 = jnp.bfloat16
I32 = jnp.int32

POOL_WINDOWS = (2, 4, 8, 16)
POOL_GW = 64
POOL_W = 256
POOL_BUF = 15
N_HEADS = 8
HEAD_DIM = 64
N_KV = 2
GROUP = N_HEADS // N_KV
CMP_STRIDE = 16
CMP_BLOCK = 32
SEL_BLOCK = 64
TOP_BLOCKS = 16
WINDOW = 512
Q_BLOCK = 128
FORCE_SCORE = 1e4
N_EXPERTS = 64
N_EGROUPS = 8
EXPERTS_PER_GROUP = N_EXPERTS // N_EGROUPS
TOPK_GROUPS = 4
TOP_K = 8
ROUTED_SCALE = 2.5
EPS = 1e-6
NEG = -1e30
SLOPES = tuple(2.0 ** (-8.0 * (h + 1.0) / N_HEADS) for h in range(N_HEADS))

LANES = 128
QPAD = N_HEADS * LANES
KVW = N_KV * HEAD_DIM
VMEM_LIMIT = 56 * 1024 * 1024


def _cparams(sem):
    return pltpu.CompilerParams(dimension_semantics=sem, vmem_limit_bytes=VMEM_LIMIT)


def _dot(a, b):
    return jnp.dot(a, b, preferred_element_type=F32)


def _dot_nt(a, b):
    return lax.dot_general(a, b, (((1,), (1,)), ((), ())), preferred_element_type=F32)


def _dot_exact(a, b):
    return jnp.dot(a, b, preferred_element_type=F32, precision=lax.Precision.HIGHEST)


def _rows2d(ref):
    v = ref[...]
    return v.reshape(v.shape[-2], v.shape[-1])


def _rmsnorm(x, g):
    return x * lax.rsqrt(jnp.mean(x * x, axis=-1, keepdims=True) + EPS) * g


def _silu(x):
    return x * jax.nn.sigmoid(x)


def _adaln_kernel(c_ref, w_ref, b_ref, o_ref):
    s = _silu(c_ref[...]).astype(BF16)
    o_ref[...] = _dot(s, w_ref[...].astype(BF16)) + b_ref[...]


def _adaln(c, w_ada, b_ada):
    rows, d = c.shape
    n = w_ada.shape[1]
    tn = 512
    return pl.pallas_call(
        _adaln_kernel,
        out_shape=jax.ShapeDtypeStruct((rows, n), F32),
        grid=(n // tn,),
        in_specs=[pl.BlockSpec((rows, d), lambda j: (0, 0)),
                  pl.BlockSpec((d, tn), lambda j: (0, j)),
                  pl.BlockSpec((1, tn), lambda j: (0, j))],
        out_specs=pl.BlockSpec((rows, tn), lambda j: (0, j)),
        compiler_params=_cparams(("arbitrary",)),
        name="adaln",
    )(c, w_ada, b_ada.reshape(1, n))


_C_VP = 0
_C_Q = _C_VP + POOL_W
_C_KV = _C_Q + QPAD
_C_GN = _C_KV + 6 * KVW
_C_GM = _C_GN + LANES


def _pool_window_sums(ext, tm):
    s2 = ext + pltpu.roll(ext, 1, 0)
    s4 = s2 + pltpu.roll(s2, 2, 0)
    s8 = s4 + pltpu.roll(s4, 4, 0)
    s16 = s8 + pltpu.roll(s8, 8, 0)
    grp = lax.broadcasted_iota(I32, (1, POOL_W), 1) // POOL_GW
    pick = jnp.where(grp == 0, s2, jnp.where(grp == 1, s4, jnp.where(grp == 2, s8, s16)))
    return pick[16:16 + tm]


def _in_proj_kernel(x_ref, shift_ref, scale_ref, g_ref, w_ref,
                    vp_ref, kc_ref, vc_ref, ks_ref, vs_ref, kw_ref, vw_ref, gm_ref, *rest, tm, d, prompt):
    x = x_ref[...].reshape(tm, d)
    u = _rmsnorm(x, g_ref[...]) * (1.0 + _rows2d(scale_ref)) + _rows2d(shift_ref)
    ub = u.astype(BF16)

    head = _dot(ub, w_ref[:, 0:_C_GM])

    def proj(c0, n):
        return head[:, c0:c0 + n] if c0 + n <= _C_GM else _dot(ub, w_ref[:, c0:c0 + n])

    vp = proj(_C_VP, POOL_W)
    vp_ref[...] = vp.reshape(vp_ref.shape)
    kv = []
    for n, o32 in enumerate((kc_ref, vc_ref, ks_ref, vs_ref, kw_ref, vw_ref)):
        v = proj(_C_KV + n * KVW, KVW)
        o32[...] = v.reshape(o32.shape)
        kv.append(v)
    gm_ref[...] = jax.nn.sigmoid(proj(_C_GM, 2 * d)).reshape(gm_ref.shape)
    gs = jax.nn.sigmoid(proj(_C_GN, LANES))

    if not prompt:
        q_ref, gs_ref = rest
        q_ref[...] = proj(_C_Q, QPAD).astype(BF16).reshape(q_ref.shape)
        gs_ref[...] = gs.reshape(gs_ref.shape)
    else:
        ksb_ref, kwb_ref, vst_ref, vwt_ref, qt_ref, gst_ref, pooled_ref, halo_ref = rest
        ksb_ref[...] = kv[2].astype(BF16).reshape(ksb_ref.shape)
        kwb_ref[...] = kv[4].astype(BF16).reshape(kwb_ref.shape)
        vst_ref[...] = kv[3].T.astype(BF16).reshape(vst_ref.shape)
        vwt_ref[...] = kv[5].T.astype(BF16).reshape(vwt_ref.shape)
        gst_ref[...] = gs.T.reshape(gst_ref.shape)
        for h in range(N_HEADS):
            qt_ref[0, h] = proj(_C_Q + h * LANES, LANES).T.astype(BF16)
        j = pl.program_id(1)

        @pl.when(j == 0)
        def _():
            halo_ref[...] = jnp.zeros_like(halo_ref)

        ext = jnp.concatenate([halo_ref[...], vp], axis=0)
        sums = _pool_window_sums(ext, tm)
        pos = j * tm + lax.broadcasted_iota(I32, (tm, 1), 0)
        wcol = 2 << (lax.broadcasted_iota(I32, (1, POOL_W), 1) // POOL_GW)
        cnt = jnp.minimum(pos + 1, wcol).astype(F32)
        pooled_ref[...] = (sums / cnt - vp).astype(BF16).reshape(pooled_ref.shape)
        halo_ref[...] = vp[tm - 16:tm]


def _in_proj(x3, shift, scale, g1, w2, *, tm, prompt):
    g, r, d = x3.shape
    nt = r // tm
    per_row = shift.ndim == 2

    def tok(width, dtype):
        return (jax.ShapeDtypeStruct((g, r, width), dtype),
                pl.BlockSpec((1, tm, width), lambda b, j: (b, j, 0)))

    def tok_t(rows, dtype):
        return (jax.ShapeDtypeStruct((g, rows, r), dtype),
                pl.BlockSpec((1, rows, tm), lambda b, j: (b, 0, j)))

    outs = [tok(POOL_W, F32)] + [tok(KVW, F32)] * 6 + [tok(2 * d, F32)]
    scratch = []
    if prompt:
        outs += [tok(KVW, BF16), tok(KVW, BF16), tok_t(KVW, BF16), tok_t(KVW, BF16)]
        outs.append((jax.ShapeDtypeStruct((g, N_HEADS, LANES, r), BF16),
                     pl.BlockSpec((1, N_HEADS, LANES, tm), lambda b, j: (b, 0, 0, j))))
        outs += [tok_t(LANES, F32), tok(POOL_W, BF16)]
        scratch.append(pltpu.VMEM((16, POOL_W), F32))
    else:
        outs += [tok(QPAD, BF16), tok(LANES, F32)]
    if per_row:
        mod_spec = lambda col: pl.BlockSpec((tm, d), lambda b, j, col=col: (b * nt + j, col))
    else:
        mod_spec = lambda col: pl.BlockSpec((1, 1, d), lambda b, j, col=col: (b, 0, col))
    kern = functools.partial(_in_proj_kernel, tm=tm, d=d, prompt=prompt)
    return pl.pallas_call(
        kern,
        out_shape=[o[0] for o in outs],
        grid=(g, nt),
        in_specs=[pl.BlockSpec((1, tm, d), lambda b, j: (b, j, 0)),
                  mod_spec(0), mod_spec(1),
                  pl.BlockSpec((1, d), lambda b, j: (0, 0)),
                  pl.BlockSpec(w2.shape, lambda b, j: (0, 0))],
        out_specs=[o[1] for o in outs],
        scratch_shapes=scratch,
        compiler_params=_cparams(("arbitrary", "arbitrary")),
        name="in_proj_prompt" if prompt else "in_proj_sample",
    )(x3, shift, scale, g1, w2)


def _compress_kernel(kc_ref, vc_ref, wk_ref, wv_ref, okc_ref, ovc_ref, sh_ref, *, nc):
    last = lax.broadcasted_iota(I32, (nc, 1), 0) == nc - 1
    for src, w_ref, dst in ((kc_ref, wk_ref, okc_ref), (vc_ref, wv_ref, ovc_ref)):
        head = jnp.zeros((nc, KVW), F32)
        tail = jnp.zeros((nc, KVW), F32)
        for r in range(CMP_STRIDE):
            rows = src[pl.ds(r, nc, stride=CMP_STRIDE), :]
            head = head + rows * w_ref[r:r + 1, :]
            tail = tail + rows * w_ref[CMP_STRIDE + r:CMP_STRIDE + r + 1, :]
        sh_ref[0:nc, :] = tail
        sh_ref[nc:nc + 8, :] = jnp.zeros((8, KVW), F32)
        out = jnp.where(last, 0.0, head + sh_ref[1:nc + 1, :])
        dst[...] = (out if dst is okc_ref else out.T).astype(BF16)


def _compress(kc, vc, wk, wv):
    b, s, _ = kc.shape
    nc = s // CMP_STRIDE
    big = pl.BlockSpec((None, s, KVW), lambda i: (i, 0, 0))
    wsp = pl.BlockSpec((CMP_BLOCK, KVW), lambda i: (0, 0))
    return pl.pallas_call(
        functools.partial(_compress_kernel, nc=nc),
        out_shape=[jax.ShapeDtypeStruct((b, nc, KVW), BF16), jax.ShapeDtypeStruct((b, KVW, nc), BF16)],
        grid=(b,),
        in_specs=[big, big, wsp, wsp],
        out_specs=[pl.BlockSpec((None, nc, KVW), lambda i: (i, 0, 0)),
                   pl.BlockSpec((None, KVW, nc), lambda i: (i, 0, 0))],
        scratch_shapes=[pltpu.VMEM((nc + 8, KVW), F32)],
        compiler_params=_cparams(("arbitrary",)),
        name="compress",
    )(kc, vc, wk, wv)


def _topk_mask(vals, blk_f, n_top, axis=1):
    sel = jnp.zeros(vals.shape, F32)
    big = float(vals.shape[axis])
    for _ in range(n_top):
        mx = jnp.max(vals, axis=axis, keepdims=True)
        first = jnp.min(jnp.where(vals == mx, blk_f, big), axis=axis, keepdims=True)
        hit = blk_f == first
        sel = jnp.where(hit, 1.0, sel)
        vals = jnp.where(hit, -jnp.inf, vals)
    return sel


def _topk_mask_by_rank(vals, blk, n_valid, n_top):
    rank = jnp.zeros(vals.shape, F32)
    for j in range(n_valid):
        vj = vals[:, j:j + 1]
        beats = (vj > vals) | ((vj == vals) & (blk > j))
        rank = rank + jnp.where(beats, 1.0, 0.0)
    return jnp.where(rank < float(n_top), 1.0, 0.0)


def _pos_features(pos):
    hi = (pos // SEL_BLOCK).astype(F32)[:, None]
    lo = (pos % SEL_BLOCK).astype(F32)[:, None]
    return jnp.concatenate([hi, lo, jnp.zeros((pos.shape[0], LANES - 2), F32)], axis=1).astype(BF16)


def _importance_matrix(nc, nsel):
    j = jnp.arange(nc)[:, None]
    s = jnp.arange(nsel)[None, :]
    r = SEL_BLOCK // CMP_STRIDE
    a = (j >= r * s) & (j <= r * s + r - 1)
    b = (j + 1 >= r * s) & (j + 1 <= r * s + r - 1)
    return a.astype(F32) + b.astype(F32)


def _nsa_prompt_kernel(qt_ref, gst_ref, kc_ref, vct_ref, ks_ref, vst_ref, kw_ref, vwt_ref,
                       cfeat_ref, wfeat_ref, qfeat_ref, slope_ref,
                       y_ref, qk_scr, m_scr, l_scr, acc_scr, o_scr, sel_scr, imp_scr, flag_scr, ids_scr,
                       *, seq, tk, wl):
    i = pl.program_id(1)
    q0 = i * Q_BLOCK
    nq = Q_BLOCK
    gq = GROUP * nq
    nc = kc_ref.shape[0]
    nsel = seq // SEL_BLOCK
    n_top = min(TOP_BLOCKS, nsel)
    blk_per_tile = tk // SEL_BLOCK
    qpos = q0 + lax.broadcasted_iota(I32, (1, nq), 1)
    gst = gst_ref[...]

    crow = lax.broadcasted_iota(I32, (nc, nq), 0)
    cend = crow * CMP_STRIDE + (CMP_BLOCK - 1)
    mask_c = qpos >= cend
    kc = jnp.concatenate([kc_ref[...], cfeat_ref[...]], axis=1)
    vct = vct_ref[...]
    blk = lax.broadcasted_iota(I32, (nsel, nq), 0)
    blk_f = blk.astype(F32)
    cur = qpos // SEL_BLOCK
    forced = (blk == 0) | (blk == cur) | (blk == cur - 1)
    visible = blk * SEL_BLOCK <= qpos
    ws = pl.multiple_of(jnp.maximum(q0 - WINDOW, 0), Q_BLOCK)
    wpos = ws + lax.broadcasted_iota(I32, (wl, nq), 0)
    valid_w = lax.bitcast_convert_type(qpos - wpos, jnp.uint32) < WINDOW
    n_tiles = (q0 + nq + tk - 1) // tk
    half_rows = lax.broadcasted_iota(I32, (KVW, nq), 0) // HEAD_DIM
    tile_pos = lax.broadcasted_iota(I32, (SEL_BLOCK, nq), 0)

    def lanes4(x):
        return jnp.concatenate([x] * GROUP, axis=1)

    def gate_row(branch, k):
        r0 = branch * N_HEADS + k * GROUP
        return jnp.concatenate([gst[r0 + g:r0 + g + 1] for g in range(GROUP)], axis=1)

    mask_c4 = lanes4(mask_c)
    valid_w4 = lanes4(valid_w)
    kwt = jnp.concatenate([kw_ref[pl.ds(ws, wl), :], wfeat_ref[...]], axis=1)
    vwtt = vwt_ref[:, pl.ds(ws, wl)]

    for k in range(N_KV):
        for g in range(GROUP):
            qk_scr[k, 0:LANES, g * nq:(g + 1) * nq] = qt_ref[k * GROUP + g]
        qk_scr[k, LANES:2 * LANES, :] = qfeat_ref[k]
        qk = qk_scr[k]

        s = jnp.where(mask_c4, _dot(kc, qk), NEG)
        e = jnp.where(mask_c4, jnp.exp(s - jnp.max(s, axis=0, keepdims=True)), 0.0)
        l = jnp.sum(e, axis=0, keepdims=True)
        p = e * jnp.where(l > 0.0, 1.0 / l, 0.0)
        o_c = _dot(vct, p.astype(BF16))
        psum = p[:, 0:nq]
        for g in range(1, GROUP):
            psum = psum + p[:, g * nq:(g + 1) * nq]

        a = psum + jnp.where(crow == 0, 0.0, pltpu.roll(psum, 1, 0))
        a = a + pltpu.roll(a, nc - 1, 0)
        imp_scr[...] = a + pltpu.roll(a, nc - 2, 0)
        imp = imp_scr[pl.ds(0, nsel, stride=nc // nsel), :]
        vals = jnp.where(visible, jnp.where(forced, FORCE_SCORE, imp), NEG)
        sel = jnp.where(visible, _topk_mask(vals, blk_f, n_top, axis=0), 0.0)
        sel_scr[k] = jnp.where(sel > 0.5, 0.0, NEG)
        blk_any = jnp.max(sel, axis=1, keepdims=True)
        for t in range(seq // tk):
            hit = (jnp.max(blk_any[t * blk_per_tile:(t + 1) * blk_per_tile, :]) > 0.5).astype(I32)
            flag_scr[t] = hit if k == 0 else flag_scr[t] | hit

        s = jnp.where(valid_w4, _dot(kwt, qk), NEG)
        e = jnp.exp(s - jnp.max(s, axis=0, keepdims=True))
        p = e / jnp.sum(e, axis=0, keepdims=True)
        o_w = _dot(vwtt, p.astype(BF16))
        o_scr[k] = gate_row(0, k) * o_c + gate_row(2, k) * o_w

    m_scr[...] = jnp.full(m_scr.shape, NEG, F32)
    l_scr[...] = jnp.zeros(l_scr.shape, F32)
    acc_scr[...] = jnp.zeros(acc_scr.shape, F32)

    n_act = jnp.int32(0)
    for t in range(seq // tk):
        ids_scr[n_act] = t
        n_act = n_act + jnp.where((flag_scr[t] > 0) & (t < n_tiles), 1, 0)
    lane0 = lax.broadcasted_iota(I32, (tk, LANES), 1) == 0
    feat = wfeat_ref[0:tk, :]

    def sel_pair(i, carry):
        ta = ids_scr[2 * i]
        has_b = 2 * i + 1 < n_act
        tb = jnp.where(has_b, ids_scr[2 * i + 1], ta)
        ka = pl.multiple_of(ta * tk, tk)
        kb = pl.multiple_of(tb * tk, tk)
        feat_b = jnp.where(lane0, (feat.astype(F32) + ((tb - ta) * blk_per_tile).astype(F32)), feat.astype(F32))
        kt = jnp.concatenate([jnp.concatenate([ks_ref[pl.ds(ka, tk), :], feat], axis=1),
                              jnp.concatenate([ks_ref[pl.ds(kb, tk), :], feat_b.astype(BF16)], axis=1)],
                             axis=0)
        vtt = jnp.concatenate([vst_ref[:, pl.ds(ka, tk)], vst_ref[:, pl.ds(kb, tk)]], axis=1)
        base = (ka - q0).astype(F32)
        masked_b = jnp.where(has_b, 0.0, NEG)
        scores = _dot(kt, jnp.concatenate([qk_scr[k] for k in range(N_KV)], axis=1))
        probs, alphas = [], []
        for k in range(N_KV):
            neg = []
            for t, k0, extra in ((ta, ka, 0.0), (tb, kb, masked_b)):
                for j in range(blk_per_tile):
                    row = sel_scr[k, pl.ds(t * blk_per_tile + j, 1), :] + extra
                    causal = qpos >= k0 + j * SEL_BLOCK + tile_pos
                    neg.append(jnp.where(causal, jnp.broadcast_to(row, (SEL_BLOCK, nq)), NEG))
            neg = lanes4(jnp.concatenate(neg, axis=0))
            off = slope_ref[k] * base
            s = scores[:, k * gq:(k + 1) * gq] + neg
            m_old = m_scr[k]
            m_new = jnp.maximum(m_old, jnp.max(s, axis=0, keepdims=True) + off)
            alpha = jnp.exp(m_old - m_new)
            p = jnp.exp(s - (m_new - off))
            l_scr[k] = alpha * l_scr[k] + jnp.sum(p, axis=0, keepdims=True)
            m_scr[k] = m_new
            probs.append(p.astype(BF16))
            alphas.append(alpha)
        pv = _dot(vtt, jnp.concatenate(probs, axis=1))
        for k in range(N_KV):
            acc_scr[k] = acc_scr[k] * alphas[k] + pv[:, k * gq:(k + 1) * gq]
        return carry

    lax.fori_loop(0, (n_act + 1) // 2, sel_pair, 0)

    for k in range(N_KV):
        o = o_scr[k] + gate_row(1, k) * (acc_scr[k] / l_scr[k])
        for g in range(GROUP):
            h = k * GROUP + g
            oh = jnp.where(half_rows == k, o[:, g * nq:(g + 1) * nq], 0.0)
            y_ref[:, h * LANES:(h + 1) * LANES] = oh.T.astype(BF16)


def _nsa_prompt(qt, gst, kcmp, vcmpt, ksb, vst, kwb, vwt):
    b, _, _, s = qt.shape
    nq = Q_BLOCK
    gq = GROUP * nq
    nc = kcmp.shape[1]
    nsel = s // SEL_BLOCK
    tk = 256
    wl = WINDOW + Q_BLOCK
    assert s % tk == 0 and s >= wl
    assert s // SEL_BLOCK <= 2 * LANES, "position // 64 must stay exact in bf16"
    cfeat = _pos_features(jnp.arange(nc) * CMP_STRIDE + (CMP_BLOCK - 1))
    wfeat = _pos_features(jnp.arange(wl))
    slope_rows = jnp.repeat(jnp.asarray(SLOPES, F32).reshape(N_KV, 1, GROUP), nq, axis=2)
    qfeat = jnp.concatenate([slope_rows * SEL_BLOCK, slope_rows, jnp.zeros((N_KV, LANES - 2, gq), F32)],
                            axis=1).astype(BF16)
    rows = lambda r: pl.BlockSpec((None, r, KVW), lambda bi, i: (bi, 0, 0))
    cols = lambda c: pl.BlockSpec((None, KVW, c), lambda bi, i: (bi, 0, 0))
    const = lambda a: pl.BlockSpec(a.shape, lambda bi, i: (0,) * a.ndim)
    return pl.pallas_call(
        functools.partial(_nsa_prompt_kernel, seq=s, tk=tk, wl=wl),
        out_shape=jax.ShapeDtypeStruct((b, s, QPAD), BF16),
        grid=(b, s // nq),
        in_specs=[pl.BlockSpec((None, N_HEADS, LANES, nq), lambda bi, i: (bi, 0, 0, i)),
                  pl.BlockSpec((None, LANES, nq), lambda bi, i: (bi, 0, i)),
                  rows(nc), cols(nc), rows(s), cols(s), rows(s), cols(s),
                  const(cfeat), const(wfeat), const(qfeat), const(slope_rows)],
        out_specs=pl.BlockSpec((None, nq, QPAD), lambda bi, i: (bi, i, 0)),
        scratch_shapes=[pltpu.VMEM((N_KV, 2 * LANES, gq), BF16),
                        pltpu.VMEM((N_KV, 1, gq), F32),
                        pltpu.VMEM((N_KV, 1, gq), F32),
                        pltpu.VMEM((N_KV, KVW, gq), F32),
                        pltpu.VMEM((N_KV, KVW, gq), F32),
                        pltpu.VMEM((N_KV, nsel, nq), F32),
                        pltpu.VMEM((nc, nq), F32),
                        pltpu.SMEM((s // tk,), I32),
                        pltpu.SMEM((s // tk,), I32)],
        compiler_params=_cparams(("arbitrary", "arbitrary")),
        name="nsa_prompt",
    )(qt, gst, kcmp, vcmpt, ksb, vst, kwb, vwt, cfeat, wfeat, qfeat, slope_rows)


def _nsa_sample_kernel(pt_ref, q_ref, gate_ref, kcn_ref, vcn_ref, ksn_ref, vsn_ref, kwn_ref, vwn_ref,
                       vpn_ref, skw_ref, svw_ref, spool_ref, wk_ref, wv_ref, imat_ref, emat_ref,
                       ckc_ref, cvc_ref, cks_ref, cvs_ref,
                       o_ref, pooled_ref, buf, buft, win_scr, tail_scr, vext_scr, sem,
                       *, sb, ns, past, n_pages, page, n_seq, ncv, ncp, nks, wls, nselp, n_sel):
    step = pl.program_id(0)
    nrow = ns * N_HEADS
    par = step % 2

    def copies(n, side, r):
        out = []
        for p in range(n_pages):
            pg = pt_ref[n * n_pages + p]
            for c, cref in enumerate((ckc_ref, cvc_ref, cks_ref, cvs_ref)):
                out.append(pltpu.make_async_copy(cref.at[pg], buft.at[side, r, c, :, pl.ds(p * page, page)],
                                                 sem.at[side, r]))
        return out

    @pl.when(step == 0)
    def _():
        buf[:, :, past:, :] = jnp.zeros((sb, 2, buf.shape[2] - past, KVW), F32)
        tail_scr[...] = jnp.zeros_like(tail_scr)
        vext_scr[...] = jnp.zeros_like(vext_scr)
        for r in range(sb):
            for cp in copies(r, 0, r):
                cp.start()

    @pl.when(step + 1 < pl.num_programs(0))
    def _():
        for r in range(sb):
            for cp in copies((step + 1) * sb + r, 1 - par, r):
                cp.start()

    for r in range(sb):
        for cp in copies(step * sb + r, par, r):
            cp.wait()

    def new_rows_t(ref, r4, r):
        tail_scr[r, 0:ns, :] = ref[pl.ds(r4, ns), :]
        return tail_scr[r].T

    row = lax.broadcasted_iota(I32, (nrow, 1), 0)
    hrow = row % N_HEADS
    qpos = past + row // N_HEADS
    slope = jnp.exp2(-8.0 * (hrow.astype(F32) + 1.0) / N_HEADS)
    kvrow = hrow // GROUP
    lane = lax.broadcasted_iota(I32, (1, LANES), 1)
    half = (lane // HEAD_DIM) == kvrow
    grow = (row // N_HEADS) * N_KV + kvrow
    row8 = lax.broadcasted_iota(I32, (ns * N_KV, 1), 0)
    qpos8 = past + lax.broadcasted_iota(I32, (ns * N_KV, 1), 0) // N_KV
    blk = lax.broadcasted_iota(I32, (1, nselp), 1)
    blk_f = blk.astype(F32)
    cur = qpos8 // SEL_BLOCK
    forced = (blk == 0) | (blk == cur) | (blk == cur - 1)
    visible = (blk * SEL_BLOCK <= qpos8)
    inrange = blk < n_sel
    cend = lax.broadcasted_iota(I32, (1, ncp), 1) * CMP_STRIDE + (CMP_BLOCK - 1)
    mask_c = qpos >= cend
    bias_c = slope * (cend - qpos).astype(F32)
    kpos = lax.broadcasted_iota(I32, (1, nks), 1)
    causal_s = qpos >= kpos
    bias_s = slope * (kpos - qpos).astype(F32)
    wbuf = wls[0]
    wpos = past - wbuf + lax.broadcasted_iota(I32, (1, wls[1]), 1)
    dw = qpos - wpos
    valid_w = lax.bitcast_convert_type(dw, jnp.uint32) < WINDOW
    bias_w = slope * (wpos - qpos).astype(F32)
    prow = lax.broadcasted_iota(I32, (vext_scr.shape[1], 1), 0)
    wcol = 2 << (lax.broadcasted_iota(I32, (1, POOL_W), 1) // POOL_GW)
    n_top = min(TOP_BLOCKS, n_sel)

    def softmax_rows(s, mask):
        s = jnp.where(mask, s, NEG)
        mx = jnp.max(s, axis=1, keepdims=True)
        e = jnp.where(mask, jnp.exp(s - mx), 0.0)
        l = jnp.sum(e, axis=1, keepdims=True)
        return e * jnp.where(l > 0.0, 1.0 / l, 0.0)

    def seq_body(r):
        r4 = r * ns
        for c, new_ref in enumerate((kcn_ref, vcn_ref)):
            for p in range(n_pages):
                buf[r, c, p * page:(p + 1) * page, :] = buft[par, r, c, :, p * page:(p + 1) * page].T
            buf[r, c, past:past + ns, :] = new_ref[pl.ds(r4, ns), :]
        for c, new_ref in ((2, ksn_ref), (3, vsn_ref)):
            buft[par, r, c, :, past:past + LANES] = new_rows_t(new_ref, r4, r)

        qall = q_ref[pl.ds(r * nrow, nrow), :]
        gates = gate_ref[pl.ds(r * nrow, nrow), :]

        cmp = []
        for c, w_ref in ((0, wk_ref), (1, wv_ref)):
            span = CMP_STRIDE * ncv
            lo = buf[r, c, 0:span, :].reshape(ncv, CMP_STRIDE, KVW) * w_ref[0:CMP_STRIDE, :][None]
            hi = (buf[r, c, CMP_STRIDE:CMP_STRIDE + span, :].reshape(ncv, CMP_STRIDE, KVW)
                  * w_ref[CMP_STRIDE:CMP_BLOCK, :][None])
            acc = jnp.sum(lo + hi, axis=1)
            cmp.append(jnp.concatenate([acc, jnp.zeros((ncp - ncv, KVW), F32)], axis=0).astype(BF16))
        p_c = softmax_rows(_dot_nt(qall, cmp[0]) + bias_c, mask_c)
        o_c = _dot(p_c.astype(BF16), cmp[1])

        psum = jnp.zeros((ns * N_KV, ncp), F32)
        for i in range(ns * N_KV):
            r0 = (i // N_KV) * N_HEADS + (i % N_KV) * GROUP
            psum = jnp.where(row8 == i, jnp.sum(p_c[r0:r0 + GROUP], axis=0, keepdims=True), psum)
        imp = _dot_exact(psum, imat_ref[...])
        vals = jnp.where(inrange, jnp.where(visible, jnp.where(forced, FORCE_SCORE, imp), NEG), -jnp.inf)
        sel8 = _topk_mask_by_rank(vals, blk, n_sel, n_top)
        sel_rows = jnp.zeros((nrow, nselp), F32)
        for i in range(ns * N_KV):
            sel_rows = jnp.where(grow == i, sel8[i:i + 1], sel_rows)
        chosen = _dot(sel_rows.astype(BF16), emat_ref[...])

        kst = buft[par, r, 2].astype(BF16)
        vst = buft[par, r, 3].astype(BF16)
        p_s = softmax_rows(_dot(qall, kst) + bias_s, causal_s & (chosen > 0.5))
        o_s = _dot_nt(p_s.astype(BF16), vst)

        outs_w = []
        for state_ref, new_ref in ((skw_ref, kwn_ref), (svw_ref, vwn_ref)):
            win_scr[r, :, 0:wbuf] = state_ref[r]
            win_scr[r, :, wbuf:wbuf + LANES] = new_rows_t(new_ref, r4, r)
            outs_w.append(win_scr[r].astype(BF16))
        p_w = softmax_rows(_dot(qall, outs_w[0]) + bias_w, valid_w)
        o_w = _dot_nt(p_w.astype(BF16), outs_w[1])

        o = gates[:, 0:1] * o_c + gates[:, 1:2] * o_s + gates[:, 2:3] * o_w
        o_ref[pl.ds(r * nrow, nrow), :] = jnp.where(half, o, 0.0).astype(BF16)

        vext_scr[r, 0:POOL_BUF, :] = spool_ref[r]
        vext_scr[r, POOL_BUF:POOL_BUF + ns, :] = vpn_ref[pl.ds(r4, ns), :]
        ext = vext_scr[r]
        for t in range(ns):
            hi = POOL_BUF + t
            inwin = (prow <= hi) & (prow > hi - wcol)
            ssum = jnp.sum(jnp.where(inwin, ext, 0.0), axis=0, keepdims=True)
            cnt = jnp.minimum(past + t + 1, wcol).astype(F32)
            pooled_ref[pl.ds(r4 + t, 1), :] = ssum / cnt - ext[hi:hi + 1, :]

    for r in range(sb):
        seq_body(r)


def _nsa_sample(page_table, q_rows, gate_rows, new6, vp_new, state_kwt, state_vwt, state_pool, wk, wv, caches):
    n_seq, n_pages = page_table.shape
    page = caches[0].shape[2]
    past = n_pages * page
    ns = vp_new.shape[0] // n_seq
    wbuf = state_kwt.shape[2]
    sb = 2
    nrow = ns * N_HEADS
    assert ns <= SEL_BLOCK and page == LANES
    t_pad = -(-(past + ns) // SEL_BLOCK) * SEL_BLOCK
    n_cmp = t_pad // CMP_STRIDE - 1
    ncv = -(-n_cmp // 8) * 8
    ncp = -(-ncv // LANES) * LANES
    nks = past + LANES
    n_sel = t_pad // SEL_BLOCK
    nselp = LANES
    assert n_sel <= nselp
    wlp = wbuf + LANES
    buf_rows = -(-(CMP_STRIDE * ncv + CMP_STRIDE) // 8) * 8
    imat = _importance_matrix(ncp, nselp)
    emat = (jnp.arange(nselp)[:, None] == (jnp.arange(nks)[None, :] // SEL_BLOCK)).astype(BF16)

    seqblk = lambda rows, w: pl.BlockSpec((sb * rows, w), lambda i, pt: (i, 0))
    const = lambda a: pl.BlockSpec(a.shape, lambda i, pt: (0,) * a.ndim)
    kern = functools.partial(
        _nsa_sample_kernel, sb=sb, ns=ns, past=past, n_pages=n_pages, page=page, n_seq=n_seq,
        ncv=ncv, ncp=ncp, nks=nks, wls=(wbuf, wlp), nselp=nselp, n_sel=n_sel)
    grid_spec = pltpu.PrefetchScalarGridSpec(
        num_scalar_prefetch=1,
        grid=(n_seq // sb,),
        in_specs=[seqblk(nrow, LANES), seqblk(nrow, LANES)] + [seqblk(ns, KVW)] * 6 + [seqblk(ns, POOL_W)]
        + [pl.BlockSpec((sb, KVW, wbuf), lambda i, pt: (i, 0, 0))] * 2
        + [pl.BlockSpec((sb, POOL_BUF, POOL_W), lambda i, pt: (i, 0, 0))]
        + [const(wk), const(wv), const(imat), const(emat)]
        + [pl.BlockSpec(memory_space=pl.ANY)] * 4,
        out_specs=[seqblk(nrow, LANES), seqblk(ns, POOL_W)],
        scratch_shapes=[pltpu.VMEM((sb, 2, buf_rows, KVW), F32),
                        pltpu.VMEM((2, sb, 4, KVW, nks), F32),
                        pltpu.VMEM((sb, KVW, wlp), F32),
                        pltpu.VMEM((sb, LANES, KVW), F32),
                        pltpu.VMEM((sb, 24, POOL_W), F32),
                        pltpu.SemaphoreType.DMA((2, sb))],
    )
    return pl.pallas_call(
        kern,
        out_shape=[jax.ShapeDtypeStruct((n_seq * nrow, LANES), BF16),
                   jax.ShapeDtypeStruct((n_seq * ns, POOL_W), F32)],
        grid_spec=grid_spec,
        compiler_params=_cparams(("arbitrary",)),
        name="nsa_sample",
    )(page_table.reshape(-1), q_rows, gate_rows, *new6, vp_new, state_kwt, state_vwt, state_pool, wk, wv,
      imat, emat, *caches)


def _route(logits_t, bias_col, tm):
    sc = jax.nn.sigmoid(logits_t)
    biased = sc + bias_col
    epg = EXPERTS_PER_GROUP
    row8 = lax.broadcasted_iota(I32, (epg, tm), 0).astype(F32)
    ninf = -jnp.inf
    grp = jnp.zeros((N_EGROUPS, tm), F32)
    for g in range(N_EGROUPS):
        bg = biased[g * epg:(g + 1) * epg]
        m1 = jnp.max(bg, axis=0, keepdims=True)
        first = jnp.min(jnp.where(bg == m1, row8, float(epg)), axis=0, keepdims=True)
        m2 = jnp.max(jnp.where(row8 == first, ninf, bg), axis=0, keepdims=True)
        grp = jnp.where(row8 == float(g), m1 + m2, grp)
    keep = jnp.zeros((N_EGROUPS, tm), F32)
    vals = grp
    for _ in range(TOPK_GROUPS):
        mx = jnp.max(vals, axis=0, keepdims=True)
        first = jnp.min(jnp.where(vals == mx, row8, float(N_EGROUPS)), axis=0, keepdims=True)
        hit = row8 == first
        keep = jnp.where(hit, 1.0, keep)
        vals = jnp.where(hit, ninf, vals)
    masked = jnp.concatenate(
        [jnp.where(keep[g:g + 1] > 0.5, biased[g * epg:(g + 1) * epg], NEG) for g in range(N_EGROUPS)], axis=0)
    rowe = lax.broadcasted_iota(I32, (N_EXPERTS, tm), 0).astype(F32)
    chosen = jnp.zeros((N_EXPERTS, tm), F32)
    vals = masked
    picks = []
    for _ in range(TOP_K):
        mx = jnp.max(vals, axis=0, keepdims=True)
        first = jnp.min(jnp.where(vals == mx, rowe, float(N_EXPERTS)), axis=0, keepdims=True)
        hit = rowe == first
        chosen = jnp.where(hit, sc, chosen)
        vals = jnp.where(hit, ninf, vals)
        picks.append((hit, first))
    return ROUTED_SCALE * chosen / jnp.sum(chosen, axis=0, keepdims=True), picks


def _pack_bf16_pairs(x):
    c = x.shape[1] // 2
    bits = lambda v: lax.bitcast_convert_type(v.astype(BF16).astype(F32), jnp.uint32)
    return (bits(x[:, :c]) >> 16) | (bits(x[:, c:]) & jnp.uint32(0xFFFF0000))


def _unpack_bf16_pairs(w):
    lo = lax.bitcast_convert_type(w << 16, F32)
    hi = lax.bitcast_convert_type(w & jnp.uint32(0xFFFF0000), F32)
    return jnp.concatenate([lo, hi], axis=1)


def _finish_kernel(x_ref, pooled_ref, y_ref, gm_ref, g1_ref, shift_ref, scale_ref,
                   wlin_ref, pscale_ref, wpo_ref, wno_ref, wo_ref, n2_ref, wr_ref, br_ref,
                   *rest, tm, d, sparse):
    if sparse:
        tri_ref, x1_ref, u2_ref, up_ref, eid_ref, gk_ref, rank_ref, cnt_ref, carry_scr = rest
    else:
        x1_ref, u2_ref, gates_ref = rest
    x = x_ref[...].reshape(tm, d)
    pooled = pooled_ref[...].reshape(tm, POOL_W).astype(BF16)
    y_pool = _dot(pooled, wlin_ref[...]) * pscale_ref[...]
    a = _dot(y_pool.astype(BF16), wpo_ref[...])
    b = _dot(y_ref[...].reshape(tm, QPAD), wno_ref[...])
    gm = gm_ref[...].reshape(tm, 2 * d)
    merged = gm[:, :d] * a + gm[:, d:] * b
    x1 = x + _rows2d(g1_ref) * _dot(merged.astype(BF16), wo_ref[...])
    x1_ref[...] = x1.reshape(x1_ref.shape)
    u2 = _rmsnorm(x1, n2_ref[...]) * (1.0 + _rows2d(scale_ref)) + _rows2d(shift_ref)
    u2b = u2.astype(BF16)
    u2_ref[...] = u2b.reshape(u2_ref.shape)
    logits_t = _dot_nt(wr_ref[...], u2b)
    gates_t, picks = _route(logits_t[:N_EXPERTS], br_ref[...], tm)
    if not sparse:
        gates_t = jnp.concatenate([gates_t, jnp.zeros((LANES - N_EXPERTS, tm), F32)], axis=0)
        gates_ref[...] = gates_t.T.reshape(gates_ref.shape)
        return

    @pl.when((pl.program_id(0) == 0) & (pl.program_id(1) == 0))
    def _():
        carry_scr[...] = jnp.zeros_like(carry_scr)

    packed = _pack_bf16_pairs(u2)
    for s in range(up_ref.shape[0]):
        up_ref[s] = packed[:, s * SC_ROW_WORDS:(s + 1) * SC_ROW_WORDS]
    hit_all = picks[0][0]
    for hit, _ in picks[1:]:
        hit_all = hit_all | hit
    hits = jnp.where(hit_all, 1.0, 0.0).astype(BF16)
    before = _dot(hits, tri_ref[...]) + jnp.concatenate([carry_scr[...]] * (tm // LANES), axis=1)
    eids, gks, ranks = [], [], []
    for hit, first in picks:
        eids.append(first)
        gks.append(jnp.sum(jnp.where(hit, gates_t, 0.0), axis=0, keepdims=True))
        ranks.append(jnp.sum(jnp.where(hit, before, 0.0), axis=0, keepdims=True))
    pick_row = lax.broadcasted_iota(I32, (TOP_K, tm), 0)

    def stack(rows):
        out = jnp.zeros((TOP_K, tm), F32)
        for r, v in enumerate(rows):
            out = jnp.where(pick_row == r, v, out)
        return out

    eid_ref[...] = stack(eids).astype(I32).reshape(eid_ref.shape)
    gk_ref[...] = stack(gks).reshape(gk_ref.shape)
    rank_ref[...] = stack(ranks).astype(I32).reshape(rank_ref.shape)
    carry_scr[...] += _dot(hits, jnp.ones((tm, LANES), BF16))
    cnt_ref[...] = carry_scr[...]


def _finish(x3, pooled, ynsa, gm, mods, wts, *, tm, sparse):
    g, r, d = x3.shape
    nt = r // tm
    g1, shift2, scale2 = mods
    per_row = g1.ndim == 2
    tok = lambda w: pl.BlockSpec((1, tm, w), lambda b, j: (b, j, 0))
    tok_t = lambda rows: pl.BlockSpec((1, rows, tm), lambda b, j: (b, 0, j))
    if per_row:
        mod_spec = lambda col: pl.BlockSpec((tm, d), lambda b, j, col=col: (b * nt + j, col))
    else:
        mod_spec = lambda col: pl.BlockSpec((1, 1, d), lambda b, j, col=col: (b, 0, col))
    const = lambda a: pl.BlockSpec(a.shape, lambda b, j: (0,) * a.ndim)
    out_shape = [jax.ShapeDtypeStruct((g, r, d), F32), jax.ShapeDtypeStruct((g, r, d), BF16)]
    out_specs = [tok(d), tok(d)]
    scratch = []
    if sparse:
        tri = (jnp.arange(tm)[:, None] < jnp.arange(tm)[None, :]).astype(BF16)
        wts = tuple(wts) + (tri,)
        split = d // 2 // SC_ROW_WORDS
        out_shape += [jax.ShapeDtypeStruct((split, g * r, SC_ROW_WORDS), jnp.uint32),
                      jax.ShapeDtypeStruct((g, TOP_K, r), I32), jax.ShapeDtypeStruct((g, TOP_K, r), F32),
                      jax.ShapeDtypeStruct((g, TOP_K, r), I32), jax.ShapeDtypeStruct((N_EXPERTS, LANES), F32)]
        out_specs += [pl.BlockSpec((split, tm, SC_ROW_WORDS), lambda b, j: (0, b * nt + j, 0)),
                      tok_t(TOP_K), tok_t(TOP_K), tok_t(TOP_K),
                      pl.BlockSpec((N_EXPERTS, LANES), lambda b, j: (0, 0))]
        scratch.append(pltpu.VMEM((N_EXPERTS, LANES), F32))
    else:
        out_shape.append(jax.ShapeDtypeStruct((g, r, LANES), F32))
        out_specs.append(tok(LANES))
    return pl.pallas_call(
        functools.partial(_finish_kernel, tm=tm, d=d, sparse=sparse),
        out_shape=out_shape,
        grid=(g, nt),
        in_specs=[tok(d), tok(POOL_W), tok(QPAD), tok(2 * d), mod_spec(2), mod_spec(3), mod_spec(4)]
        + [const(w) for w in wts],
        out_specs=out_specs,
        scratch_shapes=scratch,
        compiler_params=_cparams(("arbitrary", "arbitrary")),
        name="finish_route" if sparse else "finish",
    )(x3, pooled, ynsa, gm, g1, shift2, scale2, *wts)


def _moe_kernel(u_ref, gates_ref, x1_ref, g2_ref, nf_ref, wg_ref, wu_ref, wd_ref, sg_ref, su_ref, sd_ref,
                y_ref, acc_ref, *, tm, d, eps):
    e = pl.program_id(2)
    u = u_ref[...].reshape(tm, d)

    @pl.when(e == 0)
    def _():
        hs = _silu(_dot(u, sg_ref[...])) * _dot(u, su_ref[...])
        acc_ref[...] = _dot(hs.astype(BF16), sd_ref[...])

    gates = gates_ref[...].reshape(tm, LANES)
    lane = lax.broadcasted_iota(I32, (1, LANES), 1)
    hidden = []
    for j in range(eps):
        h = _silu(_dot(u, wg_ref[j].astype(BF16))) * _dot(u, wu_ref[j].astype(BF16))
        gate = jnp.sum(jnp.where(lane == e * eps + j, gates, 0.0), axis=1, keepdims=True)
        hidden.append((h * gate).astype(BF16))
    f = wd_ref.shape[1]
    acc_ref[...] += _dot(jnp.concatenate(hidden, axis=1), wd_ref[...].reshape(eps * f, d).astype(BF16))

    @pl.when(e == pl.num_programs(2) - 1)
    def _():
        x2 = x1_ref[...].reshape(tm, d) + _rows2d(g2_ref) * acc_ref[...]
        y_ref[...] = _rmsnorm(x2, nf_ref[...]).reshape(y_ref.shape)


def _moe(u2, gates, x1, g2, normf, w_gate, w_up, w_down, sg, su, sd, *, tm):
    g, r, d = x1.shape
    nt = r // tm
    ne, _, f = w_gate.shape
    per_row = g2.ndim == 2
    tok = lambda w: pl.BlockSpec((1, tm, w), lambda b, j, e: (b, j, 0))
    if per_row:
        g2_spec = pl.BlockSpec((tm, d), lambda b, j, e: (b * nt + j, 5))
    else:
        g2_spec = pl.BlockSpec((1, 1, d), lambda b, j, e: (b, 0, 5))
    once = pl.Buffered(buffer_count=1)
    const = lambda a: pl.BlockSpec(a.shape, lambda b, j, e: (0,) * a.ndim, pipeline_mode=once)
    eps = 4
    return pl.pallas_call(
        functools.partial(_moe_kernel, tm=tm, d=d, eps=eps),
        out_shape=jax.ShapeDtypeStruct((g, r, d), F32),
        grid=(g, nt, ne // eps),
        in_specs=[tok(d), tok(LANES),
                  pl.BlockSpec((1, tm, d), lambda b, j, e: (b, j, 0), pipeline_mode=once),
                  g2_spec, const(normf),
                  pl.BlockSpec((eps, d, f), lambda b, j, e: (e, 0, 0)),
                  pl.BlockSpec((eps, d, f), lambda b, j, e: (e, 0, 0)),
                  pl.BlockSpec((eps, f, d), lambda b, j, e: (e, 0, 0)),
                  const(sg), const(su), const(sd)],
        out_specs=tok(d),
        scratch_shapes=[pltpu.VMEM((tm, d), F32)],
        compiler_params=_cparams(("arbitrary", "arbitrary", "arbitrary")),
        name="moe",
    )(u2, gates, x1, g2, normf, w_gate, w_up, w_down, sg, su, sd)


SC_WINDOW = 128
SC_ROW_WORDS = 256
MOE_ROWS = 512


def _sc_mesh():
    return plsc.VectorSubcoreMesh(core_axis_name="c", subcore_axis_name="s")


def _sc_scatter_rows(src, dst_idx, n_dst):
    n, w = src.shape
    nk = dst_idx.shape[0]

    @pl.kernel(out_type=jax.ShapeDtypeStruct((n_dst, w), src.dtype), mesh=_sc_mesh(), scratch_types=[])
    def scatter(src_hbm, idx_hbm, dst_hbm):
        def body(rows_vmem, idx_vmem):
            pltpu.sync_copy(rows_vmem, dst_hbm.at[idx_vmem.at[0]])

        pltpu.emit_pipeline(
            body,
            grid=(nk, n // SC_WINDOW),
            in_specs=[pl.BlockSpec((SC_WINDOW, w), index_map=lambda k, i: (i, 0)),
                      pl.BlockSpec((1, SC_WINDOW), index_map=lambda k, i: (k, i))],
            out_specs=[],
            core_axis_name=("c", "s"),
            dimension_semantics=(pltpu.PARALLEL, pltpu.PARALLEL),
        )(src_hbm, idx_hbm)

    return scatter(src, dst_idx)


def _sc_gather_rows(src, idx):
    n, w = idx.shape[0], src.shape[1]

    @pl.kernel(out_type=jax.ShapeDtypeStruct((n, w), src.dtype), mesh=_sc_mesh(), scratch_types=[])
    def gather(src_hbm, idx_hbm, out_hbm):
        def body(idx_vmem, out_vmem):
            pltpu.sync_copy(src_hbm.at[idx_vmem.at[0]], out_vmem)

        pltpu.emit_pipeline(
            body,
            grid=(n // SC_WINDOW,),
            in_specs=[pl.BlockSpec((1, SC_WINDOW), index_map=lambda i: (0, i))],
            out_specs=[pl.BlockSpec((SC_WINDOW, w), index_map=lambda i: (i, 0))],
            core_axis_name=("c", "s"),
            dimension_semantics=(pltpu.PARALLEL,),
        )(idx_hbm, out_hbm)

    return gather(src, idx.reshape(1, n))


MOE_BLOCKS_PER_STEP = 4


def _expert_rows_kernel(te_ref, nt_ref, x_ref, *refs):
    y_ref = refs[-1]
    split = x_ref.shape[0]
    for j in range(MOE_BLOCKS_PER_STEP):
        wg_ref, wu_ref, wd_ref = refs[3 * j:3 * j + 3]
        rows = slice(j * MOE_ROWS, (j + 1) * MOE_ROWS)

        @pl.when(pl.program_id(0) * MOE_BLOCKS_PER_STEP + j < nt_ref[0])
        def _():
            x = _unpack_bf16_pairs(jnp.concatenate([x_ref[s, rows] for s in range(split)], axis=1)).astype(BF16)
            h = _silu(_dot(x, wg_ref[...].astype(BF16))) * _dot(x, wu_ref[...].astype(BF16))
            y = _pack_bf16_pairs(_dot(h.astype(BF16), wd_ref[...].astype(BF16)))
            for s in range(split):
                y_ref[s, rows] = y[:, s * SC_ROW_WORDS:(s + 1) * SC_ROW_WORDS]


def _expert_rows(tile_expert, n_tiles, x_sorted, w_gate, w_up, w_down):
    split, p, words = x_sorted.shape
    ne, d, f = w_gate.shape
    bps = MOE_BLOCKS_PER_STEP
    wspec = lambda a, b, j: pl.BlockSpec((None, a, b), lambda i, te, nt: (te[i * bps + j], 0, 0))
    rows = pl.BlockSpec((split, bps * MOE_ROWS, words), lambda i, te, nt: (0, i, 0))
    weights, wspecs = [], []
    for j in range(bps):
        weights += [w_gate, w_up, w_down]
        wspecs += [wspec(d, f, j), wspec(d, f, j), wspec(f, d, j)]
    grid_spec = pltpu.PrefetchScalarGridSpec(
        num_scalar_prefetch=2,
        grid=(p // (bps * MOE_ROWS),),
        in_specs=[rows] + wspecs,
        out_specs=rows,
    )
    return pl.pallas_call(
        _expert_rows_kernel,
        out_shape=jax.ShapeDtypeStruct((split, p, words), jnp.uint32),
        grid_spec=grid_spec,
        compiler_params=_cparams(("arbitrary",)),
        name="expert_rows",
    )(tile_expert, n_tiles, x_sorted, *weights)


def _combine_kernel(yg_ref, gk_ref, u_ref, x1_ref, g2_ref, nf_ref, sg_ref, su_ref, sd_ref, y_ref, *, tm, d):
    u = u_ref[...].reshape(tm, d)
    hs = _silu(_dot(u, sg_ref[...])) * _dot(u, su_ref[...])
    acc = _dot(hs.astype(BF16), sd_ref[...])
    gk = gk_ref[...].reshape(tm, LANES)
    lane = lax.broadcasted_iota(I32, (1, LANES), 1)
    split = yg_ref.shape[0]
    for k in range(TOP_K):
        gate = jnp.sum(jnp.where(lane == k, gk, 0.0), axis=1, keepdims=True)
        words = jnp.concatenate([yg_ref[s, k] for s in range(split)], axis=1)
        acc = acc + gate * _unpack_bf16_pairs(words)
    x2 = x1_ref[...].reshape(tm, d) + _rows2d(g2_ref) * acc
    y_ref[...] = _rmsnorm(x2, nf_ref[...]).reshape(y_ref.shape)


def _combine(yg, gk, u2, x1, g2, normf, sg, su, sd, *, tm):
    g, r, d = x1.shape
    nt = r // tm
    split, _, _, words = yg.shape
    tok = lambda w: pl.BlockSpec((1, tm, w), lambda b, j: (b, j, 0))
    const = lambda a: pl.BlockSpec(a.shape, lambda b, j: (0,) * a.ndim)
    return pl.pallas_call(
        functools.partial(_combine_kernel, tm=tm, d=d),
        out_shape=jax.ShapeDtypeStruct((g, r, d), F32),
        grid=(g, nt),
        in_specs=[pl.BlockSpec((split, TOP_K, tm, words), lambda b, j: (0, 0, b * nt + j, 0)),
                  tok(LANES), tok(d), tok(d),
                  pl.BlockSpec((1, 1, d), lambda b, j: (b, 0, 5)), const(normf), const(sg), const(su), const(sd)],
        out_specs=tok(d),
        compiler_params=_cparams(("arbitrary", "arbitrary")),
        name="moe_combine",
    )(yg, gk, u2, x1, g2, normf, sg, su, sd)


def _moe_sorted_experts(u2p, eid_t, rank_t, counts, w_gate, w_up, w_down):
    split, n, _ = u2p.shape
    ne = w_gate.shape[0]
    cnt = counts[:, 0].astype(I32)
    padded = -(-cnt // MOE_ROWS) * MOE_ROWS
    seg_end = jnp.cumsum(padded)
    seg_start = seg_end - padded
    p_rows = n * TOP_K + ne * MOE_ROWS
    eid = eid_t.transpose(1, 0, 2).reshape(TOP_K, n)
    start = jnp.sum(jnp.where(eid[:, :, None] == jnp.arange(ne, dtype=I32), seg_start, 0), axis=-1)
    pos = start + rank_t.transpose(1, 0, 2).reshape(TOP_K, n)
    first_row = jnp.arange(p_rows // MOE_ROWS, dtype=I32) * MOE_ROWS
    tile_expert = jnp.minimum(jnp.sum(seg_end[None, :] <= first_row[:, None], axis=1), ne - 1).astype(I32)
    n_tiles = (seg_end[-1:] // MOE_ROWS).astype(I32)
    scat_idx = jnp.concatenate([pos + s * p_rows for s in range(split)], axis=1)
    gath_idx = jnp.concatenate([pos.reshape(-1) + s * p_rows for s in range(split)])
    x_sorted = _sc_scatter_rows(u2p.reshape(split * n, SC_ROW_WORDS), scat_idx, split * p_rows)
    y_sorted = _expert_rows(tile_expert, n_tiles, x_sorted.reshape(split, p_rows, SC_ROW_WORDS),
                            w_gate, w_up, w_down)
    return y_sorted.reshape(split * p_rows, SC_ROW_WORDS), gath_idx


def _moe_sorted_combine(y_sorted, gath_idx, gk_t, u2, x1, g2, normf, sg, su, sd):
    g, r, _ = x1.shape
    split = gath_idx.shape[0] // (TOP_K * g * r)
    yg = _sc_gather_rows(y_sorted, gath_idx)
    gk = jnp.pad(gk_t.transpose(0, 2, 1), ((0, 0), (0, 0), (0, LANES - TOP_K)))
    return _combine(yg.reshape(split, TOP_K, g * r, SC_ROW_WORDS), gk, u2, x1, g2, normf, sg, su, sd, tm=512)


def _kv_slot_mask():
    return (jnp.arange(N_HEADS)[:, None] // GROUP == jnp.arange(N_KV)[None, :]).astype(F32)


def _prep_w_in(w_in, d):
    q0 = POOL_W
    kv0 = q0 + N_HEADS * HEAD_DIM
    gn0 = kv0 + 6 * KVW
    gm0 = gn0 + 3 * N_HEADS
    wq = w_in[:, q0:kv0].reshape(d, N_HEADS, 1, HEAD_DIM) * (HEAD_DIM ** -0.5)
    wq = (wq * _kv_slot_mask()[None, :, :, None]).reshape(d, QPAD)
    wgn = jnp.pad(w_in[:, gn0:gm0], ((0, 0), (0, LANES - 3 * N_HEADS)))
    return jnp.concatenate([w_in[:, :q0], wq, w_in[:, kv0:gn0], wgn, w_in[:, gm0:]], axis=1).astype(BF16)


def _prep_w_nsa_out(w, d):
    w = w.reshape(N_HEADS, 1, HEAD_DIM, d) * _kv_slot_mask()[:, :, None, None]
    return w.reshape(QPAD, d).astype(BF16)


def _block_diag(w_lin):
    g, c, _ = w_lin.shape
    eye = jnp.eye(g, dtype=F32)
    return (w_lin[:, :, None, :] * eye[:, None, :, None]).reshape(g * c, g * c).astype(BF16)


def kernel(x_prompt, x_sample, cache_kc, cache_vc, cache_ks, cache_vs, state_kw, state_vw, state_pool,
           page_table, c_prompt, c_sample, norm1_g, norm2_g, normf_g, w_ada, b_ada, w_in, w_pool_lin,
           pool_scale, w_cmp_k, w_cmp_v, w_pool_out, w_nsa_out, w_o, w_router, b_router, w_gate, w_up,
           w_down, ws_gate, ws_up, ws_down):
    depth = w_in.shape[0]
    assert depth == 1, "single-layer stack"
    bsz, seq, d = x_prompt.shape
    n_seq, ns, _ = x_sample.shape
    wbuf = state_kw.shape[2]
    lyr = 0

    c_all = jnp.concatenate([c_prompt, c_sample], axis=0)
    rows = c_all.shape[0]
    rows_p = -(-rows // 8) * 8
    mod = _adaln(jnp.pad(c_all, ((0, rows_p - rows), (0, 0))), w_ada[lyr], b_ada[lyr])
    mod_p = mod[:bsz].reshape(bsz, 1, 6 * d)
    mod_s = jnp.repeat(mod[bsz:bsz + n_seq], ns, axis=0)

    w2 = _prep_w_in(w_in[lyr], d)
    g1n = norm1_g[lyr].reshape(1, d)
    wk = w_cmp_k[lyr].reshape(CMP_BLOCK, KVW)
    wv = w_cmp_v[lyr].reshape(CMP_BLOCK, KVW)
    fin_w = (_block_diag(w_pool_lin[lyr]), pool_scale[lyr].reshape(1, POOL_W), w_pool_out[lyr].astype(BF16),
             _prep_w_nsa_out(w_nsa_out[lyr], d), w_o[lyr].astype(BF16), norm2_g[lyr].reshape(1, d),
             jnp.pad(w_router[lyr].T, ((0, LANES - N_EXPERTS), (0, 0))).astype(BF16),
             b_router[lyr].reshape(N_EXPERTS, 1))
    moe_w = (w_gate[lyr], w_up[lyr], w_down[lyr], ws_gate[lyr].astype(BF16), ws_up[lyr].astype(BF16),
             ws_down[lyr].astype(BF16))
    nf = normf_g.reshape(1, d)

    tm_p = 512
    (vp, kc, vc, ks, vs, kw, vw, gm, ksb, kwb, vst, vwt, qt, gst, pooled) = _in_proj(
        x_prompt, mod_p, mod_p, g1n, w2, tm=tm_p, prompt=True)
    kcmp, vcmpt = _compress(kc, vc, wk, wv)
    ynsa = _nsa_prompt(qt, gst, kcmp, vcmpt, ksb, vst, kwb, vwt)
    x1, u2, u2p, eid_t, gk_t, rank_t, counts = _finish(
        x_prompt, pooled, ynsa, gm, (mod_p, mod_p, mod_p), fin_w, tm=tm_p, sparse=True)

    n_tok = n_seq * ns
    xs3 = x_sample.reshape(1, n_tok, d)
    tm_s = 128
    (vp_s, kc_s, vc_s, ks_s, vs_s, kw_s, vw_s, gm_s, q_s, gs_s) = _in_proj(
        xs3, mod_s, mod_s, g1n, w2, tm=tm_s, prompt=False)
    two = lambda a: a.reshape(n_tok, a.shape[-1])
    q_rows = q_s.reshape(n_tok * N_HEADS, LANES)
    gate_rows = two(gs_s)[:, :3 * N_HEADS].reshape(n_tok, 3, N_HEADS).transpose(0, 2, 1)
    gate_rows = jnp.pad(gate_rows.reshape(n_tok * N_HEADS, 3), ((0, 0), (0, LANES - 3)))
    n_pool = cache_kc.shape[1]
    page = cache_kc.shape[2]
    rows_minor = lambda a: jnp.transpose(a, (0, 2, 3, 1)).reshape(a.shape[0], KVW, a.shape[1])
    caches = [rows_minor(c[lyr]) for c in (cache_kc, cache_vc, cache_ks, cache_vs)]
    o_rows, pooled_s = _nsa_sample(
        page_table, q_rows, gate_rows, [two(a) for a in (kc_s, vc_s, ks_s, vs_s, kw_s, vw_s)], two(vp_s),
        rows_minor(state_kw[lyr]), rows_minor(state_vw[lyr]), state_pool[lyr], wk, wv, caches)
    ynsa_s = o_rows.reshape(1, n_tok, QPAD)
    y_sorted, gath_idx = _moe_sorted_experts(u2p, eid_t, rank_t, counts, *moe_w[:3])
    ynsa_s, y_sorted = lax.optimization_barrier((ynsa_s, y_sorted))
    y_prompt = _moe_sorted_combine(y_sorted, gath_idx, gk_t, u2, x1, mod_p, nf, *moe_w[3:])
    x1_s, u2_s, gates_s = _finish(xs3, pooled_s.reshape(1, n_tok, POOL_W), ynsa_s, gm_s,
                                  (mod_s, mod_s, mod_s), fin_w, tm=tm_s, sparse=False)
    y_sample = _moe(u2_s, gates_s, x1_s, mod_s, nf, *moe_w, tm=n_tok).reshape(n_seq, ns, d)

    kvp = lambda a: a.reshape(1, bsz, seq, N_KV, HEAD_DIM)
    tailp = lambda a: jnp.pad(a, ((0, 0), (wbuf, 0), (0, 0)))[:, -wbuf:].reshape(1, bsz, wbuf, N_KV, HEAD_DIM)
    kvs = lambda a: a.reshape(1, n_seq, ns, N_KV, HEAD_DIM)
    wins = lambda st, new: jnp.concatenate(
        [st[lyr], new.reshape(n_seq, ns, N_KV, HEAD_DIM)], axis=1)[None, :, -wbuf:]
    pool_p = vp[:, -POOL_BUF:][None]
    pool_s = jnp.concatenate([state_pool[lyr], vp_s.reshape(n_seq, ns, POOL_W)], axis=1)[None, :, -POOL_BUF:]
    return (y_prompt, y_sample, kvp(kc), kvp(vc), kvp(ks), kvp(vs), tailp(kw), tailp(vw), pool_p,
            kvs(kc_s), kvs(vc_s), kvs(ks_s), kvs(vs_s), wins(state_kw, kw_s), wins(state_vw, vw_s), pool_s)
```

```python
import functools

import jax
import jax.numpy as jnp
from jax import lax
from jax.experimental import pallas as pl
from jax.experimental.pallas import tpu as pltpu
from jax.experimental.pallas import tpu_sc as plsc

F32 = jnp.float32
BF16 = jnp.bfloat16
I32 = jnp.int32

POOL_WINDOWS = (2, 4, 8, 16)
POOL_GW = 64
POOL_W = 256
POOL_BUF = 15
N_HEADS = 8
HEAD_DIM = 64
N_KV = 2
GROUP = N_HEADS // N_KV
CMP_STRIDE = 16
CMP_BLOCK = 32
SEL_BLOCK = 64
TOP_BLOCKS = 16
WINDOW = 512
Q_BLOCK = 128
FORCE_SCORE = 1e4
N_EXPERTS = 64
N_EGROUPS = 8
EXPERTS_PER_GROUP = N_EXPERTS // N_EGROUPS
TOPK_GROUPS = 4
TOP_K = 8
ROUTED_SCALE = 2.5
EPS = 1e-6
NEG = -1e30
SLOPES = tuple(2.0 ** (-8.0 * (h + 1.0) / N_HEADS) for h in range(N_HEADS))

LANES = 128
QPAD = N_HEADS * LANES
KVW = N_KV * HEAD_DIM
VMEM_LIMIT = 56 * 1024 * 1024


def _cparams(sem):
    return pltpu.CompilerParams(dimension_semantics=sem, vmem_limit_bytes=VMEM_LIMIT)


def _dot(a, b):
    return jnp.dot(a, b, preferred_element_type=F32)


def _dot_nt(a, b):
    return lax.dot_general(a, b, (((1,), (1,)), ((), ())), preferred_element_type=F32)


def _dot_exact(a, b):
    return jnp.dot(a, b, preferred_element_type=F32, precision=lax.Precision.HIGHEST)


def _rows2d(ref):
    v = ref[...]
    return v.reshape(v.shape[-2], v.shape[-1])


def _rmsnorm(x, g):
    return x * lax.rsqrt(jnp.mean(x * x, axis=-1, keepdims=True) + EPS) * g


def _silu(x):
    return x * jax.nn.sigmoid(x)


def _adaln_kernel(c_ref, w_ref, b_ref, o_ref):
    s = _silu(c_ref[...]).astype(BF16)
    o_ref[...] = _dot(s, w_ref[...].astype(BF16)) + b_ref[...]


def _adaln(c, w_ada, b_ada):
    rows, d = c.shape
    n = w_ada.shape[1]
    tn = 512
    return pl.pallas_call(
        _adaln_kernel,
        out_shape=jax.ShapeDtypeStruct((rows, n), F32),
        grid=(n // tn,),
        in_specs=[pl.BlockSpec((rows, d), lambda j: (0, 0)),
                  pl.BlockSpec((d, tn), lambda j: (0, j)),
                  pl.BlockSpec((1, tn), lambda j: (0, j))],
        out_specs=pl.BlockSpec((rows, tn), lambda j: (0, j)),
        compiler_params=_cparams(("arbitrary",)),
        name="adaln",
    )(c, w_ada, b_ada.reshape(1, n))


_C_VP = 0
_C_Q = _C_VP + POOL_W
_C_KV = _C_Q + QPAD
_C_GN = _C_KV + 6 * KVW
_C_GM = _C_GN + LANES


def _pool_window_sums(ext, tm):
    s2 = ext + pltpu.roll(ext, 1, 0)
    s4 = s2 + pltpu.roll(s2, 2, 0)
    s8 = s4 + pltpu.roll(s4, 4, 0)
    s16 = s8 + pltpu.roll(s8, 8, 0)
    grp = lax.broadcasted_iota(I32, (1, POOL_W), 1) // POOL_GW
    pick = jnp.where(grp == 0, s2, jnp.where(grp == 1, s4, jnp.where(grp == 2, s8, s16)))
    return pick[16:16 + tm]


def _in_proj_kernel(x_ref, shift_ref, scale_ref, g_ref, w_ref,
                    vp_ref, kc_ref, vc_ref, ks_ref, vs_ref, kw_ref, vw_ref, gm_ref, *rest, tm, d, prompt):
    x = x_ref[...].reshape(tm, d)
    u = _rmsnorm(x, g_ref[...]) * (1.0 + _rows2d(scale_ref)) + _rows2d(shift_ref)
    ub = u.astype(BF16)

    head = _dot(ub, w_ref[:, 0:_C_GM])

    def proj(c0, n):
        return head[:, c0:c0 + n] if c0 + n <= _C_GM else _dot(ub, w_ref[:, c0:c0 + n])

    vp = proj(_C_VP, POOL_W)
    vp_ref[...] = vp.reshape(vp_ref.shape)
    kv = []
    for n, o32 in enumerate((kc_ref, vc_ref, ks_ref, vs_ref, kw_ref, vw_ref)):
        v = proj(_C_KV + n * KVW, KVW)
        o32[...] = v.reshape(o32.shape)
        kv.append(v)
    gm_ref[...] = jax.nn.sigmoid(proj(_C_GM, 2 * d)).reshape(gm_ref.shape)
    gs = jax.nn.sigmoid(proj(_C_GN, LANES))

    if not prompt:
        q_ref, gs_ref = rest
        q_ref[...] = proj(_C_Q, QPAD).astype(BF16).reshape(q_ref.shape)
        gs_ref[...] = gs.reshape(gs_ref.shape)
    else:
        ksb_ref, kwb_ref, vst_ref, vwt_ref, qt_ref, gst_ref, pooled_ref, halo_ref = rest
        ksb_ref[...] = kv[2].astype(BF16).reshape(ksb_ref.shape)
        kwb_ref[...] = kv[4].astype(BF16).reshape(kwb_ref.shape)
        vst_ref[...] = kv[3].T.astype(BF16).reshape(vst_ref.shape)
        vwt_ref[...] = kv[5].T.astype(BF16).reshape(vwt_ref.shape)
        gst_ref[...] = gs.T.reshape(gst_ref.shape)
        for h in range(N_HEADS):
            qt_ref[0, h] = proj(_C_Q + h * LANES, LANES).T.astype(BF16)
        j = pl.program_id(1)

        @pl.when(j == 0)
        def _():
            halo_ref[...] = jnp.zeros_like(halo_ref)

        ext = jnp.concatenate([halo_ref[...], vp], axis=0)
        sums = _pool_window_sums(ext, tm)
        pos = j * tm + lax.broadcasted_iota(I32, (tm, 1), 0)
        wcol = 2 << (lax.broadcasted_iota(I32, (1, POOL_W), 1) // POOL_GW)
        cnt = jnp.minimum(pos + 1, wcol).astype(F32)
        pooled_ref[...] = (sums / cnt - vp).astype(BF16).reshape(pooled_ref.shape)
        halo_ref[...] = vp[tm - 16:tm]


def _in_proj(x3, shift, scale, g1, w2, *, tm, prompt):
    g, r, d = x3.shape
    nt = r // tm
    per_row = shift.ndim == 2

    def tok(width, dtype):
        return (jax.ShapeDtypeStruct((g, r, width), dtype),
                pl.BlockSpec((1, tm, width), lambda b, j: (b, j, 0)))

    def tok_t(rows, dtype):
        return (jax.ShapeDtypeStruct((g, rows, r), dtype),
                pl.BlockSpec((1, rows, tm), lambda b, j: (b, 0, j)))

    outs = [tok(POOL_W, F32)] + [tok(KVW, F32)] * 6 + [tok(2 * d, F32)]
    scratch = []
    if prompt:
        outs += [tok(KVW, BF16), tok(KVW, BF16), tok_t(KVW, BF16), tok_t(KVW, BF16)]
        outs.append((jax.ShapeDtypeStruct((g, N_HEADS, LANES, r), BF16),
                     pl.BlockSpec((1, N_HEADS, LANES, tm), lambda b, j: (b, 0, 0, j))))
        outs += [tok_t(LANES, F32), tok(POOL_W, BF16)]
        scratch.append(pltpu.VMEM((16, POOL_W), F32))
    else:
        outs += [tok(QPAD, BF16), tok(LANES, F32)]
    if per_row:
        mod_spec = lambda col: pl.BlockSpec((tm, d), lambda b, j, col=col: (b * nt + j, col))
    else:
        mod_spec = lambda col: pl.BlockSpec((1, 1, d), lambda b, j, col=col: (b, 0, col))
    kern = functools.partial(_in_proj_kernel, tm=tm, d=d, prompt=prompt)
    return pl.pallas_call(
        kern,
        out_shape=[o[0] for o in outs],
        grid=(g, nt),
        in_specs=[pl.BlockSpec((1, tm, d), lambda b, j: (b, j, 0)),
                  mod_spec(0), mod_spec(1),
                  pl.BlockSpec((1, d), lambda b, j: (0, 0)),
                  pl.BlockSpec(w2.shape, lambda b, j: (0, 0))],
        out_specs=[o[1] for o in outs],
        scratch_shapes=scratch,
        compiler_params=_cparams(("arbitrary", "arbitrary")),
        name="in_proj_prompt" if prompt else "in_proj_sample",
    )(x3, shift, scale, g1, w2)


def _compress_kernel(kc_ref, vc_ref, wk_ref, wv_ref, okc_ref, ovc_ref, sh_ref, *, nc):
    last = lax.broadcasted_iota(I32, (nc, 1), 0) == nc - 1
    for src, w_ref, dst in ((kc_ref, wk_ref, okc_ref), (vc_ref, wv_ref, ovc_ref)):
        head = jnp.zeros((nc, KVW), F32)
        tail = jnp.zeros((nc, KVW), F32)
        for r in range(CMP_STRIDE):
            rows = src[pl.ds(r, nc, stride=CMP_STRIDE), :]
            head = head + rows * w_ref[r:r + 1, :]
            tail = tail + rows * w_ref[CMP_STRIDE + r:CMP_STRIDE + r + 1, :]
        sh_ref[0:nc, :] = tail
        sh_ref[nc:nc + 8, :] = jnp.zeros((8, KVW), F32)
        out = jnp.where(last, 0.0, head + sh_ref[1:nc + 1, :])
        dst[...] = (out if dst is okc_ref else out.T).astype(BF16)


def _compress(kc, vc, wk, wv):
    b, s, _ = kc.shape
    nc = s // CMP_STRIDE
    big = pl.BlockSpec((None, s, KVW), lambda i: (i, 0, 0))
    wsp = pl.BlockSpec((CMP_BLOCK, KVW), lambda i: (0, 0))
    return pl.pallas_call(
        functools.partial(_compress_kernel, nc=nc),
        out_shape=[jax.ShapeDtypeStruct((b, nc, KVW), BF16), jax.ShapeDtypeStruct((b, KVW, nc), BF16)],
        grid=(b,),
        in_specs=[big, big, wsp, wsp],
        out_specs=[pl.BlockSpec((None, nc, KVW), lambda i: (i, 0, 0)),
                   pl.BlockSpec((None, KVW, nc), lambda i: (i, 0, 0))],
        scratch_shapes=[pltpu.VMEM((nc + 8, KVW), F32)],
        compiler_params=_cparams(("arbitrary",)),
        name="compress",
    )(kc, vc, wk, wv)


def _topk_mask(vals, blk_f, n_top, axis=1):
    sel = jnp.zeros(vals.shape, F32)
    big = float(vals.shape[axis])
    for _ in range(n_top):
        mx = jnp.max(vals, axis=axis, keepdims=True)
        first = jnp.min(jnp.where(vals == mx, blk_f, big), axis=axis, keepdims=True)
        hit = blk_f == first
        sel = jnp.where(hit, 1.0, sel)
        vals = jnp.where(hit, -jnp.inf, vals)
    return sel


def _topk_mask_by_rank(vals, blk, n_valid, n_top):
    rank = jnp.zeros(vals.shape, F32)
    for j in range(n_valid):
        vj = vals[:, j:j + 1]
        beats = (vj > vals) | ((vj == vals) & (blk > j))
        rank = rank + jnp.where(beats, 1.0, 0.0)
    return jnp.where(rank < float(n_top), 1.0, 0.0)


def _pos_features(pos):
    hi = (pos // SEL_BLOCK).astype(F32)[:, None]
    lo = (pos % SEL_BLOCK).astype(F32)[:, None]
    return jnp.concatenate([hi, lo, jnp.zeros((pos.shape[0], LANES - 2), F32)], axis=1).astype(BF16)


def _importance_matrix(nc, nsel):
    j = jnp.arange(nc)[:, None]
    s = jnp.arange(nsel)[None, :]
    r = SEL_BLOCK // CMP_STRIDE
    a = (j >= r * s) & (j <= r * s + r - 1)
    b = (j + 1 >= r * s) & (j + 1 <= r * s + r - 1)
    return a.astype(F32) + b.astype(F32)


def _nsa_prompt_kernel(qt_ref, gst_ref, kc_ref, vct_ref, ks_ref, vst_ref, kw_ref, vwt_ref,
                       cfeat_ref, wfeat_ref, qfeat_ref, slope_ref,
                       y_ref, qk_scr, m_scr, l_scr, acc_scr, o_scr, sel_scr, imp_scr, flag_scr, ids_scr,
                       *, seq, tk, wl):
    i = pl.program_id(1)
    q0 = i * Q_BLOCK
    nq = Q_BLOCK
    gq = GROUP * nq
    nc = kc_ref.shape[0]
    nsel = seq // SEL_BLOCK
    n_top = min(TOP_BLOCKS, nsel)
    blk_per_tile = tk // SEL_BLOCK
    qpos = q0 + lax.broadcasted_iota(I32, (1, nq), 1)
    gst = gst_ref[...]

    crow = lax.broadcasted_iota(I32, (nc, nq), 0)
    cend = crow * CMP_STRIDE + (CMP_BLOCK - 1)
    mask_c = qpos >= cend
    kc = jnp.concatenate([kc_ref[...], cfeat_ref[...]], axis=1)
    vct = vct_ref[...]
    blk = lax.broadcasted_iota(I32, (nsel, nq), 0)
    blk_f = blk.astype(F32)
    cur = qpos // SEL_BLOCK
    forced = (blk == 0) | (blk == cur) | (blk == cur - 1)
    visible = blk * SEL_BLOCK <= qpos
    ws = pl.multiple_of(jnp.maximum(q0 - WINDOW, 0), Q_BLOCK)
    wpos = ws + lax.broadcasted_iota(I32, (wl, nq), 0)
    valid_w = lax.bitcast_convert_type(qpos - wpos, jnp.uint32) < WINDOW
    n_tiles = (q0 + nq + tk - 1) // tk
    half_rows = lax.broadcasted_iota(I32, (KVW, nq), 0) // HEAD_DIM
    tile_pos = lax.broadcasted_iota(I32, (SEL_BLOCK, nq), 0)

    def lanes4(x):
        return jnp.concatenate([x] * GROUP, axis=1)

    def gate_row(branch, k):
        r0 = branch * N_HEADS + k * GROUP
        return jnp.concatenate([gst[r0 + g:r0 + g + 1] for g in range(GROUP)], axis=1)

    mask_c4 = lanes4(mask_c)
    valid_w4 = lanes4(valid_w)
    kwt = jnp.concatenate([kw_ref[pl.ds(ws, wl), :], wfeat_ref[...]], axis=1)
    vwtt = vwt_ref[:, pl.ds(ws, wl)]

    for k in range(N_KV):
        for g in range(GROUP):
            qk_scr[k, 0:LANES, g * nq:(g + 1) * nq] = qt_ref[k * GROUP + g]
        qk_scr[k, LANES:2 * LANES, :] = qfeat_ref[k]
        qk = qk_scr[k]

        s = jnp.where(mask_c4, _dot(kc, qk), NEG)
        e = jnp.where(mask_c4, jnp.exp(s - jnp.max(s, axis=0, keepdims=True)), 0.0)
        l = jnp.sum(e, axis=0, keepdims=True)
        p = e * jnp.where(l > 0.0, 1.0 / l, 0.0)
        o_c = _dot(vct, p.astype(BF16))
        psum = p[:, 0:nq]
        for g in range(1, GROUP):
            psum = psum + p[:, g * nq:(g + 1) * nq]

        a = psum + jnp.where(crow == 0, 0.0, pltpu.roll(psum, 1, 0))
        a = a + pltpu.roll(a, nc - 1, 0)
        imp_scr[...] = a + pltpu.roll(a, nc - 2, 0)
        imp = imp_scr[pl.ds(0, nsel, stride=nc // nsel), :]
        vals = jnp.where(visible, jnp.where(forced, FORCE_SCORE, imp), NEG)
        sel = jnp.where(visible, _topk_mask(vals, blk_f, n_top, axis=0), 0.0)
        sel_scr[k] = jnp.where(sel > 0.5, 0.0, NEG)
        blk_any = jnp.max(sel, axis=1, keepdims=True)
        for t in range(seq // tk):
            hit = (jnp.max(blk_any[t * blk_per_tile:(t + 1) * blk_per_tile, :]) > 0.5).astype(I32)
            flag_scr[t] = hit if k == 0 else flag_scr[t] | hit

        s = jnp.where(valid_w4, _dot(kwt, qk), NEG)
        e = jnp.exp(s - jnp.max(s, axis=0, keepdims=True))
        p = e / jnp.sum(e, axis=0, keepdims=True)
        o_w = _dot(vwtt, p.astype(BF16))
        o_scr[k] = gate_row(0, k) * o_c + gate_row(2, k) * o_w

    m_scr[...] = jnp.full(m_scr.shape, NEG, F32)
    l_scr[...] = jnp.zeros(l_scr.shape, F32)
    acc_scr[...] = jnp.zeros(acc_scr.shape, F32)

    n_act = jnp.int32(0)
    for t in range(seq // tk):
        ids_scr[n_act] = t
        n_act = n_act + jnp.where((flag_scr[t] > 0) & (t < n_tiles), 1, 0)
    lane0 = lax.broadcasted_iota(I32, (tk, LANES), 1) == 0
    feat = wfeat_ref[0:tk, :]

    def sel_tiles(tiles):
        ta = tiles[0]
        starts = [pl.multiple_of(t * tk, tk) for t in tiles]
        keys = [jnp.concatenate([ks_ref[pl.ds(starts[0], tk), :], feat], axis=1)]
        for t, k0 in zip(tiles[1:], starts[1:]):
            shifted = jnp.where(lane0, feat.astype(F32) + ((t - ta) * blk_per_tile).astype(F32), feat.astype(F32))
            keys.append(jnp.concatenate([ks_ref[pl.ds(k0, tk), :], shifted.astype(BF16)], axis=1))
        kt = jnp.concatenate(keys, axis=0)
        vtt = jnp.concatenate([vst_ref[:, pl.ds(k0, tk)] for k0 in starts], axis=1)
        base = (starts[0] - q0).astype(F32)
        scores = _dot(kt, jnp.concatenate([qk_scr[k] for k in range(N_KV)], axis=1))
        probs, alphas = [], []
        for k in range(N_KV):
            neg = []
            for t, k0 in zip(tiles, starts):
                for j in range(blk_per_tile):
                    row = sel_scr[k, pl.ds(t * blk_per_tile + j, 1), :]
                    causal = qpos >= k0 + j * SEL_BLOCK + tile_pos
                    neg.append(jnp.where(causal, jnp.broadcast_to(row, (SEL_BLOCK, nq)), NEG))
            neg = lanes4(jnp.concatenate(neg, axis=0))
            off = slope_ref[k] * base
            s = scores[:, k * gq:(k + 1) * gq] + neg
            m_old = m_scr[k]
            m_new = jnp.maximum(m_old, jnp.max(s, axis=0, keepdims=True) + off)
            alpha = jnp.exp(m_old - m_new)
            p = jnp.exp(s - (m_new - off))
            l_scr[k] = alpha * l_scr[k] + jnp.sum(p, axis=0, keepdims=True)
            m_scr[k] = m_new
            probs.append(p.astype(BF16))
            alphas.append(alpha)
        pv = _dot(vtt, jnp.concatenate(probs, axis=1))
        for k in range(N_KV):
            acc_scr[k] = acc_scr[k] * alphas[k] + pv[:, k * gq:(k + 1) * gq]

    def sel_pair(i, carry):
        sel_tiles([ids_scr[2 * i], ids_scr[2 * i + 1]])
        return carry

    lax.fori_loop(0, n_act // 2, sel_pair, 0)
    pl.when(n_act % 2 == 1)(lambda: sel_tiles([ids_scr[n_act - 1]]))

    for k in range(N_KV):
        o = o_scr[k] + gate_row(1, k) * (acc_scr[k] / l_scr[k])
        for g in range(GROUP):
            h = k * GROUP + g
            oh = jnp.where(half_rows == k, o[:, g * nq:(g + 1) * nq], 0.0)
            y_ref[:, h * LANES:(h + 1) * LANES] = oh.T.astype(BF16)


def _nsa_prompt(qt, gst, kcmp, vcmpt, ksb, vst, kwb, vwt):
    b, _, _, s = qt.shape
    nq = Q_BLOCK
    gq = GROUP * nq
    nc = kcmp.shape[1]
    nsel = s // SEL_BLOCK
    tk = 256
    wl = WINDOW + Q_BLOCK
    assert s % tk == 0 and s >= wl
    assert s // SEL_BLOCK <= 2 * LANES, "position // 64 must stay exact in bf16"
    cfeat = _pos_features(jnp.arange(nc) * CMP_STRIDE + (CMP_BLOCK - 1))
    wfeat = _pos_features(jnp.arange(wl))
    slope_rows = jnp.repeat(jnp.asarray(SLOPES, F32).reshape(N_KV, 1, GROUP), nq, axis=2)
    qfeat = jnp.concatenate([slope_rows * SEL_BLOCK, slope_rows, jnp.zeros((N_KV, LANES - 2, gq), F32)],
                            axis=1).astype(BF16)
    rows = lambda r: pl.BlockSpec((None, r, KVW), lambda bi, i: (bi, 0, 0))
    cols = lambda c: pl.BlockSpec((None, KVW, c), lambda bi, i: (bi, 0, 0))
    const = lambda a: pl.BlockSpec(a.shape, lambda bi, i: (0,) * a.ndim)
    return pl.pallas_call(
        functools.partial(_nsa_prompt_kernel, seq=s, tk=tk, wl=wl),
        out_shape=jax.ShapeDtypeStruct((b, s, QPAD), BF16),
        grid=(b, s // nq),
        in_specs=[pl.BlockSpec((None, N_HEADS, LANES, nq), lambda bi, i: (bi, 0, 0, i)),
                  pl.BlockSpec((None, LANES, nq), lambda bi, i: (bi, 0, i)),
                  rows(nc), cols(nc), rows(s), cols(s), rows(s), cols(s),
                  const(cfeat), const(wfeat), const(qfeat), const(slope_rows)],
        out_specs=pl.BlockSpec((None, nq, QPAD), lambda bi, i: (bi, i, 0)),
        scratch_shapes=[pltpu.VMEM((N_KV, 2 * LANES, gq), BF16),
                        pltpu.VMEM((N_KV, 1, gq), F32),
                        pltpu.VMEM((N_KV, 1, gq), F32),
                        pltpu.VMEM((N_KV, KVW, gq), F32),
                        pltpu.VMEM((N_KV, KVW, gq), F32),
                        pltpu.VMEM((N_KV, nsel, nq), F32),
                        pltpu.VMEM((nc, nq), F32),
                        pltpu.SMEM((s // tk,), I32),
                        pltpu.SMEM((s // tk,), I32)],
        compiler_params=_cparams(("arbitrary", "arbitrary")),
        name="nsa_prompt",
    )(qt, gst, kcmp, vcmpt, ksb, vst, kwb, vwt, cfeat, wfeat, qfeat, slope_rows)


def _nsa_sample_kernel(pt_ref, q_ref, gate_ref, kcn_ref, vcn_ref, ksn_ref, vsn_ref, kwn_ref, vwn_ref,
                       vpn_ref, skw_ref, svw_ref, spool_ref, wk_ref, wv_ref, imat_ref, emat_ref,
                       ckc_ref, cvc_ref, cks_ref, cvs_ref,
                       o_ref, pooled_ref, buf, buft, win_scr, tail_scr, vext_scr, sem,
                       *, sb, ns, past, n_pages, page, n_seq, ncv, ncp, nks, wls, nselp, n_sel):
    step = pl.program_id(0)
    nrow = ns * N_HEADS
    par = step % 2

    def copies(n, side, r):
        out = []
        for p in range(n_pages):
            pg = pt_ref[n * n_pages + p]
            for c, cref in enumerate((ckc_ref, cvc_ref, cks_ref, cvs_ref)):
                out.append(pltpu.make_async_copy(cref.at[pg], buft.at[side, r, c, :, pl.ds(p * page, page)],
                                                 sem.at[side, r]))
        return out

    @pl.when(step == 0)
    def _():
        buf[:, :, past:, :] = jnp.zeros((sb, 2, buf.shape[2] - past, KVW), F32)
        tail_scr[...] = jnp.zeros_like(tail_scr)
        vext_scr[...] = jnp.zeros_like(vext_scr)
        for r in range(sb):
            for cp in copies(r, 0, r):
                cp.start()

    @pl.when(step + 1 < pl.num_programs(0))
    def _():
        for r in range(sb):
            for cp in copies((step + 1) * sb + r, 1 - par, r):
                cp.start()

    for r in range(sb):
        for cp in copies(step * sb + r, par, r):
            cp.wait()

    def new_rows_t(ref, r4, r):
        tail_scr[r, 0:ns, :] = ref[pl.ds(r4, ns), :]
        return tail_scr[r].T

    row = lax.broadcasted_iota(I32, (nrow, 1), 0)
    hrow = row % N_HEADS
    qpos = past + row // N_HEADS
    slope = jnp.exp2(-8.0 * (hrow.astype(F32) + 1.0) / N_HEADS)
    kvrow = hrow // GROUP
    lane = lax.broadcasted_iota(I32, (1, LANES), 1)
    half = (lane // HEAD_DIM) == kvrow
    grow = (row // N_HEADS) * N_KV + kvrow
    row8 = lax.broadcasted_iota(I32, (ns * N_KV, 1), 0)
    qpos8 = past + lax.broadcasted_iota(I32, (ns * N_KV, 1), 0) // N_KV
    blk = lax.broadcasted_iota(I32, (1, nselp), 1)
    blk_f = blk.astype(F32)
    cur = qpos8 // SEL_BLOCK
    forced = (blk == 0) | (blk == cur) | (blk == cur - 1)
    visible = (blk * SEL_BLOCK <= qpos8)
    inrange = blk < n_sel
    cend = lax.broadcasted_iota(I32, (1, ncp), 1) * CMP_STRIDE + (CMP_BLOCK - 1)
    mask_c = qpos >= cend
    bias_c = slope * (cend - qpos).astype(F32)
    kpos = lax.broadcasted_iota(I32, (1, nks), 1)
    causal_s = qpos >= kpos
    bias_s = slope * (kpos - qpos).astype(F32)
    wbuf = wls[0]
    wpos = past - wbuf + lax.broadcasted_iota(I32, (1, wls[1]), 1)
    dw = qpos - wpos
    valid_w = lax.bitcast_convert_type(dw, jnp.uint32) < WINDOW
    bias_w = slope * (wpos - qpos).astype(F32)
    prow = lax.broadcasted_iota(I32, (vext_scr.shape[1], 1), 0)
    wcol = 2 << (lax.broadcasted_iota(I32, (1, POOL_W), 1) // POOL_GW)
    n_top = min(TOP_BLOCKS, n_sel)

    def softmax_rows(s, mask):
        s = jnp.where(mask, s, NEG)
        mx = jnp.max(s, axis=1, keepdims=True)
        e = jnp.where(mask, jnp.exp(s - mx), 0.0)
        l = jnp.sum(e, axis=1, keepdims=True)
        return e * jnp.where(l > 0.0, 1.0 / l, 0.0)

    def seq_body(r):
        r4 = r * ns
        for c, new_ref in enumerate((kcn_ref, vcn_ref)):
            for p in range(n_pages):
                buf[r, c, p * page:(p + 1) * page, :] = buft[par, r, c, :, p * page:(p + 1) * page].T
            buf[r, c, past:past + ns, :] = new_ref[pl.ds(r4, ns), :]
        for c, new_ref in ((2, ksn_ref), (3, vsn_ref)):
            buft[par, r, c, :, past:past + LANES] = new_rows_t(new_ref, r4, r)

        qall = q_ref[pl.ds(r * nrow, nrow), :]
        gates = gate_ref[pl.ds(r * nrow, nrow), :]

        cmp = []
        for c, w_ref in ((0, wk_ref), (1, wv_ref)):
            span = CMP_STRIDE * ncv
            lo = buf[r, c, 0:span, :].reshape(ncv, CMP_STRIDE, KVW) * w_ref[0:CMP_STRIDE, :][None]
            hi = (buf[r, c, CMP_STRIDE:CMP_STRIDE + span, :].reshape(ncv, CMP_STRIDE, KVW)
                  * w_ref[CMP_STRIDE:CMP_BLOCK, :][None])
            acc = jnp.sum(lo + hi, axis=1)
            cmp.append(jnp.concatenate([acc, jnp.zeros((ncp - ncv, KVW), F32)], axis=0).astype(BF16))
        p_c = softmax_rows(_dot_nt(qall, cmp[0]) + bias_c, mask_c)
        o_c = _dot(p_c.astype(BF16), cmp[1])

        psum = jnp.zeros((ns * N_KV, ncp), F32)
        for i in range(ns * N_KV):
            r0 = (i // N_KV) * N_HEADS + (i % N_KV) * GROUP
            psum = jnp.where(row8 == i, jnp.sum(p_c[r0:r0 + GROUP], axis=0, keepdims=True), psum)
        imp = _dot_exact(psum, imat_ref[...])
        vals = jnp.where(inrange, jnp.where(visible, jnp.where(forced, FORCE_SCORE, imp), NEG), -jnp.inf)
        sel8 = _topk_mask_by_rank(vals, blk, n_sel, n_top)
        sel_rows = jnp.zeros((nrow, nselp), F32)
        for i in range(ns * N_KV):
            sel_rows = jnp.where(grow == i, sel8[i:i + 1], sel_rows)
        chosen = _dot(sel_rows.astype(BF16), emat_ref[...])

        kst = buft[par, r, 2].astype(BF16)
        vst = buft[par, r, 3].astype(BF16)
        p_s = softmax_rows(_dot(qall, kst) + bias_s, causal_s & (chosen > 0.5))
        o_s = _dot_nt(p_s.astype(BF16), vst)

        outs_w = []
        for state_ref, new_ref in ((skw_ref, kwn_ref), (svw_ref, vwn_ref)):
            win_scr[r, :, 0:wbuf] = state_ref[r]
            win_scr[r, :, wbuf:wbuf + LANES] = new_rows_t(new_ref, r4, r)
            outs_w.append(win_scr[r].astype(BF16))
        p_w = softmax_rows(_dot(qall, outs_w[0]) + bias_w, valid_w)
        o_w = _dot_nt(p_w.astype(BF16), outs_w[1])

        o = gates[:, 0:1] * o_c + gates[:, 1:2] * o_s + gates[:, 2:3] * o_w
        o_ref[pl.ds(r * nrow, nrow), :] = jnp.where(half, o, 0.0).astype(BF16)

        vext_scr[r, 0:POOL_BUF, :] = spool_ref[r]
        vext_scr[r, POOL_BUF:POOL_BUF + ns, :] = vpn_ref[pl.ds(r4, ns), :]
        ext = vext_scr[r]
        for t in range(ns):
            hi = POOL_BUF + t
            inwin = (prow <= hi) & (prow > hi - wcol)
            ssum = jnp.sum(jnp.where(inwin, ext, 0.0), axis=0, keepdims=True)
            cnt = jnp.minimum(past + t + 1, wcol).astype(F32)
            pooled_ref[pl.ds(r4 + t, 1), :] = ssum / cnt - ext[hi:hi + 1, :]

    for r in range(sb):
        seq_body(r)


def _nsa_sample(page_table, q_rows, gate_rows, new6, vp_new, state_kwt, state_vwt, state_pool, wk, wv, caches):
    n_seq, n_pages = page_table.shape
    page = caches[0].shape[2]
    past = n_pages * page
    ns = vp_new.shape[0] // n_seq
    wbuf = state_kwt.shape[2]
    sb = 2
    nrow = ns * N_HEADS
    assert ns <= SEL_BLOCK and page == LANES
    t_pad = -(-(past + ns) // SEL_BLOCK) * SEL_BLOCK
    n_cmp = t_pad // CMP_STRIDE - 1
    ncv = -(-n_cmp // 8) * 8
    ncp = -(-ncv // LANES) * LANES
    nks = past + LANES
    n_sel = t_pad // SEL_BLOCK
    nselp = LANES
    assert n_sel <= nselp
    wlp = wbuf + LANES
    buf_rows = -(-(CMP_STRIDE * ncv + CMP_STRIDE) // 8) * 8
    imat = _importance_matrix(ncp, nselp)
    emat = (jnp.arange(nselp)[:, None] == (jnp.arange(nks)[None, :] // SEL_BLOCK)).astype(BF16)

    seqblk = lambda rows, w: pl.BlockSpec((sb * rows, w), lambda i, pt: (i, 0))
    const = lambda a: pl.BlockSpec(a.shape, lambda i, pt: (0,) * a.ndim)
    kern = functools.partial(
        _nsa_sample_kernel, sb=sb, ns=ns, past=past, n_pages=n_pages, page=page, n_seq=n_seq,
        ncv=ncv, ncp=ncp, nks=nks, wls=(wbuf, wlp), nselp=nselp, n_sel=n_sel)
    grid_spec = pltpu.PrefetchScalarGridSpec(
        num_scalar_prefetch=1,
        grid=(n_seq // sb,),
        in_specs=[seqblk(nrow, LANES), seqblk(nrow, LANES)] + [seqblk(ns, KVW)] * 6 + [seqblk(ns, POOL_W)]
        + [pl.BlockSpec((sb, KVW, wbuf), lambda i, pt: (i, 0, 0))] * 2
        + [pl.BlockSpec((sb, POOL_BUF, POOL_W), lambda i, pt: (i, 0, 0))]
        + [const(wk), const(wv), const(imat), const(emat)]
        + [pl.BlockSpec(memory_space=pl.ANY)] * 4,
        out_specs=[seqblk(nrow, LANES), seqblk(ns, POOL_W)],
        scratch_shapes=[pltpu.VMEM((sb, 2, buf_rows, KVW), F32),
                        pltpu.VMEM((2, sb, 4, KVW, nks), F32),
                        pltpu.VMEM((sb, KVW, wlp), F32),
                        pltpu.VMEM((sb, LANES, KVW), F32),
                        pltpu.VMEM((sb, 24, POOL_W), F32),
                        pltpu.SemaphoreType.DMA((2, sb))],
    )
    return pl.pallas_call(
        kern,
        out_shape=[jax.ShapeDtypeStruct((n_seq * nrow, LANES), BF16),
                   jax.ShapeDtypeStruct((n_seq * ns, POOL_W), F32)],
        grid_spec=grid_spec,
        compiler_params=_cparams(("arbitrary",)),
        name="nsa_sample",
    )(page_table.reshape(-1), q_rows, gate_rows, *new6, vp_new, state_kwt, state_vwt, state_pool, wk, wv,
      imat, emat, *caches)


def _route(logits_t, bias_col, tm):
    sc = jax.nn.sigmoid(logits_t)
    biased = sc + bias_col
    epg = EXPERTS_PER_GROUP
    row8 = lax.broadcasted_iota(I32, (epg, tm), 0).astype(F32)
    ninf = -jnp.inf
    grp = jnp.zeros((N_EGROUPS, tm), F32)
    for g in range(N_EGROUPS):
        bg = biased[g * epg:(g + 1) * epg]
        m1 = jnp.max(bg, axis=0, keepdims=True)
        first = jnp.min(jnp.where(bg == m1, row8, float(epg)), axis=0, keepdims=True)
        m2 = jnp.max(jnp.where(row8 == first, ninf, bg), axis=0, keepdims=True)
        grp = jnp.where(row8 == float(g), m1 + m2, grp)
    keep = jnp.zeros((N_EGROUPS, tm), F32)
    vals = grp
    for _ in range(TOPK_GROUPS):
        mx = jnp.max(vals, axis=0, keepdims=True)
        first = jnp.min(jnp.where(vals == mx, row8, float(N_EGROUPS)), axis=0, keepdims=True)
        hit = row8 == first
        keep = jnp.where(hit, 1.0, keep)
        vals = jnp.where(hit, ninf, vals)
    masked = jnp.concatenate(
        [jnp.where(keep[g:g + 1] > 0.5, biased[g * epg:(g + 1) * epg], NEG) for g in range(N_EGROUPS)], axis=0)
    rowe = lax.broadcasted_iota(I32, (N_EXPERTS, tm), 0).astype(F32)
    chosen = jnp.zeros((N_EXPERTS, tm), F32)
    vals = masked
    picks = []
    for _ in range(TOP_K):
        mx = jnp.max(vals, axis=0, keepdims=True)
        first = jnp.min(jnp.where(vals == mx, rowe, float(N_EXPERTS)), axis=0, keepdims=True)
        hit = rowe == first
        chosen = jnp.where(hit, sc, chosen)
        vals = jnp.where(hit, ninf, vals)
        picks.append((hit, first))
    return ROUTED_SCALE * chosen / jnp.sum(chosen, axis=0, keepdims=True), picks


def _pack_bf16_pairs(x):
    c = x.shape[1] // 2
    bits = lambda v: lax.bitcast_convert_type(v.astype(BF16).astype(F32), jnp.uint32)
    return (bits(x[:, :c]) >> 16) | (bits(x[:, c:]) & jnp.uint32(0xFFFF0000))


def _unpack_bf16_pairs(w):
    lo = lax.bitcast_convert_type(w << 16, F32)
    hi = lax.bitcast_convert_type(w & jnp.uint32(0xFFFF0000), F32)
    return jnp.concatenate([lo, hi], axis=1)


def _finish_kernel(x_ref, pooled_ref, y_ref, gm_ref, g1_ref, shift_ref, scale_ref,
                   wlin_ref, pscale_ref, wpo_ref, wno_ref, wo_ref, n2_ref, wr_ref, br_ref,
                   *rest, tm, d, sparse):
    if sparse:
        tri_ref, x1_ref, u2_ref, up_ref, eid_ref, gk_ref, rank_ref, cnt_ref, carry_scr = rest
    else:
        x1_ref, u2_ref, gates_ref = rest
    x = x_ref[...].reshape(tm, d)
    pooled = pooled_ref[...].reshape(tm, POOL_W).astype(BF16)
    y_pool = _dot(pooled, wlin_ref[...]) * pscale_ref[...]
    a = _dot(y_pool.astype(BF16), wpo_ref[...])
    b = _dot(y_ref[...].reshape(tm, QPAD), wno_ref[...])
    gm = gm_ref[...].reshape(tm, 2 * d)
    merged = gm[:, :d] * a + gm[:, d:] * b
    x1 = x + _rows2d(g1_ref) * _dot(merged.astype(BF16), wo_ref[...])
    x1_ref[...] = x1.reshape(x1_ref.shape)
    u2 = _rmsnorm(x1, n2_ref[...]) * (1.0 + _rows2d(scale_ref)) + _rows2d(shift_ref)
    u2b = u2.astype(BF16)
    u2_ref[...] = u2b.reshape(u2_ref.shape)
    logits_t = _dot_nt(wr_ref[...], u2b)
    gates_t, picks = _route(logits_t[:N_EXPERTS], br_ref[...], tm)
    if not sparse:
        gates_t = jnp.concatenate([gates_t, jnp.zeros((LANES - N_EXPERTS, tm), F32)], axis=0)
        gates_ref[...] = gates_t.T.reshape(gates_ref.shape)
        return

    @pl.when((pl.program_id(0) == 0) & (pl.program_id(1) == 0))
    def _():
        carry_scr[...] = jnp.zeros_like(carry_scr)

    packed = _pack_bf16_pairs(u2)
    for s in range(up_ref.shape[0]):
        up_ref[s] = packed[:, s * SC_ROW_WORDS:(s + 1) * SC_ROW_WORDS]
    hit_all = picks[0][0]
    for hit, _ in picks[1:]:
        hit_all = hit_all | hit
    hits = jnp.where(hit_all, 1.0, 0.0).astype(BF16)
    before = _dot(hits, tri_ref[...]) + jnp.concatenate([carry_scr[...]] * (tm // LANES), axis=1)
    eids, gks, ranks = [], [], []
    for hit, first in picks:
        eids.append(first)
        gks.append(jnp.sum(jnp.where(hit, gates_t, 0.0), axis=0, keepdims=True))
        ranks.append(jnp.sum(jnp.where(hit, before, 0.0), axis=0, keepdims=True))
    pick_row = lax.broadcasted_iota(I32, (TOP_K, tm), 0)

    def stack(rows):
        out = jnp.zeros((TOP_K, tm), F32)
        for r, v in enumerate(rows):
            out = jnp.where(pick_row == r, v, out)
        return out

    eid_ref[...] = stack(eids).astype(I32).reshape(eid_ref.shape)
    gk_ref[...] = stack(gks).reshape(gk_ref.shape)
    rank_ref[...] = stack(ranks).astype(I32).reshape(rank_ref.shape)
    carry_scr[...] += _dot(hits, jnp.ones((tm, LANES), BF16))
    cnt_ref[...] = carry_scr[...]


def _finish(x3, pooled, ynsa, gm, mods, wts, *, tm, sparse):
    g, r, d = x3.shape
    nt = r // tm
    g1, shift2, scale2 = mods
    per_row = g1.ndim == 2
    tok = lambda w: pl.BlockSpec((1, tm, w), lambda b, j: (b, j, 0))
    tok_t = lambda rows: pl.BlockSpec((1, rows, tm), lambda b, j: (b, 0, j))
    if per_row:
        mod_spec = lambda col: pl.BlockSpec((tm, d), lambda b, j, col=col: (b * nt + j, col))
    else:
        mod_spec = lambda col: pl.BlockSpec((1, 1, d), lambda b, j, col=col: (b, 0, col))
    const = lambda a: pl.BlockSpec(a.shape, lambda b, j: (0,) * a.ndim)
    out_shape = [jax.ShapeDtypeStruct((g, r, d), F32), jax.ShapeDtypeStruct((g, r, d), BF16)]
    out_specs = [tok(d), tok(d)]
    scratch = []
    if sparse:
        tri = (jnp.arange(tm)[:, None] < jnp.arange(tm)[None, :]).astype(BF16)
        wts = tuple(wts) + (tri,)
        split = d // 2 // SC_ROW_WORDS
        out_shape += [jax.ShapeDtypeStruct((split, g * r, SC_ROW_WORDS), jnp.uint32),
                      jax.ShapeDtypeStruct((g, TOP_K, r), I32), jax.ShapeDtypeStruct((g, TOP_K, r), F32),
                      jax.ShapeDtypeStruct((g, TOP_K, r), I32), jax.ShapeDtypeStruct((N_EXPERTS, LANES), F32)]
        out_specs += [pl.BlockSpec((split, tm, SC_ROW_WORDS), lambda b, j: (0, b * nt + j, 0)),
                      tok_t(TOP_K), tok_t(TOP_K), tok_t(TOP_K),
                      pl.BlockSpec((N_EXPERTS, LANES), lambda b, j: (0, 0))]
        scratch.append(pltpu.VMEM((N_EXPERTS, LANES), F32))
    else:
        out_shape.append(jax.ShapeDtypeStruct((g, r, LANES), F32))
        out_specs.append(tok(LANES))
    return pl.pallas_call(
        functools.partial(_finish_kernel, tm=tm, d=d, sparse=sparse),
        out_shape=out_shape,
        grid=(g, nt),
        in_specs=[tok(d), tok(POOL_W), tok(QPAD), tok(2 * d), mod_spec(2), mod_spec(3), mod_spec(4)]
        + [const(w) for w in wts],
        out_specs=out_specs,
        scratch_shapes=scratch,
        compiler_params=_cparams(("arbitrary", "arbitrary")),
        name="finish_route" if sparse else "finish",
    )(x3, pooled, ynsa, gm, g1, shift2, scale2, *wts)


def _moe_kernel(u_ref, gates_ref, x1_ref, g2_ref, nf_ref, wg_ref, wu_ref, wd_ref, sg_ref, su_ref, sd_ref,
                y_ref, acc_ref, *, tm, d, eps):
    e = pl.program_id(2)
    u = u_ref[...].reshape(tm, d)

    @pl.when(e == 0)
    def _():
        hs = _silu(_dot(u, sg_ref[...])) * _dot(u, su_ref[...])
        acc_ref[...] = _dot(hs.astype(BF16), sd_ref[...])

    gates = gates_ref[...].reshape(tm, LANES)
    lane = lax.broadcasted_iota(I32, (1, LANES), 1)
    hidden = []
    for j in range(eps):
        h = _silu(_dot(u, wg_ref[j].astype(BF16))) * _dot(u, wu_ref[j].astype(BF16))
        gate = jnp.sum(jnp.where(lane == e * eps + j, gates, 0.0), axis=1, keepdims=True)
        hidden.append((h * gate).astype(BF16))
    f = wd_ref.shape[1]
    acc_ref[...] += _dot(jnp.concatenate(hidden, axis=1), wd_ref[...].reshape(eps * f, d).astype(BF16))

    @pl.when(e == pl.num_programs(2) - 1)
    def _():
        x2 = x1_ref[...].reshape(tm, d) + _rows2d(g2_ref) * acc_ref[...]
        y_ref[...] = _rmsnorm(x2, nf_ref[...]).reshape(y_ref.shape)


def _moe(u2, gates, x1, g2, normf, w_gate, w_up, w_down, sg, su, sd, *, tm):
    g, r, d = x1.shape
    nt = r // tm
    ne, _, f = w_gate.shape
    per_row = g2.ndim == 2
    tok = lambda w: pl.BlockSpec((1, tm, w), lambda b, j, e: (b, j, 0))
    if per_row:
        g2_spec = pl.BlockSpec((tm, d), lambda b, j, e: (b * nt + j, 5))
    else:
        g2_spec = pl.BlockSpec((1, 1, d), lambda b, j, e: (b, 0, 5))
    once = pl.Buffered(buffer_count=1)
    const = lambda a: pl.BlockSpec(a.shape, lambda b, j, e: (0,) * a.ndim, pipeline_mode=once)
    eps = 4
    return pl.pallas_call(
        functools.partial(_moe_kernel, tm=tm, d=d, eps=eps),
        out_shape=jax.ShapeDtypeStruct((g, r, d), F32),
        grid=(g, nt, ne // eps),
        in_specs=[tok(d), tok(LANES),
                  pl.BlockSpec((1, tm, d), lambda b, j, e: (b, j, 0), pipeline_mode=once),
                  g2_spec, const(normf),
                  pl.BlockSpec((eps, d, f), lambda b, j, e: (e, 0, 0)),
                  pl.BlockSpec((eps, d, f), lambda b, j, e: (e, 0, 0)),
                  pl.BlockSpec((eps, f, d), lambda b, j, e: (e, 0, 0)),
                  const(sg), const(su), const(sd)],
        out_specs=tok(d),
        scratch_shapes=[pltpu.VMEM((tm, d), F32)],
        compiler_params=_cparams(("arbitrary", "arbitrary", "arbitrary")),
        name="moe",
    )(u2, gates, x1, g2, normf, w_gate, w_up, w_down, sg, su, sd)


SC_WINDOW = 128
SC_ROW_WORDS = 256
MOE_ROWS = 512


def _sc_mesh():
    return plsc.VectorSubcoreMesh(core_axis_name="c", subcore_axis_name="s")


def _sc_scatter_rows(src, dst_idx, n_dst):
    n, w = src.shape
    nk = dst_idx.shape[0]

    @pl.kernel(out_type=jax.ShapeDtypeStruct((n_dst, w), src.dtype), mesh=_sc_mesh(), scratch_types=[])
    def scatter(src_hbm, idx_hbm, dst_hbm):
        def body(rows_vmem, idx_vmem):
            pltpu.sync_copy(rows_vmem, dst_hbm.at[idx_vmem.at[0]])

        pltpu.emit_pipeline(
            body,
            grid=(nk, n // SC_WINDOW),
            in_specs=[pl.BlockSpec((SC_WINDOW, w), index_map=lambda k, i: (i, 0)),
                      pl.BlockSpec((1, SC_WINDOW), index_map=lambda k, i: (k, i))],
            out_specs=[],
            core_axis_name=("c", "s"),
            dimension_semantics=(pltpu.PARALLEL, pltpu.PARALLEL),
        )(src_hbm, idx_hbm)

    return scatter(src, dst_idx)


def _sc_gather_rows(src, idx):
    n, w = idx.shape[0], src.shape[1]

    @pl.kernel(out_type=jax.ShapeDtypeStruct((n, w), src.dtype), mesh=_sc_mesh(), scratch_types=[])
    def gather(src_hbm, idx_hbm, out_hbm):
        def body(idx_vmem, out_vmem):
            pltpu.sync_copy(src_hbm.at[idx_vmem.at[0]], out_vmem)

        pltpu.emit_pipeline(
            body,
            grid=(n // SC_WINDOW,),
            in_specs=[pl.BlockSpec((1, SC_WINDOW), index_map=lambda i: (0, i))],
            out_specs=[pl.BlockSpec((SC_WINDOW, w), index_map=lambda i: (i, 0))],
            core_axis_name=("c", "s"),
            dimension_semantics=(pltpu.PARALLEL,),
        )(idx_hbm, out_hbm)

    return gather(src, idx.reshape(1, n))


MOE_BLOCKS_PER_STEP = 4


def _expert_rows_kernel(te_ref, nt_ref, x_ref, *refs):
    y_ref = refs[-1]
    split = x_ref.shape[0]
    for j in range(MOE_BLOCKS_PER_STEP):
        wg_ref, wu_ref, wd_ref = refs[3 * j:3 * j + 3]
        rows = slice(j * MOE_ROWS, (j + 1) * MOE_ROWS)

        @pl.when(pl.program_id(0) * MOE_BLOCKS_PER_STEP + j < nt_ref[0])
        def _():
            x = _unpack_bf16_pairs(jnp.concatenate([x_ref[s, rows] for s in range(split)], axis=1)).astype(BF16)
            h = _silu(_dot(x, wg_ref[...].astype(BF16))) * _dot(x, wu_ref[...].astype(BF16))
            y = _pack_bf16_pairs(_dot(h.astype(BF16), wd_ref[...].astype(BF16)))
            for s in range(split):
                y_ref[s, rows] = y[:, s * SC_ROW_WORDS:(s + 1) * SC_ROW_WORDS]


def _expert_rows(tile_expert, n_tiles, x_sorted, w_gate, w_up, w_down):
    split, p, words = x_sorted.shape
    ne, d, f = w_gate.shape
    bps = MOE_BLOCKS_PER_STEP
    wspec = lambda a, b, j: pl.BlockSpec((None, a, b), lambda i, te, nt: (te[i * bps + j], 0, 0))
    rows = pl.BlockSpec((split, bps * MOE_ROWS, words), lambda i, te, nt: (0, i, 0))
    weights, wspecs = [], []
    for j in range(bps):
        weights += [w_gate, w_up, w_down]
        wspecs += [wspec(d, f, j), wspec(d, f, j), wspec(f, d, j)]
    grid_spec = pltpu.PrefetchScalarGridSpec(
        num_scalar_prefetch=2,
        grid=(p // (bps * MOE_ROWS),),
        in_specs=[rows] + wspecs,
        out_specs=rows,
    )
    return pl.pallas_call(
        _expert_rows_kernel,
        out_shape=jax.ShapeDtypeStruct((split, p, words), jnp.uint32),
        grid_spec=grid_spec,
        compiler_params=_cparams(("arbitrary",)),
        name="expert_rows",
    )(tile_expert, n_tiles, x_sorted, *weights)


def _combine_kernel(yg_ref, gk_ref, u_ref, x1_ref, g2_ref, nf_ref, sg_ref, su_ref, sd_ref, y_ref, *, tm, d):
    u = u_ref[...].reshape(tm, d)
    hs = _silu(_dot(u, sg_ref[...])) * _dot(u, su_ref[...])
    acc = _dot(hs.astype(BF16), sd_ref[...])
    gk = gk_ref[...].reshape(tm, LANES)
    lane = lax.broadcasted_iota(I32, (1, LANES), 1)
    split = yg_ref.shape[0]
    for k in range(TOP_K):
        gate = jnp.sum(jnp.where(lane == k, gk, 0.0), axis=1, keepdims=True)
        words = jnp.concatenate([yg_ref[s, k] for s in range(split)], axis=1)
        acc = acc + gate * _unpack_bf16_pairs(words)
    x2 = x1_ref[...].reshape(tm, d) + _rows2d(g2_ref) * acc
    y_ref[...] = _rmsnorm(x2, nf_ref[...]).reshape(y_ref.shape)


def _combine(yg, gk, u2, x1, g2, normf, sg, su, sd, *, tm):
    g, r, d = x1.shape
    nt = r // tm
    split, _, _, words = yg.shape
    tok = lambda w: pl.BlockSpec((1, tm, w), lambda b, j: (b, j, 0))
    const = lambda a: pl.BlockSpec(a.shape, lambda b, j: (0,) * a.ndim)
    return pl.pallas_call(
        functools.partial(_combine_kernel, tm=tm, d=d),
        out_shape=jax.ShapeDtypeStruct((g, r, d), F32),
        grid=(g, nt),
        in_specs=[pl.BlockSpec((split, TOP_K, tm, words), lambda b, j: (0, 0, b * nt + j, 0)),
                  tok(LANES), tok(d), tok(d),
                  pl.BlockSpec((1, 1, d), lambda b, j: (b, 0, 5)), const(normf), const(sg), const(su), const(sd)],
        out_specs=tok(d),
        compiler_params=_cparams(("arbitrary", "arbitrary")),
        name="moe_combine",
    )(yg, gk, u2, x1, g2, normf, sg, su, sd)


def _moe_sorted_experts(u2p, eid_t, rank_t, counts, w_gate, w_up, w_down):
    split, n, _ = u2p.shape
    ne = w_gate.shape[0]
    cnt = counts[:, 0].astype(I32)
    padded = -(-cnt // MOE_ROWS) * MOE_ROWS
    seg_end = jnp.cumsum(padded)
    seg_start = seg_end - padded
    p_rows = n * TOP_K + ne * MOE_ROWS
    eid = eid_t.transpose(1, 0, 2).reshape(TOP_K, n)
    start = jnp.sum(jnp.where(eid[:, :, None] == jnp.arange(ne, dtype=I32), seg_start, 0), axis=-1)
    pos = start + rank_t.transpose(1, 0, 2).reshape(TOP_K, n)
    first_row = jnp.arange(p_rows // MOE_ROWS, dtype=I32) * MOE_ROWS
    tile_expert = jnp.minimum(jnp.sum(seg_end[None, :] <= first_row[:, None], axis=1), ne - 1).astype(I32)
    n_tiles = (seg_end[-1:] // MOE_ROWS).astype(I32)
    scat_idx = jnp.concatenate([pos + s * p_rows for s in range(split)], axis=1)
    gath_idx = jnp.concatenate([pos.reshape(-1) + s * p_rows for s in range(split)])
    x_sorted = _sc_scatter_rows(u2p.reshape(split * n, SC_ROW_WORDS), scat_idx, split * p_rows)
    y_sorted = _expert_rows(tile_expert, n_tiles, x_sorted.reshape(split, p_rows, SC_ROW_WORDS),
                            w_gate, w_up, w_down)
    return y_sorted.reshape(split * p_rows, SC_ROW_WORDS), gath_idx


def _moe_sorted_combine(y_sorted, gath_idx, gk_t, u2, x1, g2, normf, sg, su, sd):
    g, r, _ = x1.shape
    split = gath_idx.shape[0] // (TOP_K * g * r)
    yg = _sc_gather_rows(y_sorted, gath_idx)
    gk = jnp.pad(gk_t.transpose(0, 2, 1), ((0, 0), (0, 0), (0, LANES - TOP_K)))
    return _combine(yg.reshape(split, TOP_K, g * r, SC_ROW_WORDS), gk, u2, x1, g2, normf, sg, su, sd, tm=512)


def _kv_slot_mask():
    return (jnp.arange(N_HEADS)[:, None] // GROUP == jnp.arange(N_KV)[None, :]).astype(F32)


def _prep_w_in(w_in, d):
    q0 = POOL_W
    kv0 = q0 + N_HEADS * HEAD_DIM
    gn0 = kv0 + 6 * KVW
    gm0 = gn0 + 3 * N_HEADS
    wq = w_in[:, q0:kv0].reshape(d, N_HEADS, 1, HEAD_DIM) * (HEAD_DIM ** -0.5)
    wq = (wq * _kv_slot_mask()[None, :, :, None]).reshape(d, QPAD)
    wgn = jnp.pad(w_in[:, gn0:gm0], ((0, 0), (0, LANES - 3 * N_HEADS)))
    return jnp.concatenate([w_in[:, :q0], wq, w_in[:, kv0:gn0], wgn, w_in[:, gm0:]], axis=1).astype(BF16)


def _prep_w_nsa_out(w, d):
    w = w.reshape(N_HEADS, 1, HEAD_DIM, d) * _kv_slot_mask()[:, :, None, None]
    return w.reshape(QPAD, d).astype(BF16)


def _block_diag(w_lin):
    g, c, _ = w_lin.shape
    eye = jnp.eye(g, dtype=F32)
    return (w_lin[:, :, None, :] * eye[:, None, :, None]).reshape(g * c, g * c).astype(BF16)


def kernel(x_prompt, x_sample, cache_kc, cache_vc, cache_ks, cache_vs, state_kw, state_vw, state_pool,
           page_table, c_prompt, c_sample, norm1_g, norm2_g, normf_g, w_ada, b_ada, w_in, w_pool_lin,
           pool_scale, w_cmp_k, w_cmp_v, w_pool_out, w_nsa_out, w_o, w_router, b_router, w_gate, w_up,
           w_down, ws_gate, ws_up, ws_down):
    depth = w_in.shape[0]
    assert depth == 1, "single-layer stack"
    bsz, seq, d = x_prompt.shape
    n_seq, ns, _ = x_sample.shape
    wbuf = state_kw.shape[2]
    lyr = 0

    c_all = jnp.concatenate([c_prompt, c_sample], axis=0)
    rows = c_all.shape[0]
    rows_p = -(-rows // 8) * 8
    mod = _adaln(jnp.pad(c_all, ((0, rows_p - rows), (0, 0))), w_ada[lyr], b_ada[lyr])
    mod_p = mod[:bsz].reshape(bsz, 1, 6 * d)
    mod_s = jnp.repeat(mod[bsz:bsz + n_seq], ns, axis=0)

    w2 = _prep_w_in(w_in[lyr], d)
    g1n = norm1_g[lyr].reshape(1, d)
    wk = w_cmp_k[lyr].reshape(CMP_BLOCK, KVW)
    wv = w_cmp_v[lyr].reshape(CMP_BLOCK, KVW)
    fin_w = (_block_diag(w_pool_lin[lyr]), pool_scale[lyr].reshape(1, POOL_W), w_pool_out[lyr].astype(BF16),
             _prep_w_nsa_out(w_nsa_out[lyr], d), w_o[lyr].astype(BF16), norm2_g[lyr].reshape(1, d),
             jnp.pad(w_router[lyr].T, ((0, LANES - N_EXPERTS), (0, 0))).astype(BF16),
             b_router[lyr].reshape(N_EXPERTS, 1))
    moe_w = (w_gate[lyr], w_up[lyr], w_down[lyr], ws_gate[lyr].astype(BF16), ws_up[lyr].astype(BF16),
             ws_down[lyr].astype(BF16))
    nf = normf_g.reshape(1, d)

    tm_p = 512
    (vp, kc, vc, ks, vs, kw, vw, gm, ksb, kwb, vst, vwt, qt, gst, pooled) = _in_proj(
        x_prompt, mod_p, mod_p, g1n, w2, tm=tm_p, prompt=True)
    kcmp, vcmpt = _compress(kc, vc, wk, wv)
    ynsa = _nsa_prompt(qt, gst, kcmp, vcmpt, ksb, vst, kwb, vwt)
    x1, u2, u2p, eid_t, gk_t, rank_t, counts = _finish(
        x_prompt, pooled, ynsa, gm, (mod_p, mod_p, mod_p), fin_w, tm=tm_p, sparse=True)

    n_tok = n_seq * ns
    xs3 = x_sample.reshape(1, n_tok, d)
    tm_s = 128
    (vp_s, kc_s, vc_s, ks_s, vs_s, kw_s, vw_s, gm_s, q_s, gs_s) = _in_proj(
        xs3, mod_s, mod_s, g1n, w2, tm=tm_s, prompt=False)
    two = lambda a: a.reshape(n_tok, a.shape[-1])
    q_rows = q_s.reshape(n_tok * N_HEADS, LANES)
    gate_rows = two(gs_s)[:, :3 * N_HEADS].reshape(n_tok, 3, N_HEADS).transpose(0, 2, 1)
    gate_rows = jnp.pad(gate_rows.reshape(n_tok * N_HEADS, 3), ((0, 0), (0, LANES - 3)))
    n_pool = cache_kc.shape[1]
    page = cache_kc.shape[2]
    rows_minor = lambda a: jnp.transpose(a, (0, 2, 3, 1)).reshape(a.shape[0], KVW, a.shape[1])
    caches = [rows_minor(c[lyr]) for c in (cache_kc, cache_vc, cache_ks, cache_vs)]
    o_rows, pooled_s = _nsa_sample(
        page_table, q_rows, gate_rows, [two(a) for a in (kc_s, vc_s, ks_s, vs_s, kw_s, vw_s)], two(vp_s),
        rows_minor(state_kw[lyr]), rows_minor(state_vw[lyr]), state_pool[lyr], wk, wv, caches)
    ynsa_s = o_rows.reshape(1, n_tok, QPAD)
    y_sorted, gath_idx = _moe_sorted_experts(u2p, eid_t, rank_t, counts, *moe_w[:3])
    ynsa_s, y_sorted = lax.optimization_barrier((ynsa_s, y_sorted))
    y_prompt = _moe_sorted_combine(y_sorted, gath_idx, gk_t, u2, x1, mod_p, nf, *moe_w[3:])
    x1_s, u2_s, gates_s = _finish(xs3, pooled_s.reshape(1, n_tok, POOL_W), ynsa_s, gm_s,
                                  (mod_s, mod_s, mod_s), fin_w, tm=tm_s, sparse=False)
    y_sample = _moe(u2_s, gates_s, x1_s, mod_s, nf, *moe_w, tm=n_tok).reshape(n_seq, ns, d)

    kvp = lambda a: a.reshape(1, bsz, seq, N_KV, HEAD_DIM)
    tailp = lambda a: jnp.pad(a, ((0, 0), (wbuf, 0), (0, 0)))[:, -wbuf:].reshape(1, bsz, wbuf, N_KV, HEAD_DIM)
    kvs = lambda a: a.reshape(1, n_seq, ns, N_KV, HEAD_DIM)
    wins = lambda st, new: jnp.concatenate(
        [st[lyr], new.reshape(n_seq, ns, N_KV, HEAD_DIM)], axis=1)[None, :, -wbuf:]
    pool_p = vp[:, -POOL_BUF:][None]
    pool_s = jnp.concatenate([state_pool[lyr], vp_s.reshape(n_seq, ns, POOL_W)], axis=1)[None, :, -POOL_BUF:]
    return (y_prompt, y_sample, kvp(kc), kvp(vc), kvp(ks), kvp(vs), tailp(kw), tailp(vw), pool_p,
            kvs(kc_s), kvs(vc_s), kvs(ks_s), kvs(vs_s), wins(state_kw, kw_s), wins(state_vw, vw_s), pool_s)
```

```python
import functools

import jax
import jax.numpy as jnp
from jax import lax
from jax.experimental import pallas as pl
from jax.experimental.pallas import tpu as pltpu
from jax.experimental.pallas import tpu_sc as plsc

F32 = jnp.float32
BF16 = jnp.bfloat16
I32 = jnp.int32

POOL_WINDOWS = (2, 4, 8, 16)
POOL_GW = 64
POOL_W = 256
POOL_BUF = 15
N_HEADS = 8
HEAD_DIM = 64
N_KV = 2
GROUP = N_HEADS // N_KV
CMP_STRIDE = 16
CMP_BLOCK = 32
SEL_BLOCK = 64
TOP_BLOCKS = 16
WINDOW = 512
Q_BLOCK = 128
FORCE_SCORE = 1e4
N_EXPERTS = 64
N_EGROUPS = 8
EXPERTS_PER_GROUP = N_EXPERTS // N_EGROUPS
TOPK_GROUPS = 4
TOP_K = 8
ROUTED_SCALE = 2.5
EPS = 1e-6
NEG = -1e30
SLOPES = tuple(2.0 ** (-8.0 * (h + 1.0) / N_HEADS) for h in range(N_HEADS))

LANES = 128
QPAD = N_HEADS * LANES
KVW = N_KV * HEAD_DIM
VMEM_LIMIT = 56 * 1024 * 1024


def _cparams(sem):
    return pltpu.CompilerParams(dimension_semantics=sem, vmem_limit_bytes=VMEM_LIMIT)


def _dot(a, b):
    return jnp.dot(a, b, preferred_element_type=F32)


def _dot_nt(a, b):
    return lax.dot_general(a, b, (((1,), (1,)), ((), ())), preferred_element_type=F32)


def _dot_exact(a, b):
    return jnp.dot(a, b, preferred_element_type=F32, precision=lax.Precision.HIGHEST)


def _rows2d(ref):
    v = ref[...]
    return v.reshape(v.shape[-2], v.shape[-1])


def _rmsnorm(x, g):
    return x * lax.rsqrt(jnp.mean(x * x, axis=-1, keepdims=True) + EPS) * g


def _silu(x):
    return x * jax.nn.sigmoid(x)


def _adaln_kernel(c_ref, w_ref, b_ref, o_ref):
    s = _silu(c_ref[...]).astype(BF16)
    o_ref[...] = _dot(s, w_ref[...].astype(BF16)) + b_ref[...]


def _adaln(c, w_ada, b_ada):
    rows, d = c.shape
    n = w_ada.shape[1]
    tn = 512
    return pl.pallas_call(
        _adaln_kernel,
        out_shape=jax.ShapeDtypeStruct((rows, n), F32),
        grid=(n // tn,),
        in_specs=[pl.BlockSpec((rows, d), lambda j: (0, 0)),
                  pl.BlockSpec((d, tn), lambda j: (0, j)),
                  pl.BlockSpec((1, tn), lambda j: (0, j))],
        out_specs=pl.BlockSpec((rows, tn), lambda j: (0, j)),
        compiler_params=_cparams(("arbitrary",)),
        name="adaln",
    )(c, w_ada, b_ada.reshape(1, n))


_C_VP = 0
_C_Q = _C_VP + POOL_W
_C_KV = _C_Q + QPAD
_C_GN = _C_KV + 6 * KVW
_C_GM = _C_GN + LANES


def _pool_window_sums(ext, tm):
    s2 = ext + pltpu.roll(ext, 1, 0)
    s4 = s2 + pltpu.roll(s2, 2, 0)
    s8 = s4 + pltpu.roll(s4, 4, 0)
    s16 = s8 + pltpu.roll(s8, 8, 0)
    grp = lax.broadcasted_iota(I32, (1, POOL_W), 1) // POOL_GW
    pick = jnp.where(grp == 0, s2, jnp.where(grp == 1, s4, jnp.where(grp == 2, s8, s16)))
    return pick[16:16 + tm]


def _in_proj_kernel(x_ref, shift_ref, scale_ref, g_ref, w_ref,
                    vp_ref, kc_ref, vc_ref, ks_ref, vs_ref, kw_ref, vw_ref, gm_ref, *rest, tm, d, prompt):
    x = x_ref[...].reshape(tm, d)
    u = _rmsnorm(x, g_ref[...]) * (1.0 + _rows2d(scale_ref)) + _rows2d(shift_ref)
    ub = u.astype(BF16)

    head = _dot(ub, w_ref[:, 0:_C_GM])

    def proj(c0, n):
        return head[:, c0:c0 + n] if c0 + n <= _C_GM else _dot(ub, w_ref[:, c0:c0 + n])

    vp = proj(_C_VP, POOL_W)
    vp_ref[...] = vp.reshape(vp_ref.shape)
    kv = []
    for n, o32 in enumerate((kc_ref, vc_ref, ks_ref, vs_ref, kw_ref, vw_ref)):
        v = proj(_C_KV + n * KVW, KVW)
        o32[...] = v.reshape(o32.shape)
        kv.append(v)
    gm_ref[...] = jax.nn.sigmoid(proj(_C_GM, 2 * d)).reshape(gm_ref.shape)
    gs = jax.nn.sigmoid(proj(_C_GN, LANES))

    if not prompt:
        q_ref, gs_ref = rest
        q_ref[...] = proj(_C_Q, QPAD).astype(BF16).reshape(q_ref.shape)
        gs_ref[...] = gs.reshape(gs_ref.shape)
    else:
        ksb_ref, kwb_ref, vst_ref, vwt_ref, qt_ref, gst_ref, pooled_ref, halo_ref = rest
        ksb_ref[...] = kv[2].astype(BF16).reshape(ksb_ref.shape)
        kwb_ref[...] = kv[4].astype(BF16).reshape(kwb_ref.shape)
        vst_ref[...] = kv[3].T.astype(BF16).reshape(vst_ref.shape)
        vwt_ref[...] = kv[5].T.astype(BF16).reshape(vwt_ref.shape)
        gst_ref[...] = gs.T.reshape(gst_ref.shape)
        for h in range(N_HEADS):
            qt_ref[0, h] = proj(_C_Q + h * LANES, LANES).T.astype(BF16)
        j = pl.program_id(1)

        @pl.when(j == 0)
        def _():
            halo_ref[...] = jnp.zeros_like(halo_ref)

        ext = jnp.concatenate([halo_ref[...], vp], axis=0)
        sums = _pool_window_sums(ext, tm)
        pos = j * tm + lax.broadcasted_iota(I32, (tm, 1), 0)
        wcol = 2 << (lax.broadcasted_iota(I32, (1, POOL_W), 1) // POOL_GW)
        cnt = jnp.minimum(pos + 1, wcol).astype(F32)
        pooled_ref[...] = (sums / cnt - vp).astype(BF16).reshape(pooled_ref.shape)
        halo_ref[...] = vp[tm - 16:tm]


def _in_proj(x3, shift, scale, g1, w2, *, tm, prompt):
    g, r, d = x3.shape
    nt = r // tm
    per_row = shift.ndim == 2

    def tok(width, dtype):
        return (jax.ShapeDtypeStruct((g, r, width), dtype),
                pl.BlockSpec((1, tm, width), lambda b, j: (b, j, 0)))

    def tok_t(rows, dtype):
        return (jax.ShapeDtypeStruct((g, rows, r), dtype),
                pl.BlockSpec((1, rows, tm), lambda b, j: (b, 0, j)))

    outs = [tok(POOL_W, F32)] + [tok(KVW, F32)] * 6 + [tok(2 * d, F32)]
    scratch = []
    if prompt:
        outs += [tok(KVW, BF16), tok(KVW, BF16), tok_t(KVW, BF16), tok_t(KVW, BF16)]
        outs.append((jax.ShapeDtypeStruct((g, N_HEADS, LANES, r), BF16),
                     pl.BlockSpec((1, N_HEADS, LANES, tm), lambda b, j: (b, 0, 0, j))))
        outs += [tok_t(LANES, F32), tok(POOL_W, BF16)]
        scratch.append(pltpu.VMEM((16, POOL_W), F32))
    else:
        outs += [tok(QPAD, BF16), tok(LANES, F32)]
    if per_row:
        mod_spec = lambda col: pl.BlockSpec((tm, d), lambda b, j, col=col: (b * nt + j, col))
    else:
        mod_spec = lambda col: pl.BlockSpec((1, 1, d), lambda b, j, col=col: (b, 0, col))
    kern = functools.partial(_in_proj_kernel, tm=tm, d=d, prompt=prompt)
    return pl.pallas_call(
        kern,
        out_shape=[o[0] for o in outs],
        grid=(g, nt),
        in_specs=[pl.BlockSpec((1, tm, d), lambda b, j: (b, j, 0)),
                  mod_spec(0), mod_spec(1),
                  pl.BlockSpec((1, d), lambda b, j: (0, 0)),
                  pl.BlockSpec(w2.shape, lambda b, j: (0, 0))],
        out_specs=[o[1] for o in outs],
        scratch_shapes=scratch,
        compiler_params=_cparams(("arbitrary", "arbitrary")),
        name="in_proj_prompt" if prompt else "in_proj_sample",
    )(x3, shift, scale, g1, w2)


def _compress_kernel(kc_ref, vc_ref, wk_ref, wv_ref, okc_ref, ovc_ref, sh_ref, *, nc):
    last = lax.broadcasted_iota(I32, (nc, 1), 0) == nc - 1
    for src, w_ref, dst in ((kc_ref, wk_ref, okc_ref), (vc_ref, wv_ref, ovc_ref)):
        head = jnp.zeros((nc, KVW), F32)
        tail = jnp.zeros((nc, KVW), F32)
        for r in range(CMP_STRIDE):
            rows = src[pl.ds(r, nc, stride=CMP_STRIDE), :]
            head = head + rows * w_ref[r:r + 1, :]
            tail = tail + rows * w_ref[CMP_STRIDE + r:CMP_STRIDE + r + 1, :]
        sh_ref[0:nc, :] = tail
        sh_ref[nc:nc + 8, :] = jnp.zeros((8, KVW), F32)
        out = jnp.where(last, 0.0, head + sh_ref[1:nc + 1, :])
        dst[...] = (out if dst is okc_ref else out.T).astype(BF16)


def _compress(kc, vc, wk, wv):
    b, s, _ = kc.shape
    nc = s // CMP_STRIDE
    big = pl.BlockSpec((None, s, KVW), lambda i: (i, 0, 0))
    wsp = pl.BlockSpec((CMP_BLOCK, KVW), lambda i: (0, 0))
    return pl.pallas_call(
        functools.partial(_compress_kernel, nc=nc),
        out_shape=[jax.ShapeDtypeStruct((b, nc, KVW), BF16), jax.ShapeDtypeStruct((b, KVW, nc), BF16)],
        grid=(b,),
        in_specs=[big, big, wsp, wsp],
        out_specs=[pl.BlockSpec((None, nc, KVW), lambda i: (i, 0, 0)),
                   pl.BlockSpec((None, KVW, nc), lambda i: (i, 0, 0))],
        scratch_shapes=[pltpu.VMEM((nc + 8, KVW), F32)],
        compiler_params=_cparams(("arbitrary",)),
        name="compress",
    )(kc, vc, wk, wv)


def _topk_mask(vals, blk_f, n_top, axis=1):
    sel = jnp.zeros(vals.shape, F32)
    big = float(vals.shape[axis])
    for _ in range(n_top):
        mx = jnp.max(vals, axis=axis, keepdims=True)
        first = jnp.min(jnp.where(vals == mx, blk_f, big), axis=axis, keepdims=True)
        hit = blk_f == first
        sel = jnp.where(hit, 1.0, sel)
        vals = jnp.where(hit, -jnp.inf, vals)
    return sel


def _topk_mask_by_rank(vals, blk, n_valid, n_top):
    rank = jnp.zeros(vals.shape, F32)
    for j in range(n_valid):
        vj = vals[:, j:j + 1]
        beats = (vj > vals) | ((vj == vals) & (blk > j))
        rank = rank + jnp.where(beats, 1.0, 0.0)
    return jnp.where(rank < float(n_top), 1.0, 0.0)


def _pos_features(pos):
    hi = (pos // SEL_BLOCK).astype(F32)[:, None]
    lo = (pos % SEL_BLOCK).astype(F32)[:, None]
    return jnp.concatenate([hi, lo, jnp.zeros((pos.shape[0], LANES - 2), F32)], axis=1).astype(BF16)


def _importance_matrix(nc, nsel):
    j = jnp.arange(nc)[:, None]
    s = jnp.arange(nsel)[None, :]
    r = SEL_BLOCK // CMP_STRIDE
    a = (j >= r * s) & (j <= r * s + r - 1)
    b = (j + 1 >= r * s) & (j + 1 <= r * s + r - 1)
    return a.astype(F32) + b.astype(F32)


def _nsa_prompt_kernel(qt_ref, gst_ref, kc_ref, vct_ref, ks_ref, vst_ref, kw_ref, vwt_ref,
                       cfeat_ref, wfeat_ref, qfeat_ref, slope_ref,
                       y_ref, qk_scr, m_scr, l_scr, acc_scr, o_scr, sel_scr, imp_scr, flag_scr, ids_scr,
                       *, seq, tk, wl):
    i = pl.program_id(1)
    q0 = i * Q_BLOCK
    nq = Q_BLOCK
    gq = GROUP * nq
    nc = kc_ref.shape[0]
    nsel = seq // SEL_BLOCK
    n_top = min(TOP_BLOCKS, nsel)
    blk_per_tile = tk // SEL_BLOCK
    qpos = q0 + lax.broadcasted_iota(I32, (1, nq), 1)
    gst = gst_ref[...]

    crow = lax.broadcasted_iota(I32, (nc, nq), 0)
    cend = crow * CMP_STRIDE + (CMP_BLOCK - 1)
    mask_c = qpos >= cend
    kc = jnp.concatenate([kc_ref[...], cfeat_ref[...]], axis=1)
    vct = vct_ref[...]
    blk = lax.broadcasted_iota(I32, (nsel, nq), 0)
    blk_f = blk.astype(F32)
    cur = qpos // SEL_BLOCK
    forced = (blk == 0) | (blk == cur) | (blk == cur - 1)
    visible = blk * SEL_BLOCK <= qpos
    ws = pl.multiple_of(jnp.maximum(q0 - WINDOW, 0), Q_BLOCK)
    wpos = ws + lax.broadcasted_iota(I32, (wl, nq), 0)
    valid_w = lax.bitcast_convert_type(qpos - wpos, jnp.uint32) < WINDOW
    n_tiles = (q0 + nq + tk - 1) // tk
    half_rows = lax.broadcasted_iota(I32, (KVW, nq), 0) // HEAD_DIM
    tile_pos = lax.broadcasted_iota(I32, (SEL_BLOCK, nq), 0)

    def lanes4(x):
        return jnp.concatenate([x] * GROUP, axis=1)

    def gate_row(branch, k):
        r0 = branch * N_HEADS + k * GROUP
        return jnp.concatenate([gst[r0 + g:r0 + g + 1] for g in range(GROUP)], axis=1)

    mask_c4 = lanes4(mask_c)
    valid_w4 = lanes4(valid_w)
    kwt = jnp.concatenate([kw_ref[pl.ds(ws, wl), :], wfeat_ref[...]], axis=1)
    vwtt = vwt_ref[:, pl.ds(ws, wl)]

    for k in range(N_KV):
        for g in range(GROUP):
            qk_scr[k, 0:LANES, g * nq:(g + 1) * nq] = qt_ref[k * GROUP + g]
        qk_scr[k, LANES:2 * LANES, :] = qfeat_ref[k]
        qk = qk_scr[k]

        s = jnp.where(mask_c4, _dot(kc, qk), NEG)
        e = jnp.where(mask_c4, jnp.exp(s - jnp.max(s, axis=0, keepdims=True)), 0.0)
        l = jnp.sum(e, axis=0, keepdims=True)
        p = e * jnp.where(l > 0.0, 1.0 / l, 0.0)
        o_c = _dot(vct, p.astype(BF16))
        psum = p[:, 0:nq]
        for g in range(1, GROUP):
            psum = psum + p[:, g * nq:(g + 1) * nq]

        a = psum + jnp.where(crow == 0, 0.0, pltpu.roll(psum, 1, 0))
        a = a + pltpu.roll(a, nc - 1, 0)
        imp_scr[...] = a + pltpu.roll(a, nc - 2, 0)
        imp = imp_scr[pl.ds(0, nsel, stride=nc // nsel), :]
        vals = jnp.where(visible, jnp.where(forced, FORCE_SCORE, imp), NEG)
        sel = jnp.where(visible, _topk_mask(vals, blk_f, n_top, axis=0), 0.0)
        sel_scr[k] = jnp.where(sel > 0.5, 0.0, NEG)
        blk_any = jnp.max(sel, axis=1, keepdims=True)
        for t in range(seq // tk):
            hit = (jnp.max(blk_any[t * blk_per_tile:(t + 1) * blk_per_tile, :]) > 0.5).astype(I32)
            flag_scr[t] = hit if k == 0 else flag_scr[t] | hit

        s = jnp.where(valid_w4, _dot(kwt, qk), NEG)
        e = jnp.exp(s - jnp.max(s, axis=0, keepdims=True))
        p = e / jnp.sum(e, axis=0, keepdims=True)
        o_w = _dot(vwtt, p.astype(BF16))
        o_scr[k] = gate_row(0, k) * o_c + gate_row(2, k) * o_w

    m_scr[...] = jnp.full(m_scr.shape, NEG, F32)
    l_scr[...] = jnp.zeros(l_scr.shape, F32)
    acc_scr[...] = jnp.zeros(acc_scr.shape, F32)

    n_act = jnp.int32(0)
    for t in range(seq // tk):
        ids_scr[n_act] = t
        n_act = n_act + jnp.where((flag_scr[t] > 0) & (t < n_tiles), 1, 0)
    lane0 = lax.broadcasted_iota(I32, (tk, LANES), 1) == 0
    feat = wfeat_ref[0:tk, :]

    def sel_tiles(tiles):
        ta = tiles[0]
        starts = [pl.multiple_of(t * tk, tk) for t in tiles]
        keys = [jnp.concatenate([ks_ref[pl.ds(starts[0], tk), :], feat], axis=1)]
        for t, k0 in zip(tiles[1:], starts[1:]):
            shifted = jnp.where(lane0, feat.astype(F32) + ((t - ta) * blk_per_tile).astype(F32), feat.astype(F32))
            keys.append(jnp.concatenate([ks_ref[pl.ds(k0, tk), :], shifted.astype(BF16)], axis=1))
        kt = jnp.concatenate(keys, axis=0)
        vtt = jnp.concatenate([vst_ref[:, pl.ds(k0, tk)] for k0 in starts], axis=1)
        base = (starts[0] - q0).astype(F32)
        scores = _dot(kt, jnp.concatenate([qk_scr[k] for k in range(N_KV)], axis=1))
        probs, alphas = [], []
        for k in range(N_KV):
            neg = []
            for t, k0 in zip(tiles, starts):
                for j in range(blk_per_tile):
                    row = sel_scr[k, pl.ds(t * blk_per_tile + j, 1), :]
                    causal = qpos >= k0 + j * SEL_BLOCK + tile_pos
                    neg.append(jnp.where(causal, jnp.broadcast_to(row, (SEL_BLOCK, nq)), NEG))
            neg = lanes4(jnp.concatenate(neg, axis=0))
            off = slope_ref[k] * base
            s = scores[:, k * gq:(k + 1) * gq] + neg
            m_old = m_scr[k]
            m_new = jnp.maximum(m_old, jnp.max(s, axis=0, keepdims=True) + off)
            alpha = jnp.exp(m_old - m_new)
            p = jnp.exp(s - (m_new - off))
            l_scr[k] = alpha * l_scr[k] + jnp.sum(p, axis=0, keepdims=True)
            m_scr[k] = m_new
            probs.append(p.astype(BF16))
            alphas.append(alpha)
        pv = _dot(vtt, jnp.concatenate(probs, axis=1))
        for k in range(N_KV):
            acc_scr[k] = acc_scr[k] * alphas[k] + pv[:, k * gq:(k + 1) * gq]

    def sel_pair(i, carry):
        sel_tiles([ids_scr[2 * i], ids_scr[2 * i + 1]])
        return carry

    lax.fori_loop(0, n_act // 2, sel_pair, 0)
    pl.when(n_act % 2 == 1)(lambda: sel_tiles([ids_scr[n_act - 1]]))

    for k in range(N_KV):
        o = o_scr[k] + gate_row(1, k) * (acc_scr[k] / l_scr[k])
        for g in range(GROUP):
            h = k * GROUP + g
            oh = jnp.where(half_rows == k, o[:, g * nq:(g + 1) * nq], 0.0)
            y_ref[:, h * LANES:(h + 1) * LANES] = oh.T.astype(BF16)


def _nsa_prompt(qt, gst, kcmp, vcmpt, ksb, vst, kwb, vwt):
    b, _, _, s = qt.shape
    nq = Q_BLOCK
    gq = GROUP * nq
    nc = kcmp.shape[1]
    nsel = s // SEL_BLOCK
    tk = 256
    wl = WINDOW + Q_BLOCK
    assert s % tk == 0 and s >= wl
    assert s // SEL_BLOCK <= 2 * LANES, "position // 64 must stay exact in bf16"
    cfeat = _pos_features(jnp.arange(nc) * CMP_STRIDE + (CMP_BLOCK - 1))
    wfeat = _pos_features(jnp.arange(wl))
    slope_rows = jnp.repeat(jnp.asarray(SLOPES, F32).reshape(N_KV, 1, GROUP), nq, axis=2)
    qfeat = jnp.concatenate([slope_rows * SEL_BLOCK, slope_rows, jnp.zeros((N_KV, LANES - 2, gq), F32)],
                            axis=1).astype(BF16)
    rows = lambda r: pl.BlockSpec((None, r, KVW), lambda bi, i: (bi, 0, 0))
    cols = lambda c: pl.BlockSpec((None, KVW, c), lambda bi, i: (bi, 0, 0))
    const = lambda a: pl.BlockSpec(a.shape, lambda bi, i: (0,) * a.ndim)
    return pl.pallas_call(
        functools.partial(_nsa_prompt_kernel, seq=s, tk=tk, wl=wl),
        out_shape=jax.ShapeDtypeStruct((b, s, QPAD), BF16),
        grid=(b, s // nq),
        in_specs=[pl.BlockSpec((None, N_HEADS, LANES, nq), lambda bi, i: (bi, 0, 0, i)),
                  pl.BlockSpec((None, LANES, nq), lambda bi, i: (bi, 0, i)),
                  rows(nc), cols(nc), rows(s), cols(s), rows(s), cols(s),
                  const(cfeat), const(wfeat), const(qfeat), const(slope_rows)],
        out_specs=pl.BlockSpec((None, nq, QPAD), lambda bi, i: (bi, i, 0)),
        scratch_shapes=[pltpu.VMEM((N_KV, 2 * LANES, gq), BF16),
                        pltpu.VMEM((N_KV, 1, gq), F32),
                        pltpu.VMEM((N_KV, 1, gq), F32),
                        pltpu.VMEM((N_KV, KVW, gq), F32),
                        pltpu.VMEM((N_KV, KVW, gq), F32),
                        pltpu.VMEM((N_KV, nsel, nq), F32),
                        pltpu.VMEM((nc, nq), F32),
                        pltpu.SMEM((s // tk,), I32),
                        pltpu.SMEM((s // tk,), I32)],
        compiler_params=_cparams(("arbitrary", "arbitrary")),
        name="nsa_prompt",
    )(qt, gst, kcmp, vcmpt, ksb, vst, kwb, vwt, cfeat, wfeat, qfeat, slope_rows)


def _nsa_sample_kernel(pt_ref, q_ref, gate_ref, kcn_ref, vcn_ref, ksn_ref, vsn_ref, kwn_ref, vwn_ref,
                       vpn_ref, skw_ref, svw_ref, spool_ref, wk_ref, wv_ref, imat_ref, emat_ref,
                       ckc_ref, cvc_ref, cks_ref, cvs_ref,
                       o_ref, pooled_ref, kwo_ref, vwo_ref, buf, buft, win_scr, tail_scr, vext_scr, sem,
                       *, sb, ns, past, n_pages, page, n_seq, ncv, ncp, nks, wls, nselp, n_sel):
    step = pl.program_id(0)
    nrow = ns * N_HEADS
    par = step % 2

    def copies(n, side, r):
        out = []
        for p in range(n_pages):
            pg = pt_ref[n * n_pages + p]
            for c, cref in enumerate((ckc_ref, cvc_ref, cks_ref, cvs_ref)):
                out.append(pltpu.make_async_copy(cref.at[pg], buft.at[side, r, c, :, pl.ds(p * page, page)],
                                                 sem.at[side, r]))
        return out

    @pl.when(step == 0)
    def _():
        buf[:, :, past:, :] = jnp.zeros((sb, 2, buf.shape[2] - past, KVW), F32)
        tail_scr[...] = jnp.zeros_like(tail_scr)
        vext_scr[...] = jnp.zeros_like(vext_scr)
        for r in range(sb):
            for cp in copies(r, 0, r):
                cp.start()

    @pl.when(step + 1 < pl.num_programs(0))
    def _():
        for r in range(sb):
            for cp in copies((step + 1) * sb + r, 1 - par, r):
                cp.start()

    for r in range(sb):
        for cp in copies(step * sb + r, par, r):
            cp.wait()

    def new_rows_t(ref, r4, r):
        tail_scr[r, 0:ns, :] = ref[pl.ds(r4, ns), :]
        return tail_scr[r].T

    row = lax.broadcasted_iota(I32, (nrow, 1), 0)
    hrow = row % N_HEADS
    qpos = past + row // N_HEADS
    slope = jnp.exp2(-8.0 * (hrow.astype(F32) + 1.0) / N_HEADS)
    kvrow = hrow // GROUP
    lane = lax.broadcasted_iota(I32, (1, LANES), 1)
    half = (lane // HEAD_DIM) == kvrow
    grow = (row // N_HEADS) * N_KV + kvrow
    row8 = lax.broadcasted_iota(I32, (ns * N_KV, 1), 0)
    qpos8 = past + lax.broadcasted_iota(I32, (ns * N_KV, 1), 0) // N_KV
    blk = lax.broadcasted_iota(I32, (1, nselp), 1)
    blk_f = blk.astype(F32)
    cur = qpos8 // SEL_BLOCK
    forced = (blk == 0) | (blk == cur) | (blk == cur - 1)
    visible = (blk * SEL_BLOCK <= qpos8)
    inrange = blk < n_sel
    cend = lax.broadcasted_iota(I32, (1, ncp), 1) * CMP_STRIDE + (CMP_BLOCK - 1)
    mask_c = qpos >= cend
    bias_c = slope * (cend - qpos).astype(F32)
    kpos = lax.broadcasted_iota(I32, (1, nks), 1)
    causal_s = qpos >= kpos
    bias_s = slope * (kpos - qpos).astype(F32)
    wbuf = wls[0]
    wpos = past - wbuf + lax.broadcasted_iota(I32, (1, wls[1]), 1)
    dw = qpos - wpos
    valid_w = lax.bitcast_convert_type(dw, jnp.uint32) < WINDOW
    bias_w = slope * (wpos - qpos).astype(F32)
    prow = lax.broadcasted_iota(I32, (vext_scr.shape[1], 1), 0)
    wcol = 2 << (lax.broadcasted_iota(I32, (1, POOL_W), 1) // POOL_GW)
    n_top = min(TOP_BLOCKS, n_sel)

    def softmax_rows(s, mask):
        s = jnp.where(mask, s, NEG)
        mx = jnp.max(s, axis=1, keepdims=True)
        e = jnp.where(mask, jnp.exp(s - mx), 0.0)
        l = jnp.sum(e, axis=1, keepdims=True)
        return e * jnp.where(l > 0.0, 1.0 / l, 0.0)

    def seq_body(r):
        r4 = r * ns
        for c, new_ref in enumerate((kcn_ref, vcn_ref)):
            for p in range(n_pages):
                buf[r, c, p * page:(p + 1) * page, :] = buft[par, r, c, :, p * page:(p + 1) * page].T
            buf[r, c, past:past + ns, :] = new_ref[pl.ds(r4, ns), :]
        for c, new_ref in ((2, ksn_ref), (3, vsn_ref)):
            buft[par, r, c, :, past:past + LANES] = new_rows_t(new_ref, r4, r)

        qall = q_ref[pl.ds(r * nrow, nrow), :]
        gates = gate_ref[pl.ds(r * nrow, nrow), :]

        cmp = []
        for c, w_ref in ((0, wk_ref), (1, wv_ref)):
            span = CMP_STRIDE * ncv
            lo = buf[r, c, 0:span, :].reshape(ncv, CMP_STRIDE, KVW) * w_ref[0:CMP_STRIDE, :][None]
            hi = (buf[r, c, CMP_STRIDE:CMP_STRIDE + span, :].reshape(ncv, CMP_STRIDE, KVW)
                  * w_ref[CMP_STRIDE:CMP_BLOCK, :][None])
            acc = jnp.sum(lo + hi, axis=1)
            cmp.append(jnp.concatenate([acc, jnp.zeros((ncp - ncv, KVW), F32)], axis=0).astype(BF16))
        p_c = softmax_rows(_dot_nt(qall, cmp[0]) + bias_c, mask_c)
        o_c = _dot(p_c.astype(BF16), cmp[1])

        psum = jnp.zeros((ns * N_KV, ncp), F32)
        for i in range(ns * N_KV):
            r0 = (i // N_KV) * N_HEADS + (i % N_KV) * GROUP
            psum = jnp.where(row8 == i, jnp.sum(p_c[r0:r0 + GROUP], axis=0, keepdims=True), psum)
        imp = _dot_exact(psum, imat_ref[...])
        vals = jnp.where(inrange, jnp.where(visible, jnp.where(forced, FORCE_SCORE, imp), NEG), -jnp.inf)
        sel8 = _topk_mask_by_rank(vals, blk, n_sel, n_top)
        sel_rows = jnp.zeros((nrow, nselp), F32)
        for i in range(ns * N_KV):
            sel_rows = jnp.where(grow == i, sel8[i:i + 1], sel_rows)
        chosen = _dot(sel_rows.astype(BF16), emat_ref[...])

        kst = buft[par, r, 2].astype(BF16)
        vst = buft[par, r, 3].astype(BF16)
        p_s = softmax_rows(_dot(qall, kst) + bias_s, causal_s & (chosen > 0.5))
        o_s = _dot_nt(p_s.astype(BF16), vst)

        outs_w = []
        for state_ref, new_ref, next_ref in ((skw_ref, kwn_ref, kwo_ref), (svw_ref, vwn_ref, vwo_ref)):
            win_scr[r, :, 0:wbuf] = state_ref[r]
            win_scr[r, :, wbuf:wbuf + LANES] = new_rows_t(new_ref, r4, r)
            outs_w.append(win_scr[r].astype(BF16))
            next_ref[r] = win_scr[r, :, ns:ns + wbuf]
        p_w = softmax_rows(_dot(qall, outs_w[0]) + bias_w, valid_w)
        o_w = _dot_nt(p_w.astype(BF16), outs_w[1])

        o = gates[:, 0:1] * o_c + gates[:, 1:2] * o_s + gates[:, 2:3] * o_w
        o_ref[pl.ds(r * nrow, nrow), :] = jnp.where(half, o, 0.0).astype(BF16)

        vext_scr[r, 0:POOL_BUF, :] = spool_ref[r]
        vext_scr[r, POOL_BUF:POOL_BUF + ns, :] = vpn_ref[pl.ds(r4, ns), :]
        ext = vext_scr[r]
        for t in range(ns):
            hi = POOL_BUF + t
            inwin = (prow <= hi) & (prow > hi - wcol)
            ssum = jnp.sum(jnp.where(inwin, ext, 0.0), axis=0, keepdims=True)
            cnt = jnp.minimum(past + t + 1, wcol).astype(F32)
            pooled_ref[pl.ds(r4 + t, 1), :] = ssum / cnt - ext[hi:hi + 1, :]

    for r in range(sb):
        seq_body(r)


def _nsa_sample(page_table, q_rows, gate_rows, new6, vp_new, state_kwt, state_vwt, state_pool, wk, wv, caches):
    n_seq, n_pages = page_table.shape
    page = caches[0].shape[2]
    past = n_pages * page
    ns = vp_new.shape[0] // n_seq
    wbuf = state_kwt.shape[2]
    sb = 2
    nrow = ns * N_HEADS
    assert ns <= SEL_BLOCK and page == LANES
    t_pad = -(-(past + ns) // SEL_BLOCK) * SEL_BLOCK
    n_cmp = t_pad // CMP_STRIDE - 1
    ncv = -(-n_cmp // 8) * 8
    ncp = -(-ncv // LANES) * LANES
    nks = past + LANES
    n_sel = t_pad // SEL_BLOCK
    nselp = LANES
    assert n_sel <= nselp
    wlp = wbuf + LANES
    buf_rows = -(-(CMP_STRIDE * ncv + CMP_STRIDE) // 8) * 8
    imat = _importance_matrix(ncp, nselp)
    emat = (jnp.arange(nselp)[:, None] == (jnp.arange(nks)[None, :] // SEL_BLOCK)).astype(BF16)

    seqblk = lambda rows, w: pl.BlockSpec((sb * rows, w), lambda i, pt: (i, 0))
    const = lambda a: pl.BlockSpec(a.shape, lambda i, pt: (0,) * a.ndim)
    kern = functools.partial(
        _nsa_sample_kernel, sb=sb, ns=ns, past=past, n_pages=n_pages, page=page, n_seq=n_seq,
        ncv=ncv, ncp=ncp, nks=nks, wls=(wbuf, wlp), nselp=nselp, n_sel=n_sel)
    grid_spec = pltpu.PrefetchScalarGridSpec(
        num_scalar_prefetch=1,
        grid=(n_seq // sb,),
        in_specs=[seqblk(nrow, LANES), seqblk(nrow, LANES)] + [seqblk(ns, KVW)] * 6 + [seqblk(ns, POOL_W)]
        + [pl.BlockSpec((sb, KVW, wbuf), lambda i, pt: (i, 0, 0))] * 2
        + [pl.BlockSpec((sb, POOL_BUF, POOL_W), lambda i, pt: (i, 0, 0))]
        + [const(wk), const(wv), const(imat), const(emat)]
        + [pl.BlockSpec(memory_space=pl.ANY)] * 4,
        out_specs=[seqblk(nrow, LANES), seqblk(ns, POOL_W)]
        + [pl.BlockSpec((sb, KVW, wbuf), lambda i, pt: (i, 0, 0))] * 2,
        scratch_shapes=[pltpu.VMEM((sb, 2, buf_rows, KVW), F32),
                        pltpu.VMEM((2, sb, 4, KVW, nks), F32),
                        pltpu.VMEM((sb, KVW, wlp), F32),
                        pltpu.VMEM((sb, LANES, KVW), F32),
                        pltpu.VMEM((sb, 24, POOL_W), F32),
                        pltpu.SemaphoreType.DMA((2, sb))],
    )
    return pl.pallas_call(
        kern,
        out_shape=[jax.ShapeDtypeStruct((n_seq * nrow, LANES), BF16),
                   jax.ShapeDtypeStruct((n_seq * ns, POOL_W), F32)]
        + [jax.ShapeDtypeStruct((n_seq, KVW, wbuf), F32)] * 2,
        grid_spec=grid_spec,
        compiler_params=_cparams(("arbitrary",)),
        name="nsa_sample",
    )(page_table.reshape(-1), q_rows, gate_rows, *new6, vp_new, state_kwt, state_vwt, state_pool, wk, wv,
      imat, emat, *caches)


def _route(logits_t, bias_col, tm):
    sc = jax.nn.sigmoid(logits_t)
    biased = sc + bias_col
    epg = EXPERTS_PER_GROUP
    row8 = lax.broadcasted_iota(I32, (epg, tm), 0).astype(F32)
    ninf = -jnp.inf
    grp = jnp.zeros((N_EGROUPS, tm), F32)
    for g in range(N_EGROUPS):
        bg = biased[g * epg:(g + 1) * epg]
        m1 = jnp.max(bg, axis=0, keepdims=True)
        first = jnp.min(jnp.where(bg == m1, row8, float(epg)), axis=0, keepdims=True)
        m2 = jnp.max(jnp.where(row8 == first, ninf, bg), axis=0, keepdims=True)
        grp = jnp.where(row8 == float(g), m1 + m2, grp)
    keep = jnp.zeros((N_EGROUPS, tm), F32)
    vals = grp
    for _ in range(TOPK_GROUPS):
        mx = jnp.max(vals, axis=0, keepdims=True)
        first = jnp.min(jnp.where(vals == mx, row8, float(N_EGROUPS)), axis=0, keepdims=True)
        hit = row8 == first
        keep = jnp.where(hit, 1.0, keep)
        vals = jnp.where(hit, ninf, vals)
    masked = jnp.concatenate(
        [jnp.where(keep[g:g + 1] > 0.5, biased[g * epg:(g + 1) * epg], NEG) for g in range(N_EGROUPS)], axis=0)
    rowe = lax.broadcasted_iota(I32, (N_EXPERTS, tm), 0).astype(F32)
    chosen = jnp.zeros((N_EXPERTS, tm), F32)
    vals = masked
    picks = []
    for _ in range(TOP_K):
        mx = jnp.max(vals, axis=0, keepdims=True)
        first = jnp.min(jnp.where(vals == mx, rowe, float(N_EXPERTS)), axis=0, keepdims=True)
        hit = rowe == first
        chosen = jnp.where(hit, sc, chosen)
        vals = jnp.where(hit, ninf, vals)
        picks.append((hit, first))
    return ROUTED_SCALE * chosen / jnp.sum(chosen, axis=0, keepdims=True), picks


def _pack_bf16_pairs(x):
    c = x.shape[1] // 2
    bits = lambda v: lax.bitcast_convert_type(v.astype(BF16).astype(F32), jnp.uint32)
    return (bits(x[:, :c]) >> 16) | (bits(x[:, c:]) & jnp.uint32(0xFFFF0000))


def _unpack_bf16_pairs(w):
    lo = lax.bitcast_convert_type(w << 16, F32)
    hi = lax.bitcast_convert_type(w & jnp.uint32(0xFFFF0000), F32)
    return jnp.concatenate([lo, hi], axis=1)


def _finish_kernel(x_ref, pooled_ref, y_ref, gm_ref, g1_ref, shift_ref, scale_ref,
                   wlin_ref, pscale_ref, wpo_ref, wno_ref, wo_ref, n2_ref, wr_ref, br_ref,
                   *rest, tm, d, sparse):
    if sparse:
        tri_ref, x1_ref, u2_ref, up_ref, eid_ref, gk_ref, rank_ref, cnt_ref, carry_scr = rest
    else:
        x1_ref, u2_ref, gates_ref = rest
    x = x_ref[...].reshape(tm, d)
    pooled = pooled_ref[...].reshape(tm, POOL_W).astype(BF16)
    y_pool = _dot(pooled, wlin_ref[...]) * pscale_ref[...]
    a = _dot(y_pool.astype(BF16), wpo_ref[...])
    b = _dot(y_ref[...].reshape(tm, QPAD), wno_ref[...])
    gm = gm_ref[...].reshape(tm, 2 * d)
    merged = gm[:, :d] * a + gm[:, d:] * b
    x1 = x + _rows2d(g1_ref) * _dot(merged.astype(BF16), wo_ref[...])
    x1_ref[...] = x1.reshape(x1_ref.shape)
    u2 = _rmsnorm(x1, n2_ref[...]) * (1.0 + _rows2d(scale_ref)) + _rows2d(shift_ref)
    u2b = u2.astype(BF16)
    u2_ref[...] = u2b.reshape(u2_ref.shape)
    logits_t = _dot_nt(wr_ref[...], u2b)
    gates_t, picks = _route(logits_t[:N_EXPERTS], br_ref[...], tm)
    if not sparse:
        gates_t = jnp.concatenate([gates_t, jnp.zeros((LANES - N_EXPERTS, tm), F32)], axis=0)
        gates_ref[...] = gates_t.T.reshape(gates_ref.shape)
        return

    @pl.when((pl.program_id(0) == 0) & (pl.program_id(1) == 0))
    def _():
        carry_scr[...] = jnp.zeros_like(carry_scr)

    packed = _pack_bf16_pairs(u2)
    for s in range(up_ref.shape[0]):
        up_ref[s] = packed[:, s * SC_ROW_WORDS:(s + 1) * SC_ROW_WORDS]
    hit_all = picks[0][0]
    for hit, _ in picks[1:]:
        hit_all = hit_all | hit
    hits = jnp.where(hit_all, 1.0, 0.0).astype(BF16)
    before = _dot(hits, tri_ref[...]) + jnp.concatenate([carry_scr[...]] * (tm // LANES), axis=1)
    eids, gks, ranks = [], [], []
    for hit, first in picks:
        eids.append(first)
        gks.append(jnp.sum(jnp.where(hit, gates_t, 0.0), axis=0, keepdims=True))
        ranks.append(jnp.sum(jnp.where(hit, before, 0.0), axis=0, keepdims=True))
    pick_row = lax.broadcasted_iota(I32, (TOP_K, tm), 0)

    def stack(rows):
        out = jnp.zeros((TOP_K, tm), F32)
        for r, v in enumerate(rows):
            out = jnp.where(pick_row == r, v, out)
        return out

    eid_ref[...] = stack(eids).astype(I32).reshape(eid_ref.shape)
    gk_ref[...] = jnp.concatenate([stack(gks), jnp.zeros((LANES - TOP_K, tm), F32)], axis=0).T.reshape(gk_ref.shape)
    rank_ref[...] = stack(ranks).astype(I32).reshape(rank_ref.shape)
    carry_scr[...] += _dot(hits, jnp.ones((tm, LANES), BF16))
    cnt_ref[...] = carry_scr[...]


def _finish(x3, pooled, ynsa, gm, mods, wts, *, tm, sparse):
    g, r, d = x3.shape
    nt = r // tm
    g1, shift2, scale2 = mods
    per_row = g1.ndim == 2
    tok = lambda w: pl.BlockSpec((1, tm, w), lambda b, j: (b, j, 0))
    tok_t = lambda rows: pl.BlockSpec((1, rows, tm), lambda b, j: (b, 0, j))
    if per_row:
        mod_spec = lambda col: pl.BlockSpec((tm, d), lambda b, j, col=col: (b * nt + j, col))
    else:
        mod_spec = lambda col: pl.BlockSpec((1, 1, d), lambda b, j, col=col: (b, 0, col))
    const = lambda a: pl.BlockSpec(a.shape, lambda b, j: (0,) * a.ndim)
    out_shape = [jax.ShapeDtypeStruct((g, r, d), F32), jax.ShapeDtypeStruct((g, r, d), BF16)]
    out_specs = [tok(d), tok(d)]
    scratch = []
    if sparse:
        tri = (jnp.arange(tm)[:, None] < jnp.arange(tm)[None, :]).astype(BF16)
        wts = tuple(wts) + (tri,)
        split = d // 2 // SC_ROW_WORDS
        out_shape += [jax.ShapeDtypeStruct((split, g * r, SC_ROW_WORDS), jnp.uint32),
                      jax.ShapeDtypeStruct((g, TOP_K, r), I32), jax.ShapeDtypeStruct((g, r, LANES), F32),
                      jax.ShapeDtypeStruct((g, TOP_K, r), I32), jax.ShapeDtypeStruct((N_EXPERTS, LANES), F32)]
        out_specs += [pl.BlockSpec((split, tm, SC_ROW_WORDS), lambda b, j: (0, b * nt + j, 0)),
                      tok_t(TOP_K), tok(LANES), tok_t(TOP_K),
                      pl.BlockSpec((N_EXPERTS, LANES), lambda b, j: (0, 0))]
        scratch.append(pltpu.VMEM((N_EXPERTS, LANES), F32))
    else:
        out_shape.append(jax.ShapeDtypeStruct((g, r, LANES), F32))
        out_specs.append(tok(LANES))
    return pl.pallas_call(
        functools.partial(_finish_kernel, tm=tm, d=d, sparse=sparse),
        out_shape=out_shape,
        grid=(g, nt),
        in_specs=[tok(d), tok(POOL_W), tok(QPAD), tok(2 * d), mod_spec(2), mod_spec(3), mod_spec(4)]
        + [const(w) for w in wts],
        out_specs=out_specs,
        scratch_shapes=scratch,
        compiler_params=_cparams(("arbitrary", "arbitrary")),
        name="finish_route" if sparse else "finish",
    )(x3, pooled, ynsa, gm, g1, shift2, scale2, *wts)


def _moe_kernel(u_ref, gates_ref, x1_ref, g2_ref, nf_ref, wg_ref, wu_ref, wd_ref, sg_ref, su_ref, sd_ref,
                y_ref, acc_ref, *, tm, d, eps):
    e = pl.program_id(2)
    u = u_ref[...].reshape(tm, d)

    @pl.when(e == 0)
    def _():
        hs = _silu(_dot(u, sg_ref[...])) * _dot(u, su_ref[...])
        acc_ref[...] = _dot(hs.astype(BF16), sd_ref[...])

    gates = gates_ref[...].reshape(tm, LANES)
    lane = lax.broadcasted_iota(I32, (1, LANES), 1)
    hidden = []
    for j in range(eps):
        h = _silu(_dot(u, wg_ref[j].astype(BF16))) * _dot(u, wu_ref[j].astype(BF16))
        gate = jnp.sum(jnp.where(lane == e * eps + j, gates, 0.0), axis=1, keepdims=True)
        hidden.append((h * gate).astype(BF16))
    f = wd_ref.shape[1]
    acc_ref[...] += _dot(jnp.concatenate(hidden, axis=1), wd_ref[...].reshape(eps * f, d).astype(BF16))

    @pl.when(e == pl.num_programs(2) - 1)
    def _():
        x2 = x1_ref[...].reshape(tm, d) + _rows2d(g2_ref) * acc_ref[...]
        y_ref[...] = _rmsnorm(x2, nf_ref[...]).reshape(y_ref.shape)


def _moe(u2, gates, x1, g2, normf, w_gate, w_up, w_down, sg, su, sd, *, tm):
    g, r, d = x1.shape
    nt = r // tm
    ne, _, f = w_gate.shape
    per_row = g2.ndim == 2
    tok = lambda w: pl.BlockSpec((1, tm, w), lambda b, j, e: (b, j, 0))
    if per_row:
        g2_spec = pl.BlockSpec((tm, d), lambda b, j, e: (b * nt + j, 5))
    else:
        g2_spec = pl.BlockSpec((1, 1, d), lambda b, j, e: (b, 0, 5))
    once = pl.Buffered(buffer_count=1)
    const = lambda a: pl.BlockSpec(a.shape, lambda b, j, e: (0,) * a.ndim, pipeline_mode=once)
    eps = 4
    return pl.pallas_call(
        functools.partial(_moe_kernel, tm=tm, d=d, eps=eps),
        out_shape=jax.ShapeDtypeStruct((g, r, d), F32),
        grid=(g, nt, ne // eps),
        in_specs=[tok(d), tok(LANES),
                  pl.BlockSpec((1, tm, d), lambda b, j, e: (b, j, 0), pipeline_mode=once),
                  g2_spec, const(normf),
                  pl.BlockSpec((eps, d, f), lambda b, j, e: (e, 0, 0)),
                  pl.BlockSpec((eps, d, f), lambda b, j, e: (e, 0, 0)),
                  pl.BlockSpec((eps, f, d), lambda b, j, e: (e, 0, 0)),
                  const(sg), const(su), const(sd)],
        out_specs=tok(d),
        scratch_shapes=[pltpu.VMEM((tm, d), F32)],
        compiler_params=_cparams(("arbitrary", "arbitrary", "arbitrary")),
        name="moe",
    )(u2, gates, x1, g2, normf, w_gate, w_up, w_down, sg, su, sd)


SC_WINDOW = 128
SC_ROW_WORDS = 256
MOE_ROWS = 512


def _sc_mesh():
    return plsc.VectorSubcoreMesh(core_axis_name="c", subcore_axis_name="s")


def _sc_scatter_rows(src, dst_idx, n_dst):
    n, w = src.shape
    nk = dst_idx.shape[0]

    @pl.kernel(out_type=jax.ShapeDtypeStruct((n_dst, w), src.dtype), mesh=_sc_mesh(), scratch_types=[])
    def scatter(src_hbm, idx_hbm, dst_hbm):
        def body(rows_vmem, idx_vmem):
            pltpu.sync_copy(rows_vmem, dst_hbm.at[idx_vmem.at[0]])

        pltpu.emit_pipeline(
            body,
            grid=(nk, n // SC_WINDOW),
            in_specs=[pl.BlockSpec((SC_WINDOW, w), index_map=lambda k, i: (i, 0)),
                      pl.BlockSpec((1, SC_WINDOW), index_map=lambda k, i: (k, i))],
            out_specs=[],
            core_axis_name=("c", "s"),
            dimension_semantics=(pltpu.PARALLEL, pltpu.PARALLEL),
        )(src_hbm, idx_hbm)

    return scatter(src, dst_idx)


def _sc_gather_rows(src, idx):
    n, w = idx.shape[0], src.shape[1]

    @pl.kernel(out_type=jax.ShapeDtypeStruct((n, w), src.dtype), mesh=_sc_mesh(), scratch_types=[])
    def gather(src_hbm, idx_hbm, out_hbm):
        def body(idx_vmem, out_vmem):
            pltpu.sync_copy(src_hbm.at[idx_vmem.at[0]], out_vmem)

        pltpu.emit_pipeline(
            body,
            grid=(n // SC_WINDOW,),
            in_specs=[pl.BlockSpec((1, SC_WINDOW), index_map=lambda i: (0, i))],
            out_specs=[pl.BlockSpec((SC_WINDOW, w), index_map=lambda i: (i, 0))],
            core_axis_name=("c", "s"),
            dimension_semantics=(pltpu.PARALLEL,),
        )(idx_hbm, out_hbm)

    return gather(src, idx.reshape(1, n))


MOE_BLOCKS_PER_STEP = 4


def _expert_rows_kernel(te_ref, nt_ref, x_ref, *refs):
    y_ref = refs[-1]
    split = x_ref.shape[0]
    for j in range(MOE_BLOCKS_PER_STEP):
        wg_ref, wu_ref, wd_ref = refs[3 * j:3 * j + 3]
        rows = slice(j * MOE_ROWS, (j + 1) * MOE_ROWS)

        @pl.when(pl.program_id(0) * MOE_BLOCKS_PER_STEP + j < nt_ref[0])
        def _():
            x = _unpack_bf16_pairs(jnp.concatenate([x_ref[s, rows] for s in range(split)], axis=1)).astype(BF16)
            h = _silu(_dot(x, wg_ref[...].astype(BF16))) * _dot(x, wu_ref[...].astype(BF16))
            y = _pack_bf16_pairs(_dot(h.astype(BF16), wd_ref[...].astype(BF16)))
            for s in range(split):
                y_ref[s, rows] = y[:, s * SC_ROW_WORDS:(s + 1) * SC_ROW_WORDS]


def _expert_rows(tile_expert, n_tiles, x_sorted, w_gate, w_up, w_down):
    split, p, words = x_sorted.shape
    ne, d, f = w_gate.shape
    bps = MOE_BLOCKS_PER_STEP
    wspec = lambda a, b, j: pl.BlockSpec((None, a, b), lambda i, te, nt: (te[i * bps + j], 0, 0))
    rows = pl.BlockSpec((split, bps * MOE_ROWS, words), lambda i, te, nt: (0, i, 0))
    weights, wspecs = [], []
    for j in range(bps):
        weights += [w_gate, w_up, w_down]
        wspecs += [wspec(d, f, j), wspec(d, f, j), wspec(f, d, j)]
    grid_spec = pltpu.PrefetchScalarGridSpec(
        num_scalar_prefetch=2,
        grid=(p // (bps * MOE_ROWS),),
        in_specs=[rows] + wspecs,
        out_specs=rows,
    )
    return pl.pallas_call(
        _expert_rows_kernel,
        out_shape=jax.ShapeDtypeStruct((split, p, words), jnp.uint32),
        grid_spec=grid_spec,
        compiler_params=_cparams(("arbitrary",)),
        name="expert_rows",
    )(tile_expert, n_tiles, x_sorted, *weights)


def _combine_kernel(yg_ref, gk_ref, u_ref, x1_ref, g2_ref, nf_ref, sg_ref, su_ref, sd_ref, y_ref, *, tm, d):
    u = u_ref[...].reshape(tm, d)
    hs = _silu(_dot(u, sg_ref[...])) * _dot(u, su_ref[...])
    acc = _dot(hs.astype(BF16), sd_ref[...])
    gk = gk_ref[...].reshape(tm, LANES)
    lane = lax.broadcasted_iota(I32, (1, LANES), 1)
    split = yg_ref.shape[0]
    for k in range(TOP_K):
        gate = jnp.sum(jnp.where(lane == k, gk, 0.0), axis=1, keepdims=True)
        words = jnp.concatenate([yg_ref[s, k] for s in range(split)], axis=1)
        acc = acc + gate * _unpack_bf16_pairs(words)
    x2 = x1_ref[...].reshape(tm, d) + _rows2d(g2_ref) * acc
    y_ref[...] = _rmsnorm(x2, nf_ref[...]).reshape(y_ref.shape)


def _combine(yg, gk, u2, x1, g2, normf, sg, su, sd, *, tm):
    g, r, d = x1.shape
    nt = r // tm
    split, _, _, words = yg.shape
    tok = lambda w: pl.BlockSpec((1, tm, w), lambda b, j: (b, j, 0))
    const = lambda a: pl.BlockSpec(a.shape, lambda b, j: (0,) * a.ndim)
    return pl.pallas_call(
        functools.partial(_combine_kernel, tm=tm, d=d),
        out_shape=jax.ShapeDtypeStruct((g, r, d), F32),
        grid=(g, nt),
        in_specs=[pl.BlockSpec((split, TOP_K, tm, words), lambda b, j: (0, 0, b * nt + j, 0)),
                  tok(LANES), tok(d), tok(d),
                  pl.BlockSpec((1, 1, d), lambda b, j: (b, 0, 5)), const(normf), const(sg), const(su), const(sd)],
        out_specs=tok(d),
        compiler_params=_cparams(("arbitrary", "arbitrary")),
        name="moe_combine",
    )(yg, gk, u2, x1, g2, normf, sg, su, sd)


def _moe_sorted_experts(u2p, eid_t, rank_t, counts, w_gate, w_up, w_down):
    split, n, _ = u2p.shape
    ne = w_gate.shape[0]
    cnt = counts[:, 0].astype(I32)
    padded = -(-cnt // MOE_ROWS) * MOE_ROWS
    seg_end = jnp.cumsum(padded)
    seg_start = seg_end - padded
    p_rows = n * TOP_K + ne * MOE_ROWS
    eid = eid_t.transpose(1, 0, 2).reshape(TOP_K, n)
    start = jnp.sum(jnp.where(eid[:, :, None] == jnp.arange(ne, dtype=I32), seg_start, 0), axis=-1)
    pos = start + rank_t.transpose(1, 0, 2).reshape(TOP_K, n)
    first_row = jnp.arange(p_rows // MOE_ROWS, dtype=I32) * MOE_ROWS
    tile_expert = jnp.minimum(jnp.sum(seg_end[None, :] <= first_row[:, None], axis=1), ne - 1).astype(I32)
    n_tiles = (seg_end[-1:] // MOE_ROWS).astype(I32)
    scat_idx = jnp.concatenate([pos + s * p_rows for s in range(split)], axis=1)
    gath_idx = jnp.concatenate([pos.reshape(-1) + s * p_rows for s in range(split)])
    x_sorted = _sc_scatter_rows(u2p.reshape(split * n, SC_ROW_WORDS), scat_idx, split * p_rows)
    y_sorted = _expert_rows(tile_expert, n_tiles, x_sorted.reshape(split, p_rows, SC_ROW_WORDS),
                            w_gate, w_up, w_down)
    return y_sorted.reshape(split * p_rows, SC_ROW_WORDS), gath_idx


def _moe_sorted_combine(y_sorted, gath_idx, gk, u2, x1, g2, normf, sg, su, sd):
    g, r, _ = x1.shape
    split = gath_idx.shape[0] // (TOP_K * g * r)
    yg = _sc_gather_rows(y_sorted, gath_idx)
    return _combine(yg.reshape(split, TOP_K, g * r, SC_ROW_WORDS), gk, u2, x1, g2, normf, sg, su, sd, tm=512)


def _kv_slot_mask():
    return (jnp.arange(N_HEADS)[:, None] // GROUP == jnp.arange(N_KV)[None, :]).astype(F32)


def _prep_w_in(w_in, d):
    q0 = POOL_W
    kv0 = q0 + N_HEADS * HEAD_DIM
    gn0 = kv0 + 6 * KVW
    gm0 = gn0 + 3 * N_HEADS
    wq = w_in[:, q0:kv0].reshape(d, N_HEADS, 1, HEAD_DIM) * (HEAD_DIM ** -0.5)
    wq = (wq * _kv_slot_mask()[None, :, :, None]).reshape(d, QPAD)
    wgn = jnp.pad(w_in[:, gn0:gm0], ((0, 0), (0, LANES - 3 * N_HEADS)))
    return jnp.concatenate([w_in[:, :q0], wq, w_in[:, kv0:gn0], wgn, w_in[:, gm0:]], axis=1).astype(BF16)


def _prep_w_nsa_out(w, d):
    w = w.reshape(N_HEADS, 1, HEAD_DIM, d) * _kv_slot_mask()[:, :, None, None]
    return w.reshape(QPAD, d).astype(BF16)


def _block_diag(w_lin):
    g, c, _ = w_lin.shape
    eye = jnp.eye(g, dtype=F32)
    return (w_lin[:, :, None, :] * eye[:, None, :, None]).reshape(g * c, g * c).astype(BF16)


def kernel(x_prompt, x_sample, cache_kc, cache_vc, cache_ks, cache_vs, state_kw, state_vw, state_pool,
           page_table, c_prompt, c_sample, norm1_g, norm2_g, normf_g, w_ada, b_ada, w_in, w_pool_lin,
           pool_scale, w_cmp_k, w_cmp_v, w_pool_out, w_nsa_out, w_o, w_router, b_router, w_gate, w_up,
           w_down, ws_gate, ws_up, ws_down):
    depth = w_in.shape[0]
    assert depth == 1, "single-layer stack"
    bsz, seq, d = x_prompt.shape
    n_seq, ns, _ = x_sample.shape
    wbuf = state_kw.shape[2]
    lyr = 0

    n_tok = n_seq * ns
    pad_p = -bsz % 8
    c_all = jnp.concatenate([c_prompt, jnp.zeros((pad_p, d), F32), jnp.repeat(c_sample, ns, axis=0)], axis=0)
    mod = _adaln(c_all, w_ada[lyr], b_ada[lyr])
    mod_p = mod[:bsz].reshape(bsz, 1, 6 * d)
    mod_s = mod[bsz + pad_p:]

    w2 = _prep_w_in(w_in[lyr], d)
    g1n = norm1_g[lyr].reshape(1, d)
    wk = w_cmp_k[lyr].reshape(CMP_BLOCK, KVW)
    wv = w_cmp_v[lyr].reshape(CMP_BLOCK, KVW)
    fin_w = (_block_diag(w_pool_lin[lyr]), pool_scale[lyr].reshape(1, POOL_W), w_pool_out[lyr].astype(BF16),
             _prep_w_nsa_out(w_nsa_out[lyr], d), w_o[lyr].astype(BF16), norm2_g[lyr].reshape(1, d),
             jnp.pad(w_router[lyr].T, ((0, LANES - N_EXPERTS), (0, 0))).astype(BF16),
             b_router[lyr].reshape(N_EXPERTS, 1))
    moe_w = (w_gate[lyr], w_up[lyr], w_down[lyr], ws_gate[lyr].astype(BF16), ws_up[lyr].astype(BF16),
             ws_down[lyr].astype(BF16))
    nf = normf_g.reshape(1, d)

    tm_p = 512
    (vp, kc, vc, ks, vs, kw, vw, gm, ksb, kwb, vst, vwt, qt, gst, pooled) = _in_proj(
        x_prompt, mod_p, mod_p, g1n, w2, tm=tm_p, prompt=True)
    kcmp, vcmpt = _compress(kc, vc, wk, wv)
    ynsa = _nsa_prompt(qt, gst, kcmp, vcmpt, ksb, vst, kwb, vwt)
    x1, u2, u2p, eid_t, gk_t, rank_t, counts = _finish(
        x_prompt, pooled, ynsa, gm, (mod_p, mod_p, mod_p), fin_w, tm=tm_p, sparse=True)

    xs3 = x_sample.reshape(1, n_tok, d)
    tm_s = 128
    (vp_s, kc_s, vc_s, ks_s, vs_s, kw_s, vw_s, gm_s, q_s, gs_s) = _in_proj(
        xs3, mod_s, mod_s, g1n, w2, tm=tm_s, prompt=False)
    two = lambda a: a.reshape(n_tok, a.shape[-1])
    q_rows = q_s.reshape(n_tok * N_HEADS, LANES)
    gate_rows = two(gs_s)[:, :3 * N_HEADS].reshape(n_tok, 3, N_HEADS).transpose(0, 2, 1)
    gate_rows = jnp.pad(gate_rows.reshape(n_tok * N_HEADS, 3), ((0, 0), (0, LANES - 3)))
    n_pool = cache_kc.shape[1]
    page = cache_kc.shape[2]
    rows_minor = lambda a: jnp.transpose(a, (0, 2, 3, 1)).reshape(a.shape[0], KVW, a.shape[1])
    caches = [rows_minor(c[lyr]) for c in (cache_kc, cache_vc, cache_ks, cache_vs)]
    o_rows, pooled_s, kw_next, vw_next = _nsa_sample(
        page_table, q_rows, gate_rows, [two(a) for a in (kc_s, vc_s, ks_s, vs_s, kw_s, vw_s)], two(vp_s),
        rows_minor(state_kw[lyr]), rows_minor(state_vw[lyr]), state_pool[lyr], wk, wv, caches)
    ynsa_s = o_rows.reshape(1, n_tok, QPAD)
    y_sorted, gath_idx = _moe_sorted_experts(u2p, eid_t, rank_t, counts, *moe_w[:3])
    ynsa_s, y_sorted = lax.optimization_barrier((ynsa_s, y_sorted))
    y_prompt = _moe_sorted_combine(y_sorted, gath_idx, gk_t, u2, x1, mod_p, nf, *moe_w[3:])
    x1_s, u2_s, gates_s = _finish(xs3, pooled_s.reshape(1, n_tok, POOL_W), ynsa_s, gm_s,
                                  (mod_s, mod_s, mod_s), fin_w, tm=tm_s, sparse=False)
    y_sample = _moe(u2_s, gates_s, x1_s, mod_s, nf, *moe_w, tm=n_tok).reshape(n_seq, ns, d)

    kvp = lambda a: a.reshape(1, bsz, seq, N_KV, HEAD_DIM)
    tailp = lambda a: jnp.pad(a, ((0, 0), (wbuf, 0), (0, 0)))[:, -wbuf:].reshape(1, bsz, wbuf, N_KV, HEAD_DIM)
    kvs = lambda a: a.reshape(1, n_seq, ns, N_KV, HEAD_DIM)
    wins = lambda a: jnp.transpose(a.reshape(n_seq, N_KV, HEAD_DIM, wbuf), (0, 3, 1, 2))[None]
    pool_p = vp[:, -POOL_BUF:][None]
    pool_s = jnp.concatenate([state_pool[lyr], vp_s.reshape(n_seq, ns, POOL_W)], axis=1)[None, :, -POOL_BUF:]
    return (y_prompt, y_sample, kvp(kc), kvp(vc), kvp(ks), kvp(vs), tailp(kw), tailp(vw), pool_p,
            kvs(kc_s), kvs(vc_s), kvs(ks_s), kvs(vs_s), wins(kw_next), wins(vw_next), pool_s)
```

```python
import functools

import jax
import jax.numpy as jnp
from jax import lax
from jax.experimental import pallas as pl
from jax.experimental.pallas import tpu as pltpu
from jax.experimental.pallas import tpu_sc as plsc

F32 = jnp.float32
BF16 = jnp.bfloat16
I32 = jnp.int32

POOL_WINDOWS = (2, 4, 8, 16)
POOL_GW = 64
POOL_W = 256
POOL_BUF = 15
N_HEADS = 8
HEAD_DIM = 64
N_KV = 2
GROUP = N_HEADS // N_KV
CMP_STRIDE = 16
CMP_BLOCK = 32
SEL_BLOCK = 64
TOP_BLOCKS = 16
WINDOW = 512
Q_BLOCK = 128
FORCE_SCORE = 1e4
N_EXPERTS = 64
N_EGROUPS = 8
EXPERTS_PER_GROUP = N_EXPERTS // N_EGROUPS
TOPK_GROUPS = 4
TOP_K = 8
ROUTED_SCALE = 2.5
EPS = 1e-6
NEG = -1e30
SLOPES = tuple(2.0 ** (-8.0 * (h + 1.0) / N_HEADS) for h in range(N_HEADS))

LANES = 128
QPAD = N_HEADS * LANES
KVW = N_KV * HEAD_DIM
VMEM_LIMIT = 56 * 1024 * 1024


def _cparams(sem):
    return pltpu.CompilerParams(dimension_semantics=sem, vmem_limit_bytes=VMEM_LIMIT)


def _dot(a, b):
    return jnp.dot(a, b, preferred_element_type=F32)


def _dot_nt(a, b):
    return lax.dot_general(a, b, (((1,), (1,)), ((), ())), preferred_element_type=F32)


def _dot_exact(a, b):
    return jnp.dot(a, b, preferred_element_type=F32, precision=lax.Precision.HIGHEST)


def _rows2d(ref):
    v = ref[...]
    return v.reshape(v.shape[-2], v.shape[-1])


def _rmsnorm(x, g):
    return x * lax.rsqrt(jnp.mean(x * x, axis=-1, keepdims=True) + EPS) * g


def _silu(x):
    return x * jax.nn.sigmoid(x)


def _adaln_kernel(c_ref, w_ref, b_ref, o_ref):
    s = _silu(c_ref[...]).astype(BF16)
    o_ref[...] = _dot(s, w_ref[...].astype(BF16)) + b_ref[...]


def _adaln(c, w_ada, b_ada):
    rows, d = c.shape
    n = w_ada.shape[1]
    tn = 512
    return pl.pallas_call(
        _adaln_kernel,
        out_shape=jax.ShapeDtypeStruct((rows, n), F32),
        grid=(n // tn,),
        in_specs=[pl.BlockSpec((rows, d), lambda j: (0, 0)),
                  pl.BlockSpec((d, tn), lambda j: (0, j)),
                  pl.BlockSpec((1, tn), lambda j: (0, j))],
        out_specs=pl.BlockSpec((rows, tn), lambda j: (0, j)),
        compiler_params=_cparams(("arbitrary",)),
        name="adaln",
    )(c, w_ada, b_ada.reshape(1, n))


_C_VP = 0
_C_Q = _C_VP + POOL_W
_C_KV = _C_Q + QPAD
_C_GN = _C_KV + 6 * KVW
_C_GM = _C_GN + LANES


def _pool_window_sums(ext, tm):
    s2 = ext + pltpu.roll(ext, 1, 0)
    s4 = s2 + pltpu.roll(s2, 2, 0)
    s8 = s4 + pltpu.roll(s4, 4, 0)
    s16 = s8 + pltpu.roll(s8, 8, 0)
    grp = lax.broadcasted_iota(I32, (1, POOL_W), 1) // POOL_GW
    pick = jnp.where(grp == 0, s2, jnp.where(grp == 1, s4, jnp.where(grp == 2, s8, s16)))
    return pick[16:16 + tm]


def _in_proj_kernel(x_ref, shift_ref, scale_ref, g_ref, w_ref,
                    vp_ref, kc_ref, vc_ref, ks_ref, vs_ref, kw_ref, vw_ref, gm_ref, *rest, tm, d, prompt):
    x = x_ref[...].reshape(tm, d)
    u = _rmsnorm(x, g_ref[...]) * (1.0 + _rows2d(scale_ref)) + _rows2d(shift_ref)
    ub = u.astype(BF16)

    head = _dot(ub, w_ref[:, 0:_C_GM])

    def proj(c0, n):
        return head[:, c0:c0 + n] if c0 + n <= _C_GM else _dot(ub, w_ref[:, c0:c0 + n])

    vp = proj(_C_VP, POOL_W)
    vp_ref[...] = vp.reshape(vp_ref.shape)
    kv = []
    for n, o32 in enumerate((kc_ref, vc_ref, ks_ref, vs_ref, kw_ref, vw_ref)):
        v = proj(_C_KV + n * KVW, KVW)
        o32[...] = v.reshape(o32.shape)
        kv.append(v)
    gm_ref[...] = jax.nn.sigmoid(proj(_C_GM, 2 * d)).reshape(gm_ref.shape)
    gs = jax.nn.sigmoid(proj(_C_GN, LANES))

    if not prompt:
        q_ref, gs_ref = rest
        q_ref[...] = proj(_C_Q, QPAD).astype(BF16).reshape(q_ref.shape)
        gs_ref[...] = gs.reshape(gs_ref.shape)
    else:
        ksb_ref, kwb_ref, vst_ref, vwt_ref, qt_ref, gst_ref, pooled_ref, halo_ref = rest
        ksb_ref[...] = kv[2].astype(BF16).reshape(ksb_ref.shape)
        kwb_ref[...] = kv[4].astype(BF16).reshape(kwb_ref.shape)
        vst_ref[...] = kv[3].T.astype(BF16).reshape(vst_ref.shape)
        vwt_ref[...] = kv[5].T.astype(BF16).reshape(vwt_ref.shape)
        gst_ref[...] = gs.T.reshape(gst_ref.shape)
        for h in range(N_HEADS):
            qt_ref[0, h] = proj(_C_Q + h * LANES, LANES).T.astype(BF16)
        j = pl.program_id(1)

        @pl.when(j == 0)
        def _():
            halo_ref[...] = jnp.zeros_like(halo_ref)

        ext = jnp.concatenate([halo_ref[...], vp], axis=0)
        sums = _pool_window_sums(ext, tm)
        pos = j * tm + lax.broadcasted_iota(I32, (tm, 1), 0)
        wcol = 2 << (lax.broadcasted_iota(I32, (1, POOL_W), 1) // POOL_GW)
        cnt = jnp.minimum(pos + 1, wcol).astype(F32)
        pooled_ref[...] = (sums / cnt - vp).astype(BF16).reshape(pooled_ref.shape)
        halo_ref[...] = vp[tm - 16:tm]


def _in_proj(x3, shift, scale, g1, w2, *, tm, prompt):
    g, r, d = x3.shape
    nt = r // tm
    per_row = shift.ndim == 2

    def tok(width, dtype):
        return (jax.ShapeDtypeStruct((g, r, width), dtype),
                pl.BlockSpec((1, tm, width), lambda b, j: (b, j, 0)))

    def tok_t(rows, dtype):
        return (jax.ShapeDtypeStruct((g, rows, r), dtype),
                pl.BlockSpec((1, rows, tm), lambda b, j: (b, 0, j)))

    outs = [tok(POOL_W, F32)] + [tok(KVW, F32)] * 6 + [tok(2 * d, F32)]
    scratch = []
    if prompt:
        outs += [tok(KVW, BF16), tok(KVW, BF16), tok_t(KVW, BF16), tok_t(KVW, BF16)]
        outs.append((jax.ShapeDtypeStruct((g, N_HEADS, LANES, r), BF16),
                     pl.BlockSpec((1, N_HEADS, LANES, tm), lambda b, j: (b, 0, 0, j))))
        outs += [tok_t(LANES, F32), tok(POOL_W, BF16)]
        scratch.append(pltpu.VMEM((16, POOL_W), F32))
    else:
        outs += [tok(QPAD, BF16), tok(LANES, F32)]
    if per_row:
        mod_spec = lambda col: pl.BlockSpec((tm, d), lambda b, j, col=col: (b * nt + j, col))
    else:
        mod_spec = lambda col: pl.BlockSpec((1, 1, d), lambda b, j, col=col: (b, 0, col))
    kern = functools.partial(_in_proj_kernel, tm=tm, d=d, prompt=prompt)
    return pl.pallas_call(
        kern,
        out_shape=[o[0] for o in outs],
        grid=(g, nt),
        in_specs=[pl.BlockSpec((1, tm, d), lambda b, j: (b, j, 0)),
                  mod_spec(0), mod_spec(1),
                  pl.BlockSpec((1, d), lambda b, j: (0, 0)),
                  pl.BlockSpec(w2.shape, lambda b, j: (0, 0))],
        out_specs=[o[1] for o in outs],
        scratch_shapes=scratch,
        compiler_params=_cparams(("arbitrary", "arbitrary")),
        name="in_proj_prompt" if prompt else "in_proj_sample",
    )(x3, shift, scale, g1, w2)


def _compress_kernel(kc_ref, vc_ref, wk_ref, wv_ref, okc_ref, ovc_ref, sh_ref, *, nc):
    last = lax.broadcasted_iota(I32, (nc, 1), 0) == nc - 1
    for src, w_ref, dst in ((kc_ref, wk_ref, okc_ref), (vc_ref, wv_ref, ovc_ref)):
        head = jnp.zeros((nc, KVW), F32)
        tail = jnp.zeros((nc, KVW), F32)
        for r in range(CMP_STRIDE):
            rows = src[pl.ds(r, nc, stride=CMP_STRIDE), :]
            head = head + rows * w_ref[r:r + 1, :]
            tail = tail + rows * w_ref[CMP_STRIDE + r:CMP_STRIDE + r + 1, :]
        sh_ref[0:nc, :] = tail
        sh_ref[nc:nc + 8, :] = jnp.zeros((8, KVW), F32)
        out = jnp.where(last, 0.0, head + sh_ref[1:nc + 1, :])
        dst[...] = (out if dst is okc_ref else out.T).astype(BF16)


def _compress(kc, vc, wk, wv):
    b, s, _ = kc.shape
    nc = s // CMP_STRIDE
    big = pl.BlockSpec((None, s, KVW), lambda i: (i, 0, 0))
    wsp = pl.BlockSpec((CMP_BLOCK, KVW), lambda i: (0, 0))
    return pl.pallas_call(
        functools.partial(_compress_kernel, nc=nc),
        out_shape=[jax.ShapeDtypeStruct((b, nc, KVW), BF16), jax.ShapeDtypeStruct((b, KVW, nc), BF16)],
        grid=(b,),
        in_specs=[big, big, wsp, wsp],
        out_specs=[pl.BlockSpec((None, nc, KVW), lambda i: (i, 0, 0)),
                   pl.BlockSpec((None, KVW, nc), lambda i: (i, 0, 0))],
        scratch_shapes=[pltpu.VMEM((nc + 8, KVW), F32)],
        compiler_params=_cparams(("arbitrary",)),
        name="compress",
    )(kc, vc, wk, wv)


def _topk_mask(vals, blk_f, n_top, axis=1):
    sel = jnp.zeros(vals.shape, F32)
    big = float(vals.shape[axis])
    for _ in range(n_top):
        mx = jnp.max(vals, axis=axis, keepdims=True)
        first = jnp.min(jnp.where(vals == mx, blk_f, big), axis=axis, keepdims=True)
        hit = blk_f == first
        sel = jnp.where(hit, 1.0, sel)
        vals = jnp.where(hit, -jnp.inf, vals)
    return sel


def _topk_mask_by_rank(vals, blk, n_valid, n_top):
    rank = jnp.zeros(vals.shape, F32)
    for j in range(n_valid):
        vj = vals[:, j:j + 1]
        beats = (vj > vals) | ((vj == vals) & (blk > j))
        rank = rank + jnp.where(beats, 1.0, 0.0)
    return jnp.where(rank < float(n_top), 1.0, 0.0)


def _pos_features(pos):
    hi = (pos // SEL_BLOCK).astype(F32)[:, None]
    lo = (pos % SEL_BLOCK).astype(F32)[:, None]
    return jnp.concatenate([hi, lo, jnp.zeros((pos.shape[0], LANES - 2), F32)], axis=1).astype(BF16)


def _importance_matrix(nc, nsel):
    j = jnp.arange(nc)[:, None]
    s = jnp.arange(nsel)[None, :]
    r = SEL_BLOCK // CMP_STRIDE
    a = (j >= r * s) & (j <= r * s + r - 1)
    b = (j + 1 >= r * s) & (j + 1 <= r * s + r - 1)
    return a.astype(F32) + b.astype(F32)


def _nsa_prompt_kernel(qt_ref, gst_ref, kc_ref, vct_ref, ks_ref, vst_ref, kw_ref, vwt_ref,
                       cfeat_ref, wfeat_ref, qfeat_ref, slope_ref,
                       y_ref, qk_scr, m_scr, l_scr, acc_scr, o_scr, sel_scr, imp_scr, flag_scr, ids_scr,
                       *, seq, tk, wl):
    i = pl.program_id(1)
    q0 = i * Q_BLOCK
    nq = Q_BLOCK
    gq = GROUP * nq
    nc = kc_ref.shape[0]
    nsel = seq // SEL_BLOCK
    n_top = min(TOP_BLOCKS, nsel)
    blk_per_tile = tk // SEL_BLOCK
    qpos = q0 + lax.broadcasted_iota(I32, (1, nq), 1)
    gst = gst_ref[...]

    crow = lax.broadcasted_iota(I32, (nc, nq), 0)
    cend = crow * CMP_STRIDE + (CMP_BLOCK - 1)
    mask_c = qpos >= cend
    kc = jnp.concatenate([kc_ref[...], cfeat_ref[...]], axis=1)
    vct = vct_ref[...]
    blk = lax.broadcasted_iota(I32, (nsel, nq), 0)
    blk_f = blk.astype(F32)
    cur = qpos // SEL_BLOCK
    forced = (blk == 0) | (blk == cur) | (blk == cur - 1)
    visible = blk * SEL_BLOCK <= qpos
    ws = pl.multiple_of(jnp.maximum(q0 - WINDOW, 0), Q_BLOCK)
    wpos = ws + lax.broadcasted_iota(I32, (wl, nq), 0)
    valid_w = lax.bitcast_convert_type(qpos - wpos, jnp.uint32) < WINDOW
    n_tiles = (q0 + nq + tk - 1) // tk
    half_rows = lax.broadcasted_iota(I32, (KVW, nq), 0) // HEAD_DIM
    tile_pos = lax.broadcasted_iota(I32, (SEL_BLOCK, nq), 0)

    def lanes4(x):
        return jnp.concatenate([x] * GROUP, axis=1)

    def gate_row(branch, k):
        r0 = branch * N_HEADS + k * GROUP
        return jnp.concatenate([gst[r0 + g:r0 + g + 1] for g in range(GROUP)], axis=1)

    mask_c4 = lanes4(mask_c)
    valid_w4 = lanes4(valid_w)
    kwt = jnp.concatenate([kw_ref[pl.ds(ws, wl), :], wfeat_ref[...]], axis=1)
    vwtt = vwt_ref[:, pl.ds(ws, wl)]

    for k in range(N_KV):
        for g in range(GROUP):
            qk_scr[k, 0:LANES, g * nq:(g + 1) * nq] = qt_ref[k * GROUP + g]
        qk_scr[k, LANES:2 * LANES, :] = qfeat_ref[k]
        qk = qk_scr[k]

        s = jnp.where(mask_c4, _dot(kc, qk), NEG)
        e = jnp.where(mask_c4, jnp.exp(s - jnp.max(s, axis=0, keepdims=True)), 0.0)
        l = jnp.sum(e, axis=0, keepdims=True)
        p = e * jnp.where(l > 0.0, 1.0 / l, 0.0)
        o_c = _dot(vct, p.astype(BF16))
        psum = p[:, 0:nq]
        for g in range(1, GROUP):
            psum = psum + p[:, g * nq:(g + 1) * nq]

        a = psum + jnp.where(crow == 0, 0.0, pltpu.roll(psum, 1, 0))
        a = a + pltpu.roll(a, nc - 1, 0)
        imp_scr[...] = a + pltpu.roll(a, nc - 2, 0)
        imp = imp_scr[pl.ds(0, nsel, stride=nc // nsel), :]
        vals = jnp.where(visible, jnp.where(forced, FORCE_SCORE, imp), NEG)
        sel = jnp.where(visible, _topk_mask(vals, blk_f, n_top, axis=0), 0.0)
        sel_scr[k] = jnp.where(sel > 0.5, 0.0, NEG)
        blk_any = jnp.max(sel, axis=1, keepdims=True)
        for t in range(seq // tk):
            hit = (jnp.max(blk_any[t * blk_per_tile:(t + 1) * blk_per_tile, :]) > 0.5).astype(I32)
            flag_scr[t] = hit if k == 0 else flag_scr[t] | hit

        s = jnp.where(valid_w4, _dot(kwt, qk), NEG)
        e = jnp.exp(s - jnp.max(s, axis=0, keepdims=True))
        p = e / jnp.sum(e, axis=0, keepdims=True)
        o_w = _dot(vwtt, p.astype(BF16))
        o_scr[k] = gate_row(0, k) * o_c + gate_row(2, k) * o_w

    m_scr[...] = jnp.full(m_scr.shape, NEG, F32)
    l_scr[...] = jnp.zeros(l_scr.shape, F32)
    acc_scr[...] = jnp.zeros(acc_scr.shape, F32)

    n_act = jnp.int32(0)
    for t in range(seq // tk):
        ids_scr[n_act] = t
        n_act = n_act + jnp.where((flag_scr[t] > 0) & (t < n_tiles), 1, 0)
    lane0 = lax.broadcasted_iota(I32, (tk, LANES), 1) == 0
    feat = wfeat_ref[0:tk, :]

    def sel_tiles(tiles):
        ta = tiles[0]
        starts = [pl.multiple_of(t * tk, tk) for t in tiles]
        keys = [jnp.concatenate([ks_ref[pl.ds(starts[0], tk), :], feat], axis=1)]
        for t, k0 in zip(tiles[1:], starts[1:]):
            shifted = jnp.where(lane0, feat.astype(F32) + ((t - ta) * blk_per_tile).astype(F32), feat.astype(F32))
            keys.append(jnp.concatenate([ks_ref[pl.ds(k0, tk), :], shifted.astype(BF16)], axis=1))
        kt = jnp.concatenate(keys, axis=0)
        vtt = jnp.concatenate([vst_ref[:, pl.ds(k0, tk)] for k0 in starts], axis=1)
        base = (starts[0] - q0).astype(F32)
        scores = _dot(kt, jnp.concatenate([qk_scr[k] for k in range(N_KV)], axis=1))
        probs, alphas = [], []
        for k in range(N_KV):
            neg = []
            for t, k0 in zip(tiles, starts):
                for j in range(blk_per_tile):
                    row = sel_scr[k, pl.ds(t * blk_per_tile + j, 1), :]
                    causal = qpos >= k0 + j * SEL_BLOCK + tile_pos
                    neg.append(jnp.where(causal, jnp.broadcast_to(row, (SEL_BLOCK, nq)), NEG))
            neg = lanes4(jnp.concatenate(neg, axis=0))
            off = slope_ref[k] * base
            s = scores[:, k * gq:(k + 1) * gq] + neg
            m_old = m_scr[k]
            m_new = jnp.maximum(m_old, jnp.max(s, axis=0, keepdims=True) + off)
            alpha = jnp.exp(m_old - m_new)
            p = jnp.exp(s - (m_new - off))
            l_scr[k] = alpha * l_scr[k] + jnp.sum(p, axis=0, keepdims=True)
            m_scr[k] = m_new
            probs.append(p.astype(BF16))
            alphas.append(alpha)
        pv = _dot(vtt, jnp.concatenate(probs, axis=1))
        for k in range(N_KV):
            acc_scr[k] = acc_scr[k] * alphas[k] + pv[:, k * gq:(k + 1) * gq]

    def sel_pair(i, carry):
        sel_tiles([ids_scr[2 * i], ids_scr[2 * i + 1]])
        return carry

    lax.fori_loop(0, n_act // 2, sel_pair, 0)
    pl.when(n_act % 2 == 1)(lambda: sel_tiles([ids_scr[n_act - 1]]))

    for k in range(N_KV):
        o = o_scr[k] + gate_row(1, k) * (acc_scr[k] / l_scr[k])
        for g in range(GROUP):
            h = k * GROUP + g
            oh = jnp.where(half_rows == k, o[:, g * nq:(g + 1) * nq], 0.0)
            y_ref[:, h * LANES:(h + 1) * LANES] = oh.T.astype(BF16)


def _nsa_prompt(qt, gst, kcmp, vcmpt, ksb, vst, kwb, vwt):
    b, _, _, s = qt.shape
    nq = Q_BLOCK
    gq = GROUP * nq
    nc = kcmp.shape[1]
    nsel = s // SEL_BLOCK
    tk = 256
    wl = WINDOW + Q_BLOCK
    assert s % tk == 0 and s >= wl
    assert s // SEL_BLOCK <= 2 * LANES, "position // 64 must stay exact in bf16"
    cfeat = _pos_features(jnp.arange(nc) * CMP_STRIDE + (CMP_BLOCK - 1))
    wfeat = _pos_features(jnp.arange(wl))
    slope_rows = jnp.repeat(jnp.asarray(SLOPES, F32).reshape(N_KV, 1, GROUP), nq, axis=2)
    qfeat = jnp.concatenate([slope_rows * SEL_BLOCK, slope_rows, jnp.zeros((N_KV, LANES - 2, gq), F32)],
                            axis=1).astype(BF16)
    rows = lambda r: pl.BlockSpec((None, r, KVW), lambda bi, i: (bi, 0, 0))
    cols = lambda c: pl.BlockSpec((None, KVW, c), lambda bi, i: (bi, 0, 0))
    const = lambda a: pl.BlockSpec(a.shape, lambda bi, i: (0,) * a.ndim)
    return pl.pallas_call(
        functools.partial(_nsa_prompt_kernel, seq=s, tk=tk, wl=wl),
        out_shape=jax.ShapeDtypeStruct((b, s, QPAD), BF16),
        grid=(b, s // nq),
        in_specs=[pl.BlockSpec((None, N_HEADS, LANES, nq), lambda bi, i: (bi, 0, 0, i)),
                  pl.BlockSpec((None, LANES, nq), lambda bi, i: (bi, 0, i)),
                  rows(nc), cols(nc), rows(s), cols(s), rows(s), cols(s),
                  const(cfeat), const(wfeat), const(qfeat), const(slope_rows)],
        out_specs=pl.BlockSpec((None, nq, QPAD), lambda bi, i: (bi, i, 0)),
        scratch_shapes=[pltpu.VMEM((N_KV, 2 * LANES, gq), BF16),
                        pltpu.VMEM((N_KV, 1, gq), F32),
                        pltpu.VMEM((N_KV, 1, gq), F32),
                        pltpu.VMEM((N_KV, KVW, gq), F32),
                        pltpu.VMEM((N_KV, KVW, gq), F32),
                        pltpu.VMEM((N_KV, nsel, nq), F32),
                        pltpu.VMEM((nc, nq), F32),
                        pltpu.SMEM((s // tk,), I32),
                        pltpu.SMEM((s // tk,), I32)],
        compiler_params=_cparams(("arbitrary", "arbitrary")),
        name="nsa_prompt",
    )(qt, gst, kcmp, vcmpt, ksb, vst, kwb, vwt, cfeat, wfeat, qfeat, slope_rows)


def _nsa_sample_kernel(pt_ref, q_ref, gate_ref, kcn_ref, vcn_ref, ksn_ref, vsn_ref, kwn_ref, vwn_ref,
                       vpn_ref, skw_ref, svw_ref, spool_ref, wk_ref, wv_ref, imat_ref, emat_ref,
                       ckc_ref, cvc_ref, cks_ref, cvs_ref,
                       o_ref, pooled_ref, kwo_ref, vwo_ref, buf, buft, win_scr, tail_scr, vext_scr, sem,
                       *, sb, ns, past, n_pages, page, ncv, ncp, nks, wls, nselp, n_sel):
    step = pl.program_id(0)
    nrow = ns * N_HEADS
    par = step % 2

    def copies(n, side, r):
        out = []
        for p in range(n_pages):
            pg = pt_ref[n * n_pages + p]
            for c, cref in enumerate((ckc_ref, cvc_ref, cks_ref, cvs_ref)):
                out.append(pltpu.make_async_copy(cref.at[pg], buft.at[side, r, c, :, pl.ds(p * page, page)],
                                                 sem.at[side, r]))
        return out

    @pl.when(step == 0)
    def _():
        buf[:, :, past:, :] = jnp.zeros((sb, 2, buf.shape[2] - past, KVW), F32)
        tail_scr[...] = jnp.zeros_like(tail_scr)
        vext_scr[...] = jnp.zeros_like(vext_scr)
        for r in range(sb):
            for cp in copies(r, 0, r):
                cp.start()

    @pl.when(step + 1 < pl.num_programs(0))
    def _():
        for r in range(sb):
            for cp in copies((step + 1) * sb + r, 1 - par, r):
                cp.start()

    for r in range(sb):
        for cp in copies(step * sb + r, par, r):
            cp.wait()

    def new_rows_t(ref, r4, r):
        tail_scr[r, 0:ns, :] = ref[pl.ds(r4, ns), :]
        return tail_scr[r].T

    row = lax.broadcasted_iota(I32, (nrow, 1), 0)
    hrow = row % N_HEADS
    qpos = past + row // N_HEADS
    slope = jnp.exp2(-8.0 * (hrow.astype(F32) + 1.0) / N_HEADS)
    kvrow = hrow // GROUP
    lane = lax.broadcasted_iota(I32, (1, LANES), 1)
    half = (lane // HEAD_DIM) == kvrow
    grow = (row // N_HEADS) * N_KV + kvrow
    row8 = lax.broadcasted_iota(I32, (ns * N_KV, 1), 0)
    qpos8 = past + lax.broadcasted_iota(I32, (ns * N_KV, 1), 0) // N_KV
    blk = lax.broadcasted_iota(I32, (1, nselp), 1)
    blk_f = blk.astype(F32)
    cur = qpos8 // SEL_BLOCK
    forced = (blk == 0) | (blk == cur) | (blk == cur - 1)
    visible = (blk * SEL_BLOCK <= qpos8)
    inrange = blk < n_sel
    cend = lax.broadcasted_iota(I32, (1, ncp), 1) * CMP_STRIDE + (CMP_BLOCK - 1)
    mask_c = qpos >= cend
    bias_c = slope * (cend - qpos).astype(F32)
    kpos = lax.broadcasted_iota(I32, (1, nks), 1)
    causal_s = qpos >= kpos
    bias_s = slope * (kpos - qpos).astype(F32)
    wbuf = wls[0]
    wpos = past - wbuf + lax.broadcasted_iota(I32, (1, wls[1]), 1)
    dw = qpos - wpos
    valid_w = lax.bitcast_convert_type(dw, jnp.uint32) < WINDOW
    bias_w = slope * (wpos - qpos).astype(F32)
    prow = lax.broadcasted_iota(I32, (vext_scr.shape[1], 1), 0)
    wcol = 2 << (lax.broadcasted_iota(I32, (1, POOL_W), 1) // POOL_GW)
    n_top = min(TOP_BLOCKS, n_sel)

    def softmax_rows(s, mask):
        s = jnp.where(mask, s, NEG)
        mx = jnp.max(s, axis=1, keepdims=True)
        e = jnp.where(mask, jnp.exp(s - mx), 0.0)
        l = jnp.sum(e, axis=1, keepdims=True)
        return e * jnp.where(l > 0.0, 1.0 / l, 0.0)

    def seq_body(r):
        r4 = r * ns
        for c, new_ref in enumerate((kcn_ref, vcn_ref)):
            for p in range(n_pages):
                buf[r, c, p * page:(p + 1) * page, :] = buft[par, r, c, :, p * page:(p + 1) * page].T
            buf[r, c, past:past + ns, :] = new_ref[pl.ds(r4, ns), :]
        for c, new_ref in ((2, ksn_ref), (3, vsn_ref)):
            buft[par, r, c, :, past:past + LANES] = new_rows_t(new_ref, r4, r)

        qall = q_ref[pl.ds(r * nrow, nrow), :]
        gates = gate_ref[pl.ds(r * nrow, nrow), :]

        cmp = []
        for c, w_ref in ((0, wk_ref), (1, wv_ref)):
            span = CMP_STRIDE * ncv
            lo = buf[r, c, 0:span, :].reshape(ncv, CMP_STRIDE, KVW) * w_ref[0:CMP_STRIDE, :][None]
            hi = (buf[r, c, CMP_STRIDE:CMP_STRIDE + span, :].reshape(ncv, CMP_STRIDE, KVW)
                  * w_ref[CMP_STRIDE:CMP_BLOCK, :][None])
            acc = jnp.sum(lo + hi, axis=1)
            cmp.append(jnp.concatenate([acc, jnp.zeros((ncp - ncv, KVW), F32)], axis=0).astype(BF16))
        p_c = softmax_rows(_dot_nt(qall, cmp[0]) + bias_c, mask_c)
        o_c = _dot(p_c.astype(BF16), cmp[1])

        psum = jnp.zeros((ns * N_KV, ncp), F32)
        for i in range(ns * N_KV):
            r0 = (i // N_KV) * N_HEADS + (i % N_KV) * GROUP
            psum = jnp.where(row8 == i, jnp.sum(p_c[r0:r0 + GROUP], axis=0, keepdims=True), psum)
        imp = _dot_exact(psum, imat_ref[...])
        vals = jnp.where(inrange, jnp.where(visible, jnp.where(forced, FORCE_SCORE, imp), NEG), -jnp.inf)
        sel8 = _topk_mask_by_rank(vals, blk, n_sel, n_top)
        sel_rows = jnp.zeros((nrow, nselp), F32)
        for i in range(ns * N_KV):
            sel_rows = jnp.where(grow == i, sel8[i:i + 1], sel_rows)
        chosen = _dot(sel_rows.astype(BF16), emat_ref[...])

        kst = buft[par, r, 2].astype(BF16)
        vst = buft[par, r, 3].astype(BF16)
        p_s = softmax_rows(_dot(qall, kst) + bias_s, causal_s & (chosen > 0.5))
        o_s = _dot_nt(p_s.astype(BF16), vst)

        outs_w = []
        for state_ref, new_ref, next_ref in ((skw_ref, kwn_ref, kwo_ref), (svw_ref, vwn_ref, vwo_ref)):
            win_scr[r, :, 0:wbuf] = state_ref[r]
            win_scr[r, :, wbuf:wbuf + LANES] = new_rows_t(new_ref, r4, r)
            outs_w.append(win_scr[r].astype(BF16))
            next_ref[r] = win_scr[r, :, ns:ns + wbuf]
        p_w = softmax_rows(_dot(qall, outs_w[0]) + bias_w, valid_w)
        o_w = _dot_nt(p_w.astype(BF16), outs_w[1])

        o = gates[:, 0:1] * o_c + gates[:, 1:2] * o_s + gates[:, 2:3] * o_w
        o_ref[pl.ds(r * nrow, nrow), :] = jnp.where(half, o, 0.0).astype(BF16)

        vext_scr[r, 0:POOL_BUF, :] = spool_ref[r]
        vext_scr[r, POOL_BUF:POOL_BUF + ns, :] = vpn_ref[pl.ds(r4, ns), :]
        ext = vext_scr[r]
        for t in range(ns):
            hi = POOL_BUF + t
            inwin = (prow <= hi) & (prow > hi - wcol)
            ssum = jnp.sum(jnp.where(inwin, ext, 0.0), axis=0, keepdims=True)
            cnt = jnp.minimum(past + t + 1, wcol).astype(F32)
            pooled_ref[pl.ds(r4 + t, 1), :] = ssum / cnt - ext[hi:hi + 1, :]

    for r in range(sb):
        seq_body(r)


def _nsa_sample(page_table, q_rows, gate_rows, new6, vp_new, state_kwt, state_vwt, state_pool, wk, wv, caches):
    n_seq, n_pages = page_table.shape
    page = caches[0].shape[2]
    past = n_pages * page
    ns = vp_new.shape[0] // n_seq
    wbuf = state_kwt.shape[2]
    sb = 2
    nrow = ns * N_HEADS
    assert ns <= SEL_BLOCK and page == LANES
    t_pad = -(-(past + ns) // SEL_BLOCK) * SEL_BLOCK
    n_cmp = t_pad // CMP_STRIDE - 1
    ncv = -(-n_cmp // 8) * 8
    ncp = -(-ncv // LANES) * LANES
    nks = past + LANES
    n_sel = t_pad // SEL_BLOCK
    nselp = LANES
    assert n_sel <= nselp
    wlp = wbuf + LANES
    buf_rows = -(-(CMP_STRIDE * ncv + CMP_STRIDE) // 8) * 8
    imat = _importance_matrix(ncp, nselp)
    emat = (jnp.arange(nselp)[:, None] == (jnp.arange(nks)[None, :] // SEL_BLOCK)).astype(BF16)

    seqblk = lambda rows, w: pl.BlockSpec((sb * rows, w), lambda i, pt: (i, 0))
    const = lambda a: pl.BlockSpec(a.shape, lambda i, pt: (0,) * a.ndim)
    kern = functools.partial(
        _nsa_sample_kernel, sb=sb, ns=ns, past=past, n_pages=n_pages, page=page,
        ncv=ncv, ncp=ncp, nks=nks, wls=(wbuf, wlp), nselp=nselp, n_sel=n_sel)
    grid_spec = pltpu.PrefetchScalarGridSpec(
        num_scalar_prefetch=1,
        grid=(n_seq // sb,),
        in_specs=[seqblk(nrow, LANES), seqblk(nrow, LANES)] + [seqblk(ns, KVW)] * 6 + [seqblk(ns, POOL_W)]
        + [pl.BlockSpec((sb, KVW, wbuf), lambda i, pt: (i, 0, 0))] * 2
        + [pl.BlockSpec((sb, POOL_BUF, POOL_W), lambda i, pt: (i, 0, 0))]
        + [const(wk), const(wv), const(imat), const(emat)]
        + [pl.BlockSpec(memory_space=pl.ANY)] * 4,
        out_specs=[seqblk(nrow, LANES), seqblk(ns, POOL_W)]
        + [pl.BlockSpec((sb, KVW, wbuf), lambda i, pt: (i, 0, 0))] * 2,
        scratch_shapes=[pltpu.VMEM((sb, 2, buf_rows, KVW), F32),
                        pltpu.VMEM((2, sb, 4, KVW, nks), F32),
                        pltpu.VMEM((sb, KVW, wlp), F32),
                        pltpu.VMEM((sb, LANES, KVW), F32),
                        pltpu.VMEM((sb, 24, POOL_W), F32),
                        pltpu.SemaphoreType.DMA((2, sb))],
    )
    return pl.pallas_call(
        kern,
        out_shape=[jax.ShapeDtypeStruct((n_seq * nrow, LANES), BF16),
                   jax.ShapeDtypeStruct((n_seq * ns, POOL_W), F32)]
        + [jax.ShapeDtypeStruct((n_seq, KVW, wbuf), F32)] * 2,
        grid_spec=grid_spec,
        compiler_params=_cparams(("arbitrary",)),
        name="nsa_sample",
    )(page_table.reshape(-1), q_rows, gate_rows, *new6, vp_new, state_kwt, state_vwt, state_pool, wk, wv,
      imat, emat, *caches)


def _route(logits_t, bias_col, tm):
    sc = jax.nn.sigmoid(logits_t)
    biased = sc + bias_col
    epg = EXPERTS_PER_GROUP
    row8 = lax.broadcasted_iota(I32, (epg, tm), 0).astype(F32)
    ninf = -jnp.inf
    grp = jnp.zeros((N_EGROUPS, tm), F32)
    for g in range(N_EGROUPS):
        bg = biased[g * epg:(g + 1) * epg]
        m1 = jnp.max(bg, axis=0, keepdims=True)
        first = jnp.min(jnp.where(bg == m1, row8, float(epg)), axis=0, keepdims=True)
        m2 = jnp.max(jnp.where(row8 == first, ninf, bg), axis=0, keepdims=True)
        grp = jnp.where(row8 == float(g), m1 + m2, grp)
    keep = jnp.zeros((N_EGROUPS, tm), F32)
    vals = grp
    for _ in range(TOPK_GROUPS):
        mx = jnp.max(vals, axis=0, keepdims=True)
        first = jnp.min(jnp.where(vals == mx, row8, float(N_EGROUPS)), axis=0, keepdims=True)
        hit = row8 == first
        keep = jnp.where(hit, 1.0, keep)
        vals = jnp.where(hit, ninf, vals)
    masked = jnp.concatenate(
        [jnp.where(keep[g:g + 1] > 0.5, biased[g * epg:(g + 1) * epg], NEG) for g in range(N_EGROUPS)], axis=0)
    rowe = lax.broadcasted_iota(I32, (N_EXPERTS, tm), 0).astype(F32)
    chosen = jnp.zeros((N_EXPERTS, tm), F32)
    vals = masked
    picks = []
    for _ in range(TOP_K):
        mx = jnp.max(vals, axis=0, keepdims=True)
        first = jnp.min(jnp.where(vals == mx, rowe, float(N_EXPERTS)), axis=0, keepdims=True)
        hit = rowe == first
        chosen = jnp.where(hit, sc, chosen)
        vals = jnp.where(hit, ninf, vals)
        picks.append((hit, first))
    return ROUTED_SCALE * chosen / jnp.sum(chosen, axis=0, keepdims=True), picks


def _pack_bf16_pairs(x):
    c = x.shape[1] // 2
    bits = lambda v: lax.bitcast_convert_type(v.astype(BF16).astype(F32), jnp.uint32)
    return (bits(x[:, :c]) >> 16) | (bits(x[:, c:]) & jnp.uint32(0xFFFF0000))


def _unpack_bf16_pairs(w):
    lo = lax.bitcast_convert_type(w << 16, F32)
    hi = lax.bitcast_convert_type(w & jnp.uint32(0xFFFF0000), F32)
    return jnp.concatenate([lo, hi], axis=1)


def _finish_kernel(x_ref, pooled_ref, y_ref, gm_ref, g1_ref, shift_ref, scale_ref,
                   wlin_ref, pscale_ref, wpo_ref, wno_ref, wo_ref, n2_ref, wr_ref, br_ref,
                   *rest, tm, d, sparse):
    if sparse:
        tri_ref, x1_ref, u2_ref, up_ref, eid_ref, gk_ref, rank_ref, cnt_ref, carry_scr = rest
    else:
        x1_ref, u2_ref, gates_ref = rest
    x = x_ref[...].reshape(tm, d)
    pooled = pooled_ref[...].reshape(tm, POOL_W).astype(BF16)
    y_pool = _dot(pooled, wlin_ref[...]) * pscale_ref[...]
    a = _dot(y_pool.astype(BF16), wpo_ref[...])
    b = _dot(y_ref[...].reshape(tm, QPAD), wno_ref[...])
    gm = gm_ref[...].reshape(tm, 2 * d)
    merged = gm[:, :d] * a + gm[:, d:] * b
    x1 = x + _rows2d(g1_ref) * _dot(merged.astype(BF16), wo_ref[...])
    x1_ref[...] = x1.reshape(x1_ref.shape)
    u2 = _rmsnorm(x1, n2_ref[...]) * (1.0 + _rows2d(scale_ref)) + _rows2d(shift_ref)
    u2b = u2.astype(BF16)
    u2_ref[...] = u2b.reshape(u2_ref.shape)
    logits_t = _dot_nt(wr_ref[...], u2b)
    gates_t, picks = _route(logits_t[:N_EXPERTS], br_ref[...], tm)
    if not sparse:
        gates_t = jnp.concatenate([gates_t, jnp.zeros((LANES - N_EXPERTS, tm), F32)], axis=0)
        gates_ref[...] = gates_t.T.reshape(gates_ref.shape)
        return

    @pl.when((pl.program_id(0) == 0) & (pl.program_id(1) == 0))
    def _():
        carry_scr[...] = jnp.zeros_like(carry_scr)

    packed = _pack_bf16_pairs(u2)
    for s in range(up_ref.shape[0]):
        up_ref[s] = packed[:, s * SC_ROW_WORDS:(s + 1) * SC_ROW_WORDS]
    hit_all = picks[0][0]
    for hit, _ in picks[1:]:
        hit_all = hit_all | hit
    hits = jnp.where(hit_all, 1.0, 0.0).astype(BF16)
    before = _dot(hits, tri_ref[...]) + jnp.concatenate([carry_scr[...]] * (tm // LANES), axis=1)
    eids, gks, ranks = [], [], []
    for hit, first in picks:
        eids.append(first)
        gks.append(jnp.sum(jnp.where(hit, gates_t, 0.0), axis=0, keepdims=True))
        ranks.append(jnp.sum(jnp.where(hit, before, 0.0), axis=0, keepdims=True))
    pick_row = lax.broadcasted_iota(I32, (TOP_K, tm), 0)

    def stack(rows):
        out = jnp.zeros((TOP_K, tm), F32)
        for r, v in enumerate(rows):
            out = jnp.where(pick_row == r, v, out)
        return out

    eid_ref[...] = stack(eids).astype(I32).reshape(eid_ref.shape)
    gk_ref[...] = jnp.concatenate([stack(gks), jnp.zeros((LANES - TOP_K, tm), F32)], axis=0).T.reshape(gk_ref.shape)
    rank_ref[...] = stack(ranks).astype(I32).reshape(rank_ref.shape)
    carry_scr[...] += _dot(hits, jnp.ones((tm, LANES), BF16))
    cnt_ref[...] = carry_scr[...]


def _finish(x3, pooled, ynsa, gm, mods, wts, *, tm, sparse):
    g, r, d = x3.shape
    nt = r // tm
    g1, shift2, scale2 = mods
    per_row = g1.ndim == 2
    tok = lambda w: pl.BlockSpec((1, tm, w), lambda b, j: (b, j, 0))
    tok_t = lambda rows: pl.BlockSpec((1, rows, tm), lambda b, j: (b, 0, j))
    if per_row:
        mod_spec = lambda col: pl.BlockSpec((tm, d), lambda b, j, col=col: (b * nt + j, col))
    else:
        mod_spec = lambda col: pl.BlockSpec((1, 1, d), lambda b, j, col=col: (b, 0, col))
    const = lambda a: pl.BlockSpec(a.shape, lambda b, j: (0,) * a.ndim)
    out_shape = [jax.ShapeDtypeStruct((g, r, d), F32), jax.ShapeDtypeStruct((g, r, d), BF16)]
    out_specs = [tok(d), tok(d)]
    scratch = []
    if sparse:
        tri = (jnp.arange(tm)[:, None] < jnp.arange(tm)[None, :]).astype(BF16)
        wts = tuple(wts) + (tri,)
        split = d // 2 // SC_ROW_WORDS
        out_shape += [jax.ShapeDtypeStruct((split, g * r, SC_ROW_WORDS), jnp.uint32),
                      jax.ShapeDtypeStruct((g, TOP_K, r), I32), jax.ShapeDtypeStruct((g, r, LANES), F32),
                      jax.ShapeDtypeStruct((g, TOP_K, r), I32), jax.ShapeDtypeStruct((N_EXPERTS, LANES), F32)]
        out_specs += [pl.BlockSpec((split, tm, SC_ROW_WORDS), lambda b, j: (0, b * nt + j, 0)),
                      tok_t(TOP_K), tok(LANES), tok_t(TOP_K),
                      pl.BlockSpec((N_EXPERTS, LANES), lambda b, j: (0, 0))]
        scratch.append(pltpu.VMEM((N_EXPERTS, LANES), F32))
    else:
        out_shape.append(jax.ShapeDtypeStruct((g, r, LANES), F32))
        out_specs.append(tok(LANES))
    return pl.pallas_call(
        functools.partial(_finish_kernel, tm=tm, d=d, sparse=sparse),
        out_shape=out_shape,
        grid=(g, nt),
        in_specs=[tok(d), tok(POOL_W), tok(QPAD), tok(2 * d), mod_spec(2), mod_spec(3), mod_spec(4)]
        + [const(w) for w in wts],
        out_specs=out_specs,
        scratch_shapes=scratch,
        compiler_params=_cparams(("arbitrary", "arbitrary")),
        name="finish_route" if sparse else "finish",
    )(x3, pooled, ynsa, gm, g1, shift2, scale2, *wts)


def _moe_kernel(u_ref, gates_ref, x1_ref, g2_ref, nf_ref, wg_ref, wu_ref, wd_ref, sg_ref, su_ref, sd_ref,
                y_ref, acc_ref, *, tm, d, eps):
    e = pl.program_id(2)
    u = u_ref[...].reshape(tm, d)

    @pl.when(e == 0)
    def _():
        hs = _silu(_dot(u, sg_ref[...])) * _dot(u, su_ref[...])
        acc_ref[...] = _dot(hs.astype(BF16), sd_ref[...])

    gates = gates_ref[...].reshape(tm, LANES)
    lane = lax.broadcasted_iota(I32, (1, LANES), 1)
    hidden = []
    for j in range(eps):
        h = _silu(_dot(u, wg_ref[j].astype(BF16))) * _dot(u, wu_ref[j].astype(BF16))
        gate = jnp.sum(jnp.where(lane == e * eps + j, gates, 0.0), axis=1, keepdims=True)
        hidden.append((h * gate).astype(BF16))
    f = wd_ref.shape[1]
    acc_ref[...] += _dot(jnp.concatenate(hidden, axis=1), wd_ref[...].reshape(eps * f, d).astype(BF16))

    @pl.when(e == pl.num_programs(2) - 1)
    def _():
        x2 = x1_ref[...].reshape(tm, d) + _rows2d(g2_ref) * acc_ref[...]
        y_ref[...] = _rmsnorm(x2, nf_ref[...]).reshape(y_ref.shape)


def _moe(u2, gates, x1, g2, normf, w_gate, w_up, w_down, sg, su, sd, *, tm):
    g, r, d = x1.shape
    nt = r // tm
    ne, _, f = w_gate.shape
    per_row = g2.ndim == 2
    tok = lambda w: pl.BlockSpec((1, tm, w), lambda b, j, e: (b, j, 0))
    if per_row:
        g2_spec = pl.BlockSpec((tm, d), lambda b, j, e: (b * nt + j, 5))
    else:
        g2_spec = pl.BlockSpec((1, 1, d), lambda b, j, e: (b, 0, 5))
    once = pl.Buffered(buffer_count=1)
    const = lambda a: pl.BlockSpec(a.shape, lambda b, j, e: (0,) * a.ndim, pipeline_mode=once)
    eps = 4
    return pl.pallas_call(
        functools.partial(_moe_kernel, tm=tm, d=d, eps=eps),
        out_shape=jax.ShapeDtypeStruct((g, r, d), F32),
        grid=(g, nt, ne // eps),
        in_specs=[tok(d), tok(LANES),
                  pl.BlockSpec((1, tm, d), lambda b, j, e: (b, j, 0), pipeline_mode=once),
                  g2_spec, const(normf),
                  pl.BlockSpec((eps, d, f), lambda b, j, e: (e, 0, 0)),
                  pl.BlockSpec((eps, d, f), lambda b, j, e: (e, 0, 0)),
                  pl.BlockSpec((eps, f, d), lambda b, j, e: (e, 0, 0)),
                  const(sg), const(su), const(sd)],
        out_specs=tok(d),
        scratch_shapes=[pltpu.VMEM((tm, d), F32)],
        compiler_params=_cparams(("arbitrary", "arbitrary", "arbitrary")),
        name="moe",
    )(u2, gates, x1, g2, normf, w_gate, w_up, w_down, sg, su, sd)


SC_WINDOW = 128
SC_ROW_WORDS = 256
MOE_ROWS = 512


def _sc_mesh():
    return plsc.VectorSubcoreMesh(core_axis_name="c", subcore_axis_name="s")


def _sc_scatter_rows(src, dst_idx, n_dst):
    n, w = src.shape
    nk = dst_idx.shape[0]

    @pl.kernel(out_type=jax.ShapeDtypeStruct((n_dst, w), src.dtype), mesh=_sc_mesh(), scratch_types=[])
    def scatter(src_hbm, idx_hbm, dst_hbm):
        def body(rows_vmem, idx_vmem):
            pltpu.sync_copy(rows_vmem, dst_hbm.at[idx_vmem.at[0]])

        pltpu.emit_pipeline(
            body,
            grid=(nk, n // SC_WINDOW),
            in_specs=[pl.BlockSpec((SC_WINDOW, w), index_map=lambda k, i: (i, 0)),
                      pl.BlockSpec((1, SC_WINDOW), index_map=lambda k, i: (k, i))],
            out_specs=[],
            core_axis_name=("c", "s"),
            dimension_semantics=(pltpu.PARALLEL, pltpu.PARALLEL),
        )(src_hbm, idx_hbm)

    return scatter(src, dst_idx)


def _sc_gather_rows(src, idx):
    n, w = idx.shape[0], src.shape[1]

    @pl.kernel(out_type=jax.ShapeDtypeStruct((n, w), src.dtype), mesh=_sc_mesh(), scratch_types=[])
    def gather(src_hbm, idx_hbm, out_hbm):
        def body(idx_vmem, out_vmem):
            pltpu.sync_copy(src_hbm.at[idx_vmem.at[0]], out_vmem)

        pltpu.emit_pipeline(
            body,
            grid=(n // SC_WINDOW,),
            in_specs=[pl.BlockSpec((1, SC_WINDOW), index_map=lambda i: (0, i))],
            out_specs=[pl.BlockSpec((SC_WINDOW, w), index_map=lambda i: (i, 0))],
            core_axis_name=("c", "s"),
            dimension_semantics=(pltpu.PARALLEL,),
        )(idx_hbm, out_hbm)

    return gather(src, idx.reshape(1, n))


MOE_BLOCKS_PER_STEP = 2


def _expert_rows_kernel(te_ref, nt_ref, x_ref, *refs):
    y_ref = refs[-1]
    split = x_ref.shape[0]
    for j in range(MOE_BLOCKS_PER_STEP):
        wg_ref, wu_ref, wd_ref = refs[3 * j:3 * j + 3]
        rows = slice(j * MOE_ROWS, (j + 1) * MOE_ROWS)

        @pl.when(pl.program_id(0) * MOE_BLOCKS_PER_STEP + j < nt_ref[0])
        def _():
            x = _unpack_bf16_pairs(jnp.concatenate([x_ref[s, rows] for s in range(split)], axis=1)).astype(BF16)
            h = _silu(_dot(x, wg_ref[...].astype(BF16))) * _dot(x, wu_ref[...].astype(BF16))
            y = _pack_bf16_pairs(_dot(h.astype(BF16), wd_ref[...].astype(BF16)))
            for s in range(split):
                y_ref[s, rows] = y[:, s * SC_ROW_WORDS:(s + 1) * SC_ROW_WORDS]


def _expert_rows(tile_expert, n_tiles, x_sorted, w_gate, w_up, w_down):
    split, p, words = x_sorted.shape
    ne, d, f = w_gate.shape
    bps = MOE_BLOCKS_PER_STEP
    wspec = lambda a, b, j: pl.BlockSpec((None, a, b), lambda i, te, nt: (te[i * bps + j], 0, 0))
    rows = pl.BlockSpec((split, bps * MOE_ROWS, words), lambda i, te, nt: (0, i, 0))
    weights, wspecs = [], []
    for j in range(bps):
        weights += [w_gate, w_up, w_down]
        wspecs += [wspec(d, f, j), wspec(d, f, j), wspec(f, d, j)]
    grid_spec = pltpu.PrefetchScalarGridSpec(
        num_scalar_prefetch=2,
        grid=(p // (bps * MOE_ROWS),),
        in_specs=[rows] + wspecs,
        out_specs=rows,
    )
    return pl.pallas_call(
        _expert_rows_kernel,
        out_shape=jax.ShapeDtypeStruct((split, p, words), jnp.uint32),
        grid_spec=grid_spec,
        compiler_params=_cparams(("arbitrary",)),
        name="expert_rows",
    )(tile_expert, n_tiles, x_sorted, *weights)


def _combine_kernel(yg_ref, gk_ref, u_ref, x1_ref, g2_ref, nf_ref, sg_ref, su_ref, sd_ref, y_ref, *, tm, d):
    u = u_ref[...].reshape(tm, d)
    hs = _silu(_dot(u, sg_ref[...])) * _dot(u, su_ref[...])
    acc = _dot(hs.astype(BF16), sd_ref[...])
    gk = gk_ref[...].reshape(tm, LANES)
    lane = lax.broadcasted_iota(I32, (1, LANES), 1)
    split = yg_ref.shape[0]
    for k in range(TOP_K):
        gate = jnp.sum(jnp.where(lane == k, gk, 0.0), axis=1, keepdims=True)
        words = jnp.concatenate([yg_ref[s, k] for s in range(split)], axis=1)
        acc = acc + gate * _unpack_bf16_pairs(words)
    x2 = x1_ref[...].reshape(tm, d) + _rows2d(g2_ref) * acc
    y_ref[...] = _rmsnorm(x2, nf_ref[...]).reshape(y_ref.shape)


def _combine(yg, gk, u2, x1, g2, normf, sg, su, sd, *, tm):
    g, r, d = x1.shape
    nt = r // tm
    split, _, _, words = yg.shape
    tok = lambda w: pl.BlockSpec((1, tm, w), lambda b, j: (b, j, 0))
    const = lambda a: pl.BlockSpec(a.shape, lambda b, j: (0,) * a.ndim)
    return pl.pallas_call(
        functools.partial(_combine_kernel, tm=tm, d=d),
        out_shape=jax.ShapeDtypeStruct((g, r, d), F32),
        grid=(g, nt),
        in_specs=[pl.BlockSpec((split, TOP_K, tm, words), lambda b, j: (0, 0, b * nt + j, 0)),
                  tok(LANES), tok(d), tok(d),
                  pl.BlockSpec((1, 1, d), lambda b, j: (b, 0, 5)), const(normf), const(sg), const(su), const(sd)],
        out_specs=tok(d),
        compiler_params=_cparams(("arbitrary", "arbitrary")),
        name="moe_combine",
    )(yg, gk, u2, x1, g2, normf, sg, su, sd)


def _moe_sorted_experts(u2p, eid_t, rank_t, counts, w_gate, w_up, w_down):
    split, n, _ = u2p.shape
    ne = w_gate.shape[0]
    cnt = counts[:, 0].astype(I32)
    padded = -(-cnt // MOE_ROWS) * MOE_ROWS
    seg_end = jnp.cumsum(padded)
    seg_start = seg_end - padded
    p_rows = n * TOP_K + ne * MOE_ROWS
    eid = eid_t.transpose(1, 0, 2).reshape(TOP_K, n)
    start = jnp.sum(jnp.where(eid[:, :, None] == jnp.arange(ne, dtype=I32), seg_start, 0), axis=-1)
    pos = start + rank_t.transpose(1, 0, 2).reshape(TOP_K, n)
    first_row = jnp.arange(p_rows // MOE_ROWS, dtype=I32) * MOE_ROWS
    tile_expert = jnp.minimum(jnp.sum(seg_end[None, :] <= first_row[:, None], axis=1), ne - 1).astype(I32)
    n_tiles = (seg_end[-1:] // MOE_ROWS).astype(I32)
    scat_idx = jnp.concatenate([pos + s * p_rows for s in range(split)], axis=1)
    gath_idx = jnp.concatenate([pos.reshape(-1) + s * p_rows for s in range(split)])
    x_sorted = _sc_scatter_rows(u2p.reshape(split * n, SC_ROW_WORDS), scat_idx, split * p_rows)
    y_sorted = _expert_rows(tile_expert, n_tiles, x_sorted.reshape(split, p_rows, SC_ROW_WORDS),
                            w_gate, w_up, w_down)
    return y_sorted.reshape(split * p_rows, SC_ROW_WORDS), gath_idx


def _moe_sorted_combine(y_sorted, gath_idx, gk, u2, x1, g2, normf, sg, su, sd):
    g, r, _ = x1.shape
    split = gath_idx.shape[0] // (TOP_K * g * r)
    yg = _sc_gather_rows(y_sorted, gath_idx)
    return _combine(yg.reshape(split, TOP_K, g * r, SC_ROW_WORDS), gk, u2, x1, g2, normf, sg, su, sd, tm=512)


def _kv_slot_mask():
    return (jnp.arange(N_HEADS)[:, None] // GROUP == jnp.arange(N_KV)[None, :]).astype(F32)


def _prep_w_in(w_in, d):
    q0 = POOL_W
    kv0 = q0 + N_HEADS * HEAD_DIM
    gn0 = kv0 + 6 * KVW
    gm0 = gn0 + 3 * N_HEADS
    wq = w_in[:, q0:kv0].reshape(d, N_HEADS, 1, HEAD_DIM) * (HEAD_DIM ** -0.5)
    wq = (wq * _kv_slot_mask()[None, :, :, None]).reshape(d, QPAD)
    wgn = jnp.pad(w_in[:, gn0:gm0], ((0, 0), (0, LANES - 3 * N_HEADS)))
    return jnp.concatenate([w_in[:, :q0], wq, w_in[:, kv0:gn0], wgn, w_in[:, gm0:]], axis=1).astype(BF16)


def _prep_w_nsa_out(w, d):
    w = w.reshape(N_HEADS, 1, HEAD_DIM, d) * _kv_slot_mask()[:, :, None, None]
    return w.reshape(QPAD, d).astype(BF16)


def _block_diag(w_lin):
    g, c, _ = w_lin.shape
    eye = jnp.eye(g, dtype=F32)
    return (w_lin[:, :, None, :] * eye[:, None, :, None]).reshape(g * c, g * c).astype(BF16)


def kernel(x_prompt, x_sample, cache_kc, cache_vc, cache_ks, cache_vs, state_kw, state_vw, state_pool,
           page_table, c_prompt, c_sample, norm1_g, norm2_g, normf_g, w_ada, b_ada, w_in, w_pool_lin,
           pool_scale, w_cmp_k, w_cmp_v, w_pool_out, w_nsa_out, w_o, w_router, b_router, w_gate, w_up,
           w_down, ws_gate, ws_up, ws_down):
    depth = w_in.shape[0]
    assert depth == 1, "single-layer stack"
    bsz, seq, d = x_prompt.shape
    n_seq, ns, _ = x_sample.shape
    wbuf = state_kw.shape[2]
    lyr = 0

    n_tok = n_seq * ns
    pad_p = -bsz % 8
    c_all = jnp.concatenate([c_prompt, jnp.zeros((pad_p, d), F32), jnp.repeat(c_sample, ns, axis=0)], axis=0)
    mod = _adaln(c_all, w_ada[lyr], b_ada[lyr])
    mod_p = mod[:bsz].reshape(bsz, 1, 6 * d)
    mod_s = mod[bsz + pad_p:]

    w2 = _prep_w_in(w_in[lyr], d)
    g1n = norm1_g[lyr].reshape(1, d)
    wk = w_cmp_k[lyr].reshape(CMP_BLOCK, KVW)
    wv = w_cmp_v[lyr].reshape(CMP_BLOCK, KVW)
    fin_w = (_block_diag(w_pool_lin[lyr]), pool_scale[lyr].reshape(1, POOL_W), w_pool_out[lyr].astype(BF16),
             _prep_w_nsa_out(w_nsa_out[lyr], d), w_o[lyr].astype(BF16), norm2_g[lyr].reshape(1, d),
             jnp.pad(w_router[lyr].T, ((0, LANES - N_EXPERTS), (0, 0))).astype(BF16),
             b_router[lyr].reshape(N_EXPERTS, 1))
    moe_w = (w_gate[lyr], w_up[lyr], w_down[lyr], ws_gate[lyr].astype(BF16), ws_up[lyr].astype(BF16),
             ws_down[lyr].astype(BF16))
    nf = normf_g.reshape(1, d)

    tm_p = 512
    (vp, kc, vc, ks, vs, kw, vw, gm, ksb, kwb, vst, vwt, qt, gst, pooled) = _in_proj(
        x_prompt, mod_p, mod_p, g1n, w2, tm=tm_p, prompt=True)
    kcmp, vcmpt = _compress(kc, vc, wk, wv)
    ynsa = _nsa_prompt(qt, gst, kcmp, vcmpt, ksb, vst, kwb, vwt)
    x1, u2, u2p, eid_t, gk_t, rank_t, counts = _finish(
        x_prompt, pooled, ynsa, gm, (mod_p, mod_p, mod_p), fin_w, tm=tm_p, sparse=True)

    xs3 = x_sample.reshape(1, n_tok, d)
    tm_s = 128
    (vp_s, kc_s, vc_s, ks_s, vs_s, kw_s, vw_s, gm_s, q_s, gs_s) = _in_proj(
        xs3, mod_s, mod_s, g1n, w2, tm=tm_s, prompt=False)
    two = lambda a: a.reshape(n_tok, a.shape[-1])
    q_rows = q_s.reshape(n_tok * N_HEADS, LANES)
    gate_rows = two(gs_s)[:, :3 * N_HEADS].reshape(n_tok, 3, N_HEADS).transpose(0, 2, 1)
    gate_rows = jnp.pad(gate_rows.reshape(n_tok * N_HEADS, 3), ((0, 0), (0, LANES - 3)))
    n_pool = cache_kc.shape[1]
    page = cache_kc.shape[2]
    rows_minor = lambda a: jnp.transpose(a, (0, 2, 3, 1)).reshape(a.shape[0], KVW, a.shape[1])
    caches = [rows_minor(c[lyr]) for c in (cache_kc, cache_vc, cache_ks, cache_vs)]
    o_rows, pooled_s, kw_next, vw_next = _nsa_sample(
        page_table, q_rows, gate_rows, [two(a) for a in (kc_s, vc_s, ks_s, vs_s, kw_s, vw_s)], two(vp_s),
        rows_minor(state_kw[lyr]), rows_minor(state_vw[lyr]), state_pool[lyr], wk, wv, caches)
    ynsa_s = o_rows.reshape(1, n_tok, QPAD)
    y_sorted, gath_idx = _moe_sorted_experts(u2p, eid_t, rank_t, counts, *moe_w[:3])
    ynsa_s, y_sorted = lax.optimization_barrier((ynsa_s, y_sorted))
    y_prompt = _moe_sorted_combine(y_sorted, gath_idx, gk_t, u2, x1, mod_p, nf, *moe_w[3:])
    x1_s, u2_s, gates_s = _finish(xs3, pooled_s.reshape(1, n_tok, POOL_W), ynsa_s, gm_s,
                                  (mod_s, mod_s, mod_s), fin_w, tm=tm_s, sparse=False)
    y_sample = _moe(u2_s, gates_s, x1_s, mod_s, nf, *moe_w, tm=n_tok).reshape(n_seq, ns, d)

    kvp = lambda a: a.reshape(1, bsz, seq, N_KV, HEAD_DIM)
    tailp = lambda a: jnp.pad(a, ((0, 0), (wbuf, 0), (0, 0)))[:, -wbuf:].reshape(1, bsz, wbuf, N_KV, HEAD_DIM)
    kvs = lambda a: a.reshape(1, n_seq, ns, N_KV, HEAD_DIM)
    wins = lambda a: jnp.transpose(a.reshape(n_seq, N_KV, HEAD_DIM, wbuf), (0, 3, 1, 2))[None]
    pool_p = vp[:, -POOL_BUF:][None]
    pool_s = jnp.concatenate([state_pool[lyr], vp_s.reshape(n_seq, ns, POOL_W)], axis=1)[None, :, -POOL_BUF:]
    return (y_prompt, y_sample, kvp(kc), kvp(vc), kvp(ks), kvp(vs), tailp(kw), tailp(vw), pool_p,
            kvs(kc_s), kvs(vc_s), kvs(ks_s), kvs(vs_s), wins(kw_next), wins(vw_next), pool_s)
```

```python
import functools

import jax
import jax.numpy as jnp
from jax import lax
from jax.experimental import pallas as pl
from jax.experimental.pallas import tpu as pltpu
from jax.experimental.pallas import tpu_sc as plsc

F32 = jnp.float32
BF16 = jnp.bfloat16
I32 = jnp.int32

POOL_WINDOWS = (2, 4, 8, 16)
POOL_GW = 64
POOL_W = 256
POOL_BUF = 15
N_HEADS = 8
HEAD_DIM = 64
N_KV = 2
GROUP = N_HEADS // N_KV
CMP_STRIDE = 16
CMP_BLOCK = 32
SEL_BLOCK = 64
TOP_BLOCKS = 16
WINDOW = 512
Q_BLOCK = 128
FORCE_SCORE = 1e4
N_EXPERTS = 64
N_EGROUPS = 8
EXPERTS_PER_GROUP = N_EXPERTS // N_EGROUPS
TOPK_GROUPS = 4
TOP_K = 8
ROUTED_SCALE = 2.5
EPS = 1e-6
NEG = -1e30
SLOPES = tuple(2.0 ** (-8.0 * (h + 1.0) / N_HEADS) for h in range(N_HEADS))

LANES = 128
QPAD = N_HEADS * LANES
KVW = N_KV * HEAD_DIM
VMEM_LIMIT = 56 * 1024 * 1024


def _cparams(sem):
    return pltpu.CompilerParams(dimension_semantics=sem, vmem_limit_bytes=VMEM_LIMIT)


def _dot(a, b):
    return jnp.dot(a, b, preferred_element_type=F32)


def _dot_nt(a, b):
    return lax.dot_general(a, b, (((1,), (1,)), ((), ())), preferred_element_type=F32)


def _dot_exact(a, b):
    return jnp.dot(a, b, preferred_element_type=F32, precision=lax.Precision.HIGHEST)


def _rows2d(ref):
    v = ref[...]
    return v.reshape(v.shape[-2], v.shape[-1])


def _rmsnorm(x, g):
    return x * lax.rsqrt(jnp.mean(x * x, axis=-1, keepdims=True) + EPS) * g


def _silu(x):
    return x * jax.nn.sigmoid(x)


def _adaln_kernel(c_ref, w_ref, b_ref, o_ref):
    s = _silu(c_ref[...]).astype(BF16)
    o_ref[...] = _dot(s, w_ref[...].astype(BF16)) + b_ref[...]


def _adaln(c, w_ada, b_ada):
    rows, d = c.shape
    n = w_ada.shape[1]
    tn = 512
    return pl.pallas_call(
        _adaln_kernel,
        out_shape=jax.ShapeDtypeStruct((rows, n), F32),
        grid=(n // tn,),
        in_specs=[pl.BlockSpec((rows, d), lambda j: (0, 0)),
                  pl.BlockSpec((d, tn), lambda j: (0, j)),
                  pl.BlockSpec((1, tn), lambda j: (0, j))],
        out_specs=pl.BlockSpec((rows, tn), lambda j: (0, j)),
        compiler_params=_cparams(("arbitrary",)),
        name="adaln",
    )(c, w_ada, b_ada.reshape(1, n))


_C_VP = 0
_C_Q = _C_VP + POOL_W
_C_KV = _C_Q + QPAD
_C_GN = _C_KV + 6 * KVW
_C_GM = _C_GN + LANES


def _pool_window_sums(ext, tm):
    s2 = ext + pltpu.roll(ext, 1, 0)
    s4 = s2 + pltpu.roll(s2, 2, 0)
    s8 = s4 + pltpu.roll(s4, 4, 0)
    s16 = s8 + pltpu.roll(s8, 8, 0)
    grp = lax.broadcasted_iota(I32, (1, POOL_W), 1) // POOL_GW
    pick = jnp.where(grp == 0, s2, jnp.where(grp == 1, s4, jnp.where(grp == 2, s8, s16)))
    return pick[16:16 + tm]


def _in_proj_kernel(x_ref, shift_ref, scale_ref, g_ref, w_ref,
                    vp_ref, kc_ref, vc_ref, ks_ref, vs_ref, kw_ref, vw_ref, gm_ref, *rest, tm, d, prompt):
    x = x_ref[...].reshape(tm, d)
    u = _rmsnorm(x, g_ref[...]) * (1.0 + _rows2d(scale_ref)) + _rows2d(shift_ref)
    ub = u.astype(BF16)

    head = _dot(ub, w_ref[:, 0:_C_GM])

    def proj(c0, n):
        return head[:, c0:c0 + n] if c0 + n <= _C_GM else _dot(ub, w_ref[:, c0:c0 + n])

    vp = proj(_C_VP, POOL_W)
    vp_ref[...] = vp.reshape(vp_ref.shape)
    kv = []
    for n, o32 in enumerate((kc_ref, vc_ref, ks_ref, vs_ref, kw_ref, vw_ref)):
        v = proj(_C_KV + n * KVW, KVW)
        o32[...] = v.reshape(o32.shape)
        kv.append(v)
    gm_ref[...] = jax.nn.sigmoid(proj(_C_GM, 2 * d)).reshape(gm_ref.shape)
    gs = jax.nn.sigmoid(proj(_C_GN, LANES))

    if not prompt:
        q_ref, gs_ref = rest
        q_ref[...] = proj(_C_Q, QPAD).astype(BF16).reshape(q_ref.shape)
        gs_ref[...] = gs.reshape(gs_ref.shape)
    else:
        ksb_ref, kwb_ref, vst_ref, vwt_ref, qt_ref, gst_ref, pooled_ref, halo_ref = rest
        ksb_ref[...] = kv[2].astype(BF16).reshape(ksb_ref.shape)
        kwb_ref[...] = kv[4].astype(BF16).reshape(kwb_ref.shape)
        vst_ref[...] = kv[3].T.astype(BF16).reshape(vst_ref.shape)
        vwt_ref[...] = kv[5].T.astype(BF16).reshape(vwt_ref.shape)
        gst_ref[...] = gs.T.reshape(gst_ref.shape)
        for h in range(N_HEADS):
            qt_ref[0, h] = proj(_C_Q + h * LANES, LANES).T.astype(BF16)
        j = pl.program_id(1)

        @pl.when(j == 0)
        def _():
            halo_ref[...] = jnp.zeros_like(halo_ref)

        ext = jnp.concatenate([halo_ref[...], vp], axis=0)
        sums = _pool_window_sums(ext, tm)
        pos = j * tm + lax.broadcasted_iota(I32, (tm, 1), 0)
        wcol = 2 << (lax.broadcasted_iota(I32, (1, POOL_W), 1) // POOL_GW)
        cnt = jnp.minimum(pos + 1, wcol).astype(F32)
        pooled_ref[...] = (sums / cnt - vp).astype(BF16).reshape(pooled_ref.shape)
        halo_ref[...] = vp[tm - 16:tm]


def _in_proj(x3, shift, scale, g1, w2, *, tm, prompt):
    g, r, d = x3.shape
    nt = r // tm
    per_row = shift.ndim == 2

    def tok(width, dtype):
        return (jax.ShapeDtypeStruct((g, r, width), dtype),
                pl.BlockSpec((1, tm, width), lambda b, j: (b, j, 0)))

    def tok_t(rows, dtype):
        return (jax.ShapeDtypeStruct((g, rows, r), dtype),
                pl.BlockSpec((1, rows, tm), lambda b, j: (b, 0, j)))

    outs = [tok(POOL_W, F32)] + [tok(KVW, F32)] * 6 + [tok(2 * d, F32)]
    scratch = []
    if prompt:
        outs += [tok(KVW, BF16), tok(KVW, BF16), tok_t(KVW, BF16), tok_t(KVW, BF16)]
        outs.append((jax.ShapeDtypeStruct((g, N_HEADS, LANES, r), BF16),
                     pl.BlockSpec((1, N_HEADS, LANES, tm), lambda b, j: (b, 0, 0, j))))
        outs += [tok_t(LANES, F32), tok(POOL_W, BF16)]
        scratch.append(pltpu.VMEM((16, POOL_W), F32))
    else:
        outs += [tok(QPAD, BF16), tok(LANES, F32)]
    if per_row:
        mod_spec = lambda col: pl.BlockSpec((tm, d), lambda b, j, col=col: (b * nt + j, col))
    else:
        mod_spec = lambda col: pl.BlockSpec((1, 1, d), lambda b, j, col=col: (b, 0, col))
    kern = functools.partial(_in_proj_kernel, tm=tm, d=d, prompt=prompt)
    return pl.pallas_call(
        kern,
        out_shape=[o[0] for o in outs],
        grid=(g, nt),
        in_specs=[pl.BlockSpec((1, tm, d), lambda b, j: (b, j, 0)),
                  mod_spec(0), mod_spec(1),
                  pl.BlockSpec((1, d), lambda b, j: (0, 0)),
                  pl.BlockSpec(w2.shape, lambda b, j: (0, 0))],
        out_specs=[o[1] for o in outs],
        scratch_shapes=scratch,
        compiler_params=_cparams(("arbitrary", "arbitrary")),
        name="in_proj_prompt" if prompt else "in_proj_sample",
    )(x3, shift, scale, g1, w2)


def _compress_kernel(kc_ref, vc_ref, wk_ref, wv_ref, okc_ref, ovc_ref, sh_ref, *, nc):
    last = lax.broadcasted_iota(I32, (nc, 1), 0) == nc - 1
    for src, w_ref, dst in ((kc_ref, wk_ref, okc_ref), (vc_ref, wv_ref, ovc_ref)):
        head = jnp.zeros((nc, KVW), F32)
        tail = jnp.zeros((nc, KVW), F32)
        for r in range(CMP_STRIDE):
            rows = src[pl.ds(r, nc, stride=CMP_STRIDE), :]
            head = head + rows * w_ref[r:r + 1, :]
            tail = tail + rows * w_ref[CMP_STRIDE + r:CMP_STRIDE + r + 1, :]
        sh_ref[0:nc, :] = tail
        sh_ref[nc:nc + 8, :] = jnp.zeros((8, KVW), F32)
        out = jnp.where(last, 0.0, head + sh_ref[1:nc + 1, :])
        dst[...] = (out if dst is okc_ref else out.T).astype(BF16)


def _compress(kc, vc, wk, wv):
    b, s, _ = kc.shape
    nc = s // CMP_STRIDE
    big = pl.BlockSpec((None, s, KVW), lambda i: (i, 0, 0))
    wsp = pl.BlockSpec((CMP_BLOCK, KVW), lambda i: (0, 0))
    return pl.pallas_call(
        functools.partial(_compress_kernel, nc=nc),
        out_shape=[jax.ShapeDtypeStruct((b, nc, KVW), BF16), jax.ShapeDtypeStruct((b, KVW, nc), BF16)],
        grid=(b,),
        in_specs=[big, big, wsp, wsp],
        out_specs=[pl.BlockSpec((None, nc, KVW), lambda i: (i, 0, 0)),
                   pl.BlockSpec((None, KVW, nc), lambda i: (i, 0, 0))],
        scratch_shapes=[pltpu.VMEM((nc + 8, KVW), F32)],
        compiler_params=_cparams(("arbitrary",)),
        name="compress",
    )(kc, vc, wk, wv)


def _topk_mask(vals, blk_f, n_top, axis=1):
    sel = jnp.zeros(vals.shape, F32)
    big = float(vals.shape[axis])
    for _ in range(n_top):
        mx = jnp.max(vals, axis=axis, keepdims=True)
        first = jnp.min(jnp.where(vals == mx, blk_f, big), axis=axis, keepdims=True)
        hit = blk_f == first
        sel = jnp.where(hit, 1.0, sel)
        vals = jnp.where(hit, -jnp.inf, vals)
    return sel


def _topk_mask_by_rank(vals, blk, n_valid, n_top):
    rank = jnp.zeros(vals.shape, F32)
    for j in range(n_valid):
        vj = vals[:, j:j + 1]
        beats = (vj > vals) | ((vj == vals) & (blk > j))
        rank = rank + jnp.where(beats, 1.0, 0.0)
    return jnp.where(rank < float(n_top), 1.0, 0.0)


def _pos_features(pos):
    hi = (pos // SEL_BLOCK).astype(F32)[:, None]
    lo = (pos % SEL_BLOCK).astype(F32)[:, None]
    return jnp.concatenate([hi, lo, jnp.zeros((pos.shape[0], LANES - 2), F32)], axis=1).astype(BF16)


def _importance_matrix(nc, nsel):
    j = jnp.arange(nc)[:, None]
    s = jnp.arange(nsel)[None, :]
    r = SEL_BLOCK // CMP_STRIDE
    a = (j >= r * s) & (j <= r * s + r - 1)
    b = (j + 1 >= r * s) & (j + 1 <= r * s + r - 1)
    return a.astype(F32) + b.astype(F32)


def _nsa_prompt_kernel(qt_ref, gst_ref, kc_ref, vct_ref, ks_ref, vst_ref, kw_ref, vwt_ref,
                       cfeat_ref, wfeat_ref, qfeat_ref, slope_ref,
                       y_ref, qk_scr, m_scr, l_scr, acc_scr, o_scr, sel_scr, imp_scr, flag_scr, ids_scr,
                       *, seq, tk, wl):
    i = pl.program_id(1)
    q0 = i * Q_BLOCK
    nq = Q_BLOCK
    gq = GROUP * nq
    nc = kc_ref.shape[0]
    nsel = seq // SEL_BLOCK
    n_top = min(TOP_BLOCKS, nsel)
    blk_per_tile = tk // SEL_BLOCK
    qpos = q0 + lax.broadcasted_iota(I32, (1, nq), 1)
    gst = gst_ref[...]

    crow = lax.broadcasted_iota(I32, (nc, nq), 0)
    cend = crow * CMP_STRIDE + (CMP_BLOCK - 1)
    mask_c = qpos >= cend
    kc = jnp.concatenate([kc_ref[...], cfeat_ref[...]], axis=1)
    vct = vct_ref[...]
    blk = lax.broadcasted_iota(I32, (nsel, nq), 0)
    blk_f = blk.astype(F32)
    cur = qpos // SEL_BLOCK
    forced = (blk == 0) | (blk == cur) | (blk == cur - 1)
    visible = blk * SEL_BLOCK <= qpos
    ws = pl.multiple_of(jnp.maximum(q0 - WINDOW, 0), Q_BLOCK)
    wpos = ws + lax.broadcasted_iota(I32, (wl, nq), 0)
    valid_w = lax.bitcast_convert_type(qpos - wpos, jnp.uint32) < WINDOW
    n_tiles = (q0 + nq + tk - 1) // tk
    half_rows = lax.broadcasted_iota(I32, (KVW, nq), 0) // HEAD_DIM
    tile_pos = lax.broadcasted_iota(I32, (SEL_BLOCK, nq), 0)

    def lanes4(x):
        return jnp.concatenate([x] * GROUP, axis=1)

    def gate_row(branch, k):
        r0 = branch * N_HEADS + k * GROUP
        return jnp.concatenate([gst[r0 + g:r0 + g + 1] for g in range(GROUP)], axis=1)

    mask_c4 = lanes4(mask_c)
    valid_w4 = lanes4(valid_w)
    kwt = jnp.concatenate([kw_ref[pl.ds(ws, wl), :], wfeat_ref[...]], axis=1)
    vwtt = vwt_ref[:, pl.ds(ws, wl)]

    for k in range(N_KV):
        for g in range(GROUP):
            qk_scr[k, 0:LANES, g * nq:(g + 1) * nq] = qt_ref[k * GROUP + g]
        qk_scr[k, LANES:2 * LANES, :] = qfeat_ref[k]
        qk = qk_scr[k]

        s = jnp.where(mask_c4, _dot(kc, qk), NEG)
        e = jnp.where(mask_c4, jnp.exp(s - jnp.max(s, axis=0, keepdims=True)), 0.0)
        l = jnp.sum(e, axis=0, keepdims=True)
        p = e * jnp.where(l > 0.0, 1.0 / l, 0.0)
        o_c = _dot(vct, p.astype(BF16))
        psum = p[:, 0:nq]
        for g in range(1, GROUP):
            psum = psum + p[:, g * nq:(g + 1) * nq]

        a = psum + jnp.where(crow == 0, 0.0, pltpu.roll(psum, 1, 0))
        a = a + pltpu.roll(a, nc - 1, 0)
        imp_scr[...] = a + pltpu.roll(a, nc - 2, 0)
        imp = imp_scr[pl.ds(0, nsel, stride=nc // nsel), :]
        vals = jnp.where(visible, jnp.where(forced, FORCE_SCORE, imp), NEG)
        sel = jnp.where(visible, _topk_mask(vals, blk_f, n_top, axis=0), 0.0)
        sel_scr[k] = jnp.where(sel > 0.5, 0.0, NEG)
        blk_any = jnp.max(sel, axis=1, keepdims=True)
        for t in range(seq // tk):
            hit = (jnp.max(blk_any[t * blk_per_tile:(t + 1) * blk_per_tile, :]) > 0.5).astype(I32)
            flag_scr[t] = hit if k == 0 else flag_scr[t] | hit

        s = jnp.where(valid_w4, _dot(kwt, qk), NEG)
        e = jnp.exp(s - jnp.max(s, axis=0, keepdims=True))
        p = e / jnp.sum(e, axis=0, keepdims=True)
        o_w = _dot(vwtt, p.astype(BF16))
        o_scr[k] = gate_row(0, k) * o_c + gate_row(2, k) * o_w

    m_scr[...] = jnp.full(m_scr.shape, NEG, F32)
    l_scr[...] = jnp.zeros(l_scr.shape, F32)
    acc_scr[...] = jnp.zeros(acc_scr.shape, F32)

    n_act = jnp.int32(0)
    for t in range(seq // tk):
        ids_scr[n_act] = t
        n_act = n_act + jnp.where((flag_scr[t] > 0) & (t < n_tiles), 1, 0)
    lane0 = lax.broadcasted_iota(I32, (tk, LANES), 1) == 0
    feat = wfeat_ref[0:tk, :]

    def sel_tiles(tiles):
        ta = tiles[0]
        starts = [pl.multiple_of(t * tk, tk) for t in tiles]
        keys = [jnp.concatenate([ks_ref[pl.ds(starts[0], tk), :], feat], axis=1)]
        for t, k0 in zip(tiles[1:], starts[1:]):
            shifted = jnp.where(lane0, feat.astype(F32) + ((t - ta) * blk_per_tile).astype(F32), feat.astype(F32))
            keys.append(jnp.concatenate([ks_ref[pl.ds(k0, tk), :], shifted.astype(BF16)], axis=1))
        kt = jnp.concatenate(keys, axis=0)
        vtt = jnp.concatenate([vst_ref[:, pl.ds(k0, tk)] for k0 in starts], axis=1)
        base = (starts[0] - q0).astype(F32)
        scores = _dot(kt, jnp.concatenate([qk_scr[k] for k in range(N_KV)], axis=1))
        probs, alphas = [], []
        for k in range(N_KV):
            neg = []
            for t, k0 in zip(tiles, starts):
                for j in range(blk_per_tile):
                    row = sel_scr[k, pl.ds(t * blk_per_tile + j, 1), :]
                    causal = qpos >= k0 + j * SEL_BLOCK + tile_pos
                    neg.append(jnp.where(causal, jnp.broadcast_to(row, (SEL_BLOCK, nq)), NEG))
            neg = lanes4(jnp.concatenate(neg, axis=0))
            off = slope_ref[k] * base
            s = scores[:, k * gq:(k + 1) * gq] + neg
            m_old = m_scr[k]
            m_new = jnp.maximum(m_old, jnp.max(s, axis=0, keepdims=True) + off)
            alpha = jnp.exp(m_old - m_new)
            p = jnp.exp(s - (m_new - off))
            l_scr[k] = alpha * l_scr[k] + jnp.sum(p, axis=0, keepdims=True)
            m_scr[k] = m_new
            probs.append(p.astype(BF16))
            alphas.append(alpha)
        pv = _dot(vtt, jnp.concatenate(probs, axis=1))
        for k in range(N_KV):
            acc_scr[k] = acc_scr[k] * alphas[k] + pv[:, k * gq:(k + 1) * gq]

    def sel_pair(i, carry):
        sel_tiles([ids_scr[2 * i], ids_scr[2 * i + 1]])
        return carry

    lax.fori_loop(0, n_act // 2, sel_pair, 0)
    pl.when(n_act % 2 == 1)(lambda: sel_tiles([ids_scr[n_act - 1]]))

    for k in range(N_KV):
        o = o_scr[k] + gate_row(1, k) * (acc_scr[k] / l_scr[k])
        for g in range(GROUP):
            h = k * GROUP + g
            oh = jnp.where(half_rows == k, o[:, g * nq:(g + 1) * nq], 0.0)
            y_ref[:, h * LANES:(h + 1) * LANES] = oh.T.astype(BF16)


def _nsa_prompt(qt, gst, kcmp, vcmpt, ksb, vst, kwb, vwt):
    b, _, _, s = qt.shape
    nq = Q_BLOCK
    gq = GROUP * nq
    nc = kcmp.shape[1]
    nsel = s // SEL_BLOCK
    tk = 256
    wl = WINDOW + Q_BLOCK
    assert s % tk == 0 and s >= wl
    assert s // SEL_BLOCK <= 2 * LANES, "position // 64 must stay exact in bf16"
    cfeat = _pos_features(jnp.arange(nc) * CMP_STRIDE + (CMP_BLOCK - 1))
    wfeat = _pos_features(jnp.arange(wl))
    slope_rows = jnp.repeat(jnp.asarray(SLOPES, F32).reshape(N_KV, 1, GROUP), nq, axis=2)
    qfeat = jnp.concatenate([slope_rows * SEL_BLOCK, slope_rows, jnp.zeros((N_KV, LANES - 2, gq), F32)],
                            axis=1).astype(BF16)
    rows = lambda r: pl.BlockSpec((None, r, KVW), lambda bi, i: (bi, 0, 0))
    cols = lambda c: pl.BlockSpec((None, KVW, c), lambda bi, i: (bi, 0, 0))
    const = lambda a: pl.BlockSpec(a.shape, lambda bi, i: (0,) * a.ndim)
    return pl.pallas_call(
        functools.partial(_nsa_prompt_kernel, seq=s, tk=tk, wl=wl),
        out_shape=jax.ShapeDtypeStruct((b, s, QPAD), BF16),
        grid=(b, s // nq),
        in_specs=[pl.BlockSpec((None, N_HEADS, LANES, nq), lambda bi, i: (bi, 0, 0, i)),
                  pl.BlockSpec((None, LANES, nq), lambda bi, i: (bi, 0, i)),
                  rows(nc), cols(nc), rows(s), cols(s), rows(s), cols(s),
                  const(cfeat), const(wfeat), const(qfeat), const(slope_rows)],
        out_specs=pl.BlockSpec((None, nq, QPAD), lambda bi, i: (bi, i, 0)),
        scratch_shapes=[pltpu.VMEM((N_KV, 2 * LANES, gq), BF16),
                        pltpu.VMEM((N_KV, 1, gq), F32),
                        pltpu.VMEM((N_KV, 1, gq), F32),
                        pltpu.VMEM((N_KV, KVW, gq), F32),
                        pltpu.VMEM((N_KV, KVW, gq), F32),
                        pltpu.VMEM((N_KV, nsel, nq), F32),
                        pltpu.VMEM((nc, nq), F32),
                        pltpu.SMEM((s // tk,), I32),
                        pltpu.SMEM((s // tk,), I32)],
        compiler_params=_cparams(("arbitrary", "arbitrary")),
        name="nsa_prompt",
    )(qt, gst, kcmp, vcmpt, ksb, vst, kwb, vwt, cfeat, wfeat, qfeat, slope_rows)


def _nsa_sample_kernel(pt_ref, q_ref, gate_ref, kcn_ref, vcn_ref, ksn_ref, vsn_ref, kwn_ref, vwn_ref,
                       vpn_ref, skw_ref, svw_ref, spool_ref, wk_ref, wv_ref, imat_ref, emat_ref,
                       ckc_ref, cvc_ref, cks_ref, cvs_ref,
                       o_ref, pooled_ref, kwo_ref, vwo_ref, buf, buft, win_scr, tail_scr, vext_scr, sem,
                       *, sb, ns, past, n_pages, page, ncv, ncp, nks, wls, nselp, n_sel):
    step = pl.program_id(0)
    nrow = ns * N_HEADS
    par = step % 2

    def copies(n, side, r):
        out = []
        for p in range(n_pages):
            pg = pt_ref[n * n_pages + p]
            for c, cref in enumerate((ckc_ref, cvc_ref, cks_ref, cvs_ref)):
                out.append(pltpu.make_async_copy(cref.at[pg], buft.at[side, r, c, :, pl.ds(p * page, page)],
                                                 sem.at[side, r]))
        return out

    @pl.when(step == 0)
    def _():
        buf[:, :, past:, :] = jnp.zeros((sb, 2, buf.shape[2] - past, KVW), F32)
        tail_scr[...] = jnp.zeros_like(tail_scr)
        vext_scr[...] = jnp.zeros_like(vext_scr)
        for r in range(sb):
            for cp in copies(r, 0, r):
                cp.start()

    @pl.when(step + 1 < pl.num_programs(0))
    def _():
        for r in range(sb):
            for cp in copies((step + 1) * sb + r, 1 - par, r):
                cp.start()

    for r in range(sb):
        for cp in copies(step * sb + r, par, r):
            cp.wait()

    def new_rows_t(ref, r4, r):
        tail_scr[r, 0:ns, :] = ref[pl.ds(r4, ns), :]
        return tail_scr[r].T

    row = lax.broadcasted_iota(I32, (nrow, 1), 0)
    hrow = row % N_HEADS
    qpos = past + row // N_HEADS
    slope = jnp.exp2(-8.0 * (hrow.astype(F32) + 1.0) / N_HEADS)
    kvrow = hrow // GROUP
    lane = lax.broadcasted_iota(I32, (1, LANES), 1)
    half = (lane // HEAD_DIM) == kvrow
    grow = (row // N_HEADS) * N_KV + kvrow
    row8 = lax.broadcasted_iota(I32, (ns * N_KV, 1), 0)
    qpos8 = past + lax.broadcasted_iota(I32, (ns * N_KV, 1), 0) // N_KV
    blk = lax.broadcasted_iota(I32, (1, nselp), 1)
    blk_f = blk.astype(F32)
    cur = qpos8 // SEL_BLOCK
    forced = (blk == 0) | (blk == cur) | (blk == cur - 1)
    visible = (blk * SEL_BLOCK <= qpos8)
    inrange = blk < n_sel
    cend = lax.broadcasted_iota(I32, (1, ncp), 1) * CMP_STRIDE + (CMP_BLOCK - 1)
    mask_c = qpos >= cend
    bias_c = slope * (cend - qpos).astype(F32)
    kpos = lax.broadcasted_iota(I32, (1, nks), 1)
    causal_s = qpos >= kpos
    bias_s = slope * (kpos - qpos).astype(F32)
    wbuf = wls[0]
    wpos = past - wbuf + lax.broadcasted_iota(I32, (1, wls[1]), 1)
    dw = qpos - wpos
    valid_w = lax.bitcast_convert_type(dw, jnp.uint32) < WINDOW
    bias_w = slope * (wpos - qpos).astype(F32)
    prow = lax.broadcasted_iota(I32, (vext_scr.shape[1], 1), 0)
    wcol = 2 << (lax.broadcasted_iota(I32, (1, POOL_W), 1) // POOL_GW)
    n_top = min(TOP_BLOCKS, n_sel)

    def softmax_rows(s, mask):
        s = jnp.where(mask, s, NEG)
        mx = jnp.max(s, axis=1, keepdims=True)
        e = jnp.where(mask, jnp.exp(s - mx), 0.0)
        l = jnp.sum(e, axis=1, keepdims=True)
        return e * jnp.where(l > 0.0, 1.0 / l, 0.0)

    def seq_body(r):
        r4 = r * ns
        for c, new_ref in enumerate((kcn_ref, vcn_ref)):
            for p in range(n_pages):
                buf[r, c, p * page:(p + 1) * page, :] = buft[par, r, c, :, p * page:(p + 1) * page].T
            buf[r, c, past:past + ns, :] = new_ref[pl.ds(r4, ns), :]
        for c, new_ref in ((2, ksn_ref), (3, vsn_ref)):
            buft[par, r, c, :, past:past + LANES] = new_rows_t(new_ref, r4, r)

        qall = q_ref[pl.ds(r * nrow, nrow), :]
        gates = gate_ref[pl.ds(r * nrow, nrow), :]

        cmp = []
        for c, w_ref in ((0, wk_ref), (1, wv_ref)):
            span = CMP_STRIDE * ncv
            lo = buf[r, c, 0:span, :].reshape(ncv, CMP_STRIDE, KVW) * w_ref[0:CMP_STRIDE, :][None]
            hi = (buf[r, c, CMP_STRIDE:CMP_STRIDE + span, :].reshape(ncv, CMP_STRIDE, KVW)
                  * w_ref[CMP_STRIDE:CMP_BLOCK, :][None])
            acc = jnp.sum(lo + hi, axis=1)
            cmp.append(jnp.concatenate([acc, jnp.zeros((ncp - ncv, KVW), F32)], axis=0).astype(BF16))
        p_c = softmax_rows(_dot_nt(qall, cmp[0]) + bias_c, mask_c)
        o_c = _dot(p_c.astype(BF16), cmp[1])

        psum = jnp.zeros((ns * N_KV, ncp), F32)
        for i in range(ns * N_KV):
            r0 = (i // N_KV) * N_HEADS + (i % N_KV) * GROUP
            psum = jnp.where(row8 == i, jnp.sum(p_c[r0:r0 + GROUP], axis=0, keepdims=True), psum)
        imp = _dot_exact(psum, imat_ref[...])
        vals = jnp.where(inrange, jnp.where(visible, jnp.where(forced, FORCE_SCORE, imp), NEG), -jnp.inf)
        sel8 = _topk_mask_by_rank(vals, blk, n_sel, n_top)
        sel_rows = jnp.zeros((nrow, nselp), F32)
        for i in range(ns * N_KV):
            sel_rows = jnp.where(grow == i, sel8[i:i + 1], sel_rows)
        chosen = _dot(sel_rows.astype(BF16), emat_ref[...])

        kst = buft[par, r, 2].astype(BF16)
        vst = buft[par, r, 3].astype(BF16)
        p_s = softmax_rows(_dot(qall, kst) + bias_s, causal_s & (chosen > 0.5))
        o_s = _dot_nt(p_s.astype(BF16), vst)

        outs_w = []
        for state_ref, new_ref, next_ref in ((skw_ref, kwn_ref, kwo_ref), (svw_ref, vwn_ref, vwo_ref)):
            win_scr[r, :, 0:wbuf] = state_ref[r]
            win_scr[r, :, wbuf:wbuf + LANES] = new_rows_t(new_ref, r4, r)
            outs_w.append(win_scr[r].astype(BF16))
            next_ref[r] = win_scr[r, :, ns:ns + wbuf]
        p_w = softmax_rows(_dot(qall, outs_w[0]) + bias_w, valid_w)
        o_w = _dot_nt(p_w.astype(BF16), outs_w[1])

        o = gates[:, 0:1] * o_c + gates[:, 1:2] * o_s + gates[:, 2:3] * o_w
        o_ref[pl.ds(r * nrow, nrow), :] = jnp.where(half, o, 0.0).astype(BF16)

        vext_scr[r, 0:POOL_BUF, :] = spool_ref[r]
        vext_scr[r, POOL_BUF:POOL_BUF + ns, :] = vpn_ref[pl.ds(r4, ns), :]
        ext = vext_scr[r]
        for t in range(ns):
            hi = POOL_BUF + t
            inwin = (prow <= hi) & (prow > hi - wcol)
            ssum = jnp.sum(jnp.where(inwin, ext, 0.0), axis=0, keepdims=True)
            cnt = jnp.minimum(past + t + 1, wcol).astype(F32)
            pooled_ref[pl.ds(r4 + t, 1), :] = ssum / cnt - ext[hi:hi + 1, :]

    for r in range(sb):
        seq_body(r)


def _nsa_sample(page_table, q_rows, gate_rows, new6, vp_new, state_kwt, state_vwt, state_pool, wk, wv, caches):
    n_seq, n_pages = page_table.shape
    page = caches[0].shape[2]
    past = n_pages * page
    ns = vp_new.shape[0] // n_seq
    wbuf = state_kwt.shape[2]
    sb = 2
    nrow = ns * N_HEADS
    assert ns <= SEL_BLOCK and page == LANES
    t_pad = -(-(past + ns) // SEL_BLOCK) * SEL_BLOCK
    n_cmp = t_pad // CMP_STRIDE - 1
    ncv = -(-n_cmp // 8) * 8
    ncp = -(-ncv // LANES) * LANES
    nks = past + LANES
    n_sel = t_pad // SEL_BLOCK
    nselp = LANES
    assert n_sel <= nselp
    wlp = wbuf + LANES
    buf_rows = -(-(CMP_STRIDE * ncv + CMP_STRIDE) // 8) * 8
    imat = _importance_matrix(ncp, nselp)
    emat = (jnp.arange(nselp)[:, None] == (jnp.arange(nks)[None, :] // SEL_BLOCK)).astype(BF16)

    seqblk = lambda rows, w: pl.BlockSpec((sb * rows, w), lambda i, pt: (i, 0))
    const = lambda a: pl.BlockSpec(a.shape, lambda i, pt: (0,) * a.ndim)
    kern = functools.partial(
        _nsa_sample_kernel, sb=sb, ns=ns, past=past, n_pages=n_pages, page=page,
        ncv=ncv, ncp=ncp, nks=nks, wls=(wbuf, wlp), nselp=nselp, n_sel=n_sel)
    grid_spec = pltpu.PrefetchScalarGridSpec(
        num_scalar_prefetch=1,
        grid=(n_seq // sb,),
        in_specs=[seqblk(nrow, LANES), seqblk(nrow, LANES)] + [seqblk(ns, KVW)] * 6 + [seqblk(ns, POOL_W)]
        + [pl.BlockSpec((sb, KVW, wbuf), lambda i, pt: (i, 0, 0))] * 2
        + [pl.BlockSpec((sb, POOL_BUF, POOL_W), lambda i, pt: (i, 0, 0))]
        + [const(wk), const(wv), const(imat), const(emat)]
        + [pl.BlockSpec(memory_space=pl.ANY)] * 4,
        out_specs=[seqblk(nrow, LANES), seqblk(ns, POOL_W)]
        + [pl.BlockSpec((sb, KVW, wbuf), lambda i, pt: (i, 0, 0))] * 2,
        scratch_shapes=[pltpu.VMEM((sb, 2, buf_rows, KVW), F32),
                        pltpu.VMEM((2, sb, 4, KVW, nks), F32),
                        pltpu.VMEM((sb, KVW, wlp), F32),
                        pltpu.VMEM((sb, LANES, KVW), F32),
                        pltpu.VMEM((sb, 24, POOL_W), F32),
                        pltpu.SemaphoreType.DMA((2, sb))],
    )
    return pl.pallas_call(
        kern,
        out_shape=[jax.ShapeDtypeStruct((n_seq * nrow, LANES), BF16),
                   jax.ShapeDtypeStruct((n_seq * ns, POOL_W), F32)]
        + [jax.ShapeDtypeStruct((n_seq, KVW, wbuf), F32)] * 2,
        grid_spec=grid_spec,
        compiler_params=_cparams(("arbitrary",)),
        name="nsa_sample",
    )(page_table.reshape(-1), q_rows, gate_rows, *new6, vp_new, state_kwt, state_vwt, state_pool, wk, wv,
      imat, emat, *caches)


def _route(logits_t, bias_col, tm):
    sc = jax.nn.sigmoid(logits_t)
    biased = sc + bias_col
    epg = EXPERTS_PER_GROUP
    row8 = lax.broadcasted_iota(I32, (epg, tm), 0).astype(F32)
    ninf = -jnp.inf
    grp = jnp.zeros((N_EGROUPS, tm), F32)
    for g in range(N_EGROUPS):
        bg = biased[g * epg:(g + 1) * epg]
        m1 = jnp.max(bg, axis=0, keepdims=True)
        first = jnp.min(jnp.where(bg == m1, row8, float(epg)), axis=0, keepdims=True)
        m2 = jnp.max(jnp.where(row8 == first, ninf, bg), axis=0, keepdims=True)
        grp = jnp.where(row8 == float(g), m1 + m2, grp)
    keep = jnp.zeros((N_EGROUPS, tm), F32)
    vals = grp
    for _ in range(TOPK_GROUPS):
        mx = jnp.max(vals, axis=0, keepdims=True)
        first = jnp.min(jnp.where(vals == mx, row8, float(N_EGROUPS)), axis=0, keepdims=True)
        hit = row8 == first
        keep = jnp.where(hit, 1.0, keep)
        vals = jnp.where(hit, ninf, vals)
    masked = jnp.concatenate(
        [jnp.where(keep[g:g + 1] > 0.5, biased[g * epg:(g + 1) * epg], NEG) for g in range(N_EGROUPS)], axis=0)
    rowe = lax.broadcasted_iota(I32, (N_EXPERTS, tm), 0).astype(F32)
    chosen = jnp.zeros((N_EXPERTS, tm), F32)
    vals = masked
    picks = []
    for _ in range(TOP_K):
        mx = jnp.max(vals, axis=0, keepdims=True)
        first = jnp.min(jnp.where(vals == mx, rowe, float(N_EXPERTS)), axis=0, keepdims=True)
        hit = rowe == first
        chosen = jnp.where(hit, sc, chosen)
        vals = jnp.where(hit, ninf, vals)
        picks.append((hit, first))
    return ROUTED_SCALE * chosen / jnp.sum(chosen, axis=0, keepdims=True), picks


def _pack_bf16_pairs(x):
    c = x.shape[1] // 2
    bits = lambda v: lax.bitcast_convert_type(v.astype(BF16).astype(F32), jnp.uint32)
    return (bits(x[:, :c]) >> 16) | (bits(x[:, c:]) & jnp.uint32(0xFFFF0000))


def _unpack_bf16_pairs(w):
    lo = lax.bitcast_convert_type(w << 16, F32)
    hi = lax.bitcast_convert_type(w & jnp.uint32(0xFFFF0000), F32)
    return jnp.concatenate([lo, hi], axis=1)


def _finish_kernel(x_ref, pooled_ref, y_ref, gm_ref, g1_ref, shift_ref, scale_ref,
                   wlin_ref, pscale_ref, wpo_ref, wno_ref, wo_ref, n2_ref, wr_ref, br_ref,
                   *rest, tm, d, sparse):
    if sparse:
        tri_ref, x1_ref, u2_ref, up_ref, eid_ref, gk_ref, rank_ref, cnt_ref, carry_scr = rest
    else:
        x1_ref, u2_ref, gates_ref = rest
    x = x_ref[...].reshape(tm, d)
    pooled = pooled_ref[...].reshape(tm, POOL_W).astype(BF16)
    y_pool = _dot(pooled, wlin_ref[...]) * pscale_ref[...]
    a = _dot(y_pool.astype(BF16), wpo_ref[...])
    b = _dot(y_ref[...].reshape(tm, QPAD), wno_ref[...])
    gm = gm_ref[...].reshape(tm, 2 * d)
    merged = gm[:, :d] * a + gm[:, d:] * b
    x1 = x + _rows2d(g1_ref) * _dot(merged.astype(BF16), wo_ref[...])
    x1_ref[...] = x1.reshape(x1_ref.shape)
    u2 = _rmsnorm(x1, n2_ref[...]) * (1.0 + _rows2d(scale_ref)) + _rows2d(shift_ref)
    u2b = u2.astype(BF16)
    u2_ref[...] = u2b.reshape(u2_ref.shape)
    logits_t = _dot_nt(wr_ref[...], u2b)
    gates_t, picks = _route(logits_t[:N_EXPERTS], br_ref[...], tm)
    if not sparse:
        gates_t = jnp.concatenate([gates_t, jnp.zeros((LANES - N_EXPERTS, tm), F32)], axis=0)
        gates_ref[...] = gates_t.T.reshape(gates_ref.shape)
        return

    @pl.when((pl.program_id(0) == 0) & (pl.program_id(1) == 0))
    def _():
        carry_scr[...] = jnp.zeros_like(carry_scr)

    packed = _pack_bf16_pairs(u2)
    for s in range(up_ref.shape[0]):
        up_ref[s] = packed[:, s * SC_ROW_WORDS:(s + 1) * SC_ROW_WORDS]
    hit_all = picks[0][0]
    for hit, _ in picks[1:]:
        hit_all = hit_all | hit
    hits = jnp.where(hit_all, 1.0, 0.0).astype(BF16)
    before = _dot(hits, tri_ref[...]) + jnp.concatenate([carry_scr[...]] * (tm // LANES), axis=1)
    eids, gks, ranks = [], [], []
    for hit, first in picks:
        eids.append(first)
        gks.append(jnp.sum(jnp.where(hit, gates_t, 0.0), axis=0, keepdims=True))
        ranks.append(jnp.sum(jnp.where(hit, before, 0.0), axis=0, keepdims=True))
    pick_row = lax.broadcasted_iota(I32, (TOP_K, tm), 0)

    def stack(rows):
        out = jnp.zeros((TOP_K, tm), F32)
        for r, v in enumerate(rows):
            out = jnp.where(pick_row == r, v, out)
        return out

    eid_ref[...] = stack(eids).astype(I32).reshape(eid_ref.shape)
    gk_ref[...] = jnp.concatenate([stack(gks), jnp.zeros((LANES - TOP_K, tm), F32)], axis=0).T.reshape(gk_ref.shape)
    rank_ref[...] = stack(ranks).astype(I32).reshape(rank_ref.shape)
    carry_scr[...] += _dot(hits, jnp.ones((tm, LANES), BF16))
    cnt_ref[...] = carry_scr[...]


def _finish(x3, pooled, ynsa, gm, mods, wts, *, tm, sparse):
    g, r, d = x3.shape
    nt = r // tm
    g1, shift2, scale2 = mods
    per_row = g1.ndim == 2
    tok = lambda w: pl.BlockSpec((1, tm, w), lambda b, j: (b, j, 0))
    tok_t = lambda rows: pl.BlockSpec((1, rows, tm), lambda b, j: (b, 0, j))
    if per_row:
        mod_spec = lambda col: pl.BlockSpec((tm, d), lambda b, j, col=col: (b * nt + j, col))
    else:
        mod_spec = lambda col: pl.BlockSpec((1, 1, d), lambda b, j, col=col: (b, 0, col))
    const = lambda a: pl.BlockSpec(a.shape, lambda b, j: (0,) * a.ndim)
    out_shape = [jax.ShapeDtypeStruct((g, r, d), F32), jax.ShapeDtypeStruct((g, r, d), BF16)]
    out_specs = [tok(d), tok(d)]
    scratch = []
    if sparse:
        tri = (jnp.arange(tm)[:, None] < jnp.arange(tm)[None, :]).astype(BF16)
        wts = tuple(wts) + (tri,)
        split = d // 2 // SC_ROW_WORDS
        out_shape += [jax.ShapeDtypeStruct((split, g * r, SC_ROW_WORDS), jnp.uint32),
                      jax.ShapeDtypeStruct((g, TOP_K, r), I32), jax.ShapeDtypeStruct((g, r, LANES), F32),
                      jax.ShapeDtypeStruct((g, TOP_K, r), I32), jax.ShapeDtypeStruct((N_EXPERTS, LANES), F32)]
        out_specs += [pl.BlockSpec((split, tm, SC_ROW_WORDS), lambda b, j: (0, b * nt + j, 0)),
                      tok_t(TOP_K), tok(LANES), tok_t(TOP_K),
                      pl.BlockSpec((N_EXPERTS, LANES), lambda b, j: (0, 0))]
        scratch.append(pltpu.VMEM((N_EXPERTS, LANES), F32))
    else:
        out_shape.append(jax.ShapeDtypeStruct((g, r, LANES), F32))
        out_specs.append(tok(LANES))
    return pl.pallas_call(
        functools.partial(_finish_kernel, tm=tm, d=d, sparse=sparse),
        out_shape=out_shape,
        grid=(g, nt),
        in_specs=[tok(d), tok(POOL_W), tok(QPAD), tok(2 * d), mod_spec(2), mod_spec(3), mod_spec(4)]
        + [const(w) for w in wts],
        out_specs=out_specs,
        scratch_shapes=scratch,
        compiler_params=_cparams(("arbitrary", "arbitrary")),
        name="finish_route" if sparse else "finish",
    )(x3, pooled, ynsa, gm, g1, shift2, scale2, *wts)


def _moe_kernel(u_ref, gates_ref, x1_ref, g2_ref, nf_ref, wg_ref, wu_ref, wd_ref, sg_ref, su_ref, sd_ref,
                y_ref, acc_ref, *, tm, d, eps):
    e = pl.program_id(2)
    u = u_ref[...].reshape(tm, d)

    @pl.when(e == 0)
    def _():
        hs = _silu(_dot(u, sg_ref[...])) * _dot(u, su_ref[...])
        acc_ref[...] = _dot(hs.astype(BF16), sd_ref[...])

    gates = gates_ref[...].reshape(tm, LANES)
    lane = lax.broadcasted_iota(I32, (1, LANES), 1)
    hidden = []
    for j in range(eps):
        h = _silu(_dot(u, wg_ref[j].astype(BF16))) * _dot(u, wu_ref[j].astype(BF16))
        gate = jnp.sum(jnp.where(lane == e * eps + j, gates, 0.0), axis=1, keepdims=True)
        hidden.append((h * gate).astype(BF16))
    f = wd_ref.shape[1]
    acc_ref[...] += _dot(jnp.concatenate(hidden, axis=1), wd_ref[...].reshape(eps * f, d).astype(BF16))

    @pl.when(e == pl.num_programs(2) - 1)
    def _():
        x2 = x1_ref[...].reshape(tm, d) + _rows2d(g2_ref) * acc_ref[...]
        y_ref[...] = _rmsnorm(x2, nf_ref[...]).reshape(y_ref.shape)


def _moe(u2, gates, x1, g2, normf, w_gate, w_up, w_down, sg, su, sd, *, tm):
    g, r, d = x1.shape
    nt = r // tm
    ne, _, f = w_gate.shape
    per_row = g2.ndim == 2
    tok = lambda w: pl.BlockSpec((1, tm, w), lambda b, j, e: (b, j, 0))
    if per_row:
        g2_spec = pl.BlockSpec((tm, d), lambda b, j, e: (b * nt + j, 5))
    else:
        g2_spec = pl.BlockSpec((1, 1, d), lambda b, j, e: (b, 0, 5))
    once = pl.Buffered(buffer_count=1)
    const = lambda a: pl.BlockSpec(a.shape, lambda b, j, e: (0,) * a.ndim, pipeline_mode=once)
    eps = 4
    return pl.pallas_call(
        functools.partial(_moe_kernel, tm=tm, d=d, eps=eps),
        out_shape=jax.ShapeDtypeStruct((g, r, d), F32),
        grid=(g, nt, ne // eps),
        in_specs=[tok(d), tok(LANES),
                  pl.BlockSpec((1, tm, d), lambda b, j, e: (b, j, 0), pipeline_mode=once),
                  g2_spec, const(normf),
                  pl.BlockSpec((eps, d, f), lambda b, j, e: (e, 0, 0)),
                  pl.BlockSpec((eps, d, f), lambda b, j, e: (e, 0, 0)),
                  pl.BlockSpec((eps, f, d), lambda b, j, e: (e, 0, 0)),
                  const(sg), const(su), const(sd)],
        out_specs=tok(d),
        scratch_shapes=[pltpu.VMEM((tm, d), F32)],
        compiler_params=_cparams(("arbitrary", "arbitrary", "arbitrary")),
        name="moe",
    )(u2, gates, x1, g2, normf, w_gate, w_up, w_down, sg, su, sd)


SC_WINDOW = 128
SC_ROW_WORDS = 256
MOE_ROWS = 512


def _sc_mesh():
    return plsc.VectorSubcoreMesh(core_axis_name="c", subcore_axis_name="s")


def _sc_scatter_rows(src, dst_idx, n_dst):
    n, w = src.shape
    nk = dst_idx.shape[0]

    @pl.kernel(out_type=jax.ShapeDtypeStruct((n_dst, w), src.dtype), mesh=_sc_mesh(), scratch_types=[])
    def scatter(src_hbm, idx_hbm, dst_hbm):
        def body(rows_vmem, idx_vmem):
            pltpu.sync_copy(rows_vmem, dst_hbm.at[idx_vmem.at[0]])

        pltpu.emit_pipeline(
            body,
            grid=(nk, n // SC_WINDOW),
            in_specs=[pl.BlockSpec((SC_WINDOW, w), index_map=lambda k, i: (i, 0)),
                      pl.BlockSpec((1, SC_WINDOW), index_map=lambda k, i: (k, i))],
            out_specs=[],
            core_axis_name=("c", "s"),
            dimension_semantics=(pltpu.PARALLEL, pltpu.PARALLEL),
        )(src_hbm, idx_hbm)

    return scatter(src, dst_idx)


def _sc_gather_rows(src, idx):
    n, w = idx.shape[0], src.shape[1]

    @pl.kernel(out_type=jax.ShapeDtypeStruct((n, w), src.dtype), mesh=_sc_mesh(), scratch_types=[])
    def gather(src_hbm, idx_hbm, out_hbm):
        def body(idx_vmem, out_vmem):
            pltpu.sync_copy(src_hbm.at[idx_vmem.at[0]], out_vmem)

        pltpu.emit_pipeline(
            body,
            grid=(n // SC_WINDOW,),
            in_specs=[pl.BlockSpec((1, SC_WINDOW), index_map=lambda i: (0, i))],
            out_specs=[pl.BlockSpec((SC_WINDOW, w), index_map=lambda i: (i, 0))],
            core_axis_name=("c", "s"),
            dimension_semantics=(pltpu.PARALLEL,),
        )(idx_hbm, out_hbm)

    return gather(src, idx.reshape(1, n))


MOE_BLOCKS_PER_STEP = 2


def _expert_rows_kernel(te_ref, nt_ref, x_ref, *refs):
    y_ref = refs[-1]
    split = x_ref.shape[0]
    for j in range(MOE_BLOCKS_PER_STEP):
        wg_ref, wu_ref, wd_ref = refs[3 * j:3 * j + 3]
        rows = slice(j * MOE_ROWS, (j + 1) * MOE_ROWS)

        @pl.when(pl.program_id(0) * MOE_BLOCKS_PER_STEP + j < nt_ref[0])
        def _():
            x = _unpack_bf16_pairs(jnp.concatenate([x_ref[s, rows] for s in range(split)], axis=1)).astype(BF16)
            h = _silu(_dot(x, wg_ref[...].astype(BF16))) * _dot(x, wu_ref[...].astype(BF16))
            y = _pack_bf16_pairs(_dot(h.astype(BF16), wd_ref[...].astype(BF16)))
            for s in range(split):
                y_ref[s, rows] = y[:, s * SC_ROW_WORDS:(s + 1) * SC_ROW_WORDS]


def _expert_rows(tile_expert, n_tiles, x_sorted, w_gate, w_up, w_down):
    split, p, words = x_sorted.shape
    ne, d, f = w_gate.shape
    bps = MOE_BLOCKS_PER_STEP
    wspec = lambda a, b, j: pl.BlockSpec((None, a, b), lambda i, te, nt: (te[i * bps + j], 0, 0))
    rows = pl.BlockSpec((split, bps * MOE_ROWS, words), lambda i, te, nt: (0, i, 0))
    weights, wspecs = [], []
    for j in range(bps):
        weights += [w_gate, w_up, w_down]
        wspecs += [wspec(d, f, j), wspec(d, f, j), wspec(f, d, j)]
    grid_spec = pltpu.PrefetchScalarGridSpec(
        num_scalar_prefetch=2,
        grid=(p // (bps * MOE_ROWS),),
        in_specs=[rows] + wspecs,
        out_specs=rows,
    )
    return pl.pallas_call(
        _expert_rows_kernel,
        out_shape=jax.ShapeDtypeStruct((split, p, words), jnp.uint32),
        grid_spec=grid_spec,
        compiler_params=_cparams(("arbitrary",)),
        name="expert_rows",
    )(tile_expert, n_tiles, x_sorted, *weights)


def _combine_kernel(yg_ref, gk_ref, u_ref, x1_ref, g2_ref, nf_ref, sg_ref, su_ref, sd_ref, y_ref, *, tm, d):
    u = u_ref[...].reshape(tm, d)
    hs = _silu(_dot(u, sg_ref[...])) * _dot(u, su_ref[...])
    acc = _dot(hs.astype(BF16), sd_ref[...])
    gk = gk_ref[...].reshape(tm, LANES)
    lane = lax.broadcasted_iota(I32, (1, LANES), 1)
    split = yg_ref.shape[0]
    for k in range(TOP_K):
        gate = jnp.sum(jnp.where(lane == k, gk, 0.0), axis=1, keepdims=True)
        words = jnp.concatenate([yg_ref[s, k] for s in range(split)], axis=1)
        acc = acc + gate * _unpack_bf16_pairs(words)
    x2 = x1_ref[...].reshape(tm, d) + _rows2d(g2_ref) * acc
    y_ref[...] = _rmsnorm(x2, nf_ref[...]).reshape(y_ref.shape)


def _combine(yg, gk, u2, x1, g2, normf, sg, su, sd, *, tm):
    g, r, d = x1.shape
    nt = r // tm
    split, _, _, words = yg.shape
    tok = lambda w: pl.BlockSpec((1, tm, w), lambda b, j: (b, j, 0))
    const = lambda a: pl.BlockSpec(a.shape, lambda b, j: (0,) * a.ndim)
    return pl.pallas_call(
        functools.partial(_combine_kernel, tm=tm, d=d),
        out_shape=jax.ShapeDtypeStruct((g, r, d), F32),
        grid=(g, nt),
        in_specs=[pl.BlockSpec((split, TOP_K, tm, words), lambda b, j: (0, 0, b * nt + j, 0)),
                  tok(LANES), tok(d), tok(d),
                  pl.BlockSpec((1, 1, d), lambda b, j: (b, 0, 5)), const(normf), const(sg), const(su), const(sd)],
        out_specs=tok(d),
        compiler_params=_cparams(("arbitrary", "arbitrary")),
        name="moe_combine",
    )(yg, gk, u2, x1, g2, normf, sg, su, sd)


def _moe_sorted_experts(u2p, eid_t, rank_t, counts, w_gate, w_up, w_down):
    split, n, _ = u2p.shape
    ne = w_gate.shape[0]
    cnt = counts[:, 0].astype(I32)
    padded = -(-cnt // MOE_ROWS) * MOE_ROWS
    seg_end = jnp.cumsum(padded)
    seg_start = seg_end - padded
    p_rows = n * TOP_K + ne * MOE_ROWS
    eid = eid_t.transpose(1, 0, 2).reshape(TOP_K, n)
    start = jnp.sum(jnp.where(eid[:, :, None] == jnp.arange(ne, dtype=I32), seg_start, 0), axis=-1)
    pos = start + rank_t.transpose(1, 0, 2).reshape(TOP_K, n)
    first_row = jnp.arange(p_rows // MOE_ROWS, dtype=I32) * MOE_ROWS
    tile_expert = jnp.minimum(jnp.sum(seg_end[None, :] <= first_row[:, None], axis=1), ne - 1).astype(I32)
    n_tiles = (seg_end[-1:] // MOE_ROWS).astype(I32)
    scat_idx = jnp.concatenate([pos + s * p_rows for s in range(split)], axis=1)
    gath_idx = jnp.concatenate([pos.reshape(-1) + s * p_rows for s in range(split)])
    x_sorted = _sc_scatter_rows(u2p.reshape(split * n, SC_ROW_WORDS), scat_idx, split * p_rows)
    y_sorted = _expert_rows(tile_expert, n_tiles, x_sorted.reshape(split, p_rows, SC_ROW_WORDS),
                            w_gate, w_up, w_down)
    return y_sorted.reshape(split * p_rows, SC_ROW_WORDS), gath_idx


def _moe_sorted_combine(y_sorted, gath_idx, gk, u2, x1, g2, normf, sg, su, sd):
    g, r, _ = x1.shape
    split = gath_idx.shape[0] // (TOP_K * g * r)
    yg = _sc_gather_rows(y_sorted, gath_idx)
    return _combine(yg.reshape(split, TOP_K, g * r, SC_ROW_WORDS), gk, u2, x1, g2, normf, sg, su, sd, tm=512)


def _kv_slot_mask():
    return (jnp.arange(N_HEADS)[:, None] // GROUP == jnp.arange(N_KV)[None, :]).astype(F32)


def _prep_w_in(w_in, d):
    q0 = POOL_W
    kv0 = q0 + N_HEADS * HEAD_DIM
    gn0 = kv0 + 6 * KVW
    gm0 = gn0 + 3 * N_HEADS
    wq = w_in[:, q0:kv0].reshape(d, N_HEADS, 1, HEAD_DIM) * (HEAD_DIM ** -0.5)
    wq = (wq * _kv_slot_mask()[None, :, :, None]).reshape(d, QPAD)
    wgn = jnp.pad(w_in[:, gn0:gm0], ((0, 0), (0, LANES - 3 * N_HEADS)))
    return jnp.concatenate([w_in[:, :q0], wq, w_in[:, kv0:gn0], wgn, w_in[:, gm0:]], axis=1).astype(BF16)


def _prep_w_nsa_out(w, d):
    w = w.reshape(N_HEADS, 1, HEAD_DIM, d) * _kv_slot_mask()[:, :, None, None]
    return w.reshape(QPAD, d).astype(BF16)


def _block_diag(w_lin):
    g, c, _ = w_lin.shape
    eye = jnp.eye(g, dtype=F32)
    return (w_lin[:, :, None, :] * eye[:, None, :, None]).reshape(g * c, g * c).astype(BF16)


def kernel(x_prompt, x_sample, cache_kc, cache_vc, cache_ks, cache_vs, state_kw, state_vw, state_pool,
           page_table, c_prompt, c_sample, norm1_g, norm2_g, normf_g, w_ada, b_ada, w_in, w_pool_lin,
           pool_scale, w_cmp_k, w_cmp_v, w_pool_out, w_nsa_out, w_o, w_router, b_router, w_gate, w_up,
           w_down, ws_gate, ws_up, ws_down):
    depth = w_in.shape[0]
    assert depth == 1, "single-layer stack"
    bsz, seq, d = x_prompt.shape
    n_seq, ns, _ = x_sample.shape
    wbuf = state_kw.shape[2]
    lyr = 0

    n_tok = n_seq * ns
    c_all = jnp.concatenate([jnp.repeat(c_sample, ns, axis=0), c_prompt, jnp.zeros((-bsz % 8, d), F32)], axis=0)
    mod = _adaln(c_all, w_ada[lyr], b_ada[lyr])
    mod_p = mod[n_tok:n_tok + bsz].reshape(bsz, 1, 6 * d)
    mod_s = mod

    w2 = _prep_w_in(w_in[lyr], d)
    g1n = norm1_g[lyr].reshape(1, d)
    wk = w_cmp_k[lyr].reshape(CMP_BLOCK, KVW)
    wv = w_cmp_v[lyr].reshape(CMP_BLOCK, KVW)
    fin_w = (_block_diag(w_pool_lin[lyr]), pool_scale[lyr].reshape(1, POOL_W), w_pool_out[lyr].astype(BF16),
             _prep_w_nsa_out(w_nsa_out[lyr], d), w_o[lyr].astype(BF16), norm2_g[lyr].reshape(1, d),
             jnp.pad(w_router[lyr].T, ((0, LANES - N_EXPERTS), (0, 0))).astype(BF16),
             b_router[lyr].reshape(N_EXPERTS, 1))
    moe_w = (w_gate[lyr], w_up[lyr], w_down[lyr], ws_gate[lyr].astype(BF16), ws_up[lyr].astype(BF16),
             ws_down[lyr].astype(BF16))
    nf = normf_g.reshape(1, d)

    tm_p = 512
    (vp, kc, vc, ks, vs, kw, vw, gm, ksb, kwb, vst, vwt, qt, gst, pooled) = _in_proj(
        x_prompt, mod_p, mod_p, g1n, w2, tm=tm_p, prompt=True)
    kcmp, vcmpt = _compress(kc, vc, wk, wv)
    ynsa = _nsa_prompt(qt, gst, kcmp, vcmpt, ksb, vst, kwb, vwt)
    x1, u2, u2p, eid_t, gk_t, rank_t, counts = _finish(
        x_prompt, pooled, ynsa, gm, (mod_p, mod_p, mod_p), fin_w, tm=tm_p, sparse=True)

    xs3 = x_sample.reshape(1, n_tok, d)
    tm_s = 128
    (vp_s, kc_s, vc_s, ks_s, vs_s, kw_s, vw_s, gm_s, q_s, gs_s) = _in_proj(
        xs3, mod_s, mod_s, g1n, w2, tm=tm_s, prompt=False)
    two = lambda a: a.reshape(n_tok, a.shape[-1])
    q_rows = q_s.reshape(n_tok * N_HEADS, LANES)
    gate_rows = two(gs_s)[:, :3 * N_HEADS].reshape(n_tok, 3, N_HEADS).transpose(0, 2, 1)
    gate_rows = jnp.pad(gate_rows.reshape(n_tok * N_HEADS, 3), ((0, 0), (0, LANES - 3)))
    n_pool = cache_kc.shape[1]
    page = cache_kc.shape[2]
    rows_minor = lambda a: jnp.transpose(a, (0, 2, 3, 1)).reshape(a.shape[0], KVW, a.shape[1])
    caches = [rows_minor(c[lyr]) for c in (cache_kc, cache_vc, cache_ks, cache_vs)]
    o_rows, pooled_s, kw_next, vw_next = _nsa_sample(
        page_table, q_rows, gate_rows, [two(a) for a in (kc_s, vc_s, ks_s, vs_s, kw_s, vw_s)], two(vp_s),
        rows_minor(state_kw[lyr]), rows_minor(state_vw[lyr]), state_pool[lyr], wk, wv, caches)
    ynsa_s = o_rows.reshape(1, n_tok, QPAD)
    y_sorted, gath_idx = _moe_sorted_experts(u2p, eid_t, rank_t, counts, *moe_w[:3])
    ynsa_s, y_sorted = lax.optimization_barrier((ynsa_s, y_sorted))
    y_prompt = _moe_sorted_combine(y_sorted, gath_idx, gk_t, u2, x1, mod_p, nf, *moe_w[3:])
    x1_s, u2_s, gates_s = _finish(xs3, pooled_s.reshape(1, n_tok, POOL_W), ynsa_s, gm_s,
                                  (mod_s, mod_s, mod_s), fin_w, tm=tm_s, sparse=False)
    y_sample = _moe(u2_s, gates_s, x1_s, mod_s, nf, *moe_w, tm=n_tok).reshape(n_seq, ns, d)

    kvp = lambda a: a.reshape(1, bsz, seq, N_KV, HEAD_DIM)
    tailp = lambda a: jnp.pad(a, ((0, 0), (wbuf, 0), (0, 0)))[:, -wbuf:].reshape(1, bsz, wbuf, N_KV, HEAD_DIM)
    kvs = lambda a: a.reshape(1, n_seq, ns, N_KV, HEAD_DIM)
    wins = lambda a: jnp.transpose(a.reshape(n_seq, N_KV, HEAD_DIM, wbuf), (0, 3, 1, 2))[None]
    pool_p = vp[:, -POOL_BUF:][None]
    pool_s = jnp.concatenate([state_pool[lyr], vp_s.reshape(n_seq, ns, POOL_W)], axis=1)[None, :, -POOL_BUF:]
    return (y_prompt, y_sample, kvp(kc), kvp(vc), kvp(ks), kvp(vs), tailp(kw), tailp(vw), pool_p,
            kvs(kc_s), kvs(vc_s), kvs(ks_s), kvs(vs_s), wins(kw_next), wins(vw_next), pool_s)
```

```python
import functools

import jax
import jax.numpy as jnp
from jax import lax
from jax.experimental import pallas as pl
from jax.experimental.pallas import tpu as pltpu
from jax.experimental.pallas import tpu_sc as plsc

F32 = jnp.float32
BF16 = jnp.bfloat16
I32 = jnp.int32

POOL_WINDOWS = (2, 4, 8, 16)
POOL_GW = 64
POOL_W = 256
POOL_BUF = 15
N_HEADS = 8
HEAD_DIM = 64
N_KV = 2
GROUP = N_HEADS // N_KV
CMP_STRIDE = 16
CMP_BLOCK = 32
SEL_BLOCK = 64
TOP_BLOCKS = 16
WINDOW = 512
Q_BLOCK = 128
FORCE_SCORE = 1e4
N_EXPERTS = 64
N_EGROUPS = 8
EXPERTS_PER_GROUP = N_EXPERTS // N_EGROUPS
TOPK_GROUPS = 4
TOP_K = 8
ROUTED_SCALE = 2.5
EPS = 1e-6
NEG = -1e30
SLOPES = tuple(2.0 ** (-8.0 * (h + 1.0) / N_HEADS) for h in range(N_HEADS))

LANES = 128
QPAD = N_HEADS * LANES
KVW = N_KV * HEAD_DIM
VMEM_LIMIT = 56 * 1024 * 1024


def _cparams(sem):
    return pltpu.CompilerParams(dimension_semantics=sem, vmem_limit_bytes=VMEM_LIMIT)


def _dot(a, b):
    return jnp.dot(a, b, preferred_element_type=F32)


def _dot_nt(a, b):
    return lax.dot_general(a, b, (((1,), (1,)), ((), ())), preferred_element_type=F32)


def _dot_exact(a, b):
    return jnp.dot(a, b, preferred_element_type=F32, precision=lax.Precision.HIGHEST)


def _rows2d(ref):
    v = ref[...]
    return v.reshape(v.shape[-2], v.shape[-1])


def _rmsnorm(x, g):
    return x * lax.rsqrt(jnp.mean(x * x, axis=-1, keepdims=True) + EPS) * g


def _silu(x):
    return x * jax.nn.sigmoid(x)


def _adaln_kernel(c_ref, w_ref, b_ref, o_ref):
    s = _silu(c_ref[...]).astype(BF16)
    o_ref[...] = _dot(s, w_ref[...].astype(BF16)) + b_ref[...]


def _adaln(c, w_ada, b_ada):
    rows, d = c.shape
    n = w_ada.shape[1]
    tn = 512
    return pl.pallas_call(
        _adaln_kernel,
        out_shape=jax.ShapeDtypeStruct((rows, n), F32),
        grid=(n // tn,),
        in_specs=[pl.BlockSpec((rows, d), lambda j: (0, 0)),
                  pl.BlockSpec((d, tn), lambda j: (0, j)),
                  pl.BlockSpec((1, tn), lambda j: (0, j))],
        out_specs=pl.BlockSpec((rows, tn), lambda j: (0, j)),
        compiler_params=_cparams(("arbitrary",)),
        name="adaln",
    )(c, w_ada, b_ada.reshape(1, n))


_C_VP = 0
_C_Q = _C_VP + POOL_W
_C_KV = _C_Q + QPAD
_C_GN = _C_KV + 6 * KVW
_C_GM = _C_GN + LANES


def _pool_window_sums(ext, tm):
    s2 = ext + pltpu.roll(ext, 1, 0)
    s4 = s2 + pltpu.roll(s2, 2, 0)
    s8 = s4 + pltpu.roll(s4, 4, 0)
    s16 = s8 + pltpu.roll(s8, 8, 0)
    grp = lax.broadcasted_iota(I32, (1, POOL_W), 1) // POOL_GW
    pick = jnp.where(grp == 0, s2, jnp.where(grp == 1, s4, jnp.where(grp == 2, s8, s16)))
    return pick[16:16 + tm]


def _in_proj_kernel(x_ref, shift_ref, scale_ref, g_ref, w_ref,
                    vp_ref, kc_ref, vc_ref, ks_ref, vs_ref, kw_ref, vw_ref, gm_ref, *rest, tm, d, prompt):
    x = x_ref[...].reshape(tm, d)
    u = _rmsnorm(x, g_ref[...]) * (1.0 + _rows2d(scale_ref)) + _rows2d(shift_ref)
    ub = u.astype(BF16)

    head = _dot(ub, w_ref[:, 0:_C_GM])

    def proj(c0, n):
        return head[:, c0:c0 + n] if c0 + n <= _C_GM else _dot(ub, w_ref[:, c0:c0 + n])

    vp = proj(_C_VP, POOL_W)
    vp_ref[...] = vp.reshape(vp_ref.shape)
    kv = []
    for n, o32 in enumerate((kc_ref, vc_ref, ks_ref, vs_ref, kw_ref, vw_ref)):
        v = proj(_C_KV + n * KVW, KVW)
        o32[...] = v.reshape(o32.shape)
        kv.append(v)
    gm_ref[...] = jax.nn.sigmoid(proj(_C_GM, 2 * d)).reshape(gm_ref.shape)
    gs = jax.nn.sigmoid(proj(_C_GN, LANES))

    if not prompt:
        q_ref, gs_ref = rest
        q_ref[...] = proj(_C_Q, QPAD).astype(BF16).reshape(q_ref.shape)
        gs_ref[...] = gs.reshape(gs_ref.shape)
    else:
        ksb_ref, kwb_ref, vst_ref, vwt_ref, qt_ref, gst_ref, pooled_ref, halo_ref = rest
        ksb_ref[...] = kv[2].astype(BF16).reshape(ksb_ref.shape)
        kwb_ref[...] = kv[4].astype(BF16).reshape(kwb_ref.shape)
        vst_ref[...] = kv[3].T.astype(BF16).reshape(vst_ref.shape)
        vwt_ref[...] = kv[5].T.astype(BF16).reshape(vwt_ref.shape)
        gst_ref[...] = gs.T.reshape(gst_ref.shape)
        for h in range(N_HEADS):
            qt_ref[0, h] = proj(_C_Q + h * LANES, LANES).T.astype(BF16)
        j = pl.program_id(1)

        @pl.when(j == 0)
        def _():
            halo_ref[...] = jnp.zeros_like(halo_ref)

        ext = jnp.concatenate([halo_ref[...], vp], axis=0)
        sums = _pool_window_sums(ext, tm)
        pos = j * tm + lax.broadcasted_iota(I32, (tm, 1), 0)
        wcol = 2 << (lax.broadcasted_iota(I32, (1, POOL_W), 1) // POOL_GW)
        cnt = jnp.minimum(pos + 1, wcol).astype(F32)
        pooled_ref[...] = (sums / cnt - vp).astype(BF16).reshape(pooled_ref.shape)
        halo_ref[...] = vp[tm - 16:tm]


def _in_proj(x3, shift, scale, g1, w2, *, tm, prompt):
    g, r, d = x3.shape
    nt = r // tm
    per_row = shift.ndim == 2

    def tok(width, dtype):
        return (jax.ShapeDtypeStruct((g, r, width), dtype),
                pl.BlockSpec((1, tm, width), lambda b, j: (b, j, 0)))

    def tok_t(rows, dtype):
        return (jax.ShapeDtypeStruct((g, rows, r), dtype),
                pl.BlockSpec((1, rows, tm), lambda b, j: (b, 0, j)))

    outs = [tok(POOL_W, F32)] + [tok(KVW, F32)] * 6 + [tok(2 * d, F32)]
    scratch = []
    if prompt:
        outs += [tok(KVW, BF16), tok(KVW, BF16), tok_t(KVW, BF16), tok_t(KVW, BF16)]
        outs.append((jax.ShapeDtypeStruct((g, N_HEADS, LANES, r), BF16),
                     pl.BlockSpec((1, N_HEADS, LANES, tm), lambda b, j: (b, 0, 0, j))))
        outs += [tok_t(LANES, F32), tok(POOL_W, BF16)]
        scratch.append(pltpu.VMEM((16, POOL_W), F32))
    else:
        outs += [tok(QPAD, BF16), tok(LANES, F32)]
    if per_row:
        mod_spec = lambda col: pl.BlockSpec((tm, d), lambda b, j, col=col: (b * nt + j, col))
    else:
        mod_spec = lambda col: pl.BlockSpec((1, 1, d), lambda b, j, col=col: (b, 0, col))
    kern = functools.partial(_in_proj_kernel, tm=tm, d=d, prompt=prompt)
    return pl.pallas_call(
        kern,
        out_shape=[o[0] for o in outs],
        grid=(g, nt),
        in_specs=[pl.BlockSpec((1, tm, d), lambda b, j: (b, j, 0)),
                  mod_spec(0), mod_spec(1),
                  pl.BlockSpec((1, d), lambda b, j: (0, 0)),
                  pl.BlockSpec(w2.shape, lambda b, j: (0, 0))],
        out_specs=[o[1] for o in outs],
        scratch_shapes=scratch,
        compiler_params=_cparams(("arbitrary", "arbitrary")),
        name="in_proj_prompt" if prompt else "in_proj_sample",
    )(x3, shift, scale, g1, w2)


def _compress_kernel(kc_ref, vc_ref, wk_ref, wv_ref, okc_ref, ovc_ref, sh_ref, *, nc):
    last = lax.broadcasted_iota(I32, (nc, 1), 0) == nc - 1
    for src, w_ref, dst in ((kc_ref, wk_ref, okc_ref), (vc_ref, wv_ref, ovc_ref)):
        head = jnp.zeros((nc, KVW), F32)
        tail = jnp.zeros((nc, KVW), F32)
        for r in range(CMP_STRIDE):
            rows = src[pl.ds(r, nc, stride=CMP_STRIDE), :]
            head = head + rows * w_ref[r:r + 1, :]
            tail = tail + rows * w_ref[CMP_STRIDE + r:CMP_STRIDE + r + 1, :]
        sh_ref[0:nc, :] = tail
        sh_ref[nc:nc + 8, :] = jnp.zeros((8, KVW), F32)
        out = jnp.where(last, 0.0, head + sh_ref[1:nc + 1, :])
        dst[...] = (out if dst is okc_ref else out.T).astype(BF16)


def _compress(kc, vc, wk, wv):
    b, s, _ = kc.shape
    nc = s // CMP_STRIDE
    big = pl.BlockSpec((None, s, KVW), lambda i: (i, 0, 0))
    wsp = pl.BlockSpec((CMP_BLOCK, KVW), lambda i: (0, 0))
    return pl.pallas_call(
        functools.partial(_compress_kernel, nc=nc),
        out_shape=[jax.ShapeDtypeStruct((b, nc, KVW), BF16), jax.ShapeDtypeStruct((b, KVW, nc), BF16)],
        grid=(b,),
        in_specs=[big, big, wsp, wsp],
        out_specs=[pl.BlockSpec((None, nc, KVW), lambda i: (i, 0, 0)),
                   pl.BlockSpec((None, KVW, nc), lambda i: (i, 0, 0))],
        scratch_shapes=[pltpu.VMEM((nc + 8, KVW), F32)],
        compiler_params=_cparams(("arbitrary",)),
        name="compress",
    )(kc, vc, wk, wv)


def _topk_mask(vals, blk_f, n_top, axis=1):
    sel = jnp.zeros(vals.shape, F32)
    big = float(vals.shape[axis])
    for _ in range(n_top):
        mx = jnp.max(vals, axis=axis, keepdims=True)
        first = jnp.min(jnp.where(vals == mx, blk_f, big), axis=axis, keepdims=True)
        hit = blk_f == first
        sel = jnp.where(hit, 1.0, sel)
        vals = jnp.where(hit, -jnp.inf, vals)
    return sel


def _topk_mask_by_rank(vals, blk, n_valid, n_top):
    rank = jnp.zeros(vals.shape, F32)
    for j in range(n_valid):
        vj = vals[:, j:j + 1]
        beats = (vj > vals) | ((vj == vals) & (blk > j))
        rank = rank + jnp.where(beats, 1.0, 0.0)
    return jnp.where(rank < float(n_top), 1.0, 0.0)


def _pos_features(pos):
    hi = (pos // SEL_BLOCK).astype(F32)[:, None]
    lo = (pos % SEL_BLOCK).astype(F32)[:, None]
    return jnp.concatenate([hi, lo, jnp.zeros((pos.shape[0], LANES - 2), F32)], axis=1).astype(BF16)


def _importance_matrix(nc, nsel):
    j = jnp.arange(nc)[:, None]
    s = jnp.arange(nsel)[None, :]
    r = SEL_BLOCK // CMP_STRIDE
    a = (j >= r * s) & (j <= r * s + r - 1)
    b = (j + 1 >= r * s) & (j + 1 <= r * s + r - 1)
    return a.astype(F32) + b.astype(F32)


def _nsa_prompt_kernel(qt_ref, gst_ref, kc_ref, vct_ref, ks_ref, vst_ref, kw_ref, vwt_ref,
                       cfeat_ref, wfeat_ref, qfeat_ref, slope_ref,
                       y_ref, qk_scr, m_scr, l_scr, acc_scr, o_scr, sel_scr, imp_scr, flag_scr, ids_scr,
                       *, seq, tk, wl):
    i = pl.program_id(1)
    q0 = i * Q_BLOCK
    nq = Q_BLOCK
    gq = GROUP * nq
    nc = kc_ref.shape[0]
    nsel = seq // SEL_BLOCK
    n_top = min(TOP_BLOCKS, nsel)
    blk_per_tile = tk // SEL_BLOCK
    qpos = q0 + lax.broadcasted_iota(I32, (1, nq), 1)
    gst = gst_ref[...]

    crow = lax.broadcasted_iota(I32, (nc, nq), 0)
    cend = crow * CMP_STRIDE + (CMP_BLOCK - 1)
    mask_c = qpos >= cend
    kc = jnp.concatenate([kc_ref[...], cfeat_ref[...]], axis=1)
    vct = vct_ref[...]
    blk = lax.broadcasted_iota(I32, (nsel, nq), 0)
    blk_f = blk.astype(F32)
    cur = qpos // SEL_BLOCK
    forced = (blk == 0) | (blk == cur) | (blk == cur - 1)
    visible = blk * SEL_BLOCK <= qpos
    ws = pl.multiple_of(jnp.maximum(q0 - WINDOW, 0), Q_BLOCK)
    wpos = ws + lax.broadcasted_iota(I32, (wl, nq), 0)
    valid_w = lax.bitcast_convert_type(qpos - wpos, jnp.uint32) < WINDOW
    n_tiles = (q0 + nq + tk - 1) // tk
    half_rows = lax.broadcasted_iota(I32, (KVW, nq), 0) // HEAD_DIM
    tile_pos = lax.broadcasted_iota(I32, (SEL_BLOCK, nq), 0)

    def lanes4(x):
        return jnp.concatenate([x] * GROUP, axis=1)

    def gate_row(branch, k):
        r0 = branch * N_HEADS + k * GROUP
        return jnp.concatenate([gst[r0 + g:r0 + g + 1] for g in range(GROUP)], axis=1)

    mask_c4 = lanes4(mask_c)
    valid_w4 = lanes4(valid_w)
    kwt = jnp.concatenate([kw_ref[pl.ds(ws, wl), :], wfeat_ref[...]], axis=1)
    vwtt = vwt_ref[:, pl.ds(ws, wl)]

    for k in range(N_KV):
        for g in range(GROUP):
            qk_scr[k, 0:LANES, g * nq:(g + 1) * nq] = qt_ref[k * GROUP + g]
        qk_scr[k, LANES:2 * LANES, :] = qfeat_ref[k]
        qk = qk_scr[k]

        s = jnp.where(mask_c4, _dot(kc, qk), NEG)
        e = jnp.where(mask_c4, jnp.exp(s - jnp.max(s, axis=0, keepdims=True)), 0.0)
        l = jnp.sum(e, axis=0, keepdims=True)
        p = e * jnp.where(l > 0.0, 1.0 / l, 0.0)
        o_c = _dot(vct, p.astype(BF16))
        psum = p[:, 0:nq]
        for g in range(1, GROUP):
            psum = psum + p[:, g * nq:(g + 1) * nq]

        a = psum + jnp.where(crow == 0, 0.0, pltpu.roll(psum, 1, 0))
        a = a + pltpu.roll(a, nc - 1, 0)
        imp_scr[...] = a + pltpu.roll(a, nc - 2, 0)
        imp = imp_scr[pl.ds(0, nsel, stride=nc // nsel), :]
        vals = jnp.where(visible, jnp.where(forced, FORCE_SCORE, imp), NEG)
        sel = jnp.where(visible, _topk_mask(vals, blk_f, n_top, axis=0), 0.0)
        sel_scr[k] = jnp.where(sel > 0.5, 0.0, NEG)
        blk_any = jnp.max(sel, axis=1, keepdims=True)
        for t in range(seq // tk):
            hit = (jnp.max(blk_any[t * blk_per_tile:(t + 1) * blk_per_tile, :]) > 0.5).astype(I32)
            flag_scr[t] = hit if k == 0 else flag_scr[t] | hit

        s = jnp.where(valid_w4, _dot(kwt, qk), NEG)
        e = jnp.exp(s - jnp.max(s, axis=0, keepdims=True))
        p = e / jnp.sum(e, axis=0, keepdims=True)
        o_w = _dot(vwtt, p.astype(BF16))
        o_scr[k] = gate_row(0, k) * o_c + gate_row(2, k) * o_w

    m_scr[...] = jnp.full(m_scr.shape, NEG, F32)
    l_scr[...] = jnp.zeros(l_scr.shape, F32)
    acc_scr[...] = jnp.zeros(acc_scr.shape, F32)

    n_act = jnp.int32(0)
    for t in range(seq // tk):
        ids_scr[n_act] = t
        n_act = n_act + jnp.where((flag_scr[t] > 0) & (t < n_tiles), 1, 0)
    lane0 = lax.broadcasted_iota(I32, (tk, LANES), 1) == 0
    feat = wfeat_ref[0:tk, :]

    def sel_tiles(tiles):
        ta = tiles[0]
        starts = [pl.multiple_of(t * tk, tk) for t in tiles]
        keys = [jnp.concatenate([ks_ref[pl.ds(starts[0], tk), :], feat], axis=1)]
        for t, k0 in zip(tiles[1:], starts[1:]):
            shifted = jnp.where(lane0, feat.astype(F32) + ((t - ta) * blk_per_tile).astype(F32), feat.astype(F32))
            keys.append(jnp.concatenate([ks_ref[pl.ds(k0, tk), :], shifted.astype(BF16)], axis=1))
        kt = jnp.concatenate(keys, axis=0)
        vtt = jnp.concatenate([vst_ref[:, pl.ds(k0, tk)] for k0 in starts], axis=1)
        base = (starts[0] - q0).astype(F32)
        scores = _dot(kt, jnp.concatenate([qk_scr[k] for k in range(N_KV)], axis=1))
        probs, alphas = [], []
        for k in range(N_KV):
            neg = []
            for t, k0 in zip(tiles, starts):
                for j in range(blk_per_tile):
                    row = sel_scr[k, pl.ds(t * blk_per_tile + j, 1), :]
                    causal = qpos >= k0 + j * SEL_BLOCK + tile_pos
                    neg.append(jnp.where(causal, jnp.broadcast_to(row, (SEL_BLOCK, nq)), NEG))
            neg = lanes4(jnp.concatenate(neg, axis=0))
            off = slope_ref[k] * base
            s = scores[:, k * gq:(k + 1) * gq] + neg
            m_old = m_scr[k]
            m_new = jnp.maximum(m_old, jnp.max(s, axis=0, keepdims=True) + off)
            alpha = jnp.exp(m_old - m_new)
            p = jnp.exp(s - (m_new - off))
            l_scr[k] = alpha * l_scr[k] + jnp.sum(p, axis=0, keepdims=True)
            m_scr[k] = m_new
            probs.append(p.astype(BF16))
            alphas.append(alpha)
        pv = _dot(vtt, jnp.concatenate(probs, axis=1))
        for k in range(N_KV):
            acc_scr[k] = acc_scr[k] * alphas[k] + pv[:, k * gq:(k + 1) * gq]

    def sel_quad(i, carry):
        sel_tiles([ids_scr[4 * i + j] for j in range(4)])
        return carry

    n_quads = n_act // 4
    lax.fori_loop(0, n_quads, sel_quad, 0)
    pl.when(n_act % 4 >= 2)(lambda: sel_tiles([ids_scr[4 * n_quads], ids_scr[4 * n_quads + 1]]))
    pl.when(n_act % 2 == 1)(lambda: sel_tiles([ids_scr[n_act - 1]]))

    for k in range(N_KV):
        o = o_scr[k] + gate_row(1, k) * (acc_scr[k] / l_scr[k])
        for g in range(GROUP):
            h = k * GROUP + g
            oh = jnp.where(half_rows == k, o[:, g * nq:(g + 1) * nq], 0.0)
            y_ref[:, h * LANES:(h + 1) * LANES] = oh.T.astype(BF16)


def _nsa_prompt(qt, gst, kcmp, vcmpt, ksb, vst, kwb, vwt):
    b, _, _, s = qt.shape
    nq = Q_BLOCK
    gq = GROUP * nq
    nc = kcmp.shape[1]
    nsel = s // SEL_BLOCK
    tk = 128
    wl = WINDOW + Q_BLOCK
    assert s % tk == 0 and s >= wl
    assert s // SEL_BLOCK <= 2 * LANES, "position // 64 must stay exact in bf16"
    cfeat = _pos_features(jnp.arange(nc) * CMP_STRIDE + (CMP_BLOCK - 1))
    wfeat = _pos_features(jnp.arange(wl))
    slope_rows = jnp.repeat(jnp.asarray(SLOPES, F32).reshape(N_KV, 1, GROUP), nq, axis=2)
    qfeat = jnp.concatenate([slope_rows * SEL_BLOCK, slope_rows, jnp.zeros((N_KV, LANES - 2, gq), F32)],
                            axis=1).astype(BF16)
    rows = lambda r: pl.BlockSpec((None, r, KVW), lambda bi, i: (bi, 0, 0))
    cols = lambda c: pl.BlockSpec((None, KVW, c), lambda bi, i: (bi, 0, 0))
    const = lambda a: pl.BlockSpec(a.shape, lambda bi, i: (0,) * a.ndim)
    return pl.pallas_call(
        functools.partial(_nsa_prompt_kernel, seq=s, tk=tk, wl=wl),
        out_shape=jax.ShapeDtypeStruct((b, s, QPAD), BF16),
        grid=(b, s // nq),
        in_specs=[pl.BlockSpec((None, N_HEADS, LANES, nq), lambda bi, i: (bi, 0, 0, i)),
                  pl.BlockSpec((None, LANES, nq), lambda bi, i: (bi, 0, i)),
                  rows(nc), cols(nc), rows(s), cols(s), rows(s), cols(s),
                  const(cfeat), const(wfeat), const(qfeat), const(slope_rows)],
        out_specs=pl.BlockSpec((None, nq, QPAD), lambda bi, i: (bi, i, 0)),
        scratch_shapes=[pltpu.VMEM((N_KV, 2 * LANES, gq), BF16),
                        pltpu.VMEM((N_KV, 1, gq), F32),
                        pltpu.VMEM((N_KV, 1, gq), F32),
                        pltpu.VMEM((N_KV, KVW, gq), F32),
                        pltpu.VMEM((N_KV, KVW, gq), F32),
                        pltpu.VMEM((N_KV, nsel, nq), F32),
                        pltpu.VMEM((nc, nq), F32),
                        pltpu.SMEM((s // tk,), I32),
                        pltpu.SMEM((s // tk,), I32)],
        compiler_params=_cparams(("arbitrary", "arbitrary")),
        name="nsa_prompt",
    )(qt, gst, kcmp, vcmpt, ksb, vst, kwb, vwt, cfeat, wfeat, qfeat, slope_rows)


def _nsa_sample_kernel(pt_ref, q_ref, gate_ref, kcn_ref, vcn_ref, ksn_ref, vsn_ref, kwn_ref, vwn_ref,
                       vpn_ref, skw_ref, svw_ref, spool_ref, wk_ref, wv_ref, imat_ref, emat_ref,
                       ckc_ref, cvc_ref, cks_ref, cvs_ref,
                       o_ref, pooled_ref, kwo_ref, vwo_ref, buf, buft, win_scr, tail_scr, vext_scr, sem,
                       *, sb, ns, past, n_pages, page, ncv, ncp, nks, wls, nselp, n_sel):
    step = pl.program_id(0)
    nrow = ns * N_HEADS
    par = step % 2

    def copies(n, side, r):
        out = []
        for p in range(n_pages):
            pg = pt_ref[n * n_pages + p]
            for c, cref in enumerate((ckc_ref, cvc_ref, cks_ref, cvs_ref)):
                out.append(pltpu.make_async_copy(cref.at[pg], buft.at[side, r, c, :, pl.ds(p * page, page)],
                                                 sem.at[side, r]))
        return out

    @pl.when(step == 0)
    def _():
        buf[:, :, past:, :] = jnp.zeros((sb, 2, buf.shape[2] - past, KVW), F32)
        tail_scr[...] = jnp.zeros_like(tail_scr)
        vext_scr[...] = jnp.zeros_like(vext_scr)
        for r in range(sb):
            for cp in copies(r, 0, r):
                cp.start()

    @pl.when(step + 1 < pl.num_programs(0))
    def _():
        for r in range(sb):
            for cp in copies((step + 1) * sb + r, 1 - par, r):
                cp.start()

    for r in range(sb):
        for cp in copies(step * sb + r, par, r):
            cp.wait()

    def new_rows_t(ref, r4, r):
        tail_scr[r, 0:ns, :] = ref[pl.ds(r4, ns), :]
        return tail_scr[r].T

    row = lax.broadcasted_iota(I32, (nrow, 1), 0)
    hrow = row % N_HEADS
    qpos = past + row // N_HEADS
    slope = jnp.exp2(-8.0 * (hrow.astype(F32) + 1.0) / N_HEADS)
    kvrow = hrow // GROUP
    lane = lax.broadcasted_iota(I32, (1, LANES), 1)
    half = (lane // HEAD_DIM) == kvrow
    grow = (row // N_HEADS) * N_KV + kvrow
    row8 = lax.broadcasted_iota(I32, (ns * N_KV, 1), 0)
    qpos8 = past + lax.broadcasted_iota(I32, (ns * N_KV, 1), 0) // N_KV
    blk = lax.broadcasted_iota(I32, (1, nselp), 1)
    blk_f = blk.astype(F32)
    cur = qpos8 // SEL_BLOCK
    forced = (blk == 0) | (blk == cur) | (blk == cur - 1)
    visible = (blk * SEL_BLOCK <= qpos8)
    inrange = blk < n_sel
    cend = lax.broadcasted_iota(I32, (1, ncp), 1) * CMP_STRIDE + (CMP_BLOCK - 1)
    mask_c = qpos >= cend
    bias_c = slope * (cend - qpos).astype(F32)
    kpos = lax.broadcasted_iota(I32, (1, nks), 1)
    causal_s = qpos >= kpos
    bias_s = slope * (kpos - qpos).astype(F32)
    wbuf = wls[0]
    wpos = past - wbuf + lax.broadcasted_iota(I32, (1, wls[1]), 1)
    dw = qpos - wpos
    valid_w = lax.bitcast_convert_type(dw, jnp.uint32) < WINDOW
    bias_w = slope * (wpos - qpos).astype(F32)
    prow = lax.broadcasted_iota(I32, (vext_scr.shape[1], 1), 0)
    wcol = 2 << (lax.broadcasted_iota(I32, (1, POOL_W), 1) // POOL_GW)
    n_top = min(TOP_BLOCKS, n_sel)

    def softmax_rows(s, mask):
        s = jnp.where(mask, s, NEG)
        mx = jnp.max(s, axis=1, keepdims=True)
        e = jnp.where(mask, jnp.exp(s - mx), 0.0)
        l = jnp.sum(e, axis=1, keepdims=True)
        return e * jnp.where(l > 0.0, 1.0 / l, 0.0)

    def seq_body(r):
        r4 = r * ns
        for c, new_ref in enumerate((kcn_ref, vcn_ref)):
            for p in range(n_pages):
                buf[r, c, p * page:(p + 1) * page, :] = buft[par, r, c, :, p * page:(p + 1) * page].T
            buf[r, c, past:past + ns, :] = new_ref[pl.ds(r4, ns), :]
        for c, new_ref in ((2, ksn_ref), (3, vsn_ref)):
            buft[par, r, c, :, past:past + LANES] = new_rows_t(new_ref, r4, r)

        qall = q_ref[pl.ds(r * nrow, nrow), :]
        gates = gate_ref[pl.ds(r * nrow, nrow), :]

        cmp = []
        for c, w_ref in ((0, wk_ref), (1, wv_ref)):
            span = CMP_STRIDE * ncv
            lo = buf[r, c, 0:span, :].reshape(ncv, CMP_STRIDE, KVW) * w_ref[0:CMP_STRIDE, :][None]
            hi = (buf[r, c, CMP_STRIDE:CMP_STRIDE + span, :].reshape(ncv, CMP_STRIDE, KVW)
                  * w_ref[CMP_STRIDE:CMP_BLOCK, :][None])
            acc = jnp.sum(lo + hi, axis=1)
            cmp.append(jnp.concatenate([acc, jnp.zeros((ncp - ncv, KVW), F32)], axis=0).astype(BF16))
        p_c = softmax_rows(_dot_nt(qall, cmp[0]) + bias_c, mask_c)
        o_c = _dot(p_c.astype(BF16), cmp[1])

        psum = jnp.zeros((ns * N_KV, ncp), F32)
        for i in range(ns * N_KV):
            r0 = (i // N_KV) * N_HEADS + (i % N_KV) * GROUP
            psum = jnp.where(row8 == i, jnp.sum(p_c[r0:r0 + GROUP], axis=0, keepdims=True), psum)
        imp = _dot_exact(psum, imat_ref[...])
        vals = jnp.where(inrange, jnp.where(visible, jnp.where(forced, FORCE_SCORE, imp), NEG), -jnp.inf)
        sel8 = _topk_mask_by_rank(vals, blk, n_sel, n_top)
        sel_rows = jnp.zeros((nrow, nselp), F32)
        for i in range(ns * N_KV):
            sel_rows = jnp.where(grow == i, sel8[i:i + 1], sel_rows)
        chosen = _dot(sel_rows.astype(BF16), emat_ref[...])

        kst = buft[par, r, 2].astype(BF16)
        vst = buft[par, r, 3].astype(BF16)
        p_s = softmax_rows(_dot(qall, kst) + bias_s, causal_s & (chosen > 0.5))
        o_s = _dot_nt(p_s.astype(BF16), vst)

        outs_w = []
        for state_ref, new_ref, next_ref in ((skw_ref, kwn_ref, kwo_ref), (svw_ref, vwn_ref, vwo_ref)):
            win_scr[r, :, 0:wbuf] = state_ref[r]
            win_scr[r, :, wbuf:wbuf + LANES] = new_rows_t(new_ref, r4, r)
            outs_w.append(win_scr[r].astype(BF16))
            next_ref[r] = win_scr[r, :, ns:ns + wbuf]
        p_w = softmax_rows(_dot(qall, outs_w[0]) + bias_w, valid_w)
        o_w = _dot_nt(p_w.astype(BF16), outs_w[1])

        o = gates[:, 0:1] * o_c + gates[:, 1:2] * o_s + gates[:, 2:3] * o_w
        o_ref[pl.ds(r * nrow, nrow), :] = jnp.where(half, o, 0.0).astype(BF16)

        vext_scr[r, 0:POOL_BUF, :] = spool_ref[r]
        vext_scr[r, POOL_BUF:POOL_BUF + ns, :] = vpn_ref[pl.ds(r4, ns), :]
        ext = vext_scr[r]
        for t in range(ns):
            hi = POOL_BUF + t
            inwin = (prow <= hi) & (prow > hi - wcol)
            ssum = jnp.sum(jnp.where(inwin, ext, 0.0), axis=0, keepdims=True)
            cnt = jnp.minimum(past + t + 1, wcol).astype(F32)
            pooled_ref[pl.ds(r4 + t, 1), :] = ssum / cnt - ext[hi:hi + 1, :]

    for r in range(sb):
        seq_body(r)


def _nsa_sample(page_table, q_rows, gate_rows, new6, vp_new, state_kwt, state_vwt, state_pool, wk, wv, caches):
    n_seq, n_pages = page_table.shape
    page = caches[0].shape[2]
    past = n_pages * page
    ns = vp_new.shape[0] // n_seq
    wbuf = state_kwt.shape[2]
    sb = 2
    nrow = ns * N_HEADS
    assert ns <= SEL_BLOCK and page == LANES
    t_pad = -(-(past + ns) // SEL_BLOCK) * SEL_BLOCK
    n_cmp = t_pad // CMP_STRIDE - 1
    ncv = -(-n_cmp // 8) * 8
    ncp = -(-ncv // LANES) * LANES
    nks = past + LANES
    n_sel = t_pad // SEL_BLOCK
    nselp = LANES
    assert n_sel <= nselp
    wlp = wbuf + LANES
    buf_rows = -(-(CMP_STRIDE * ncv + CMP_STRIDE) // 8) * 8
    imat = _importance_matrix(ncp, nselp)
    emat = (jnp.arange(nselp)[:, None] == (jnp.arange(nks)[None, :] // SEL_BLOCK)).astype(BF16)

    seqblk = lambda rows, w: pl.BlockSpec((sb * rows, w), lambda i, pt: (i, 0))
    const = lambda a: pl.BlockSpec(a.shape, lambda i, pt: (0,) * a.ndim)
    kern = functools.partial(
        _nsa_sample_kernel, sb=sb, ns=ns, past=past, n_pages=n_pages, page=page,
        ncv=ncv, ncp=ncp, nks=nks, wls=(wbuf, wlp), nselp=nselp, n_sel=n_sel)
    grid_spec = pltpu.PrefetchScalarGridSpec(
        num_scalar_prefetch=1,
        grid=(n_seq // sb,),
        in_specs=[seqblk(nrow, LANES), seqblk(nrow, LANES)] + [seqblk(ns, KVW)] * 6 + [seqblk(ns, POOL_W)]
        + [pl.BlockSpec((sb, KVW, wbuf), lambda i, pt: (i, 0, 0))] * 2
        + [pl.BlockSpec((sb, POOL_BUF, POOL_W), lambda i, pt: (i, 0, 0))]
        + [const(wk), const(wv), const(imat), const(emat)]
        + [pl.BlockSpec(memory_space=pl.ANY)] * 4,
        out_specs=[seqblk(nrow, LANES), seqblk(ns, POOL_W)]
        + [pl.BlockSpec((sb, KVW, wbuf), lambda i, pt: (i, 0, 0))] * 2,
        scratch_shapes=[pltpu.VMEM((sb, 2, buf_rows, KVW), F32),
                        pltpu.VMEM((2, sb, 4, KVW, nks), F32),
                        pltpu.VMEM((sb, KVW, wlp), F32),
                        pltpu.VMEM((sb, LANES, KVW), F32),
                        pltpu.VMEM((sb, 24, POOL_W), F32),
                        pltpu.SemaphoreType.DMA((2, sb))],
    )
    return pl.pallas_call(
        kern,
        out_shape=[jax.ShapeDtypeStruct((n_seq * nrow, LANES), BF16),
                   jax.ShapeDtypeStruct((n_seq * ns, POOL_W), F32)]
        + [jax.ShapeDtypeStruct((n_seq, KVW, wbuf), F32)] * 2,
        grid_spec=grid_spec,
        compiler_params=_cparams(("arbitrary",)),
        name="nsa_sample",
    )(page_table.reshape(-1), q_rows, gate_rows, *new6, vp_new, state_kwt, state_vwt, state_pool, wk, wv,
      imat, emat, *caches)


def _route(logits_t, bias_col, tm):
    sc = jax.nn.sigmoid(logits_t)
    biased = sc + bias_col
    epg = EXPERTS_PER_GROUP
    row8 = lax.broadcasted_iota(I32, (epg, tm), 0).astype(F32)
    ninf = -jnp.inf
    grp = jnp.zeros((N_EGROUPS, tm), F32)
    for g in range(N_EGROUPS):
        bg = biased[g * epg:(g + 1) * epg]
        m1 = jnp.max(bg, axis=0, keepdims=True)
        first = jnp.min(jnp.where(bg == m1, row8, float(epg)), axis=0, keepdims=True)
        m2 = jnp.max(jnp.where(row8 == first, ninf, bg), axis=0, keepdims=True)
        grp = jnp.where(row8 == float(g), m1 + m2, grp)
    keep = jnp.zeros((N_EGROUPS, tm), F32)
    vals = grp
    for _ in range(TOPK_GROUPS):
        mx = jnp.max(vals, axis=0, keepdims=True)
        first = jnp.min(jnp.where(vals == mx, row8, float(N_EGROUPS)), axis=0, keepdims=True)
        hit = row8 == first
        keep = jnp.where(hit, 1.0, keep)
        vals = jnp.where(hit, ninf, vals)
    masked = jnp.concatenate(
        [jnp.where(keep[g:g + 1] > 0.5, biased[g * epg:(g + 1) * epg], NEG) for g in range(N_EGROUPS)], axis=0)
    rowe = lax.broadcasted_iota(I32, (N_EXPERTS, tm), 0).astype(F32)
    chosen = jnp.zeros((N_EXPERTS, tm), F32)
    vals = masked
    picks = []
    for _ in range(TOP_K):
        mx = jnp.max(vals, axis=0, keepdims=True)
        first = jnp.min(jnp.where(vals == mx, rowe, float(N_EXPERTS)), axis=0, keepdims=True)
        hit = rowe == first
        chosen = jnp.where(hit, sc, chosen)
        vals = jnp.where(hit, ninf, vals)
        picks.append((hit, first))
    return ROUTED_SCALE * chosen / jnp.sum(chosen, axis=0, keepdims=True), picks


def _pack_bf16_pairs(x):
    c = x.shape[1] // 2
    bits = lambda v: lax.bitcast_convert_type(v.astype(BF16).astype(F32), jnp.uint32)
    return (bits(x[:, :c]) >> 16) | (bits(x[:, c:]) & jnp.uint32(0xFFFF0000))


def _unpack_bf16_pairs(w):
    lo = lax.bitcast_convert_type(w << 16, F32)
    hi = lax.bitcast_convert_type(w & jnp.uint32(0xFFFF0000), F32)
    return jnp.concatenate([lo, hi], axis=1)


def _finish_kernel(x_ref, pooled_ref, y_ref, gm_ref, g1_ref, shift_ref, scale_ref,
                   wlin_ref, pscale_ref, wpo_ref, wno_ref, wo_ref, n2_ref, wr_ref, br_ref,
                   *rest, tm, d, sparse):
    if sparse:
        tri_ref, x1_ref, u2_ref, up_ref, eid_ref, gk_ref, rank_ref, cnt_ref, carry_scr = rest
    else:
        x1_ref, u2_ref, gates_ref = rest
    x = x_ref[...].reshape(tm, d)
    pooled = pooled_ref[...].reshape(tm, POOL_W).astype(BF16)
    y_pool = _dot(pooled, wlin_ref[...]) * pscale_ref[...]
    a = _dot(y_pool.astype(BF16), wpo_ref[...])
    b = _dot(y_ref[...].reshape(tm, QPAD), wno_ref[...])
    gm = gm_ref[...].reshape(tm, 2 * d)
    merged = gm[:, :d] * a + gm[:, d:] * b
    x1 = x + _rows2d(g1_ref) * _dot(merged.astype(BF16), wo_ref[...])
    x1_ref[...] = x1.reshape(x1_ref.shape)
    u2 = _rmsnorm(x1, n2_ref[...]) * (1.0 + _rows2d(scale_ref)) + _rows2d(shift_ref)
    u2b = u2.astype(BF16)
    u2_ref[...] = u2b.reshape(u2_ref.shape)
    logits_t = _dot_nt(wr_ref[...], u2b)
    gates_t, picks = _route(logits_t[:N_EXPERTS], br_ref[...], tm)
    if not sparse:
        gates_t = jnp.concatenate([gates_t, jnp.zeros((LANES - N_EXPERTS, tm), F32)], axis=0)
        gates_ref[...] = gates_t.T.reshape(gates_ref.shape)
        return

    @pl.when((pl.program_id(0) == 0) & (pl.program_id(1) == 0))
    def _():
        carry_scr[...] = jnp.zeros_like(carry_scr)

    packed = _pack_bf16_pairs(u2)
    for s in range(up_ref.shape[0]):
        up_ref[s] = packed[:, s * SC_ROW_WORDS:(s + 1) * SC_ROW_WORDS]
    hit_all = picks[0][0]
    for hit, _ in picks[1:]:
        hit_all = hit_all | hit
    hits = jnp.where(hit_all, 1.0, 0.0).astype(BF16)
    before = _dot(hits, tri_ref[...]) + jnp.concatenate([carry_scr[...]] * (tm // LANES), axis=1)
    eids, gks, ranks = [], [], []
    for hit, first in picks:
        eids.append(first)
        gks.append(jnp.sum(jnp.where(hit, gates_t, 0.0), axis=0, keepdims=True))
        ranks.append(jnp.sum(jnp.where(hit, before, 0.0), axis=0, keepdims=True))
    pick_row = lax.broadcasted_iota(I32, (TOP_K, tm), 0)

    def stack(rows):
        out = jnp.zeros((TOP_K, tm), F32)
        for r, v in enumerate(rows):
            out = jnp.where(pick_row == r, v, out)
        return out

    eid_ref[...] = stack(eids).astype(I32).reshape(eid_ref.shape)
    gk_ref[...] = jnp.concatenate([stack(gks), jnp.zeros((LANES - TOP_K, tm), F32)], axis=0).T.reshape(gk_ref.shape)
    rank_ref[...] = stack(ranks).astype(I32).reshape(rank_ref.shape)
    carry_scr[...] += _dot(hits, jnp.ones((tm, LANES), BF16))
    cnt_ref[...] = carry_scr[...]


def _finish(x3, pooled, ynsa, gm, mods, wts, *, tm, sparse):
    g, r, d = x3.shape
    nt = r // tm
    g1, shift2, scale2 = mods
    per_row = g1.ndim == 2
    tok = lambda w: pl.BlockSpec((1, tm, w), lambda b, j: (b, j, 0))
    tok_t = lambda rows: pl.BlockSpec((1, rows, tm), lambda b, j: (b, 0, j))
    if per_row:
        mod_spec = lambda col: pl.BlockSpec((tm, d), lambda b, j, col=col: (b * nt + j, col))
    else:
        mod_spec = lambda col: pl.BlockSpec((1, 1, d), lambda b, j, col=col: (b, 0, col))
    const = lambda a: pl.BlockSpec(a.shape, lambda b, j: (0,) * a.ndim)
    out_shape = [jax.ShapeDtypeStruct((g, r, d), F32), jax.ShapeDtypeStruct((g, r, d), BF16)]
    out_specs = [tok(d), tok(d)]
    scratch = []
    if sparse:
        tri = (jnp.arange(tm)[:, None] < jnp.arange(tm)[None, :]).astype(BF16)
        wts = tuple(wts) + (tri,)
        split = d // 2 // SC_ROW_WORDS
        out_shape += [jax.ShapeDtypeStruct((split, g * r, SC_ROW_WORDS), jnp.uint32),
                      jax.ShapeDtypeStruct((g, TOP_K, r), I32), jax.ShapeDtypeStruct((g, r, LANES), F32),
                      jax.ShapeDtypeStruct((g, TOP_K, r), I32), jax.ShapeDtypeStruct((N_EXPERTS, LANES), F32)]
        out_specs += [pl.BlockSpec((split, tm, SC_ROW_WORDS), lambda b, j: (0, b * nt + j, 0)),
                      tok_t(TOP_K), tok(LANES), tok_t(TOP_K),
                      pl.BlockSpec((N_EXPERTS, LANES), lambda b, j: (0, 0))]
        scratch.append(pltpu.VMEM((N_EXPERTS, LANES), F32))
    else:
        out_shape.append(jax.ShapeDtypeStruct((g, r, LANES), F32))
        out_specs.append(tok(LANES))
    return pl.pallas_call(
        functools.partial(_finish_kernel, tm=tm, d=d, sparse=sparse),
        out_shape=out_shape,
        grid=(g, nt),
        in_specs=[tok(d), tok(POOL_W), tok(QPAD), tok(2 * d), mod_spec(2), mod_spec(3), mod_spec(4)]
        + [const(w) for w in wts],
        out_specs=out_specs,
        scratch_shapes=scratch,
        compiler_params=_cparams(("arbitrary", "arbitrary")),
        name="finish_route" if sparse else "finish",
    )(x3, pooled, ynsa, gm, g1, shift2, scale2, *wts)


def _moe_kernel(u_ref, gates_ref, x1_ref, g2_ref, nf_ref, wg_ref, wu_ref, wd_ref, sg_ref, su_ref, sd_ref,
                y_ref, acc_ref, *, tm, d, eps):
    e = pl.program_id(2)
    u = u_ref[...].reshape(tm, d)

    @pl.when(e == 0)
    def _():
        hs = _silu(_dot(u, sg_ref[...])) * _dot(u, su_ref[...])
        acc_ref[...] = _dot(hs.astype(BF16), sd_ref[...])

    gates = gates_ref[...].reshape(tm, LANES)
    lane = lax.broadcasted_iota(I32, (1, LANES), 1)
    hidden = []
    for j in range(eps):
        h = _silu(_dot(u, wg_ref[j].astype(BF16))) * _dot(u, wu_ref[j].astype(BF16))
        gate = jnp.sum(jnp.where(lane == e * eps + j, gates, 0.0), axis=1, keepdims=True)
        hidden.append((h * gate).astype(BF16))
    f = wd_ref.shape[1]
    acc_ref[...] += _dot(jnp.concatenate(hidden, axis=1), wd_ref[...].reshape(eps * f, d).astype(BF16))

    @pl.when(e == pl.num_programs(2) - 1)
    def _():
        x2 = x1_ref[...].reshape(tm, d) + _rows2d(g2_ref) * acc_ref[...]
        y_ref[...] = _rmsnorm(x2, nf_ref[...]).reshape(y_ref.shape)


def _moe(u2, gates, x1, g2, normf, w_gate, w_up, w_down, sg, su, sd, *, tm):
    g, r, d = x1.shape
    nt = r // tm
    ne, _, f = w_gate.shape
    per_row = g2.ndim == 2
    tok = lambda w: pl.BlockSpec((1, tm, w), lambda b, j, e: (b, j, 0))
    if per_row:
        g2_spec = pl.BlockSpec((tm, d), lambda b, j, e: (b * nt + j, 5))
    else:
        g2_spec = pl.BlockSpec((1, 1, d), lambda b, j, e: (b, 0, 5))
    once = pl.Buffered(buffer_count=1)
    const = lambda a: pl.BlockSpec(a.shape, lambda b, j, e: (0,) * a.ndim, pipeline_mode=once)
    eps = 4
    return pl.pallas_call(
        functools.partial(_moe_kernel, tm=tm, d=d, eps=eps),
        out_shape=jax.ShapeDtypeStruct((g, r, d), F32),
        grid=(g, nt, ne // eps),
        in_specs=[tok(d), tok(LANES),
                  pl.BlockSpec((1, tm, d), lambda b, j, e: (b, j, 0), pipeline_mode=once),
                  g2_spec, const(normf),
                  pl.BlockSpec((eps, d, f), lambda b, j, e: (e, 0, 0)),
                  pl.BlockSpec((eps, d, f), lambda b, j, e: (e, 0, 0)),
                  pl.BlockSpec((eps, f, d), lambda b, j, e: (e, 0, 0)),
                  const(sg), const(su), const(sd)],
        out_specs=tok(d),
        scratch_shapes=[pltpu.VMEM((tm, d), F32)],
        compiler_params=_cparams(("arbitrary", "arbitrary", "arbitrary")),
        name="moe",
    )(u2, gates, x1, g2, normf, w_gate, w_up, w_down, sg, su, sd)


SC_WINDOW = 128
SC_ROW_WORDS = 256
MOE_ROWS = 512


def _sc_mesh():
    return plsc.VectorSubcoreMesh(core_axis_name="c", subcore_axis_name="s")


def _sc_scatter_rows(src, dst_idx, n_dst):
    n, w = src.shape
    nk = dst_idx.shape[0]

    @pl.kernel(out_type=jax.ShapeDtypeStruct((n_dst, w), src.dtype), mesh=_sc_mesh(), scratch_types=[])
    def scatter(src_hbm, idx_hbm, dst_hbm):
        def body(rows_vmem, idx_vmem):
            pltpu.sync_copy(rows_vmem, dst_hbm.at[idx_vmem.at[0]])

        pltpu.emit_pipeline(
            body,
            grid=(nk, n // SC_WINDOW),
            in_specs=[pl.BlockSpec((SC_WINDOW, w), index_map=lambda k, i: (i, 0)),
                      pl.BlockSpec((1, SC_WINDOW), index_map=lambda k, i: (k, i))],
            out_specs=[],
            core_axis_name=("c", "s"),
            dimension_semantics=(pltpu.PARALLEL, pltpu.PARALLEL),
        )(src_hbm, idx_hbm)

    return scatter(src, dst_idx)


def _sc_gather_rows(src, idx):
    n, w = idx.shape[0], src.shape[1]

    @pl.kernel(out_type=jax.ShapeDtypeStruct((n, w), src.dtype), mesh=_sc_mesh(), scratch_types=[])
    def gather(src_hbm, idx_hbm, out_hbm):
        def body(idx_vmem, out_vmem):
            pltpu.sync_copy(src_hbm.at[idx_vmem.at[0]], out_vmem)

        pltpu.emit_pipeline(
            body,
            grid=(n // SC_WINDOW,),
            in_specs=[pl.BlockSpec((1, SC_WINDOW), index_map=lambda i: (0, i))],
            out_specs=[pl.BlockSpec((SC_WINDOW, w), index_map=lambda i: (i, 0))],
            core_axis_name=("c", "s"),
            dimension_semantics=(pltpu.PARALLEL,),
        )(idx_hbm, out_hbm)

    return gather(src, idx.reshape(1, n))


MOE_BLOCKS_PER_STEP = 2


def _expert_rows_kernel(te_ref, nt_ref, x_ref, *refs):
    y_ref = refs[-1]
    split = x_ref.shape[0]
    for j in range(MOE_BLOCKS_PER_STEP):
        wg_ref, wu_ref, wd_ref = refs[3 * j:3 * j + 3]
        rows = slice(j * MOE_ROWS, (j + 1) * MOE_ROWS)

        @pl.when(pl.program_id(0) * MOE_BLOCKS_PER_STEP + j < nt_ref[0])
        def _():
            x = _unpack_bf16_pairs(jnp.concatenate([x_ref[s, rows] for s in range(split)], axis=1)).astype(BF16)
            h = _silu(_dot(x, wg_ref[...].astype(BF16))) * _dot(x, wu_ref[...].astype(BF16))
            y = _pack_bf16_pairs(_dot(h.astype(BF16), wd_ref[...].astype(BF16)))
            for s in range(split):
                y_ref[s, rows] = y[:, s * SC_ROW_WORDS:(s + 1) * SC_ROW_WORDS]


def _expert_rows(tile_expert, n_tiles, x_sorted, w_gate, w_up, w_down):
    split, p, words = x_sorted.shape
    ne, d, f = w_gate.shape
    bps = MOE_BLOCKS_PER_STEP
    wspec = lambda a, b, j: pl.BlockSpec((None, a, b), lambda i, te, nt: (te[i * bps + j], 0, 0))
    rows = pl.BlockSpec((split, bps * MOE_ROWS, words), lambda i, te, nt: (0, i, 0))
    weights, wspecs = [], []
    for j in range(bps):
        weights += [w_gate, w_up, w_down]
        wspecs += [wspec(d, f, j), wspec(d, f, j), wspec(f, d, j)]
    grid_spec = pltpu.PrefetchScalarGridSpec(
        num_scalar_prefetch=2,
        grid=(p // (bps * MOE_ROWS),),
        in_specs=[rows] + wspecs,
        out_specs=rows,
    )
    return pl.pallas_call(
        _expert_rows_kernel,
        out_shape=jax.ShapeDtypeStruct((split, p, words), jnp.uint32),
        grid_spec=grid_spec,
        compiler_params=_cparams(("arbitrary",)),
        name="expert_rows",
    )(tile_expert, n_tiles, x_sorted, *weights)


def _combine_kernel(yg_ref, gk_ref, u_ref, x1_ref, g2_ref, nf_ref, sg_ref, su_ref, sd_ref, y_ref, *, tm, d):
    u = u_ref[...].reshape(tm, d)
    hs = _silu(_dot(u, sg_ref[...])) * _dot(u, su_ref[...])
    acc = _dot(hs.astype(BF16), sd_ref[...])
    gk = gk_ref[...].reshape(tm, LANES)
    lane = lax.broadcasted_iota(I32, (1, LANES), 1)
    split = yg_ref.shape[0]
    for k in range(TOP_K):
        gate = jnp.sum(jnp.where(lane == k, gk, 0.0), axis=1, keepdims=True)
        words = jnp.concatenate([yg_ref[s, k] for s in range(split)], axis=1)
        acc = acc + gate * _unpack_bf16_pairs(words)
    x2 = x1_ref[...].reshape(tm, d) + _rows2d(g2_ref) * acc
    y_ref[...] = _rmsnorm(x2, nf_ref[...]).reshape(y_ref.shape)


def _combine(yg, gk, u2, x1, g2, normf, sg, su, sd, *, tm):
    g, r, d = x1.shape
    nt = r // tm
    split, _, _, words = yg.shape
    tok = lambda w: pl.BlockSpec((1, tm, w), lambda b, j: (b, j, 0))
    const = lambda a: pl.BlockSpec(a.shape, lambda b, j: (0,) * a.ndim)
    return pl.pallas_call(
        functools.partial(_combine_kernel, tm=tm, d=d),
        out_shape=jax.ShapeDtypeStruct((g, r, d), F32),
        grid=(g, nt),
        in_specs=[pl.BlockSpec((split, TOP_K, tm, words), lambda b, j: (0, 0, b * nt + j, 0)),
                  tok(LANES), tok(d), tok(d),
                  pl.BlockSpec((1, 1, d), lambda b, j: (b, 0, 5)), const(normf), const(sg), const(su), const(sd)],
        out_specs=tok(d),
        compiler_params=_cparams(("arbitrary", "arbitrary")),
        name="moe_combine",
    )(yg, gk, u2, x1, g2, normf, sg, su, sd)


def _moe_sorted_experts(u2p, eid_t, rank_t, counts, w_gate, w_up, w_down):
    split, n, _ = u2p.shape
    ne = w_gate.shape[0]
    cnt = counts[:, 0].astype(I32)
    padded = -(-cnt // MOE_ROWS) * MOE_ROWS
    seg_end = jnp.cumsum(padded)
    seg_start = seg_end - padded
    p_rows = n * TOP_K + ne * MOE_ROWS
    eid = eid_t.transpose(1, 0, 2).reshape(TOP_K, n)
    start = jnp.sum(jnp.where(eid[:, :, None] == jnp.arange(ne, dtype=I32), seg_start, 0), axis=-1)
    pos = start + rank_t.transpose(1, 0, 2).reshape(TOP_K, n)
    first_row = jnp.arange(p_rows // MOE_ROWS, dtype=I32) * MOE_ROWS
    tile_expert = jnp.minimum(jnp.sum(seg_end[None, :] <= first_row[:, None], axis=1), ne - 1).astype(I32)
    n_tiles = (seg_end[-1:] // MOE_ROWS).astype(I32)
    scat_idx = jnp.concatenate([pos + s * p_rows for s in range(split)], axis=1)
    gath_idx = jnp.concatenate([pos.reshape(-1) + s * p_rows for s in range(split)])
    x_sorted = _sc_scatter_rows(u2p.reshape(split * n, SC_ROW_WORDS), scat_idx, split * p_rows)
    y_sorted = _expert_rows(tile_expert, n_tiles, x_sorted.reshape(split, p_rows, SC_ROW_WORDS),
                            w_gate, w_up, w_down)
    return y_sorted.reshape(split * p_rows, SC_ROW_WORDS), gath_idx


def _moe_sorted_combine(y_sorted, gath_idx, gk, u2, x1, g2, normf, sg, su, sd):
    g, r, _ = x1.shape
    split = gath_idx.shape[0] // (TOP_K * g * r)
    yg = _sc_gather_rows(y_sorted, gath_idx)
    return _combine(yg.reshape(split, TOP_K, g * r, SC_ROW_WORDS), gk, u2, x1, g2, normf, sg, su, sd, tm=512)


def _kv_slot_mask():
    return (jnp.arange(N_HEADS)[:, None] // GROUP == jnp.arange(N_KV)[None, :]).astype(F32)


def _prep_w_in(w_in, d):
    q0 = POOL_W
    kv0 = q0 + N_HEADS * HEAD_DIM
    gn0 = kv0 + 6 * KVW
    gm0 = gn0 + 3 * N_HEADS
    wq = w_in[:, q0:kv0].reshape(d, N_HEADS, 1, HEAD_DIM) * (HEAD_DIM ** -0.5)
    wq = (wq * _kv_slot_mask()[None, :, :, None]).reshape(d, QPAD)
    wgn = jnp.pad(w_in[:, gn0:gm0], ((0, 0), (0, LANES - 3 * N_HEADS)))
    return jnp.concatenate([w_in[:, :q0], wq, w_in[:, kv0:gn0], wgn, w_in[:, gm0:]], axis=1).astype(BF16)


def _prep_w_nsa_out(w, d):
    w = w.reshape(N_HEADS, 1, HEAD_DIM, d) * _kv_slot_mask()[:, :, None, None]
    return w.reshape(QPAD, d).astype(BF16)


def _block_diag(w_lin):
    g, c, _ = w_lin.shape
    eye = jnp.eye(g, dtype=F32)
    return (w_lin[:, :, None, :] * eye[:, None, :, None]).reshape(g * c, g * c).astype(BF16)


def kernel(x_prompt, x_sample, cache_kc, cache_vc, cache_ks, cache_vs, state_kw, state_vw, state_pool,
           page_table, c_prompt, c_sample, norm1_g, norm2_g, normf_g, w_ada, b_ada, w_in, w_pool_lin,
           pool_scale, w_cmp_k, w_cmp_v, w_pool_out, w_nsa_out, w_o, w_router, b_router, w_gate, w_up,
           w_down, ws_gate, ws_up, ws_down):
    depth = w_in.shape[0]
    assert depth == 1, "single-layer stack"
    bsz, seq, d = x_prompt.shape
    n_seq, ns, _ = x_sample.shape
    wbuf = state_kw.shape[2]
    lyr = 0

    n_tok = n_seq * ns
    c_all = jnp.concatenate([jnp.repeat(c_sample, ns, axis=0), c_prompt, jnp.zeros((-bsz % 8, d), F32)], axis=0)
    mod = _adaln(c_all, w_ada[lyr], b_ada[lyr])
    mod_p = mod[n_tok:n_tok + bsz].reshape(bsz, 1, 6 * d)
    mod_s = mod

    w2 = _prep_w_in(w_in[lyr], d)
    g1n = norm1_g[lyr].reshape(1, d)
    wk = w_cmp_k[lyr].reshape(CMP_BLOCK, KVW)
    wv = w_cmp_v[lyr].reshape(CMP_BLOCK, KVW)
    fin_w = (_block_diag(w_pool_lin[lyr]), pool_scale[lyr].reshape(1, POOL_W), w_pool_out[lyr].astype(BF16),
             _prep_w_nsa_out(w_nsa_out[lyr], d), w_o[lyr].astype(BF16), norm2_g[lyr].reshape(1, d),
             jnp.pad(w_router[lyr].T, ((0, LANES - N_EXPERTS), (0, 0))).astype(BF16),
             b_router[lyr].reshape(N_EXPERTS, 1))
    moe_w = (w_gate[lyr], w_up[lyr], w_down[lyr], ws_gate[lyr].astype(BF16), ws_up[lyr].astype(BF16),
             ws_down[lyr].astype(BF16))
    nf = normf_g.reshape(1, d)

    tm_p = 512
    (vp, kc, vc, ks, vs, kw, vw, gm, ksb, kwb, vst, vwt, qt, gst, pooled) = _in_proj(
        x_prompt, mod_p, mod_p, g1n, w2, tm=tm_p, prompt=True)
    kcmp, vcmpt = _compress(kc, vc, wk, wv)
    ynsa = _nsa_prompt(qt, gst, kcmp, vcmpt, ksb, vst, kwb, vwt)
    x1, u2, u2p, eid_t, gk_t, rank_t, counts = _finish(
        x_prompt, pooled, ynsa, gm, (mod_p, mod_p, mod_p), fin_w, tm=tm_p, sparse=True)

    xs3 = x_sample.reshape(1, n_tok, d)
    tm_s = 128
    (vp_s, kc_s, vc_s, ks_s, vs_s, kw_s, vw_s, gm_s, q_s, gs_s) = _in_proj(
        xs3, mod_s, mod_s, g1n, w2, tm=tm_s, prompt=False)
    two = lambda a: a.reshape(n_tok, a.shape[-1])
    q_rows = q_s.reshape(n_tok * N_HEADS, LANES)
    gate_rows = two(gs_s)[:, :3 * N_HEADS].reshape(n_tok, 3, N_HEADS).transpose(0, 2, 1)
    gate_rows = jnp.pad(gate_rows.reshape(n_tok * N_HEADS, 3), ((0, 0), (0, LANES - 3)))
    n_pool = cache_kc.shape[1]
    page = cache_kc.shape[2]
    rows_minor = lambda a: jnp.transpose(a, (0, 2, 3, 1)).reshape(a.shape[0], KVW, a.shape[1])
    caches = [rows_minor(c[lyr]) for c in (cache_kc, cache_vc, cache_ks, cache_vs)]
    o_rows, pooled_s, kw_next, vw_next = _nsa_sample(
        page_table, q_rows, gate_rows, [two(a) for a in (kc_s, vc_s, ks_s, vs_s, kw_s, vw_s)], two(vp_s),
        rows_minor(state_kw[lyr]), rows_minor(state_vw[lyr]), state_pool[lyr], wk, wv, caches)
    ynsa_s = o_rows.reshape(1, n_tok, QPAD)
    y_sorted, gath_idx = _moe_sorted_experts(u2p, eid_t, rank_t, counts, *moe_w[:3])
    ynsa_s, y_sorted = lax.optimization_barrier((ynsa_s, y_sorted))
    y_prompt = _moe_sorted_combine(y_sorted, gath_idx, gk_t, u2, x1, mod_p, nf, *moe_w[3:])
    x1_s, u2_s, gates_s = _finish(xs3, pooled_s.reshape(1, n_tok, POOL_W), ynsa_s, gm_s,
                                  (mod_s, mod_s, mod_s), fin_w, tm=tm_s, sparse=False)
    y_sample = _moe(u2_s, gates_s, x1_s, mod_s, nf, *moe_w, tm=n_tok).reshape(n_seq, ns, d)

    kvp = lambda a: a.reshape(1, bsz, seq, N_KV, HEAD_DIM)
    tailp = lambda a: jnp.pad(a, ((0, 0), (wbuf, 0), (0, 0)))[:, -wbuf:].reshape(1, bsz, wbuf, N_KV, HEAD_DIM)
    kvs = lambda a: a.reshape(1, n_seq, ns, N_KV, HEAD_DIM)
    wins = lambda a: jnp.transpose(a.reshape(n_seq, N_KV, HEAD_DIM, wbuf), (0, 3, 1, 2))[None]
    pool_p = vp[:, -POOL_BUF:][None]
    pool_s = jnp.concatenate([state_pool[lyr], vp_s.reshape(n_seq, ns, POOL_W)], axis=1)[None, :, -POOL_BUF:]
    return (y_prompt, y_sample, kvp(kc), kvp(vc), kvp(ks), kvp(vs), tailp(kw), tailp(vw), pool_p,
            kvs(kc_s), kvs(vc_s), kvs(ks_s), kvs(vs_s), wins(kw_next), wins(vw_next), pool_s)
```

```python
import functools

import jax
import jax.numpy as jnp
from jax import lax
from jax.experimental import pallas as pl
from jax.experimental.pallas import tpu as pltpu
from jax.experimental.pallas import tpu_sc as plsc

F32 = jnp.float32
BF16 = jnp.bfloat16
I32 = jnp.int32

POOL_WINDOWS = (2, 4, 8, 16)
POOL_GW = 64
POOL_W = 256
POOL_BUF = 15
N_HEADS = 8
HEAD_DIM = 64
N_KV = 2
GROUP = N_HEADS // N_KV
CMP_STRIDE = 16
CMP_BLOCK = 32
SEL_BLOCK = 64
TOP_BLOCKS = 16
WINDOW = 512
Q_BLOCK = 128
FORCE_SCORE = 1e4
N_EXPERTS = 64
N_EGROUPS = 8
EXPERTS_PER_GROUP = N_EXPERTS // N_EGROUPS
TOPK_GROUPS = 4
TOP_K = 8
ROUTED_SCALE = 2.5
EPS = 1e-6
NEG = -1e30
SLOPES = tuple(2.0 ** (-8.0 * (h + 1.0) / N_HEADS) for h in range(N_HEADS))

LANES = 128
QPAD = N_HEADS * LANES
KVW = N_KV * HEAD_DIM
VMEM_LIMIT = 56 * 1024 * 1024


def _cparams(sem):
    return pltpu.CompilerParams(dimension_semantics=sem, vmem_limit_bytes=VMEM_LIMIT)


def _dot(a, b):
    return jnp.dot(a, b, preferred_element_type=F32)


def _dot_nt(a, b):
    return lax.dot_general(a, b, (((1,), (1,)), ((), ())), preferred_element_type=F32)


def _dot_exact(a, b):
    return jnp.dot(a, b, preferred_element_type=F32, precision=lax.Precision.HIGHEST)


def _rows2d(ref):
    v = ref[...]
    return v.reshape(v.shape[-2], v.shape[-1])


def _rmsnorm(x, g):
    return x * lax.rsqrt(jnp.mean(x * x, axis=-1, keepdims=True) + EPS) * g


def _silu(x):
    return x * jax.nn.sigmoid(x)


def _adaln_kernel(c_ref, w_ref, b_ref, o_ref):
    s = _silu(c_ref[...]).astype(BF16)
    o_ref[...] = _dot(s, w_ref[...].astype(BF16)) + b_ref[...]


def _adaln(c, w_ada, b_ada):
    rows, d = c.shape
    n = w_ada.shape[1]
    tn = 512
    return pl.pallas_call(
        _adaln_kernel,
        out_shape=jax.ShapeDtypeStruct((rows, n), F32),
        grid=(n // tn,),
        in_specs=[pl.BlockSpec((rows, d), lambda j: (0, 0)),
                  pl.BlockSpec((d, tn), lambda j: (0, j)),
                  pl.BlockSpec((1, tn), lambda j: (0, j))],
        out_specs=pl.BlockSpec((rows, tn), lambda j: (0, j)),
        compiler_params=_cparams(("arbitrary",)),
        name="adaln",
    )(c, w_ada, b_ada.reshape(1, n))


_C_VP = 0
_C_Q = _C_VP + POOL_W
_C_KV = _C_Q + QPAD
_C_GN = _C_KV + 6 * KVW
_C_GM = _C_GN + LANES


def _pool_window_sums(ext, tm):
    s2 = ext + pltpu.roll(ext, 1, 0)
    s4 = s2 + pltpu.roll(s2, 2, 0)
    s8 = s4 + pltpu.roll(s4, 4, 0)
    s16 = s8 + pltpu.roll(s8, 8, 0)
    grp = lax.broadcasted_iota(I32, (1, POOL_W), 1) // POOL_GW
    pick = jnp.where(grp == 0, s2, jnp.where(grp == 1, s4, jnp.where(grp == 2, s8, s16)))
    return pick[16:16 + tm]


def _in_proj_kernel(x_ref, shift_ref, scale_ref, g_ref, w_ref,
                    vp_ref, kc_ref, vc_ref, ks_ref, vs_ref, kw_ref, vw_ref, gm_ref, *rest, tm, d, prompt):
    x = x_ref[...].reshape(tm, d)
    u = _rmsnorm(x, g_ref[...]) * (1.0 + _rows2d(scale_ref)) + _rows2d(shift_ref)
    ub = u.astype(BF16)

    head = _dot(ub, w_ref[:, 0:_C_GM])

    def proj(c0, n):
        return head[:, c0:c0 + n] if c0 + n <= _C_GM else _dot(ub, w_ref[:, c0:c0 + n])

    vp = proj(_C_VP, POOL_W)
    vp_ref[...] = vp.reshape(vp_ref.shape)
    kv = []
    for n, o32 in enumerate((kc_ref, vc_ref, ks_ref, vs_ref, kw_ref, vw_ref)):
        v = proj(_C_KV + n * KVW, KVW)
        o32[...] = v.reshape(o32.shape)
        kv.append(v)
    gm_ref[...] = jax.nn.sigmoid(proj(_C_GM, 2 * d)).reshape(gm_ref.shape)
    gs = jax.nn.sigmoid(proj(_C_GN, LANES))

    if not prompt:
        q_ref, gs_ref = rest
        q_ref[...] = proj(_C_Q, QPAD).astype(BF16).reshape(q_ref.shape)
        gs_ref[...] = gs.reshape(gs_ref.shape)
    else:
        ksb_ref, kwb_ref, vst_ref, vwt_ref, qt_ref, gst_ref, pooled_ref, halo_ref = rest
        ksb_ref[...] = kv[2].astype(BF16).reshape(ksb_ref.shape)
        kwb_ref[...] = kv[4].astype(BF16).reshape(kwb_ref.shape)
        vst_ref[...] = kv[3].T.astype(BF16).reshape(vst_ref.shape)
        vwt_ref[...] = kv[5].T.astype(BF16).reshape(vwt_ref.shape)
        gst_ref[...] = gs.T.reshape(gst_ref.shape)
        for h in range(N_HEADS):
            qt_ref[0, h] = proj(_C_Q + h * LANES, LANES).T.astype(BF16)
        j = pl.program_id(1)

        @pl.when(j == 0)
        def _():
            halo_ref[...] = jnp.zeros_like(halo_ref)

        ext = jnp.concatenate([halo_ref[...], vp], axis=0)
        sums = _pool_window_sums(ext, tm)
        pos = j * tm + lax.broadcasted_iota(I32, (tm, 1), 0)
        wcol = 2 << (lax.broadcasted_iota(I32, (1, POOL_W), 1) // POOL_GW)
        cnt = jnp.minimum(pos + 1, wcol).astype(F32)
        pooled_ref[...] = (sums / cnt - vp).astype(BF16).reshape(pooled_ref.shape)
        halo_ref[...] = vp[tm - 16:tm]


def _in_proj(x3, shift, scale, g1, w2, *, tm, prompt):
    g, r, d = x3.shape
    nt = r // tm
    per_row = shift.ndim == 2

    def tok(width, dtype):
        return (jax.ShapeDtypeStruct((g, r, width), dtype),
                pl.BlockSpec((1, tm, width), lambda b, j: (b, j, 0)))

    def tok_t(rows, dtype):
        return (jax.ShapeDtypeStruct((g, rows, r), dtype),
                pl.BlockSpec((1, rows, tm), lambda b, j: (b, 0, j)))

    outs = [tok(POOL_W, F32)] + [tok(KVW, F32)] * 6 + [tok(2 * d, F32)]
    scratch = []
    if prompt:
        outs += [tok(KVW, BF16), tok(KVW, BF16), tok_t(KVW, BF16), tok_t(KVW, BF16)]
        outs.append((jax.ShapeDtypeStruct((g, N_HEADS, LANES, r), BF16),
                     pl.BlockSpec((1, N_HEADS, LANES, tm), lambda b, j: (b, 0, 0, j))))
        outs += [tok_t(LANES, F32), tok(POOL_W, BF16)]
        scratch.append(pltpu.VMEM((16, POOL_W), F32))
    else:
        outs += [tok(QPAD, BF16), tok(LANES, F32)]
    if per_row:
        mod_spec = lambda col: pl.BlockSpec((tm, d), lambda b, j, col=col: (b * nt + j, col))
    else:
        mod_spec = lambda col: pl.BlockSpec((1, 1, d), lambda b, j, col=col: (b, 0, col))
    kern = functools.partial(_in_proj_kernel, tm=tm, d=d, prompt=prompt)
    return pl.pallas_call(
        kern,
        out_shape=[o[0] for o in outs],
        grid=(g, nt),
        in_specs=[pl.BlockSpec((1, tm, d), lambda b, j: (b, j, 0)),
                  mod_spec(0), mod_spec(1),
                  pl.BlockSpec((1, d), lambda b, j: (0, 0)),
                  pl.BlockSpec(w2.shape, lambda b, j: (0, 0))],
        out_specs=[o[1] for o in outs],
        scratch_shapes=scratch,
        compiler_params=_cparams(("arbitrary", "arbitrary")),
        name="in_proj_prompt" if prompt else "in_proj_sample",
    )(x3, shift, scale, g1, w2)


def _compress_kernel(kc_ref, vc_ref, wk_ref, wv_ref, okc_ref, ovc_ref, sh_ref, *, nc):
    last = lax.broadcasted_iota(I32, (nc, 1), 0) == nc - 1
    for src, w_ref, dst in ((kc_ref, wk_ref, okc_ref), (vc_ref, wv_ref, ovc_ref)):
        head = jnp.zeros((nc, KVW), F32)
        tail = jnp.zeros((nc, KVW), F32)
        for r in range(CMP_STRIDE):
            rows = src[pl.ds(r, nc, stride=CMP_STRIDE), :]
            head = head + rows * w_ref[r:r + 1, :]
            tail = tail + rows * w_ref[CMP_STRIDE + r:CMP_STRIDE + r + 1, :]
        sh_ref[0:nc, :] = tail
        sh_ref[nc:nc + 8, :] = jnp.zeros((8, KVW), F32)
        out = jnp.where(last, 0.0, head + sh_ref[1:nc + 1, :])
        dst[...] = (out if dst is okc_ref else out.T).astype(BF16)


def _compress(kc, vc, wk, wv):
    b, s, _ = kc.shape
    nc = s // CMP_STRIDE
    big = pl.BlockSpec((None, s, KVW), lambda i: (i, 0, 0))
    wsp = pl.BlockSpec((CMP_BLOCK, KVW), lambda i: (0, 0))
    return pl.pallas_call(
        functools.partial(_compress_kernel, nc=nc),
        out_shape=[jax.ShapeDtypeStruct((b, nc, KVW), BF16), jax.ShapeDtypeStruct((b, KVW, nc), BF16)],
        grid=(b,),
        in_specs=[big, big, wsp, wsp],
        out_specs=[pl.BlockSpec((None, nc, KVW), lambda i: (i, 0, 0)),
                   pl.BlockSpec((None, KVW, nc), lambda i: (i, 0, 0))],
        scratch_shapes=[pltpu.VMEM((nc + 8, KVW), F32)],
        compiler_params=_cparams(("arbitrary",)),
        name="compress",
    )(kc, vc, wk, wv)


def _topk_mask(vals, blk_f, n_top, axis=1):
    sel = jnp.zeros(vals.shape, F32)
    big = float(vals.shape[axis])
    for _ in range(n_top):
        mx = jnp.max(vals, axis=axis, keepdims=True)
        first = jnp.min(jnp.where(vals == mx, blk_f, big), axis=axis, keepdims=True)
        hit = blk_f == first
        sel = jnp.where(hit, 1.0, sel)
        vals = jnp.where(hit, -jnp.inf, vals)
    return sel


def _topk_mask_by_rank(vals, blk, n_valid, n_top):
    rank = jnp.zeros(vals.shape, F32)
    for j in range(n_valid):
        vj = vals[:, j:j + 1]
        beats = (vj > vals) | ((vj == vals) & (blk > j))
        rank = rank + jnp.where(beats, 1.0, 0.0)
    return jnp.where(rank < float(n_top), 1.0, 0.0)


def _pos_features(pos):
    hi = (pos // SEL_BLOCK).astype(F32)[:, None]
    lo = (pos % SEL_BLOCK).astype(F32)[:, None]
    return jnp.concatenate([hi, lo, jnp.zeros((pos.shape[0], LANES - 2), F32)], axis=1).astype(BF16)


def _importance_matrix(nc, nsel):
    j = jnp.arange(nc)[:, None]
    s = jnp.arange(nsel)[None, :]
    r = SEL_BLOCK // CMP_STRIDE
    a = (j >= r * s) & (j <= r * s + r - 1)
    b = (j + 1 >= r * s) & (j + 1 <= r * s + r - 1)
    return a.astype(F32) + b.astype(F32)


def _nsa_prompt_kernel(qt_ref, gst_ref, kc_ref, vct_ref, ks_ref, vst_ref, kw_ref, vwt_ref,
                       cfeat_ref, wfeat_ref, qfeat_ref, slope_ref,
                       y_ref, qk_scr, m_scr, l_scr, acc_scr, o_scr, sel_scr, imp_scr, flag_scr, ids_scr,
                       *, seq, tk, wl):
    i = pl.program_id(1)
    q0 = i * Q_BLOCK
    nq = Q_BLOCK
    gq = GROUP * nq
    nc = kc_ref.shape[0]
    nsel = seq // SEL_BLOCK
    n_top = min(TOP_BLOCKS, nsel)
    blk_per_tile = tk // SEL_BLOCK
    qpos = q0 + lax.broadcasted_iota(I32, (1, nq), 1)
    gst = gst_ref[...]

    crow = lax.broadcasted_iota(I32, (nc, nq), 0)
    cend = crow * CMP_STRIDE + (CMP_BLOCK - 1)
    mask_c = qpos >= cend
    kc = jnp.concatenate([kc_ref[...], cfeat_ref[...]], axis=1)
    vct = vct_ref[...]
    blk = lax.broadcasted_iota(I32, (nsel, nq), 0)
    blk_f = blk.astype(F32)
    cur = qpos // SEL_BLOCK
    forced = (blk == 0) | (blk == cur) | (blk == cur - 1)
    visible = blk * SEL_BLOCK <= qpos
    ws = pl.multiple_of(jnp.maximum(q0 - WINDOW, 0), Q_BLOCK)
    wpos = ws + lax.broadcasted_iota(I32, (wl, nq), 0)
    valid_w = lax.bitcast_convert_type(qpos - wpos, jnp.uint32) < WINDOW
    n_tiles = (q0 + nq + tk - 1) // tk
    half_rows = lax.broadcasted_iota(I32, (KVW, nq), 0) // HEAD_DIM
    tile_pos = lax.broadcasted_iota(I32, (SEL_BLOCK, nq), 0)

    def lanes4(x):
        return jnp.concatenate([x] * GROUP, axis=1)

    def gate_row(branch, k):
        r0 = branch * N_HEADS + k * GROUP
        return jnp.concatenate([gst[r0 + g:r0 + g + 1] for g in range(GROUP)], axis=1)

    mask_c4 = lanes4(mask_c)
    valid_w4 = lanes4(valid_w)
    kwt = jnp.concatenate([kw_ref[pl.ds(ws, wl), :], wfeat_ref[...]], axis=1)
    vwtt = vwt_ref[:, pl.ds(ws, wl)]

    for k in range(N_KV):
        for g in range(GROUP):
            qk_scr[k, 0:LANES, g * nq:(g + 1) * nq] = qt_ref[k * GROUP + g]
        qk_scr[k, LANES:2 * LANES, :] = qfeat_ref[k]
        qk = qk_scr[k]

        s = jnp.where(mask_c4, _dot(kc, qk), NEG)
        e = jnp.where(mask_c4, jnp.exp(s - jnp.max(s, axis=0, keepdims=True)), 0.0)
        l = jnp.sum(e, axis=0, keepdims=True)
        p = e * jnp.where(l > 0.0, 1.0 / l, 0.0)
        o_c = _dot(vct, p.astype(BF16))
        psum = p[:, 0:nq]
        for g in range(1, GROUP):
            psum = psum + p[:, g * nq:(g + 1) * nq]

        a = psum + jnp.where(crow == 0, 0.0, pltpu.roll(psum, 1, 0))
        a = a + pltpu.roll(a, nc - 1, 0)
        imp_scr[...] = a + pltpu.roll(a, nc - 2, 0)
        imp = imp_scr[pl.ds(0, nsel, stride=nc // nsel), :]
        vals = jnp.where(visible, jnp.where(forced, FORCE_SCORE, imp), NEG)
        sel = jnp.where(visible, _topk_mask(vals, blk_f, n_top, axis=0), 0.0)
        sel_scr[k] = jnp.where(sel > 0.5, 0.0, NEG)
        blk_any = jnp.max(sel, axis=1, keepdims=True)
        for t in range(seq // tk):
            hit = (jnp.max(blk_any[t * blk_per_tile:(t + 1) * blk_per_tile, :]) > 0.5).astype(I32)
            flag_scr[t] = hit if k == 0 else flag_scr[t] | hit

        s = jnp.where(valid_w4, _dot(kwt, qk), NEG)
        e = jnp.exp(s - jnp.max(s, axis=0, keepdims=True))
        p = e / jnp.sum(e, axis=0, keepdims=True)
        o_w = _dot(vwtt, p.astype(BF16))
        o_scr[k] = gate_row(0, k) * o_c + gate_row(2, k) * o_w

    m_scr[...] = jnp.full(m_scr.shape, NEG, F32)
    l_scr[...] = jnp.zeros(l_scr.shape, F32)
    acc_scr[...] = jnp.zeros(acc_scr.shape, F32)

    n_act = jnp.int32(0)
    for t in range(seq // tk):
        ids_scr[n_act] = t
        n_act = n_act + jnp.where((flag_scr[t] > 0) & (t < n_tiles), 1, 0)
    lane0 = lax.broadcasted_iota(I32, (tk, LANES), 1) == 0
    feat = wfeat_ref[0:tk, :]

    def sel_tiles(tiles):
        ta = tiles[0]
        starts = [pl.multiple_of(t * tk, tk) for t in tiles]
        keys = [jnp.concatenate([ks_ref[pl.ds(starts[0], tk), :], feat], axis=1)]
        for t, k0 in zip(tiles[1:], starts[1:]):
            shifted = jnp.where(lane0, feat.astype(F32) + ((t - ta) * blk_per_tile).astype(F32), feat.astype(F32))
            keys.append(jnp.concatenate([ks_ref[pl.ds(k0, tk), :], shifted.astype(BF16)], axis=1))
        kt = jnp.concatenate(keys, axis=0)
        vtt = jnp.concatenate([vst_ref[:, pl.ds(k0, tk)] for k0 in starts], axis=1)
        base = (starts[0] - q0).astype(F32)
        scores = _dot(kt, jnp.concatenate([qk_scr[k] for k in range(N_KV)], axis=1))
        probs, alphas = [], []
        for k in range(N_KV):
            neg = []
            for t, k0 in zip(tiles, starts):
                for j in range(blk_per_tile):
                    row = sel_scr[k, pl.ds(t * blk_per_tile + j, 1), :]
                    causal = qpos >= k0 + j * SEL_BLOCK + tile_pos
                    neg.append(jnp.where(causal, jnp.broadcast_to(row, (SEL_BLOCK, nq)), NEG))
            neg = lanes4(jnp.concatenate(neg, axis=0))
            off = slope_ref[k] * base
            s = scores[:, k * gq:(k + 1) * gq] + neg
            m_old = m_scr[k]
            m_new = jnp.maximum(m_old, jnp.max(s, axis=0, keepdims=True) + off)
            alpha = jnp.exp(m_old - m_new)
            p = jnp.exp(s - (m_new - off))
            l_scr[k] = alpha * l_scr[k] + jnp.sum(p, axis=0, keepdims=True)
            m_scr[k] = m_new
            probs.append(p.astype(BF16))
            alphas.append(alpha)
        pv = _dot(vtt, jnp.concatenate(probs, axis=1))
        for k in range(N_KV):
            acc_scr[k] = acc_scr[k] * alphas[k] + pv[:, k * gq:(k + 1) * gq]

    def sel_octet(i, carry):
        sel_tiles([ids_scr[8 * i + j] for j in range(8)])
        return carry

    done = (n_act // 8) * 8
    lax.fori_loop(0, n_act // 8, sel_octet, 0)
    pl.when(n_act % 8 >= 4)(lambda: sel_tiles([ids_scr[done + j] for j in range(4)]))
    pair0 = (n_act // 4) * 4
    pl.when(n_act % 4 >= 2)(lambda: sel_tiles([ids_scr[pair0], ids_scr[pair0 + 1]]))
    pl.when(n_act % 2 == 1)(lambda: sel_tiles([ids_scr[n_act - 1]]))

    for k in range(N_KV):
        o = o_scr[k] + gate_row(1, k) * (acc_scr[k] / l_scr[k])
        for g in range(GROUP):
            h = k * GROUP + g
            oh = jnp.where(half_rows == k, o[:, g * nq:(g + 1) * nq], 0.0)
            y_ref[:, h * LANES:(h + 1) * LANES] = oh.T.astype(BF16)


def _nsa_prompt(qt, gst, kcmp, vcmpt, ksb, vst, kwb, vwt):
    b, _, _, s = qt.shape
    nq = Q_BLOCK
    gq = GROUP * nq
    nc = kcmp.shape[1]
    nsel = s // SEL_BLOCK
    tk = 128
    wl = WINDOW + Q_BLOCK
    assert s % tk == 0 and s >= wl
    assert s // SEL_BLOCK <= 2 * LANES, "position // 64 must stay exact in bf16"
    cfeat = _pos_features(jnp.arange(nc) * CMP_STRIDE + (CMP_BLOCK - 1))
    wfeat = _pos_features(jnp.arange(wl))
    slope_rows = jnp.repeat(jnp.asarray(SLOPES, F32).reshape(N_KV, 1, GROUP), nq, axis=2)
    qfeat = jnp.concatenate([slope_rows * SEL_BLOCK, slope_rows, jnp.zeros((N_KV, LANES - 2, gq), F32)],
                            axis=1).astype(BF16)
    rows = lambda r: pl.BlockSpec((None, r, KVW), lambda bi, i: (bi, 0, 0))
    cols = lambda c: pl.BlockSpec((None, KVW, c), lambda bi, i: (bi, 0, 0))
    const = lambda a: pl.BlockSpec(a.shape, lambda bi, i: (0,) * a.ndim)
    return pl.pallas_call(
        functools.partial(_nsa_prompt_kernel, seq=s, tk=tk, wl=wl),
        out_shape=jax.ShapeDtypeStruct((b, s, QPAD), BF16),
        grid=(b, s // nq),
        in_specs=[pl.BlockSpec((None, N_HEADS, LANES, nq), lambda bi, i: (bi, 0, 0, i)),
                  pl.BlockSpec((None, LANES, nq), lambda bi, i: (bi, 0, i)),
                  rows(nc), cols(nc), rows(s), cols(s), rows(s), cols(s),
                  const(cfeat), const(wfeat), const(qfeat), const(slope_rows)],
        out_specs=pl.BlockSpec((None, nq, QPAD), lambda bi, i: (bi, i, 0)),
        scratch_shapes=[pltpu.VMEM((N_KV, 2 * LANES, gq), BF16),
                        pltpu.VMEM((N_KV, 1, gq), F32),
                        pltpu.VMEM((N_KV, 1, gq), F32),
                        pltpu.VMEM((N_KV, KVW, gq), F32),
                        pltpu.VMEM((N_KV, KVW, gq), F32),
                        pltpu.VMEM((N_KV, nsel, nq), F32),
                        pltpu.VMEM((nc, nq), F32),
                        pltpu.SMEM((s // tk,), I32),
                        pltpu.SMEM((s // tk,), I32)],
        compiler_params=_cparams(("arbitrary", "arbitrary")),
        name="nsa_prompt",
    )(qt, gst, kcmp, vcmpt, ksb, vst, kwb, vwt, cfeat, wfeat, qfeat, slope_rows)


def _nsa_sample_kernel(pt_ref, q_ref, gate_ref, kcn_ref, vcn_ref, ksn_ref, vsn_ref, kwn_ref, vwn_ref,
                       vpn_ref, skw_ref, svw_ref, spool_ref, wk_ref, wv_ref, imat_ref, emat_ref,
                       ckc_ref, cvc_ref, cks_ref, cvs_ref,
                       o_ref, pooled_ref, kwo_ref, vwo_ref, buf, buft, win_scr, tail_scr, vext_scr, sem,
                       *, sb, ns, past, n_pages, page, ncv, ncp, nks, wls, nselp, n_sel):
    step = pl.program_id(0)
    nrow = ns * N_HEADS
    par = step % 2

    def copies(n, side, r):
        out = []
        for p in range(n_pages):
            pg = pt_ref[n * n_pages + p]
            for c, cref in enumerate((ckc_ref, cvc_ref, cks_ref, cvs_ref)):
                out.append(pltpu.make_async_copy(cref.at[pg], buft.at[side, r, c, :, pl.ds(p * page, page)],
                                                 sem.at[side, r]))
        return out

    @pl.when(step == 0)
    def _():
        buf[:, :, past:, :] = jnp.zeros((sb, 2, buf.shape[2] - past, KVW), F32)
        tail_scr[...] = jnp.zeros_like(tail_scr)
        vext_scr[...] = jnp.zeros_like(vext_scr)
        for r in range(sb):
            for cp in copies(r, 0, r):
                cp.start()

    @pl.when(step + 1 < pl.num_programs(0))
    def _():
        for r in range(sb):
            for cp in copies((step + 1) * sb + r, 1 - par, r):
                cp.start()

    for r in range(sb):
        for cp in copies(step * sb + r, par, r):
            cp.wait()

    def new_rows_t(ref, r4, r):
        tail_scr[r, 0:ns, :] = ref[pl.ds(r4, ns), :]
        return tail_scr[r].T

    row = lax.broadcasted_iota(I32, (nrow, 1), 0)
    hrow = row % N_HEADS
    qpos = past + row // N_HEADS
    slope = jnp.exp2(-8.0 * (hrow.astype(F32) + 1.0) / N_HEADS)
    kvrow = hrow // GROUP
    lane = lax.broadcasted_iota(I32, (1, LANES), 1)
    half = (lane // HEAD_DIM) == kvrow
    grow = (row // N_HEADS) * N_KV + kvrow
    row8 = lax.broadcasted_iota(I32, (ns * N_KV, 1), 0)
    qpos8 = past + lax.broadcasted_iota(I32, (ns * N_KV, 1), 0) // N_KV
    blk = lax.broadcasted_iota(I32, (1, nselp), 1)
    blk_f = blk.astype(F32)
    cur = qpos8 // SEL_BLOCK
    forced = (blk == 0) | (blk == cur) | (blk == cur - 1)
    visible = (blk * SEL_BLOCK <= qpos8)
    inrange = blk < n_sel
    cend = lax.broadcasted_iota(I32, (1, ncp), 1) * CMP_STRIDE + (CMP_BLOCK - 1)
    mask_c = qpos >= cend
    bias_c = slope * (cend - qpos).astype(F32)
    kpos = lax.broadcasted_iota(I32, (1, nks), 1)
    causal_s = qpos >= kpos
    bias_s = slope * (kpos - qpos).astype(F32)
    wbuf = wls[0]
    wpos = past - wbuf + lax.broadcasted_iota(I32, (1, wls[1]), 1)
    dw = qpos - wpos
    valid_w = lax.bitcast_convert_type(dw, jnp.uint32) < WINDOW
    bias_w = slope * (wpos - qpos).astype(F32)
    prow = lax.broadcasted_iota(I32, (vext_scr.shape[1], 1), 0)
    wcol = 2 << (lax.broadcasted_iota(I32, (1, POOL_W), 1) // POOL_GW)
    n_top = min(TOP_BLOCKS, n_sel)

    def softmax_rows(s, mask):
        s = jnp.where(mask, s, NEG)
        mx = jnp.max(s, axis=1, keepdims=True)
        e = jnp.where(mask, jnp.exp(s - mx), 0.0)
        l = jnp.sum(e, axis=1, keepdims=True)
        return e * jnp.where(l > 0.0, 1.0 / l, 0.0)

    def seq_body(r):
        r4 = r * ns
        for c, new_ref in enumerate((kcn_ref, vcn_ref)):
            for p in range(n_pages):
                buf[r, c, p * page:(p + 1) * page, :] = buft[par, r, c, :, p * page:(p + 1) * page].T
            buf[r, c, past:past + ns, :] = new_ref[pl.ds(r4, ns), :]
        for c, new_ref in ((2, ksn_ref), (3, vsn_ref)):
            buft[par, r, c, :, past:past + LANES] = new_rows_t(new_ref, r4, r)

        qall = q_ref[pl.ds(r * nrow, nrow), :]
        gates = gate_ref[pl.ds(r * nrow, nrow), :]

        cmp = []
        for c, w_ref in ((0, wk_ref), (1, wv_ref)):
            span = CMP_STRIDE * ncv
            lo = buf[r, c, 0:span, :].reshape(ncv, CMP_STRIDE, KVW) * w_ref[0:CMP_STRIDE, :][None]
            hi = (buf[r, c, CMP_STRIDE:CMP_STRIDE + span, :].reshape(ncv, CMP_STRIDE, KVW)
                  * w_ref[CMP_STRIDE:CMP_BLOCK, :][None])
            acc = jnp.sum(lo + hi, axis=1)
            cmp.append(jnp.concatenate([acc, jnp.zeros((ncp - ncv, KVW), F32)], axis=0).astype(BF16))
        p_c = softmax_rows(_dot_nt(qall, cmp[0]) + bias_c, mask_c)
        o_c = _dot(p_c.astype(BF16), cmp[1])

        psum = jnp.zeros((ns * N_KV, ncp), F32)
        for i in range(ns * N_KV):
            r0 = (i // N_KV) * N_HEADS + (i % N_KV) * GROUP
            psum = jnp.where(row8 == i, jnp.sum(p_c[r0:r0 + GROUP], axis=0, keepdims=True), psum)
        imp = _dot_exact(psum, imat_ref[...])
        vals = jnp.where(inrange, jnp.where(visible, jnp.where(forced, FORCE_SCORE, imp), NEG), -jnp.inf)
        sel8 = _topk_mask_by_rank(vals, blk, n_sel, n_top)
        sel_rows = jnp.zeros((nrow, nselp), F32)
        for i in range(ns * N_KV):
            sel_rows = jnp.where(grow == i, sel8[i:i + 1], sel_rows)
        chosen = _dot(sel_rows.astype(BF16), emat_ref[...])

        kst = buft[par, r, 2].astype(BF16)
        vst = buft[par, r, 3].astype(BF16)
        p_s = softmax_rows(_dot(qall, kst) + bias_s, causal_s & (chosen > 0.5))
        o_s = _dot_nt(p_s.astype(BF16), vst)

        outs_w = []
        for state_ref, new_ref, next_ref in ((skw_ref, kwn_ref, kwo_ref), (svw_ref, vwn_ref, vwo_ref)):
            win_scr[r, :, 0:wbuf] = state_ref[r]
            win_scr[r, :, wbuf:wbuf + LANES] = new_rows_t(new_ref, r4, r)
            outs_w.append(win_scr[r].astype(BF16))
            next_ref[r] = win_scr[r, :, ns:ns + wbuf]
        p_w = softmax_rows(_dot(qall, outs_w[0]) + bias_w, valid_w)
        o_w = _dot_nt(p_w.astype(BF16), outs_w[1])

        o = gates[:, 0:1] * o_c + gates[:, 1:2] * o_s + gates[:, 2:3] * o_w
        o_ref[pl.ds(r * nrow, nrow), :] = jnp.where(half, o, 0.0).astype(BF16)

        vext_scr[r, 0:POOL_BUF, :] = spool_ref[r]
        vext_scr[r, POOL_BUF:POOL_BUF + ns, :] = vpn_ref[pl.ds(r4, ns), :]
        ext = vext_scr[r]
        for t in range(ns):
            hi = POOL_BUF + t
            inwin = (prow <= hi) & (prow > hi - wcol)
            ssum = jnp.sum(jnp.where(inwin, ext, 0.0), axis=0, keepdims=True)
            cnt = jnp.minimum(past + t + 1, wcol).astype(F32)
            pooled_ref[pl.ds(r4 + t, 1), :] = ssum / cnt - ext[hi:hi + 1, :]

    for r in range(sb):
        seq_body(r)


def _nsa_sample(page_table, q_rows, gate_rows, new6, vp_new, state_kwt, state_vwt, state_pool, wk, wv, caches):
    n_seq, n_pages = page_table.shape
    page = caches[0].shape[2]
    past = n_pages * page
    ns = vp_new.shape[0] // n_seq
    wbuf = state_kwt.shape[2]
    sb = 2
    nrow = ns * N_HEADS
    assert ns <= SEL_BLOCK and page == LANES
    t_pad = -(-(past + ns) // SEL_BLOCK) * SEL_BLOCK
    n_cmp = t_pad // CMP_STRIDE - 1
    ncv = -(-n_cmp // 8) * 8
    ncp = -(-ncv // LANES) * LANES
    nks = past + LANES
    n_sel = t_pad // SEL_BLOCK
    nselp = LANES
    assert n_sel <= nselp
    wlp = wbuf + LANES
    buf_rows = -(-(CMP_STRIDE * ncv + CMP_STRIDE) // 8) * 8
    imat = _importance_matrix(ncp, nselp)
    emat = (jnp.arange(nselp)[:, None] == (jnp.arange(nks)[None, :] // SEL_BLOCK)).astype(BF16)

    seqblk = lambda rows, w: pl.BlockSpec((sb * rows, w), lambda i, pt: (i, 0))
    const = lambda a: pl.BlockSpec(a.shape, lambda i, pt: (0,) * a.ndim)
    kern = functools.partial(
        _nsa_sample_kernel, sb=sb, ns=ns, past=past, n_pages=n_pages, page=page,
        ncv=ncv, ncp=ncp, nks=nks, wls=(wbuf, wlp), nselp=nselp, n_sel=n_sel)
    grid_spec = pltpu.PrefetchScalarGridSpec(
        num_scalar_prefetch=1,
        grid=(n_seq // sb,),
        in_specs=[seqblk(nrow, LANES), seqblk(nrow, LANES)] + [seqblk(ns, KVW)] * 6 + [seqblk(ns, POOL_W)]
        + [pl.BlockSpec((sb, KVW, wbuf), lambda i, pt: (i, 0, 0))] * 2
        + [pl.BlockSpec((sb, POOL_BUF, POOL_W), lambda i, pt: (i, 0, 0))]
        + [const(wk), const(wv), const(imat), const(emat)]
        + [pl.BlockSpec(memory_space=pl.ANY)] * 4,
        out_specs=[seqblk(nrow, LANES), seqblk(ns, POOL_W)]
        + [pl.BlockSpec((sb, KVW, wbuf), lambda i, pt: (i, 0, 0))] * 2,
        scratch_shapes=[pltpu.VMEM((sb, 2, buf_rows, KVW), F32),
                        pltpu.VMEM((2, sb, 4, KVW, nks), F32),
                        pltpu.VMEM((sb, KVW, wlp), F32),
                        pltpu.VMEM((sb, LANES, KVW), F32),
                        pltpu.VMEM((sb, 24, POOL_W), F32),
                        pltpu.SemaphoreType.DMA((2, sb))],
    )
    return pl.pallas_call(
        kern,
        out_shape=[jax.ShapeDtypeStruct((n_seq * nrow, LANES), BF16),
                   jax.ShapeDtypeStruct((n_seq * ns, POOL_W), F32)]
        + [jax.ShapeDtypeStruct((n_seq, KVW, wbuf), F32)] * 2,
        grid_spec=grid_spec,
        compiler_params=_cparams(("arbitrary",)),
        name="nsa_sample",
    )(page_table.reshape(-1), q_rows, gate_rows, *new6, vp_new, state_kwt, state_vwt, state_pool, wk, wv,
      imat, emat, *caches)


def _route(logits_t, bias_col, tm):
    sc = jax.nn.sigmoid(logits_t)
    biased = sc + bias_col
    epg = EXPERTS_PER_GROUP
    row8 = lax.broadcasted_iota(I32, (epg, tm), 0).astype(F32)
    ninf = -jnp.inf
    grp = jnp.zeros((N_EGROUPS, tm), F32)
    for g in range(N_EGROUPS):
        bg = biased[g * epg:(g + 1) * epg]
        m1 = jnp.max(bg, axis=0, keepdims=True)
        first = jnp.min(jnp.where(bg == m1, row8, float(epg)), axis=0, keepdims=True)
        m2 = jnp.max(jnp.where(row8 == first, ninf, bg), axis=0, keepdims=True)
        grp = jnp.where(row8 == float(g), m1 + m2, grp)
    keep = jnp.zeros((N_EGROUPS, tm), F32)
    vals = grp
    for _ in range(TOPK_GROUPS):
        mx = jnp.max(vals, axis=0, keepdims=True)
        first = jnp.min(jnp.where(vals == mx, row8, float(N_EGROUPS)), axis=0, keepdims=True)
        hit = row8 == first
        keep = jnp.where(hit, 1.0, keep)
        vals = jnp.where(hit, ninf, vals)
    masked = jnp.concatenate(
        [jnp.where(keep[g:g + 1] > 0.5, biased[g * epg:(g + 1) * epg], NEG) for g in range(N_EGROUPS)], axis=0)
    rowe = lax.broadcasted_iota(I32, (N_EXPERTS, tm), 0).astype(F32)
    chosen = jnp.zeros((N_EXPERTS, tm), F32)
    vals = masked
    picks = []
    for _ in range(TOP_K):
        mx = jnp.max(vals, axis=0, keepdims=True)
        first = jnp.min(jnp.where(vals == mx, rowe, float(N_EXPERTS)), axis=0, keepdims=True)
        hit = rowe == first
        chosen = jnp.where(hit, sc, chosen)
        vals = jnp.where(hit, ninf, vals)
        picks.append((hit, first))
    return ROUTED_SCALE * chosen / jnp.sum(chosen, axis=0, keepdims=True), picks


def _pack_bf16_pairs(x):
    c = x.shape[1] // 2
    bits = lambda v: lax.bitcast_convert_type(v.astype(BF16).astype(F32), jnp.uint32)
    return (bits(x[:, :c]) >> 16) | (bits(x[:, c:]) & jnp.uint32(0xFFFF0000))


def _unpack_bf16_pairs(w):
    lo = lax.bitcast_convert_type(w << 16, F32)
    hi = lax.bitcast_convert_type(w & jnp.uint32(0xFFFF0000), F32)
    return jnp.concatenate([lo, hi], axis=1)


def _finish_kernel(x_ref, pooled_ref, y_ref, gm_ref, g1_ref, shift_ref, scale_ref,
                   wlin_ref, pscale_ref, wpo_ref, wno_ref, wo_ref, n2_ref, wr_ref, br_ref,
                   *rest, tm, d, sparse):
    if sparse:
        tri_ref, x1_ref, u2_ref, up_ref, eid_ref, gk_ref, rank_ref, cnt_ref, carry_scr = rest
    else:
        x1_ref, u2_ref, gates_ref = rest
    x = x_ref[...].reshape(tm, d)
    pooled = pooled_ref[...].reshape(tm, POOL_W).astype(BF16)
    y_pool = _dot(pooled, wlin_ref[...]) * pscale_ref[...]
    a = _dot(y_pool.astype(BF16), wpo_ref[...])
    b = _dot(y_ref[...].reshape(tm, QPAD), wno_ref[...])
    gm = gm_ref[...].reshape(tm, 2 * d)
    merged = gm[:, :d] * a + gm[:, d:] * b
    x1 = x + _rows2d(g1_ref) * _dot(merged.astype(BF16), wo_ref[...])
    x1_ref[...] = x1.reshape(x1_ref.shape)
    u2 = _rmsnorm(x1, n2_ref[...]) * (1.0 + _rows2d(scale_ref)) + _rows2d(shift_ref)
    u2b = u2.astype(BF16)
    u2_ref[...] = u2b.reshape(u2_ref.shape)
    logits_t = _dot_nt(wr_ref[...], u2b)
    gates_t, picks = _route(logits_t[:N_EXPERTS], br_ref[...], tm)
    if not sparse:
        gates_t = jnp.concatenate([gates_t, jnp.zeros((LANES - N_EXPERTS, tm), F32)], axis=0)
        gates_ref[...] = gates_t.T.reshape(gates_ref.shape)
        return

    @pl.when((pl.program_id(0) == 0) & (pl.program_id(1) == 0))
    def _():
        carry_scr[...] = jnp.zeros_like(carry_scr)

    packed = _pack_bf16_pairs(u2)
    for s in range(up_ref.shape[0]):
        up_ref[s] = packed[:, s * SC_ROW_WORDS:(s + 1) * SC_ROW_WORDS]
    hit_all = picks[0][0]
    for hit, _ in picks[1:]:
        hit_all = hit_all | hit
    hits = jnp.where(hit_all, 1.0, 0.0).astype(BF16)
    before = _dot(hits, tri_ref[...]) + jnp.concatenate([carry_scr[...]] * (tm // LANES), axis=1)
    eids, gks, ranks = [], [], []
    for hit, first in picks:
        eids.append(first)
        gks.append(jnp.sum(jnp.where(hit, gates_t, 0.0), axis=0, keepdims=True))
        ranks.append(jnp.sum(jnp.where(hit, before, 0.0), axis=0, keepdims=True))
    pick_row = lax.broadcasted_iota(I32, (TOP_K, tm), 0)

    def stack(rows):
        out = jnp.zeros((TOP_K, tm), F32)
        for r, v in enumerate(rows):
            out = jnp.where(pick_row == r, v, out)
        return out

    eid_ref[...] = stack(eids).astype(I32).reshape(eid_ref.shape)
    gk_ref[...] = jnp.concatenate([stack(gks), jnp.zeros((LANES - TOP_K, tm), F32)], axis=0).T.reshape(gk_ref.shape)
    rank_ref[...] = stack(ranks).astype(I32).reshape(rank_ref.shape)
    carry_scr[...] += _dot(hits, jnp.ones((tm, LANES), BF16))
    cnt_ref[...] = carry_scr[...]


def _finish(x3, pooled, ynsa, gm, mods, wts, *, tm, sparse):
    g, r, d = x3.shape
    nt = r // tm
    g1, shift2, scale2 = mods
    per_row = g1.ndim == 2
    tok = lambda w: pl.BlockSpec((1, tm, w), lambda b, j: (b, j, 0))
    tok_t = lambda rows: pl.BlockSpec((1, rows, tm), lambda b, j: (b, 0, j))
    if per_row:
        mod_spec = lambda col: pl.BlockSpec((tm, d), lambda b, j, col=col: (b * nt + j, col))
    else:
        mod_spec = lambda col: pl.BlockSpec((1, 1, d), lambda b, j, col=col: (b, 0, col))
    const = lambda a: pl.BlockSpec(a.shape, lambda b, j: (0,) * a.ndim)
    out_shape = [jax.ShapeDtypeStruct((g, r, d), F32), jax.ShapeDtypeStruct((g, r, d), BF16)]
    out_specs = [tok(d), tok(d)]
    scratch = []
    if sparse:
        tri = (jnp.arange(tm)[:, None] < jnp.arange(tm)[None, :]).astype(BF16)
        wts = tuple(wts) + (tri,)
        split = d // 2 // SC_ROW_WORDS
        out_shape += [jax.ShapeDtypeStruct((split, g * r, SC_ROW_WORDS), jnp.uint32),
                      jax.ShapeDtypeStruct((g, TOP_K, r), I32), jax.ShapeDtypeStruct((g, r, LANES), F32),
                      jax.ShapeDtypeStruct((g, TOP_K, r), I32), jax.ShapeDtypeStruct((N_EXPERTS, LANES), F32)]
        out_specs += [pl.BlockSpec((split, tm, SC_ROW_WORDS), lambda b, j: (0, b * nt + j, 0)),
                      tok_t(TOP_K), tok(LANES), tok_t(TOP_K),
                      pl.BlockSpec((N_EXPERTS, LANES), lambda b, j: (0, 0))]
        scratch.append(pltpu.VMEM((N_EXPERTS, LANES), F32))
    else:
        out_shape.append(jax.ShapeDtypeStruct((g, r, LANES), F32))
        out_specs.append(tok(LANES))
    return pl.pallas_call(
        functools.partial(_finish_kernel, tm=tm, d=d, sparse=sparse),
        out_shape=out_shape,
        grid=(g, nt),
        in_specs=[tok(d), tok(POOL_W), tok(QPAD), tok(2 * d), mod_spec(2), mod_spec(3), mod_spec(4)]
        + [const(w) for w in wts],
        out_specs=out_specs,
        scratch_shapes=scratch,
        compiler_params=_cparams(("arbitrary", "arbitrary")),
        name="finish_route" if sparse else "finish",
    )(x3, pooled, ynsa, gm, g1, shift2, scale2, *wts)


def _moe_kernel(u_ref, gates_ref, x1_ref, g2_ref, nf_ref, wg_ref, wu_ref, wd_ref, sg_ref, su_ref, sd_ref,
                y_ref, acc_ref, *, tm, d, eps):
    e = pl.program_id(2)
    u = u_ref[...].reshape(tm, d)

    @pl.when(e == 0)
    def _():
        hs = _silu(_dot(u, sg_ref[...])) * _dot(u, su_ref[...])
        acc_ref[...] = _dot(hs.astype(BF16), sd_ref[...])

    gates = gates_ref[...].reshape(tm, LANES)
    lane = lax.broadcasted_iota(I32, (1, LANES), 1)
    hidden = []
    for j in range(eps):
        h = _silu(_dot(u, wg_ref[j].astype(BF16))) * _dot(u, wu_ref[j].astype(BF16))
        gate = jnp.sum(jnp.where(lane == e * eps + j, gates, 0.0), axis=1, keepdims=True)
        hidden.append((h * gate).astype(BF16))
    f = wd_ref.shape[1]
    acc_ref[...] += _dot(jnp.concatenate(hidden, axis=1), wd_ref[...].reshape(eps * f, d).astype(BF16))

    @pl.when(e == pl.num_programs(2) - 1)
    def _():
        x2 = x1_ref[...].reshape(tm, d) + _rows2d(g2_ref) * acc_ref[...]
        y_ref[...] = _rmsnorm(x2, nf_ref[...]).reshape(y_ref.shape)


def _moe(u2, gates, x1, g2, normf, w_gate, w_up, w_down, sg, su, sd, *, tm):
    g, r, d = x1.shape
    nt = r // tm
    ne, _, f = w_gate.shape
    per_row = g2.ndim == 2
    tok = lambda w: pl.BlockSpec((1, tm, w), lambda b, j, e: (b, j, 0))
    if per_row:
        g2_spec = pl.BlockSpec((tm, d), lambda b, j, e: (b * nt + j, 5))
    else:
        g2_spec = pl.BlockSpec((1, 1, d), lambda b, j, e: (b, 0, 5))
    once = pl.Buffered(buffer_count=1)
    const = lambda a: pl.BlockSpec(a.shape, lambda b, j, e: (0,) * a.ndim, pipeline_mode=once)
    eps = 4
    return pl.pallas_call(
        functools.partial(_moe_kernel, tm=tm, d=d, eps=eps),
        out_shape=jax.ShapeDtypeStruct((g, r, d), F32),
        grid=(g, nt, ne // eps),
        in_specs=[tok(d), tok(LANES),
                  pl.BlockSpec((1, tm, d), lambda b, j, e: (b, j, 0), pipeline_mode=once),
                  g2_spec, const(normf),
                  pl.BlockSpec((eps, d, f), lambda b, j, e: (e, 0, 0)),
                  pl.BlockSpec((eps, d, f), lambda b, j, e: (e, 0, 0)),
                  pl.BlockSpec((eps, f, d), lambda b, j, e: (e, 0, 0)),
                  const(sg), const(su), const(sd)],
        out_specs=tok(d),
        scratch_shapes=[pltpu.VMEM((tm, d), F32)],
        compiler_params=_cparams(("arbitrary", "arbitrary", "arbitrary")),
        name="moe",
    )(u2, gates, x1, g2, normf, w_gate, w_up, w_down, sg, su, sd)


SC_WINDOW = 128
SC_ROW_WORDS = 256
MOE_ROWS = 512


def _sc_mesh():
    return plsc.VectorSubcoreMesh(core_axis_name="c", subcore_axis_name="s")


def _sc_scatter_rows(src, dst_idx, n_dst):
    n, w = src.shape
    nk = dst_idx.shape[0]

    @pl.kernel(out_type=jax.ShapeDtypeStruct((n_dst, w), src.dtype), mesh=_sc_mesh(), scratch_types=[])
    def scatter(src_hbm, idx_hbm, dst_hbm):
        def body(rows_vmem, idx_vmem):
            pltpu.sync_copy(rows_vmem, dst_hbm.at[idx_vmem.at[0]])

        pltpu.emit_pipeline(
            body,
            grid=(nk, n // SC_WINDOW),
            in_specs=[pl.BlockSpec((SC_WINDOW, w), index_map=lambda k, i: (i, 0)),
                      pl.BlockSpec((1, SC_WINDOW), index_map=lambda k, i: (k, i))],
            out_specs=[],
            core_axis_name=("c", "s"),
            dimension_semantics=(pltpu.PARALLEL, pltpu.PARALLEL),
        )(src_hbm, idx_hbm)

    return scatter(src, dst_idx)


def _sc_gather_rows(src, idx):
    n, w = idx.shape[0], src.shape[1]

    @pl.kernel(out_type=jax.ShapeDtypeStruct((n, w), src.dtype), mesh=_sc_mesh(), scratch_types=[])
    def gather(src_hbm, idx_hbm, out_hbm):
        def body(idx_vmem, out_vmem):
            pltpu.sync_copy(src_hbm.at[idx_vmem.at[0]], out_vmem)

        pltpu.emit_pipeline(
            body,
            grid=(n // SC_WINDOW,),
            in_specs=[pl.BlockSpec((1, SC_WINDOW), index_map=lambda i: (0, i))],
            out_specs=[pl.BlockSpec((SC_WINDOW, w), index_map=lambda i: (i, 0))],
            core_axis_name=("c", "s"),
            dimension_semantics=(pltpu.PARALLEL,),
        )(idx_hbm, out_hbm)

    return gather(src, idx.reshape(1, n))


MOE_BLOCKS_PER_STEP = 2


def _expert_rows_kernel(te_ref, nt_ref, x_ref, *refs):
    y_ref = refs[-1]
    split = x_ref.shape[0]
    for j in range(MOE_BLOCKS_PER_STEP):
        wg_ref, wu_ref, wd_ref = refs[3 * j:3 * j + 3]
        rows = slice(j * MOE_ROWS, (j + 1) * MOE_ROWS)

        @pl.when(pl.program_id(0) * MOE_BLOCKS_PER_STEP + j < nt_ref[0])
        def _():
            x = _unpack_bf16_pairs(jnp.concatenate([x_ref[s, rows] for s in range(split)], axis=1)).astype(BF16)
            h = _silu(_dot(x, wg_ref[...].astype(BF16))) * _dot(x, wu_ref[...].astype(BF16))
            y = _pack_bf16_pairs(_dot(h.astype(BF16), wd_ref[...].astype(BF16)))
            for s in range(split):
                y_ref[s, rows] = y[:, s * SC_ROW_WORDS:(s + 1) * SC_ROW_WORDS]


def _expert_rows(tile_expert, n_tiles, x_sorted, w_gate, w_up, w_down):
    split, p, words = x_sorted.shape
    ne, d, f = w_gate.shape
    bps = MOE_BLOCKS_PER_STEP
    wspec = lambda a, b, j: pl.BlockSpec((None, a, b), lambda i, te, nt: (te[i * bps + j], 0, 0))
    rows = pl.BlockSpec((split, bps * MOE_ROWS, words), lambda i, te, nt: (0, i, 0))
    weights, wspecs = [], []
    for j in range(bps):
        weights += [w_gate, w_up, w_down]
        wspecs += [wspec(d, f, j), wspec(d, f, j), wspec(f, d, j)]
    grid_spec = pltpu.PrefetchScalarGridSpec(
        num_scalar_prefetch=2,
        grid=(p // (bps * MOE_ROWS),),
        in_specs=[rows] + wspecs,
        out_specs=rows,
    )
    return pl.pallas_call(
        _expert_rows_kernel,
        out_shape=jax.ShapeDtypeStruct((split, p, words), jnp.uint32),
        grid_spec=grid_spec,
        compiler_params=_cparams(("arbitrary",)),
        name="expert_rows",
    )(tile_expert, n_tiles, x_sorted, *weights)


def _combine_kernel(yg_ref, gk_ref, u_ref, x1_ref, g2_ref, nf_ref, sg_ref, su_ref, sd_ref, y_ref, *, tm, d):
    u = u_ref[...].reshape(tm, d)
    hs = _silu(_dot(u, sg_ref[...])) * _dot(u, su_ref[...])
    acc = _dot(hs.astype(BF16), sd_ref[...])
    gk = gk_ref[...].reshape(tm, LANES)
    lane = lax.broadcasted_iota(I32, (1, LANES), 1)
    split = yg_ref.shape[0]
    for k in range(TOP_K):
        gate = jnp.sum(jnp.where(lane == k, gk, 0.0), axis=1, keepdims=True)
        words = jnp.concatenate([yg_ref[s, k] for s in range(split)], axis=1)
        acc = acc + gate * _unpack_bf16_pairs(words)
    x2 = x1_ref[...].reshape(tm, d) + _rows2d(g2_ref) * acc
    y_ref[...] = _rmsnorm(x2, nf_ref[...]).reshape(y_ref.shape)


def _combine(yg, gk, u2, x1, g2, normf, sg, su, sd, *, tm):
    g, r, d = x1.shape
    nt = r // tm
    split, _, _, words = yg.shape
    tok = lambda w: pl.BlockSpec((1, tm, w), lambda b, j: (b, j, 0))
    const = lambda a: pl.BlockSpec(a.shape, lambda b, j: (0,) * a.ndim)
    return pl.pallas_call(
        functools.partial(_combine_kernel, tm=tm, d=d),
        out_shape=jax.ShapeDtypeStruct((g, r, d), F32),
        grid=(g, nt),
        in_specs=[pl.BlockSpec((split, TOP_K, tm, words), lambda b, j: (0, 0, b * nt + j, 0)),
                  tok(LANES), tok(d), tok(d),
                  pl.BlockSpec((1, 1, d), lambda b, j: (b, 0, 5)), const(normf), const(sg), const(su), const(sd)],
        out_specs=tok(d),
        compiler_params=_cparams(("arbitrary", "arbitrary")),
        name="moe_combine",
    )(yg, gk, u2, x1, g2, normf, sg, su, sd)


def _moe_sorted_experts(u2p, eid_t, rank_t, counts, w_gate, w_up, w_down):
    split, n, _ = u2p.shape
    ne = w_gate.shape[0]
    cnt = counts[:, 0].astype(I32)
    padded = -(-cnt // MOE_ROWS) * MOE_ROWS
    seg_end = jnp.cumsum(padded)
    seg_start = seg_end - padded
    p_rows = n * TOP_K + ne * MOE_ROWS
    eid = eid_t.transpose(1, 0, 2).reshape(TOP_K, n)
    start = jnp.sum(jnp.where(eid[:, :, None] == jnp.arange(ne, dtype=I32), seg_start, 0), axis=-1)
    pos = start + rank_t.transpose(1, 0, 2).reshape(TOP_K, n)
    first_row = jnp.arange(p_rows // MOE_ROWS, dtype=I32) * MOE_ROWS
    tile_expert = jnp.minimum(jnp.sum(seg_end[None, :] <= first_row[:, None], axis=1), ne - 1).astype(I32)
    n_tiles = (seg_end[-1:] // MOE_ROWS).astype(I32)
    scat_idx = jnp.concatenate([pos + s * p_rows for s in range(split)], axis=1)
    gath_idx = jnp.concatenate([pos.reshape(-1) + s * p_rows for s in range(split)])
    x_sorted = _sc_scatter_rows(u2p.reshape(split * n, SC_ROW_WORDS), scat_idx, split * p_rows)
    y_sorted = _expert_rows(tile_expert, n_tiles, x_sorted.reshape(split, p_rows, SC_ROW_WORDS),
                            w_gate, w_up, w_down)
    return y_sorted.reshape(split * p_rows, SC_ROW_WORDS), gath_idx


def _moe_sorted_combine(y_sorted, gath_idx, gk, u2, x1, g2, normf, sg, su, sd):
    g, r, _ = x1.shape
    split = gath_idx.shape[0] // (TOP_K * g * r)
    yg = _sc_gather_rows(y_sorted, gath_idx)
    return _combine(yg.reshape(split, TOP_K, g * r, SC_ROW_WORDS), gk, u2, x1, g2, normf, sg, su, sd, tm=512)


def _kv_slot_mask():
    return (jnp.arange(N_HEADS)[:, None] // GROUP == jnp.arange(N_KV)[None, :]).astype(F32)


def _prep_w_in(w_in, d):
    q0 = POOL_W
    kv0 = q0 + N_HEADS * HEAD_DIM
    gn0 = kv0 + 6 * KVW
    gm0 = gn0 + 3 * N_HEADS
    wq = w_in[:, q0:kv0].reshape(d, N_HEADS, 1, HEAD_DIM) * (HEAD_DIM ** -0.5)
    wq = (wq * _kv_slot_mask()[None, :, :, None]).reshape(d, QPAD)
    wgn = jnp.pad(w_in[:, gn0:gm0], ((0, 0), (0, LANES - 3 * N_HEADS)))
    return jnp.concatenate([w_in[:, :q0], wq, w_in[:, kv0:gn0], wgn, w_in[:, gm0:]], axis=1).astype(BF16)


def _prep_w_nsa_out(w, d):
    w = w.reshape(N_HEADS, 1, HEAD_DIM, d) * _kv_slot_mask()[:, :, None, None]
    return w.reshape(QPAD, d).astype(BF16)


def _block_diag(w_lin):
    g, c, _ = w_lin.shape
    eye = jnp.eye(g, dtype=F32)
    return (w_lin[:, :, None, :] * eye[:, None, :, None]).reshape(g * c, g * c).astype(BF16)


def kernel(x_prompt, x_sample, cache_kc, cache_vc, cache_ks, cache_vs, state_kw, state_vw, state_pool,
           page_table, c_prompt, c_sample, norm1_g, norm2_g, normf_g, w_ada, b_ada, w_in, w_pool_lin,
           pool_scale, w_cmp_k, w_cmp_v, w_pool_out, w_nsa_out, w_o, w_router, b_router, w_gate, w_up,
           w_down, ws_gate, ws_up, ws_down):
    depth = w_in.shape[0]
    assert depth == 1, "single-layer stack"
    bsz, seq, d = x_prompt.shape
    n_seq, ns, _ = x_sample.shape
    wbuf = state_kw.shape[2]
    lyr = 0

    n_tok = n_seq * ns
    c_all = jnp.concatenate([jnp.repeat(c_sample, ns, axis=0), c_prompt, jnp.zeros((-bsz % 8, d), F32)], axis=0)
    mod = _adaln(c_all, w_ada[lyr], b_ada[lyr])
    mod_p = mod[n_tok:n_tok + bsz].reshape(bsz, 1, 6 * d)
    mod_s = mod

    w2 = _prep_w_in(w_in[lyr], d)
    g1n = norm1_g[lyr].reshape(1, d)
    wk = w_cmp_k[lyr].reshape(CMP_BLOCK, KVW)
    wv = w_cmp_v[lyr].reshape(CMP_BLOCK, KVW)
    fin_w = (_block_diag(w_pool_lin[lyr]), pool_scale[lyr].reshape(1, POOL_W), w_pool_out[lyr].astype(BF16),
             _prep_w_nsa_out(w_nsa_out[lyr], d), w_o[lyr].astype(BF16), norm2_g[lyr].reshape(1, d),
             jnp.pad(w_router[lyr].T, ((0, LANES - N_EXPERTS), (0, 0))).astype(BF16),
             b_router[lyr].reshape(N_EXPERTS, 1))
    moe_w = (w_gate[lyr], w_up[lyr], w_down[lyr], ws_gate[lyr].astype(BF16), ws_up[lyr].astype(BF16),
             ws_down[lyr].astype(BF16))
    nf = normf_g.reshape(1, d)

    tm_p = 512
    (vp, kc, vc, ks, vs, kw, vw, gm, ksb, kwb, vst, vwt, qt, gst, pooled) = _in_proj(
        x_prompt, mod_p, mod_p, g1n, w2, tm=tm_p, prompt=True)
    kcmp, vcmpt = _compress(kc, vc, wk, wv)
    ynsa = _nsa_prompt(qt, gst, kcmp, vcmpt, ksb, vst, kwb, vwt)
    x1, u2, u2p, eid_t, gk_t, rank_t, counts = _finish(
        x_prompt, pooled, ynsa, gm, (mod_p, mod_p, mod_p), fin_w, tm=tm_p, sparse=True)

    xs3 = x_sample.reshape(1, n_tok, d)
    tm_s = 128
    (vp_s, kc_s, vc_s, ks_s, vs_s, kw_s, vw_s, gm_s, q_s, gs_s) = _in_proj(
        xs3, mod_s, mod_s, g1n, w2, tm=tm_s, prompt=False)
    two = lambda a: a.reshape(n_tok, a.shape[-1])
    q_rows = q_s.reshape(n_tok * N_HEADS, LANES)
    gate_rows = two(gs_s)[:, :3 * N_HEADS].reshape(n_tok, 3, N_HEADS).transpose(0, 2, 1)
    gate_rows = jnp.pad(gate_rows.reshape(n_tok * N_HEADS, 3), ((0, 0), (0, LANES - 3)))
    n_pool = cache_kc.shape[1]
    page = cache_kc.shape[2]
    rows_minor = lambda a: jnp.transpose(a, (0, 2, 3, 1)).reshape(a.shape[0], KVW, a.shape[1])
    caches = [rows_minor(c[lyr]) for c in (cache_kc, cache_vc, cache_ks, cache_vs)]
    o_rows, pooled_s, kw_next, vw_next = _nsa_sample(
        page_table, q_rows, gate_rows, [two(a) for a in (kc_s, vc_s, ks_s, vs_s, kw_s, vw_s)], two(vp_s),
        rows_minor(state_kw[lyr]), rows_minor(state_vw[lyr]), state_pool[lyr], wk, wv, caches)
    ynsa_s = o_rows.reshape(1, n_tok, QPAD)
    y_sorted, gath_idx = _moe_sorted_experts(u2p, eid_t, rank_t, counts, *moe_w[:3])
    ynsa_s, y_sorted = lax.optimization_barrier((ynsa_s, y_sorted))
    y_prompt = _moe_sorted_combine(y_sorted, gath_idx, gk_t, u2, x1, mod_p, nf, *moe_w[3:])
    x1_s, u2_s, gates_s = _finish(xs3, pooled_s.reshape(1, n_tok, POOL_W), ynsa_s, gm_s,
                                  (mod_s, mod_s, mod_s), fin_w, tm=tm_s, sparse=False)
    y_sample = _moe(u2_s, gates_s, x1_s, mod_s, nf, *moe_w, tm=n_tok).reshape(n_seq, ns, d)

    kvp = lambda a: a.reshape(1, bsz, seq, N_KV, HEAD_DIM)
    tailp = lambda a: jnp.pad(a, ((0, 0), (wbuf, 0), (0, 0)))[:, -wbuf:].reshape(1, bsz, wbuf, N_KV, HEAD_DIM)
    kvs = lambda a: a.reshape(1, n_seq, ns, N_KV, HEAD_DIM)
    wins = lambda a: jnp.transpose(a.reshape(n_seq, N_KV, HEAD_DIM, wbuf), (0, 3, 1, 2))[None]
    pool_p = vp[:, -POOL_BUF:][None]
    pool_s = jnp.concatenate([state_pool[lyr], vp_s.reshape(n_seq, ns, POOL_W)], axis=1)[None, :, -POOL_BUF:]
    return (y_prompt, y_sample, kvp(kc), kvp(vc), kvp(ks), kvp(vs), tailp(kw), tailp(vw), pool_p,
            kvs(kc_s), kvs(vc_s), kvs(ks_s), kvs(vs_s), wins(kw_next), wins(vw_next), pool_s)
```

```python
import functools

import jax
import jax.numpy as jnp
from jax import lax
from jax.experimental import pallas as pl
from jax.experimental.pallas import tpu as pltpu
from jax.experimental.pallas import tpu_sc as plsc

F32 = jnp.float32
BF16 = jnp.bfloat16
I32 = jnp.int32

POOL_WINDOWS = (2, 4, 8, 16)
POOL_GW = 64
POOL_W = 256
POOL_BUF = 15
N_HEADS = 8
HEAD_DIM = 64
N_KV = 2
GROUP = N_HEADS // N_KV
CMP_STRIDE = 16
CMP_BLOCK = 32
SEL_BLOCK = 64
TOP_BLOCKS = 16
WINDOW = 512
Q_BLOCK = 128
FORCE_SCORE = 1e4
N_EXPERTS = 64
N_EGROUPS = 8
EXPERTS_PER_GROUP = N_EXPERTS // N_EGROUPS
TOPK_GROUPS = 4
TOP_K = 8
ROUTED_SCALE = 2.5
EPS = 1e-6
NEG = -1e30
SLOPES = tuple(2.0 ** (-8.0 * (h + 1.0) / N_HEADS) for h in range(N_HEADS))

LANES = 128
QPAD = N_HEADS * LANES
KVW = N_KV * HEAD_DIM
VMEM_LIMIT = 56 * 1024 * 1024


def _cparams(sem):
    return pltpu.CompilerParams(dimension_semantics=sem, vmem_limit_bytes=VMEM_LIMIT)


def _dot(a, b):
    return jnp.dot(a, b, preferred_element_type=F32)


def _dot_nt(a, b):
    return lax.dot_general(a, b, (((1,), (1,)), ((), ())), preferred_element_type=F32)


def _dot_exact(a, b):
    return jnp.dot(a, b, preferred_element_type=F32, precision=lax.Precision.HIGHEST)


def _rows2d(ref):
    v = ref[...]
    return v.reshape(v.shape[-2], v.shape[-1])


def _rmsnorm(x, g):
    return x * lax.rsqrt(jnp.mean(x * x, axis=-1, keepdims=True) + EPS) * g


def _silu(x):
    return x * jax.nn.sigmoid(x)


def _adaln_kernel(c_ref, w_ref, b_ref, o_ref):
    s = _silu(c_ref[...]).astype(BF16)
    o_ref[...] = _dot(s, w_ref[...].astype(BF16)) + b_ref[...]


def _adaln(c, w_ada, b_ada):
    rows, d = c.shape
    n = w_ada.shape[1]
    tn = 512
    return pl.pallas_call(
        _adaln_kernel,
        out_shape=jax.ShapeDtypeStruct((rows, n), F32),
        grid=(n // tn,),
        in_specs=[pl.BlockSpec((rows, d), lambda j: (0, 0)),
                  pl.BlockSpec((d, tn), lambda j: (0, j)),
                  pl.BlockSpec((1, tn), lambda j: (0, j))],
        out_specs=pl.BlockSpec((rows, tn), lambda j: (0, j)),
        compiler_params=_cparams(("arbitrary",)),
        name="adaln",
    )(c, w_ada, b_ada.reshape(1, n))


_C_VP = 0
_C_Q = _C_VP + POOL_W
_C_KV = _C_Q + QPAD
_C_GN = _C_KV + 6 * KVW
_C_GM = _C_GN + LANES


def _pool_window_sums(ext, tm):
    s2 = ext + pltpu.roll(ext, 1, 0)
    s4 = s2 + pltpu.roll(s2, 2, 0)
    s8 = s4 + pltpu.roll(s4, 4, 0)
    s16 = s8 + pltpu.roll(s8, 8, 0)
    grp = lax.broadcasted_iota(I32, (1, POOL_W), 1) // POOL_GW
    pick = jnp.where(grp == 0, s2, jnp.where(grp == 1, s4, jnp.where(grp == 2, s8, s16)))
    return pick[16:16 + tm]


def _in_proj_kernel(x_ref, shift_ref, scale_ref, g_ref, w_ref,
                    vp_ref, kc_ref, vc_ref, ks_ref, vs_ref, kw_ref, vw_ref, gm_ref, *rest, tm, d, prompt):
    x = x_ref[...].reshape(tm, d)
    u = _rmsnorm(x, g_ref[...]) * (1.0 + _rows2d(scale_ref)) + _rows2d(shift_ref)
    ub = u.astype(BF16)

    head = _dot(ub, w_ref[:, 0:_C_GM])

    def proj(c0, n):
        return head[:, c0:c0 + n] if c0 + n <= _C_GM else _dot(ub, w_ref[:, c0:c0 + n])

    vp = proj(_C_VP, POOL_W)
    vp_ref[...] = vp.reshape(vp_ref.shape)
    kv = []
    for n, o32 in enumerate((kc_ref, vc_ref, ks_ref, vs_ref, kw_ref, vw_ref)):
        v = proj(_C_KV + n * KVW, KVW)
        o32[...] = v.reshape(o32.shape)
        kv.append(v)
    gm_ref[...] = jax.nn.sigmoid(proj(_C_GM, 2 * d)).reshape(gm_ref.shape)
    gs = jax.nn.sigmoid(proj(_C_GN, LANES))

    if not prompt:
        q_ref, gs_ref = rest
        q_ref[...] = proj(_C_Q, QPAD).astype(BF16).reshape(q_ref.shape)
        gs_ref[...] = gs.reshape(gs_ref.shape)
    else:
        ksb_ref, kwb_ref, vst_ref, vwt_ref, qt_ref, gst_ref, pooled_ref, halo_ref = rest
        ksb_ref[...] = kv[2].astype(BF16).reshape(ksb_ref.shape)
        kwb_ref[...] = kv[4].astype(BF16).reshape(kwb_ref.shape)
        vst_ref[...] = kv[3].T.astype(BF16).reshape(vst_ref.shape)
        vwt_ref[...] = kv[5].T.astype(BF16).reshape(vwt_ref.shape)
        gst_ref[...] = gs.T.reshape(gst_ref.shape)
        for h in range(N_HEADS):
            qt_ref[0, h] = proj(_C_Q + h * LANES, LANES).T.astype(BF16)
        j = pl.program_id(1)

        @pl.when(j == 0)
        def _():
            halo_ref[...] = jnp.zeros_like(halo_ref)

        ext = jnp.concatenate([halo_ref[...], vp], axis=0)
        sums = _pool_window_sums(ext, tm)
        pos = j * tm + lax.broadcasted_iota(I32, (tm, 1), 0)
        wcol = 2 << (lax.broadcasted_iota(I32, (1, POOL_W), 1) // POOL_GW)
        cnt = jnp.minimum(pos + 1, wcol).astype(F32)
        pooled_ref[...] = (sums / cnt - vp).astype(BF16).reshape(pooled_ref.shape)
        halo_ref[...] = vp[tm - 16:tm]


def _in_proj(x3, shift, scale, g1, w2, *, tm, prompt):
    g, r, d = x3.shape
    nt = r // tm
    per_row = shift.ndim == 2

    def tok(width, dtype):
        return (jax.ShapeDtypeStruct((g, r, width), dtype),
                pl.BlockSpec((1, tm, width), lambda b, j: (b, j, 0)))

    def tok_t(rows, dtype):
        return (jax.ShapeDtypeStruct((g, rows, r), dtype),
                pl.BlockSpec((1, rows, tm), lambda b, j: (b, 0, j)))

    outs = [tok(POOL_W, F32)] + [tok(KVW, F32)] * 6 + [tok(2 * d, F32)]
    scratch = []
    if prompt:
        outs += [tok(KVW, BF16), tok(KVW, BF16), tok_t(KVW, BF16), tok_t(KVW, BF16)]
        outs.append((jax.ShapeDtypeStruct((g, N_HEADS, LANES, r), BF16),
                     pl.BlockSpec((1, N_HEADS, LANES, tm), lambda b, j: (b, 0, 0, j))))
        outs += [tok_t(LANES, F32), tok(POOL_W, BF16)]
        scratch.append(pltpu.VMEM((16, POOL_W), F32))
    else:
        outs += [tok(QPAD, BF16), tok(LANES, F32)]
    if per_row:
        mod_spec = lambda col: pl.BlockSpec((tm, d), lambda b, j, col=col: (b * nt + j, col))
    else:
        mod_spec = lambda col: pl.BlockSpec((1, 1, d), lambda b, j, col=col: (b, 0, col))
    kern = functools.partial(_in_proj_kernel, tm=tm, d=d, prompt=prompt)
    return pl.pallas_call(
        kern,
        out_shape=[o[0] for o in outs],
        grid=(g, nt),
        in_specs=[pl.BlockSpec((1, tm, d), lambda b, j: (b, j, 0)),
                  mod_spec(0), mod_spec(1),
                  pl.BlockSpec((1, d), lambda b, j: (0, 0)),
                  pl.BlockSpec(w2.shape, lambda b, j: (0, 0))],
        out_specs=[o[1] for o in outs],
        scratch_shapes=scratch,
        compiler_params=_cparams(("arbitrary", "arbitrary")),
        name="in_proj_prompt" if prompt else "in_proj_sample",
    )(x3, shift, scale, g1, w2)


def _compress_kernel(kc_ref, vc_ref, wk_ref, wv_ref, okc_ref, ovc_ref, sh_ref, *, nc):
    last = lax.broadcasted_iota(I32, (nc, 1), 0) == nc - 1
    for src, w_ref, dst in ((kc_ref, wk_ref, okc_ref), (vc_ref, wv_ref, ovc_ref)):
        head = jnp.zeros((nc, KVW), F32)
        tail = jnp.zeros((nc, KVW), F32)
        for r in range(CMP_STRIDE):
            rows = src[pl.ds(r, nc, stride=CMP_STRIDE), :]
            head = head + rows * w_ref[r:r + 1, :]
            tail = tail + rows * w_ref[CMP_STRIDE + r:CMP_STRIDE + r + 1, :]
        sh_ref[0:nc, :] = tail
        sh_ref[nc:nc + 8, :] = jnp.zeros((8, KVW), F32)
        out = jnp.where(last, 0.0, head + sh_ref[1:nc + 1, :])
        dst[...] = (out if dst is okc_ref else out.T).astype(BF16)


def _compress(kc, vc, wk, wv):
    b, s, _ = kc.shape
    nc = s // CMP_STRIDE
    big = pl.BlockSpec((None, s, KVW), lambda i: (i, 0, 0))
    wsp = pl.BlockSpec((CMP_BLOCK, KVW), lambda i: (0, 0))
    return pl.pallas_call(
        functools.partial(_compress_kernel, nc=nc),
        out_shape=[jax.ShapeDtypeStruct((b, nc, KVW), BF16), jax.ShapeDtypeStruct((b, KVW, nc), BF16)],
        grid=(b,),
        in_specs=[big, big, wsp, wsp],
        out_specs=[pl.BlockSpec((None, nc, KVW), lambda i: (i, 0, 0)),
                   pl.BlockSpec((None, KVW, nc), lambda i: (i, 0, 0))],
        scratch_shapes=[pltpu.VMEM((nc + 8, KVW), F32)],
        compiler_params=_cparams(("arbitrary",)),
        name="compress",
    )(kc, vc, wk, wv)


def _topk_mask(vals, blk_f, n_top, axis=1):
    sel = jnp.zeros(vals.shape, F32)
    big = float(vals.shape[axis])
    for _ in range(n_top):
        mx = jnp.max(vals, axis=axis, keepdims=True)
        first = jnp.min(jnp.where(vals == mx, blk_f, big), axis=axis, keepdims=True)
        hit = blk_f == first
        sel = jnp.where(hit, 1.0, sel)
        vals = jnp.where(hit, -jnp.inf, vals)
    return sel


def _topk_mask_by_rank(vals, blk, n_valid, n_top):
    rank = jnp.zeros(vals.shape, F32)
    for j in range(n_valid):
        vj = vals[:, j:j + 1]
        beats = (vj > vals) | ((vj == vals) & (blk > j))
        rank = rank + jnp.where(beats, 1.0, 0.0)
    return jnp.where(rank < float(n_top), 1.0, 0.0)


def _pos_features(pos):
    hi = (pos // SEL_BLOCK).astype(F32)[:, None]
    lo = (pos % SEL_BLOCK).astype(F32)[:, None]
    return jnp.concatenate([hi, lo, jnp.zeros((pos.shape[0], LANES - 2), F32)], axis=1).astype(BF16)


def _importance_matrix(nc, nsel):
    j = jnp.arange(nc)[:, None]
    s = jnp.arange(nsel)[None, :]
    r = SEL_BLOCK // CMP_STRIDE
    a = (j >= r * s) & (j <= r * s + r - 1)
    b = (j + 1 >= r * s) & (j + 1 <= r * s + r - 1)
    return a.astype(F32) + b.astype(F32)


def _nsa_prompt_kernel(qt_ref, gst_ref, kc_ref, vct_ref, ks_ref, vst_ref, kw_ref, vwt_ref,
                       cfeat_ref, wfeat_ref, qfeat_ref, slope_ref,
                       y_ref, qk_scr, m_scr, l_scr, acc_scr, o_scr, sel_scr, imp_scr, flag_scr, ids_scr,
                       *, seq, tk, wl):
    i = pl.program_id(1)
    q0 = i * Q_BLOCK
    nq = Q_BLOCK
    gq = GROUP * nq
    nc = kc_ref.shape[0]
    nsel = seq // SEL_BLOCK
    n_top = min(TOP_BLOCKS, nsel)
    blk_per_tile = tk // SEL_BLOCK
    qpos = q0 + lax.broadcasted_iota(I32, (1, nq), 1)
    gst = gst_ref[...]

    crow = lax.broadcasted_iota(I32, (nc, nq), 0)
    cend = crow * CMP_STRIDE + (CMP_BLOCK - 1)
    mask_c = qpos >= cend
    kc = jnp.concatenate([kc_ref[...], cfeat_ref[...]], axis=1)
    vct = vct_ref[...]
    blk = lax.broadcasted_iota(I32, (nsel, nq), 0)
    blk_f = blk.astype(F32)
    cur = qpos // SEL_BLOCK
    forced = (blk == 0) | (blk == cur) | (blk == cur - 1)
    visible = blk * SEL_BLOCK <= qpos
    ws = pl.multiple_of(jnp.maximum(q0 - WINDOW, 0), Q_BLOCK)
    wpos = ws + lax.broadcasted_iota(I32, (wl, nq), 0)
    valid_w = lax.bitcast_convert_type(qpos - wpos, jnp.uint32) < WINDOW
    n_tiles = (q0 + nq + tk - 1) // tk
    half_rows = lax.broadcasted_iota(I32, (KVW, nq), 0) // HEAD_DIM
    tile_pos = lax.broadcasted_iota(I32, (SEL_BLOCK, nq), 0)

    def lanes4(x):
        return jnp.concatenate([x] * GROUP, axis=1)

    def gate_row(branch, k):
        r0 = branch * N_HEADS + k * GROUP
        return jnp.concatenate([gst[r0 + g:r0 + g + 1] for g in range(GROUP)], axis=1)

    mask_c4 = lanes4(mask_c)
    valid_w4 = lanes4(valid_w)
    kwt = jnp.concatenate([kw_ref[pl.ds(ws, wl), :], wfeat_ref[...]], axis=1)
    vwtt = vwt_ref[:, pl.ds(ws, wl)]

    for k in range(N_KV):
        for g in range(GROUP):
            qk_scr[k, 0:LANES, g * nq:(g + 1) * nq] = qt_ref[k * GROUP + g]
        qk_scr[k, LANES:2 * LANES, :] = qfeat_ref[k]
        qk = qk_scr[k]

        s = jnp.where(mask_c4, _dot(kc, qk), NEG)
        e = jnp.where(mask_c4, jnp.exp(s - jnp.max(s, axis=0, keepdims=True)), 0.0)
        l = jnp.sum(e, axis=0, keepdims=True)
        p = e * jnp.where(l > 0.0, 1.0 / l, 0.0)
        o_c = _dot(vct, p.astype(BF16))
        psum = p[:, 0:nq]
        for g in range(1, GROUP):
            psum = psum + p[:, g * nq:(g + 1) * nq]

        a = psum + jnp.where(crow == 0, 0.0, pltpu.roll(psum, 1, 0))
        a = a + pltpu.roll(a, nc - 1, 0)
        imp_scr[...] = a + pltpu.roll(a, nc - 2, 0)
        imp = imp_scr[pl.ds(0, nsel, stride=nc // nsel), :]
        vals = jnp.where(visible, jnp.where(forced, FORCE_SCORE, imp), NEG)
        sel = jnp.where(visible, _topk_mask(vals, blk_f, n_top, axis=0), 0.0)
        sel_scr[k] = jnp.where(sel > 0.5, 0.0, NEG)
        blk_any = jnp.max(sel, axis=1, keepdims=True)
        for t in range(seq // tk):
            hit = (jnp.max(blk_any[t * blk_per_tile:(t + 1) * blk_per_tile, :]) > 0.5).astype(I32)
            flag_scr[t] = hit if k == 0 else flag_scr[t] | hit

        s = jnp.where(valid_w4, _dot(kwt, qk), NEG)
        e = jnp.exp(s - jnp.max(s, axis=0, keepdims=True))
        p = e / jnp.sum(e, axis=0, keepdims=True)
        o_w = _dot(vwtt, p.astype(BF16))
        o_scr[k] = gate_row(0, k) * o_c + gate_row(2, k) * o_w

    m_scr[...] = jnp.full(m_scr.shape, NEG, F32)
    l_scr[...] = jnp.zeros(l_scr.shape, F32)
    acc_scr[...] = jnp.zeros(acc_scr.shape, F32)

    n_act = jnp.int32(0)
    for t in range(seq // tk):
        ids_scr[n_act] = t
        n_act = n_act + jnp.where((flag_scr[t] > 0) & (t < n_tiles), 1, 0)
    lane0 = lax.broadcasted_iota(I32, (tk, LANES), 1) == 0
    feat = wfeat_ref[0:tk, :]

    def sel_tiles(tiles):
        ta = tiles[0]
        starts = [pl.multiple_of(t * tk, tk) for t in tiles]
        keys = [jnp.concatenate([ks_ref[pl.ds(starts[0], tk), :], feat], axis=1)]
        for t, k0 in zip(tiles[1:], starts[1:]):
            shifted = jnp.where(lane0, feat.astype(F32) + ((t - ta) * blk_per_tile).astype(F32), feat.astype(F32))
            keys.append(jnp.concatenate([ks_ref[pl.ds(k0, tk), :], shifted.astype(BF16)], axis=1))
        kt = jnp.concatenate(keys, axis=0)
        vtt = jnp.concatenate([vst_ref[:, pl.ds(k0, tk)] for k0 in starts], axis=1)
        base = (starts[0] - q0).astype(F32)
        scores = _dot(kt, jnp.concatenate([qk_scr[k] for k in range(N_KV)], axis=1))
        probs, alphas = [], []
        for k in range(N_KV):
            neg = []
            for t, k0 in zip(tiles, starts):
                for j in range(blk_per_tile):
                    row = sel_scr[k, pl.ds(t * blk_per_tile + j, 1), :]
                    causal = qpos >= k0 + j * SEL_BLOCK + tile_pos
                    neg.append(jnp.where(causal, jnp.broadcast_to(row, (SEL_BLOCK, nq)), NEG))
            neg = lanes4(jnp.concatenate(neg, axis=0))
            off = slope_ref[k] * base
            s = scores[:, k * gq:(k + 1) * gq] + neg
            m_old = m_scr[k]
            m_new = jnp.maximum(m_old, jnp.max(s, axis=0, keepdims=True) + off)
            alpha = jnp.exp(m_old - m_new)
            p = jnp.exp(s - (m_new - off))
            l_scr[k] = alpha * l_scr[k] + jnp.sum(p, axis=0, keepdims=True)
            m_scr[k] = m_new
            probs.append(p.astype(BF16))
            alphas.append(alpha)
        pv = _dot(vtt, jnp.concatenate(probs, axis=1))
        for k in range(N_KV):
            acc_scr[k] = acc_scr[k] * alphas[k] + pv[:, k * gq:(k + 1) * gq]

    def sel_octet(i, carry):
        sel_tiles([ids_scr[8 * i + j] for j in range(8)])
        return carry

    done = (n_act // 8) * 8
    lax.fori_loop(0, n_act // 8, sel_octet, 0)
    pl.when(n_act % 8 >= 4)(lambda: sel_tiles([ids_scr[done + j] for j in range(4)]))
    pair0 = (n_act // 4) * 4
    pl.when(n_act % 4 >= 2)(lambda: sel_tiles([ids_scr[pair0], ids_scr[pair0 + 1]]))
    pl.when(n_act % 2 == 1)(lambda: sel_tiles([ids_scr[n_act - 1]]))

    for k in range(N_KV):
        o = o_scr[k] + gate_row(1, k) * (acc_scr[k] / l_scr[k])
        for g in range(GROUP):
            h = k * GROUP + g
            oh = jnp.where(half_rows == k, o[:, g * nq:(g + 1) * nq], 0.0)
            y_ref[:, h * LANES:(h + 1) * LANES] = oh.T.astype(BF16)


def _nsa_prompt(qt, gst, kcmp, vcmpt, ksb, vst, kwb, vwt):
    b, _, _, s = qt.shape
    nq = Q_BLOCK
    gq = GROUP * nq
    nc = kcmp.shape[1]
    nsel = s // SEL_BLOCK
    tk = 128
    wl = WINDOW + Q_BLOCK
    assert s % tk == 0 and s >= wl
    assert s // SEL_BLOCK <= 2 * LANES, "position // 64 must stay exact in bf16"
    cfeat = _pos_features(jnp.arange(nc) * CMP_STRIDE + (CMP_BLOCK - 1))
    wfeat = _pos_features(jnp.arange(wl))
    slope_rows = jnp.repeat(jnp.asarray(SLOPES, F32).reshape(N_KV, 1, GROUP), nq, axis=2)
    qfeat = jnp.concatenate([slope_rows * SEL_BLOCK, slope_rows, jnp.zeros((N_KV, LANES - 2, gq), F32)],
                            axis=1).astype(BF16)
    rows = lambda r: pl.BlockSpec((None, r, KVW), lambda bi, i: (bi, 0, 0))
    cols = lambda c: pl.BlockSpec((None, KVW, c), lambda bi, i: (bi, 0, 0))
    const = lambda a: pl.BlockSpec(a.shape, lambda bi, i: (0,) * a.ndim)
    return pl.pallas_call(
        functools.partial(_nsa_prompt_kernel, seq=s, tk=tk, wl=wl),
        out_shape=jax.ShapeDtypeStruct((b, s, QPAD), BF16),
        grid=(b, s // nq),
        in_specs=[pl.BlockSpec((None, N_HEADS, LANES, nq), lambda bi, i: (bi, 0, 0, i)),
                  pl.BlockSpec((None, LANES, nq), lambda bi, i: (bi, 0, i)),
                  rows(nc), cols(nc), rows(s), cols(s), rows(s), cols(s),
                  const(cfeat), const(wfeat), const(qfeat), const(slope_rows)],
        out_specs=pl.BlockSpec((None, nq, QPAD), lambda bi, i: (bi, i, 0)),
        scratch_shapes=[pltpu.VMEM((N_KV, 2 * LANES, gq), BF16),
                        pltpu.VMEM((N_KV, 1, gq), F32),
                        pltpu.VMEM((N_KV, 1, gq), F32),
                        pltpu.VMEM((N_KV, KVW, gq), F32),
                        pltpu.VMEM((N_KV, KVW, gq), F32),
                        pltpu.VMEM((N_KV, nsel, nq), F32),
                        pltpu.VMEM((nc, nq), F32),
                        pltpu.SMEM((s // tk,), I32),
                        pltpu.SMEM((s // tk,), I32)],
        compiler_params=_cparams(("arbitrary", "arbitrary")),
        name="nsa_prompt",
    )(qt, gst, kcmp, vcmpt, ksb, vst, kwb, vwt, cfeat, wfeat, qfeat, slope_rows)


def _nsa_sample_kernel(pt_ref, q_ref, gate_ref, kcn_ref, vcn_ref, ksn_ref, vsn_ref, kwn_ref, vwn_ref,
                       vpn_ref, skw_ref, svw_ref, spool_ref, wk_ref, wv_ref, imat_ref, emat_ref,
                       ckc_ref, cvc_ref, cks_ref, cvs_ref,
                       o_ref, pooled_ref, kwo_ref, vwo_ref, buf, buft, win_scr, tail_scr, vext_scr, sem,
                       *, sb, ns, past, n_pages, page, ncv, ncp, nks, wls, nselp, n_sel):
    step = pl.program_id(0)
    nrow = ns * N_HEADS
    par = step % 2

    def copies(n, side, r):
        out = []
        for p in range(n_pages):
            pg = pt_ref[n * n_pages + p]
            for c, cref in enumerate((ckc_ref, cvc_ref, cks_ref, cvs_ref)):
                out.append(pltpu.make_async_copy(cref.at[pg], buft.at[side, r, c, :, pl.ds(p * page, page)],
                                                 sem.at[side, r]))
        return out

    @pl.when(step == 0)
    def _():
        buf[:, :, past:, :] = jnp.zeros((sb, 2, buf.shape[2] - past, KVW), F32)
        tail_scr[...] = jnp.zeros_like(tail_scr)
        vext_scr[...] = jnp.zeros_like(vext_scr)
        for r in range(sb):
            for cp in copies(r, 0, r):
                cp.start()

    @pl.when(step + 1 < pl.num_programs(0))
    def _():
        for r in range(sb):
            for cp in copies((step + 1) * sb + r, 1 - par, r):
                cp.start()

    for r in range(sb):
        for cp in copies(step * sb + r, par, r):
            cp.wait()

    def new_rows_t(ref, r4, r):
        tail_scr[r, 0:ns, :] = ref[pl.ds(r4, ns), :]
        return tail_scr[r].T

    row = lax.broadcasted_iota(I32, (nrow, 1), 0)
    hrow = row % N_HEADS
    qpos = past + row // N_HEADS
    slope = jnp.exp2(-8.0 * (hrow.astype(F32) + 1.0) / N_HEADS)
    kvrow = hrow // GROUP
    lane = lax.broadcasted_iota(I32, (1, LANES), 1)
    half = (lane // HEAD_DIM) == kvrow
    grow = (row // N_HEADS) * N_KV + kvrow
    row8 = lax.broadcasted_iota(I32, (ns * N_KV, 1), 0)
    qpos8 = past + lax.broadcasted_iota(I32, (ns * N_KV, 1), 0) // N_KV
    blk = lax.broadcasted_iota(I32, (1, nselp), 1)
    blk_f = blk.astype(F32)
    cur = qpos8 // SEL_BLOCK
    forced = (blk == 0) | (blk == cur) | (blk == cur - 1)
    visible = (blk * SEL_BLOCK <= qpos8)
    inrange = blk < n_sel
    cend = lax.broadcasted_iota(I32, (1, ncp), 1) * CMP_STRIDE + (CMP_BLOCK - 1)
    mask_c = qpos >= cend
    bias_c = slope * (cend - qpos).astype(F32)
    kpos = lax.broadcasted_iota(I32, (1, nks), 1)
    causal_s = qpos >= kpos
    bias_s = slope * (kpos - qpos).astype(F32)
    wbuf = wls[0]
    wpos = past - wbuf + lax.broadcasted_iota(I32, (1, wls[1]), 1)
    dw = qpos - wpos
    valid_w = lax.bitcast_convert_type(dw, jnp.uint32) < WINDOW
    bias_w = slope * (wpos - qpos).astype(F32)
    prow = lax.broadcasted_iota(I32, (vext_scr.shape[1], 1), 0)
    wcol = 2 << (lax.broadcasted_iota(I32, (1, POOL_W), 1) // POOL_GW)
    n_top = min(TOP_BLOCKS, n_sel)

    def softmax_rows(s, mask):
        s = jnp.where(mask, s, NEG)
        mx = jnp.max(s, axis=1, keepdims=True)
        e = jnp.where(mask, jnp.exp(s - mx), 0.0)
        l = jnp.sum(e, axis=1, keepdims=True)
        return e * jnp.where(l > 0.0, 1.0 / l, 0.0)

    def seq_body(r):
        r4 = r * ns
        for c, new_ref in enumerate((kcn_ref, vcn_ref)):
            for p in range(n_pages):
                buf[r, c, p * page:(p + 1) * page, :] = buft[par, r, c, :, p * page:(p + 1) * page].T
            buf[r, c, past:past + ns, :] = new_ref[pl.ds(r4, ns), :]
        for c, new_ref in ((2, ksn_ref), (3, vsn_ref)):
            buft[par, r, c, :, past:past + LANES] = new_rows_t(new_ref, r4, r)

        qall = q_ref[pl.ds(r * nrow, nrow), :]
        gates = gate_ref[pl.ds(r * nrow, nrow), :]

        cmp = []
        for c, w_ref in ((0, wk_ref), (1, wv_ref)):
            span = CMP_STRIDE * ncv
            lo = buf[r, c, 0:span, :].reshape(ncv, CMP_STRIDE, KVW) * w_ref[0:CMP_STRIDE, :][None]
            hi = (buf[r, c, CMP_STRIDE:CMP_STRIDE + span, :].reshape(ncv, CMP_STRIDE, KVW)
                  * w_ref[CMP_STRIDE:CMP_BLOCK, :][None])
            acc = jnp.sum(lo + hi, axis=1)
            cmp.append(jnp.concatenate([acc, jnp.zeros((ncp - ncv, KVW), F32)], axis=0).astype(BF16))
        p_c = softmax_rows(_dot_nt(qall, cmp[0]) + bias_c, mask_c)
        o_c = _dot(p_c.astype(BF16), cmp[1])

        psum = jnp.zeros((ns * N_KV, ncp), F32)
        for i in range(ns * N_KV):
            r0 = (i // N_KV) * N_HEADS + (i % N_KV) * GROUP
            psum = jnp.where(row8 == i, jnp.sum(p_c[r0:r0 + GROUP], axis=0, keepdims=True), psum)
        imp = _dot_exact(psum, imat_ref[...])
        vals = jnp.where(inrange, jnp.where(visible, jnp.where(forced, FORCE_SCORE, imp), NEG), -jnp.inf)
        sel8 = _topk_mask_by_rank(vals, blk, n_sel, n_top)
        sel_rows = jnp.zeros((nrow, nselp), F32)
        for i in range(ns * N_KV):
            sel_rows = jnp.where(grow == i, sel8[i:i + 1], sel_rows)
        chosen = _dot(sel_rows.astype(BF16), emat_ref[...])

        kst = buft[par, r, 2].astype(BF16)
        vst = buft[par, r, 3].astype(BF16)
        p_s = softmax_rows(_dot(qall, kst) + bias_s, causal_s & (chosen > 0.5))
        o_s = _dot_nt(p_s.astype(BF16), vst)

        outs_w = []
        for state_ref, new_ref, next_ref in ((skw_ref, kwn_ref, kwo_ref), (svw_ref, vwn_ref, vwo_ref)):
            win_scr[r, :, 0:wbuf] = state_ref[r]
            win_scr[r, :, wbuf:wbuf + LANES] = new_rows_t(new_ref, r4, r)
            outs_w.append(win_scr[r].astype(BF16))
            next_ref[r] = win_scr[r, :, ns:ns + wbuf]
        p_w = softmax_rows(_dot(qall, outs_w[0]) + bias_w, valid_w)
        o_w = _dot_nt(p_w.astype(BF16), outs_w[1])

        o = gates[:, 0:1] * o_c + gates[:, 1:2] * o_s + gates[:, 2:3] * o_w
        o_ref[pl.ds(r * nrow, nrow), :] = jnp.where(half, o, 0.0).astype(BF16)

        vext_scr[r, 0:POOL_BUF, :] = spool_ref[r]
        vext_scr[r, POOL_BUF:POOL_BUF + ns, :] = vpn_ref[pl.ds(r4, ns), :]
        ext = vext_scr[r]
        for t in range(ns):
            hi = POOL_BUF + t
            inwin = (prow <= hi) & (prow > hi - wcol)
            ssum = jnp.sum(jnp.where(inwin, ext, 0.0), axis=0, keepdims=True)
            cnt = jnp.minimum(past + t + 1, wcol).astype(F32)
            pooled_ref[pl.ds(r4 + t, 1), :] = ssum / cnt - ext[hi:hi + 1, :]

    for r in range(sb):
        seq_body(r)


def _nsa_sample(page_table, q_rows, gate_rows, new6, vp_new, state_kwt, state_vwt, state_pool, wk, wv, caches):
    n_seq, n_pages = page_table.shape
    page = caches[0].shape[2]
    past = n_pages * page
    ns = vp_new.shape[0] // n_seq
    wbuf = state_kwt.shape[2]
    sb = 2
    nrow = ns * N_HEADS
    assert ns <= SEL_BLOCK and page == LANES
    t_pad = -(-(past + ns) // SEL_BLOCK) * SEL_BLOCK
    n_cmp = t_pad // CMP_STRIDE - 1
    ncv = -(-n_cmp // 8) * 8
    ncp = -(-ncv // LANES) * LANES
    nks = past + LANES
    n_sel = t_pad // SEL_BLOCK
    nselp = LANES
    assert n_sel <= nselp
    wlp = wbuf + LANES
    buf_rows = -(-(CMP_STRIDE * ncv + CMP_STRIDE) // 8) * 8
    imat = _importance_matrix(ncp, nselp)
    emat = (jnp.arange(nselp)[:, None] == (jnp.arange(nks)[None, :] // SEL_BLOCK)).astype(BF16)

    seqblk = lambda rows, w: pl.BlockSpec((sb * rows, w), lambda i, pt: (i, 0))
    const = lambda a: pl.BlockSpec(a.shape, lambda i, pt: (0,) * a.ndim)
    kern = functools.partial(
        _nsa_sample_kernel, sb=sb, ns=ns, past=past, n_pages=n_pages, page=page,
        ncv=ncv, ncp=ncp, nks=nks, wls=(wbuf, wlp), nselp=nselp, n_sel=n_sel)
    grid_spec = pltpu.PrefetchScalarGridSpec(
        num_scalar_prefetch=1,
        grid=(n_seq // sb,),
        in_specs=[seqblk(nrow, LANES), seqblk(nrow, LANES)] + [seqblk(ns, KVW)] * 6 + [seqblk(ns, POOL_W)]
        + [pl.BlockSpec((sb, KVW, wbuf), lambda i, pt: (i, 0, 0))] * 2
        + [pl.BlockSpec((sb, POOL_BUF, POOL_W), lambda i, pt: (i, 0, 0))]
        + [const(wk), const(wv), const(imat), const(emat)]
        + [pl.BlockSpec(memory_space=pl.ANY)] * 4,
        out_specs=[seqblk(nrow, LANES), seqblk(ns, POOL_W)]
        + [pl.BlockSpec((sb, KVW, wbuf), lambda i, pt: (i, 0, 0))] * 2,
        scratch_shapes=[pltpu.VMEM((sb, 2, buf_rows, KVW), F32),
                        pltpu.VMEM((2, sb, 4, KVW, nks), F32),
                        pltpu.VMEM((sb, KVW, wlp), F32),
                        pltpu.VMEM((sb, LANES, KVW), F32),
                        pltpu.VMEM((sb, 24, POOL_W), F32),
                        pltpu.SemaphoreType.DMA((2, sb))],
    )
    return pl.pallas_call(
        kern,
        out_shape=[jax.ShapeDtypeStruct((n_seq * nrow, LANES), BF16),
                   jax.ShapeDtypeStruct((n_seq * ns, POOL_W), F32)]
        + [jax.ShapeDtypeStruct((n_seq, KVW, wbuf), F32)] * 2,
        grid_spec=grid_spec,
        compiler_params=_cparams(("arbitrary",)),
        name="nsa_sample",
    )(page_table.reshape(-1), q_rows, gate_rows, *new6, vp_new, state_kwt, state_vwt, state_pool, wk, wv,
      imat, emat, *caches)


def _route(logits_t, bias_col, tm):
    sc = jax.nn.sigmoid(logits_t)
    biased = sc + bias_col
    epg = EXPERTS_PER_GROUP
    row8 = lax.broadcasted_iota(I32, (epg, tm), 0).astype(F32)
    ninf = -jnp.inf
    grp = jnp.zeros((N_EGROUPS, tm), F32)
    for g in range(N_EGROUPS):
        bg = biased[g * epg:(g + 1) * epg]
        m1 = jnp.max(bg, axis=0, keepdims=True)
        first = jnp.min(jnp.where(bg == m1, row8, float(epg)), axis=0, keepdims=True)
        m2 = jnp.max(jnp.where(row8 == first, ninf, bg), axis=0, keepdims=True)
        grp = jnp.where(row8 == float(g), m1 + m2, grp)
    keep = jnp.zeros((N_EGROUPS, tm), F32)
    vals = grp
    for _ in range(TOPK_GROUPS):
        mx = jnp.max(vals, axis=0, keepdims=True)
        first = jnp.min(jnp.where(vals == mx, row8, float(N_EGROUPS)), axis=0, keepdims=True)
        hit = row8 == first
        keep = jnp.where(hit, 1.0, keep)
        vals = jnp.where(hit, ninf, vals)
    masked = jnp.concatenate(
        [jnp.where(keep[g:g + 1] > 0.5, biased[g * epg:(g + 1) * epg], NEG) for g in range(N_EGROUPS)], axis=0)
    rowe = lax.broadcasted_iota(I32, (N_EXPERTS, tm), 0).astype(F32)
    chosen = jnp.zeros((N_EXPERTS, tm), F32)
    vals = masked
    picks = []
    for _ in range(TOP_K):
        mx = jnp.max(vals, axis=0, keepdims=True)
        first = jnp.min(jnp.where(vals == mx, rowe, float(N_EXPERTS)), axis=0, keepdims=True)
        hit = rowe == first
        chosen = jnp.where(hit, sc, chosen)
        vals = jnp.where(hit, ninf, vals)
        picks.append((hit, first))
    return ROUTED_SCALE * chosen / jnp.sum(chosen, axis=0, keepdims=True), picks


def _pack_bf16_pairs(x):
    c = x.shape[1] // 2
    bits = lambda v: lax.bitcast_convert_type(v.astype(BF16).astype(F32), jnp.uint32)
    return (bits(x[:, :c]) >> 16) | (bits(x[:, c:]) & jnp.uint32(0xFFFF0000))


def _unpack_bf16_pairs(w):
    lo = lax.bitcast_convert_type(w << 16, F32)
    hi = lax.bitcast_convert_type(w & jnp.uint32(0xFFFF0000), F32)
    return jnp.concatenate([lo, hi], axis=1)


def _finish_kernel(x_ref, pooled_ref, y_ref, gm_ref, g1_ref, shift_ref, scale_ref,
                   wlin_ref, pscale_ref, wpo_ref, wno_ref, wo_ref, n2_ref, wr_ref, br_ref,
                   *rest, tm, d, sparse):
    if sparse:
        tri_ref, x1_ref, u2_ref, up_ref, eid_ref, gk_ref, rank_ref, cnt_ref, carry_scr = rest
    else:
        x1_ref, u2_ref, gates_ref = rest
    x = x_ref[...].reshape(tm, d)
    pooled = pooled_ref[...].reshape(tm, POOL_W).astype(BF16)
    y_pool = _dot(pooled, wlin_ref[...]) * pscale_ref[...]
    a = _dot(y_pool.astype(BF16), wpo_ref[...])
    b = _dot(y_ref[...].reshape(tm, QPAD), wno_ref[...])
    gm = gm_ref[...].reshape(tm, 2 * d)
    merged = gm[:, :d] * a + gm[:, d:] * b
    x1 = x + _rows2d(g1_ref) * _dot(merged.astype(BF16), wo_ref[...])
    x1_ref[...] = x1.reshape(x1_ref.shape)
    u2 = _rmsnorm(x1, n2_ref[...]) * (1.0 + _rows2d(scale_ref)) + _rows2d(shift_ref)
    u2b = u2.astype(BF16)
    u2_ref[...] = u2b.reshape(u2_ref.shape)
    logits_t = _dot_nt(wr_ref[...], u2b)
    gates_t, picks = _route(logits_t[:N_EXPERTS], br_ref[...], tm)
    if not sparse:
        gates_t = jnp.concatenate([gates_t, jnp.zeros((LANES - N_EXPERTS, tm), F32)], axis=0)
        gates_ref[...] = gates_t.T.reshape(gates_ref.shape)
        return

    @pl.when((pl.program_id(0) == 0) & (pl.program_id(1) == 0))
    def _():
        carry_scr[...] = jnp.zeros_like(carry_scr)

    packed = _pack_bf16_pairs(u2)
    for s in range(up_ref.shape[0]):
        up_ref[s] = packed[:, s * SC_ROW_WORDS:(s + 1) * SC_ROW_WORDS]
    hit_all = picks[0][0]
    for hit, _ in picks[1:]:
        hit_all = hit_all | hit
    hits = jnp.where(hit_all, 1.0, 0.0).astype(BF16)
    before = _dot(hits, tri_ref[...]) + jnp.concatenate([carry_scr[...]] * (tm // LANES), axis=1)
    eids, gks, ranks = [], [], []
    for hit, first in picks:
        eids.append(first)
        gks.append(jnp.sum(jnp.where(hit, gates_t, 0.0), axis=0, keepdims=True))
        ranks.append(jnp.sum(jnp.where(hit, before, 0.0), axis=0, keepdims=True))
    pick_row = lax.broadcasted_iota(I32, (TOP_K, tm), 0)

    def stack(rows):
        out = jnp.zeros((TOP_K, tm), F32)
        for r, v in enumerate(rows):
            out = jnp.where(pick_row == r, v, out)
        return out

    eid_ref[...] = stack(eids).astype(I32).reshape(eid_ref.shape)
    gk_ref[...] = jnp.concatenate([stack(gks), jnp.zeros((LANES - TOP_K, tm), F32)], axis=0).T.reshape(gk_ref.shape)
    rank_ref[...] = stack(ranks).astype(I32).reshape(rank_ref.shape)
    carry_scr[...] += _dot(hits, jnp.ones((tm, LANES), BF16))
    cnt_ref[...] = carry_scr[...]


def _finish(x3, pooled, ynsa, gm, mods, wts, *, tm, sparse):
    g, r, d = x3.shape
    nt = r // tm
    g1, shift2, scale2 = mods
    per_row = g1.ndim == 2
    tok = lambda w: pl.BlockSpec((1, tm, w), lambda b, j: (b, j, 0))
    tok_t = lambda rows: pl.BlockSpec((1, rows, tm), lambda b, j: (b, 0, j))
    if per_row:
        mod_spec = lambda col: pl.BlockSpec((tm, d), lambda b, j, col=col: (b * nt + j, col))
    else:
        mod_spec = lambda col: pl.BlockSpec((1, 1, d), lambda b, j, col=col: (b, 0, col))
    const = lambda a: pl.BlockSpec(a.shape, lambda b, j: (0,) * a.ndim)
    out_shape = [jax.ShapeDtypeStruct((g, r, d), F32), jax.ShapeDtypeStruct((g, r, d), BF16)]
    out_specs = [tok(d), tok(d)]
    scratch = []
    if sparse:
        tri = (jnp.arange(tm)[:, None] < jnp.arange(tm)[None, :]).astype(BF16)
        wts = tuple(wts) + (tri,)
        split = d // 2 // SC_ROW_WORDS
        out_shape += [jax.ShapeDtypeStruct((split, g * r, SC_ROW_WORDS), jnp.uint32),
                      jax.ShapeDtypeStruct((g, TOP_K, r), I32), jax.ShapeDtypeStruct((g, r, LANES), F32),
                      jax.ShapeDtypeStruct((g, TOP_K, r), I32), jax.ShapeDtypeStruct((N_EXPERTS, LANES), F32)]
        out_specs += [pl.BlockSpec((split, tm, SC_ROW_WORDS), lambda b, j: (0, b * nt + j, 0)),
                      tok_t(TOP_K), tok(LANES), tok_t(TOP_K),
                      pl.BlockSpec((N_EXPERTS, LANES), lambda b, j: (0, 0))]
        scratch.append(pltpu.VMEM((N_EXPERTS, LANES), F32))
    else:
        out_shape.append(jax.ShapeDtypeStruct((g, r, LANES), F32))
        out_specs.append(tok(LANES))
    return pl.pallas_call(
        functools.partial(_finish_kernel, tm=tm, d=d, sparse=sparse),
        out_shape=out_shape,
        grid=(g, nt),
        in_specs=[tok(d), tok(POOL_W), tok(QPAD), tok(2 * d), mod_spec(2), mod_spec(3), mod_spec(4)]
        + [const(w) for w in wts],
        out_specs=out_specs,
        scratch_shapes=scratch,
        compiler_params=_cparams(("arbitrary", "arbitrary")),
        name="finish_route" if sparse else "finish",
    )(x3, pooled, ynsa, gm, g1, shift2, scale2, *wts)


def _moe_kernel(u_ref, gates_ref, x1_ref, g2_ref, nf_ref, wg_ref, wu_ref, wd_ref, sg_ref, su_ref, sd_ref,
                y_ref, acc_ref, *, tm, d, eps):
    e = pl.program_id(2)
    u = u_ref[...].reshape(tm, d)

    @pl.when(e == 0)
    def _():
        hs = _silu(_dot(u, sg_ref[...])) * _dot(u, su_ref[...])
        acc_ref[...] = _dot(hs.astype(BF16), sd_ref[...])

    gates = gates_ref[...].reshape(tm, LANES)
    lane = lax.broadcasted_iota(I32, (1, LANES), 1)
    hidden = []
    for j in range(eps):
        h = _silu(_dot(u, wg_ref[j].astype(BF16))) * _dot(u, wu_ref[j].astype(BF16))
        gate = jnp.sum(jnp.where(lane == e * eps + j, gates, 0.0), axis=1, keepdims=True)
        hidden.append((h * gate).astype(BF16))
    f = wd_ref.shape[1]
    acc_ref[...] += _dot(jnp.concatenate(hidden, axis=1), wd_ref[...].reshape(eps * f, d).astype(BF16))

    @pl.when(e == pl.num_programs(2) - 1)
    def _():
        x2 = x1_ref[...].reshape(tm, d) + _rows2d(g2_ref) * acc_ref[...]
        y_ref[...] = _rmsnorm(x2, nf_ref[...]).reshape(y_ref.shape)


def _moe(u2, gates, x1, g2, normf, w_gate, w_up, w_down, sg, su, sd, *, tm):
    g, r, d = x1.shape
    nt = r // tm
    ne, _, f = w_gate.shape
    per_row = g2.ndim == 2
    tok = lambda w: pl.BlockSpec((1, tm, w), lambda b, j, e: (b, j, 0))
    if per_row:
        g2_spec = pl.BlockSpec((tm, d), lambda b, j, e: (b * nt + j, 5))
    else:
        g2_spec = pl.BlockSpec((1, 1, d), lambda b, j, e: (b, 0, 5))
    once = pl.Buffered(buffer_count=1)
    const = lambda a: pl.BlockSpec(a.shape, lambda b, j, e: (0,) * a.ndim, pipeline_mode=once)
    eps = 4
    return pl.pallas_call(
        functools.partial(_moe_kernel, tm=tm, d=d, eps=eps),
        out_shape=jax.ShapeDtypeStruct((g, r, d), F32),
        grid=(g, nt, ne // eps),
        in_specs=[tok(d), tok(LANES),
                  pl.BlockSpec((1, tm, d), lambda b, j, e: (b, j, 0), pipeline_mode=once),
                  g2_spec, const(normf),
                  pl.BlockSpec((eps, d, f), lambda b, j, e: (e, 0, 0)),
                  pl.BlockSpec((eps, d, f), lambda b, j, e: (e, 0, 0)),
                  pl.BlockSpec((eps, f, d), lambda b, j, e: (e, 0, 0)),
                  const(sg), const(su), const(sd)],
        out_specs=tok(d),
        scratch_shapes=[pltpu.VMEM((tm, d), F32)],
        compiler_params=_cparams(("arbitrary", "arbitrary", "arbitrary")),
        name="moe",
    )(u2, gates, x1, g2, normf, w_gate, w_up, w_down, sg, su, sd)


SC_WINDOW = 128
SC_ROW_WORDS = 256
MOE_ROWS = 256


def _sc_mesh():
    return plsc.VectorSubcoreMesh(core_axis_name="c", subcore_axis_name="s")


def _sc_scatter_rows(src, dst_idx, n_dst):
    n, w = src.shape
    nk = dst_idx.shape[0]

    @pl.kernel(out_type=jax.ShapeDtypeStruct((n_dst, w), src.dtype), mesh=_sc_mesh(), scratch_types=[])
    def scatter(src_hbm, idx_hbm, dst_hbm):
        def body(rows_vmem, idx_vmem):
            pltpu.sync_copy(rows_vmem, dst_hbm.at[idx_vmem.at[0]])

        pltpu.emit_pipeline(
            body,
            grid=(nk, n // SC_WINDOW),
            in_specs=[pl.BlockSpec((SC_WINDOW, w), index_map=lambda k, i: (i, 0)),
                      pl.BlockSpec((1, SC_WINDOW), index_map=lambda k, i: (k, i))],
            out_specs=[],
            core_axis_name=("c", "s"),
            dimension_semantics=(pltpu.PARALLEL, pltpu.PARALLEL),
        )(src_hbm, idx_hbm)

    return scatter(src, dst_idx)


def _sc_gather_rows(src, idx):
    n, w = idx.shape[0], src.shape[1]

    @pl.kernel(out_type=jax.ShapeDtypeStruct((n, w), src.dtype), mesh=_sc_mesh(), scratch_types=[])
    def gather(src_hbm, idx_hbm, out_hbm):
        def body(idx_vmem, out_vmem):
            pltpu.sync_copy(src_hbm.at[idx_vmem.at[0]], out_vmem)

        pltpu.emit_pipeline(
            body,
            grid=(n // SC_WINDOW,),
            in_specs=[pl.BlockSpec((1, SC_WINDOW), index_map=lambda i: (0, i))],
            out_specs=[pl.BlockSpec((SC_WINDOW, w), index_map=lambda i: (i, 0))],
            core_axis_name=("c", "s"),
            dimension_semantics=(pltpu.PARALLEL,),
        )(idx_hbm, out_hbm)

    return gather(src, idx.reshape(1, n))


MOE_BLOCKS_PER_STEP = 4


def _expert_rows_kernel(te_ref, nt_ref, x_ref, *refs):
    y_ref = refs[-1]
    split = x_ref.shape[0]
    for j in range(MOE_BLOCKS_PER_STEP):
        wg_ref, wu_ref, wd_ref = refs[3 * j:3 * j + 3]
        rows = slice(j * MOE_ROWS, (j + 1) * MOE_ROWS)

        @pl.when(pl.program_id(0) * MOE_BLOCKS_PER_STEP + j < nt_ref[0])
        def _():
            x = _unpack_bf16_pairs(jnp.concatenate([x_ref[s, rows] for s in range(split)], axis=1)).astype(BF16)
            h = _silu(_dot(x, wg_ref[...].astype(BF16))) * _dot(x, wu_ref[...].astype(BF16))
            y = _pack_bf16_pairs(_dot(h.astype(BF16), wd_ref[...].astype(BF16)))
            for s in range(split):
                y_ref[s, rows] = y[:, s * SC_ROW_WORDS:(s + 1) * SC_ROW_WORDS]


def _expert_rows(tile_expert, n_tiles, x_sorted, w_gate, w_up, w_down):
    split, p, words = x_sorted.shape
    ne, d, f = w_gate.shape
    bps = MOE_BLOCKS_PER_STEP
    wspec = lambda a, b, j: pl.BlockSpec((None, a, b), lambda i, te, nt: (te[i * bps + j], 0, 0))
    rows = pl.BlockSpec((split, bps * MOE_ROWS, words), lambda i, te, nt: (0, i, 0))
    weights, wspecs = [], []
    for j in range(bps):
        weights += [w_gate, w_up, w_down]
        wspecs += [wspec(d, f, j), wspec(d, f, j), wspec(f, d, j)]
    grid_spec = pltpu.PrefetchScalarGridSpec(
        num_scalar_prefetch=2,
        grid=(p // (bps * MOE_ROWS),),
        in_specs=[rows] + wspecs,
        out_specs=rows,
    )
    return pl.pallas_call(
        _expert_rows_kernel,
        out_shape=jax.ShapeDtypeStruct((split, p, words), jnp.uint32),
        grid_spec=grid_spec,
        compiler_params=_cparams(("arbitrary",)),
        name="expert_rows",
    )(tile_expert, n_tiles, x_sorted, *weights)


def _combine_kernel(yg_ref, gk_ref, u_ref, x1_ref, g2_ref, nf_ref, sg_ref, su_ref, sd_ref, y_ref, *, tm, d):
    u = u_ref[...].reshape(tm, d)
    hs = _silu(_dot(u, sg_ref[...])) * _dot(u, su_ref[...])
    acc = _dot(hs.astype(BF16), sd_ref[...])
    gk = gk_ref[...].reshape(tm, LANES)
    lane = lax.broadcasted_iota(I32, (1, LANES), 1)
    split = yg_ref.shape[0]
    for k in range(TOP_K):
        gate = jnp.sum(jnp.where(lane == k, gk, 0.0), axis=1, keepdims=True)
        words = jnp.concatenate([yg_ref[s, k] for s in range(split)], axis=1)
        acc = acc + gate * _unpack_bf16_pairs(words)
    x2 = x1_ref[...].reshape(tm, d) + _rows2d(g2_ref) * acc
    y_ref[...] = _rmsnorm(x2, nf_ref[...]).reshape(y_ref.shape)


def _combine(yg, gk, u2, x1, g2, normf, sg, su, sd, *, tm):
    g, r, d = x1.shape
    nt = r // tm
    split, _, _, words = yg.shape
    tok = lambda w: pl.BlockSpec((1, tm, w), lambda b, j: (b, j, 0))
    const = lambda a: pl.BlockSpec(a.shape, lambda b, j: (0,) * a.ndim)
    return pl.pallas_call(
        functools.partial(_combine_kernel, tm=tm, d=d),
        out_shape=jax.ShapeDtypeStruct((g, r, d), F32),
        grid=(g, nt),
        in_specs=[pl.BlockSpec((split, TOP_K, tm, words), lambda b, j: (0, 0, b * nt + j, 0)),
                  tok(LANES), tok(d), tok(d),
                  pl.BlockSpec((1, 1, d), lambda b, j: (b, 0, 5)), const(normf), const(sg), const(su), const(sd)],
        out_specs=tok(d),
        compiler_params=_cparams(("arbitrary", "arbitrary")),
        name="moe_combine",
    )(yg, gk, u2, x1, g2, normf, sg, su, sd)


def _moe_sorted_experts(u2p, eid_t, rank_t, counts, w_gate, w_up, w_down):
    split, n, _ = u2p.shape
    ne = w_gate.shape[0]
    cnt = counts[:, 0].astype(I32)
    padded = -(-cnt // MOE_ROWS) * MOE_ROWS
    seg_end = jnp.cumsum(padded)
    seg_start = seg_end - padded
    p_rows = n * TOP_K + ne * MOE_ROWS
    eid = eid_t.transpose(1, 0, 2).reshape(TOP_K, n)
    start = jnp.sum(jnp.where(eid[:, :, None] == jnp.arange(ne, dtype=I32), seg_start, 0), axis=-1)
    pos = start + rank_t.transpose(1, 0, 2).reshape(TOP_K, n)
    first_row = jnp.arange(p_rows // MOE_ROWS, dtype=I32) * MOE_ROWS
    tile_expert = jnp.minimum(jnp.sum(seg_end[None, :] <= first_row[:, None], axis=1), ne - 1).astype(I32)
    n_tiles = (seg_end[-1:] // MOE_ROWS).astype(I32)
    scat_idx = jnp.concatenate([pos + s * p_rows for s in range(split)], axis=1)
    gath_idx = jnp.concatenate([pos.reshape(-1) + s * p_rows for s in range(split)])
    x_sorted = _sc_scatter_rows(u2p.reshape(split * n, SC_ROW_WORDS), scat_idx, split * p_rows)
    y_sorted = _expert_rows(tile_expert, n_tiles, x_sorted.reshape(split, p_rows, SC_ROW_WORDS),
                            w_gate, w_up, w_down)
    return y_sorted.reshape(split * p_rows, SC_ROW_WORDS), gath_idx


def _moe_sorted_combine(y_sorted, gath_idx, gk, u2, x1, g2, normf, sg, su, sd):
    g, r, _ = x1.shape
    split = gath_idx.shape[0] // (TOP_K * g * r)
    yg = _sc_gather_rows(y_sorted, gath_idx)
    return _combine(yg.reshape(split, TOP_K, g * r, SC_ROW_WORDS), gk, u2, x1, g2, normf, sg, su, sd, tm=512)


def _kv_slot_mask():
    return (jnp.arange(N_HEADS)[:, None] // GROUP == jnp.arange(N_KV)[None, :]).astype(F32)


def _prep_w_in(w_in, d):
    q0 = POOL_W
    kv0 = q0 + N_HEADS * HEAD_DIM
    gn0 = kv0 + 6 * KVW
    gm0 = gn0 + 3 * N_HEADS
    wq = w_in[:, q0:kv0].reshape(d, N_HEADS, 1, HEAD_DIM) * (HEAD_DIM ** -0.5)
    wq = (wq * _kv_slot_mask()[None, :, :, None]).reshape(d, QPAD)
    wgn = jnp.pad(w_in[:, gn0:gm0], ((0, 0), (0, LANES - 3 * N_HEADS)))
    return jnp.concatenate([w_in[:, :q0], wq, w_in[:, kv0:gn0], wgn, w_in[:, gm0:]], axis=1).astype(BF16)


def _prep_w_nsa_out(w, d):
    w = w.reshape(N_HEADS, 1, HEAD_DIM, d) * _kv_slot_mask()[:, :, None, None]
    return w.reshape(QPAD, d).astype(BF16)


def _block_diag(w_lin):
    g, c, _ = w_lin.shape
    eye = jnp.eye(g, dtype=F32)
    return (w_lin[:, :, None, :] * eye[:, None, :, None]).reshape(g * c, g * c).astype(BF16)


def kernel(x_prompt, x_sample, cache_kc, cache_vc, cache_ks, cache_vs, state_kw, state_vw, state_pool,
           page_table, c_prompt, c_sample, norm1_g, norm2_g, normf_g, w_ada, b_ada, w_in, w_pool_lin,
           pool_scale, w_cmp_k, w_cmp_v, w_pool_out, w_nsa_out, w_o, w_router, b_router, w_gate, w_up,
           w_down, ws_gate, ws_up, ws_down):
    depth = w_in.shape[0]
    assert depth == 1, "single-layer stack"
    bsz, seq, d = x_prompt.shape
    n_seq, ns, _ = x_sample.shape
    wbuf = state_kw.shape[2]
    lyr = 0

    n_tok = n_seq * ns
    c_all = jnp.concatenate([jnp.repeat(c_sample, ns, axis=0), c_prompt, jnp.zeros((-bsz % 8, d), F32)], axis=0)
    mod = _adaln(c_all, w_ada[lyr], b_ada[lyr])
    mod_p = mod[n_tok:n_tok + bsz].reshape(bsz, 1, 6 * d)
    mod_s = mod

    w2 = _prep_w_in(w_in[lyr], d)
    g1n = norm1_g[lyr].reshape(1, d)
    wk = w_cmp_k[lyr].reshape(CMP_BLOCK, KVW)
    wv = w_cmp_v[lyr].reshape(CMP_BLOCK, KVW)
    fin_w = (_block_diag(w_pool_lin[lyr]), pool_scale[lyr].reshape(1, POOL_W), w_pool_out[lyr].astype(BF16),
             _prep_w_nsa_out(w_nsa_out[lyr], d), w_o[lyr].astype(BF16), norm2_g[lyr].reshape(1, d),
             jnp.pad(w_router[lyr].T, ((0, LANES - N_EXPERTS), (0, 0))).astype(BF16),
             b_router[lyr].reshape(N_EXPERTS, 1))
    moe_w = (w_gate[lyr], w_up[lyr], w_down[lyr], ws_gate[lyr].astype(BF16), ws_up[lyr].astype(BF16),
             ws_down[lyr].astype(BF16))
    nf = normf_g.reshape(1, d)

    tm_p = 512
    (vp, kc, vc, ks, vs, kw, vw, gm, ksb, kwb, vst, vwt, qt, gst, pooled) = _in_proj(
        x_prompt, mod_p, mod_p, g1n, w2, tm=tm_p, prompt=True)
    kcmp, vcmpt = _compress(kc, vc, wk, wv)
    ynsa = _nsa_prompt(qt, gst, kcmp, vcmpt, ksb, vst, kwb, vwt)
    x1, u2, u2p, eid_t, gk_t, rank_t, counts = _finish(
        x_prompt, pooled, ynsa, gm, (mod_p, mod_p, mod_p), fin_w, tm=tm_p, sparse=True)

    xs3 = x_sample.reshape(1, n_tok, d)
    tm_s = 128
    (vp_s, kc_s, vc_s, ks_s, vs_s, kw_s, vw_s, gm_s, q_s, gs_s) = _in_proj(
        xs3, mod_s, mod_s, g1n, w2, tm=tm_s, prompt=False)
    two = lambda a: a.reshape(n_tok, a.shape[-1])
    q_rows = q_s.reshape(n_tok * N_HEADS, LANES)
    gate_rows = two(gs_s)[:, :3 * N_HEADS].reshape(n_tok, 3, N_HEADS).transpose(0, 2, 1)
    gate_rows = jnp.pad(gate_rows.reshape(n_tok * N_HEADS, 3), ((0, 0), (0, LANES - 3)))
    n_pool = cache_kc.shape[1]
    page = cache_kc.shape[2]
    rows_minor = lambda a: jnp.transpose(a, (0, 2, 3, 1)).reshape(a.shape[0], KVW, a.shape[1])
    caches = [rows_minor(c[lyr]) for c in (cache_kc, cache_vc, cache_ks, cache_vs)]
    o_rows, pooled_s, kw_next, vw_next = _nsa_sample(
        page_table, q_rows, gate_rows, [two(a) for a in (kc_s, vc_s, ks_s, vs_s, kw_s, vw_s)], two(vp_s),
        rows_minor(state_kw[lyr]), rows_minor(state_vw[lyr]), state_pool[lyr], wk, wv, caches)
    ynsa_s = o_rows.reshape(1, n_tok, QPAD)
    y_sorted, gath_idx = _moe_sorted_experts(u2p, eid_t, rank_t, counts, *moe_w[:3])
    ynsa_s, y_sorted = lax.optimization_barrier((ynsa_s, y_sorted))
    y_prompt = _moe_sorted_combine(y_sorted, gath_idx, gk_t, u2, x1, mod_p, nf, *moe_w[3:])
    x1_s, u2_s, gates_s = _finish(xs3, pooled_s.reshape(1, n_tok, POOL_W), ynsa_s, gm_s,
                                  (mod_s, mod_s, mod_s), fin_w, tm=tm_s, sparse=False)
    y_sample = _moe(u2_s, gates_s, x1_s, mod_s, nf, *moe_w, tm=n_tok).reshape(n_seq, ns, d)

    kvp = lambda a: a.reshape(1, bsz, seq, N_KV, HEAD_DIM)
    tailp = lambda a: jnp.pad(a, ((0, 0), (wbuf, 0), (0, 0)))[:, -wbuf:].reshape(1, bsz, wbuf, N_KV, HEAD_DIM)
    kvs = lambda a: a.reshape(1, n_seq, ns, N_KV, HEAD_DIM)
    wins = lambda a: jnp.transpose(a.reshape(n_seq, N_KV, HEAD_DIM, wbuf), (0, 3, 1, 2))[None]
    pool_p = vp[:, -POOL_BUF:][None]
    pool_s = jnp.concatenate([state_pool[lyr], vp_s.reshape(n_seq, ns, POOL_W)], axis=1)[None, :, -POOL_BUF:]
    return (y_prompt, y_sample, kvp(kc), kvp(vc), kvp(ks), kvp(vs), tailp(kw), tailp(vw), pool_p,
            kvs(kc_s), kvs(vc_s), kvs(ks_s), kvs(vs_s), wins(kw_next), wins(vw_next), pool_s)
```
